```python
import math
import jax, jax.numpy as jnp
from jax import lax
import numpy as np

D_MODEL = 1024
BATCH = 8
SEQ = 8192
DEPTH = 1

MEM_LEN = 256
EPS = 1e-6
LRU_WIDTH = 512
LRU_BLOCKS = 8
LRU_BLOCK = LRU_WIDTH // LRU_BLOCKS
CONV_WIDTH = 4
LRU_C = 8.0
MLA_HEADS = 8
QK_NOPE = 64
QK_ROPE = 32
QK_HEAD = QK_NOPE + QK_ROPE
V_DIM = 64
Q_LORA = 256
KV_LORA = 128
MLA_WIDTH = MLA_HEADS * V_DIM
ROPE_THETA = 10000.0
Q_BLOCK = 128
MIX_WIDTH = LRU_WIDTH + MLA_WIDTH
OFF_Y = LRU_WIDTH
OFF_CQ = 2 * LRU_WIDTH
OFF_CKV = OFF_CQ + Q_LORA
OFF_KR = OFF_CKV + KV_LORA
IN_COLS = OFF_KR + QK_ROPE
MEM_HEADS = 4
MEM_HEAD_DIM = 128
MEM_WIDTH = MEM_HEADS * MEM_HEAD_DIM
D_FF = 2816
FFN_CONV = 3

kernel_name = "hybrid_rglru_mla_memxattn_convffn_encoder"


def rms_norm(x, g):
    xf = x.astype(jnp.float32)
    y = xf * lax.rsqrt(jnp.mean(xf * xf, axis=-1, keepdims=True) + EPS)
    return (y * g.astype(jnp.float32)).astype(x.dtype)


def depthwise_conv(x, w, b, left, right):
    S = x.shape[1]
    xp = jnp.pad(x, ((0, 0), (left, right), (0, 0)))
    out = xp[:, 0:S] * w[0] + b
    for k in range(1, w.shape[0]):
        out = out + xp[:, k:k + S] * w[k]
    return out


def rope_tables(positions):
    inv = ROPE_THETA ** (-jnp.arange(0, QK_ROPE, 2, dtype=jnp.float32) / QK_ROPE)
    ang = positions.astype(jnp.float32)[..., None] * inv
    return jnp.cos(ang), jnp.sin(ang)


def apply_rope(t, cos, sin):
    half = QK_ROPE // 2
    c = cos[:, :, None, :].astype(t.dtype)
    s = sin[:, :, None, :].astype(t.dtype)
    t1, t2 = t[..., :half], t[..., half:]
    return jnp.concatenate([t1 * c - t2 * s, t1 * s + t2 * c], axis=-1)


def block_diag(x, w):
    B_, S_, _ = x.shape
    xb = x.reshape(B_, S_, LRU_BLOCKS, LRU_BLOCK)
    return jnp.einsum('bsnc,ncd->bsnd', xb, w).reshape(B_, S_, LRU_WIDTH)


def rg_lru(x, w_a, b_a, w_i, b_i, lam, reverse):
    r = jax.nn.sigmoid((block_diag(x, w_a) + b_a).astype(jnp.float32))
    i = jax.nn.sigmoid((block_diag(x, w_i) + b_i).astype(jnp.float32))
    log_a = -LRU_C * r * jax.nn.softplus(-lam.astype(jnp.float32))
    a = jnp.exp(log_a)
    mult = jnp.sqrt(-jnp.expm1(2.0 * log_a))
    b = mult * (i * x.astype(jnp.float32))

    def combine(lhs, rhs):
        a_l, b_l = lhs
        a_r, b_r = rhs
        return a_l * a_r, a_r * b_l + b_r

    _, h = lax.associative_scan(combine, (a, b), reverse=reverse, axis=1)
    return h


def mla_attention(proj, cos, sin, q_a_norm, w_uq, kv_a_norm, w_ukv, q_norm, k_norm):
    B_, S_, _ = proj.shape
    c_q = rms_norm(proj[..., OFF_CQ:OFF_CKV], q_a_norm)
    c_kv = rms_norm(proj[..., OFF_CKV:OFF_KR], kv_a_norm)
    k_rope = proj[..., OFF_KR:IN_COLS]
    q = (c_q @ w_uq).reshape(B_, S_, MLA_HEADS, QK_HEAD)
    kv = (c_kv @ w_ukv).reshape(B_, S_, MLA_HEADS, QK_NOPE + V_DIM)
    k_nope, v = kv[..., :QK_NOPE], kv[..., QK_NOPE:]
    k_rope_h = jnp.broadcast_to(k_rope[:, :, None, :], (B_, S_, MLA_HEADS, QK_ROPE))
    k = jnp.concatenate([k_nope, k_rope_h], axis=-1)
    q = rms_norm(q, q_norm)
    k = rms_norm(k, k_norm)
    q = jnp.concatenate([q[..., :QK_NOPE], apply_rope(q[..., QK_NOPE:], cos, sin)], axis=-1)
    k = jnp.concatenate([k[..., :QK_NOPE], apply_rope(k[..., QK_NOPE:], cos, sin)], axis=-1)
    scale = QK_HEAD ** -0.5
    kh = k.transpose(0, 2, 1, 3)
    vh = v.transpose(0, 2, 1, 3)
    nb = S_ // Q_BLOCK
    qb = q.transpose(0, 2, 1, 3).reshape(B_, MLA_HEADS, nb, Q_BLOCK, QK_HEAD).transpose(2, 0, 1, 3, 4)

    def attend(q_blk):
        s = jnp.einsum('bhqd,bhkd->bhqk', q_blk, kh).astype(jnp.float32) * scale
        p = jax.nn.softmax(s, axis=-1)
        return jnp.einsum('bhqk,bhkd->bhqd', p.astype(vh.dtype), vh)

    o = lax.map(attend, qb)
    return o.transpose(1, 0, 3, 2, 4).reshape(B_, S_, MLA_WIDTH)


def memory_cross_attention(h, mem_n, w_q, w_kv, q_norm, k_norm, w_o):
    B_, S_, _ = h.shape
    M = mem_n.shape[1]
    q = (h @ w_q).reshape(B_, S_, MEM_HEADS, MEM_HEAD_DIM)
    kv = mem_n @ w_kv
    k = kv[..., :MEM_WIDTH].reshape(B_, M, MEM_HEADS, MEM_HEAD_DIM)
    v = kv[..., MEM_WIDTH:].reshape(B_, M, MEM_HEADS, MEM_HEAD_DIM)
    q = rms_norm(q, q_norm)
    k = rms_norm(k, k_norm)
    s = jnp.einsum('bqhd,bkhd->bhqk', q, k).astype(jnp.float32) * (MEM_HEAD_DIM ** -0.5)
    p = jax.nn.softmax(s, axis=-1)
    o = jnp.einsum('bhqk,bkhd->bqhd', p.astype(v.dtype), v).reshape(B_, S_, MEM_WIDTH)
    return o @ w_o


def hybrid_layer(x, mem, cos, sin, attn_norm, w_in, lru_conv_w, lru_conv_b, lru_w_a, lru_b_a,
                 lru_w_i, lru_b_i, lru_lambda, q_a_norm, w_uq, kv_a_norm, w_ukv, mla_q_norm,
                 mla_k_norm, lru_out_norm, mla_out_norm, w_out, mem_attn_norm, mem_norm, w_mem_q,
                 w_mem_kv, mem_q_norm, mem_k_norm, w_mem_o, ffn_norm, w_up, ffn_conv_w, ffn_conv_b,
                 w_down):
    h = rms_norm(x, attn_norm)
    proj = h @ w_in
    xr = proj[..., :OFF_Y]
    yg = proj[..., OFF_Y:OFF_CQ]
    xf = depthwise_conv(xr, lru_conv_w[0], lru_conv_b[0], CONV_WIDTH - 1, 0)
    xb = depthwise_conv(xr, lru_conv_w[1], lru_conv_b[1], 0, CONV_WIDTH - 1)
    hf = rg_lru(xf, lru_w_a[0], lru_b_a[0], lru_w_i[0], lru_b_i[0], lru_lambda[0], False)
    hb = rg_lru(xb, lru_w_a[1], lru_b_a[1], lru_w_i[1], lru_b_i[1], lru_lambda[1], True)
    lru_out = (hf + hb).astype(x.dtype) * jax.nn.gelu(yg)
    mla_out = mla_attention(proj, cos, sin, q_a_norm, w_uq, kv_a_norm, w_ukv, mla_q_norm, mla_k_norm)
    mixed = jnp.concatenate([rms_norm(lru_out, lru_out_norm), rms_norm(mla_out, mla_out_norm)], axis=-1)
    x = x + mixed @ w_out
    x = x + memory_cross_attention(rms_norm(x, mem_attn_norm), rms_norm(mem, mem_norm),
                                   w_mem_q, w_mem_kv, mem_q_norm, mem_k_norm, w_mem_o)
    gu = rms_norm(x, ffn_norm) @ w_up
    gu = depthwise_conv(gu, ffn_conv_w, ffn_conv_b, FFN_CONV // 2, FFN_CONV // 2)
    g, u = gu[..., :D_FF], gu[..., D_FF:]
    x = x + (jax.nn.silu(g) * u) @ w_down
    return x


def _fwd_setup_inputs(seed: int = 0) -> dict:
    key = jax.random.key(seed)
    ks = iter(jax.random.split(key, 40))
    f32 = jnp.float32

    def w(shape, fan_in):
        return jax.random.normal(next(ks), (DEPTH,) + shape, f32) * (fan_in ** -0.5)

    def gain(shape):
        return 1.0 + 0.05 * jax.random.normal(next(ks), (DEPTH,) + shape, f32)

    def bias(shape):
        return 0.01 * jax.random.normal(next(ks), (DEPTH,) + shape, f32)

    x = jax.random.normal(next(ks), (BATCH, SEQ, D_MODEL), f32)
    mem = jax.random.normal(next(ks), (BATCH, MEM_LEN, D_MODEL), f32)
    positions = jnp.broadcast_to(jnp.arange(SEQ, dtype=jnp.int32)[None, :], (BATCH, SEQ))
    u = jax.random.uniform(next(ks), (DEPTH, 2, LRU_WIDTH), f32, 0.9, 0.999)
    s = u ** (1.0 / LRU_C)
    lru_lambda = jnp.log(s) - jnp.log1p(-s)
    return {
        "x": x,
        "mem": mem,
        "positions": positions,
        "attn_norm": gain((D_MODEL,)),
        "w_in": w((D_MODEL, IN_COLS), D_MODEL),
        "lru_conv_w": w((2, CONV_WIDTH, LRU_WIDTH), CONV_WIDTH),
        "lru_conv_b": bias((2, LRU_WIDTH)),
        "lru_w_a": w((2, LRU_BLOCKS, LRU_BLOCK, LRU_BLOCK), LRU_BLOCK),
        "lru_b_a": bias((2, LRU_WIDTH)),
        "lru_w_i": w((2, LRU_BLOCKS, LRU_BLOCK, LRU_BLOCK), LRU_BLOCK),
        "lru_b_i": bias((2, LRU_WIDTH)),
        "lru_lambda": lru_lambda,
        "q_a_norm": gain((Q_LORA,)),
        "w_uq": w((Q_LORA, MLA_HEADS * QK_HEAD), Q_LORA),
        "kv_a_norm": gain((KV_LORA,)),
        "w_ukv": w((KV_LORA, MLA_HEADS * (QK_NOPE + V_DIM)), KV_LORA),
        "mla_q_norm": gain((QK_HEAD,)),
        "mla_k_norm": gain((QK_HEAD,)),
        "lru_out_norm": gain((LRU_WIDTH,)),
        "mla_out_norm": gain((MLA_WIDTH,)),
        "w_out": w((MIX_WIDTH, D_MODEL), MIX_WIDTH),
        "mem_attn_norm": gain((D_MODEL,)),
        "mem_norm": gain((D_MODEL,)),
        "w_mem_q": w((D_MODEL, MEM_WIDTH), D_MODEL),
        "w_mem_kv": w((D_MODEL, 2 * MEM_WIDTH), D_MODEL),
        "mem_q_norm": gain((MEM_HEAD_DIM,)),
        "mem_k_norm": gain((MEM_HEAD_DIM,)),
        "w_mem_o": w((MEM_WIDTH, D_MODEL), MEM_WIDTH),
        "ffn_norm": gain((D_MODEL,)),
        "w_up": w((D_MODEL, 2 * D_FF), D_MODEL),
        "ffn_conv_w": w((FFN_CONV, 2 * D_FF), FFN_CONV),
        "ffn_conv_b": bias((2 * D_FF,)),
        "w_down": w((D_FF, D_MODEL), D_FF),
    }


def _fwd_reference(x, mem, positions, attn_norm, w_in, lru_conv_w, lru_conv_b, lru_w_a, lru_b_a,
              lru_w_i, lru_b_i, lru_lambda, q_a_norm, w_uq, kv_a_norm, w_ukv, mla_q_norm,
              mla_k_norm, lru_out_norm, mla_out_norm, w_out, mem_attn_norm, mem_norm, w_mem_q,
              w_mem_kv, mem_q_norm, mem_k_norm, w_mem_o, ffn_norm, w_up, ffn_conv_w, ffn_conv_b,
              w_down):
    cos, sin = rope_tables(positions)
    for l in range(DEPTH):
        x = hybrid_layer(x, mem, cos, sin, attn_norm[l], w_in[l], lru_conv_w[l], lru_conv_b[l],
                         lru_w_a[l], lru_b_a[l], lru_w_i[l], lru_b_i[l], lru_lambda[l],
                         q_a_norm[l], w_uq[l], kv_a_norm[l], w_ukv[l], mla_q_norm[l],
                         mla_k_norm[l], lru_out_norm[l], mla_out_norm[l], w_out[l],
                         mem_attn_norm[l], mem_norm[l], w_mem_q[l], w_mem_kv[l], mem_q_norm[l],
                         mem_k_norm[l], w_mem_o[l], ffn_norm[l], w_up[l], ffn_conv_w[l],
                         ffn_conv_b[l], w_down[l])
    return x


import jax as _jax
import jax.numpy as _jnp

TWIN_FORMAT = 'train_step'
FWD_PARAMS = ['x', 'mem', 'positions', 'attn_norm', 'w_in', 'lru_conv_w', 'lru_conv_b', 'lru_w_a', 'lru_b_a', 'lru_w_i', 'lru_b_i', 'lru_lambda', 'q_a_norm', 'w_uq', 'kv_a_norm', 'w_ukv', 'mla_q_norm', 'mla_k_norm', 'lru_out_norm', 'mla_out_norm', 'w_out', 'mem_attn_norm', 'mem_norm', 'w_mem_q', 'w_mem_kv', 'mem_q_norm', 'mem_k_norm', 'w_mem_o', 'ffn_norm', 'w_up', 'ffn_conv_w', 'ffn_conv_b', 'w_down']
TWIN_WEIGHTS = ['attn_norm', 'w_in', 'lru_conv_w', 'lru_conv_b', 'lru_w_a', 'lru_b_a', 'lru_w_i', 'lru_b_i', 'lru_lambda', 'q_a_norm', 'w_uq', 'kv_a_norm', 'w_ukv', 'mla_q_norm', 'mla_k_norm', 'lru_out_norm', 'mla_out_norm', 'w_out', 'mem_attn_norm', 'mem_norm', 'w_mem_q', 'w_mem_kv', 'mem_q_norm', 'mem_k_norm', 'w_mem_o', 'ffn_norm', 'w_up', 'ffn_conv_w', 'ffn_conv_b', 'w_down']
TWIN_DIFF_INPUT = 'x'
TWIN_INPUTS = ['x', 'mem', 'positions', 'attn_norm', 'w_in', 'lru_conv_w', 'lru_conv_b', 'lru_w_a', 'lru_b_a', 'lru_w_i', 'lru_b_i', 'lru_lambda', 'q_a_norm', 'w_uq', 'kv_a_norm', 'w_ukv', 'mla_q_norm', 'mla_k_norm', 'lru_out_norm', 'mla_out_norm', 'w_out', 'mem_attn_norm', 'mem_norm', 'w_mem_q', 'w_mem_kv', 'mem_q_norm', 'mem_k_norm', 'w_mem_o', 'ffn_norm', 'w_up', 'ffn_conv_w', 'ffn_conv_b', 'w_down', 'loss_target', 'm_attn_norm', 'm_w_in', 'm_lru_conv_w', 'm_lru_conv_b', 'm_lru_w_a', 'm_lru_b_a', 'm_lru_w_i', 'm_lru_b_i', 'm_lru_lambda', 'm_q_a_norm', 'm_w_uq', 'm_kv_a_norm', 'm_w_ukv', 'm_mla_q_norm', 'm_mla_k_norm', 'm_lru_out_norm', 'm_mla_out_norm', 'm_w_out', 'm_mem_attn_norm', 'm_mem_norm', 'm_w_mem_q', 'm_w_mem_kv', 'm_mem_q_norm', 'm_mem_k_norm', 'm_w_mem_o', 'm_ffn_norm', 'm_w_up', 'm_ffn_conv_w', 'm_ffn_conv_b', 'm_w_down', 'v_attn_norm', 'v_w_in', 'v_lru_conv_w', 'v_lru_conv_b', 'v_lru_w_a', 'v_lru_b_a', 'v_lru_w_i', 'v_lru_b_i', 'v_lru_lambda', 'v_q_a_norm', 'v_w_uq', 'v_kv_a_norm', 'v_w_ukv', 'v_mla_q_norm', 'v_mla_k_norm', 'v_lru_out_norm', 'v_mla_out_norm', 'v_w_out', 'v_mem_attn_norm', 'v_mem_norm', 'v_w_mem_q', 'v_w_mem_kv', 'v_mem_q_norm', 'v_mem_k_norm', 'v_w_mem_o', 'v_ffn_norm', 'v_w_up', 'v_ffn_conv_w', 'v_ffn_conv_b', 'v_w_down']
TWIN_OUTPUTS = ['loss', 'grad_x', 'grad_attn_norm', 'grad_w_in', 'grad_lru_conv_w', 'grad_lru_conv_b', 'grad_lru_w_a', 'grad_lru_b_a', 'grad_lru_w_i', 'grad_lru_b_i', 'grad_lru_lambda', 'grad_q_a_norm', 'grad_w_uq', 'grad_kv_a_norm', 'grad_w_ukv', 'grad_mla_q_norm', 'grad_mla_k_norm', 'grad_lru_out_norm', 'grad_mla_out_norm', 'grad_w_out', 'grad_mem_attn_norm', 'grad_mem_norm', 'grad_w_mem_q', 'grad_w_mem_kv', 'grad_mem_q_norm', 'grad_mem_k_norm', 'grad_w_mem_o', 'grad_ffn_norm', 'grad_w_up', 'grad_ffn_conv_w', 'grad_ffn_conv_b', 'grad_w_down', 'delta_attn_norm', 'delta_w_in', 'delta_lru_conv_w', 'delta_lru_conv_b', 'delta_lru_w_a', 'delta_lru_b_a', 'delta_lru_w_i', 'delta_lru_b_i', 'delta_lru_lambda', 'delta_q_a_norm', 'delta_w_uq', 'delta_kv_a_norm', 'delta_w_ukv', 'delta_mla_q_norm', 'delta_mla_k_norm', 'delta_lru_out_norm', 'delta_mla_out_norm', 'delta_w_out', 'delta_mem_attn_norm', 'delta_mem_norm', 'delta_w_mem_q', 'delta_w_mem_kv', 'delta_mem_q_norm', 'delta_mem_k_norm', 'delta_w_mem_o', 'delta_ffn_norm', 'delta_w_up', 'delta_ffn_conv_w', 'delta_ffn_conv_b', 'delta_w_down', 'new_m_attn_norm', 'new_m_w_in', 'new_m_lru_conv_w', 'new_m_lru_conv_b', 'new_m_lru_w_a', 'new_m_lru_b_a', 'new_m_lru_w_i', 'new_m_lru_b_i', 'new_m_lru_lambda', 'new_m_q_a_norm', 'new_m_w_uq', 'new_m_kv_a_norm', 'new_m_w_ukv', 'new_m_mla_q_norm', 'new_m_mla_k_norm', 'new_m_lru_out_norm', 'new_m_mla_out_norm', 'new_m_w_out', 'new_m_mem_attn_norm', 'new_m_mem_norm', 'new_m_w_mem_q', 'new_m_w_mem_kv', 'new_m_mem_q_norm', 'new_m_mem_k_norm', 'new_m_w_mem_o', 'new_m_ffn_norm', 'new_m_w_up', 'new_m_ffn_conv_w', 'new_m_ffn_conv_b', 'new_m_w_down', 'new_v_attn_norm', 'new_v_w_in', 'new_v_lru_conv_w', 'new_v_lru_conv_b', 'new_v_lru_w_a', 'new_v_lru_b_a', 'new_v_lru_w_i', 'new_v_lru_b_i', 'new_v_lru_lambda', 'new_v_q_a_norm', 'new_v_w_uq', 'new_v_kv_a_norm', 'new_v_w_ukv', 'new_v_mla_q_norm', 'new_v_mla_k_norm', 'new_v_lru_out_norm', 'new_v_mla_out_norm', 'new_v_w_out', 'new_v_mem_attn_norm', 'new_v_mem_norm', 'new_v_w_mem_q', 'new_v_w_mem_kv', 'new_v_mem_q_norm', 'new_v_mem_k_norm', 'new_v_w_mem_o', 'new_v_ffn_norm', 'new_v_w_up', 'new_v_ffn_conv_w', 'new_v_ffn_conv_b', 'new_v_w_down']
TWIN_LEAF_KINDS = {'loss': 'loss', 'grad_x': 'grad_x', 'grad_attn_norm': 'grad_w', 'grad_w_in': 'grad_w', 'grad_lru_conv_w': 'grad_w', 'grad_lru_conv_b': 'grad_w', 'grad_lru_w_a': 'grad_w', 'grad_lru_b_a': 'grad_w', 'grad_lru_w_i': 'grad_w', 'grad_lru_b_i': 'grad_w', 'grad_lru_lambda': 'grad_w', 'grad_q_a_norm': 'grad_w', 'grad_w_uq': 'grad_w', 'grad_kv_a_norm': 'grad_w', 'grad_w_ukv': 'grad_w', 'grad_mla_q_norm': 'grad_w', 'grad_mla_k_norm': 'grad_w', 'grad_lru_out_norm': 'grad_w', 'grad_mla_out_norm': 'grad_w', 'grad_w_out': 'grad_w', 'grad_mem_attn_norm': 'grad_w', 'grad_mem_norm': 'grad_w', 'grad_w_mem_q': 'grad_w', 'grad_w_mem_kv': 'grad_w', 'grad_mem_q_norm': 'grad_w', 'grad_mem_k_norm': 'grad_w', 'grad_w_mem_o': 'grad_w', 'grad_ffn_norm': 'grad_w', 'grad_w_up': 'grad_w', 'grad_ffn_conv_w': 'grad_w', 'grad_ffn_conv_b': 'grad_w', 'grad_w_down': 'grad_w', 'delta_attn_norm': 'delta_w', 'delta_w_in': 'delta_w', 'delta_lru_conv_w': 'delta_w', 'delta_lru_conv_b': 'delta_w', 'delta_lru_w_a': 'delta_w', 'delta_lru_b_a': 'delta_w', 'delta_lru_w_i': 'delta_w', 'delta_lru_b_i': 'delta_w', 'delta_lru_lambda': 'delta_w', 'delta_q_a_norm': 'delta_w', 'delta_w_uq': 'delta_w', 'delta_kv_a_norm': 'delta_w', 'delta_w_ukv': 'delta_w', 'delta_mla_q_norm': 'delta_w', 'delta_mla_k_norm': 'delta_w', 'delta_lru_out_norm': 'delta_w', 'delta_mla_out_norm': 'delta_w', 'delta_w_out': 'delta_w', 'delta_mem_attn_norm': 'delta_w', 'delta_mem_norm': 'delta_w', 'delta_w_mem_q': 'delta_w', 'delta_w_mem_kv': 'delta_w', 'delta_mem_q_norm': 'delta_w', 'delta_mem_k_norm': 'delta_w', 'delta_w_mem_o': 'delta_w', 'delta_ffn_norm': 'delta_w', 'delta_w_up': 'delta_w', 'delta_ffn_conv_w': 'delta_w', 'delta_ffn_conv_b': 'delta_w', 'delta_w_down': 'delta_w', 'new_m_attn_norm': 'new_m', 'new_m_w_in': 'new_m', 'new_m_lru_conv_w': 'new_m', 'new_m_lru_conv_b': 'new_m', 'new_m_lru_w_a': 'new_m', 'new_m_lru_b_a': 'new_m', 'new_m_lru_w_i': 'new_m', 'new_m_lru_b_i': 'new_m', 'new_m_lru_lambda': 'new_m', 'new_m_q_a_norm': 'new_m', 'new_m_w_uq': 'new_m', 'new_m_kv_a_norm': 'new_m', 'new_m_w_ukv': 'new_m', 'new_m_mla_q_norm': 'new_m', 'new_m_mla_k_norm': 'new_m', 'new_m_lru_out_norm': 'new_m', 'new_m_mla_out_norm': 'new_m', 'new_m_w_out': 'new_m', 'new_m_mem_attn_norm': 'new_m', 'new_m_mem_norm': 'new_m', 'new_m_w_mem_q': 'new_m', 'new_m_w_mem_kv': 'new_m', 'new_m_mem_q_norm': 'new_m', 'new_m_mem_k_norm': 'new_m', 'new_m_w_mem_o': 'new_m', 'new_m_ffn_norm': 'new_m', 'new_m_w_up': 'new_m', 'new_m_ffn_conv_w': 'new_m', 'new_m_ffn_conv_b': 'new_m', 'new_m_w_down': 'new_m', 'new_v_attn_norm': 'new_v', 'new_v_w_in': 'new_v', 'new_v_lru_conv_w': 'new_v', 'new_v_lru_conv_b': 'new_v', 'new_v_lru_w_a': 'new_v', 'new_v_lru_b_a': 'new_v', 'new_v_lru_w_i': 'new_v', 'new_v_lru_b_i': 'new_v', 'new_v_lru_lambda': 'new_v', 'new_v_q_a_norm': 'new_v', 'new_v_w_uq': 'new_v', 'new_v_kv_a_norm': 'new_v', 'new_v_w_ukv': 'new_v', 'new_v_mla_q_norm': 'new_v', 'new_v_mla_k_norm': 'new_v', 'new_v_lru_out_norm': 'new_v', 'new_v_mla_out_norm': 'new_v', 'new_v_w_out': 'new_v', 'new_v_mem_attn_norm': 'new_v', 'new_v_mem_norm': 'new_v', 'new_v_w_mem_q': 'new_v', 'new_v_w_mem_kv': 'new_v', 'new_v_mem_q_norm': 'new_v', 'new_v_mem_k_norm': 'new_v', 'new_v_w_mem_o': 'new_v', 'new_v_ffn_norm': 'new_v', 'new_v_w_up': 'new_v', 'new_v_ffn_conv_w': 'new_v', 'new_v_ffn_conv_b': 'new_v', 'new_v_w_down': 'new_v'}


def _forward(args):
    return _fwd_reference(*[args[k] for k in FWD_PARAMS])


def _output_shape():
    def fwd():
        inp = _fwd_setup_inputs(0)
        return _fwd_reference(*[inp[k] for k in FWD_PARAMS])
    out = _jax.eval_shape(fwd)
    return out.shape, out.dtype

N_MICROBATCH = 1
ADAM_LR = 0.001
ADAM_B1 = 0.9
ADAM_B2 = 0.999
ADAM_EPS = 1e-08
ADAM_WD = 0.01
ADAM_STEP = 10
PER_EXAMPLE_BATCH_AXIS = {'x': 0, 'mem': 0, 'positions': 0, 'loss_target': 0}
SHARED_INPUTS = []
_WEIGHT_DTYPES = {'attn_norm': _jnp.float32, 'w_in': _jnp.float32, 'lru_conv_w': _jnp.float32, 'lru_conv_b': _jnp.float32, 'lru_w_a': _jnp.float32, 'lru_b_a': _jnp.float32, 'lru_w_i': _jnp.float32, 'lru_b_i': _jnp.float32, 'lru_lambda': _jnp.float32, 'q_a_norm': _jnp.float32, 'w_uq': _jnp.float32, 'kv_a_norm': _jnp.float32, 'w_ukv': _jnp.float32, 'mla_q_norm': _jnp.float32, 'mla_k_norm': _jnp.float32, 'lru_out_norm': _jnp.float32, 'mla_out_norm': _jnp.float32, 'w_out': _jnp.float32, 'mem_attn_norm': _jnp.float32, 'mem_norm': _jnp.float32, 'w_mem_q': _jnp.float32, 'w_mem_kv': _jnp.float32, 'mem_q_norm': _jnp.float32, 'mem_k_norm': _jnp.float32, 'w_mem_o': _jnp.float32, 'ffn_norm': _jnp.float32, 'w_up': _jnp.float32, 'ffn_conv_w': _jnp.float32, 'ffn_conv_b': _jnp.float32, 'w_down': _jnp.float32}
MOMENT_SCALE = {'attn_norm': 2.797749e+00, 'w_in': 1.607086e+00, 'lru_conv_w': 3.601228e+00, 'lru_conv_b': 5.053285e+01, 'lru_w_a': 1.872121e+00, 'lru_b_a': 1.089889e+00, 'lru_w_i': 3.440753e+00, 'lru_b_i': 1.238632e+00, 'lru_lambda': 1.616914e+00, 'q_a_norm': 5.258413e+00, 'w_uq': 2.368633e+00, 'kv_a_norm': 1.747573e+01, 'w_ukv': 3.244574e+00, 'mla_q_norm': 9.265353e+00, 'mla_k_norm': 9.182974e+00, 'lru_out_norm': 8.683079e+01, 'mla_out_norm': 6.627172e+01, 'w_out': 4.104045e+00, 'mem_attn_norm': 1.413578e-01, 'mem_norm': 8.131222e-01, 'w_mem_q': 2.092267e-01, 'w_mem_kv': 4.628371e-01, 'mem_q_norm': 5.168733e+00, 'mem_k_norm': 5.163183e+00, 'w_mem_o': 4.246890e-01, 'ffn_norm': 5.481604e+01, 'w_up': 1.102622e+00, 'ffn_conv_w': 7.499763e+00, 'ffn_conv_b': 6.787067e+00, 'w_down': 8.186396e-01}


def _to_microbatches(a, axis):
    t = _jnp.moveaxis(a, axis, 0)
    t = t.reshape((N_MICROBATCH, t.shape[0] // N_MICROBATCH) + t.shape[1:])
    return _jnp.moveaxis(t, 1, axis + 1)


def setup_inputs(seed: int = 0) -> dict:
    inp = _fwd_setup_inputs(seed)
    key = _jax.random.fold_in(_jax.random.key(seed), 7919)
    shape, _ = _output_shape()
    out = dict(inp)
    out["loss_target"] = _jax.random.normal(_jax.random.fold_in(key, 0), shape, _jnp.float32)
    for i, name in enumerate(TWIN_WEIGHTS):
        w = inp[name].astype(_jnp.float32)
        if MOMENT_SCALE is None:
            s = _jnp.sqrt(_jnp.mean(_jnp.square(w)) + 1e-30)
        else:
            s = MOMENT_SCALE[name]
        km, kv = _jax.random.split(_jax.random.fold_in(key, i + 1))
        out[name] = w
        out["m_" + name] = s * _jax.random.normal(km, w.shape, _jnp.float32)
        out["v_" + name] = (s * s) * _jax.random.uniform(kv, w.shape, _jnp.float32, 0.5, 1.5)
    if N_MICROBATCH > 1:
        for name, axis in PER_EXAMPLE_BATCH_AXIS.items():
            out[name] = _to_microbatches(out[name], axis)
    return {'x': out['x'], 'mem': out['mem'], 'positions': out['positions'], 'attn_norm': out['attn_norm'], 'w_in': out['w_in'], 'lru_conv_w': out['lru_conv_w'], 'lru_conv_b': out['lru_conv_b'], 'lru_w_a': out['lru_w_a'], 'lru_b_a': out['lru_b_a'], 'lru_w_i': out['lru_w_i'], 'lru_b_i': out['lru_b_i'], 'lru_lambda': out['lru_lambda'], 'q_a_norm': out['q_a_norm'], 'w_uq': out['w_uq'], 'kv_a_norm': out['kv_a_norm'], 'w_ukv': out['w_ukv'], 'mla_q_norm': out['mla_q_norm'], 'mla_k_norm': out['mla_k_norm'], 'lru_out_norm': out['lru_out_norm'], 'mla_out_norm': out['mla_out_norm'], 'w_out': out['w_out'], 'mem_attn_norm': out['mem_attn_norm'], 'mem_norm': out['mem_norm'], 'w_mem_q': out['w_mem_q'], 'w_mem_kv': out['w_mem_kv'], 'mem_q_norm': out['mem_q_norm'], 'mem_k_norm': out['mem_k_norm'], 'w_mem_o': out['w_mem_o'], 'ffn_norm': out['ffn_norm'], 'w_up': out['w_up'], 'ffn_conv_w': out['ffn_conv_w'], 'ffn_conv_b': out['ffn_conv_b'], 'w_down': out['w_down'], 'loss_target': out['loss_target'], 'm_attn_norm': out['m_attn_norm'], 'm_w_in': out['m_w_in'], 'm_lru_conv_w': out['m_lru_conv_w'], 'm_lru_conv_b': out['m_lru_conv_b'], 'm_lru_w_a': out['m_lru_w_a'], 'm_lru_b_a': out['m_lru_b_a'], 'm_lru_w_i': out['m_lru_w_i'], 'm_lru_b_i': out['m_lru_b_i'], 'm_lru_lambda': out['m_lru_lambda'], 'm_q_a_norm': out['m_q_a_norm'], 'm_w_uq': out['m_w_uq'], 'm_kv_a_norm': out['m_kv_a_norm'], 'm_w_ukv': out['m_w_ukv'], 'm_mla_q_norm': out['m_mla_q_norm'], 'm_mla_k_norm': out['m_mla_k_norm'], 'm_lru_out_norm': out['m_lru_out_norm'], 'm_mla_out_norm': out['m_mla_out_norm'], 'm_w_out': out['m_w_out'], 'm_mem_attn_norm': out['m_mem_attn_norm'], 'm_mem_norm': out['m_mem_norm'], 'm_w_mem_q': out['m_w_mem_q'], 'm_w_mem_kv': out['m_w_mem_kv'], 'm_mem_q_norm': out['m_mem_q_norm'], 'm_mem_k_norm': out['m_mem_k_norm'], 'm_w_mem_o': out['m_w_mem_o'], 'm_ffn_norm': out['m_ffn_norm'], 'm_w_up': out['m_w_up'], 'm_ffn_conv_w': out['m_ffn_conv_w'], 'm_ffn_conv_b': out['m_ffn_conv_b'], 'm_w_down': out['m_w_down'], 'v_attn_norm': out['v_attn_norm'], 'v_w_in': out['v_w_in'], 'v_lru_conv_w': out['v_lru_conv_w'], 'v_lru_conv_b': out['v_lru_conv_b'], 'v_lru_w_a': out['v_lru_w_a'], 'v_lru_b_a': out['v_lru_b_a'], 'v_lru_w_i': out['v_lru_w_i'], 'v_lru_b_i': out['v_lru_b_i'], 'v_lru_lambda': out['v_lru_lambda'], 'v_q_a_norm': out['v_q_a_norm'], 'v_w_uq': out['v_w_uq'], 'v_kv_a_norm': out['v_kv_a_norm'], 'v_w_ukv': out['v_w_ukv'], 'v_mla_q_norm': out['v_mla_q_norm'], 'v_mla_k_norm': out['v_mla_k_norm'], 'v_lru_out_norm': out['v_lru_out_norm'], 'v_mla_out_norm': out['v_mla_out_norm'], 'v_w_out': out['v_w_out'], 'v_mem_attn_norm': out['v_mem_attn_norm'], 'v_mem_norm': out['v_mem_norm'], 'v_w_mem_q': out['v_w_mem_q'], 'v_w_mem_kv': out['v_w_mem_kv'], 'v_mem_q_norm': out['v_mem_q_norm'], 'v_mem_k_norm': out['v_mem_k_norm'], 'v_w_mem_o': out['v_w_mem_o'], 'v_ffn_norm': out['v_ffn_norm'], 'v_w_up': out['v_w_up'], 'v_ffn_conv_w': out['v_ffn_conv_w'], 'v_ffn_conv_b': out['v_ffn_conv_b'], 'v_w_down': out['v_w_down']}


def _loss(weights, diff, rest, loss_target):
    with _jax.named_scope("forward"):
        args = {**rest, TWIN_DIFF_INPUT: diff, **{k: w.astype(_WEIGHT_DTYPES[k]) for k, w in weights.items()}}
        y = _forward(args)
    with _jax.named_scope("loss_head"):
        err = _jnp.square(y.astype(_jnp.float32) - loss_target)
        return 0.5 * _jnp.sum(_jnp.mean(err, axis=-1)) if err.ndim else 0.5 * err


def _adamw(w, g, m, v):
    m = ADAM_B1 * m + (1.0 - ADAM_B1) * g
    v = ADAM_B2 * v + (1.0 - ADAM_B2) * _jnp.square(g)
    m_hat = m / (1.0 - ADAM_B1 ** ADAM_STEP)
    v_hat = v / (1.0 - ADAM_B2 ** ADAM_STEP)
    delta = -ADAM_LR * (m_hat / (_jnp.sqrt(v_hat) + ADAM_EPS) + ADAM_WD * w)
    return delta, m, v


def reference(x, mem, positions, attn_norm, w_in, lru_conv_w, lru_conv_b, lru_w_a, lru_b_a, lru_w_i, lru_b_i, lru_lambda, q_a_norm, w_uq, kv_a_norm, w_ukv, mla_q_norm, mla_k_norm, lru_out_norm, mla_out_norm, w_out, mem_attn_norm, mem_norm, w_mem_q, w_mem_kv, mem_q_norm, mem_k_norm, w_mem_o, ffn_norm, w_up, ffn_conv_w, ffn_conv_b, w_down, loss_target, m_attn_norm, m_w_in, m_lru_conv_w, m_lru_conv_b, m_lru_w_a, m_lru_b_a, m_lru_w_i, m_lru_b_i, m_lru_lambda, m_q_a_norm, m_w_uq, m_kv_a_norm, m_w_ukv, m_mla_q_norm, m_mla_k_norm, m_lru_out_norm, m_mla_out_norm, m_w_out, m_mem_attn_norm, m_mem_norm, m_w_mem_q, m_w_mem_kv, m_mem_q_norm, m_mem_k_norm, m_w_mem_o, m_ffn_norm, m_w_up, m_ffn_conv_w, m_ffn_conv_b, m_w_down, v_attn_norm, v_w_in, v_lru_conv_w, v_lru_conv_b, v_lru_w_a, v_lru_b_a, v_lru_w_i, v_lru_b_i, v_lru_lambda, v_q_a_norm, v_w_uq, v_kv_a_norm, v_w_ukv, v_mla_q_norm, v_mla_k_norm, v_lru_out_norm, v_mla_out_norm, v_w_out, v_mem_attn_norm, v_mem_norm, v_w_mem_q, v_w_mem_kv, v_mem_q_norm, v_mem_k_norm, v_w_mem_o, v_ffn_norm, v_w_up, v_ffn_conv_w, v_ffn_conv_b, v_w_down):
    given = dict(x=x, mem=mem, positions=positions, attn_norm=attn_norm, w_in=w_in, lru_conv_w=lru_conv_w, lru_conv_b=lru_conv_b, lru_w_a=lru_w_a, lru_b_a=lru_b_a, lru_w_i=lru_w_i, lru_b_i=lru_b_i, lru_lambda=lru_lambda, q_a_norm=q_a_norm, w_uq=w_uq, kv_a_norm=kv_a_norm, w_ukv=w_ukv, mla_q_norm=mla_q_norm, mla_k_norm=mla_k_norm, lru_out_norm=lru_out_norm, mla_out_norm=mla_out_norm, w_out=w_out, mem_attn_norm=mem_attn_norm, mem_norm=mem_norm, w_mem_q=w_mem_q, w_mem_kv=w_mem_kv, mem_q_norm=mem_q_norm, mem_k_norm=mem_k_norm, w_mem_o=w_mem_o, ffn_norm=ffn_norm, w_up=w_up, ffn_conv_w=ffn_conv_w, ffn_conv_b=ffn_conv_b, w_down=w_down, loss_target=loss_target, m_attn_norm=m_attn_norm, m_w_in=m_w_in, m_lru_conv_w=m_lru_conv_w, m_lru_conv_b=m_lru_conv_b, m_lru_w_a=m_lru_w_a, m_lru_b_a=m_lru_b_a, m_lru_w_i=m_lru_w_i, m_lru_b_i=m_lru_b_i, m_lru_lambda=m_lru_lambda, m_q_a_norm=m_q_a_norm, m_w_uq=m_w_uq, m_kv_a_norm=m_kv_a_norm, m_w_ukv=m_w_ukv, m_mla_q_norm=m_mla_q_norm, m_mla_k_norm=m_mla_k_norm, m_lru_out_norm=m_lru_out_norm, m_mla_out_norm=m_mla_out_norm, m_w_out=m_w_out, m_mem_attn_norm=m_mem_attn_norm, m_mem_norm=m_mem_norm, m_w_mem_q=m_w_mem_q, m_w_mem_kv=m_w_mem_kv, m_mem_q_norm=m_mem_q_norm, m_mem_k_norm=m_mem_k_norm, m_w_mem_o=m_w_mem_o, m_ffn_norm=m_ffn_norm, m_w_up=m_w_up, m_ffn_conv_w=m_ffn_conv_w, m_ffn_conv_b=m_ffn_conv_b, m_w_down=m_w_down, v_attn_norm=v_attn_norm, v_w_in=v_w_in, v_lru_conv_w=v_lru_conv_w, v_lru_conv_b=v_lru_conv_b, v_lru_w_a=v_lru_w_a, v_lru_b_a=v_lru_b_a, v_lru_w_i=v_lru_w_i, v_lru_b_i=v_lru_b_i, v_lru_lambda=v_lru_lambda, v_q_a_norm=v_q_a_norm, v_w_uq=v_w_uq, v_kv_a_norm=v_kv_a_norm, v_w_ukv=v_w_ukv, v_mla_q_norm=v_mla_q_norm, v_mla_k_norm=v_mla_k_norm, v_lru_out_norm=v_lru_out_norm, v_mla_out_norm=v_mla_out_norm, v_w_out=v_w_out, v_mem_attn_norm=v_mem_attn_norm, v_mem_norm=v_mem_norm, v_w_mem_q=v_w_mem_q, v_w_mem_kv=v_w_mem_kv, v_mem_q_norm=v_mem_q_norm, v_mem_k_norm=v_mem_k_norm, v_w_mem_o=v_w_mem_o, v_ffn_norm=v_ffn_norm, v_w_up=v_w_up, v_ffn_conv_w=v_ffn_conv_w, v_ffn_conv_b=v_ffn_conv_b, v_w_down=v_w_down)
    weights = {n: given[n] for n in TWIN_WEIGHTS}
    shared = {n: given[n] for n in SHARED_INPUTS}
    per_example = {n: given[n] for n in ['x', 'mem', 'positions']}
    grad_fn = _jax.value_and_grad(_loss, argnums=(0, 1))

    def one_microbatch(ex, loss_target):
        ex = dict(ex)
        diff = ex.pop(TWIN_DIFF_INPUT)
        return grad_fn(weights, diff, {**shared, **ex}, loss_target)

    if N_MICROBATCH == 1:
        loss, (grad_w, grad_x) = one_microbatch(per_example, given["loss_target"])
    else:
        def body(carry, xs):
            loss_sum, grad_sum = carry
            l_k, (gw_k, gx_k) = one_microbatch(xs[0], xs[1])
            with _jax.named_scope("update"):
                return (loss_sum + l_k, _jax.tree.map(_jnp.add, grad_sum, gw_k)), gx_k

        init = (_jnp.zeros((), _jnp.float32), _jax.tree.map(_jnp.zeros_like, weights))
        (loss, grad_w), grad_x = _jax.lax.scan(body, init, (per_example, given["loss_target"]))
    with _jax.named_scope("update"):
        delta_w, new_m, new_v = {}, {}, {}
        for n in TWIN_WEIGHTS:
            delta_w[n], new_m[n], new_v[n] = _adamw(weights[n], grad_w[n], given["m_" + n], given["v_" + n])
    return (loss, grad_x, *[grad_w[n] for n in TWIN_WEIGHTS], *[delta_w[n] for n in TWIN_WEIGHTS],
            *[new_m[n] for n in TWIN_WEIGHTS], *[new_v[n] for n in TWIN_WEIGHTS])
```

```python
import functools

import jax
import jax.numpy as jnp
from jax import lax
from jax.experimental import pallas as pl
from jax.experimental.pallas import tpu as pltpu

F32, BF16 = jnp.float32, jnp.bfloat16
MESH = pl.DeviceIdType.MESH

D_MODEL = 1024
EPS = 1e-6
LRU_W = 512
LRU_BLOCKS = 8
LRU_C = 8.0
CONV_W = 4
HEADS = 8
QK_NOPE, QK_ROPE, QK_HEAD, V_DIM = 64, 32, 96, 64
Q_LORA, KV_LORA = 256, 128
MLA_W = HEADS * V_DIM
ROPE_THETA = 10000.0
IN_COLS = 2 * LRU_W + Q_LORA + KV_LORA + QK_ROPE
OFF_KR = IN_COLS - QK_ROPE
IN_PAD = 1536
MEM_HEADS, MEM_HD = 4, 128
MEM_W = MEM_HEADS * MEM_HD
D_FF = 2816
N_CHIPS = 4
ADAM_LR, ADAM_B1, ADAM_B2, ADAM_EPS, ADAM_WD, ADAM_STEP = 0.001, 0.9, 0.999, 1e-08, 0.01, 10

LANES = 128
SUBLANES = 8
VMEM_LIMIT = 56 * 1024 * 1024
PACK_ROWS = 2048

SHARDED = (
    ("w_in", (D_MODEL, IN_COLS), 1, True),
    ("lru_conv_w", (2, CONV_W, LRU_W), 2, False),
    ("lru_conv_b", (2, LRU_W), 1, False),
    ("lru_b_a", (2, LRU_W), 1, False),
    ("lru_b_i", (2, LRU_W), 1, False),
    ("lru_lambda", (2, LRU_W), 1, False),
    ("w_uq", (Q_LORA, HEADS * QK_HEAD), 1, True),
    ("w_ukv", (KV_LORA, HEADS * (QK_NOPE + V_DIM)), 1, True),
    ("w_out", (2 * LRU_W, D_MODEL), 0, True),
    ("w_mem_q", (D_MODEL, MEM_W), 0, True),
    ("w_mem_kv", (D_MODEL, 2 * MEM_W), 0, True),
    ("w_mem_o", (MEM_W, D_MODEL), 1, True),
    ("w_up", (D_MODEL, 2 * D_FF), 1, True),
    ("ffn_conv_w", (3, 2 * D_FF), 1, False),
    ("w_down", (D_FF, D_MODEL), 0, True),
)
REPLICATED = (
    ("attn_norm", (D_MODEL,)), ("lru_w_a", (2, LRU_BLOCKS, 64, 64)), ("lru_w_i", (2, LRU_BLOCKS, 64, 64)),
    ("q_a_norm", (Q_LORA,)), ("kv_a_norm", (KV_LORA,)), ("mla_q_norm", (QK_HEAD,)), ("mla_k_norm", (QK_HEAD,)),
    ("lru_out_norm", (LRU_W,)), ("mla_out_norm", (MLA_W,)), ("mem_attn_norm", (D_MODEL,)), ("mem_norm", (D_MODEL,)),
    ("mem_q_norm", (MEM_HD,)), ("mem_k_norm", (MEM_HD,)), ("ffn_norm", (D_MODEL,)), ("ffn_conv_b", (2 * D_FF,)),
)
WEIGHT_ORDER = ('attn_norm', 'w_in', 'lru_conv_w', 'lru_conv_b', 'lru_w_a', 'lru_b_a', 'lru_w_i', 'lru_b_i', 'lru_lambda',
                'q_a_norm', 'w_uq', 'kv_a_norm', 'w_ukv', 'mla_q_norm', 'mla_k_norm', 'lru_out_norm', 'mla_out_norm', 'w_out',
                'mem_attn_norm', 'mem_norm', 'w_mem_q', 'w_mem_kv', 'mem_q_norm', 'mem_k_norm', 'w_mem_o', 'ffn_norm', 'w_up',
                'ffn_conv_w', 'ffn_conv_b', 'w_down')


def _numel(shape):
    n = 1
    for s in shape:
        n *= s
    return n


def _cparams(n_axes):
    return pltpu.CompilerParams(dimension_semantics=("arbitrary",) * n_axes, vmem_limit_bytes=VMEM_LIMIT)


def _bdot(a, b):
    return jnp.dot(a.astype(BF16), b.astype(BF16), preferred_element_type=F32)


def _bdot_nt(a, b):
    return lax.dot_general(a.astype(BF16), b.astype(BF16), (((1,), (1,)), ((), ())), preferred_element_type=F32)


def _bdot_tn(a, b):
    return lax.dot_general(a.astype(BF16), b.astype(BF16), (((0,), (0,)), ((), ())), preferred_element_type=F32)


def _rstd(x, n=None):
    n = x.shape[-1] if n is None else n
    return lax.rsqrt(jnp.sum(x * x, axis=-1, keepdims=True) * (1.0 / n) + EPS)


def _norm_bwd(x, rs, g, dy, n=None):
    n = x.shape[-1] if n is None else n
    xhat = x * rs
    dxh = dy * g
    dx = rs * (dxh - xhat * (jnp.sum(dxh * xhat, axis=-1, keepdims=True) * (1.0 / n)))
    return dx, dy * xhat


def _acc_row(ref, r, val):
    ref[r:r + 1, :] += jnp.sum(val, axis=0, keepdims=True)


def _zero_first(i, *refs):
    @pl.when(i == 0)
    def _():
        for r in refs:
            r[...] = jnp.zeros_like(r)


def _shift_down(x, j, halo):
    if j == 0:
        return x
    xs = pltpu.roll(x, j, 0)
    hs = pltpu.roll(halo, j, 0)
    row = lax.broadcasted_iota(jnp.int32, hs.shape, 0)
    top = jnp.where(row < j, hs, xs[:SUBLANES])
    return jnp.concatenate([top, xs[SUBLANES:]], axis=0)


def _shift_up(x, j, halo):
    if j == 0:
        return x
    t = x.shape[0]
    xs = pltpu.roll(x, t - j, 0)
    hs = pltpu.roll(halo, SUBLANES - j, 0)
    row = lax.broadcasted_iota(jnp.int32, hs.shape, 0)
    bot = jnp.where(row >= SUBLANES - j, hs, xs[t - SUBLANES:])
    return jnp.concatenate([xs[:t - SUBLANES], bot], axis=0)


def _shift(x, j, halo, down):
    return _shift_down(x, j, halo) if down else _shift_up(x, j, halo)


def _scan(a, b, down):
    t = a.shape[0]
    row = lax.broadcasted_iota(jnp.int32, a.shape, 0)
    d = 1
    while d < t:
        if down:
            keep = row >= d
            a_s = jnp.where(keep, pltpu.roll(a, d, 0), 1.0)
            b_s = jnp.where(keep, pltpu.roll(b, d, 0), 0.0)
        else:
            keep = row < t - d
            a_s = jnp.where(keep, pltpu.roll(a, t - d, 0), 1.0)
            b_s = jnp.where(keep, pltpu.roll(b, t - d, 0), 0.0)
        b = a * b_s + b
        a = a * a_s
        d *= 2
    return a, b


def _sigmoid(x):
    return 1.0 / (1.0 + jnp.exp(-x))


GELU_K = 0.7978845608028654
GELU_C = 0.044715


def _gelu(x):
    return 0.5 * x * (1.0 + jnp.tanh(GELU_K * (x + GELU_C * x * x * x)))


def _gelu_grad(x):
    t = jnp.tanh(GELU_K * (x + GELU_C * x * x * x))
    return 0.5 * (1.0 + t) + 0.5 * x * (1.0 - t * t) * GELU_K * (1.0 + 3.0 * GELU_C * x * x)


def _rope_partner(x):
    lane = lax.broadcasted_iota(jnp.int32, x.shape, 1)
    half = QK_ROPE // 2
    sw = jnp.where(lane < QK_NOPE + half, pltpu.roll(x, LANES - half, 1), pltpu.roll(x, half, 1))
    return jnp.where((lane >= QK_NOPE) & (lane < QK_HEAD), sw, 0.0)


def _rope(x, cos_t, sin_t):
    return x * cos_t + _rope_partner(x) * sin_t


def _rope_t(dy, cos_t, sin_t):
    return dy * cos_t + _rope_partner(dy * sin_t)


def _rowwise(body, name, s, tm, rows=(), halos=(), fulls=(), outs=(), accs=()):
    n = s // tm
    hb = tm // SUBLANES
    last8 = s // SUBLANES - 1
    in_specs, args = [], []
    for a in rows:
        in_specs.append(pl.BlockSpec((tm, a.shape[1]), lambda i: (i, 0)))
        args.append(a)
    for a in halos:
        in_specs.append(pl.BlockSpec((SUBLANES, a.shape[1]), lambda i: (jnp.maximum(i * hb - 1, 0), 0)))
        in_specs.append(pl.BlockSpec((SUBLANES, a.shape[1]), lambda i: (jnp.minimum((i + 1) * hb, last8), 0)))
        args += [a, a]
    for a in fulls:
        in_specs.append(pl.BlockSpec(a.shape, lambda i, nd=a.ndim: (0,) * nd))
        args.append(a)
    out_shape, out_specs = [], []
    for c, dt in outs:
        out_shape.append(jax.ShapeDtypeStruct((s, c), dt))
        out_specs.append(pl.BlockSpec((tm, c), lambda i: (i, 0)))
    for shp, dt in accs:
        out_shape.append(jax.ShapeDtypeStruct(shp, dt))
        out_specs.append(pl.BlockSpec(shp, lambda i, nd=len(shp): (0,) * nd))

    def kern(*refs):
        body(pl.program_id(0), n, *refs)

    return pl.pallas_call(kern, grid=(n,), in_specs=in_specs, out_specs=out_specs, out_shape=out_shape, name=name,
                          compiler_params=_cparams(1))(*args)


def _matmul_tn(a, b, name):
    t, m = a.shape
    n = b.shape[1]
    bm = 256 if m % 256 == 0 else m
    bn = n if n <= 2048 else 1408
    bt = min(512, t)
    nt = t // bt

    def kern(a_ref, b_ref, o_ref):
        @pl.when(pl.program_id(2) == 0)
        def _():
            o_ref[...] = jnp.zeros_like(o_ref)
        o_ref[...] += _bdot_tn(a_ref[...], b_ref[...])

    return pl.pallas_call(
        kern, grid=(m // bm, n // bn, nt),
        in_specs=[pl.BlockSpec((bt, bm), lambda i, j, k: (k, i)), pl.BlockSpec((bt, bn), lambda i, j, k: (k, j))],
        out_specs=pl.BlockSpec((bm, bn), lambda i, j, k: (i, j)),
        out_shape=jax.ShapeDtypeStruct((m, n), F32), name=name, compiler_params=_cparams(3))(a, b)


def _in_proj(x, g, w_in_p, tm):
    def body(i, n, x_ref, g_ref, w_ref, xr, yg, cq, ckv, krp, hb):
        xv = x_ref[...]
        h = (xv * _rstd(xv) * g_ref[...]).astype(BF16)
        hb[...] = h
        p = jnp.dot(h, w_ref[...], preferred_element_type=F32)
        xr[...] = p[:, :LRU_W]
        yg[...] = p[:, LRU_W:2 * LRU_W]
        cq[...] = p[:, 2 * LRU_W:2 * LRU_W + Q_LORA]
        ckv[...] = p[:, 2 * LRU_W + Q_LORA:OFF_KR]
        krp[...] = p[:, OFF_KR:IN_PAD]

    return _rowwise(body, "in_proj", x.shape[0], tm, rows=[x], fulls=[g, w_in_p],
                    outs=[(LRU_W, F32), (LRU_W, F32), (Q_LORA, F32), (KV_LORA, F32), (LANES, F32), (D_MODEL, BF16)])


def _lru_gates(x, halo, cw_ref, pv_ref, wa_ref, wi_ref, rev):
    down = not rev
    xc = pv_ref[0:1, :] + jnp.zeros_like(x)
    for j in range(CONV_W):
        k = j if rev else CONV_W - 1 - j
        xc = xc + cw_ref[k:k + 1, :] * _shift(x, j, halo, down)
    r = _sigmoid(_bdot(xc, wa_ref[...]) + pv_ref[1:2, :])
    ig = _sigmoid(_bdot(xc, wi_ref[...]) + pv_ref[2:3, :])
    lam = pv_ref[3:4, :]
    sp = jnp.maximum(-lam, 0.0) + jnp.log(1.0 + jnp.exp(-jnp.abs(lam)))
    log_a = (-LRU_C) * r * sp
    a = jnp.exp(log_a)
    z = 2.0 * log_a
    series = -(z * (1.0 + z * (0.5 + z * (1.0 / 6.0 + z * (1.0 / 24.0)))))
    om = jnp.where(z > -0.02, series, 1.0 - jnp.exp(z))
    mult = jnp.sqrt(om)
    return xc, r, ig, sp, a, mult


def _lru_scan_fwd(xr, cw, pv, wa, wi, rev, t):
    s = xr.shape[0]
    n = s // t
    hb = t // SUBLANES
    last8 = s // SUBLANES - 1
    down = not rev

    def kern(x_ref, halo_ref, cw_ref, pv_ref, wa_ref, wi_ref, h_ref, carry_ref):
        i = pl.program_id(0)
        _zero_first(i, carry_ref)
        halo = jnp.where(i == 0, 0.0, halo_ref[...])
        xc, r, ig, sp, a, mult = _lru_gates(x_ref[...], halo, cw_ref, pv_ref, wa_ref, wi_ref, rev)
        aa, bb = _scan(a, mult * ig * xc, down)
        h_ref[...] = aa * carry_ref[...] + bb
        carry_ref[...] = h_ref[pl.ds(t - 1 if down else 0, 1), :]

    if rev:
        blk = lambda i: (n - 1 - i, 0)
        hal = lambda i: (jnp.minimum((n - i) * hb, last8), 0)
    else:
        blk = lambda i: (i, 0)
        hal = lambda i: (jnp.maximum(i * hb - 1, 0), 0)
    full = lambda a: pl.BlockSpec(a.shape, lambda i: (0, 0))
    return pl.pallas_call(
        kern, grid=(n,),
        in_specs=[pl.BlockSpec((t, LRU_W), blk), pl.BlockSpec((SUBLANES, LRU_W), hal), full(cw), full(pv), full(wa), full(wi)],
        out_specs=pl.BlockSpec((t, LRU_W), blk), out_shape=jax.ShapeDtypeStruct((s, LRU_W), F32),
        scratch_shapes=[pltpu.VMEM((1, LRU_W), F32)], name="lru_scan_rev" if rev else "lru_scan_fwd",
        compiler_params=_cparams(1))(xr, xr, cw, pv, wa, wi)


def _lru_scan_bwd(xr, h, dh, cw, pv, wa, wi, rev, t):
    s = xr.shape[0]
    n = s // t
    hb = t // SUBLANES
    last8 = s // SUBLANES - 1
    down = not rev

    def kern(x_ref, xh_ref, h_ref, hh_ref, dh_ref, cw_ref, pv_ref, wa_ref, wi_ref,
             dx_ref, gwa_ref, gwi_ref, gv_ref, p_ref, dxc_halo_ref, tmp_ref):
        i = pl.program_id(0)
        _zero_first(i, gwa_ref, gwi_ref, gv_ref, p_ref, dxc_halo_ref)
        at_start = i == n - 1
        x = x_ref[...]
        xhalo = jnp.where(at_start, 0.0, xh_ref[...])
        hhalo = jnp.where(at_start, 0.0, hh_ref[...])
        xc, r, ig, sp, a, mult = _lru_gates(x, xhalo, cw_ref, pv_ref, wa_ref, wi_ref, rev)
        h_prev = _shift(h_ref[...], 1, hhalo, down)
        row = lax.broadcasted_iota(jnp.int32, x.shape, 0)
        edge = t - 1 if down else 0
        dh_mod = dh_ref[...] + jnp.where(row == edge, p_ref[...], 0.0)
        a_next = _shift(a, 1, jnp.zeros((SUBLANES, LRU_W), F32), not down)
        _, g = _scan(a_next, dh_mod, not down)
        tmp_ref[...] = a * g
        p_ref[...] = tmp_ref[pl.ds(0 if down else t - 1, 1), :]
        da = g * h_prev
        d_ig = g * mult * xc
        d_xc = g * mult * ig
        d_om = g * ig * xc * (0.5 / jnp.maximum(mult, 1e-30))
        d_log_a = da * a - 2.0 * d_om * a * a
        d_r = d_log_a * ((-LRU_C) * sp)
        d_sp = jnp.sum(d_log_a * ((-LRU_C) * r), axis=0, keepdims=True)
        lam = pv_ref[3:4, :]
        gv_ref[7:8, :] += d_sp * (-_sigmoid(-lam))
        d_ga = d_r * r * (1.0 - r)
        d_gi = d_ig * ig * (1.0 - ig)
        _acc_row(gv_ref, 5, d_ga)
        _acc_row(gv_ref, 6, d_gi)
        d_xc = d_xc + _bdot_nt(d_ga, wa_ref[...]) + _bdot_nt(d_gi, wi_ref[...])
        gwa_ref[...] += _bdot_tn(xc, d_ga)
        gwi_ref[...] += _bdot_tn(xc, d_gi)
        _acc_row(gv_ref, 4, d_xc)
        dx = jnp.zeros_like(x)
        dxc_halo = dxc_halo_ref[...]
        for j in range(CONV_W):
            k = j if rev else CONV_W - 1 - j
            _acc_row(gv_ref, k, d_xc * _shift(x, j, xhalo, down))
            dx = dx + cw_ref[k:k + 1, :] * _shift(d_xc, j, dxc_halo, not down)
        dx_ref[...] = dx
        dxc_halo_ref[...] = d_xc[:SUBLANES] if down else d_xc[t - SUBLANES:]

    if rev:
        blk = lambda i: (i, 0)
        hal = lambda i: (jnp.minimum((i + 1) * hb, last8), 0)
    else:
        blk = lambda i: (n - 1 - i, 0)
        hal = lambda i: (jnp.maximum((n - 1 - i) * hb - 1, 0), 0)
    full = lambda a: pl.BlockSpec(a.shape, lambda i: (0, 0))
    bs = pl.BlockSpec((t, LRU_W), blk)
    hs = pl.BlockSpec((SUBLANES, LRU_W), hal)
    return pl.pallas_call(
        kern, grid=(n,),
        in_specs=[bs, hs, bs, hs, bs, full(cw), full(pv), full(wa), full(wi)],
        out_specs=[bs, pl.BlockSpec((LRU_W, LRU_W), lambda i: (0, 0)), pl.BlockSpec((LRU_W, LRU_W), lambda i: (0, 0)),
                   pl.BlockSpec((SUBLANES, LRU_W), lambda i: (0, 0))],
        out_shape=[jax.ShapeDtypeStruct((s, LRU_W), F32), jax.ShapeDtypeStruct((LRU_W, LRU_W), F32),
                   jax.ShapeDtypeStruct((LRU_W, LRU_W), F32), jax.ShapeDtypeStruct((SUBLANES, LRU_W), F32)],
        scratch_shapes=[pltpu.VMEM((1, LRU_W), F32), pltpu.VMEM((SUBLANES, LRU_W), F32), pltpu.VMEM((t, LRU_W), F32)],
        name="lru_bwd_rev" if rev else "lru_bwd_fwd", compiler_params=_cparams(1))(xr, xr, h, h, dh, cw, pv, wa, wi)


def _mla_qkv(cq, ckv, krp, cos_t, sin_t, g_qa, g_kva, g_qn, g_kn, w_uq_p, w_uk_p, w_uv, tm):
    scale = QK_HEAD ** -0.5

    def body(i, n, cq_ref, ckv_ref, kr_ref, c_ref, s_ref, gqa, gkva, gqn, gkn, wq, wk, wv, q_out, k_out, v_out):
        cosv, sinv = c_ref[...], s_ref[...]
        cqv = cq_ref[...]
        qr = _bdot(cqv * _rstd(cqv) * gqa[...], wq[...])
        ckvv = ckv_ref[...]
        c_kv = (ckvv * _rstd(ckvv) * gkva[...]).astype(BF16)
        kn = jnp.dot(c_kv, wk[...], preferred_element_type=F32)
        v_out[...] = jnp.dot(c_kv, wv[...], preferred_element_type=F32).astype(BF16)
        kr = kr_ref[...]
        for h in range(HEADS):
            sl = slice(h * LANES, (h + 1) * LANES)
            qh = qr[:, sl]
            qh = _rope(qh * _rstd(qh, QK_HEAD) * gqn[...], cosv, sinv) * scale
            q_out[:, sl] = qh.astype(BF16)
            kh = kn[:, sl] + kr
            kh = _rope(kh * _rstd(kh, QK_HEAD) * gkn[...], cosv, sinv)
            k_out[:, sl] = kh.astype(BF16)

    return _rowwise(body, "mla_qkv", cq.shape[0], tm, rows=[cq, ckv, krp, cos_t, sin_t],
                    fulls=[g_qa, g_kva, g_qn, g_kn, w_uq_p, w_uk_p, w_uv],
                    outs=[(HEADS * LANES, BF16), (HEADS * LANES, BF16), (MLA_W, BF16)])


def _attn_fwd(q, k, v, tq, tk):
    s = q.shape[0]
    nq, nk = s // tq, s // tk

    def kern(q_ref, k_ref, v_ref, o_ref, lse_ref):
        qs = (q_ref[:, :LANES], q_ref[:, LANES:])

        def step(j, carry):
            off = pl.multiple_of(j * tk, tk)
            kc = k_ref[pl.ds(off, tk), :]
            vc = v_ref[pl.ds(off, tk), :]
            out = []
            for h in range(2):
                m, l, acc = carry[3 * h:3 * h + 3]
                sc = lax.dot_general(qs[h], kc[:, h * LANES:(h + 1) * LANES], (((1,), (1,)), ((), ())),
                                     preferred_element_type=F32)
                mn = jnp.maximum(m, jnp.max(sc, axis=-1, keepdims=True))
                al = jnp.exp(m - mn)
                p = jnp.exp(sc - mn)
                l = al * l + jnp.sum(p, axis=-1, keepdims=True)
                acc = al * acc + jnp.dot(p.astype(BF16), vc, preferred_element_type=F32)
                out += [mn, l, acc]
            return tuple(out)

        init = (jnp.full((tq, 1), -1e30, F32), jnp.zeros((tq, 1), F32), jnp.zeros((tq, LANES), F32)) * 2
        m0, l0, a0, m1, l1, a1 = lax.fori_loop(0, nk, step, init)
        lane = lax.broadcasted_iota(jnp.int32, (tq, LANES), 1)
        o_ref[...] = jnp.where(lane < V_DIM, a0 / l0, a1 / l1)
        lse_ref[0, :, 0:1] = m0 + jnp.log(l0)
        lse_ref[0, :, 1:2] = m1 + jnp.log(l1)

    return pl.pallas_call(
        kern, grid=(HEADS // 2, nq),
        in_specs=[pl.BlockSpec((tq, 2 * LANES), lambda p, i: (i, p)), pl.BlockSpec((s, 2 * LANES), lambda p, i: (0, p)),
                  pl.BlockSpec((s, LANES), lambda p, i: (0, p))],
        out_specs=[pl.BlockSpec((tq, LANES), lambda p, i: (i, p)), pl.BlockSpec((1, tq, 2), lambda p, i: (p, i, 0))],
        out_shape=[jax.ShapeDtypeStruct((s, MLA_W), F32), jax.ShapeDtypeStruct((HEADS // 2, s, 2), F32)],
        name="attn_fwd", compiler_params=_cparams(2))(q, k, v)


def _attn_bwd(q, k, v, do, lse, delta, tq, tk):
    s = q.shape[0]
    nq, nk = s // tq, s // tk

    def kern(k_ref, v_ref, q_ref, do_ref, lse_ref, dl_ref, dk_ref, dv_ref, dq_ref):
        _zero_first(pl.program_id(1), dq_ref)
        dk_ref[...] = jnp.zeros_like(dk_ref)
        dv_ref[...] = jnp.zeros_like(dv_ref)
        lane_k = lax.broadcasted_iota(jnp.int32, (tk, LANES), 1)
        vp = v_ref[...]
        zero = jnp.zeros_like(vp)
        vs = (jnp.where(lane_k < V_DIM, vp, zero), jnp.where(lane_k >= V_DIM, vp, zero))
        ks = (k_ref[:, :LANES], k_ref[:, LANES:])

        def step(i, carry):
            off = pl.multiple_of(i * tq, tq)
            qc = q_ref[pl.ds(off, tq), :]
            doc = do_ref[pl.ds(off, tq), :]
            lane_q = lax.broadcasted_iota(jnp.int32, (tq, LANES), 1)
            zq = jnp.zeros_like(doc)
            dos = (jnp.where(lane_q < V_DIM, doc, zq), jnp.where(lane_q >= V_DIM, doc, zq))
            for h in range(2):
                sl = slice(h * LANES, (h + 1) * LANES)
                qh = qc[:, sl]
                lse_h = lse_ref[0, h:h + 1, pl.ds(off, tq)]
                dl_h = dl_ref[0, h:h + 1, pl.ds(off, tq)]
                st = lax.dot_general(ks[h], qh, (((1,), (1,)), ((), ())), preferred_element_type=F32)
                pt = jnp.exp(st - lse_h)
                dpt = lax.dot_general(vs[h], doc, (((1,), (1,)), ((), ())), preferred_element_type=F32)
                dst = (pt * (dpt - dl_h)).astype(BF16)
                dv_ref[...] += jnp.dot(pt.astype(BF16), dos[h], preferred_element_type=F32)
                dk_ref[:, sl] += jnp.dot(dst, qh, preferred_element_type=F32)
                dq_ref[pl.ds(off, tq), sl] += lax.dot_general(dst, ks[h], (((0,), (0,)), ((), ())),
                                                               preferred_element_type=F32)
            return carry

        lax.fori_loop(0, nq, step, 0)

    return pl.pallas_call(
        kern, grid=(HEADS // 2, nk),
        in_specs=[pl.BlockSpec((tk, 2 * LANES), lambda p, j: (j, p)), pl.BlockSpec((tk, LANES), lambda p, j: (j, p)),
                  pl.BlockSpec((s, 2 * LANES), lambda p, j: (0, p)), pl.BlockSpec((s, LANES), lambda p, j: (0, p)),
                  pl.BlockSpec((1, 2, s), lambda p, j: (p, 0, 0)), pl.BlockSpec((1, 2, s), lambda p, j: (p, 0, 0))],
        out_specs=[pl.BlockSpec((tk, 2 * LANES), lambda p, j: (j, p)), pl.BlockSpec((tk, LANES), lambda p, j: (j, p)),
                   pl.BlockSpec((s, 2 * LANES), lambda p, j: (0, p))],
        out_shape=[jax.ShapeDtypeStruct((s, HEADS * LANES), F32), jax.ShapeDtypeStruct((s, MLA_W), F32),
                   jax.ShapeDtypeStruct((s, HEADS * LANES), F32)],
        name="attn_bwd", compiler_params=_cparams(2))(k, v, q, do, lse, delta)


def _mix_out(hf, hb, yg, o, x, g_lru, g_mla, w_out, tm):
    def body(i, n, hf_ref, hb_ref, yg_ref, o_ref, x_ref, gl, gm, w_ref, x1_ref, mix_ref):
        lo = (hf_ref[...] + hb_ref[...]) * _gelu(yg_ref[...])
        ov = o_ref[...]
        mix_ref[:, :LRU_W] = (lo * _rstd(lo) * gl[...]).astype(BF16)
        mix_ref[:, LRU_W:] = (ov * _rstd(ov) * gm[...]).astype(BF16)
        x1_ref[...] = x_ref[...] + jnp.dot(mix_ref[...], w_ref[...], preferred_element_type=F32)

    return _rowwise(body, "mix_out", x.shape[0], tm, rows=[hf, hb, yg, o, x], fulls=[g_lru, g_mla, w_out],
                    outs=[(D_MODEL, F32), (2 * LRU_W, BF16)])


def _mem_kv(mem, g_mem, w_kv, g_k):
    m = mem.shape[0]

    def body(i, n, mem_ref, g_ref, w_ref, gk_ref, km_ref, vm_ref):
        mv = mem_ref[...]
        kv = _bdot(mv * _rstd(mv) * g_ref[...], w_ref[...])
        vm_ref[...] = kv[:, MEM_W:].astype(BF16)
        for h in range(MEM_HEADS):
            sl = slice(h * MEM_HD, (h + 1) * MEM_HD)
            kh = kv[:, sl]
            km_ref[:, sl] = (kh * _rstd(kh) * gk_ref[...]).astype(BF16)

    return _rowwise(body, "mem_kv", m, m, rows=[mem], fulls=[g_mem, w_kv, g_k], outs=[(MEM_W, BF16), (MEM_W, BF16)])


def _mem_attn_core(x1v, g_ref, wq_ref, gq_ref, km_ref, vm_ref):
    scale = MEM_HD ** -0.5
    hm = (x1v * _rstd(x1v) * g_ref[...]).astype(BF16)
    qr = jnp.dot(hm, wq_ref[...], preferred_element_type=F32)
    heads = []
    for h in range(MEM_HEADS):
        sl = slice(h * MEM_HD, (h + 1) * MEM_HD)
        qh = qr[:, sl]
        rs = _rstd(qh)
        qn = (qh * rs * gq_ref[...]).astype(BF16)
        sc = lax.dot_general(qn, km_ref[:, sl], (((1,), (1,)), ((), ())), preferred_element_type=F32) * scale
        e = jnp.exp(sc - jnp.max(sc, axis=-1, keepdims=True))
        p = e / jnp.sum(e, axis=-1, keepdims=True)
        oh = jnp.dot(p.astype(BF16), vm_ref[:, sl], preferred_element_type=F32)
        heads.append((qh, rs, qn, p, oh))
    return hm, heads


def _mem_attn(x1, g, w_q, g_q, km, vm, w_o, tm):
    def body(i, n, x1_ref, g_ref, wq_ref, gq_ref, km_ref, vm_ref, wo_ref, x2_ref, ob_ref):
        x1v = x1_ref[...]
        _, heads = _mem_attn_core(x1v, g_ref, wq_ref, gq_ref, km_ref, vm_ref)
        for h in range(MEM_HEADS):
            ob_ref[:, h * MEM_HD:(h + 1) * MEM_HD] = heads[h][4].astype(BF16)
        x2_ref[...] = x1v + jnp.dot(ob_ref[...], wo_ref[...], preferred_element_type=F32)

    return _rowwise(body, "mem_attn", x1.shape[0], tm, rows=[x1], fulls=[g, w_q, g_q, km, vm, w_o],
                    outs=[(D_MODEL, F32), (MEM_W, BF16)])


def _ffn_up(x2, g, w_up, tm):
    def body(i, n, x_ref, g_ref, w_ref, gu_ref, hb_ref):
        xv = x_ref[...]
        hb_ref[...] = (xv * _rstd(xv) * g_ref[...]).astype(BF16)
        gu_ref[...] = jnp.dot(hb_ref[...], w_ref[...], preferred_element_type=F32)

    return _rowwise(body, "ffn_up", x2.shape[0], tm, rows=[x2], fulls=[g, w_up], outs=[(2 * D_FF, F32), (D_MODEL, BF16)])


def _ffn_conv(gu, prev, nxt, cw_ref, i, n):
    prev = jnp.where(i == 0, 0.0, prev)
    nxt = jnp.where(i == n - 1, 0.0, nxt)
    return (cw_ref[3:4, :] + cw_ref[0:1, :] * _shift_down(gu, 1, prev) + cw_ref[1:2, :] * gu
            + cw_ref[2:3, :] * _shift_up(gu, 1, nxt))


def _ffn_down_loss(gu_pre, x2, target, cw, w_down, tm):
    def body(i, n, gu_ref, x_ref, t_ref, pv_ref, nx_ref, cw_ref, w_ref, dy_ref, dyb_ref, act_ref, loss_ref):
        _zero_first(i, loss_ref)
        gu = _ffn_conv(gu_ref[...], pv_ref[...], nx_ref[...], cw_ref, i, n)
        g, u = gu[:, :D_FF], gu[:, D_FF:]
        act_ref[...] = (g * _sigmoid(g) * u).astype(BF16)
        y = x_ref[...] + jnp.dot(act_ref[...], w_ref[...], preferred_element_type=F32)
        e = y - t_ref[...]
        loss_ref[...] += jnp.sum(e * e)
        dy = e * (1.0 / D_MODEL)
        dy_ref[...] = dy
        dyb_ref[...] = dy.astype(BF16)

    return _rowwise(body, "ffn_down_loss", x2.shape[0], tm, rows=[gu_pre, x2, target], halos=[gu_pre], fulls=[cw, w_down],
                    outs=[(D_MODEL, F32), (D_MODEL, BF16), (D_FF, BF16)], accs=[((SUBLANES, LANES), F32)])


def _ffn_bwd_act(dyb, gu_pre, cw, w_down, tm):
    def body(i, n, dy_ref, gu_ref, pv_ref, nx_ref, cw_ref, w_ref, dgu_ref):
        d_act = lax.dot_general(dy_ref[...], w_ref[...], (((1,), (1,)), ((), ())), preferred_element_type=F32)
        gu = _ffn_conv(gu_ref[...], pv_ref[...], nx_ref[...], cw_ref, i, n)
        g, u = gu[:, :D_FF], gu[:, D_FF:]
        sg = _sigmoid(g)
        dgu_ref[:, :D_FF] = d_act * u * sg * (1.0 + g * (1.0 - sg))
        dgu_ref[:, D_FF:] = d_act * g * sg

    return _rowwise(body, "ffn_bwd_act", dyb.shape[0], tm, rows=[dyb, gu_pre], halos=[gu_pre], fulls=[cw, w_down],
                    outs=[(2 * D_FF, F32)])


def _ffn_bwd_conv(dgu, gu_pre, cw, tm):
    def body(i, n, d_ref, g_ref, dp_ref, dn_ref, gp_ref, gn_ref, cw_ref, dpre_ref, gc_ref):
        _zero_first(i, gc_ref)
        first, last = i == 0, i == n - 1
        d = d_ref[...]
        g = g_ref[...]
        dpv, dnx = jnp.where(first, 0.0, dp_ref[...]), jnp.where(last, 0.0, dn_ref[...])
        gpv, gnx = jnp.where(first, 0.0, gp_ref[...]), jnp.where(last, 0.0, gn_ref[...])
        dpre_ref[...] = (cw_ref[0:1, :] * _shift_up(d, 1, dnx) + cw_ref[1:2, :] * d
                         + cw_ref[2:3, :] * _shift_down(d, 1, dpv)).astype(BF16)
        _acc_row(gc_ref, 0, d * _shift_down(g, 1, gpv))
        _acc_row(gc_ref, 1, d * g)
        _acc_row(gc_ref, 2, d * _shift_up(g, 1, gnx))
        _acc_row(gc_ref, 3, d)

    return _rowwise(body, "ffn_bwd_conv", dgu.shape[0], tm, rows=[dgu, gu_pre], halos=[dgu, gu_pre], fulls=[cw],
                    outs=[(2 * D_FF, BF16)], accs=[((SUBLANES, 2 * D_FF), F32)])


def _ffn_bwd_in(dpre, x2, dy, g, w_up, tm):
    def body(i, n, dp_ref, x_ref, dy_ref, g_ref, w_ref, dx_ref, dxb_ref, gg_ref):
        _zero_first(i, gg_ref)
        d_h = lax.dot_general(dp_ref[...], w_ref[...], (((1,), (1,)), ((), ())), preferred_element_type=F32)
        xv = x_ref[...]
        dx, dg = _norm_bwd(xv, _rstd(xv), g_ref[...], d_h)
        _acc_row(gg_ref, 0, dg)
        dx = dx + dy_ref[...]
        dx_ref[...] = dx
        dxb_ref[...] = dx.astype(BF16)

    return _rowwise(body, "ffn_bwd_in", x2.shape[0], tm, rows=[dpre, x2, dy], fulls=[g, w_up],
                    outs=[(D_MODEL, F32), (D_MODEL, BF16)], accs=[((SUBLANES, D_MODEL), F32)])


def _mem_attn_bwd(x1, dx2, dx2b, g, w_q, g_q, km, vm, w_o, tm):
    scale = MEM_HD ** -0.5
    m = km.shape[0]

    def body(i, n, x1_ref, dx2_ref, dx2b_ref, g_ref, wq_ref, gq_ref, km_ref, vm_ref, wo_ref,
             dx1_ref, dx1b_ref, hm_ref, dqr_ref, dkm_ref, dvm_ref, gg_ref, ggq_ref):
        _zero_first(i, dkm_ref, dvm_ref, gg_ref, ggq_ref)
        x1v = x1_ref[...]
        hm, heads = _mem_attn_core(x1v, g_ref, wq_ref, gq_ref, km_ref, vm_ref)
        hm_ref[...] = hm
        d_o = lax.dot_general(dx2b_ref[...], wo_ref[...], (((1,), (1,)), ((), ())), preferred_element_type=F32)
        for h in range(MEM_HEADS):
            sl = slice(h * MEM_HD, (h + 1) * MEM_HD)
            qh, rs, qn, p, _ = heads[h]
            d_oh = d_o[:, sl].astype(BF16)
            dp = lax.dot_general(d_oh, vm_ref[:, sl], (((1,), (1,)), ((), ())), preferred_element_type=F32)
            ds = (p * (dp - jnp.sum(dp * p, axis=-1, keepdims=True)) * scale).astype(BF16)
            dqn = jnp.dot(ds, km_ref[:, sl], preferred_element_type=F32)
            dkm_ref[:, sl] += lax.dot_general(ds, qn, (((0,), (0,)), ((), ())), preferred_element_type=F32)
            dvm_ref[:, sl] += lax.dot_general(p.astype(BF16), d_oh, (((0,), (0,)), ((), ())), preferred_element_type=F32)
            dqh, dgq = _norm_bwd(qh, rs, gq_ref[...], dqn)
            _acc_row(ggq_ref, 0, dgq)
            dqr_ref[:, sl] = dqh.astype(BF16)
        d_hm = lax.dot_general(dqr_ref[...], wq_ref[...], (((1,), (1,)), ((), ())), preferred_element_type=F32)
        dx, dg = _norm_bwd(x1v, _rstd(x1v), g_ref[...], d_hm)
        _acc_row(gg_ref, 0, dg)
        dx = dx + dx2_ref[...]
        dx1_ref[...] = dx
        dx1b_ref[...] = dx.astype(BF16)

    return _rowwise(body, "mem_attn_bwd", x1.shape[0], tm, rows=[x1, dx2, dx2b], fulls=[g, w_q, g_q, km, vm, w_o],
                    outs=[(D_MODEL, F32), (D_MODEL, BF16), (D_MODEL, BF16), (MEM_W, BF16)],
                    accs=[((m, MEM_W), F32), ((m, MEM_W), F32), ((SUBLANES, D_MODEL), F32), ((SUBLANES, MEM_HD), F32)])


def _mem_kv_bwd(mem, g_mem, w_kv, g_k, dkm, dvm):
    m = mem.shape[0]

    def body(i, n, mem_ref, dkm_ref, dvm_ref, g_ref, w_ref, gk_ref, gw_ref, gg_ref, ggk_ref, dkv_ref):
        gg_ref[...] = jnp.zeros_like(gg_ref)
        ggk_ref[...] = jnp.zeros_like(ggk_ref)
        mv = mem_ref[...]
        rs_m = _rstd(mv)
        mem_n = (mv * rs_m * g_ref[...]).astype(BF16)
        kv = jnp.dot(mem_n, w_ref[...], preferred_element_type=F32)
        for h in range(MEM_HEADS):
            sl = slice(h * MEM_HD, (h + 1) * MEM_HD)
            kh = kv[:, sl]
            dkh, dgk = _norm_bwd(kh, _rstd(kh), gk_ref[...], dkm_ref[:, sl])
            _acc_row(ggk_ref, 0, dgk)
            dkv_ref[:, sl] = dkh.astype(BF16)
        dkv_ref[:, MEM_W:] = dvm_ref[...].astype(BF16)
        gw_ref[...] = lax.dot_general(mem_n, dkv_ref[...], (((0,), (0,)), ((), ())), preferred_element_type=F32)
        d_mn = lax.dot_general(dkv_ref[...], w_ref[...], (((1,), (1,)), ((), ())), preferred_element_type=F32)
        _acc_row(gg_ref, 0, d_mn * (mv * rs_m))

    return _rowwise(body, "mem_kv_bwd", m, m, rows=[mem, dkm, dvm], fulls=[g_mem, w_kv, g_k],
                    accs=[((D_MODEL, 2 * MEM_W), F32), ((SUBLANES, D_MODEL), F32), ((SUBLANES, MEM_HD), F32),
                          ((m, 2 * MEM_W), BF16)])


def _mix_out_bwd(dx1b, hf, hb, yg, o, g_lru, g_mla, w_out, tm):
    def body(i, n, dx_ref, hf_ref, hb_ref, yg_ref, o_ref, gl, gm, w_ref, dh_ref, dyg_ref, dob_ref, dl_ref, ggl_ref, ggm_ref):
        _zero_first(i, ggl_ref, ggm_ref)
        dmix = lax.dot_general(dx_ref[...], w_ref[...], (((1,), (1,)), ((), ())), preferred_element_type=F32)
        hs = hf_ref[...] + hb_ref[...]
        ygv = yg_ref[...]
        ge = _gelu(ygv)
        lo = hs * ge
        d_lo, dgl = _norm_bwd(lo, _rstd(lo), gl[...], dmix[:, :LRU_W])
        _acc_row(ggl_ref, 0, dgl)
        dh_ref[...] = d_lo * ge
        dyg_ref[...] = d_lo * hs * _gelu_grad(ygv)
        ov = o_ref[...]
        d_o, dgm = _norm_bwd(ov, _rstd(ov), gm[...], dmix[:, LRU_W:])
        _acc_row(ggm_ref, 0, dgm)
        dob_ref[...] = d_o.astype(BF16)
        prod = d_o * ov
        lane_w = lax.broadcasted_iota(jnp.int32, prod.shape, 1)
        lane = lax.broadcasted_iota(jnp.int32, (prod.shape[0], LANES), 1)
        dl = jnp.zeros((prod.shape[0], LANES), F32)
        for h in range(HEADS):
            in_head = (lane_w >= h * V_DIM) & (lane_w < (h + 1) * V_DIM)
            dl = dl + jnp.where(lane == h, jnp.sum(jnp.where(in_head, prod, 0.0), axis=-1, keepdims=True), 0.0)
        dl_ref[...] = dl

    return _rowwise(body, "mix_out_bwd", dx1b.shape[0], tm, rows=[dx1b, hf, hb, yg, o], fulls=[g_lru, g_mla, w_out],
                    outs=[(LRU_W, F32), (LRU_W, F32), (MLA_W, BF16), (LANES, F32)],
                    accs=[((SUBLANES, LRU_W), F32), ((SUBLANES, MLA_W), F32)])


def _mla_qkv_bwd(cq, ckv, krp, cos_t, sin_t, dq, dk, dv, g_qa, g_kva, g_qn, g_kn, w_uq_p, w_uk_p, w_uv, tm):
    scale = QK_HEAD ** -0.5

    def body(i, n, cq_ref, ckv_ref, kr_ref, c_ref, s_ref, dq_ref, dk_ref, dv_ref, gqa, gkva, gqn, gkn, wq, wk, wv,
             dcq_ref, dckv_ref, dkr_ref, cqb_ref, dqr_ref, ckvb_ref, dkn_ref, dvb_ref, ggqa, ggkva, ggqn, ggkn):
        _zero_first(i, ggqa, ggkva, ggqn, ggkn)
        cosv, sinv = c_ref[...], s_ref[...]
        cqv = cq_ref[...]
        rs_q = _rstd(cqv)
        cqb_ref[...] = (cqv * rs_q * gqa[...]).astype(BF16)
        qr = jnp.dot(cqb_ref[...], wq[...], preferred_element_type=F32)
        ckvv = ckv_ref[...]
        rs_kv = _rstd(ckvv)
        ckvb_ref[...] = (ckvv * rs_kv * gkva[...]).astype(BF16)
        kn = jnp.dot(ckvb_ref[...], wk[...], preferred_element_type=F32)
        kr = kr_ref[...]
        dkr = jnp.zeros_like(kr)
        for h in range(HEADS):
            sl = slice(h * LANES, (h + 1) * LANES)
            qh = qr[:, sl]
            d_qn = _rope_t(dq_ref[:, sl] * scale, cosv, sinv)
            dqh, dgq = _norm_bwd(qh, _rstd(qh, QK_HEAD), gqn[...], d_qn, QK_HEAD)
            _acc_row(ggqn, 0, dgq)
            dqr_ref[:, sl] = dqh.astype(BF16)
            kh = kn[:, sl] + kr
            d_kn = _rope_t(dk_ref[:, sl], cosv, sinv)
            dkh, dgk = _norm_bwd(kh, _rstd(kh, QK_HEAD), gkn[...], d_kn, QK_HEAD)
            _acc_row(ggkn, 0, dgk)
            dkn_ref[:, sl] = dkh.astype(BF16)
            dkr = dkr + dkh
        dkr_ref[...] = dkr
        dvb_ref[...] = dv_ref[...].astype(BF16)
        d_cq = lax.dot_general(dqr_ref[...], wq[...], (((1,), (1,)), ((), ())), preferred_element_type=F32)
        dcq, dg = _norm_bwd(cqv, rs_q, gqa[...], d_cq)
        _acc_row(ggqa, 0, dg)
        dcq_ref[...] = dcq
        d_ckv = (lax.dot_general(dkn_ref[...], wk[...], (((1,), (1,)), ((), ())), preferred_element_type=F32)
                 + lax.dot_general(dvb_ref[...], wv[...], (((1,), (1,)), ((), ())), preferred_element_type=F32))
        dckv, dg = _norm_bwd(ckvv, rs_kv, gkva[...], d_ckv)
        _acc_row(ggkva, 0, dg)
        dckv_ref[...] = dckv

    return _rowwise(body, "mla_qkv_bwd", cq.shape[0], tm, rows=[cq, ckv, krp, cos_t, sin_t, dq, dk, dv],
                    fulls=[g_qa, g_kva, g_qn, g_kn, w_uq_p, w_uk_p, w_uv],
                    outs=[(Q_LORA, F32), (KV_LORA, F32), (LANES, F32), (Q_LORA, BF16), (HEADS * LANES, BF16),
                          (KV_LORA, BF16), (HEADS * LANES, BF16), (MLA_W, BF16)],
                    accs=[((SUBLANES, Q_LORA), F32), ((SUBLANES, KV_LORA), F32), ((SUBLANES, LANES), F32),
                          ((SUBLANES, LANES), F32)])


def _in_proj_bwd(x, dx1, dxr_f, dxr_b, dyg, dcq, dckv, dkrp, g, w_in_p, tm):
    def body(i, n, x_ref, dx1_ref, df_ref, db_ref, dyg_ref, dcq_ref, dckv_ref, dkr_ref, g_ref, w_ref, gx_ref, dp_ref, gg_ref):
        _zero_first(i, gg_ref)
        dp_ref[:, :LRU_W] = (df_ref[...] + db_ref[...]).astype(BF16)
        dp_ref[:, LRU_W:2 * LRU_W] = dyg_ref[...].astype(BF16)
        dp_ref[:, 2 * LRU_W:2 * LRU_W + Q_LORA] = dcq_ref[...].astype(BF16)
        dp_ref[:, 2 * LRU_W + Q_LORA:OFF_KR] = dckv_ref[...].astype(BF16)
        dp_ref[:, OFF_KR:] = dkr_ref[...].astype(BF16)
        d_h = lax.dot_general(dp_ref[...], w_ref[...], (((1,), (1,)), ((), ())), preferred_element_type=F32)
        xv = x_ref[...]
        dx, dg = _norm_bwd(xv, _rstd(xv), g_ref[...], d_h)
        _acc_row(gg_ref, 0, dg)
        gx_ref[...] = dx + dx1_ref[...]

    return _rowwise(body, "in_proj_bwd", x.shape[0], tm, rows=[x, dx1, dxr_f, dxr_b, dyg, dcq, dckv, dkrp],
                    fulls=[g, w_in_p], outs=[(D_MODEL, F32), (IN_PAD, BF16)], accs=[((SUBLANES, D_MODEL), F32)])


ANY = pl.BlockSpec(memory_space=pl.ANY)


def _chip_peers(x, y):
    return ((1 - x, y), (x, 1 - y), (1 - x, 1 - y))


def _gather_chips(w):
    def kern(w_ref, o_ref, ssem, rsem, lsem):
        x, y, c = lax.axis_index("x"), lax.axis_index("y"), lax.axis_index("c")
        mine = pltpu.make_async_copy(w_ref, o_ref.at[2 * x + y], lsem)
        mine.start()
        cps = [pltpu.make_async_remote_copy(w_ref, o_ref.at[2 * x + y], ssem.at[j], rsem.at[j],
                                            device_id=(px, py, c), device_id_type=MESH)
               for j, (px, py) in enumerate(_chip_peers(x, y))]
        for cp in cps:
            cp.start()
        for cp in cps:
            cp.wait()
        mine.wait()

    return pl.pallas_call(
        kern, in_specs=[ANY], out_specs=ANY, out_shape=jax.ShapeDtypeStruct((N_CHIPS,) + w.shape, w.dtype),
        scratch_shapes=[pltpu.SemaphoreType.DMA((3,)), pltpu.SemaphoreType.DMA((3,)), pltpu.SemaphoreType.DMA],
        name="gather_weights")(w)


def _swap_halves(g):
    def kern(g_ref, o_ref, ssem, rsem):
        x, y, c = lax.axis_index("x"), lax.axis_index("y"), lax.axis_index("c")
        cps = [pltpu.make_async_remote_copy(g_ref.at[k, 1 - c], o_ref.at[k], ssem.at[k], rsem.at[k],
                                            device_id=(x, y, 1 - c), device_id_type=MESH) for k in range(N_CHIPS)]
        for cp in cps:
            cp.start()
        for cp in cps:
            cp.wait()

    return pl.pallas_call(
        kern, in_specs=[ANY], out_specs=ANY, out_shape=jax.ShapeDtypeStruct((N_CHIPS,) + g.shape[2:], g.dtype),
        scratch_shapes=[pltpu.SemaphoreType.DMA((N_CHIPS,)), pltpu.SemaphoreType.DMA((N_CHIPS,))],
        name="grad_swap_halves")(g)


def _scatter_chips(a):
    def kern(a_ref, o_ref, ssem, rsem, lsem):
        x, y, c = lax.axis_index("x"), lax.axis_index("y"), lax.axis_index("c")
        me = 2 * x + y
        mine = pltpu.make_async_copy(a_ref.at[me], o_ref.at[me], lsem)
        mine.start()
        cps = [pltpu.make_async_remote_copy(a_ref.at[2 * px + py], o_ref.at[me], ssem.at[j], rsem.at[j],
                                            device_id=(px, py, c), device_id_type=MESH)
               for j, (px, py) in enumerate(_chip_peers(x, y))]
        for cp in cps:
            cp.start()
        for cp in cps:
            cp.wait()
        mine.wait()

    return pl.pallas_call(
        kern, in_specs=[ANY], out_specs=ANY, out_shape=jax.ShapeDtypeStruct(a.shape, a.dtype),
        scratch_shapes=[pltpu.SemaphoreType.DMA((3,)), pltpu.SemaphoreType.DMA((3,)), pltpu.SemaphoreType.DMA],
        name="grad_scatter_chips")(a)


def _join_halves(r):
    def kern(r_ref, o_ref, ssem, rsem, lsem):
        x, y, c = lax.axis_index("x"), lax.axis_index("y"), lax.axis_index("c")
        mine = pltpu.make_async_copy(r_ref, o_ref.at[c], lsem)
        mine.start()
        cp = pltpu.make_async_remote_copy(r_ref, o_ref.at[c], ssem, rsem, device_id=(x, y, 1 - c), device_id_type=MESH)
        cp.start()
        cp.wait()
        mine.wait()

    return pl.pallas_call(
        kern, in_specs=[ANY], out_specs=ANY, out_shape=jax.ShapeDtypeStruct((2,) + r.shape, r.dtype),
        scratch_shapes=[pltpu.SemaphoreType.DMA, pltpu.SemaphoreType.DMA, pltpu.SemaphoreType.DMA],
        name="grad_join_halves")(r)


def _add_halves(g, got, c):
    hrows = g.shape[2]

    def kern(c_ref, g_ref, b_ref, o_ref):
        o_ref[...] = g_ref[...] + b_ref[...]

    blk = (None, PACK_ROWS, LANES)
    return pl.pallas_call(
        kern,
        grid_spec=pltpu.PrefetchScalarGridSpec(
            num_scalar_prefetch=1, grid=(N_CHIPS, hrows // PACK_ROWS),
            in_specs=[pl.BlockSpec((None, None, PACK_ROWS, LANES), lambda k, i, c_ref: (k, c_ref[0], i, 0)),
                      pl.BlockSpec(blk, lambda k, i, c_ref: (k, i, 0))],
            out_specs=pl.BlockSpec(blk, lambda k, i, c_ref: (k, i, 0))),
        out_shape=jax.ShapeDtypeStruct(got.shape, F32), name="grad_add_halves", compiler_params=_cparams(2))(c, g, got)


def _sum_chips(b):
    hrows = b.shape[1]

    def kern(b_ref, o_ref):
        o_ref[...] = ((b_ref[0] + b_ref[1]) + b_ref[2]) + b_ref[3]

    return pl.pallas_call(
        kern, grid=(hrows // PACK_ROWS,), in_specs=[pl.BlockSpec((N_CHIPS, PACK_ROWS, LANES), lambda i: (0, i, 0))],
        out_specs=pl.BlockSpec((PACK_ROWS, LANES), lambda i: (i, 0)),
        out_shape=jax.ShapeDtypeStruct(b.shape[1:], F32), name="grad_sum_chips", compiler_params=_cparams(1))(b)


def _adamw(w, g, m, v):
    rows = w.shape[0]
    c1 = 1.0 - ADAM_B1 ** ADAM_STEP
    c2 = 1.0 - ADAM_B2 ** ADAM_STEP

    def kern(w_ref, g_ref, m_ref, v_ref, d_ref, mo_ref, vo_ref):
        gv = g_ref[...]
        mn = ADAM_B1 * m_ref[...] + (1.0 - ADAM_B1) * gv
        vn = ADAM_B2 * v_ref[...] + (1.0 - ADAM_B2) * (gv * gv)
        mo_ref[...] = mn
        vo_ref[...] = vn
        d_ref[...] = (-ADAM_LR) * ((mn / c1) / (jnp.sqrt(vn / c2) + ADAM_EPS) + ADAM_WD * w_ref[...])

    spec = pl.BlockSpec((PACK_ROWS, LANES), lambda i: (i, 0))
    return pl.pallas_call(
        kern, grid=(rows // PACK_ROWS,), in_specs=[spec] * 4, out_specs=[spec] * 3,
        out_shape=[jax.ShapeDtypeStruct(w.shape, F32)] * 3, name="adamw", compiler_params=_cparams(1))(w, g, m, v)


def _pad_rows(flat, rows):
    return jnp.pad(flat, (0, rows * LANES - flat.shape[0])).reshape(rows, LANES)


def _round_up(n, m):
    return (n + m - 1) // m * m


def _shard_shape(shape, axis):
    return tuple(s // N_CHIPS if a == axis else s for a, s in enumerate(shape))


def _to_shards(full, axis):
    shape = full.shape
    t = full.reshape(shape[:axis] + (N_CHIPS, shape[axis] // N_CHIPS) + shape[axis + 1:])
    return jnp.moveaxis(t, axis, 0).reshape(N_CHIPS, -1)


def _from_shards(sh, shape, axis):
    t = sh.reshape((N_CHIPS,) + _shard_shape(shape, axis))
    t = jnp.moveaxis(t, 0, axis)
    return t.reshape(shape)


def _pack_weights_bf16(p):
    parts = []
    for name, shape, axis, big in SHARDED:
        a = p[name].reshape(-1)
        parts.append(a.astype(BF16) if big else lax.bitcast_convert_type(a, BF16).reshape(-1))
    flat = jnp.concatenate(parts)
    return _pad_rows(flat, _round_up(-(-flat.shape[0] // LANES), 16))


def _unpack_weights_bf16(gathered):
    flat = gathered.reshape(N_CHIPS, -1)
    out, off = {}, 0
    for name, shape, axis, big in SHARDED:
        n = _numel(shape) // N_CHIPS
        if big:
            out[name] = _from_shards(flat[:, off:off + n], shape, axis)
            off += n
        else:
            seg = lax.bitcast_convert_type(flat[:, off:off + 2 * n].reshape(N_CHIPS, n, 2), F32)
            out[name] = _from_shards(seg, shape, axis)
            off += 2 * n
    return out


def _pack_local(p, prefix=""):
    parts = [p[prefix + name].reshape(-1) for name, _, _, _ in SHARDED]
    parts += [p[prefix + name].reshape(-1) for name, _ in REPLICATED]
    return jnp.concatenate(parts)


def _pack_grads(g):
    parts = [_to_shards(g[name], axis) for name, _, axis, _ in SHARDED]
    rep = jnp.concatenate([g[name].reshape(-1) for name, _ in REPLICATED])
    parts.append(jnp.broadcast_to(rep[None], (N_CHIPS, rep.shape[0])))
    return jnp.concatenate(parts, axis=1)


def _unpack_local(flat):
    out, off = {}, 0
    for name, shape, axis, _ in SHARDED:
        n = _numel(shape) // N_CHIPS
        out[name] = flat[off:off + n].reshape((1,) + _shard_shape(shape, axis))
        off += n
    for name, shape in REPLICATED:
        n = _numel(shape)
        out[name] = flat[off:off + n].reshape((1,) + shape)
        off += n
    return out


def _block_diag(w):
    eye = jnp.eye(LRU_BLOCKS, dtype=w.dtype)
    return jnp.einsum("ncd,nm->ncmd", w, eye).reshape(LRU_W, LRU_W)


def _block_diag_t(g):
    g4 = g.reshape(LRU_BLOCKS, 64, LRU_BLOCKS, 64)
    return jnp.stack([g4[n, :, n, :] for n in range(LRU_BLOCKS)])


def _pad8(a):
    return jnp.pad(a, ((0, SUBLANES - a.shape[0]), (0, 0)))


def kernel(x, mem, positions, attn_norm, w_in, lru_conv_w, lru_conv_b, lru_w_a, lru_b_a, lru_w_i, lru_b_i, lru_lambda, q_a_norm, w_uq, kv_a_norm, w_ukv, mla_q_norm, mla_k_norm, lru_out_norm, mla_out_norm, w_out, mem_attn_norm, mem_norm, w_mem_q, w_mem_kv, mem_q_norm, mem_k_norm, w_mem_o, ffn_norm, w_up, ffn_conv_w, ffn_conv_b, w_down, loss_target, m_attn_norm, m_w_in, m_lru_conv_w, m_lru_conv_b, m_lru_w_a, m_lru_b_a, m_lru_w_i, m_lru_b_i, m_lru_lambda, m_q_a_norm, m_w_uq, m_kv_a_norm, m_w_ukv, m_mla_q_norm, m_mla_k_norm, m_lru_out_norm, m_mla_out_norm, m_w_out, m_mem_attn_norm, m_mem_norm, m_w_mem_q, m_w_mem_kv, m_mem_q_norm, m_mem_k_norm, m_w_mem_o, m_ffn_norm, m_w_up, m_ffn_conv_w, m_ffn_conv_b, m_w_down, v_attn_norm, v_w_in, v_lru_conv_w, v_lru_conv_b, v_lru_w_a, v_lru_b_a, v_lru_w_i, v_lru_b_i, v_lru_lambda, v_q_a_norm, v_w_uq, v_kv_a_norm, v_w_ukv, v_mla_q_norm, v_mla_k_norm, v_lru_out_norm, v_mla_out_norm, v_w_out, v_mem_attn_norm, v_mem_norm, v_w_mem_q, v_w_mem_kv, v_mem_q_norm, v_mem_k_norm, v_w_mem_o, v_ffn_norm, v_w_up, v_ffn_conv_w, v_ffn_conv_b, v_w_down):
    given = dict(locals())
    local = {name: given[name][0] for name in WEIGHT_ORDER}
    s = x.shape[1]
    x2d, mem2d, tgt = x[0], mem[0], loss_target[0]
    tm = min(256, s)
    tm_ffn = min(128, s)
    t_scan = min(256, s)
    tq, tk = min(256, s), min(512, s)

    full = _unpack_weights_bf16(_gather_chips(_pack_weights_bf16(local)))
    row = lambda a: a.reshape(1, -1)
    b16 = lambda a: a.astype(BF16)
    zeros = lambda r, c: jnp.zeros((r, c), BF16)
    w_in_f = full["w_in"]
    w_in_p = jnp.concatenate([w_in_f[:, :OFF_KR], zeros(D_MODEL, QK_NOPE), w_in_f[:, OFF_KR:],
                              zeros(D_MODEL, LANES - QK_HEAD)], axis=1)
    w_uq_p = jnp.pad(full["w_uq"].reshape(Q_LORA, HEADS, QK_HEAD), ((0, 0), (0, 0), (0, LANES - QK_HEAD))).reshape(Q_LORA, -1)
    ukv = full["w_ukv"].reshape(KV_LORA, HEADS, QK_NOPE + V_DIM)
    w_uk_p = jnp.pad(ukv[:, :, :QK_NOPE], ((0, 0), (0, 0), (0, LANES - QK_NOPE))).reshape(KV_LORA, -1)
    w_uv = ukv[:, :, QK_NOPE:].reshape(KV_LORA, MLA_W)
    wa = [b16(_block_diag(local["lru_w_a"][d])) for d in range(2)]
    wi = [b16(_block_diag(local["lru_w_i"][d])) for d in range(2)]
    cw = [_pad8(full["lru_conv_w"][d]) for d in range(2)]
    pv = [_pad8(jnp.stack([full["lru_conv_b"][d], full["lru_b_a"][d], full["lru_b_i"][d], full["lru_lambda"][d]]))
          for d in range(2)]
    ffn_cw = _pad8(jnp.concatenate([full["ffn_conv_w"], row(local["ffn_conv_b"])], axis=0))
    g_attn, g_qa, g_kva = row(local["attn_norm"]), row(local["q_a_norm"]), row(local["kv_a_norm"])
    g_qn = jnp.pad(row(local["mla_q_norm"]), ((0, 0), (0, LANES - QK_HEAD)))
    g_kn = jnp.pad(row(local["mla_k_norm"]), ((0, 0), (0, LANES - QK_HEAD)))
    g_lru, g_mla = row(local["lru_out_norm"]), row(local["mla_out_norm"])
    g_memattn, g_mem = row(local["mem_attn_norm"]), row(local["mem_norm"])
    g_mq, g_mk, g_ffn = row(local["mem_q_norm"]), row(local["mem_k_norm"]), row(local["ffn_norm"])

    inv = ROPE_THETA ** (-jnp.arange(0, QK_ROPE, 2, dtype=F32) / QK_ROPE)
    ang = positions[0].astype(F32)[:, None] * inv
    cosv, sinv = jnp.cos(ang), jnp.sin(ang)
    ones, zer = jnp.ones((s, QK_NOPE), F32), jnp.zeros((s, LANES - QK_HEAD), F32)
    cos_t = jnp.concatenate([ones, cosv, cosv, zer + 1.0], axis=1)
    sin_t = jnp.concatenate([ones * 0.0, -sinv, sinv, zer], axis=1)

    xr, yg, cq, ckv, krp, hb_in = _in_proj(x2d, g_attn, w_in_p, tm)
    h_f = _lru_scan_fwd(xr, cw[0], pv[0], wa[0], wi[0], False, t_scan)
    h_b = _lru_scan_fwd(xr, cw[1], pv[1], wa[1], wi[1], True, t_scan)
    q, k, v = _mla_qkv(cq, ckv, krp, cos_t, sin_t, g_qa, g_kva, g_qn, g_kn, w_uq_p, w_uk_p, w_uv, tm)
    o, lse = _attn_fwd(q, k, v, tq, tk)
    x1, mixed = _mix_out(h_f, h_b, yg, o, x2d, g_lru, g_mla, full["w_out"], tm)
    km, vm = _mem_kv(mem2d, g_mem, full["w_mem_kv"], g_mk)
    x2, o_mem = _mem_attn(x1, g_memattn, full["w_mem_q"], g_mq, km, vm, full["w_mem_o"], tm)
    gu_pre, hb_ffn = _ffn_up(x2, g_ffn, full["w_up"], tm)
    dy, dyb, act, loss_acc = _ffn_down_loss(gu_pre, x2, tgt, ffn_cw, full["w_down"], tm_ffn)
    loss = lax.psum(loss_acc[0, 0] * (0.5 / D_MODEL), ("x", "y", "c"))

    grads = {}
    grads["w_down"] = _matmul_tn(act, dyb, "grad_w_down")
    (dgu,) = _ffn_bwd_act(dyb, gu_pre, ffn_cw, full["w_down"], tm_ffn)
    dpre, g_conv = _ffn_bwd_conv(dgu, gu_pre, ffn_cw, tm_ffn)
    grads["ffn_conv_w"], grads["ffn_conv_b"] = g_conv[:3], g_conv[3]
    grads["w_up"] = _matmul_tn(hb_ffn, dpre, "grad_w_up")
    dx2, dx2b, gg = _ffn_bwd_in(dpre, x2, dy, g_ffn, full["w_up"], tm)
    grads["ffn_norm"] = gg[0]
    grads["w_mem_o"] = _matmul_tn(o_mem, dx2b, "grad_w_mem_o")
    dx1, dx1b, hm, dqr_mem, dkm, dvm, gg, ggq = _mem_attn_bwd(x1, dx2, dx2b, g_memattn, full["w_mem_q"], g_mq, km, vm,
                                                                 full["w_mem_o"], tm)
    grads["mem_attn_norm"], grads["mem_q_norm"] = gg[0], ggq[0]
    grads["w_mem_q"] = _matmul_tn(hm, dqr_mem, "grad_w_mem_q")
    grads["w_mem_kv"], gg, ggk, _ = _mem_kv_bwd(mem2d, g_mem, full["w_mem_kv"], g_mk, dkm, dvm)
    grads["mem_norm"], grads["mem_k_norm"] = gg[0], ggk[0]
    grads["w_out"] = _matmul_tn(mixed, dx1b, "grad_w_out")
    dh, dyg, dob, dl128, ggl, ggm = _mix_out_bwd(dx1b, h_f, h_b, yg, o, g_lru, g_mla, full["w_out"], tm)
    grads["lru_out_norm"], grads["mla_out_norm"] = ggl[0], ggm[0]
    lse_t = jnp.transpose(lse, (0, 2, 1))
    delta_t = jnp.transpose(dl128[:, :HEADS]).reshape(HEADS // 2, 2, s)
    dk, dv, dq = _attn_bwd(q, k, v, dob, lse_t, delta_t, tq, tk)
    (dcq, dckv, dkrp, cqb, dqr, ckvb, dkn, dvb, ggqa, ggkva, ggqn, ggkn) = _mla_qkv_bwd(
        cq, ckv, krp, cos_t, sin_t, dq, dk, dv, g_qa, g_kva, g_qn, g_kn, w_uq_p, w_uk_p, w_uv, tm)
    grads["q_a_norm"], grads["kv_a_norm"] = ggqa[0], ggkva[0]
    grads["mla_q_norm"], grads["mla_k_norm"] = ggqn[0, :QK_HEAD], ggkn[0, :QK_HEAD]
    g_uq_p = _matmul_tn(cqb, dqr, "grad_w_uq")
    grads["w_uq"] = g_uq_p.reshape(Q_LORA, HEADS, LANES)[:, :, :QK_HEAD].reshape(Q_LORA, -1)
    g_uk_p = _matmul_tn(ckvb, dkn, "grad_w_uk").reshape(KV_LORA, HEADS, LANES)[:, :, :QK_NOPE]
    g_uv = _matmul_tn(ckvb, dvb, "grad_w_uv").reshape(KV_LORA, HEADS, V_DIM)
    grads["w_ukv"] = jnp.concatenate([g_uk_p, g_uv], axis=2).reshape(KV_LORA, -1)
    dxr, gwa, gwi, gvec = [], [], [], []
    for d, hd in enumerate((h_f, h_b)):
        r = _lru_scan_bwd(xr, hd, dh, cw[d], pv[d], wa[d], wi[d], d == 1, t_scan)
        dxr.append(r[0])
        gwa.append(_block_diag_t(r[1]))
        gwi.append(_block_diag_t(r[2]))
        gvec.append(r[3])
    grads["lru_w_a"], grads["lru_w_i"] = jnp.stack(gwa), jnp.stack(gwi)
    grads["lru_conv_w"] = jnp.stack([gv[:CONV_W] for gv in gvec])
    for r_i, name in ((4, "lru_conv_b"), (5, "lru_b_a"), (6, "lru_b_i"), (7, "lru_lambda")):
        grads[name] = jnp.stack([gv[r_i] for gv in gvec])
    grad_x, dproj, gg = _in_proj_bwd(x2d, dx1, dxr[0], dxr[1], dyg, dcq, dckv, dkrp, g_attn, w_in_p, tm)
    grads["attn_norm"] = gg[0]
    g_in_p = _matmul_tn(hb_in, dproj, "grad_w_in")
    grads["w_in"] = jnp.concatenate([g_in_p[:, :OFF_KR], g_in_p[:, OFF_KR + QK_NOPE:OFF_KR + QK_HEAD]], axis=1)

    packed = _pack_grads(grads)
    length = packed.shape[1]
    hrows = _round_up(-(-length // (2 * LANES)), PACK_ROWS)
    packed = jnp.pad(packed, ((0, 0), (0, 2 * hrows * LANES - length))).reshape(N_CHIPS, 2, hrows, LANES)
    c_idx = lax.axis_index("c").astype(jnp.int32).reshape(1)
    chip_sum = _add_halves(packed, _swap_halves(packed), c_idx)
    reduced = _join_halves(_sum_chips(_scatter_chips(chip_sum))).reshape(2 * hrows, LANES)

    pack = lambda prefix: _pad_rows(_pack_local({n: given[prefix + n] for n in WEIGHT_ORDER}), 2 * hrows)
    delta, new_m, new_v = _adamw(pack(""), reduced, pack("m_"), pack("v_"))
    outs = [_unpack_local(a.reshape(-1)) for a in (reduced, delta, new_m, new_v)]
    return (loss, grad_x[None], *[o_[n] for o_ in outs for n in WEIGHT_ORDER])
```

```python
import functools

import jax
import jax.numpy as jnp
from jax import lax
from jax.experimental import pallas as pl
from jax.experimental.pallas import tpu as pltpu

F32, BF16 = jnp.float32, jnp.bfloat16
MESH = pl.DeviceIdType.MESH

D_MODEL = 1024
EPS = 1e-6
LRU_W = 512
LRU_BLOCKS = 8
LRU_C = 8.0
CONV_W = 4
HEADS = 8
QK_NOPE, QK_ROPE, QK_HEAD, V_DIM = 64, 32, 96, 64
Q_LORA, KV_LORA = 256, 128
MLA_W = HEADS * V_DIM
ROPE_THETA = 10000.0
IN_COLS = 2 * LRU_W + Q_LORA + KV_LORA + QK_ROPE
OFF_KR = IN_COLS - QK_ROPE
IN_PAD = 1536
MEM_HEADS, MEM_HD = 4, 128
MEM_W = MEM_HEADS * MEM_HD
D_FF = 2816
N_CHIPS = 4
ADAM_LR, ADAM_B1, ADAM_B2, ADAM_EPS, ADAM_WD, ADAM_STEP = 0.001, 0.9, 0.999, 1e-08, 0.01, 10

LANES = 128
SUBLANES = 8
VMEM_LIMIT = 56 * 1024 * 1024
PACK_ROWS = 2048

SHARDED = (
    ("w_in", (D_MODEL, IN_COLS), 1, True),
    ("lru_conv_w", (2, CONV_W, LRU_W), 2, False),
    ("lru_conv_b", (2, LRU_W), 1, False),
    ("lru_b_a", (2, LRU_W), 1, False),
    ("lru_b_i", (2, LRU_W), 1, False),
    ("lru_lambda", (2, LRU_W), 1, False),
    ("w_uq", (Q_LORA, HEADS * QK_HEAD), 1, True),
    ("w_ukv", (KV_LORA, HEADS * (QK_NOPE + V_DIM)), 1, True),
    ("w_out", (2 * LRU_W, D_MODEL), 0, True),
    ("w_mem_q", (D_MODEL, MEM_W), 0, True),
    ("w_mem_kv", (D_MODEL, 2 * MEM_W), 0, True),
    ("w_mem_o", (MEM_W, D_MODEL), 1, True),
    ("w_up", (D_MODEL, 2 * D_FF), 1, True),
    ("ffn_conv_w", (3, 2 * D_FF), 1, False),
    ("w_down", (D_FF, D_MODEL), 0, True),
)
REPLICATED = (
    ("attn_norm", (D_MODEL,)), ("lru_w_a", (2, LRU_BLOCKS, 64, 64)), ("lru_w_i", (2, LRU_BLOCKS, 64, 64)),
    ("q_a_norm", (Q_LORA,)), ("kv_a_norm", (KV_LORA,)), ("mla_q_norm", (QK_HEAD,)), ("mla_k_norm", (QK_HEAD,)),
    ("lru_out_norm", (LRU_W,)), ("mla_out_norm", (MLA_W,)), ("mem_attn_norm", (D_MODEL,)), ("mem_norm", (D_MODEL,)),
    ("mem_q_norm", (MEM_HD,)), ("mem_k_norm", (MEM_HD,)), ("ffn_norm", (D_MODEL,)), ("ffn_conv_b", (2 * D_FF,)),
)
WEIGHT_ORDER = ('attn_norm', 'w_in', 'lru_conv_w', 'lru_conv_b', 'lru_w_a', 'lru_b_a', 'lru_w_i', 'lru_b_i', 'lru_lambda',
                'q_a_norm', 'w_uq', 'kv_a_norm', 'w_ukv', 'mla_q_norm', 'mla_k_norm', 'lru_out_norm', 'mla_out_norm', 'w_out',
                'mem_attn_norm', 'mem_norm', 'w_mem_q', 'w_mem_kv', 'mem_q_norm', 'mem_k_norm', 'w_mem_o', 'ffn_norm', 'w_up',
                'ffn_conv_w', 'ffn_conv_b', 'w_down')


def _numel(shape):
    n = 1
    for s in shape:
        n *= s
    return n


def _cparams(n_axes):
    return pltpu.CompilerParams(dimension_semantics=("arbitrary",) * n_axes, vmem_limit_bytes=VMEM_LIMIT)


def _bdot(a, b):
    return jnp.dot(a.astype(BF16), b.astype(BF16), preferred_element_type=F32)


def _bdot_nt(a, b):
    return lax.dot_general(a.astype(BF16), b.astype(BF16), (((1,), (1,)), ((), ())), preferred_element_type=F32)


def _bdot_tn(a, b):
    return lax.dot_general(a.astype(BF16), b.astype(BF16), (((0,), (0,)), ((), ())), preferred_element_type=F32)


def _rstd(x, n=None):
    n = x.shape[-1] if n is None else n
    return lax.rsqrt(jnp.sum(x * x, axis=-1, keepdims=True) * (1.0 / n) + EPS)


def _norm_bwd(x, rs, g, dy, n=None):
    n = x.shape[-1] if n is None else n
    xhat = x * rs
    dxh = dy * g
    dx = rs * (dxh - xhat * (jnp.sum(dxh * xhat, axis=-1, keepdims=True) * (1.0 / n)))
    return dx, dy * xhat


def _acc_row(ref, r, val):
    ref[r:r + 1, :] += jnp.sum(val, axis=0, keepdims=True)


def _zero_first(i, *refs):
    @pl.when(i == 0)
    def _():
        for r in refs:
            r[...] = jnp.zeros_like(r)


def _shift_down(x, j, halo):
    if j == 0:
        return x
    xs = pltpu.roll(x, j, 0)
    hs = pltpu.roll(halo, j, 0)
    row = lax.broadcasted_iota(jnp.int32, hs.shape, 0)
    top = jnp.where(row < j, hs, xs[:SUBLANES])
    return jnp.concatenate([top, xs[SUBLANES:]], axis=0)


def _shift_up(x, j, halo):
    if j == 0:
        return x
    t = x.shape[0]
    xs = pltpu.roll(x, t - j, 0)
    hs = pltpu.roll(halo, SUBLANES - j, 0)
    row = lax.broadcasted_iota(jnp.int32, hs.shape, 0)
    bot = jnp.where(row >= SUBLANES - j, hs, xs[t - SUBLANES:])
    return jnp.concatenate([xs[:t - SUBLANES], bot], axis=0)


def _shift(x, j, halo, down):
    return _shift_down(x, j, halo) if down else _shift_up(x, j, halo)


def _scan(a, b, down):
    t = a.shape[0]
    row = lax.broadcasted_iota(jnp.int32, a.shape, 0)
    d = 1
    while d < t:
        if down:
            keep = row >= d
            a_s = jnp.where(keep, pltpu.roll(a, d, 0), 1.0)
            b_s = jnp.where(keep, pltpu.roll(b, d, 0), 0.0)
        else:
            keep = row < t - d
            a_s = jnp.where(keep, pltpu.roll(a, t - d, 0), 1.0)
            b_s = jnp.where(keep, pltpu.roll(b, t - d, 0), 0.0)
        b = a * b_s + b
        a = a * a_s
        d *= 2
    return a, b


def _sigmoid(x):
    return 1.0 / (1.0 + jnp.exp(-x))


GELU_K = 0.7978845608028654
GELU_C = 0.044715


def _gelu(x):
    return 0.5 * x * (1.0 + jnp.tanh(GELU_K * (x + GELU_C * x * x * x)))


def _gelu_grad(x):
    t = jnp.tanh(GELU_K * (x + GELU_C * x * x * x))
    return 0.5 * (1.0 + t) + 0.5 * x * (1.0 - t * t) * GELU_K * (1.0 + 3.0 * GELU_C * x * x)


def _rope_partner(x):
    lane = lax.broadcasted_iota(jnp.int32, x.shape, 1)
    half = QK_ROPE // 2
    sw = jnp.where(lane < QK_NOPE + half, pltpu.roll(x, LANES - half, 1), pltpu.roll(x, half, 1))
    return jnp.where((lane >= QK_NOPE) & (lane < QK_HEAD), sw, 0.0)


def _rope(x, cos_t, sin_t):
    return x * cos_t + _rope_partner(x) * sin_t


def _rope_t(dy, cos_t, sin_t):
    return dy * cos_t + _rope_partner(dy * sin_t)


def _rowwise(body, name, s, tm, rows=(), halos=(), fulls=(), outs=(), accs=()):
    n = s // tm
    hb = tm // SUBLANES
    last8 = s // SUBLANES - 1
    in_specs, args = [], []
    for a in rows:
        in_specs.append(pl.BlockSpec((tm, a.shape[1]), lambda i: (i, 0)))
        args.append(a)
    for a in halos:
        in_specs.append(pl.BlockSpec((SUBLANES, a.shape[1]), lambda i: (jnp.maximum(i * hb - 1, 0), 0)))
        in_specs.append(pl.BlockSpec((SUBLANES, a.shape[1]), lambda i: (jnp.minimum((i + 1) * hb, last8), 0)))
        args += [a, a]
    for a in fulls:
        in_specs.append(pl.BlockSpec(a.shape, lambda i, nd=a.ndim: (0,) * nd))
        args.append(a)
    out_shape, out_specs = [], []
    for c, dt in outs:
        out_shape.append(jax.ShapeDtypeStruct((s, c), dt))
        out_specs.append(pl.BlockSpec((tm, c), lambda i: (i, 0)))
    for shp, dt in accs:
        out_shape.append(jax.ShapeDtypeStruct(shp, dt))
        out_specs.append(pl.BlockSpec(shp, lambda i, nd=len(shp): (0,) * nd))

    def kern(*refs):
        body(pl.program_id(0), n, *refs)

    return pl.pallas_call(kern, grid=(n,), in_specs=in_specs, out_specs=out_specs, out_shape=out_shape, name=name,
                          compiler_params=_cparams(1))(*args)


def _matmul_tn(a, b, name, col_shards=False):
    t, m = a.shape
    n = b.shape[1]
    bm = 256 if m % 256 == 0 else m
    bn = n // N_CHIPS if col_shards else (n if n <= 2048 else 1408)
    bt = min(512, t)
    nt = t // bt

    def kern(a_ref, b_ref, o_ref):
        @pl.when(pl.program_id(2) == 0)
        def _():
            o_ref[...] = jnp.zeros_like(o_ref)
        o_ref[...] += _bdot_tn(a_ref[...], b_ref[...])

    if col_shards:
        out_spec = pl.BlockSpec((None, bm, bn), lambda i, j, k: (j, i, 0))
        out_shape = jax.ShapeDtypeStruct((N_CHIPS, m, bn), F32)
    else:
        out_spec = pl.BlockSpec((bm, bn), lambda i, j, k: (i, j))
        out_shape = jax.ShapeDtypeStruct((m, n), F32)
    return pl.pallas_call(
        kern, grid=(m // bm, n // bn, nt),
        in_specs=[pl.BlockSpec((bt, bm), lambda i, j, k: (k, i)), pl.BlockSpec((bt, bn), lambda i, j, k: (k, j))],
        out_specs=out_spec, out_shape=out_shape, name=name, compiler_params=_cparams(3))(a, b)


def _in_proj(x, g, w_in_p, tm):
    def body(i, n, x_ref, g_ref, w_ref, xr, yg, cq, ckv, krp, hb):
        xv = x_ref[...]
        h = (xv * _rstd(xv) * g_ref[...]).astype(BF16)
        hb[...] = h
        p = jnp.dot(h, w_ref[...], preferred_element_type=F32)
        xr[...] = p[:, :LRU_W]
        yg[...] = p[:, LRU_W:2 * LRU_W]
        cq[...] = p[:, 2 * LRU_W:2 * LRU_W + Q_LORA]
        ckv[...] = p[:, 2 * LRU_W + Q_LORA:OFF_KR]
        krp[...] = p[:, OFF_KR:IN_PAD]

    return _rowwise(body, "in_proj", x.shape[0], tm, rows=[x], fulls=[g, w_in_p],
                    outs=[(LRU_W, F32), (LRU_W, F32), (Q_LORA, F32), (KV_LORA, F32), (LANES, F32), (D_MODEL, BF16)])


def _lru_gates(x, halo, cw_ref, pv_ref, wa_ref, wi_ref, rev):
    down = not rev
    xc = pv_ref[0:1, :] + jnp.zeros_like(x)
    for j in range(CONV_W):
        k = j if rev else CONV_W - 1 - j
        xc = xc + cw_ref[k:k + 1, :] * _shift(x, j, halo, down)
    r = _sigmoid(_bdot(xc, wa_ref[...]) + pv_ref[1:2, :])
    ig = _sigmoid(_bdot(xc, wi_ref[...]) + pv_ref[2:3, :])
    lam = pv_ref[3:4, :]
    sp = jnp.maximum(-lam, 0.0) + jnp.log(1.0 + jnp.exp(-jnp.abs(lam)))
    log_a = (-LRU_C) * r * sp
    a = jnp.exp(log_a)
    z = 2.0 * log_a
    series = -(z * (1.0 + z * (0.5 + z * (1.0 / 6.0 + z * (1.0 / 24.0)))))
    om = jnp.where(z > -0.02, series, 1.0 - jnp.exp(z))
    mult = jnp.sqrt(om)
    return xc, r, ig, sp, a, mult


def _lru_scan_fwd(xr, cw, pv, wa, wi, rev, t):
    s = xr.shape[0]
    n = s // t
    hb = t // SUBLANES
    last8 = s // SUBLANES - 1
    down = not rev

    def kern(x_ref, halo_ref, cw_ref, pv_ref, wa_ref, wi_ref, h_ref, carry_ref):
        i = pl.program_id(0)
        _zero_first(i, carry_ref)
        halo = jnp.where(i == 0, 0.0, halo_ref[...])
        xc, r, ig, sp, a, mult = _lru_gates(x_ref[...], halo, cw_ref, pv_ref, wa_ref, wi_ref, rev)
        aa, bb = _scan(a, mult * ig * xc, down)
        h_ref[...] = aa * carry_ref[...] + bb
        carry_ref[...] = h_ref[pl.ds(t - 1 if down else 0, 1), :]

    if rev:
        blk = lambda i: (n - 1 - i, 0)
        hal = lambda i: (jnp.minimum((n - i) * hb, last8), 0)
    else:
        blk = lambda i: (i, 0)
        hal = lambda i: (jnp.maximum(i * hb - 1, 0), 0)
    full = lambda a: pl.BlockSpec(a.shape, lambda i: (0, 0))
    return pl.pallas_call(
        kern, grid=(n,),
        in_specs=[pl.BlockSpec((t, LRU_W), blk), pl.BlockSpec((SUBLANES, LRU_W), hal), full(cw), full(pv), full(wa), full(wi)],
        out_specs=pl.BlockSpec((t, LRU_W), blk), out_shape=jax.ShapeDtypeStruct((s, LRU_W), F32),
        scratch_shapes=[pltpu.VMEM((1, LRU_W), F32)], name="lru_scan_rev" if rev else "lru_scan_fwd",
        compiler_params=_cparams(1))(xr, xr, cw, pv, wa, wi)


def _lru_scan_bwd(xr, h, dh, cw, pv, wa, wi, rev, t):
    s = xr.shape[0]
    n = s // t
    hb = t // SUBLANES
    last8 = s // SUBLANES - 1
    down = not rev

    def kern(x_ref, xh_ref, h_ref, hh_ref, dh_ref, cw_ref, pv_ref, wa_ref, wi_ref,
             dx_ref, gwa_ref, gwi_ref, gv_ref, p_ref, dxc_halo_ref, tmp_ref):
        i = pl.program_id(0)
        _zero_first(i, gwa_ref, gwi_ref, gv_ref, p_ref, dxc_halo_ref)
        at_start = i == n - 1
        x = x_ref[...]
        xhalo = jnp.where(at_start, 0.0, xh_ref[...])
        hhalo = jnp.where(at_start, 0.0, hh_ref[...])
        xc, r, ig, sp, a, mult = _lru_gates(x, xhalo, cw_ref, pv_ref, wa_ref, wi_ref, rev)
        h_prev = _shift(h_ref[...], 1, hhalo, down)
        row = lax.broadcasted_iota(jnp.int32, x.shape, 0)
        edge = t - 1 if down else 0
        dh_mod = dh_ref[...] + jnp.where(row == edge, p_ref[...], 0.0)
        a_next = _shift(a, 1, jnp.zeros((SUBLANES, LRU_W), F32), not down)
        _, g = _scan(a_next, dh_mod, not down)
        tmp_ref[...] = a * g
        p_ref[...] = tmp_ref[pl.ds(0 if down else t - 1, 1), :]
        da = g * h_prev
        d_ig = g * mult * xc
        d_xc = g * mult * ig
        d_om = g * ig * xc * (0.5 / jnp.maximum(mult, 1e-30))
        d_log_a = da * a - 2.0 * d_om * a * a
        d_r = d_log_a * ((-LRU_C) * sp)
        d_sp = jnp.sum(d_log_a * ((-LRU_C) * r), axis=0, keepdims=True)
        lam = pv_ref[3:4, :]
        gv_ref[7:8, :] += d_sp * (-_sigmoid(-lam))
        d_ga = d_r * r * (1.0 - r)
        d_gi = d_ig * ig * (1.0 - ig)
        _acc_row(gv_ref, 5, d_ga)
        _acc_row(gv_ref, 6, d_gi)
        d_xc = d_xc + _bdot_nt(d_ga, wa_ref[...]) + _bdot_nt(d_gi, wi_ref[...])
        gwa_ref[...] += _bdot_tn(xc, d_ga)
        gwi_ref[...] += _bdot_tn(xc, d_gi)
        _acc_row(gv_ref, 4, d_xc)
        dx = jnp.zeros_like(x)
        dxc_halo = dxc_halo_ref[...]
        for j in range(CONV_W):
            k = j if rev else CONV_W - 1 - j
            _acc_row(gv_ref, k, d_xc * _shift(x, j, xhalo, down))
            dx = dx + cw_ref[k:k + 1, :] * _shift(d_xc, j, dxc_halo, not down)
        dx_ref[...] = dx
        dxc_halo_ref[...] = d_xc[:SUBLANES] if down else d_xc[t - SUBLANES:]

    if rev:
        blk = lambda i: (i, 0)
        hal = lambda i: (jnp.minimum((i + 1) * hb, last8), 0)
    else:
        blk = lambda i: (n - 1 - i, 0)
        hal = lambda i: (jnp.maximum((n - 1 - i) * hb - 1, 0), 0)
    full = lambda a: pl.BlockSpec(a.shape, lambda i: (0, 0))
    bs = pl.BlockSpec((t, LRU_W), blk)
    hs = pl.BlockSpec((SUBLANES, LRU_W), hal)
    return pl.pallas_call(
        kern, grid=(n,),
        in_specs=[bs, hs, bs, hs, bs, full(cw), full(pv), full(wa), full(wi)],
        out_specs=[bs, pl.BlockSpec((LRU_W, LRU_W), lambda i: (0, 0)), pl.BlockSpec((LRU_W, LRU_W), lambda i: (0, 0)),
                   pl.BlockSpec((SUBLANES, LRU_W), lambda i: (0, 0))],
        out_shape=[jax.ShapeDtypeStruct((s, LRU_W), F32), jax.ShapeDtypeStruct((LRU_W, LRU_W), F32),
                   jax.ShapeDtypeStruct((LRU_W, LRU_W), F32), jax.ShapeDtypeStruct((SUBLANES, LRU_W), F32)],
        scratch_shapes=[pltpu.VMEM((1, LRU_W), F32), pltpu.VMEM((SUBLANES, LRU_W), F32), pltpu.VMEM((t, LRU_W), F32)],
        name="lru_bwd_rev" if rev else "lru_bwd_fwd", compiler_params=_cparams(1))(xr, xr, h, h, dh, cw, pv, wa, wi)


def _mla_qkv(cq, ckv, krp, cos_t, sin_t, g_qa, g_kva, g_qn, g_kn, w_uq_p, w_uk_p, w_uv, tm):
    scale = QK_HEAD ** -0.5

    def body(i, n, cq_ref, ckv_ref, kr_ref, c_ref, s_ref, gqa, gkva, gqn, gkn, wq, wk, wv, q_out, k_out, v_out):
        cosv, sinv = c_ref[...], s_ref[...]
        cqv = cq_ref[...]
        qr = _bdot(cqv * _rstd(cqv) * gqa[...], wq[...])
        ckvv = ckv_ref[...]
        c_kv = (ckvv * _rstd(ckvv) * gkva[...]).astype(BF16)
        kn = jnp.dot(c_kv, wk[...], preferred_element_type=F32)
        v_out[...] = jnp.dot(c_kv, wv[...], preferred_element_type=F32).astype(BF16)
        kr = kr_ref[...]
        for h in range(HEADS):
            sl = slice(h * LANES, (h + 1) * LANES)
            qh = qr[:, sl]
            qh = _rope(qh * _rstd(qh, QK_HEAD) * gqn[...], cosv, sinv) * scale
            q_out[:, sl] = qh.astype(BF16)
            kh = kn[:, sl] + kr
            kh = _rope(kh * _rstd(kh, QK_HEAD) * gkn[...], cosv, sinv)
            k_out[:, sl] = kh.astype(BF16)

    return _rowwise(body, "mla_qkv", cq.shape[0], tm, rows=[cq, ckv, krp, cos_t, sin_t],
                    fulls=[g_qa, g_kva, g_qn, g_kn, w_uq_p, w_uk_p, w_uv],
                    outs=[(HEADS * LANES, BF16), (HEADS * LANES, BF16), (MLA_W, BF16)])


def _attn_fwd(q, k, v, tq, tk):
    s = q.shape[0]
    nq, nk = s // tq, s // tk

    def kern(q_ref, k_ref, v_ref, o_ref, lse_ref):
        qs = (q_ref[:, :LANES], q_ref[:, LANES:])

        def step(j, carry):
            off = pl.multiple_of(j * tk, tk)
            kc = k_ref[pl.ds(off, tk), :]
            vc = v_ref[pl.ds(off, tk), :]
            out = []
            for h in range(2):
                m, l, acc = carry[3 * h:3 * h + 3]
                sc = lax.dot_general(qs[h], kc[:, h * LANES:(h + 1) * LANES], (((1,), (1,)), ((), ())),
                                     preferred_element_type=F32)
                mn = jnp.maximum(m, jnp.max(sc, axis=-1, keepdims=True))
                al = jnp.exp(m - mn)
                p = jnp.exp(sc - mn)
                l = al * l + jnp.sum(p, axis=-1, keepdims=True)
                acc = al * acc + jnp.dot(p.astype(BF16), vc, preferred_element_type=F32)
                out += [mn, l, acc]
            return tuple(out)

        init = (jnp.full((tq, 1), -1e30, F32), jnp.zeros((tq, 1), F32), jnp.zeros((tq, LANES), F32)) * 2
        m0, l0, a0, m1, l1, a1 = lax.fori_loop(0, nk, step, init)
        lane = lax.broadcasted_iota(jnp.int32, (tq, LANES), 1)
        o_ref[...] = jnp.where(lane < V_DIM, a0 / l0, a1 / l1)
        lse_ref[0, :, 0:1] = m0 + jnp.log(l0)
        lse_ref[0, :, 1:2] = m1 + jnp.log(l1)

    return pl.pallas_call(
        kern, grid=(HEADS // 2, nq),
        in_specs=[pl.BlockSpec((tq, 2 * LANES), lambda p, i: (i, p)), pl.BlockSpec((s, 2 * LANES), lambda p, i: (0, p)),
                  pl.BlockSpec((s, LANES), lambda p, i: (0, p))],
        out_specs=[pl.BlockSpec((tq, LANES), lambda p, i: (i, p)), pl.BlockSpec((1, tq, 2), lambda p, i: (p, i, 0))],
        out_shape=[jax.ShapeDtypeStruct((s, MLA_W), F32), jax.ShapeDtypeStruct((HEADS // 2, s, 2), F32)],
        name="attn_fwd", compiler_params=_cparams(2))(q, k, v)


def _attn_bwd(q, k, v, do, lse, delta, tq, tk):
    s = q.shape[0]
    nq, nk = s // tq, s // tk

    def kern(k_ref, v_ref, q_ref, do_ref, lse_ref, dl_ref, dk_ref, dv_ref, dq_ref):
        _zero_first(pl.program_id(1), dq_ref)
        dk_ref[...] = jnp.zeros_like(dk_ref)
        dv_ref[...] = jnp.zeros_like(dv_ref)
        lane_k = lax.broadcasted_iota(jnp.int32, (tk, LANES), 1)
        vp = v_ref[...]
        zero = jnp.zeros_like(vp)
        vs = (jnp.where(lane_k < V_DIM, vp, zero), jnp.where(lane_k >= V_DIM, vp, zero))
        ks = (k_ref[:, :LANES], k_ref[:, LANES:])

        def step(i, carry):
            off = pl.multiple_of(i * tq, tq)
            qc = q_ref[pl.ds(off, tq), :]
            doc = do_ref[pl.ds(off, tq), :]
            lane_q = lax.broadcasted_iota(jnp.int32, (tq, LANES), 1)
            zq = jnp.zeros_like(doc)
            dos = (jnp.where(lane_q < V_DIM, doc, zq), jnp.where(lane_q >= V_DIM, doc, zq))
            for h in range(2):
                sl = slice(h * LANES, (h + 1) * LANES)
                qh = qc[:, sl]
                lse_h = lse_ref[0, h:h + 1, pl.ds(off, tq)]
                dl_h = dl_ref[0, h:h + 1, pl.ds(off, tq)]
                st = lax.dot_general(ks[h], qh, (((1,), (1,)), ((), ())), preferred_element_type=F32)
                pt = jnp.exp(st - lse_h)
                dpt = lax.dot_general(vs[h], doc, (((1,), (1,)), ((), ())), preferred_element_type=F32)
                dst = (pt * (dpt - dl_h)).astype(BF16)
                dv_ref[...] += jnp.dot(pt.astype(BF16), dos[h], preferred_element_type=F32)
                dk_ref[:, sl] += jnp.dot(dst, qh, preferred_element_type=F32)
                dq_ref[pl.ds(off, tq), sl] += lax.dot_general(dst, ks[h], (((0,), (0,)), ((), ())),
                                                               preferred_element_type=F32)
            return carry

        lax.fori_loop(0, nq, step, 0)

    return pl.pallas_call(
        kern, grid=(HEADS // 2, nk),
        in_specs=[pl.BlockSpec((tk, 2 * LANES), lambda p, j: (j, p)), pl.BlockSpec((tk, LANES), lambda p, j: (j, p)),
                  pl.BlockSpec((s, 2 * LANES), lambda p, j: (0, p)), pl.BlockSpec((s, LANES), lambda p, j: (0, p)),
                  pl.BlockSpec((1, 2, s), lambda p, j: (p, 0, 0)), pl.BlockSpec((1, 2, s), lambda p, j: (p, 0, 0))],
        out_specs=[pl.BlockSpec((tk, 2 * LANES), lambda p, j: (j, p)), pl.BlockSpec((tk, LANES), lambda p, j: (j, p)),
                   pl.BlockSpec((s, 2 * LANES), lambda p, j: (0, p))],
        out_shape=[jax.ShapeDtypeStruct((s, HEADS * LANES), F32), jax.ShapeDtypeStruct((s, MLA_W), F32),
                   jax.ShapeDtypeStruct((s, HEADS * LANES), F32)],
        name="attn_bwd", compiler_params=_cparams(2))(k, v, q, do, lse, delta)


def _mix_out(hf, hb, yg, o, x, g_lru, g_mla, w_out, tm):
    def body(i, n, hf_ref, hb_ref, yg_ref, o_ref, x_ref, gl, gm, w_ref, x1_ref, mix_ref):
        lo = (hf_ref[...] + hb_ref[...]) * _gelu(yg_ref[...])
        ov = o_ref[...]
        mix_ref[:, :LRU_W] = (lo * _rstd(lo) * gl[...]).astype(BF16)
        mix_ref[:, LRU_W:] = (ov * _rstd(ov) * gm[...]).astype(BF16)
        x1_ref[...] = x_ref[...] + jnp.dot(mix_ref[...], w_ref[...], preferred_element_type=F32)

    return _rowwise(body, "mix_out", x.shape[0], tm, rows=[hf, hb, yg, o, x], fulls=[g_lru, g_mla, w_out],
                    outs=[(D_MODEL, F32), (2 * LRU_W, BF16)])


def _mem_kv(mem, g_mem, w_kv, g_k):
    m = mem.shape[0]

    def body(i, n, mem_ref, g_ref, w_ref, gk_ref, km_ref, vm_ref):
        mv = mem_ref[...]
        kv = _bdot(mv * _rstd(mv) * g_ref[...], w_ref[...])
        vm_ref[...] = kv[:, MEM_W:].astype(BF16)
        for h in range(MEM_HEADS):
            sl = slice(h * MEM_HD, (h + 1) * MEM_HD)
            kh = kv[:, sl]
            km_ref[:, sl] = (kh * _rstd(kh) * gk_ref[...]).astype(BF16)

    return _rowwise(body, "mem_kv", m, m, rows=[mem], fulls=[g_mem, w_kv, g_k], outs=[(MEM_W, BF16), (MEM_W, BF16)])


def _mem_attn_core(x1v, g_ref, wq_ref, gq_ref, km_ref, vm_ref):
    scale = MEM_HD ** -0.5
    hm = (x1v * _rstd(x1v) * g_ref[...]).astype(BF16)
    qr = jnp.dot(hm, wq_ref[...], preferred_element_type=F32)
    heads = []
    for h in range(MEM_HEADS):
        sl = slice(h * MEM_HD, (h + 1) * MEM_HD)
        qh = qr[:, sl]
        rs = _rstd(qh)
        qn = (qh * rs * gq_ref[...]).astype(BF16)
        sc = lax.dot_general(qn, km_ref[:, sl], (((1,), (1,)), ((), ())), preferred_element_type=F32) * scale
        e = jnp.exp(sc - jnp.max(sc, axis=-1, keepdims=True))
        p = e / jnp.sum(e, axis=-1, keepdims=True)
        oh = jnp.dot(p.astype(BF16), vm_ref[:, sl], preferred_element_type=F32)
        heads.append((qh, rs, qn, p, oh))
    return hm, heads


def _mem_attn(x1, g, w_q, g_q, km, vm, w_o, tm):
    cs = D_MODEL // N_CHIPS

    def body(i, n, x1_ref, g_ref, wq_ref, gq_ref, km_ref, vm_ref, wo_ref, x2_ref, ob_ref):
        x1v = x1_ref[...]
        _, heads = _mem_attn_core(x1v, g_ref, wq_ref, gq_ref, km_ref, vm_ref)
        for h in range(MEM_HEADS):
            ob_ref[:, h * MEM_HD:(h + 1) * MEM_HD] = heads[h][4].astype(BF16)
        for k in range(N_CHIPS):
            sl = slice(k * cs, (k + 1) * cs)
            x2_ref[:, sl] = x1v[:, sl] + jnp.dot(ob_ref[...], wo_ref[k], preferred_element_type=F32)

    return _rowwise(body, "mem_attn", x1.shape[0], tm, rows=[x1], fulls=[g, w_q, g_q, km, vm, w_o],
                    outs=[(D_MODEL, F32), (MEM_W, BF16)])


def _ffn_up(x2, g, w_up, tm):
    cs = 2 * D_FF // N_CHIPS

    def body(i, n, x_ref, g_ref, w_ref, gu_ref, hb_ref):
        xv = x_ref[...]
        hb_ref[...] = (xv * _rstd(xv) * g_ref[...]).astype(BF16)
        for k in range(N_CHIPS):
            gu_ref[:, k * cs:(k + 1) * cs] = jnp.dot(hb_ref[...], w_ref[k], preferred_element_type=F32)

    return _rowwise(body, "ffn_up", x2.shape[0], tm, rows=[x2], fulls=[g, w_up], outs=[(2 * D_FF, F32), (D_MODEL, BF16)])


def _ffn_conv(gu, prev, nxt, cw_ref, i, n):
    prev = jnp.where(i == 0, 0.0, prev)
    nxt = jnp.where(i == n - 1, 0.0, nxt)
    return (cw_ref[3:4, :] + cw_ref[0:1, :] * _shift_down(gu, 1, prev) + cw_ref[1:2, :] * gu
            + cw_ref[2:3, :] * _shift_up(gu, 1, nxt))


def _ffn_down_loss(gu_pre, x2, target, cw, w_down, tm):
    def body(i, n, gu_ref, x_ref, t_ref, pv_ref, nx_ref, cw_ref, w_ref, dy_ref, dyb_ref, act_ref, loss_ref):
        _zero_first(i, loss_ref)
        gu = _ffn_conv(gu_ref[...], pv_ref[...], nx_ref[...], cw_ref, i, n)
        g, u = gu[:, :D_FF], gu[:, D_FF:]
        act_ref[...] = (g * _sigmoid(g) * u).astype(BF16)
        y = x_ref[...] + jnp.dot(act_ref[...], w_ref[...], preferred_element_type=F32)
        e = y - t_ref[...]
        loss_ref[...] += jnp.sum(e * e)
        dy = e * (1.0 / D_MODEL)
        dy_ref[...] = dy
        dyb_ref[...] = dy.astype(BF16)

    return _rowwise(body, "ffn_down_loss", x2.shape[0], tm, rows=[gu_pre, x2, target], halos=[gu_pre], fulls=[cw, w_down],
                    outs=[(D_MODEL, F32), (D_MODEL, BF16), (D_FF, BF16)], accs=[((SUBLANES, LANES), F32)])


def _ffn_bwd_act(dyb, gu_pre, cw, w_down, tm):
    def body(i, n, dy_ref, gu_ref, pv_ref, nx_ref, cw_ref, w_ref, dgu_ref):
        d_act = lax.dot_general(dy_ref[...], w_ref[...], (((1,), (1,)), ((), ())), preferred_element_type=F32)
        gu = _ffn_conv(gu_ref[...], pv_ref[...], nx_ref[...], cw_ref, i, n)
        g, u = gu[:, :D_FF], gu[:, D_FF:]
        sg = _sigmoid(g)
        dgu_ref[:, :D_FF] = d_act * u * sg * (1.0 + g * (1.0 - sg))
        dgu_ref[:, D_FF:] = d_act * g * sg

    return _rowwise(body, "ffn_bwd_act", dyb.shape[0], tm, rows=[dyb, gu_pre], halos=[gu_pre], fulls=[cw, w_down],
                    outs=[(2 * D_FF, F32)])


def _ffn_bwd_conv(dgu, gu_pre, cw, tm):
    def body(i, n, d_ref, g_ref, dp_ref, dn_ref, gp_ref, gn_ref, cw_ref, dpre_ref, gc_ref):
        _zero_first(i, gc_ref)
        first, last = i == 0, i == n - 1
        d = d_ref[...]
        g = g_ref[...]
        dpv, dnx = jnp.where(first, 0.0, dp_ref[...]), jnp.where(last, 0.0, dn_ref[...])
        gpv, gnx = jnp.where(first, 0.0, gp_ref[...]), jnp.where(last, 0.0, gn_ref[...])
        dpre_ref[...] = (cw_ref[0:1, :] * _shift_up(d, 1, dnx) + cw_ref[1:2, :] * d
                         + cw_ref[2:3, :] * _shift_down(d, 1, dpv)).astype(BF16)
        _acc_row(gc_ref, 0, d * _shift_down(g, 1, gpv))
        _acc_row(gc_ref, 1, d * g)
        _acc_row(gc_ref, 2, d * _shift_up(g, 1, gnx))
        _acc_row(gc_ref, 3, d)

    return _rowwise(body, "ffn_bwd_conv", dgu.shape[0], tm, rows=[dgu, gu_pre], halos=[dgu, gu_pre], fulls=[cw],
                    outs=[(2 * D_FF, BF16)], accs=[((SUBLANES, 2 * D_FF), F32)])


def _ffn_bwd_in(dpre, x2, dy, g, w_up, tm):
    cs = 2 * D_FF // N_CHIPS

    def body(i, n, dp_ref, x_ref, dy_ref, g_ref, w_ref, dx_ref, dxb_ref, gg_ref):
        _zero_first(i, gg_ref)
        d_h = jnp.zeros(x_ref.shape, F32)
        for k in range(N_CHIPS):
            d_h = d_h + lax.dot_general(dp_ref[:, k * cs:(k + 1) * cs], w_ref[k], (((1,), (1,)), ((), ())),
                                        preferred_element_type=F32)
        xv = x_ref[...]
        dx, dg = _norm_bwd(xv, _rstd(xv), g_ref[...], d_h)
        _acc_row(gg_ref, 0, dg)
        dx = dx + dy_ref[...]
        dx_ref[...] = dx
        dxb_ref[...] = dx.astype(BF16)

    return _rowwise(body, "ffn_bwd_in", x2.shape[0], tm, rows=[dpre, x2, dy], fulls=[g, w_up],
                    outs=[(D_MODEL, F32), (D_MODEL, BF16)], accs=[((SUBLANES, D_MODEL), F32)])


def _mem_attn_bwd(x1, dx2, dx2b, g, w_q, g_q, km, vm, w_o, tm):
    scale = MEM_HD ** -0.5
    m = km.shape[0]

    def body(i, n, x1_ref, dx2_ref, dx2b_ref, g_ref, wq_ref, gq_ref, km_ref, vm_ref, wo_ref,
             dx1_ref, dx1b_ref, hm_ref, dqr_ref, dkm_ref, dvm_ref, gg_ref, ggq_ref):
        _zero_first(i, dkm_ref, dvm_ref, gg_ref, ggq_ref)
        x1v = x1_ref[...]
        hm, heads = _mem_attn_core(x1v, g_ref, wq_ref, gq_ref, km_ref, vm_ref)
        hm_ref[...] = hm
        cs = D_MODEL // N_CHIPS
        d_o = jnp.zeros((x1v.shape[0], MEM_W), F32)
        for k in range(N_CHIPS):
            d_o = d_o + lax.dot_general(dx2b_ref[:, k * cs:(k + 1) * cs], wo_ref[k], (((1,), (1,)), ((), ())),
                                        preferred_element_type=F32)
        for h in range(MEM_HEADS):
            sl = slice(h * MEM_HD, (h + 1) * MEM_HD)
            qh, rs, qn, p, _ = heads[h]
            d_oh = d_o[:, sl].astype(BF16)
            dp = lax.dot_general(d_oh, vm_ref[:, sl], (((1,), (1,)), ((), ())), preferred_element_type=F32)
            ds = (p * (dp - jnp.sum(dp * p, axis=-1, keepdims=True)) * scale).astype(BF16)
            dqn = jnp.dot(ds, km_ref[:, sl], preferred_element_type=F32)
            dkm_ref[:, sl] += lax.dot_general(ds, qn, (((0,), (0,)), ((), ())), preferred_element_type=F32)
            dvm_ref[:, sl] += lax.dot_general(p.astype(BF16), d_oh, (((0,), (0,)), ((), ())), preferred_element_type=F32)
            dqh, dgq = _norm_bwd(qh, rs, gq_ref[...], dqn)
            _acc_row(ggq_ref, 0, dgq)
            dqr_ref[:, sl] = dqh.astype(BF16)
        d_hm = lax.dot_general(dqr_ref[...], wq_ref[...], (((1,), (1,)), ((), ())), preferred_element_type=F32)
        dx, dg = _norm_bwd(x1v, _rstd(x1v), g_ref[...], d_hm)
        _acc_row(gg_ref, 0, dg)
        dx = dx + dx2_ref[...]
        dx1_ref[...] = dx
        dx1b_ref[...] = dx.astype(BF16)

    return _rowwise(body, "mem_attn_bwd", x1.shape[0], tm, rows=[x1, dx2, dx2b], fulls=[g, w_q, g_q, km, vm, w_o],
                    outs=[(D_MODEL, F32), (D_MODEL, BF16), (D_MODEL, BF16), (MEM_W, BF16)],
                    accs=[((m, MEM_W), F32), ((m, MEM_W), F32), ((SUBLANES, D_MODEL), F32), ((SUBLANES, MEM_HD), F32)])


def _mem_kv_bwd(mem, g_mem, w_kv, g_k, dkm, dvm):
    m = mem.shape[0]

    def body(i, n, mem_ref, dkm_ref, dvm_ref, g_ref, w_ref, gk_ref, gw_ref, gg_ref, ggk_ref, dkv_ref):
        gg_ref[...] = jnp.zeros_like(gg_ref)
        ggk_ref[...] = jnp.zeros_like(ggk_ref)
        mv = mem_ref[...]
        rs_m = _rstd(mv)
        mem_n = (mv * rs_m * g_ref[...]).astype(BF16)
        kv = jnp.dot(mem_n, w_ref[...], preferred_element_type=F32)
        for h in range(MEM_HEADS):
            sl = slice(h * MEM_HD, (h + 1) * MEM_HD)
            kh = kv[:, sl]
            dkh, dgk = _norm_bwd(kh, _rstd(kh), gk_ref[...], dkm_ref[:, sl])
            _acc_row(ggk_ref, 0, dgk)
            dkv_ref[:, sl] = dkh.astype(BF16)
        dkv_ref[:, MEM_W:] = dvm_ref[...].astype(BF16)
        gw_ref[...] = lax.dot_general(mem_n, dkv_ref[...], (((0,), (0,)), ((), ())), preferred_element_type=F32)
        d_mn = lax.dot_general(dkv_ref[...], w_ref[...], (((1,), (1,)), ((), ())), preferred_element_type=F32)
        _acc_row(gg_ref, 0, d_mn * (mv * rs_m))

    return _rowwise(body, "mem_kv_bwd", m, m, rows=[mem, dkm, dvm], fulls=[g_mem, w_kv, g_k],
                    accs=[((D_MODEL, 2 * MEM_W), F32), ((SUBLANES, D_MODEL), F32), ((SUBLANES, MEM_HD), F32),
                          ((m, 2 * MEM_W), BF16)])


def _mix_out_bwd(dx1b, hf, hb, yg, o, g_lru, g_mla, w_out, tm):
    def body(i, n, dx_ref, hf_ref, hb_ref, yg_ref, o_ref, gl, gm, w_ref, dh_ref, dyg_ref, dob_ref, dl_ref, ggl_ref, ggm_ref):
        _zero_first(i, ggl_ref, ggm_ref)
        dmix = lax.dot_general(dx_ref[...], w_ref[...], (((1,), (1,)), ((), ())), preferred_element_type=F32)
        hs = hf_ref[...] + hb_ref[...]
        ygv = yg_ref[...]
        ge = _gelu(ygv)
        lo = hs * ge
        d_lo, dgl = _norm_bwd(lo, _rstd(lo), gl[...], dmix[:, :LRU_W])
        _acc_row(ggl_ref, 0, dgl)
        dh_ref[...] = d_lo * ge
        dyg_ref[...] = d_lo * hs * _gelu_grad(ygv)
        ov = o_ref[...]
        d_o, dgm = _norm_bwd(ov, _rstd(ov), gm[...], dmix[:, LRU_W:])
        _acc_row(ggm_ref, 0, dgm)
        dob_ref[...] = d_o.astype(BF16)
        prod = d_o * ov
        lane_w = lax.broadcasted_iota(jnp.int32, prod.shape, 1)
        lane = lax.broadcasted_iota(jnp.int32, (prod.shape[0], LANES), 1)
        dl = jnp.zeros((prod.shape[0], LANES), F32)
        for h in range(HEADS):
            in_head = (lane_w >= h * V_DIM) & (lane_w < (h + 1) * V_DIM)
            dl = dl + jnp.where(lane == h, jnp.sum(jnp.where(in_head, prod, 0.0), axis=-1, keepdims=True), 0.0)
        dl_ref[...] = dl

    return _rowwise(body, "mix_out_bwd", dx1b.shape[0], tm, rows=[dx1b, hf, hb, yg, o], fulls=[g_lru, g_mla, w_out],
                    outs=[(LRU_W, F32), (LRU_W, F32), (MLA_W, BF16), (LANES, F32)],
                    accs=[((SUBLANES, LRU_W), F32), ((SUBLANES, MLA_W), F32)])


def _mla_qkv_bwd(cq, ckv, krp, cos_t, sin_t, dq, dk, dv, g_qa, g_kva, g_qn, g_kn, w_uq_p, w_uk_p, w_uv, tm):
    scale = QK_HEAD ** -0.5

    def body(i, n, cq_ref, ckv_ref, kr_ref, c_ref, s_ref, dq_ref, dk_ref, dv_ref, gqa, gkva, gqn, gkn, wq, wk, wv,
             dcq_ref, dckv_ref, dkr_ref, cqb_ref, dqr_ref, ckvb_ref, dkn_ref, dvb_ref, ggqa, ggkva, ggqn, ggkn):
        _zero_first(i, ggqa, ggkva, ggqn, ggkn)
        cosv, sinv = c_ref[...], s_ref[...]
        cqv = cq_ref[...]
        rs_q = _rstd(cqv)
        cqb_ref[...] = (cqv * rs_q * gqa[...]).astype(BF16)
        qr = jnp.dot(cqb_ref[...], wq[...], preferred_element_type=F32)
        ckvv = ckv_ref[...]
        rs_kv = _rstd(ckvv)
        ckvb_ref[...] = (ckvv * rs_kv * gkva[...]).astype(BF16)
        kn = jnp.dot(ckvb_ref[...], wk[...], preferred_element_type=F32)
        kr = kr_ref[...]
        dkr = jnp.zeros_like(kr)
        for h in range(HEADS):
            sl = slice(h * LANES, (h + 1) * LANES)
            qh = qr[:, sl]
            d_qn = _rope_t(dq_ref[:, sl] * scale, cosv, sinv)
            dqh, dgq = _norm_bwd(qh, _rstd(qh, QK_HEAD), gqn[...], d_qn, QK_HEAD)
            _acc_row(ggqn, 0, dgq)
            dqr_ref[:, sl] = dqh.astype(BF16)
            kh = kn[:, sl] + kr
            d_kn = _rope_t(dk_ref[:, sl], cosv, sinv)
            dkh, dgk = _norm_bwd(kh, _rstd(kh, QK_HEAD), gkn[...], d_kn, QK_HEAD)
            _acc_row(ggkn, 0, dgk)
            dkn_ref[:, sl] = dkh.astype(BF16)
            dkr = dkr + dkh
        dkr_ref[...] = dkr
        dvb_ref[...] = dv_ref[...].astype(BF16)
        d_cq = lax.dot_general(dqr_ref[...], wq[...], (((1,), (1,)), ((), ())), preferred_element_type=F32)
        dcq, dg = _norm_bwd(cqv, rs_q, gqa[...], d_cq)
        _acc_row(ggqa, 0, dg)
        dcq_ref[...] = dcq
        d_ckv = (lax.dot_general(dkn_ref[...], wk[...], (((1,), (1,)), ((), ())), preferred_element_type=F32)
                 + lax.dot_general(dvb_ref[...], wv[...], (((1,), (1,)), ((), ())), preferred_element_type=F32))
        dckv, dg = _norm_bwd(ckvv, rs_kv, gkva[...], d_ckv)
        _acc_row(ggkva, 0, dg)
        dckv_ref[...] = dckv

    return _rowwise(body, "mla_qkv_bwd", cq.shape[0], tm, rows=[cq, ckv, krp, cos_t, sin_t, dq, dk, dv],
                    fulls=[g_qa, g_kva, g_qn, g_kn, w_uq_p, w_uk_p, w_uv],
                    outs=[(Q_LORA, F32), (KV_LORA, F32), (LANES, F32), (Q_LORA, BF16), (HEADS * LANES, BF16),
                          (KV_LORA, BF16), (HEADS * LANES, BF16), (MLA_W, BF16)],
                    accs=[((SUBLANES, Q_LORA), F32), ((SUBLANES, KV_LORA), F32), ((SUBLANES, LANES), F32),
                          ((SUBLANES, LANES), F32)])


def _in_proj_bwd(x, dx1, dxr_f, dxr_b, dyg, dcq, dckv, dkrp, g, w_in_p, tm):
    def body(i, n, x_ref, dx1_ref, df_ref, db_ref, dyg_ref, dcq_ref, dckv_ref, dkr_ref, g_ref, w_ref, gx_ref, dp_ref, gg_ref):
        _zero_first(i, gg_ref)
        dp_ref[:, :LRU_W] = (df_ref[...] + db_ref[...]).astype(BF16)
        dp_ref[:, LRU_W:2 * LRU_W] = dyg_ref[...].astype(BF16)
        dp_ref[:, 2 * LRU_W:2 * LRU_W + Q_LORA] = dcq_ref[...].astype(BF16)
        dp_ref[:, 2 * LRU_W + Q_LORA:OFF_KR] = dckv_ref[...].astype(BF16)
        dp_ref[:, OFF_KR:] = dkr_ref[...].astype(BF16)
        d_h = lax.dot_general(dp_ref[...], w_ref[...], (((1,), (1,)), ((), ())), preferred_element_type=F32)
        xv = x_ref[...]
        dx, dg = _norm_bwd(xv, _rstd(xv), g_ref[...], d_h)
        _acc_row(gg_ref, 0, dg)
        gx_ref[...] = dx + dx1_ref[...]

    return _rowwise(body, "in_proj_bwd", x.shape[0], tm, rows=[x, dx1, dxr_f, dxr_b, dyg, dcq, dckv, dkrp],
                    fulls=[g, w_in_p], outs=[(D_MODEL, F32), (IN_PAD, BF16)], accs=[((SUBLANES, D_MODEL), F32)])


ANY = pl.BlockSpec(memory_space=pl.ANY)


def _chip_peers(x, y):
    return ((1 - x, y), (x, 1 - y), (1 - x, 1 - y))


def _exchange_call(kern, name, ins, out_shapes, n_sems, aliases=None):
    return pl.pallas_call(
        kern, in_specs=[ANY] * len(ins), out_specs=[ANY] * len(out_shapes), out_shape=out_shapes,
        scratch_shapes=[pltpu.SemaphoreType.DMA((n,)) for n in n_sems], input_output_aliases=aliases or {},
        name=name)(*ins)


def _start_then_wait(copies):
    for cp in copies:
        cp.start()
    for cp in copies:
        cp.wait()


def _gather_chips(arrs):
    n = len(arrs)

    def kern(*refs):
        ins, outs, (ssem, rsem, lsem) = refs[:n], refs[n:2 * n], refs[2 * n:]
        x, y, c = lax.axis_index("x"), lax.axis_index("y"), lax.axis_index("c")
        me = 2 * x + y
        cps = []
        for i in range(n):
            cps.append(pltpu.make_async_copy(ins[i], outs[i].at[me], lsem.at[i]))
            for j, (px, py) in enumerate(_chip_peers(x, y)):
                cps.append(pltpu.make_async_remote_copy(ins[i], outs[i].at[me], ssem.at[3 * i + j], rsem.at[3 * i + j],
                                                        device_id=(px, py, c), device_id_type=MESH))
        _start_then_wait(cps)

    outs = [jax.ShapeDtypeStruct((N_CHIPS,) + a.shape, a.dtype) for a in arrs]
    return _exchange_call(kern, "gather_weights", arrs, outs, (3 * n, 3 * n, n))


def _swap_halves(gs):
    n = len(gs)

    def kern(*refs):
        ins, outs, (ssem, rsem) = refs[:n], refs[n:2 * n], refs[2 * n:]
        x, y, c = lax.axis_index("x"), lax.axis_index("y"), lax.axis_index("c")
        _start_then_wait([
            pltpu.make_async_remote_copy(ins[i].at[k, 1 - c], outs[i].at[k], ssem.at[N_CHIPS * i + k],
                                         rsem.at[N_CHIPS * i + k], device_id=(x, y, 1 - c), device_id_type=MESH)
            for i in range(n) for k in range(N_CHIPS)])

    outs = [jax.ShapeDtypeStruct((N_CHIPS,) + g.shape[2:], g.dtype) for g in gs]
    return _exchange_call(kern, "grad_swap_halves", gs, outs, (N_CHIPS * n, N_CHIPS * n))


def _scatter_chips(arrs):
    n = len(arrs)

    def kern(*refs):
        ins, outs, (ssem, rsem, lsem) = refs[:n], refs[n:2 * n], refs[2 * n:]
        x, y, c = lax.axis_index("x"), lax.axis_index("y"), lax.axis_index("c")
        me = 2 * x + y
        cps = []
        for i in range(n):
            cps.append(pltpu.make_async_copy(ins[i].at[me], outs[i].at[me], lsem.at[i]))
            for j, (px, py) in enumerate(_chip_peers(x, y)):
                cps.append(pltpu.make_async_remote_copy(ins[i].at[2 * px + py], outs[i].at[me], ssem.at[3 * i + j],
                                                        rsem.at[3 * i + j], device_id=(px, py, c), device_id_type=MESH))
        _start_then_wait(cps)

    outs = [jax.ShapeDtypeStruct(a.shape, a.dtype) for a in arrs]
    return _exchange_call(kern, "grad_scatter_chips", arrs, outs, (3 * n, 3 * n, n))


def _join_halves(arrs):
    n = len(arrs)

    def kern(*refs):
        outs, (ssem, rsem) = refs[n:2 * n], refs[2 * n:]
        x, y, c = lax.axis_index("x"), lax.axis_index("y"), lax.axis_index("c")
        _start_then_wait([
            pltpu.make_async_remote_copy(outs[i].at[c], outs[i].at[c], ssem.at[i], rsem.at[i],
                                         device_id=(x, y, 1 - c), device_id_type=MESH) for i in range(n)])

    outs = [jax.ShapeDtypeStruct(a.shape, a.dtype) for a in arrs]
    return _exchange_call(kern, "grad_join_halves", arrs, outs, (n, n), aliases={i: i for i in range(n)})


def _row_block(rows, row_bytes, limit=1 << 20):
    best = None
    for d in range(16, rows + 1, 16):
        if rows % d == 0 and d * row_bytes <= limit:
            best = d
    return best if best is not None else rows


def _add_halves(g, got, c, out_dtype, name):
    _, _, h, cols = g.shape
    hb = _row_block(h, cols * 4)

    def kern(c_ref, g_ref, b_ref, o_ref):
        o_ref[...] = (g_ref[...] + b_ref[...]).astype(out_dtype)

    blk = (None, hb, cols)
    return pl.pallas_call(
        kern,
        grid_spec=pltpu.PrefetchScalarGridSpec(
            num_scalar_prefetch=1, grid=(N_CHIPS, h // hb),
            in_specs=[pl.BlockSpec((None, None, hb, cols), lambda k, i, c_ref: (k, c_ref[0], i, 0)),
                      pl.BlockSpec(blk, lambda k, i, c_ref: (k, i, 0))],
            out_specs=pl.BlockSpec(blk, lambda k, i, c_ref: (k, i, 0))),
        out_shape=jax.ShapeDtypeStruct(got.shape, out_dtype), name=name, compiler_params=_cparams(2))(c, g, got)


def _sum_chips(b, c, name):
    _, h, cols = b.shape
    hb = _row_block(h, cols * 4)

    def kern(c_ref, b_ref, o_ref):
        f = lambda k: b_ref[k].astype(F32)
        o_ref[...] = ((f(0) + f(1)) + f(2)) + f(3)

    return pl.pallas_call(
        kern,
        grid_spec=pltpu.PrefetchScalarGridSpec(
            num_scalar_prefetch=1, grid=(h // hb,),
            in_specs=[pl.BlockSpec((N_CHIPS, hb, cols), lambda i, c_ref: (0, i, 0))],
            out_specs=pl.BlockSpec((None, hb, cols), lambda i, c_ref: (c_ref[0], i, 0))),
        out_shape=jax.ShapeDtypeStruct((2, h, cols), F32), name=name, compiler_params=_cparams(1))(c, b)


def _adamw(w, g, m, v, name):
    rows, cols = w.shape
    rb = _row_block(rows, cols * 4)
    c1 = 1.0 - ADAM_B1 ** ADAM_STEP
    c2 = 1.0 - ADAM_B2 ** ADAM_STEP

    def kern(w_ref, g_ref, m_ref, v_ref, d_ref, mo_ref, vo_ref):
        gv = g_ref[...]
        mn = ADAM_B1 * m_ref[...] + (1.0 - ADAM_B1) * gv
        vn = ADAM_B2 * v_ref[...] + (1.0 - ADAM_B2) * (gv * gv)
        mo_ref[...] = mn
        vo_ref[...] = vn
        d_ref[...] = (-ADAM_LR) * ((mn / c1) / (jnp.sqrt(vn / c2) + ADAM_EPS) + ADAM_WD * w_ref[...])

    spec = pl.BlockSpec((rb, cols), lambda i: (i, 0))
    return pl.pallas_call(
        kern, grid=(rows // rb,), in_specs=[spec] * 4, out_specs=[spec] * 3,
        out_shape=[jax.ShapeDtypeStruct(w.shape, F32)] * 3, name=name, compiler_params=_cparams(1))(w, g, m, v)


def _pad_rows(flat, rows):
    return jnp.pad(flat, (0, rows * LANES - flat.shape[0])).reshape(rows, LANES)


def _round_up(n, m):
    return (n + m - 1) // m * m


def _shard_shape(shape, axis):
    return tuple(s // N_CHIPS if a == axis else s for a, s in enumerate(shape))


def _to_shards(full, axis):
    shape = full.shape
    t = full.reshape(shape[:axis] + (N_CHIPS, shape[axis] // N_CHIPS) + shape[axis + 1:])
    return jnp.moveaxis(t, axis, 0).reshape(N_CHIPS, -1)


def _from_shards(sh, shape, axis):
    t = sh.reshape((N_CHIPS,) + _shard_shape(shape, axis))
    t = jnp.moveaxis(t, 0, axis)
    return t.reshape(shape)


BIG = tuple((name, shape, axis) for name, shape, axis, big in SHARDED if big)
SMALL_SHARDED = tuple((name, shape, axis) for name, shape, axis, big in SHARDED if not big)


def _pack_small_weights(p):
    flat = jnp.concatenate([p[name].reshape(-1) for name, _, _ in SMALL_SHARDED])
    return _pad_rows(flat, _round_up(-(-flat.shape[0] // LANES), SUBLANES))


def _unpack_small_weights(gathered):
    flat = gathered.reshape(N_CHIPS, -1)
    out, off = {}, 0
    for name, shape, axis in SMALL_SHARDED:
        n = _numel(shape) // N_CHIPS
        out[name] = _from_shards(flat[:, off:off + n], shape, axis)
        off += n
    return out


def _pack_small_local(p, prefix=""):
    parts = [p[prefix + name].reshape(-1) for name, _, _ in SMALL_SHARDED]
    parts += [p[prefix + name].reshape(-1) for name, _ in REPLICATED]
    return jnp.concatenate(parts)


def _pack_small_grads(g):
    parts = [_to_shards(g[name], axis) for name, _, axis in SMALL_SHARDED]
    rep = jnp.concatenate([g[name].reshape(-1) for name, _ in REPLICATED])
    parts.append(jnp.broadcast_to(rep[None], (N_CHIPS, rep.shape[0])))
    return jnp.concatenate(parts, axis=1)


def _unpack_small_local(flat):
    out, off = {}, 0
    for name, shape, axis in SMALL_SHARDED:
        n = _numel(shape) // N_CHIPS
        out[name] = flat[off:off + n].reshape((1,) + _shard_shape(shape, axis))
        off += n
    for name, shape in REPLICATED:
        n = _numel(shape)
        out[name] = flat[off:off + n].reshape((1,) + shape)
        off += n
    return out


def _grad_shards(g, shape, axis):
    if axis == 0:
        return g.reshape((N_CHIPS,) + _shard_shape(shape, axis))
    return jnp.transpose(g.reshape(shape[0], N_CHIPS, shape[1] // N_CHIPS), (1, 0, 2))


def _cols_from_shards(w4):
    return jnp.transpose(w4, (1, 0, 2)).reshape(w4.shape[1], -1)


def _block_diag(w):
    eye = jnp.eye(LRU_BLOCKS, dtype=w.dtype)
    return jnp.einsum("ncd,nm->ncmd", w, eye).reshape(LRU_W, LRU_W)


def _block_diag_t(g):
    g4 = g.reshape(LRU_BLOCKS, 64, LRU_BLOCKS, 64)
    return jnp.stack([g4[n, :, n, :] for n in range(LRU_BLOCKS)])


def _pad8(a):
    return jnp.pad(a, ((0, SUBLANES - a.shape[0]), (0, 0)))


def kernel(x, mem, positions, attn_norm, w_in, lru_conv_w, lru_conv_b, lru_w_a, lru_b_a, lru_w_i, lru_b_i, lru_lambda, q_a_norm, w_uq, kv_a_norm, w_ukv, mla_q_norm, mla_k_norm, lru_out_norm, mla_out_norm, w_out, mem_attn_norm, mem_norm, w_mem_q, w_mem_kv, mem_q_norm, mem_k_norm, w_mem_o, ffn_norm, w_up, ffn_conv_w, ffn_conv_b, w_down, loss_target, m_attn_norm, m_w_in, m_lru_conv_w, m_lru_conv_b, m_lru_w_a, m_lru_b_a, m_lru_w_i, m_lru_b_i, m_lru_lambda, m_q_a_norm, m_w_uq, m_kv_a_norm, m_w_ukv, m_mla_q_norm, m_mla_k_norm, m_lru_out_norm, m_mla_out_norm, m_w_out, m_mem_attn_norm, m_mem_norm, m_w_mem_q, m_w_mem_kv, m_mem_q_norm, m_mem_k_norm, m_w_mem_o, m_ffn_norm, m_w_up, m_ffn_conv_w, m_ffn_conv_b, m_w_down, v_attn_norm, v_w_in, v_lru_conv_w, v_lru_conv_b, v_lru_w_a, v_lru_b_a, v_lru_w_i, v_lru_b_i, v_lru_lambda, v_q_a_norm, v_w_uq, v_kv_a_norm, v_w_ukv, v_mla_q_norm, v_mla_k_norm, v_lru_out_norm, v_mla_out_norm, v_w_out, v_mem_attn_norm, v_mem_norm, v_w_mem_q, v_w_mem_kv, v_mem_q_norm, v_mem_k_norm, v_w_mem_o, v_ffn_norm, v_w_up, v_ffn_conv_w, v_ffn_conv_b, v_w_down):
    given = dict(locals())
    local = {name: given[name][0] for name in WEIGHT_ORDER}
    s = x.shape[1]
    x2d, mem2d, tgt = x[0], mem[0], loss_target[0]
    tm = min(256, s)
    tm_ffn = min(128, s)
    t_scan = min(256, s)
    tq, tk = min(256, s), min(512, s)

    got = _gather_chips([local[name].astype(BF16) for name, _, _ in BIG] + [_pack_small_weights(local)])
    full = _unpack_small_weights(got[-1])
    for (name, shape, axis), w4 in zip(BIG, got):
        if axis == 0:
            full[name] = w4.reshape(shape)
        elif name in ("w_up", "w_mem_o"):
            full[name] = w4
        else:
            full[name] = _cols_from_shards(w4)
    row = lambda a: a.reshape(1, -1)
    b16 = lambda a: a.astype(BF16)
    zeros = lambda r, c: jnp.zeros((r, c), BF16)
    w_in_f = full["w_in"]
    w_in_p = jnp.concatenate([w_in_f[:, :OFF_KR], zeros(D_MODEL, QK_NOPE), w_in_f[:, OFF_KR:],
                              zeros(D_MODEL, LANES - QK_HEAD)], axis=1)
    w_uq_p = jnp.pad(full["w_uq"].reshape(Q_LORA, HEADS, QK_HEAD), ((0, 0), (0, 0), (0, LANES - QK_HEAD))).reshape(Q_LORA, -1)
    ukv = full["w_ukv"].reshape(KV_LORA, HEADS, QK_NOPE + V_DIM)
    w_uk_p = jnp.pad(ukv[:, :, :QK_NOPE], ((0, 0), (0, 0), (0, LANES - QK_NOPE))).reshape(KV_LORA, -1)
    w_uv = ukv[:, :, QK_NOPE:].reshape(KV_LORA, MLA_W)
    wa = [b16(_block_diag(local["lru_w_a"][d])) for d in range(2)]
    wi = [b16(_block_diag(local["lru_w_i"][d])) for d in range(2)]
    cw = [_pad8(full["lru_conv_w"][d]) for d in range(2)]
    pv = [_pad8(jnp.stack([full["lru_conv_b"][d], full["lru_b_a"][d], full["lru_b_i"][d], full["lru_lambda"][d]]))
          for d in range(2)]
    ffn_cw = _pad8(jnp.concatenate([full["ffn_conv_w"], row(local["ffn_conv_b"])], axis=0))
    g_attn, g_qa, g_kva = row(local["attn_norm"]), row(local["q_a_norm"]), row(local["kv_a_norm"])
    g_qn = jnp.pad(row(local["mla_q_norm"]), ((0, 0), (0, LANES - QK_HEAD)))
    g_kn = jnp.pad(row(local["mla_k_norm"]), ((0, 0), (0, LANES - QK_HEAD)))
    g_lru, g_mla = row(local["lru_out_norm"]), row(local["mla_out_norm"])
    g_memattn, g_mem = row(local["mem_attn_norm"]), row(local["mem_norm"])
    g_mq, g_mk, g_ffn = row(local["mem_q_norm"]), row(local["mem_k_norm"]), row(local["ffn_norm"])

    inv = ROPE_THETA ** (-jnp.arange(0, QK_ROPE, 2, dtype=F32) / QK_ROPE)
    ang = positions[0].astype(F32)[:, None] * inv
    cosv, sinv = jnp.cos(ang), jnp.sin(ang)
    ones, zer = jnp.ones((s, QK_NOPE), F32), jnp.zeros((s, LANES - QK_HEAD), F32)
    cos_t = jnp.concatenate([ones, cosv, cosv, zer + 1.0], axis=1)
    sin_t = jnp.concatenate([ones * 0.0, -sinv, sinv, zer], axis=1)

    xr, yg, cq, ckv, krp, hb_in = _in_proj(x2d, g_attn, w_in_p, tm)
    h_f = _lru_scan_fwd(xr, cw[0], pv[0], wa[0], wi[0], False, t_scan)
    h_b = _lru_scan_fwd(xr, cw[1], pv[1], wa[1], wi[1], True, t_scan)
    q, k, v = _mla_qkv(cq, ckv, krp, cos_t, sin_t, g_qa, g_kva, g_qn, g_kn, w_uq_p, w_uk_p, w_uv, tm)
    o, lse = _attn_fwd(q, k, v, tq, tk)
    x1, mixed = _mix_out(h_f, h_b, yg, o, x2d, g_lru, g_mla, full["w_out"], tm)
    km, vm = _mem_kv(mem2d, g_mem, full["w_mem_kv"], g_mk)
    x2, o_mem = _mem_attn(x1, g_memattn, full["w_mem_q"], g_mq, km, vm, full["w_mem_o"], tm)
    gu_pre, hb_ffn = _ffn_up(x2, g_ffn, full["w_up"], tm)
    dy, dyb, act, loss_acc = _ffn_down_loss(gu_pre, x2, tgt, ffn_cw, full["w_down"], tm_ffn)
    loss = lax.psum(loss_acc[0, 0] * (0.5 / D_MODEL), ("x", "y", "c"))

    grads = {}
    grads["w_down"] = _matmul_tn(act, dyb, "grad_w_down")
    (dgu,) = _ffn_bwd_act(dyb, gu_pre, ffn_cw, full["w_down"], tm_ffn)
    dpre, g_conv = _ffn_bwd_conv(dgu, gu_pre, ffn_cw, tm_ffn)
    grads["ffn_conv_w"], grads["ffn_conv_b"] = g_conv[:3], g_conv[3]
    grads["w_up"] = _matmul_tn(hb_ffn, dpre, "grad_w_up", col_shards=True)
    dx2, dx2b, gg = _ffn_bwd_in(dpre, x2, dy, g_ffn, full["w_up"], tm)
    grads["ffn_norm"] = gg[0]
    grads["w_mem_o"] = _matmul_tn(o_mem, dx2b, "grad_w_mem_o", col_shards=True)
    dx1, dx1b, hm, dqr_mem, dkm, dvm, gg, ggq = _mem_attn_bwd(x1, dx2, dx2b, g_memattn, full["w_mem_q"], g_mq, km, vm,
                                                                 full["w_mem_o"], tm)
    grads["mem_attn_norm"], grads["mem_q_norm"] = gg[0], ggq[0]
    grads["w_mem_q"] = _matmul_tn(hm, dqr_mem, "grad_w_mem_q")
    grads["w_mem_kv"], gg, ggk, _ = _mem_kv_bwd(mem2d, g_mem, full["w_mem_kv"], g_mk, dkm, dvm)
    grads["mem_norm"], grads["mem_k_norm"] = gg[0], ggk[0]
    grads["w_out"] = _matmul_tn(mixed, dx1b, "grad_w_out")
    dh, dyg, dob, dl128, ggl, ggm = _mix_out_bwd(dx1b, h_f, h_b, yg, o, g_lru, g_mla, full["w_out"], tm)
    grads["lru_out_norm"], grads["mla_out_norm"] = ggl[0], ggm[0]
    lse_t = jnp.transpose(lse, (0, 2, 1))
    delta_t = jnp.transpose(dl128[:, :HEADS]).reshape(HEADS // 2, 2, s)
    dk, dv, dq = _attn_bwd(q, k, v, dob, lse_t, delta_t, tq, tk)
    (dcq, dckv, dkrp, cqb, dqr, ckvb, dkn, dvb, ggqa, ggkva, ggqn, ggkn) = _mla_qkv_bwd(
        cq, ckv, krp, cos_t, sin_t, dq, dk, dv, g_qa, g_kva, g_qn, g_kn, w_uq_p, w_uk_p, w_uv, tm)
    grads["q_a_norm"], grads["kv_a_norm"] = ggqa[0], ggkva[0]
    grads["mla_q_norm"], grads["mla_k_norm"] = ggqn[0, :QK_HEAD], ggkn[0, :QK_HEAD]
    g_uq_p = _matmul_tn(cqb, dqr, "grad_w_uq")
    grads["w_uq"] = g_uq_p.reshape(Q_LORA, HEADS, LANES)[:, :, :QK_HEAD].reshape(Q_LORA, -1)
    g_uk_p = _matmul_tn(ckvb, dkn, "grad_w_uk").reshape(KV_LORA, HEADS, LANES)[:, :, :QK_NOPE]
    g_uv = _matmul_tn(ckvb, dvb, "grad_w_uv").reshape(KV_LORA, HEADS, V_DIM)
    grads["w_ukv"] = jnp.concatenate([g_uk_p, g_uv], axis=2).reshape(KV_LORA, -1)
    dxr, gwa, gwi, gvec = [], [], [], []
    for d, hd in enumerate((h_f, h_b)):
        r = _lru_scan_bwd(xr, hd, dh, cw[d], pv[d], wa[d], wi[d], d == 1, t_scan)
        dxr.append(r[0])
        gwa.append(_block_diag_t(r[1]))
        gwi.append(_block_diag_t(r[2]))
        gvec.append(r[3])
    grads["lru_w_a"], grads["lru_w_i"] = jnp.stack(gwa), jnp.stack(gwi)
    grads["lru_conv_w"] = jnp.stack([gv[:CONV_W] for gv in gvec])
    for r_i, name in ((4, "lru_conv_b"), (5, "lru_b_a"), (6, "lru_b_i"), (7, "lru_lambda")):
        grads[name] = jnp.stack([gv[r_i] for gv in gvec])
    grad_x, dproj, gg = _in_proj_bwd(x2d, dx1, dxr[0], dxr[1], dyg, dcq, dckv, dkrp, g_attn, w_in_p, tm)
    grads["attn_norm"] = gg[0]
    g_in_p = _matmul_tn(hb_in, dproj, "grad_w_in")
    grads["w_in"] = jnp.concatenate([g_in_p[:, :OFF_KR], g_in_p[:, OFF_KR + QK_NOPE:OFF_KR + QK_HEAD]], axis=1)

    small = _pack_small_grads(grads)
    length = small.shape[1]
    hrows = _round_up(-(-length // (2 * LANES)), 16)
    small = jnp.pad(small, ((0, 0), (0, 2 * hrows * LANES - length))).reshape(N_CHIPS, 2, hrows, LANES)
    parts = []
    for name, shape, axis in BIG:
        g4 = grads[name] if grads[name].ndim == 3 else _grad_shards(grads[name], shape, axis)
        parts.append(g4.reshape(N_CHIPS, 2, g4.shape[1] // 2, g4.shape[2]))
    parts.append(small)
    names = [name for name, _, _ in BIG] + ["small"]
    c_idx = lax.axis_index("c").astype(jnp.int32).reshape(1)
    sibling = _swap_halves(parts)
    chip_sums = [_add_halves(g, b, c_idx, F32 if n == "small" else BF16, "grad_add_halves_" + n)
                 for n, g, b in zip(names, parts, sibling)]
    arrived = _scatter_chips(chip_sums)
    reduced = _join_halves([_sum_chips(b, c_idx, "grad_sum_chips_" + n) for n, b in zip(names, arrived)])

    outs = [{}, {}, {}, {}]
    for (name, shape, axis), r in zip(BIG, reduced):
        g2 = r.reshape(_shard_shape(shape, axis))
        res = _adamw(local[name], g2, given["m_" + name][0], given["v_" + name][0], "adamw_" + name)
        for o_, a in zip(outs, (g2, *res)):
            o_[name] = a[None]
    pack = lambda prefix: _pad_rows(_pack_small_local({n: given[prefix + n] for n in WEIGHT_ORDER}), 2 * hrows)
    g_small = reduced[-1].reshape(2 * hrows, LANES)
    res = _adamw(pack(""), g_small, pack("m_"), pack("v_"), "adamw_small")
    for o_, a in zip(outs, (g_small, *res)):
        o_.update(_unpack_small_local(a.reshape(-1)))
    return (loss, grad_x[None], *[o_[n] for o_ in outs for n in WEIGHT_ORDER])
```

```python
import functools

import jax
import jax.numpy as jnp
from jax import lax
from jax.experimental import pallas as pl
from jax.experimental.pallas import tpu as pltpu

F32, BF16 = jnp.float32, jnp.bfloat16
MESH = pl.DeviceIdType.MESH

D_MODEL = 1024
EPS = 1e-6
LRU_W = 512
LRU_BLOCKS = 8
LRU_C = 8.0
CONV_W = 4
HEADS = 8
QK_NOPE, QK_ROPE, QK_HEAD, V_DIM = 64, 32, 96, 64
Q_LORA, KV_LORA = 256, 128
MLA_W = HEADS * V_DIM
ROPE_THETA = 10000.0
IN_COLS = 2 * LRU_W + Q_LORA + KV_LORA + QK_ROPE
OFF_KR = IN_COLS - QK_ROPE
IN_PAD = 1536
MEM_HEADS, MEM_HD = 4, 128
MEM_W = MEM_HEADS * MEM_HD
D_FF = 2816
N_CHIPS = 4
ADAM_LR, ADAM_B1, ADAM_B2, ADAM_EPS, ADAM_WD, ADAM_STEP = 0.001, 0.9, 0.999, 1e-08, 0.01, 10

LANES = 128
SUBLANES = 8
VMEM_LIMIT = 56 * 1024 * 1024
PACK_ROWS = 2048

SHARDED = (
    ("w_in", (D_MODEL, IN_COLS), 1, True),
    ("lru_conv_w", (2, CONV_W, LRU_W), 2, False),
    ("lru_conv_b", (2, LRU_W), 1, False),
    ("lru_b_a", (2, LRU_W), 1, False),
    ("lru_b_i", (2, LRU_W), 1, False),
    ("lru_lambda", (2, LRU_W), 1, False),
    ("w_uq", (Q_LORA, HEADS * QK_HEAD), 1, True),
    ("w_ukv", (KV_LORA, HEADS * (QK_NOPE + V_DIM)), 1, True),
    ("w_out", (2 * LRU_W, D_MODEL), 0, True),
    ("w_mem_q", (D_MODEL, MEM_W), 0, True),
    ("w_mem_kv", (D_MODEL, 2 * MEM_W), 0, True),
    ("w_mem_o", (MEM_W, D_MODEL), 1, True),
    ("w_up", (D_MODEL, 2 * D_FF), 1, True),
    ("ffn_conv_w", (3, 2 * D_FF), 1, False),
    ("w_down", (D_FF, D_MODEL), 0, True),
)
REPLICATED = (
    ("attn_norm", (D_MODEL,)), ("lru_w_a", (2, LRU_BLOCKS, 64, 64)), ("lru_w_i", (2, LRU_BLOCKS, 64, 64)),
    ("q_a_norm", (Q_LORA,)), ("kv_a_norm", (KV_LORA,)), ("mla_q_norm", (QK_HEAD,)), ("mla_k_norm", (QK_HEAD,)),
    ("lru_out_norm", (LRU_W,)), ("mla_out_norm", (MLA_W,)), ("mem_attn_norm", (D_MODEL,)), ("mem_norm", (D_MODEL,)),
    ("mem_q_norm", (MEM_HD,)), ("mem_k_norm", (MEM_HD,)), ("ffn_norm", (D_MODEL,)), ("ffn_conv_b", (2 * D_FF,)),
)
WEIGHT_ORDER = ('attn_norm', 'w_in', 'lru_conv_w', 'lru_conv_b', 'lru_w_a', 'lru_b_a', 'lru_w_i', 'lru_b_i', 'lru_lambda',
                'q_a_norm', 'w_uq', 'kv_a_norm', 'w_ukv', 'mla_q_norm', 'mla_k_norm', 'lru_out_norm', 'mla_out_norm', 'w_out',
                'mem_attn_norm', 'mem_norm', 'w_mem_q', 'w_mem_kv', 'mem_q_norm', 'mem_k_norm', 'w_mem_o', 'ffn_norm', 'w_up',
                'ffn_conv_w', 'ffn_conv_b', 'w_down')


def _numel(shape):
    n = 1
    for s in shape:
        n *= s
    return n


def _cparams(n_axes):
    return pltpu.CompilerParams(dimension_semantics=("arbitrary",) * n_axes, vmem_limit_bytes=VMEM_LIMIT)


def _bdot(a, b):
    return jnp.dot(a.astype(BF16), b.astype(BF16), preferred_element_type=F32)


def _bdot_nt(a, b):
    return lax.dot_general(a.astype(BF16), b.astype(BF16), (((1,), (1,)), ((), ())), preferred_element_type=F32)


def _bdot_tn(a, b):
    return lax.dot_general(a.astype(BF16), b.astype(BF16), (((0,), (0,)), ((), ())), preferred_element_type=F32)


def _rstd(x, n=None):
    n = x.shape[-1] if n is None else n
    return lax.rsqrt(jnp.sum(x * x, axis=-1, keepdims=True) * (1.0 / n) + EPS)


def _norm_bwd(x, rs, g, dy, n=None):
    n = x.shape[-1] if n is None else n
    xhat = x * rs
    dxh = dy * g
    dx = rs * (dxh - xhat * (jnp.sum(dxh * xhat, axis=-1, keepdims=True) * (1.0 / n)))
    return dx, dy * xhat


def _acc_row(ref, r, val):
    ref[r:r + 1, :] += jnp.sum(val, axis=0, keepdims=True)


def _zero_first(i, *refs):
    @pl.when(i == 0)
    def _():
        for r in refs:
            r[...] = jnp.zeros_like(r)


def _shift_down(x, j, halo):
    if j == 0:
        return x
    xs = pltpu.roll(x, j, 0)
    hs = pltpu.roll(halo, j, 0)
    row = lax.broadcasted_iota(jnp.int32, hs.shape, 0)
    top = jnp.where(row < j, hs, xs[:SUBLANES])
    return jnp.concatenate([top, xs[SUBLANES:]], axis=0)


def _shift_up(x, j, halo):
    if j == 0:
        return x
    t = x.shape[0]
    xs = pltpu.roll(x, t - j, 0)
    hs = pltpu.roll(halo, SUBLANES - j, 0)
    row = lax.broadcasted_iota(jnp.int32, hs.shape, 0)
    bot = jnp.where(row >= SUBLANES - j, hs, xs[t - SUBLANES:])
    return jnp.concatenate([xs[:t - SUBLANES], bot], axis=0)


def _shift(x, j, halo, down):
    return _shift_down(x, j, halo) if down else _shift_up(x, j, halo)


def _scan(a, b, down):
    t = a.shape[0]
    row = lax.broadcasted_iota(jnp.int32, a.shape, 0)
    d = 1
    while d < t:
        if down:
            keep = row >= d
            a_s = jnp.where(keep, pltpu.roll(a, d, 0), 1.0)
            b_s = jnp.where(keep, pltpu.roll(b, d, 0), 0.0)
        else:
            keep = row < t - d
            a_s = jnp.where(keep, pltpu.roll(a, t - d, 0), 1.0)
            b_s = jnp.where(keep, pltpu.roll(b, t - d, 0), 0.0)
        b = a * b_s + b
        a = a * a_s
        d *= 2
    return a, b


def _sigmoid(x):
    return 1.0 / (1.0 + jnp.exp(-x))


LOG2E = 1.4426950408889634
GELU_K = 0.7978845608028654
GELU_C = 0.044715


def _gelu(x):
    return 0.5 * x * (1.0 + jnp.tanh(GELU_K * (x + GELU_C * x * x * x)))


def _gelu_grad(x):
    t = jnp.tanh(GELU_K * (x + GELU_C * x * x * x))
    return 0.5 * (1.0 + t) + 0.5 * x * (1.0 - t * t) * GELU_K * (1.0 + 3.0 * GELU_C * x * x)


def _rope_partner(x):
    lane = lax.broadcasted_iota(jnp.int32, x.shape, 1)
    half = QK_ROPE // 2
    sw = jnp.where(lane < QK_NOPE + half, pltpu.roll(x, LANES - half, 1), pltpu.roll(x, half, 1))
    return jnp.where((lane >= QK_NOPE) & (lane < QK_HEAD), sw, 0.0)


def _rope(x, cos_t, sin_t):
    return x * cos_t + _rope_partner(x) * sin_t


def _rope_t(dy, cos_t, sin_t):
    return dy * cos_t + _rope_partner(dy * sin_t)


def _rowwise(body, name, s, tm, rows=(), halos=(), fulls=(), outs=(), accs=()):
    n = s // tm
    hb = tm // SUBLANES
    last8 = s // SUBLANES - 1
    in_specs, args = [], []
    for a in rows:
        in_specs.append(pl.BlockSpec((tm, a.shape[1]), lambda i: (i, 0)))
        args.append(a)
    for a in halos:
        in_specs.append(pl.BlockSpec((SUBLANES, a.shape[1]), lambda i: (jnp.maximum(i * hb - 1, 0), 0)))
        in_specs.append(pl.BlockSpec((SUBLANES, a.shape[1]), lambda i: (jnp.minimum((i + 1) * hb, last8), 0)))
        args += [a, a]
    for a in fulls:
        in_specs.append(pl.BlockSpec(a.shape, lambda i, nd=a.ndim: (0,) * nd))
        args.append(a)
    out_shape, out_specs = [], []
    for c, dt in outs:
        out_shape.append(jax.ShapeDtypeStruct((s, c), dt))
        out_specs.append(pl.BlockSpec((tm, c), lambda i: (i, 0)))
    for shp, dt in accs:
        out_shape.append(jax.ShapeDtypeStruct(shp, dt))
        out_specs.append(pl.BlockSpec(shp, lambda i, nd=len(shp): (0,) * nd))

    def kern(*refs):
        body(pl.program_id(0), n, *refs)

    return pl.pallas_call(kern, grid=(n,), in_specs=in_specs, out_specs=out_specs, out_shape=out_shape, name=name,
                          compiler_params=_cparams(1))(*args)


def _matmul_tn(a, b, name, col_shards=False):
    t, m = a.shape
    n = b.shape[1]
    bm = 256 if m % 256 == 0 else m
    bn = n // N_CHIPS if col_shards else (n if n <= 2048 else 1408)
    bt = min(512, t)
    nt = t // bt

    def kern(a_ref, b_ref, o_ref):
        @pl.when(pl.program_id(2) == 0)
        def _():
            o_ref[...] = jnp.zeros_like(o_ref)
        o_ref[...] += _bdot_tn(a_ref[...], b_ref[...])

    if col_shards:
        out_spec = pl.BlockSpec((None, bm, bn), lambda i, j, k: (j, i, 0))
        out_shape = jax.ShapeDtypeStruct((N_CHIPS, m, bn), F32)
    else:
        out_spec = pl.BlockSpec((bm, bn), lambda i, j, k: (i, j))
        out_shape = jax.ShapeDtypeStruct((m, n), F32)
    return pl.pallas_call(
        kern, grid=(m // bm, n // bn, nt),
        in_specs=[pl.BlockSpec((bt, bm), lambda i, j, k: (k, i)), pl.BlockSpec((bt, bn), lambda i, j, k: (k, j))],
        out_specs=out_spec, out_shape=out_shape, name=name, compiler_params=_cparams(3))(a, b)


def _in_proj(x, g, w_in_p, tm):
    def body(i, n, x_ref, g_ref, w_ref, xr, yg, cq, ckv, krp, hb):
        xv = x_ref[...]
        h = (xv * _rstd(xv) * g_ref[...]).astype(BF16)
        hb[...] = h
        p = jnp.dot(h, w_ref[...], preferred_element_type=F32)
        xr[...] = p[:, :LRU_W]
        yg[...] = p[:, LRU_W:2 * LRU_W]
        cq[...] = p[:, 2 * LRU_W:2 * LRU_W + Q_LORA]
        ckv[...] = p[:, 2 * LRU_W + Q_LORA:OFF_KR]
        krp[...] = p[:, OFF_KR:IN_PAD]

    return _rowwise(body, "in_proj", x.shape[0], tm, rows=[x], fulls=[g, w_in_p],
                    outs=[(LRU_W, F32), (LRU_W, F32), (Q_LORA, F32), (KV_LORA, F32), (LANES, F32), (D_MODEL, BF16)])


def _lru_gates(x, halo, cw_ref, pv_ref, wa_ref, wi_ref, rev):
    down = not rev
    xc = pv_ref[0:1, :] + jnp.zeros_like(x)
    for j in range(CONV_W):
        k = j if rev else CONV_W - 1 - j
        xc = xc + cw_ref[k:k + 1, :] * _shift(x, j, halo, down)
    r = _sigmoid(_bdot(xc, wa_ref[...]) + pv_ref[1:2, :])
    ig = _sigmoid(_bdot(xc, wi_ref[...]) + pv_ref[2:3, :])
    lam = pv_ref[3:4, :]
    sp = jnp.maximum(-lam, 0.0) + jnp.log(1.0 + jnp.exp(-jnp.abs(lam)))
    log_a = (-LRU_C) * r * sp
    a = jnp.exp(log_a)
    z = 2.0 * log_a
    series = -(z * (1.0 + z * (0.5 + z * (1.0 / 6.0 + z * (1.0 / 24.0)))))
    om = jnp.where(z > -0.02, series, 1.0 - jnp.exp(z))
    mult = jnp.sqrt(om)
    return xc, r, ig, sp, a, mult


def _lru_scan_fwd(xr, cw, pv, wa, wi, rev, t):
    s = xr.shape[0]
    n = s // t
    hb = t // SUBLANES
    last8 = s // SUBLANES - 1
    down = not rev

    def kern(x_ref, halo_ref, cw_ref, pv_ref, wa_ref, wi_ref, h_ref, carry_ref):
        i = pl.program_id(0)
        _zero_first(i, carry_ref)
        halo = jnp.where(i == 0, 0.0, halo_ref[...])
        xc, r, ig, sp, a, mult = _lru_gates(x_ref[...], halo, cw_ref, pv_ref, wa_ref, wi_ref, rev)
        aa, bb = _scan(a, mult * ig * xc, down)
        h_ref[...] = aa * carry_ref[...] + bb
        carry_ref[...] = h_ref[pl.ds(t - 1 if down else 0, 1), :]

    if rev:
        blk = lambda i: (n - 1 - i, 0)
        hal = lambda i: (jnp.minimum((n - i) * hb, last8), 0)
    else:
        blk = lambda i: (i, 0)
        hal = lambda i: (jnp.maximum(i * hb - 1, 0), 0)
    full = lambda a: pl.BlockSpec(a.shape, lambda i: (0, 0))
    return pl.pallas_call(
        kern, grid=(n,),
        in_specs=[pl.BlockSpec((t, LRU_W), blk), pl.BlockSpec((SUBLANES, LRU_W), hal), full(cw), full(pv), full(wa), full(wi)],
        out_specs=pl.BlockSpec((t, LRU_W), blk), out_shape=jax.ShapeDtypeStruct((s, LRU_W), F32),
        scratch_shapes=[pltpu.VMEM((1, LRU_W), F32)], name="lru_scan_rev" if rev else "lru_scan_fwd",
        compiler_params=_cparams(1))(xr, xr, cw, pv, wa, wi)


def _lru_scan_bwd(xr, h, dh, cw, pv, wa, wi, rev, t):
    s = xr.shape[0]
    n = s // t
    hb = t // SUBLANES
    last8 = s // SUBLANES - 1
    down = not rev

    def kern(x_ref, xh_ref, h_ref, hh_ref, dh_ref, cw_ref, pv_ref, wa_ref, wi_ref,
             dx_ref, gwa_ref, gwi_ref, gv_ref, p_ref, dxc_halo_ref, tmp_ref):
        i = pl.program_id(0)
        _zero_first(i, gwa_ref, gwi_ref, gv_ref, p_ref, dxc_halo_ref)
        at_start = i == n - 1
        x = x_ref[...]
        xhalo = jnp.where(at_start, 0.0, xh_ref[...])
        hhalo = jnp.where(at_start, 0.0, hh_ref[...])
        xc, r, ig, sp, a, mult = _lru_gates(x, xhalo, cw_ref, pv_ref, wa_ref, wi_ref, rev)
        h_prev = _shift(h_ref[...], 1, hhalo, down)
        row = lax.broadcasted_iota(jnp.int32, x.shape, 0)
        edge = t - 1 if down else 0
        dh_mod = dh_ref[...] + jnp.where(row == edge, p_ref[...], 0.0)
        a_next = _shift(a, 1, jnp.zeros((SUBLANES, LRU_W), F32), not down)
        _, g = _scan(a_next, dh_mod, not down)
        tmp_ref[...] = a * g
        p_ref[...] = tmp_ref[pl.ds(0 if down else t - 1, 1), :]
        da = g * h_prev
        d_ig = g * mult * xc
        d_xc = g * mult * ig
        d_om = g * ig * xc * (0.5 / jnp.maximum(mult, 1e-30))
        d_log_a = da * a - 2.0 * d_om * a * a
        d_r = d_log_a * ((-LRU_C) * sp)
        d_sp = jnp.sum(d_log_a * ((-LRU_C) * r), axis=0, keepdims=True)
        lam = pv_ref[3:4, :]
        gv_ref[7:8, :] += d_sp * (-_sigmoid(-lam))
        d_ga = d_r * r * (1.0 - r)
        d_gi = d_ig * ig * (1.0 - ig)
        _acc_row(gv_ref, 5, d_ga)
        _acc_row(gv_ref, 6, d_gi)
        d_xc = d_xc + _bdot_nt(d_ga, wa_ref[...]) + _bdot_nt(d_gi, wi_ref[...])
        gwa_ref[...] += _bdot_tn(xc, d_ga)
        gwi_ref[...] += _bdot_tn(xc, d_gi)
        _acc_row(gv_ref, 4, d_xc)
        dx = jnp.zeros_like(x)
        dxc_halo = dxc_halo_ref[...]
        for j in range(CONV_W):
            k = j if rev else CONV_W - 1 - j
            _acc_row(gv_ref, k, d_xc * _shift(x, j, xhalo, down))
            dx = dx + cw_ref[k:k + 1, :] * _shift(d_xc, j, dxc_halo, not down)
        dx_ref[...] = dx
        dxc_halo_ref[...] = d_xc[:SUBLANES] if down else d_xc[t - SUBLANES:]

    if rev:
        blk = lambda i: (i, 0)
        hal = lambda i: (jnp.minimum((i + 1) * hb, last8), 0)
    else:
        blk = lambda i: (n - 1 - i, 0)
        hal = lambda i: (jnp.maximum((n - 1 - i) * hb - 1, 0), 0)
    full = lambda a: pl.BlockSpec(a.shape, lambda i: (0, 0))
    bs = pl.BlockSpec((t, LRU_W), blk)
    hs = pl.BlockSpec((SUBLANES, LRU_W), hal)
    return pl.pallas_call(
        kern, grid=(n,),
        in_specs=[bs, hs, bs, hs, bs, full(cw), full(pv), full(wa), full(wi)],
        out_specs=[bs, pl.BlockSpec((LRU_W, LRU_W), lambda i: (0, 0)), pl.BlockSpec((LRU_W, LRU_W), lambda i: (0, 0)),
                   pl.BlockSpec((SUBLANES, LRU_W), lambda i: (0, 0))],
        out_shape=[jax.ShapeDtypeStruct((s, LRU_W), F32), jax.ShapeDtypeStruct((LRU_W, LRU_W), F32),
                   jax.ShapeDtypeStruct((LRU_W, LRU_W), F32), jax.ShapeDtypeStruct((SUBLANES, LRU_W), F32)],
        scratch_shapes=[pltpu.VMEM((1, LRU_W), F32), pltpu.VMEM((SUBLANES, LRU_W), F32), pltpu.VMEM((t, LRU_W), F32)],
        name="lru_bwd_rev" if rev else "lru_bwd_fwd", compiler_params=_cparams(1))(xr, xr, h, h, dh, cw, pv, wa, wi)


def _mla_qkv(cq, ckv, krp, cos_t, sin_t, g_qa, g_kva, g_qn, g_kn, w_uq_p, w_uk_p, w_uv, tm):
    scale = QK_HEAD ** -0.5 * LOG2E

    def body(i, n, cq_ref, ckv_ref, kr_ref, c_ref, s_ref, gqa, gkva, gqn, gkn, wq, wk, wv, q_out, k_out, v_out):
        cosv, sinv = c_ref[...], s_ref[...]
        cqv = cq_ref[...]
        qr = _bdot(cqv * _rstd(cqv) * gqa[...], wq[...])
        ckvv = ckv_ref[...]
        c_kv = (ckvv * _rstd(ckvv) * gkva[...]).astype(BF16)
        kn = jnp.dot(c_kv, wk[...], preferred_element_type=F32)
        v_out[...] = jnp.dot(c_kv, wv[...], preferred_element_type=F32).astype(BF16)
        kr = kr_ref[...]
        for h in range(HEADS):
            sl = slice(h * LANES, (h + 1) * LANES)
            qh = qr[:, sl]
            qh = _rope(qh * _rstd(qh, QK_HEAD) * gqn[...], cosv, sinv) * scale
            q_out[:, sl] = qh.astype(BF16)
            kh = kn[:, sl] + kr
            kh = _rope(kh * _rstd(kh, QK_HEAD) * gkn[...], cosv, sinv)
            k_out[:, sl] = kh.astype(BF16)

    return _rowwise(body, "mla_qkv", cq.shape[0], tm, rows=[cq, ckv, krp, cos_t, sin_t],
                    fulls=[g_qa, g_kva, g_qn, g_kn, w_uq_p, w_uk_p, w_uv],
                    outs=[(HEADS * LANES, BF16), (HEADS * LANES, BF16), (MLA_W, BF16)])


NT_DIMS = (((1,), (1,)), ((), ()))


def _attn_fwd(q, k, vt, tq, tk):
    s = q.shape[0]
    nq, nk = s // tq, s // tk

    def kern(q_ref, k_ref, vt_ref, o_ref, lse_ref, acc_ref):
        qs = (q_ref[:, :LANES], q_ref[:, LANES:])
        acc_ref[...] = jnp.zeros_like(acc_ref)

        def step(j, carry):
            off = pl.multiple_of(j * tk, tk)
            vtc = vt_ref[:, pl.ds(off, tk)]
            out = []
            for h in range(2):
                m, l = carry[2 * h:2 * h + 2]
                st = lax.dot_general(k_ref[pl.ds(off, tk), h * LANES:(h + 1) * LANES], qs[h], NT_DIMS,
                                     preferred_element_type=F32)
                mn = jnp.maximum(m, jnp.max(st, axis=0, keepdims=True))
                al = jnp.exp2(m - mn)
                pt = jnp.exp2(st - mn)
                l = al * l + jnp.sum(pt, axis=0, keepdims=True)
                acc_ref[h] = al * acc_ref[h] + jnp.dot(vtc, pt.astype(BF16), preferred_element_type=F32)
                out += [mn, l]
            return tuple(out)

        init = (jnp.full((1, tq), -1e30, F32), jnp.zeros((1, tq), F32)) * 2
        m0, l0, m1, l1 = lax.fori_loop(0, nk, step, init)
        row = lax.broadcasted_iota(jnp.int32, (LANES, tq), 0)
        o_ref[...] = jnp.where(row < V_DIM, acc_ref[0] / l0, acc_ref[1] / l1).T
        lse_ref[0, 0:1, :] = m0 + jnp.log2(l0)
        lse_ref[0, 1:2, :] = m1 + jnp.log2(l1)

    return pl.pallas_call(
        kern, grid=(HEADS // 2, nq),
        in_specs=[pl.BlockSpec((tq, 2 * LANES), lambda p, i: (i, p)), pl.BlockSpec((s, 2 * LANES), lambda p, i: (0, p)),
                  pl.BlockSpec((LANES, s), lambda p, i: (p, 0))],
        out_specs=[pl.BlockSpec((tq, LANES), lambda p, i: (i, p)), pl.BlockSpec((1, 2, tq), lambda p, i: (p, 0, i))],
        out_shape=[jax.ShapeDtypeStruct((s, MLA_W), F32), jax.ShapeDtypeStruct((HEADS // 2, 2, s), F32)],
        scratch_shapes=[pltpu.VMEM((2, LANES, tq), F32)], name="attn_fwd", compiler_params=_cparams(2))(q, k, vt)


def _attn_bwd(q, k, kt, v, do, lse, delta, tq, tk):
    s = q.shape[0]
    nq, nk = s // tq, s // tk

    def kern(q_ref, do_ref, lse_ref, dl_ref, k_ref, kt_ref, v_ref, dqt_ref, dk_ref, dv_ref, acc_ref):
        _zero_first(pl.program_id(1), dk_ref, dv_ref)
        acc_ref[...] = jnp.zeros_like(acc_ref)
        qs = (q_ref[:, :LANES], q_ref[:, LANES:])
        doc = do_ref[...]
        lane_q = lax.broadcasted_iota(jnp.int32, (tq, LANES), 1)
        zq = jnp.zeros_like(doc)
        dos = (jnp.where(lane_q < V_DIM, doc, zq), jnp.where(lane_q >= V_DIM, doc, zq))
        lses = (lse_ref[0, 0:1, :], lse_ref[0, 1:2, :])
        dls = (dl_ref[0, 0:1, :], dl_ref[0, 1:2, :])

        def step(j, carry):
            off = pl.multiple_of(j * tk, tk)
            vp = v_ref[pl.ds(off, tk), :]
            lane_k = lax.broadcasted_iota(jnp.int32, (tk, LANES), 1)
            zero = jnp.zeros_like(vp)
            vs = (jnp.where(lane_k < V_DIM, vp, zero), jnp.where(lane_k >= V_DIM, vp, zero))
            for h in range(2):
                sl = slice(h * LANES, (h + 1) * LANES)
                st = lax.dot_general(k_ref[pl.ds(off, tk), sl], qs[h], NT_DIMS, preferred_element_type=F32)
                pt = jnp.exp2(st - lses[h])
                dpt = lax.dot_general(vs[h], doc, NT_DIMS, preferred_element_type=F32)
                dst = (pt * (dpt - dls[h])).astype(BF16)
                dv_ref[pl.ds(off, tk), :] += jnp.dot(pt.astype(BF16), dos[h], preferred_element_type=F32)
                dk_ref[pl.ds(off, tk), sl] += jnp.dot(dst, qs[h], preferred_element_type=F32)
                acc_ref[h] += jnp.dot(kt_ref[sl, pl.ds(off, tk)], dst, preferred_element_type=F32)
            return carry

        lax.fori_loop(0, nk, step, 0)
        dqt_ref[:LANES, :] = acc_ref[0]
        dqt_ref[LANES:, :] = acc_ref[1]

    return pl.pallas_call(
        kern, grid=(HEADS // 2, nq),
        in_specs=[pl.BlockSpec((tq, 2 * LANES), lambda p, i: (i, p)), pl.BlockSpec((tq, LANES), lambda p, i: (i, p)),
                  pl.BlockSpec((1, 2, tq), lambda p, i: (p, 0, i)), pl.BlockSpec((1, 2, tq), lambda p, i: (p, 0, i)),
                  pl.BlockSpec((s, 2 * LANES), lambda p, i: (0, p)), pl.BlockSpec((2 * LANES, s), lambda p, i: (p, 0)),
                  pl.BlockSpec((s, LANES), lambda p, i: (0, p))],
        out_specs=[pl.BlockSpec((2 * LANES, tq), lambda p, i: (p, i)), pl.BlockSpec((s, 2 * LANES), lambda p, i: (0, p)),
                   pl.BlockSpec((s, LANES), lambda p, i: (0, p))],
        out_shape=[jax.ShapeDtypeStruct((HEADS * LANES, s), F32), jax.ShapeDtypeStruct((s, HEADS * LANES), F32),
                   jax.ShapeDtypeStruct((s, MLA_W), F32)],
        scratch_shapes=[pltpu.VMEM((2, LANES, tq), F32)], name="attn_bwd", compiler_params=_cparams(2))(
            q, do, lse, delta, k, kt, v)


def _mix_out(hf, hb, yg, o, x, g_lru, g_mla, w_out, tm):
    def body(i, n, hf_ref, hb_ref, yg_ref, o_ref, x_ref, gl, gm, w_ref, x1_ref, mix_ref):
        lo = (hf_ref[...] + hb_ref[...]) * _gelu(yg_ref[...])
        ov = o_ref[...]
        mix_ref[:, :LRU_W] = (lo * _rstd(lo) * gl[...]).astype(BF16)
        mix_ref[:, LRU_W:] = (ov * _rstd(ov) * gm[...]).astype(BF16)
        x1_ref[...] = x_ref[...] + jnp.dot(mix_ref[...], w_ref[...], preferred_element_type=F32)

    return _rowwise(body, "mix_out", x.shape[0], tm, rows=[hf, hb, yg, o, x], fulls=[g_lru, g_mla, w_out],
                    outs=[(D_MODEL, F32), (2 * LRU_W, BF16)])


def _mem_kv(mem, g_mem, w_kv, g_k):
    m = mem.shape[0]

    def body(i, n, mem_ref, g_ref, w_ref, gk_ref, km_ref, vm_ref):
        mv = mem_ref[...]
        kv = _bdot(mv * _rstd(mv) * g_ref[...], w_ref[...])
        vm_ref[...] = kv[:, MEM_W:].astype(BF16)
        for h in range(MEM_HEADS):
            sl = slice(h * MEM_HD, (h + 1) * MEM_HD)
            kh = kv[:, sl]
            km_ref[:, sl] = (kh * _rstd(kh) * gk_ref[...]).astype(BF16)

    return _rowwise(body, "mem_kv", m, m, rows=[mem], fulls=[g_mem, w_kv, g_k], outs=[(MEM_W, BF16), (MEM_W, BF16)])


def _mem_attn_core(x1v, g_ref, wq_ref, gq_ref, km_ref, vm_ref):
    scale = MEM_HD ** -0.5
    hm = (x1v * _rstd(x1v) * g_ref[...]).astype(BF16)
    qr = jnp.dot(hm, wq_ref[...], preferred_element_type=F32)
    heads = []
    for h in range(MEM_HEADS):
        sl = slice(h * MEM_HD, (h + 1) * MEM_HD)
        qh = qr[:, sl]
        rs = _rstd(qh)
        qn = (qh * rs * gq_ref[...]).astype(BF16)
        sc = lax.dot_general(qn, km_ref[:, sl], (((1,), (1,)), ((), ())), preferred_element_type=F32) * scale
        e = jnp.exp(sc - jnp.max(sc, axis=-1, keepdims=True))
        p = e / jnp.sum(e, axis=-1, keepdims=True)
        oh = jnp.dot(p.astype(BF16), vm_ref[:, sl], preferred_element_type=F32)
        heads.append((qh, rs, qn, p, oh))
    return hm, heads


def _mem_attn(x1, g, w_q, g_q, km, vm, w_o, tm):
    cs = D_MODEL // N_CHIPS

    def body(i, n, x1_ref, g_ref, wq_ref, gq_ref, km_ref, vm_ref, wo_ref, x2_ref, ob_ref):
        x1v = x1_ref[...]
        _, heads = _mem_attn_core(x1v, g_ref, wq_ref, gq_ref, km_ref, vm_ref)
        for h in range(MEM_HEADS):
            ob_ref[:, h * MEM_HD:(h + 1) * MEM_HD] = heads[h][4].astype(BF16)
        for k in range(N_CHIPS):
            sl = slice(k * cs, (k + 1) * cs)
            x2_ref[:, sl] = x1v[:, sl] + jnp.dot(ob_ref[...], wo_ref[k], preferred_element_type=F32)

    return _rowwise(body, "mem_attn", x1.shape[0], tm, rows=[x1], fulls=[g, w_q, g_q, km, vm, w_o],
                    outs=[(D_MODEL, F32), (MEM_W, BF16)])


def _ffn_up(x2, g, w_up, tm):
    cs = 2 * D_FF // N_CHIPS

    def body(i, n, x_ref, g_ref, w_ref, gu_ref, hb_ref):
        xv = x_ref[...]
        hb_ref[...] = (xv * _rstd(xv) * g_ref[...]).astype(BF16)
        for k in range(N_CHIPS):
            gu_ref[:, k * cs:(k + 1) * cs] = jnp.dot(hb_ref[...], w_ref[k], preferred_element_type=F32)

    return _rowwise(body, "ffn_up", x2.shape[0], tm, rows=[x2], fulls=[g, w_up], outs=[(2 * D_FF, F32), (D_MODEL, BF16)])


def _ffn_conv(gu, prev, nxt, cw_ref, i, n):
    prev = jnp.where(i == 0, 0.0, prev)
    nxt = jnp.where(i == n - 1, 0.0, nxt)
    return (cw_ref[3:4, :] + cw_ref[0:1, :] * _shift_down(gu, 1, prev) + cw_ref[1:2, :] * gu
            + cw_ref[2:3, :] * _shift_up(gu, 1, nxt))


def _ffn_down_loss(gu_pre, x2, target, cw, w_down, tm):
    def body(i, n, gu_ref, x_ref, t_ref, pv_ref, nx_ref, cw_ref, w_ref, dy_ref, dyb_ref, act_ref, loss_ref):
        _zero_first(i, loss_ref)
        gu = _ffn_conv(gu_ref[...], pv_ref[...], nx_ref[...], cw_ref, i, n)
        g, u = gu[:, :D_FF], gu[:, D_FF:]
        act_ref[...] = (g * _sigmoid(g) * u).astype(BF16)
        y = x_ref[...] + jnp.dot(act_ref[...], w_ref[...], preferred_element_type=F32)
        e = y - t_ref[...]
        loss_ref[...] += jnp.sum(e * e)
        dy = e * (1.0 / D_MODEL)
        dy_ref[...] = dy
        dyb_ref[...] = dy.astype(BF16)

    return _rowwise(body, "ffn_down_loss", x2.shape[0], tm, rows=[gu_pre, x2, target], halos=[gu_pre], fulls=[cw, w_down],
                    outs=[(D_MODEL, F32), (D_MODEL, BF16), (D_FF, BF16)], accs=[((SUBLANES, LANES), F32)])


def _ffn_bwd_act(dyb, gu_pre, cw, w_down, tm):
    def body(i, n, dy_ref, gu_ref, pv_ref, nx_ref, cw_ref, w_ref, dgu_ref):
        d_act = lax.dot_general(dy_ref[...], w_ref[...], (((1,), (1,)), ((), ())), preferred_element_type=F32)
        gu = _ffn_conv(gu_ref[...], pv_ref[...], nx_ref[...], cw_ref, i, n)
        g, u = gu[:, :D_FF], gu[:, D_FF:]
        sg = _sigmoid(g)
        dgu_ref[:, :D_FF] = d_act * u * sg * (1.0 + g * (1.0 - sg))
        dgu_ref[:, D_FF:] = d_act * g * sg

    return _rowwise(body, "ffn_bwd_act", dyb.shape[0], tm, rows=[dyb, gu_pre], halos=[gu_pre], fulls=[cw, w_down],
                    outs=[(2 * D_FF, F32)])


def _ffn_bwd_conv(dgu, gu_pre, cw, tm):
    def body(i, n, d_ref, g_ref, dp_ref, dn_ref, gp_ref, gn_ref, cw_ref, dpre_ref, gc_ref):
        _zero_first(i, gc_ref)
        first, last = i == 0, i == n - 1
        d = d_ref[...]
        g = g_ref[...]
        dpv, dnx = jnp.where(first, 0.0, dp_ref[...]), jnp.where(last, 0.0, dn_ref[...])
        gpv, gnx = jnp.where(first, 0.0, gp_ref[...]), jnp.where(last, 0.0, gn_ref[...])
        dpre_ref[...] = (cw_ref[0:1, :] * _shift_up(d, 1, dnx) + cw_ref[1:2, :] * d
                         + cw_ref[2:3, :] * _shift_down(d, 1, dpv)).astype(BF16)
        _acc_row(gc_ref, 0, d * _shift_down(g, 1, gpv))
        _acc_row(gc_ref, 1, d * g)
        _acc_row(gc_ref, 2, d * _shift_up(g, 1, gnx))
        _acc_row(gc_ref, 3, d)

    return _rowwise(body, "ffn_bwd_conv", dgu.shape[0], tm, rows=[dgu, gu_pre], halos=[dgu, gu_pre], fulls=[cw],
                    outs=[(2 * D_FF, BF16)], accs=[((SUBLANES, 2 * D_FF), F32)])


def _ffn_bwd_in(dpre, x2, dy, g, w_up, tm):
    cs = 2 * D_FF // N_CHIPS

    def body(i, n, dp_ref, x_ref, dy_ref, g_ref, w_ref, dx_ref, dxb_ref, gg_ref):
        _zero_first(i, gg_ref)
        d_h = jnp.zeros(x_ref.shape, F32)
        for k in range(N_CHIPS):
            d_h = d_h + lax.dot_general(dp_ref[:, k * cs:(k + 1) * cs], w_ref[k], (((1,), (1,)), ((), ())),
                                        preferred_element_type=F32)
        xv = x_ref[...]
        dx, dg = _norm_bwd(xv, _rstd(xv), g_ref[...], d_h)
        _acc_row(gg_ref, 0, dg)
        dx = dx + dy_ref[...]
        dx_ref[...] = dx
        dxb_ref[...] = dx.astype(BF16)

    return _rowwise(body, "ffn_bwd_in", x2.shape[0], tm, rows=[dpre, x2, dy], fulls=[g, w_up],
                    outs=[(D_MODEL, F32), (D_MODEL, BF16)], accs=[((SUBLANES, D_MODEL), F32)])


def _mem_attn_bwd(x1, dx2, dx2b, g, w_q, g_q, km, vm, w_o, tm):
    scale = MEM_HD ** -0.5
    m = km.shape[0]

    def body(i, n, x1_ref, dx2_ref, dx2b_ref, g_ref, wq_ref, gq_ref, km_ref, vm_ref, wo_ref,
             dx1_ref, dx1b_ref, hm_ref, dqr_ref, dkm_ref, dvm_ref, gg_ref, ggq_ref):
        _zero_first(i, dkm_ref, dvm_ref, gg_ref, ggq_ref)
        x1v = x1_ref[...]
        hm, heads = _mem_attn_core(x1v, g_ref, wq_ref, gq_ref, km_ref, vm_ref)
        hm_ref[...] = hm
        cs = D_MODEL // N_CHIPS
        d_o = jnp.zeros((x1v.shape[0], MEM_W), F32)
        for k in range(N_CHIPS):
            d_o = d_o + lax.dot_general(dx2b_ref[:, k * cs:(k + 1) * cs], wo_ref[k], (((1,), (1,)), ((), ())),
                                        preferred_element_type=F32)
        for h in range(MEM_HEADS):
            sl = slice(h * MEM_HD, (h + 1) * MEM_HD)
            qh, rs, qn, p, _ = heads[h]
            d_oh = d_o[:, sl].astype(BF16)
            dp = lax.dot_general(d_oh, vm_ref[:, sl], (((1,), (1,)), ((), ())), preferred_element_type=F32)
            ds = (p * (dp - jnp.sum(dp * p, axis=-1, keepdims=True)) * scale).astype(BF16)
            dqn = jnp.dot(ds, km_ref[:, sl], preferred_element_type=F32)
            dkm_ref[:, sl] += lax.dot_general(ds, qn, (((0,), (0,)), ((), ())), preferred_element_type=F32)
            dvm_ref[:, sl] += lax.dot_general(p.astype(BF16), d_oh, (((0,), (0,)), ((), ())), preferred_element_type=F32)
            dqh, dgq = _norm_bwd(qh, rs, gq_ref[...], dqn)
            _acc_row(ggq_ref, 0, dgq)
            dqr_ref[:, sl] = dqh.astype(BF16)
        d_hm = lax.dot_general(dqr_ref[...], wq_ref[...], (((1,), (1,)), ((), ())), preferred_element_type=F32)
        dx, dg = _norm_bwd(x1v, _rstd(x1v), g_ref[...], d_hm)
        _acc_row(gg_ref, 0, dg)
        dx = dx + dx2_ref[...]
        dx1_ref[...] = dx
        dx1b_ref[...] = dx.astype(BF16)

    return _rowwise(body, "mem_attn_bwd", x1.shape[0], tm, rows=[x1, dx2, dx2b], fulls=[g, w_q, g_q, km, vm, w_o],
                    outs=[(D_MODEL, F32), (D_MODEL, BF16), (D_MODEL, BF16), (MEM_W, BF16)],
                    accs=[((m, MEM_W), F32), ((m, MEM_W), F32), ((SUBLANES, D_MODEL), F32), ((SUBLANES, MEM_HD), F32)])


def _mem_kv_bwd(mem, g_mem, w_kv, g_k, dkm, dvm):
    m = mem.shape[0]

    def body(i, n, mem_ref, dkm_ref, dvm_ref, g_ref, w_ref, gk_ref, gw_ref, gg_ref, ggk_ref, dkv_ref):
        gg_ref[...] = jnp.zeros_like(gg_ref)
        ggk_ref[...] = jnp.zeros_like(ggk_ref)
        mv = mem_ref[...]
        rs_m = _rstd(mv)
        mem_n = (mv * rs_m * g_ref[...]).astype(BF16)
        kv = jnp.dot(mem_n, w_ref[...], preferred_element_type=F32)
        for h in range(MEM_HEADS):
            sl = slice(h * MEM_HD, (h + 1) * MEM_HD)
            kh = kv[:, sl]
            dkh, dgk = _norm_bwd(kh, _rstd(kh), gk_ref[...], dkm_ref[:, sl])
            _acc_row(ggk_ref, 0, dgk)
            dkv_ref[:, sl] = dkh.astype(BF16)
        dkv_ref[:, MEM_W:] = dvm_ref[...].astype(BF16)
        gw_ref[...] = lax.dot_general(mem_n, dkv_ref[...], (((0,), (0,)), ((), ())), preferred_element_type=F32)
        d_mn = lax.dot_general(dkv_ref[...], w_ref[...], (((1,), (1,)), ((), ())), preferred_element_type=F32)
        _acc_row(gg_ref, 0, d_mn * (mv * rs_m))

    return _rowwise(body, "mem_kv_bwd", m, m, rows=[mem, dkm, dvm], fulls=[g_mem, w_kv, g_k],
                    accs=[((D_MODEL, 2 * MEM_W), F32), ((SUBLANES, D_MODEL), F32), ((SUBLANES, MEM_HD), F32),
                          ((m, 2 * MEM_W), BF16)])


def _mix_out_bwd(dx1b, hf, hb, yg, o, g_lru, g_mla, w_out, tm):
    def body(i, n, dx_ref, hf_ref, hb_ref, yg_ref, o_ref, gl, gm, w_ref, dh_ref, dyg_ref, dob_ref, dl_ref, ggl_ref, ggm_ref):
        _zero_first(i, ggl_ref, ggm_ref)
        dmix = lax.dot_general(dx_ref[...], w_ref[...], (((1,), (1,)), ((), ())), preferred_element_type=F32)
        hs = hf_ref[...] + hb_ref[...]
        ygv = yg_ref[...]
        ge = _gelu(ygv)
        lo = hs * ge
        d_lo, dgl = _norm_bwd(lo, _rstd(lo), gl[...], dmix[:, :LRU_W])
        _acc_row(ggl_ref, 0, dgl)
        dh_ref[...] = d_lo * ge
        dyg_ref[...] = d_lo * hs * _gelu_grad(ygv)
        ov = o_ref[...]
        d_o, dgm = _norm_bwd(ov, _rstd(ov), gm[...], dmix[:, LRU_W:])
        _acc_row(ggm_ref, 0, dgm)
        dob_ref[...] = d_o.astype(BF16)
        prod = d_o * ov
        lane_w = lax.broadcasted_iota(jnp.int32, prod.shape, 1)
        lane = lax.broadcasted_iota(jnp.int32, (prod.shape[0], LANES), 1)
        dl = jnp.zeros((prod.shape[0], LANES), F32)
        for h in range(HEADS):
            in_head = (lane_w >= h * V_DIM) & (lane_w < (h + 1) * V_DIM)
            dl = dl + jnp.where(lane == h, jnp.sum(jnp.where(in_head, prod, 0.0), axis=-1, keepdims=True), 0.0)
        dl_ref[...] = dl

    return _rowwise(body, "mix_out_bwd", dx1b.shape[0], tm, rows=[dx1b, hf, hb, yg, o], fulls=[g_lru, g_mla, w_out],
                    outs=[(LRU_W, F32), (LRU_W, F32), (MLA_W, BF16), (LANES, F32)],
                    accs=[((SUBLANES, LRU_W), F32), ((SUBLANES, MLA_W), F32)])


def _mla_qkv_bwd(cq, ckv, krp, cos_t, sin_t, dq, dk, dv, g_qa, g_kva, g_qn, g_kn, w_uq_p, w_uk_p, w_uv, tm):
    scale = QK_HEAD ** -0.5

    def body(i, n, cq_ref, ckv_ref, kr_ref, c_ref, s_ref, dq_ref, dk_ref, dv_ref, gqa, gkva, gqn, gkn, wq, wk, wv,
             dcq_ref, dckv_ref, dkr_ref, cqb_ref, dqr_ref, ckvb_ref, dkn_ref, dvb_ref, ggqa, ggkva, ggqn, ggkn):
        _zero_first(i, ggqa, ggkva, ggqn, ggkn)
        cosv, sinv = c_ref[...], s_ref[...]
        cqv = cq_ref[...]
        rs_q = _rstd(cqv)
        cqb_ref[...] = (cqv * rs_q * gqa[...]).astype(BF16)
        qr = jnp.dot(cqb_ref[...], wq[...], preferred_element_type=F32)
        ckvv = ckv_ref[...]
        rs_kv = _rstd(ckvv)
        ckvb_ref[...] = (ckvv * rs_kv * gkva[...]).astype(BF16)
        kn = jnp.dot(ckvb_ref[...], wk[...], preferred_element_type=F32)
        kr = kr_ref[...]
        dkr = jnp.zeros_like(kr)
        for h in range(HEADS):
            sl = slice(h * LANES, (h + 1) * LANES)
            qh = qr[:, sl]
            d_qn = _rope_t(dq_ref[:, sl] * scale, cosv, sinv)
            dqh, dgq = _norm_bwd(qh, _rstd(qh, QK_HEAD), gqn[...], d_qn, QK_HEAD)
            _acc_row(ggqn, 0, dgq)
            dqr_ref[:, sl] = dqh.astype(BF16)
            kh = kn[:, sl] + kr
            d_kn = _rope_t(dk_ref[:, sl] * (1.0 / LOG2E), cosv, sinv)
            dkh, dgk = _norm_bwd(kh, _rstd(kh, QK_HEAD), gkn[...], d_kn, QK_HEAD)
            _acc_row(ggkn, 0, dgk)
            dkn_ref[:, sl] = dkh.astype(BF16)
            dkr = dkr + dkh
        dkr_ref[...] = dkr
        dvb_ref[...] = dv_ref[...].astype(BF16)
        d_cq = lax.dot_general(dqr_ref[...], wq[...], (((1,), (1,)), ((), ())), preferred_element_type=F32)
        dcq, dg = _norm_bwd(cqv, rs_q, gqa[...], d_cq)
        _acc_row(ggqa, 0, dg)
        dcq_ref[...] = dcq
        d_ckv = (lax.dot_general(dkn_ref[...], wk[...], (((1,), (1,)), ((), ())), preferred_element_type=F32)
                 + lax.dot_general(dvb_ref[...], wv[...], (((1,), (1,)), ((), ())), preferred_element_type=F32))
        dckv, dg = _norm_bwd(ckvv, rs_kv, gkva[...], d_ckv)
        _acc_row(ggkva, 0, dg)
        dckv_ref[...] = dckv

    return _rowwise(body, "mla_qkv_bwd", cq.shape[0], tm, rows=[cq, ckv, krp, cos_t, sin_t, dq, dk, dv],
                    fulls=[g_qa, g_kva, g_qn, g_kn, w_uq_p, w_uk_p, w_uv],
                    outs=[(Q_LORA, F32), (KV_LORA, F32), (LANES, F32), (Q_LORA, BF16), (HEADS * LANES, BF16),
                          (KV_LORA, BF16), (HEADS * LANES, BF16), (MLA_W, BF16)],
                    accs=[((SUBLANES, Q_LORA), F32), ((SUBLANES, KV_LORA), F32), ((SUBLANES, LANES), F32),
                          ((SUBLANES, LANES), F32)])


def _in_proj_bwd(x, dx1, dxr_f, dxr_b, dyg, dcq, dckv, dkrp, g, w_in_p, tm):
    def body(i, n, x_ref, dx1_ref, df_ref, db_ref, dyg_ref, dcq_ref, dckv_ref, dkr_ref, g_ref, w_ref, gx_ref, dp_ref, gg_ref):
        _zero_first(i, gg_ref)
        dp_ref[:, :LRU_W] = (df_ref[...] + db_ref[...]).astype(BF16)
        dp_ref[:, LRU_W:2 * LRU_W] = dyg_ref[...].astype(BF16)
        dp_ref[:, 2 * LRU_W:2 * LRU_W + Q_LORA] = dcq_ref[...].astype(BF16)
        dp_ref[:, 2 * LRU_W + Q_LORA:OFF_KR] = dckv_ref[...].astype(BF16)
        dp_ref[:, OFF_KR:] = dkr_ref[...].astype(BF16)
        d_h = lax.dot_general(dp_ref[...], w_ref[...], (((1,), (1,)), ((), ())), preferred_element_type=F32)
        xv = x_ref[...]
        dx, dg = _norm_bwd(xv, _rstd(xv), g_ref[...], d_h)
        _acc_row(gg_ref, 0, dg)
        gx_ref[...] = dx + dx1_ref[...]

    return _rowwise(body, "in_proj_bwd", x.shape[0], tm, rows=[x, dx1, dxr_f, dxr_b, dyg, dcq, dckv, dkrp],
                    fulls=[g, w_in_p], outs=[(D_MODEL, F32), (IN_PAD, BF16)], accs=[((SUBLANES, D_MODEL), F32)])


ANY = pl.BlockSpec(memory_space=pl.ANY)


def _chip_peers(x, y):
    return ((1 - x, y), (x, 1 - y), (1 - x, 1 - y))


def _exchange_call(kern, name, ins, out_shapes, n_sems, aliases=None):
    return pl.pallas_call(
        kern, in_specs=[ANY] * len(ins), out_specs=[ANY] * len(out_shapes), out_shape=out_shapes,
        scratch_shapes=[pltpu.SemaphoreType.DMA((n,)) for n in n_sems], input_output_aliases=aliases or {},
        name=name)(*ins)


def _start_then_wait(copies):
    for cp in copies:
        cp.start()
    for cp in copies:
        cp.wait()


def _gather_chips(arrs):
    n = len(arrs)

    def kern(*refs):
        ins, outs, (ssem, rsem, lsem) = refs[:n], refs[n:2 * n], refs[2 * n:]
        x, y, c = lax.axis_index("x"), lax.axis_index("y"), lax.axis_index("c")
        me = 2 * x + y
        cps = []
        for i in range(n):
            cps.append(pltpu.make_async_copy(ins[i], outs[i].at[me], lsem.at[i]))
            for j, (px, py) in enumerate(_chip_peers(x, y)):
                cps.append(pltpu.make_async_remote_copy(ins[i], outs[i].at[me], ssem.at[3 * i + j], rsem.at[3 * i + j],
                                                        device_id=(px, py, c), device_id_type=MESH))
        _start_then_wait(cps)

    outs = [jax.ShapeDtypeStruct((N_CHIPS,) + a.shape, a.dtype) for a in arrs]
    return _exchange_call(kern, "gather_weights", arrs, outs, (3 * n, 3 * n, n))


def _swap_halves(gs):
    n = len(gs)

    def kern(*refs):
        ins, outs, (ssem, rsem) = refs[:n], refs[n:2 * n], refs[2 * n:]
        x, y, c = lax.axis_index("x"), lax.axis_index("y"), lax.axis_index("c")
        _start_then_wait([
            pltpu.make_async_remote_copy(ins[i].at[k, 1 - c], outs[i].at[k], ssem.at[N_CHIPS * i + k],
                                         rsem.at[N_CHIPS * i + k], device_id=(x, y, 1 - c), device_id_type=MESH)
            for i in range(n) for k in range(N_CHIPS)])

    outs = [jax.ShapeDtypeStruct((N_CHIPS,) + g.shape[2:], g.dtype) for g in gs]
    return _exchange_call(kern, "grad_swap_halves", gs, outs, (N_CHIPS * n, N_CHIPS * n))


def _scatter_chips(arrs):
    n = len(arrs)

    def kern(*refs):
        ins, outs, (ssem, rsem, lsem) = refs[:n], refs[n:2 * n], refs[2 * n:]
        x, y, c = lax.axis_index("x"), lax.axis_index("y"), lax.axis_index("c")
        me = 2 * x + y
        cps = []
        for i in range(n):
            cps.append(pltpu.make_async_copy(ins[i].at[me], outs[i].at[me], lsem.at[i]))
            for j, (px, py) in enumerate(_chip_peers(x, y)):
                cps.append(pltpu.make_async_remote_copy(ins[i].at[2 * px + py], outs[i].at[me], ssem.at[3 * i + j],
                                                        rsem.at[3 * i + j], device_id=(px, py, c), device_id_type=MESH))
        _start_then_wait(cps)

    outs = [jax.ShapeDtypeStruct(a.shape, a.dtype) for a in arrs]
    return _exchange_call(kern, "grad_scatter_chips", arrs, outs, (3 * n, 3 * n, n))


def _join_halves(arrs):
    n = len(arrs)

    def kern(*refs):
        outs, (ssem, rsem) = refs[n:2 * n], refs[2 * n:]
        x, y, c = lax.axis_index("x"), lax.axis_index("y"), lax.axis_index("c")
        _start_then_wait([
            pltpu.make_async_remote_copy(outs[i].at[c], outs[i].at[c], ssem.at[i], rsem.at[i],
                                         device_id=(x, y, 1 - c), device_id_type=MESH) for i in range(n)])

    outs = [jax.ShapeDtypeStruct(a.shape, a.dtype) for a in arrs]
    return _exchange_call(kern, "grad_join_halves", arrs, outs, (n, n), aliases={i: i for i in range(n)})


def _row_block(rows, row_bytes, limit=1 << 20):
    best = None
    for d in range(16, rows + 1, 16):
        if rows % d == 0 and d * row_bytes <= limit:
            best = d
    return best if best is not None else rows


def _add_halves(g, got, c, out_dtype, name):
    _, _, h, cols = g.shape
    hb = _row_block(h, cols * 4)

    def kern(c_ref, g_ref, b_ref, o_ref):
        o_ref[...] = (g_ref[...] + b_ref[...]).astype(out_dtype)

    blk = (None, hb, cols)
    return pl.pallas_call(
        kern,
        grid_spec=pltpu.PrefetchScalarGridSpec(
            num_scalar_prefetch=1, grid=(N_CHIPS, h // hb),
            in_specs=[pl.BlockSpec((None, None, hb, cols), lambda k, i, c_ref: (k, c_ref[0], i, 0)),
                      pl.BlockSpec(blk, lambda k, i, c_ref: (k, i, 0))],
            out_specs=pl.BlockSpec(blk, lambda k, i, c_ref: (k, i, 0))),
        out_shape=jax.ShapeDtypeStruct(got.shape, out_dtype), name=name, compiler_params=_cparams(2))(c, g, got)


def _sum_chips(b, c, name):
    _, h, cols = b.shape
    hb = _row_block(h, cols * 4)

    def kern(c_ref, b_ref, o_ref):
        f = lambda k: b_ref[k].astype(F32)
        o_ref[...] = ((f(0) + f(1)) + f(2)) + f(3)

    return pl.pallas_call(
        kern,
        grid_spec=pltpu.PrefetchScalarGridSpec(
            num_scalar_prefetch=1, grid=(h // hb,),
            in_specs=[pl.BlockSpec((N_CHIPS, hb, cols), lambda i, c_ref: (0, i, 0))],
            out_specs=pl.BlockSpec((None, hb, cols), lambda i, c_ref: (c_ref[0], i, 0))),
        out_shape=jax.ShapeDtypeStruct((2, h, cols), F32), name=name, compiler_params=_cparams(1))(c, b)


def _adamw(w, g, m, v, name):
    rows, cols = w.shape
    rb = _row_block(rows, cols * 4)
    c1 = 1.0 - ADAM_B1 ** ADAM_STEP
    c2 = 1.0 - ADAM_B2 ** ADAM_STEP

    def kern(w_ref, g_ref, m_ref, v_ref, d_ref, mo_ref, vo_ref):
        gv = g_ref[...]
        mn = ADAM_B1 * m_ref[...] + (1.0 - ADAM_B1) * gv
        vn = ADAM_B2 * v_ref[...] + (1.0 - ADAM_B2) * (gv * gv)
        mo_ref[...] = mn
        vo_ref[...] = vn
        d_ref[...] = (-ADAM_LR) * ((mn / c1) / (jnp.sqrt(vn / c2) + ADAM_EPS) + ADAM_WD * w_ref[...])

    spec = pl.BlockSpec((rb, cols), lambda i: (i, 0))
    return pl.pallas_call(
        kern, grid=(rows // rb,), in_specs=[spec] * 4, out_specs=[spec] * 3,
        out_shape=[jax.ShapeDtypeStruct(w.shape, F32)] * 3, name=name, compiler_params=_cparams(1))(w, g, m, v)


def _pad_rows(flat, rows):
    return jnp.pad(flat, (0, rows * LANES - flat.shape[0])).reshape(rows, LANES)


def _round_up(n, m):
    return (n + m - 1) // m * m


def _shard_shape(shape, axis):
    return tuple(s // N_CHIPS if a == axis else s for a, s in enumerate(shape))


def _to_shards(full, axis):
    shape = full.shape
    t = full.reshape(shape[:axis] + (N_CHIPS, shape[axis] // N_CHIPS) + shape[axis + 1:])
    return jnp.moveaxis(t, axis, 0).reshape(N_CHIPS, -1)


def _from_shards(sh, shape, axis):
    t = sh.reshape((N_CHIPS,) + _shard_shape(shape, axis))
    t = jnp.moveaxis(t, 0, axis)
    return t.reshape(shape)


BIG = tuple((name, shape, axis) for name, shape, axis, big in SHARDED if big)
SMALL_SHARDED = tuple((name, shape, axis) for name, shape, axis, big in SHARDED if not big)


def _pack_small_weights(p):
    flat = jnp.concatenate([p[name].reshape(-1) for name, _, _ in SMALL_SHARDED])
    return _pad_rows(flat, _round_up(-(-flat.shape[0] // LANES), SUBLANES))


def _unpack_small_weights(gathered):
    flat = gathered.reshape(N_CHIPS, -1)
    out, off = {}, 0
    for name, shape, axis in SMALL_SHARDED:
        n = _numel(shape) // N_CHIPS
        out[name] = _from_shards(flat[:, off:off + n], shape, axis)
        off += n
    return out


def _pack_small_local(p, prefix=""):
    parts = [p[prefix + name].reshape(-1) for name, _, _ in SMALL_SHARDED]
    parts += [p[prefix + name].reshape(-1) for name, _ in REPLICATED]
    return jnp.concatenate(parts)


def _pack_small_grads(g):
    parts = [_to_shards(g[name], axis) for name, _, axis in SMALL_SHARDED]
    rep = jnp.concatenate([g[name].reshape(-1) for name, _ in REPLICATED])
    parts.append(jnp.broadcast_to(rep[None], (N_CHIPS, rep.shape[0])))
    return jnp.concatenate(parts, axis=1)


def _unpack_small_local(flat):
    out, off = {}, 0
    for name, shape, axis in SMALL_SHARDED:
        n = _numel(shape) // N_CHIPS
        out[name] = flat[off:off + n].reshape((1,) + _shard_shape(shape, axis))
        off += n
    for name, shape in REPLICATED:
        n = _numel(shape)
        out[name] = flat[off:off + n].reshape((1,) + shape)
        off += n
    return out


def _grad_shards(g, shape, axis):
    if axis == 0:
        return g.reshape((N_CHIPS,) + _shard_shape(shape, axis))
    return jnp.transpose(g.reshape(shape[0], N_CHIPS, shape[1] // N_CHIPS), (1, 0, 2))


def _cols_from_shards(w4):
    return jnp.transpose(w4, (1, 0, 2)).reshape(w4.shape[1], -1)


def _block_diag(w):
    eye = jnp.eye(LRU_BLOCKS, dtype=w.dtype)
    return jnp.einsum("ncd,nm->ncmd", w, eye).reshape(LRU_W, LRU_W)


def _block_diag_t(g):
    g4 = g.reshape(LRU_BLOCKS, 64, LRU_BLOCKS, 64)
    return jnp.stack([g4[n, :, n, :] for n in range(LRU_BLOCKS)])


def _pad8(a):
    return jnp.pad(a, ((0, SUBLANES - a.shape[0]), (0, 0)))


def kernel(x, mem, positions, attn_norm, w_in, lru_conv_w, lru_conv_b, lru_w_a, lru_b_a, lru_w_i, lru_b_i, lru_lambda, q_a_norm, w_uq, kv_a_norm, w_ukv, mla_q_norm, mla_k_norm, lru_out_norm, mla_out_norm, w_out, mem_attn_norm, mem_norm, w_mem_q, w_mem_kv, mem_q_norm, mem_k_norm, w_mem_o, ffn_norm, w_up, ffn_conv_w, ffn_conv_b, w_down, loss_target, m_attn_norm, m_w_in, m_lru_conv_w, m_lru_conv_b, m_lru_w_a, m_lru_b_a, m_lru_w_i, m_lru_b_i, m_lru_lambda, m_q_a_norm, m_w_uq, m_kv_a_norm, m_w_ukv, m_mla_q_norm, m_mla_k_norm, m_lru_out_norm, m_mla_out_norm, m_w_out, m_mem_attn_norm, m_mem_norm, m_w_mem_q, m_w_mem_kv, m_mem_q_norm, m_mem_k_norm, m_w_mem_o, m_ffn_norm, m_w_up, m_ffn_conv_w, m_ffn_conv_b, m_w_down, v_attn_norm, v_w_in, v_lru_conv_w, v_lru_conv_b, v_lru_w_a, v_lru_b_a, v_lru_w_i, v_lru_b_i, v_lru_lambda, v_q_a_norm, v_w_uq, v_kv_a_norm, v_w_ukv, v_mla_q_norm, v_mla_k_norm, v_lru_out_norm, v_mla_out_norm, v_w_out, v_mem_attn_norm, v_mem_norm, v_w_mem_q, v_w_mem_kv, v_mem_q_norm, v_mem_k_norm, v_w_mem_o, v_ffn_norm, v_w_up, v_ffn_conv_w, v_ffn_conv_b, v_w_down):
    given = dict(locals())
    local = {name: given[name][0] for name in WEIGHT_ORDER}
    s = x.shape[1]
    x2d, mem2d, tgt = x[0], mem[0], loss_target[0]
    tm = min(256, s)
    tm_ffn = min(128, s)
    t_scan = min(256, s)
    tq_f, tq_b, tk = min(2048, s), min(1024, s), min(512, s)

    got = _gather_chips([local[name].astype(BF16) for name, _, _ in BIG] + [_pack_small_weights(local)])
    full = _unpack_small_weights(got[-1])
    for (name, shape, axis), w4 in zip(BIG, got):
        if axis == 0:
            full[name] = w4.reshape(shape)
        elif name in ("w_up", "w_mem_o"):
            full[name] = w4
        else:
            full[name] = _cols_from_shards(w4)
    row = lambda a: a.reshape(1, -1)
    b16 = lambda a: a.astype(BF16)
    zeros = lambda r, c: jnp.zeros((r, c), BF16)
    w_in_f = full["w_in"]
    w_in_p = jnp.concatenate([w_in_f[:, :OFF_KR], zeros(D_MODEL, QK_NOPE), w_in_f[:, OFF_KR:],
                              zeros(D_MODEL, LANES - QK_HEAD)], axis=1)
    w_uq_p = jnp.pad(full["w_uq"].reshape(Q_LORA, HEADS, QK_HEAD), ((0, 0), (0, 0), (0, LANES - QK_HEAD))).reshape(Q_LORA, -1)
    ukv = full["w_ukv"].reshape(KV_LORA, HEADS, QK_NOPE + V_DIM)
    w_uk_p = jnp.pad(ukv[:, :, :QK_NOPE], ((0, 0), (0, 0), (0, LANES - QK_NOPE))).reshape(KV_LORA, -1)
    w_uv = ukv[:, :, QK_NOPE:].reshape(KV_LORA, MLA_W)
    wa = [b16(_block_diag(local["lru_w_a"][d])) for d in range(2)]
    wi = [b16(_block_diag(local["lru_w_i"][d])) for d in range(2)]
    cw = [_pad8(full["lru_conv_w"][d]) for d in range(2)]
    pv = [_pad8(jnp.stack([full["lru_conv_b"][d], full["lru_b_a"][d], full["lru_b_i"][d], full["lru_lambda"][d]]))
          for d in range(2)]
    ffn_cw = _pad8(jnp.concatenate([full["ffn_conv_w"], row(local["ffn_conv_b"])], axis=0))
    g_attn, g_qa, g_kva = row(local["attn_norm"]), row(local["q_a_norm"]), row(local["kv_a_norm"])
    g_qn = jnp.pad(row(local["mla_q_norm"]), ((0, 0), (0, LANES - QK_HEAD)))
    g_kn = jnp.pad(row(local["mla_k_norm"]), ((0, 0), (0, LANES - QK_HEAD)))
    g_lru, g_mla = row(local["lru_out_norm"]), row(local["mla_out_norm"])
    g_memattn, g_mem = row(local["mem_attn_norm"]), row(local["mem_norm"])
    g_mq, g_mk, g_ffn = row(local["mem_q_norm"]), row(local["mem_k_norm"]), row(local["ffn_norm"])

    inv = ROPE_THETA ** (-jnp.arange(0, QK_ROPE, 2, dtype=F32) / QK_ROPE)
    ang = positions[0].astype(F32)[:, None] * inv
    cosv, sinv = jnp.cos(ang), jnp.sin(ang)
    ones, zer = jnp.ones((s, QK_NOPE), F32), jnp.zeros((s, LANES - QK_HEAD), F32)
    cos_t = jnp.concatenate([ones, cosv, cosv, zer + 1.0], axis=1)
    sin_t = jnp.concatenate([ones * 0.0, -sinv, sinv, zer], axis=1)

    xr, yg, cq, ckv, krp, hb_in = _in_proj(x2d, g_attn, w_in_p, tm)
    h_f = _lru_scan_fwd(xr, cw[0], pv[0], wa[0], wi[0], False, t_scan)
    h_b = _lru_scan_fwd(xr, cw[1], pv[1], wa[1], wi[1], True, t_scan)
    q, k, v = _mla_qkv(cq, ckv, krp, cos_t, sin_t, g_qa, g_kva, g_qn, g_kn, w_uq_p, w_uk_p, w_uv, tm)
    o, lse = _attn_fwd(q, k, jnp.transpose(v), tq_f, tk)
    x1, mixed = _mix_out(h_f, h_b, yg, o, x2d, g_lru, g_mla, full["w_out"], tm)
    km, vm = _mem_kv(mem2d, g_mem, full["w_mem_kv"], g_mk)
    x2, o_mem = _mem_attn(x1, g_memattn, full["w_mem_q"], g_mq, km, vm, full["w_mem_o"], tm)
    gu_pre, hb_ffn = _ffn_up(x2, g_ffn, full["w_up"], tm)
    dy, dyb, act, loss_acc = _ffn_down_loss(gu_pre, x2, tgt, ffn_cw, full["w_down"], tm_ffn)
    loss = lax.psum(loss_acc[0, 0] * (0.5 / D_MODEL), ("x", "y", "c"))

    grads = {}
    grads["w_down"] = _matmul_tn(act, dyb, "grad_w_down")
    (dgu,) = _ffn_bwd_act(dyb, gu_pre, ffn_cw, full["w_down"], tm_ffn)
    dpre, g_conv = _ffn_bwd_conv(dgu, gu_pre, ffn_cw, tm_ffn)
    grads["ffn_conv_w"], grads["ffn_conv_b"] = g_conv[:3], g_conv[3]
    grads["w_up"] = _matmul_tn(hb_ffn, dpre, "grad_w_up", col_shards=True)
    dx2, dx2b, gg = _ffn_bwd_in(dpre, x2, dy, g_ffn, full["w_up"], tm)
    grads["ffn_norm"] = gg[0]
    grads["w_mem_o"] = _matmul_tn(o_mem, dx2b, "grad_w_mem_o", col_shards=True)
    dx1, dx1b, hm, dqr_mem, dkm, dvm, gg, ggq = _mem_attn_bwd(x1, dx2, dx2b, g_memattn, full["w_mem_q"], g_mq, km, vm,
                                                                 full["w_mem_o"], tm)
    grads["mem_attn_norm"], grads["mem_q_norm"] = gg[0], ggq[0]
    grads["w_mem_q"] = _matmul_tn(hm, dqr_mem, "grad_w_mem_q")
    grads["w_mem_kv"], gg, ggk, _ = _mem_kv_bwd(mem2d, g_mem, full["w_mem_kv"], g_mk, dkm, dvm)
    grads["mem_norm"], grads["mem_k_norm"] = gg[0], ggk[0]
    grads["w_out"] = _matmul_tn(mixed, dx1b, "grad_w_out")
    dh, dyg, dob, dl128, ggl, ggm = _mix_out_bwd(dx1b, h_f, h_b, yg, o, g_lru, g_mla, full["w_out"], tm)
    grads["lru_out_norm"], grads["mla_out_norm"] = ggl[0], ggm[0]
    delta_t = jnp.transpose(dl128[:, :HEADS]).reshape(HEADS // 2, 2, s)
    dq_t, dk, dv = _attn_bwd(q, k, jnp.transpose(k), v, dob, lse, delta_t, tq_b, tk)
    dq = jnp.transpose(dq_t)
    (dcq, dckv, dkrp, cqb, dqr, ckvb, dkn, dvb, ggqa, ggkva, ggqn, ggkn) = _mla_qkv_bwd(
        cq, ckv, krp, cos_t, sin_t, dq, dk, dv, g_qa, g_kva, g_qn, g_kn, w_uq_p, w_uk_p, w_uv, tm)
    grads["q_a_norm"], grads["kv_a_norm"] = ggqa[0], ggkva[0]
    grads["mla_q_norm"], grads["mla_k_norm"] = ggqn[0, :QK_HEAD], ggkn[0, :QK_HEAD]
    g_uq_p = _matmul_tn(cqb, dqr, "grad_w_uq")
    grads["w_uq"] = g_uq_p.reshape(Q_LORA, HEADS, LANES)[:, :, :QK_HEAD].reshape(Q_LORA, -1)
    g_uk_p = _matmul_tn(ckvb, dkn, "grad_w_uk").reshape(KV_LORA, HEADS, LANES)[:, :, :QK_NOPE]
    g_uv = _matmul_tn(ckvb, dvb, "grad_w_uv").reshape(KV_LORA, HEADS, V_DIM)
    grads["w_ukv"] = jnp.concatenate([g_uk_p, g_uv], axis=2).reshape(KV_LORA, -1)
    dxr, gwa, gwi, gvec = [], [], [], []
    for d, hd in enumerate((h_f, h_b)):
        r = _lru_scan_bwd(xr, hd, dh, cw[d], pv[d], wa[d], wi[d], d == 1, t_scan)
        dxr.append(r[0])
        gwa.append(_block_diag_t(r[1]))
        gwi.append(_block_diag_t(r[2]))
        gvec.append(r[3])
    grads["lru_w_a"], grads["lru_w_i"] = jnp.stack(gwa), jnp.stack(gwi)
    grads["lru_conv_w"] = jnp.stack([gv[:CONV_W] for gv in gvec])
    for r_i, name in ((4, "lru_conv_b"), (5, "lru_b_a"), (6, "lru_b_i"), (7, "lru_lambda")):
        grads[name] = jnp.stack([gv[r_i] for gv in gvec])
    grad_x, dproj, gg = _in_proj_bwd(x2d, dx1, dxr[0], dxr[1], dyg, dcq, dckv, dkrp, g_attn, w_in_p, tm)
    grads["attn_norm"] = gg[0]
    g_in_p = _matmul_tn(hb_in, dproj, "grad_w_in")
    grads["w_in"] = jnp.concatenate([g_in_p[:, :OFF_KR], g_in_p[:, OFF_KR + QK_NOPE:OFF_KR + QK_HEAD]], axis=1)

    small = _pack_small_grads(grads)
    length = small.shape[1]
    hrows = _round_up(-(-length // (2 * LANES)), 16)
    small = jnp.pad(small, ((0, 0), (0, 2 * hrows * LANES - length))).reshape(N_CHIPS, 2, hrows, LANES)
    parts = []
    for name, shape, axis in BIG:
        g4 = grads[name] if grads[name].ndim == 3 else _grad_shards(grads[name], shape, axis)
        parts.append(g4.reshape(N_CHIPS, 2, g4.shape[1] // 2, g4.shape[2]))
    parts.append(small)
    names = [name for name, _, _ in BIG] + ["small"]
    c_idx = lax.axis_index("c").astype(jnp.int32).reshape(1)
    sibling = _swap_halves(parts)
    chip_sums = [_add_halves(g, b, c_idx, F32 if n == "small" else BF16, "grad_add_halves_" + n)
                 for n, g, b in zip(names, parts, sibling)]
    arrived = _scatter_chips(chip_sums)
    reduced = _join_halves([_sum_chips(b, c_idx, "grad_sum_chips_" + n) for n, b in zip(names, arrived)])

    outs = [{}, {}, {}, {}]
    for (name, shape, axis), r in zip(BIG, reduced):
        g2 = r.reshape(_shard_shape(shape, axis))
        res = _adamw(local[name], g2, given["m_" + name][0], given["v_" + name][0], "adamw_" + name)
        for o_, a in zip(outs, (g2, *res)):
            o_[name] = a[None]
    pack = lambda prefix: _pad_rows(_pack_small_local({n: given[prefix + n] for n in WEIGHT_ORDER}), 2 * hrows)
    g_small = reduced[-1].reshape(2 * hrows, LANES)
    res = _adamw(pack(""), g_small, pack("m_"), pack("v_"), "adamw_small")
    for o_, a in zip(outs, (g_small, *res)):
        o_.update(_unpack_small_local(a.reshape(-1)))
    return (loss, grad_x[None], *[o_[n] for o_ in outs for n in WEIGHT_ORDER])
```

```python
import functools

import jax
import jax.numpy as jnp
from jax import lax
from jax.experimental import pallas as pl
from jax.experimental.pallas import tpu as pltpu

F32, BF16 = jnp.float32, jnp.bfloat16
MESH = pl.DeviceIdType.MESH

D_MODEL = 1024
EPS = 1e-6
LRU_W = 512
LRU_BLOCKS = 8
LRU_C = 8.0
CONV_W = 4
HEADS = 8
QK_NOPE, QK_ROPE, QK_HEAD, V_DIM = 64, 32, 96, 64
Q_LORA, KV_LORA = 256, 128
MLA_W = HEADS * V_DIM
ROPE_THETA = 10000.0
IN_COLS = 2 * LRU_W + Q_LORA + KV_LORA + QK_ROPE
OFF_KR = IN_COLS - QK_ROPE
IN_PAD = 1536
MEM_HEADS, MEM_HD = 4, 128
MEM_W = MEM_HEADS * MEM_HD
D_FF = 2816
N_CHIPS = 4
ADAM_LR, ADAM_B1, ADAM_B2, ADAM_EPS, ADAM_WD, ADAM_STEP = 0.001, 0.9, 0.999, 1e-08, 0.01, 10

LANES = 128
SUBLANES = 8
VMEM_LIMIT = 56 * 1024 * 1024
PACK_ROWS = 2048

SHARDED = (
    ("w_in", (D_MODEL, IN_COLS), 1, True),
    ("lru_conv_w", (2, CONV_W, LRU_W), 2, False),
    ("lru_conv_b", (2, LRU_W), 1, False),
    ("lru_b_a", (2, LRU_W), 1, False),
    ("lru_b_i", (2, LRU_W), 1, False),
    ("lru_lambda", (2, LRU_W), 1, False),
    ("w_uq", (Q_LORA, HEADS * QK_HEAD), 1, True),
    ("w_ukv", (KV_LORA, HEADS * (QK_NOPE + V_DIM)), 1, True),
    ("w_out", (2 * LRU_W, D_MODEL), 0, True),
    ("w_mem_q", (D_MODEL, MEM_W), 0, True),
    ("w_mem_kv", (D_MODEL, 2 * MEM_W), 0, True),
    ("w_mem_o", (MEM_W, D_MODEL), 1, True),
    ("w_up", (D_MODEL, 2 * D_FF), 1, True),
    ("ffn_conv_w", (3, 2 * D_FF), 1, False),
    ("w_down", (D_FF, D_MODEL), 0, True),
)
REPLICATED = (
    ("attn_norm", (D_MODEL,)), ("lru_w_a", (2, LRU_BLOCKS, 64, 64)), ("lru_w_i", (2, LRU_BLOCKS, 64, 64)),
    ("q_a_norm", (Q_LORA,)), ("kv_a_norm", (KV_LORA,)), ("mla_q_norm", (QK_HEAD,)), ("mla_k_norm", (QK_HEAD,)),
    ("lru_out_norm", (LRU_W,)), ("mla_out_norm", (MLA_W,)), ("mem_attn_norm", (D_MODEL,)), ("mem_norm", (D_MODEL,)),
    ("mem_q_norm", (MEM_HD,)), ("mem_k_norm", (MEM_HD,)), ("ffn_norm", (D_MODEL,)), ("ffn_conv_b", (2 * D_FF,)),
)
WEIGHT_ORDER = ('attn_norm', 'w_in', 'lru_conv_w', 'lru_conv_b', 'lru_w_a', 'lru_b_a', 'lru_w_i', 'lru_b_i', 'lru_lambda',
                'q_a_norm', 'w_uq', 'kv_a_norm', 'w_ukv', 'mla_q_norm', 'mla_k_norm', 'lru_out_norm', 'mla_out_norm', 'w_out',
                'mem_attn_norm', 'mem_norm', 'w_mem_q', 'w_mem_kv', 'mem_q_norm', 'mem_k_norm', 'w_mem_o', 'ffn_norm', 'w_up',
                'ffn_conv_w', 'ffn_conv_b', 'w_down')


def _numel(shape):
    n = 1
    for s in shape:
        n *= s
    return n


def _cparams(n_axes):
    return pltpu.CompilerParams(dimension_semantics=("arbitrary",) * n_axes, vmem_limit_bytes=VMEM_LIMIT)


def _bdot(a, b):
    return jnp.dot(a.astype(BF16), b.astype(BF16), preferred_element_type=F32)


def _bdot_nt(a, b):
    return lax.dot_general(a.astype(BF16), b.astype(BF16), (((1,), (1,)), ((), ())), preferred_element_type=F32)


def _bdot_tn(a, b):
    return lax.dot_general(a.astype(BF16), b.astype(BF16), (((0,), (0,)), ((), ())), preferred_element_type=F32)


def _rstd(x, n=None):
    n = x.shape[-1] if n is None else n
    return lax.rsqrt(jnp.sum(x * x, axis=-1, keepdims=True) * (1.0 / n) + EPS)


def _norm_bwd(x, rs, g, dy, n=None):
    n = x.shape[-1] if n is None else n
    xhat = x * rs
    dxh = dy * g
    dx = rs * (dxh - xhat * (jnp.sum(dxh * xhat, axis=-1, keepdims=True) * (1.0 / n)))
    return dx, dy * xhat


def _acc_row(ref, r, val):
    ref[r:r + 1, :] += jnp.sum(val, axis=0, keepdims=True)


def _zero_first(i, *refs):
    @pl.when(i == 0)
    def _():
        for r in refs:
            r[...] = jnp.zeros_like(r)


def _shift_down(x, j, halo):
    if j == 0:
        return x
    xs = pltpu.roll(x, j, 0)
    hs = pltpu.roll(halo, j, 0)
    row = lax.broadcasted_iota(jnp.int32, hs.shape, 0)
    top = jnp.where(row < j, hs, xs[:SUBLANES])
    return jnp.concatenate([top, xs[SUBLANES:]], axis=0)


def _shift_up(x, j, halo):
    if j == 0:
        return x
    t = x.shape[0]
    xs = pltpu.roll(x, t - j, 0)
    hs = pltpu.roll(halo, SUBLANES - j, 0)
    row = lax.broadcasted_iota(jnp.int32, hs.shape, 0)
    bot = jnp.where(row >= SUBLANES - j, hs, xs[t - SUBLANES:])
    return jnp.concatenate([xs[:t - SUBLANES], bot], axis=0)


def _shift(x, j, halo, down):
    return _shift_down(x, j, halo) if down else _shift_up(x, j, halo)


def _scan(a, b, down):
    t = a.shape[0]
    row = lax.broadcasted_iota(jnp.int32, a.shape, 0)
    d = 1
    while d < t:
        if down:
            keep = row >= d
            a_s = jnp.where(keep, pltpu.roll(a, d, 0), 1.0)
            b_s = jnp.where(keep, pltpu.roll(b, d, 0), 0.0)
        else:
            keep = row < t - d
            a_s = jnp.where(keep, pltpu.roll(a, t - d, 0), 1.0)
            b_s = jnp.where(keep, pltpu.roll(b, t - d, 0), 0.0)
        b = a * b_s + b
        a = a * a_s
        d *= 2
    return a, b


def _sigmoid(x):
    return 0.5 * jnp.tanh(0.5 * x) + 0.5


LOG2E = 1.4426950408889634
GELU_K = 0.7978845608028654
GELU_C = 0.044715


def _gelu(x):
    return 0.5 * x * (1.0 + jnp.tanh(GELU_K * (x + GELU_C * x * x * x)))


def _gelu_grad(x):
    t = jnp.tanh(GELU_K * (x + GELU_C * x * x * x))
    return 0.5 * (1.0 + t) + 0.5 * x * (1.0 - t * t) * GELU_K * (1.0 + 3.0 * GELU_C * x * x)


def _rope_partner(x):
    lane = lax.broadcasted_iota(jnp.int32, x.shape, 1)
    half = QK_ROPE // 2
    sw = jnp.where(lane < QK_NOPE + half, pltpu.roll(x, LANES - half, 1), pltpu.roll(x, half, 1))
    return jnp.where((lane >= QK_NOPE) & (lane < QK_HEAD), sw, 0.0)


def _rope(x, cos_t, sin_t):
    return x * cos_t + _rope_partner(x) * sin_t


def _rope_t(dy, cos_t, sin_t):
    return dy * cos_t + _rope_partner(dy * sin_t)


def _rowwise(body, name, s, tm, rows=(), halos=(), fulls=(), outs=(), accs=()):
    n = s // tm
    hb = tm // SUBLANES
    last8 = s // SUBLANES - 1
    in_specs, args = [], []
    for a in rows:
        in_specs.append(pl.BlockSpec((tm, a.shape[1]), lambda i: (i, 0)))
        args.append(a)
    for a in halos:
        in_specs.append(pl.BlockSpec((SUBLANES, a.shape[1]), lambda i: (jnp.maximum(i * hb - 1, 0), 0)))
        in_specs.append(pl.BlockSpec((SUBLANES, a.shape[1]), lambda i: (jnp.minimum((i + 1) * hb, last8), 0)))
        args += [a, a]
    for a in fulls:
        in_specs.append(pl.BlockSpec(a.shape, lambda i, nd=a.ndim: (0,) * nd))
        args.append(a)
    out_shape, out_specs = [], []
    for c, dt in outs:
        out_shape.append(jax.ShapeDtypeStruct((s, c), dt))
        out_specs.append(pl.BlockSpec((tm, c), lambda i: (i, 0)))
    for shp, dt in accs:
        out_shape.append(jax.ShapeDtypeStruct(shp, dt))
        out_specs.append(pl.BlockSpec(shp, lambda i, nd=len(shp): (0,) * nd))

    def kern(*refs):
        body(pl.program_id(0), n, *refs)

    return pl.pallas_call(kern, grid=(n,), in_specs=in_specs, out_specs=out_specs, out_shape=out_shape, name=name,
                          compiler_params=_cparams(1))(*args)


def _matmul_tn(a, b, name, col_shards=False):
    t, m = a.shape
    n = b.shape[1]
    bm = m
    for cand in range(LANES, m + 1, LANES):
        if m % cand == 0 and cand * (n // N_CHIPS if col_shards else min(n, 2048)) * 4 <= 6 * 1024 * 1024:
            bm = cand
    bn = n // N_CHIPS if col_shards else (n if n <= 2048 else 1408)
    bt = min(512, t)
    nt = t // bt

    def kern(a_ref, b_ref, o_ref):
        @pl.when(pl.program_id(2) == 0)
        def _():
            o_ref[...] = jnp.zeros_like(o_ref)
        o_ref[...] += _bdot_tn(a_ref[...], b_ref[...])

    if col_shards:
        out_spec = pl.BlockSpec((None, bm, bn), lambda i, j, k: (j, i, 0))
        out_shape = jax.ShapeDtypeStruct((N_CHIPS, m, bn), F32)
    else:
        out_spec = pl.BlockSpec((bm, bn), lambda i, j, k: (i, j))
        out_shape = jax.ShapeDtypeStruct((m, n), F32)
    return pl.pallas_call(
        kern, grid=(m // bm, n // bn, nt),
        in_specs=[pl.BlockSpec((bt, bm), lambda i, j, k: (k, i)), pl.BlockSpec((bt, bn), lambda i, j, k: (k, j))],
        out_specs=out_spec, out_shape=out_shape, name=name, compiler_params=_cparams(3))(a, b)


def _in_proj(x, g, w_in_p, tm):
    def body(i, n, x_ref, g_ref, w_ref, xr, yg, cq, ckv, krp, hb):
        xv = x_ref[...]
        h = (xv * _rstd(xv) * g_ref[...]).astype(BF16)
        hb[...] = h
        p = jnp.dot(h, w_ref[...], preferred_element_type=F32)
        xr[...] = p[:, :LRU_W]
        yg[...] = p[:, LRU_W:2 * LRU_W]
        cq[...] = p[:, 2 * LRU_W:2 * LRU_W + Q_LORA]
        ckv[...] = p[:, 2 * LRU_W + Q_LORA:OFF_KR]
        krp[...] = p[:, OFF_KR:IN_PAD]

    return _rowwise(body, "in_proj", x.shape[0], tm, rows=[x], fulls=[g, w_in_p],
                    outs=[(LRU_W, F32), (LRU_W, F32), (Q_LORA, F32), (KV_LORA, F32), (LANES, F32), (D_MODEL, BF16)])


def _lru_gates(x, halo, cw_ref, pv_ref, wa_ref, wi_ref, rev):
    down = not rev
    xc = pv_ref[0:1, :] + jnp.zeros_like(x)
    for j in range(CONV_W):
        k = j if rev else CONV_W - 1 - j
        xc = xc + cw_ref[k:k + 1, :] * _shift(x, j, halo, down)
    r = _sigmoid(_bdot(xc, wa_ref[...]) + pv_ref[1:2, :])
    ig = _sigmoid(_bdot(xc, wi_ref[...]) + pv_ref[2:3, :])
    lam = pv_ref[3:4, :]
    sp = jnp.maximum(-lam, 0.0) + jnp.log(1.0 + jnp.exp(-jnp.abs(lam)))
    log_a = (-LRU_C) * r * sp
    a = jnp.exp(log_a)
    z = 2.0 * log_a
    series = -(z * (1.0 + z * (0.5 + z * (1.0 / 6.0 + z * (1.0 / 24.0)))))
    om = jnp.where(z > -0.02, series, 1.0 - jnp.exp(z))
    mult = jnp.sqrt(om)
    return xc, r, ig, sp, a, mult


def _lru_scan_fwd(xr, cw, pv, wa, wi, rev, t):
    s = xr.shape[0]
    n = s // t
    hb = t // SUBLANES
    last8 = s // SUBLANES - 1
    down = not rev

    def kern(x_ref, halo_ref, cw_ref, pv_ref, wa_ref, wi_ref, h_ref, carry_ref):
        i = pl.program_id(0)
        _zero_first(i, carry_ref)
        halo = jnp.where(i == 0, 0.0, halo_ref[...])
        xc, r, ig, sp, a, mult = _lru_gates(x_ref[...], halo, cw_ref, pv_ref, wa_ref, wi_ref, rev)
        aa, bb = _scan(a, mult * ig * xc, down)
        h_ref[...] = aa * carry_ref[...] + bb
        carry_ref[...] = h_ref[pl.ds(t - 1 if down else 0, 1), :]

    if rev:
        blk = lambda i: (n - 1 - i, 0)
        hal = lambda i: (jnp.minimum((n - i) * hb, last8), 0)
    else:
        blk = lambda i: (i, 0)
        hal = lambda i: (jnp.maximum(i * hb - 1, 0), 0)
    full = lambda a: pl.BlockSpec(a.shape, lambda i: (0, 0))
    return pl.pallas_call(
        kern, grid=(n,),
        in_specs=[pl.BlockSpec((t, LRU_W), blk), pl.BlockSpec((SUBLANES, LRU_W), hal), full(cw), full(pv), full(wa), full(wi)],
        out_specs=pl.BlockSpec((t, LRU_W), blk), out_shape=jax.ShapeDtypeStruct((s, LRU_W), F32),
        scratch_shapes=[pltpu.VMEM((1, LRU_W), F32)], name="lru_scan_rev" if rev else "lru_scan_fwd",
        compiler_params=_cparams(1))(xr, xr, cw, pv, wa, wi)


def _lru_scan_bwd(xr, h, dh, cw, pv, wa, wi, rev, t):
    s = xr.shape[0]
    n = s // t
    hb = t // SUBLANES
    last8 = s // SUBLANES - 1
    down = not rev

    def kern(x_ref, xh_ref, h_ref, hh_ref, dh_ref, cw_ref, pv_ref, wa_ref, wi_ref,
             dx_ref, gwa_ref, gwi_ref, gv_ref, p_ref, dxc_halo_ref, tmp_ref):
        i = pl.program_id(0)
        _zero_first(i, gwa_ref, gwi_ref, gv_ref, p_ref, dxc_halo_ref)
        at_start = i == n - 1
        x = x_ref[...]
        xhalo = jnp.where(at_start, 0.0, xh_ref[...])
        hhalo = jnp.where(at_start, 0.0, hh_ref[...])
        xc, r, ig, sp, a, mult = _lru_gates(x, xhalo, cw_ref, pv_ref, wa_ref, wi_ref, rev)
        h_prev = _shift(h_ref[...], 1, hhalo, down)
        row = lax.broadcasted_iota(jnp.int32, x.shape, 0)
        edge = t - 1 if down else 0
        dh_mod = dh_ref[...] + jnp.where(row == edge, p_ref[...], 0.0)
        a_next = _shift(a, 1, jnp.zeros((SUBLANES, LRU_W), F32), not down)
        _, g = _scan(a_next, dh_mod, not down)
        tmp_ref[...] = a * g
        p_ref[...] = tmp_ref[pl.ds(0 if down else t - 1, 1), :]
        da = g * h_prev
        d_ig = g * mult * xc
        d_xc = g * mult * ig
        d_om = g * ig * xc * (0.5 / jnp.maximum(mult, 1e-30))
        d_log_a = da * a - 2.0 * d_om * a * a
        d_r = d_log_a * ((-LRU_C) * sp)
        d_sp = jnp.sum(d_log_a * ((-LRU_C) * r), axis=0, keepdims=True)
        lam = pv_ref[3:4, :]
        gv_ref[7:8, :] += d_sp * (-_sigmoid(-lam))
        d_ga = d_r * r * (1.0 - r)
        d_gi = d_ig * ig * (1.0 - ig)
        _acc_row(gv_ref, 5, d_ga)
        _acc_row(gv_ref, 6, d_gi)
        d_xc = d_xc + _bdot_nt(d_ga, wa_ref[...]) + _bdot_nt(d_gi, wi_ref[...])
        gwa_ref[...] += _bdot_tn(xc, d_ga)
        gwi_ref[...] += _bdot_tn(xc, d_gi)
        _acc_row(gv_ref, 4, d_xc)
        dx = jnp.zeros_like(x)
        dxc_halo = dxc_halo_ref[...]
        for j in range(CONV_W):
            k = j if rev else CONV_W - 1 - j
            _acc_row(gv_ref, k, d_xc * _shift(x, j, xhalo, down))
            dx = dx + cw_ref[k:k + 1, :] * _shift(d_xc, j, dxc_halo, not down)
        dx_ref[...] = dx
        dxc_halo_ref[...] = d_xc[:SUBLANES] if down else d_xc[t - SUBLANES:]

    if rev:
        blk = lambda i: (i, 0)
        hal = lambda i: (jnp.minimum((i + 1) * hb, last8), 0)
    else:
        blk = lambda i: (n - 1 - i, 0)
        hal = lambda i: (jnp.maximum((n - 1 - i) * hb - 1, 0), 0)
    full = lambda a: pl.BlockSpec(a.shape, lambda i: (0, 0))
    bs = pl.BlockSpec((t, LRU_W), blk)
    hs = pl.BlockSpec((SUBLANES, LRU_W), hal)
    return pl.pallas_call(
        kern, grid=(n,),
        in_specs=[bs, hs, bs, hs, bs, full(cw), full(pv), full(wa), full(wi)],
        out_specs=[bs, pl.BlockSpec((LRU_W, LRU_W), lambda i: (0, 0)), pl.BlockSpec((LRU_W, LRU_W), lambda i: (0, 0)),
                   pl.BlockSpec((SUBLANES, LRU_W), lambda i: (0, 0))],
        out_shape=[jax.ShapeDtypeStruct((s, LRU_W), F32), jax.ShapeDtypeStruct((LRU_W, LRU_W), F32),
                   jax.ShapeDtypeStruct((LRU_W, LRU_W), F32), jax.ShapeDtypeStruct((SUBLANES, LRU_W), F32)],
        scratch_shapes=[pltpu.VMEM((1, LRU_W), F32), pltpu.VMEM((SUBLANES, LRU_W), F32), pltpu.VMEM((t, LRU_W), F32)],
        name="lru_bwd_rev" if rev else "lru_bwd_fwd", compiler_params=_cparams(1))(xr, xr, h, h, dh, cw, pv, wa, wi)


def _mla_qkv(cq, ckv, krp, cos_t, sin_t, g_qa, g_kva, g_qn, g_kn, w_uq_p, w_uk_p, w_uv, tm):
    scale = QK_HEAD ** -0.5 * LOG2E

    def body(i, n, cq_ref, ckv_ref, kr_ref, c_ref, s_ref, gqa, gkva, gqn, gkn, wq, wk, wv, q_out, k_out, v_out):
        cosv, sinv = c_ref[...], s_ref[...]
        cqv = cq_ref[...]
        qr = _bdot(cqv * _rstd(cqv) * gqa[...], wq[...])
        ckvv = ckv_ref[...]
        c_kv = (ckvv * _rstd(ckvv) * gkva[...]).astype(BF16)
        kn = jnp.dot(c_kv, wk[...], preferred_element_type=F32)
        v_out[...] = jnp.dot(c_kv, wv[...], preferred_element_type=F32).astype(BF16)
        kr = kr_ref[...]
        for h in range(HEADS):
            sl = slice(h * LANES, (h + 1) * LANES)
            qh = qr[:, sl]
            qh = _rope(qh * _rstd(qh, QK_HEAD) * gqn[...], cosv, sinv) * scale
            q_out[:, sl] = qh.astype(BF16)
            kh = kn[:, sl] + kr
            kh = _rope(kh * _rstd(kh, QK_HEAD) * gkn[...], cosv, sinv)
            k_out[:, sl] = kh.astype(BF16)

    return _rowwise(body, "mla_qkv", cq.shape[0], tm, rows=[cq, ckv, krp, cos_t, sin_t],
                    fulls=[g_qa, g_kva, g_qn, g_kn, w_uq_p, w_uk_p, w_uv],
                    outs=[(HEADS * LANES, BF16), (HEADS * LANES, BF16), (MLA_W, BF16)])


NT_DIMS = (((1,), (1,)), ((), ()))


def _attn_fwd(q, k, vt, tq, tk):
    s = q.shape[0]
    nq, nk = s // tq, s // tk

    def kern(q_ref, k_ref, vt_ref, o_ref, lse_ref, acc_ref):
        qs = (q_ref[:, :LANES], q_ref[:, LANES:])
        acc_ref[...] = jnp.zeros_like(acc_ref)

        def step(j, carry):
            off = pl.multiple_of(j * tk, tk)
            vtc = vt_ref[:, pl.ds(off, tk)]
            out = []
            for h in range(2):
                m, l = carry[2 * h:2 * h + 2]
                st = lax.dot_general(k_ref[pl.ds(off, tk), h * LANES:(h + 1) * LANES], qs[h], NT_DIMS,
                                     preferred_element_type=F32)
                mn = jnp.maximum(m, jnp.max(st, axis=0, keepdims=True))
                al = jnp.exp2(m - mn)
                pt = jnp.exp2(st - mn)
                l = al * l + jnp.sum(pt, axis=0, keepdims=True)
                acc_ref[h] = al * acc_ref[h] + jnp.dot(vtc, pt.astype(BF16), preferred_element_type=F32)
                out += [mn, l]
            return tuple(out)

        init = (jnp.full((1, tq), -1e30, F32), jnp.zeros((1, tq), F32)) * 2
        m0, l0, m1, l1 = lax.fori_loop(0, nk, step, init)
        row = lax.broadcasted_iota(jnp.int32, (LANES, tq), 0)
        o_ref[...] = jnp.where(row < V_DIM, acc_ref[0] / l0, acc_ref[1] / l1).T
        lse_ref[0, 0:1, :] = m0 + jnp.log2(l0)
        lse_ref[0, 1:2, :] = m1 + jnp.log2(l1)

    return pl.pallas_call(
        kern, grid=(HEADS // 2, nq),
        in_specs=[pl.BlockSpec((tq, 2 * LANES), lambda p, i: (i, p)), pl.BlockSpec((s, 2 * LANES), lambda p, i: (0, p)),
                  pl.BlockSpec((LANES, s), lambda p, i: (p, 0))],
        out_specs=[pl.BlockSpec((tq, LANES), lambda p, i: (i, p)), pl.BlockSpec((1, 2, tq), lambda p, i: (p, 0, i))],
        out_shape=[jax.ShapeDtypeStruct((s, MLA_W), F32), jax.ShapeDtypeStruct((HEADS // 2, 2, s), F32)],
        scratch_shapes=[pltpu.VMEM((2, LANES, tq), F32)], name="attn_fwd", compiler_params=_cparams(2))(q, k, vt)


def _attn_bwd(q, k, kt, v, do, lse, delta, tq, tk):
    s = q.shape[0]
    nq, nk = s // tq, s // tk

    def kern(q_ref, do_ref, lse_ref, dl_ref, k_ref, kt_ref, v_ref, dqt_ref, dk_ref, dv_ref, acc_ref):
        _zero_first(pl.program_id(1), dk_ref, dv_ref)
        acc_ref[...] = jnp.zeros_like(acc_ref)
        qs = (q_ref[:, :LANES], q_ref[:, LANES:])
        doc = do_ref[...]
        lane_q = lax.broadcasted_iota(jnp.int32, (tq, LANES), 1)
        zq = jnp.zeros_like(doc)
        dos = (jnp.where(lane_q < V_DIM, doc, zq), jnp.where(lane_q >= V_DIM, doc, zq))
        lses = (lse_ref[0, 0:1, :], lse_ref[0, 1:2, :])
        dls = (dl_ref[0, 0:1, :], dl_ref[0, 1:2, :])

        def step(j, carry):
            off = pl.multiple_of(j * tk, tk)
            vp = v_ref[pl.ds(off, tk), :]
            lane_k = lax.broadcasted_iota(jnp.int32, (tk, LANES), 1)
            zero = jnp.zeros_like(vp)
            vs = (jnp.where(lane_k < V_DIM, vp, zero), jnp.where(lane_k >= V_DIM, vp, zero))
            for h in range(2):
                sl = slice(h * LANES, (h + 1) * LANES)
                st = lax.dot_general(k_ref[pl.ds(off, tk), sl], qs[h], NT_DIMS, preferred_element_type=F32)
                pt = jnp.exp2(st - lses[h])
                dpt = lax.dot_general(vs[h], doc, NT_DIMS, preferred_element_type=F32)
                dst = (pt * (dpt - dls[h])).astype(BF16)
                dv_ref[pl.ds(off, tk), :] += jnp.dot(pt.astype(BF16), dos[h], preferred_element_type=F32)
                dk_ref[pl.ds(off, tk), sl] += jnp.dot(dst, qs[h], preferred_element_type=F32)
                acc_ref[h] += jnp.dot(kt_ref[sl, pl.ds(off, tk)], dst, preferred_element_type=F32)
            return carry

        lax.fori_loop(0, nk, step, 0)
        dqt_ref[:LANES, :] = acc_ref[0]
        dqt_ref[LANES:, :] = acc_ref[1]

    return pl.pallas_call(
        kern, grid=(HEADS // 2, nq),
        in_specs=[pl.BlockSpec((tq, 2 * LANES), lambda p, i: (i, p)), pl.BlockSpec((tq, LANES), lambda p, i: (i, p)),
                  pl.BlockSpec((1, 2, tq), lambda p, i: (p, 0, i)), pl.BlockSpec((1, 2, tq), lambda p, i: (p, 0, i)),
                  pl.BlockSpec((s, 2 * LANES), lambda p, i: (0, p)), pl.BlockSpec((2 * LANES, s), lambda p, i: (p, 0)),
                  pl.BlockSpec((s, LANES), lambda p, i: (0, p))],
        out_specs=[pl.BlockSpec((2 * LANES, tq), lambda p, i: (p, i)), pl.BlockSpec((s, 2 * LANES), lambda p, i: (0, p)),
                   pl.BlockSpec((s, LANES), lambda p, i: (0, p))],
        out_shape=[jax.ShapeDtypeStruct((HEADS * LANES, s), F32), jax.ShapeDtypeStruct((s, HEADS * LANES), F32),
                   jax.ShapeDtypeStruct((s, MLA_W), F32)],
        scratch_shapes=[pltpu.VMEM((2, LANES, tq), F32)], name="attn_bwd", compiler_params=_cparams(2))(
            q, do, lse, delta, k, kt, v)


def _mix_out(hf, hb, yg, o, x, g_lru, g_mla, w_out, tm):
    def body(i, n, hf_ref, hb_ref, yg_ref, o_ref, x_ref, gl, gm, w_ref, x1_ref, mix_ref):
        lo = (hf_ref[...] + hb_ref[...]) * _gelu(yg_ref[...])
        ov = o_ref[...]
        mix_ref[:, :LRU_W] = (lo * _rstd(lo) * gl[...]).astype(BF16)
        mix_ref[:, LRU_W:] = (ov * _rstd(ov) * gm[...]).astype(BF16)
        x1_ref[...] = x_ref[...] + jnp.dot(mix_ref[...], w_ref[...], preferred_element_type=F32)

    return _rowwise(body, "mix_out", x.shape[0], tm, rows=[hf, hb, yg, o, x], fulls=[g_lru, g_mla, w_out],
                    outs=[(D_MODEL, F32), (2 * LRU_W, BF16)])


def _mem_kv(mem, g_mem, w_kv, g_k):
    m = mem.shape[0]

    def body(i, n, mem_ref, g_ref, w_ref, gk_ref, km_ref, vm_ref):
        mv = mem_ref[...]
        kv = _bdot(mv * _rstd(mv) * g_ref[...], w_ref[...])
        vm_ref[...] = kv[:, MEM_W:].astype(BF16)
        for h in range(MEM_HEADS):
            sl = slice(h * MEM_HD, (h + 1) * MEM_HD)
            kh = kv[:, sl]
            km_ref[:, sl] = (kh * _rstd(kh) * gk_ref[...]).astype(BF16)

    return _rowwise(body, "mem_kv", m, m, rows=[mem], fulls=[g_mem, w_kv, g_k], outs=[(MEM_W, BF16), (MEM_W, BF16)])


def _mem_attn_core(x1v, g_ref, wq_ref, gq_ref, km_ref, vm_ref):
    scale = MEM_HD ** -0.5
    hm = (x1v * _rstd(x1v) * g_ref[...]).astype(BF16)
    qr = jnp.dot(hm, wq_ref[...], preferred_element_type=F32)
    heads = []
    for h in range(MEM_HEADS):
        sl = slice(h * MEM_HD, (h + 1) * MEM_HD)
        qh = qr[:, sl]
        rs = _rstd(qh)
        qn = (qh * rs * gq_ref[...]).astype(BF16)
        sc = lax.dot_general(qn, km_ref[:, sl], (((1,), (1,)), ((), ())), preferred_element_type=F32) * scale
        e = jnp.exp(sc - jnp.max(sc, axis=-1, keepdims=True))
        p = e / jnp.sum(e, axis=-1, keepdims=True)
        oh = jnp.dot(p.astype(BF16), vm_ref[:, sl], preferred_element_type=F32)
        heads.append((qh, rs, qn, p, oh))
    return hm, heads


def _mem_attn(x1, g, w_q, g_q, km, vm, w_o, tm):
    cs = D_MODEL // N_CHIPS

    def body(i, n, x1_ref, g_ref, wq_ref, gq_ref, km_ref, vm_ref, wo_ref, x2_ref, ob_ref):
        x1v = x1_ref[...]
        _, heads = _mem_attn_core(x1v, g_ref, wq_ref, gq_ref, km_ref, vm_ref)
        for h in range(MEM_HEADS):
            ob_ref[:, h * MEM_HD:(h + 1) * MEM_HD] = heads[h][4].astype(BF16)
        for k in range(N_CHIPS):
            sl = slice(k * cs, (k + 1) * cs)
            x2_ref[:, sl] = x1v[:, sl] + jnp.dot(ob_ref[...], wo_ref[k], preferred_element_type=F32)

    return _rowwise(body, "mem_attn", x1.shape[0], tm, rows=[x1], fulls=[g, w_q, g_q, km, vm, w_o],
                    outs=[(D_MODEL, F32), (MEM_W, BF16)])


def _ffn_up(x2, g, w_up, tm):
    cs = 2 * D_FF // N_CHIPS

    def body(i, n, x_ref, g_ref, w_ref, gu_ref, hb_ref):
        xv = x_ref[...]
        hb_ref[...] = (xv * _rstd(xv) * g_ref[...]).astype(BF16)
        for k in range(N_CHIPS):
            gu_ref[:, k * cs:(k + 1) * cs] = jnp.dot(hb_ref[...], w_ref[k], preferred_element_type=F32)

    return _rowwise(body, "ffn_up", x2.shape[0], tm, rows=[x2], fulls=[g, w_up], outs=[(2 * D_FF, F32), (D_MODEL, BF16)])


def _ffn_conv(gu, prev, nxt, cw_ref, i, n):
    prev = jnp.where(i == 0, 0.0, prev)
    nxt = jnp.where(i == n - 1, 0.0, nxt)
    return (cw_ref[3:4, :] + cw_ref[0:1, :] * _shift_down(gu, 1, prev) + cw_ref[1:2, :] * gu
            + cw_ref[2:3, :] * _shift_up(gu, 1, nxt))


def _ffn_down_loss(gu_pre, x2, target, cw, w_down, tm):
    def body(i, n, gu_ref, x_ref, t_ref, pv_ref, nx_ref, cw_ref, w_ref, dy_ref, dyb_ref, act_ref, guc_ref, loss_ref):
        _zero_first(i, loss_ref)
        gu = _ffn_conv(gu_ref[...], pv_ref[...], nx_ref[...], cw_ref, i, n)
        guc_ref[...] = gu
        g, u = gu[:, :D_FF], gu[:, D_FF:]
        act_ref[...] = (g * _sigmoid(g) * u).astype(BF16)
        y = x_ref[...] + jnp.dot(act_ref[...], w_ref[...], preferred_element_type=F32)
        e = y - t_ref[...]
        loss_ref[...] += jnp.sum(e * e)
        dy = e * (1.0 / D_MODEL)
        dy_ref[...] = dy
        dyb_ref[...] = dy.astype(BF16)

    return _rowwise(body, "ffn_down_loss", x2.shape[0], tm, rows=[gu_pre, x2, target], halos=[gu_pre], fulls=[cw, w_down],
                    outs=[(D_MODEL, F32), (D_MODEL, BF16), (D_FF, BF16), (2 * D_FF, F32)], accs=[((SUBLANES, LANES), F32)])


def _ffn_bwd_act(dyb, gu, w_down, tm):
    def body(i, n, dy_ref, gu_ref, w_ref, dgu_ref):
        d_act = lax.dot_general(dy_ref[...], w_ref[...], (((1,), (1,)), ((), ())), preferred_element_type=F32)
        g, u = gu_ref[:, :D_FF], gu_ref[:, D_FF:]
        sg = _sigmoid(g)
        a = g * sg
        dgu_ref[:, :D_FF] = (d_act * u) * (sg + a - a * sg)
        dgu_ref[:, D_FF:] = d_act * a

    return _rowwise(body, "ffn_bwd_act", dyb.shape[0], tm, rows=[dyb, gu], fulls=[w_down], outs=[(2 * D_FF, F32)])


def _ffn_bwd_conv(dgu, gu_pre, cw, tm):
    def body(i, n, d_ref, g_ref, dp_ref, dn_ref, cw_ref, dpre_ref, gc_ref):
        _zero_first(i, gc_ref)
        d = d_ref[...]
        g = g_ref[...]
        d_next = _shift_up(d, 1, jnp.where(i == n - 1, 0.0, dn_ref[...]))
        d_prev = _shift_down(d, 1, jnp.where(i == 0, 0.0, dp_ref[...]))
        dpre_ref[...] = (cw_ref[0:1, :] * d_next + cw_ref[1:2, :] * d + cw_ref[2:3, :] * d_prev).astype(BF16)
        _acc_row(gc_ref, 0, d_next * g)
        _acc_row(gc_ref, 1, d * g)
        _acc_row(gc_ref, 2, d_prev * g)
        _acc_row(gc_ref, 3, d)

    return _rowwise(body, "ffn_bwd_conv", dgu.shape[0], tm, rows=[dgu, gu_pre], halos=[dgu], fulls=[cw],
                    outs=[(2 * D_FF, BF16)], accs=[((SUBLANES, 2 * D_FF), F32)])


def _ffn_bwd_in(dpre, x2, dy, g, w_up, tm):
    cs = 2 * D_FF // N_CHIPS

    def body(i, n, dp_ref, x_ref, dy_ref, g_ref, w_ref, dx_ref, dxb_ref, gg_ref):
        _zero_first(i, gg_ref)
        d_h = jnp.zeros(x_ref.shape, F32)
        for k in range(N_CHIPS):
            d_h = d_h + lax.dot_general(dp_ref[:, k * cs:(k + 1) * cs], w_ref[k], (((1,), (1,)), ((), ())),
                                        preferred_element_type=F32)
        xv = x_ref[...]
        dx, dg = _norm_bwd(xv, _rstd(xv), g_ref[...], d_h)
        _acc_row(gg_ref, 0, dg)
        dx = dx + dy_ref[...]
        dx_ref[...] = dx
        dxb_ref[...] = dx.astype(BF16)

    return _rowwise(body, "ffn_bwd_in", x2.shape[0], tm, rows=[dpre, x2, dy], fulls=[g, w_up],
                    outs=[(D_MODEL, F32), (D_MODEL, BF16)], accs=[((SUBLANES, D_MODEL), F32)])


def _mem_attn_bwd(x1, dx2, dx2b, g, w_q, g_q, km, vm, w_o, tm):
    scale = MEM_HD ** -0.5
    m = km.shape[0]

    def body(i, n, x1_ref, dx2_ref, dx2b_ref, g_ref, wq_ref, gq_ref, km_ref, vm_ref, wo_ref,
             dx1_ref, dx1b_ref, hm_ref, dqr_ref, dkm_ref, dvm_ref, gg_ref, ggq_ref):
        _zero_first(i, dkm_ref, dvm_ref, gg_ref, ggq_ref)
        x1v = x1_ref[...]
        hm, heads = _mem_attn_core(x1v, g_ref, wq_ref, gq_ref, km_ref, vm_ref)
        hm_ref[...] = hm
        cs = D_MODEL // N_CHIPS
        d_o = jnp.zeros((x1v.shape[0], MEM_W), F32)
        for k in range(N_CHIPS):
            d_o = d_o + lax.dot_general(dx2b_ref[:, k * cs:(k + 1) * cs], wo_ref[k], (((1,), (1,)), ((), ())),
                                        preferred_element_type=F32)
        for h in range(MEM_HEADS):
            sl = slice(h * MEM_HD, (h + 1) * MEM_HD)
            qh, rs, qn, p, _ = heads[h]
            d_oh = d_o[:, sl].astype(BF16)
            dp = lax.dot_general(d_oh, vm_ref[:, sl], (((1,), (1,)), ((), ())), preferred_element_type=F32)
            ds = (p * (dp - jnp.sum(dp * p, axis=-1, keepdims=True)) * scale).astype(BF16)
            dqn = jnp.dot(ds, km_ref[:, sl], preferred_element_type=F32)
            dkm_ref[:, sl] += lax.dot_general(ds, qn, (((0,), (0,)), ((), ())), preferred_element_type=F32)
            dvm_ref[:, sl] += lax.dot_general(p.astype(BF16), d_oh, (((0,), (0,)), ((), ())), preferred_element_type=F32)
            dqh, dgq = _norm_bwd(qh, rs, gq_ref[...], dqn)
            _acc_row(ggq_ref, 0, dgq)
            dqr_ref[:, sl] = dqh.astype(BF16)
        d_hm = lax.dot_general(dqr_ref[...], wq_ref[...], (((1,), (1,)), ((), ())), preferred_element_type=F32)
        dx, dg = _norm_bwd(x1v, _rstd(x1v), g_ref[...], d_hm)
        _acc_row(gg_ref, 0, dg)
        dx = dx + dx2_ref[...]
        dx1_ref[...] = dx
        dx1b_ref[...] = dx.astype(BF16)

    return _rowwise(body, "mem_attn_bwd", x1.shape[0], tm, rows=[x1, dx2, dx2b], fulls=[g, w_q, g_q, km, vm, w_o],
                    outs=[(D_MODEL, F32), (D_MODEL, BF16), (D_MODEL, BF16), (MEM_W, BF16)],
                    accs=[((m, MEM_W), F32), ((m, MEM_W), F32), ((SUBLANES, D_MODEL), F32), ((SUBLANES, MEM_HD), F32)])


def _mem_kv_bwd(mem, g_mem, w_kv, g_k, dkm, dvm):
    m = mem.shape[0]

    def body(i, n, mem_ref, dkm_ref, dvm_ref, g_ref, w_ref, gk_ref, gw_ref, gg_ref, ggk_ref, dkv_ref):
        gg_ref[...] = jnp.zeros_like(gg_ref)
        ggk_ref[...] = jnp.zeros_like(ggk_ref)
        mv = mem_ref[...]
        rs_m = _rstd(mv)
        mem_n = (mv * rs_m * g_ref[...]).astype(BF16)
        kv = jnp.dot(mem_n, w_ref[...], preferred_element_type=F32)
        for h in range(MEM_HEADS):
            sl = slice(h * MEM_HD, (h + 1) * MEM_HD)
            kh = kv[:, sl]
            dkh, dgk = _norm_bwd(kh, _rstd(kh), gk_ref[...], dkm_ref[:, sl])
            _acc_row(ggk_ref, 0, dgk)
            dkv_ref[:, sl] = dkh.astype(BF16)
        dkv_ref[:, MEM_W:] = dvm_ref[...].astype(BF16)
        gw_ref[...] = lax.dot_general(mem_n, dkv_ref[...], (((0,), (0,)), ((), ())), preferred_element_type=F32)
        d_mn = lax.dot_general(dkv_ref[...], w_ref[...], (((1,), (1,)), ((), ())), preferred_element_type=F32)
        _acc_row(gg_ref, 0, d_mn * (mv * rs_m))

    return _rowwise(body, "mem_kv_bwd", m, m, rows=[mem, dkm, dvm], fulls=[g_mem, w_kv, g_k],
                    accs=[((D_MODEL, 2 * MEM_W), F32), ((SUBLANES, D_MODEL), F32), ((SUBLANES, MEM_HD), F32),
                          ((m, 2 * MEM_W), BF16)])


def _mix_out_bwd(dx1b, hf, hb, yg, o, g_lru, g_mla, w_out, tm):
    def body(i, n, dx_ref, hf_ref, hb_ref, yg_ref, o_ref, gl, gm, w_ref, dh_ref, dyg_ref, dob_ref, dl_ref, ggl_ref, ggm_ref):
        _zero_first(i, ggl_ref, ggm_ref)
        dmix = lax.dot_general(dx_ref[...], w_ref[...], (((1,), (1,)), ((), ())), preferred_element_type=F32)
        hs = hf_ref[...] + hb_ref[...]
        ygv = yg_ref[...]
        ge = _gelu(ygv)
        lo = hs * ge
        d_lo, dgl = _norm_bwd(lo, _rstd(lo), gl[...], dmix[:, :LRU_W])
        _acc_row(ggl_ref, 0, dgl)
        dh_ref[...] = d_lo * ge
        dyg_ref[...] = d_lo * hs * _gelu_grad(ygv)
        ov = o_ref[...]
        d_o, dgm = _norm_bwd(ov, _rstd(ov), gm[...], dmix[:, LRU_W:])
        _acc_row(ggm_ref, 0, dgm)
        dob_ref[...] = d_o.astype(BF16)
        prod = d_o * ov
        lane_w = lax.broadcasted_iota(jnp.int32, prod.shape, 1)
        lane = lax.broadcasted_iota(jnp.int32, (prod.shape[0], LANES), 1)
        dl = jnp.zeros((prod.shape[0], LANES), F32)
        for h in range(HEADS):
            in_head = (lane_w >= h * V_DIM) & (lane_w < (h + 1) * V_DIM)
            dl = dl + jnp.where(lane == h, jnp.sum(jnp.where(in_head, prod, 0.0), axis=-1, keepdims=True), 0.0)
        dl_ref[...] = dl

    return _rowwise(body, "mix_out_bwd", dx1b.shape[0], tm, rows=[dx1b, hf, hb, yg, o], fulls=[g_lru, g_mla, w_out],
                    outs=[(LRU_W, F32), (LRU_W, F32), (MLA_W, BF16), (LANES, F32)],
                    accs=[((SUBLANES, LRU_W), F32), ((SUBLANES, MLA_W), F32)])


def _mla_qkv_bwd(cq, ckv, krp, cos_t, sin_t, dq, dk, dv, g_qa, g_kva, g_qn, g_kn, w_uq_p, w_uk_p, w_uv, tm):
    scale = QK_HEAD ** -0.5

    def body(i, n, cq_ref, ckv_ref, kr_ref, c_ref, s_ref, dq_ref, dk_ref, dv_ref, gqa, gkva, gqn, gkn, wq, wk, wv,
             dcq_ref, dckv_ref, dkr_ref, cqb_ref, dqr_ref, ckvb_ref, dkn_ref, dvb_ref, ggqa, ggkva, ggqn, ggkn):
        _zero_first(i, ggqa, ggkva, ggqn, ggkn)
        cosv, sinv = c_ref[...], s_ref[...]
        cqv = cq_ref[...]
        rs_q = _rstd(cqv)
        cqb_ref[...] = (cqv * rs_q * gqa[...]).astype(BF16)
        qr = jnp.dot(cqb_ref[...], wq[...], preferred_element_type=F32)
        ckvv = ckv_ref[...]
        rs_kv = _rstd(ckvv)
        ckvb_ref[...] = (ckvv * rs_kv * gkva[...]).astype(BF16)
        kn = jnp.dot(ckvb_ref[...], wk[...], preferred_element_type=F32)
        kr = kr_ref[...]
        dkr = jnp.zeros_like(kr)
        for h in range(HEADS):
            sl = slice(h * LANES, (h + 1) * LANES)
            qh = qr[:, sl]
            d_qn = _rope_t(dq_ref[:, sl] * scale, cosv, sinv)
            dqh, dgq = _norm_bwd(qh, _rstd(qh, QK_HEAD), gqn[...], d_qn, QK_HEAD)
            _acc_row(ggqn, 0, dgq)
            dqr_ref[:, sl] = dqh.astype(BF16)
            kh = kn[:, sl] + kr
            d_kn = _rope_t(dk_ref[:, sl] * (1.0 / LOG2E), cosv, sinv)
            dkh, dgk = _norm_bwd(kh, _rstd(kh, QK_HEAD), gkn[...], d_kn, QK_HEAD)
            _acc_row(ggkn, 0, dgk)
            dkn_ref[:, sl] = dkh.astype(BF16)
            dkr = dkr + dkh
        dkr_ref[...] = dkr
        dvb_ref[...] = dv_ref[...].astype(BF16)
        d_cq = lax.dot_general(dqr_ref[...], wq[...], (((1,), (1,)), ((), ())), preferred_element_type=F32)
        dcq, dg = _norm_bwd(cqv, rs_q, gqa[...], d_cq)
        _acc_row(ggqa, 0, dg)
        dcq_ref[...] = dcq
        d_ckv = (lax.dot_general(dkn_ref[...], wk[...], (((1,), (1,)), ((), ())), preferred_element_type=F32)
                 + lax.dot_general(dvb_ref[...], wv[...], (((1,), (1,)), ((), ())), preferred_element_type=F32))
        dckv, dg = _norm_bwd(ckvv, rs_kv, gkva[...], d_ckv)
        _acc_row(ggkva, 0, dg)
        dckv_ref[...] = dckv

    return _rowwise(body, "mla_qkv_bwd", cq.shape[0], tm, rows=[cq, ckv, krp, cos_t, sin_t, dq, dk, dv],
                    fulls=[g_qa, g_kva, g_qn, g_kn, w_uq_p, w_uk_p, w_uv],
                    outs=[(Q_LORA, F32), (KV_LORA, F32), (LANES, F32), (Q_LORA, BF16), (HEADS * LANES, BF16),
                          (KV_LORA, BF16), (HEADS * LANES, BF16), (MLA_W, BF16)],
                    accs=[((SUBLANES, Q_LORA), F32), ((SUBLANES, KV_LORA), F32), ((SUBLANES, LANES), F32),
                          ((SUBLANES, LANES), F32)])


def _in_proj_bwd(x, dx1, dxr_f, dxr_b, dyg, dcq, dckv, dkrp, g, w_in_p, tm):
    def body(i, n, x_ref, dx1_ref, df_ref, db_ref, dyg_ref, dcq_ref, dckv_ref, dkr_ref, g_ref, w_ref, gx_ref, dp_ref, gg_ref):
        _zero_first(i, gg_ref)
        dp_ref[:, :LRU_W] = (df_ref[...] + db_ref[...]).astype(BF16)
        dp_ref[:, LRU_W:2 * LRU_W] = dyg_ref[...].astype(BF16)
        dp_ref[:, 2 * LRU_W:2 * LRU_W + Q_LORA] = dcq_ref[...].astype(BF16)
        dp_ref[:, 2 * LRU_W + Q_LORA:OFF_KR] = dckv_ref[...].astype(BF16)
        dp_ref[:, OFF_KR:] = dkr_ref[...].astype(BF16)
        d_h = lax.dot_general(dp_ref[...], w_ref[...], (((1,), (1,)), ((), ())), preferred_element_type=F32)
        xv = x_ref[...]
        dx, dg = _norm_bwd(xv, _rstd(xv), g_ref[...], d_h)
        _acc_row(gg_ref, 0, dg)
        gx_ref[...] = dx + dx1_ref[...]

    return _rowwise(body, "in_proj_bwd", x.shape[0], tm, rows=[x, dx1, dxr_f, dxr_b, dyg, dcq, dckv, dkrp],
                    fulls=[g, w_in_p], outs=[(D_MODEL, F32), (IN_PAD, BF16)], accs=[((SUBLANES, D_MODEL), F32)])


ANY = pl.BlockSpec(memory_space=pl.ANY)


def _chip_peers(x, y):
    return ((1 - x, y), (x, 1 - y), (1 - x, 1 - y))


def _exchange_call(kern, name, ins, out_shapes, n_sems, aliases=None):
    return pl.pallas_call(
        kern, in_specs=[ANY] * len(ins), out_specs=[ANY] * len(out_shapes), out_shape=out_shapes,
        scratch_shapes=[pltpu.SemaphoreType.DMA((n,)) for n in n_sems], input_output_aliases=aliases or {},
        name=name)(*ins)


def _start_then_wait(copies):
    for cp in copies:
        cp.start()
    for cp in copies:
        cp.wait()


def _gather_chips(arrs):
    n = len(arrs)

    def kern(*refs):
        ins, outs, (ssem, rsem, lsem) = refs[:n], refs[n:2 * n], refs[2 * n:]
        x, y, c = lax.axis_index("x"), lax.axis_index("y"), lax.axis_index("c")
        me = 2 * x + y
        cps = []
        for i in range(n):
            cps.append(pltpu.make_async_copy(ins[i], outs[i].at[me], lsem.at[i]))
            for j, (px, py) in enumerate(_chip_peers(x, y)):
                cps.append(pltpu.make_async_remote_copy(ins[i], outs[i].at[me], ssem.at[3 * i + j], rsem.at[3 * i + j],
                                                        device_id=(px, py, c), device_id_type=MESH))
        _start_then_wait(cps)

    outs = [jax.ShapeDtypeStruct((N_CHIPS,) + a.shape, a.dtype) for a in arrs]
    return _exchange_call(kern, "gather_weights", arrs, outs, (3 * n, 3 * n, n))


def _swap_halves(gs):
    n = len(gs)

    def kern(*refs):
        ins, outs, (ssem, rsem) = refs[:n], refs[n:2 * n], refs[2 * n:]
        x, y, c = lax.axis_index("x"), lax.axis_index("y"), lax.axis_index("c")
        _start_then_wait([
            pltpu.make_async_remote_copy(ins[i].at[k, 1 - c], outs[i].at[k], ssem.at[N_CHIPS * i + k],
                                         rsem.at[N_CHIPS * i + k], device_id=(x, y, 1 - c), device_id_type=MESH)
            for i in range(n) for k in range(N_CHIPS)])

    outs = [jax.ShapeDtypeStruct((N_CHIPS,) + g.shape[2:], g.dtype) for g in gs]
    return _exchange_call(kern, "grad_swap_halves", gs, outs, (N_CHIPS * n, N_CHIPS * n))


def _scatter_chips(arrs):
    n = len(arrs)

    def kern(*refs):
        ins, outs, (ssem, rsem, lsem) = refs[:n], refs[n:2 * n], refs[2 * n:]
        x, y, c = lax.axis_index("x"), lax.axis_index("y"), lax.axis_index("c")
        me = 2 * x + y
        cps = []
        for i in range(n):
            cps.append(pltpu.make_async_copy(ins[i].at[me], outs[i].at[me], lsem.at[i]))
            for j, (px, py) in enumerate(_chip_peers(x, y)):
                cps.append(pltpu.make_async_remote_copy(ins[i].at[2 * px + py], outs[i].at[me], ssem.at[3 * i + j],
                                                        rsem.at[3 * i + j], device_id=(px, py, c), device_id_type=MESH))
        _start_then_wait(cps)

    outs = [jax.ShapeDtypeStruct(a.shape, a.dtype) for a in arrs]
    return _exchange_call(kern, "grad_scatter_chips", arrs, outs, (3 * n, 3 * n, n))


def _join_halves(arrs):
    n = len(arrs)

    def kern(*refs):
        outs, (ssem, rsem) = refs[n:2 * n], refs[2 * n:]
        x, y, c = lax.axis_index("x"), lax.axis_index("y"), lax.axis_index("c")
        _start_then_wait([
            pltpu.make_async_remote_copy(outs[i].at[c], outs[i].at[c], ssem.at[i], rsem.at[i],
                                         device_id=(x, y, 1 - c), device_id_type=MESH) for i in range(n)])

    outs = [jax.ShapeDtypeStruct(a.shape, a.dtype) for a in arrs]
    return _exchange_call(kern, "grad_join_halves", arrs, outs, (n, n), aliases={i: i for i in range(n)})


def _row_block(rows, row_bytes, limit=1 << 20):
    best = None
    for d in range(16, rows + 1, 16):
        if rows % d == 0 and d * row_bytes <= limit:
            best = d
    return best if best is not None else rows


def _add_halves(g, got, c, out_dtype, name):
    _, _, h, cols = g.shape
    hb = _row_block(h, cols * 4)

    def kern(c_ref, g_ref, b_ref, o_ref):
        o_ref[...] = (g_ref[...] + b_ref[...]).astype(out_dtype)

    blk = (None, hb, cols)
    return pl.pallas_call(
        kern,
        grid_spec=pltpu.PrefetchScalarGridSpec(
            num_scalar_prefetch=1, grid=(N_CHIPS, h // hb),
            in_specs=[pl.BlockSpec((None, None, hb, cols), lambda k, i, c_ref: (k, c_ref[0], i, 0)),
                      pl.BlockSpec(blk, lambda k, i, c_ref: (k, i, 0))],
            out_specs=pl.BlockSpec(blk, lambda k, i, c_ref: (k, i, 0))),
        out_shape=jax.ShapeDtypeStruct(got.shape, out_dtype), name=name, compiler_params=_cparams(2))(c, g, got)


def _sum_chips(b, c, name):
    _, h, cols = b.shape
    hb = _row_block(h, cols * 4)

    def kern(c_ref, b_ref, o_ref):
        f = lambda k: b_ref[k].astype(F32)
        o_ref[...] = ((f(0) + f(1)) + f(2)) + f(3)

    return pl.pallas_call(
        kern,
        grid_spec=pltpu.PrefetchScalarGridSpec(
            num_scalar_prefetch=1, grid=(h // hb,),
            in_specs=[pl.BlockSpec((N_CHIPS, hb, cols), lambda i, c_ref: (0, i, 0))],
            out_specs=pl.BlockSpec((None, hb, cols), lambda i, c_ref: (c_ref[0], i, 0))),
        out_shape=jax.ShapeDtypeStruct((2, h, cols), F32), name=name, compiler_params=_cparams(1))(c, b)


def _adamw(w, g, m, v, name):
    rows, cols = w.shape
    rb = _row_block(rows, cols * 4)
    c1 = 1.0 - ADAM_B1 ** ADAM_STEP
    c2 = 1.0 - ADAM_B2 ** ADAM_STEP

    def kern(w_ref, g_ref, m_ref, v_ref, d_ref, mo_ref, vo_ref):
        gv = g_ref[...]
        mn = ADAM_B1 * m_ref[...] + (1.0 - ADAM_B1) * gv
        vn = ADAM_B2 * v_ref[...] + (1.0 - ADAM_B2) * (gv * gv)
        mo_ref[...] = mn
        vo_ref[...] = vn
        d_ref[...] = (-ADAM_LR) * ((mn / c1) / (jnp.sqrt(vn / c2) + ADAM_EPS) + ADAM_WD * w_ref[...])

    spec = pl.BlockSpec((rb, cols), lambda i: (i, 0))
    return pl.pallas_call(
        kern, grid=(rows // rb,), in_specs=[spec] * 4, out_specs=[spec] * 3,
        out_shape=[jax.ShapeDtypeStruct(w.shape, F32)] * 3, name=name, compiler_params=_cparams(1))(w, g, m, v)


def _pad_rows(flat, rows):
    return jnp.pad(flat, (0, rows * LANES - flat.shape[0])).reshape(rows, LANES)


def _round_up(n, m):
    return (n + m - 1) // m * m


def _shard_shape(shape, axis):
    return tuple(s // N_CHIPS if a == axis else s for a, s in enumerate(shape))


def _to_shards(full, axis):
    shape = full.shape
    t = full.reshape(shape[:axis] + (N_CHIPS, shape[axis] // N_CHIPS) + shape[axis + 1:])
    return jnp.moveaxis(t, axis, 0).reshape(N_CHIPS, -1)


def _from_shards(sh, shape, axis):
    t = sh.reshape((N_CHIPS,) + _shard_shape(shape, axis))
    t = jnp.moveaxis(t, 0, axis)
    return t.reshape(shape)


BIG = tuple((name, shape, axis) for name, shape, axis, big in SHARDED if big)
SMALL_SHARDED = tuple((name, shape, axis) for name, shape, axis, big in SHARDED if not big)


def _pack_small_weights(p):
    flat = jnp.concatenate([p[name].reshape(-1) for name, _, _ in SMALL_SHARDED])
    return _pad_rows(flat, _round_up(-(-flat.shape[0] // LANES), SUBLANES))


def _unpack_small_weights(gathered):
    flat = gathered.reshape(N_CHIPS, -1)
    out, off = {}, 0
    for name, shape, axis in SMALL_SHARDED:
        n = _numel(shape) // N_CHIPS
        out[name] = _from_shards(flat[:, off:off + n], shape, axis)
        off += n
    return out


def _pack_small_local(p, prefix=""):
    parts = [p[prefix + name].reshape(-1) for name, _, _ in SMALL_SHARDED]
    parts += [p[prefix + name].reshape(-1) for name, _ in REPLICATED]
    return jnp.concatenate(parts)


def _pack_small_grads(g):
    parts = [_to_shards(g[name], axis) for name, _, axis in SMALL_SHARDED]
    rep = jnp.concatenate([g[name].reshape(-1) for name, _ in REPLICATED])
    parts.append(jnp.broadcast_to(rep[None], (N_CHIPS, rep.shape[0])))
    return jnp.concatenate(parts, axis=1)


def _unpack_small_local(flat):
    out, off = {}, 0
    for name, shape, axis in SMALL_SHARDED:
        n = _numel(shape) // N_CHIPS
        out[name] = flat[off:off + n].reshape((1,) + _shard_shape(shape, axis))
        off += n
    for name, shape in REPLICATED:
        n = _numel(shape)
        out[name] = flat[off:off + n].reshape((1,) + shape)
        off += n
    return out


def _grad_shards(g, shape, axis):
    if axis == 0:
        return g.reshape((N_CHIPS,) + _shard_shape(shape, axis))
    return jnp.transpose(g.reshape(shape[0], N_CHIPS, shape[1] // N_CHIPS), (1, 0, 2))


def _cols_from_shards(w4):
    return jnp.transpose(w4, (1, 0, 2)).reshape(w4.shape[1], -1)


def _block_diag(w):
    eye = jnp.eye(LRU_BLOCKS, dtype=w.dtype)
    return jnp.einsum("ncd,nm->ncmd", w, eye).reshape(LRU_W, LRU_W)


def _block_diag_t(g):
    g4 = g.reshape(LRU_BLOCKS, 64, LRU_BLOCKS, 64)
    return jnp.stack([g4[n, :, n, :] for n in range(LRU_BLOCKS)])


def _pad8(a):
    return jnp.pad(a, ((0, SUBLANES - a.shape[0]), (0, 0)))


def kernel(x, mem, positions, attn_norm, w_in, lru_conv_w, lru_conv_b, lru_w_a, lru_b_a, lru_w_i, lru_b_i, lru_lambda, q_a_norm, w_uq, kv_a_norm, w_ukv, mla_q_norm, mla_k_norm, lru_out_norm, mla_out_norm, w_out, mem_attn_norm, mem_norm, w_mem_q, w_mem_kv, mem_q_norm, mem_k_norm, w_mem_o, ffn_norm, w_up, ffn_conv_w, ffn_conv_b, w_down, loss_target, m_attn_norm, m_w_in, m_lru_conv_w, m_lru_conv_b, m_lru_w_a, m_lru_b_a, m_lru_w_i, m_lru_b_i, m_lru_lambda, m_q_a_norm, m_w_uq, m_kv_a_norm, m_w_ukv, m_mla_q_norm, m_mla_k_norm, m_lru_out_norm, m_mla_out_norm, m_w_out, m_mem_attn_norm, m_mem_norm, m_w_mem_q, m_w_mem_kv, m_mem_q_norm, m_mem_k_norm, m_w_mem_o, m_ffn_norm, m_w_up, m_ffn_conv_w, m_ffn_conv_b, m_w_down, v_attn_norm, v_w_in, v_lru_conv_w, v_lru_conv_b, v_lru_w_a, v_lru_b_a, v_lru_w_i, v_lru_b_i, v_lru_lambda, v_q_a_norm, v_w_uq, v_kv_a_norm, v_w_ukv, v_mla_q_norm, v_mla_k_norm, v_lru_out_norm, v_mla_out_norm, v_w_out, v_mem_attn_norm, v_mem_norm, v_w_mem_q, v_w_mem_kv, v_mem_q_norm, v_mem_k_norm, v_w_mem_o, v_ffn_norm, v_w_up, v_ffn_conv_w, v_ffn_conv_b, v_w_down):
    given = dict(locals())
    local = {name: given[name][0] for name in WEIGHT_ORDER}
    s = x.shape[1]
    x2d, mem2d, tgt = x[0], mem[0], loss_target[0]
    tm = min(256, s)
    tm_ffn = min(128, s)
    t_scan = min(256, s)
    tq_f, tq_b, tk = min(2048, s), min(1024, s), min(512, s)

    got = _gather_chips([local[name].astype(BF16) for name, _, _ in BIG] + [_pack_small_weights(local)])
    full = _unpack_small_weights(got[-1])
    for (name, shape, axis), w4 in zip(BIG, got):
        if axis == 0:
            full[name] = w4.reshape(shape)
        elif name in ("w_up", "w_mem_o"):
            full[name] = w4
        else:
            full[name] = _cols_from_shards(w4)
    row = lambda a: a.reshape(1, -1)
    b16 = lambda a: a.astype(BF16)
    zeros = lambda r, c: jnp.zeros((r, c), BF16)
    w_in_f = full["w_in"]
    w_in_p = jnp.concatenate([w_in_f[:, :OFF_KR], zeros(D_MODEL, QK_NOPE), w_in_f[:, OFF_KR:],
                              zeros(D_MODEL, LANES - QK_HEAD)], axis=1)
    w_uq_p = jnp.pad(full["w_uq"].reshape(Q_LORA, HEADS, QK_HEAD), ((0, 0), (0, 0), (0, LANES - QK_HEAD))).reshape(Q_LORA, -1)
    ukv = full["w_ukv"].reshape(KV_LORA, HEADS, QK_NOPE + V_DIM)
    w_uk_p = jnp.pad(ukv[:, :, :QK_NOPE], ((0, 0), (0, 0), (0, LANES - QK_NOPE))).reshape(KV_LORA, -1)
    w_uv = ukv[:, :, QK_NOPE:].reshape(KV_LORA, MLA_W)
    wa = [b16(_block_diag(local["lru_w_a"][d])) for d in range(2)]
    wi = [b16(_block_diag(local["lru_w_i"][d])) for d in range(2)]
    cw = [_pad8(full["lru_conv_w"][d]) for d in range(2)]
    pv = [_pad8(jnp.stack([full["lru_conv_b"][d], full["lru_b_a"][d], full["lru_b_i"][d], full["lru_lambda"][d]]))
          for d in range(2)]
    ffn_cw = _pad8(jnp.concatenate([full["ffn_conv_w"], row(local["ffn_conv_b"])], axis=0))
    g_attn, g_qa, g_kva = row(local["attn_norm"]), row(local["q_a_norm"]), row(local["kv_a_norm"])
    g_qn = jnp.pad(row(local["mla_q_norm"]), ((0, 0), (0, LANES - QK_HEAD)))
    g_kn = jnp.pad(row(local["mla_k_norm"]), ((0, 0), (0, LANES - QK_HEAD)))
    g_lru, g_mla = row(local["lru_out_norm"]), row(local["mla_out_norm"])
    g_memattn, g_mem = row(local["mem_attn_norm"]), row(local["mem_norm"])
    g_mq, g_mk, g_ffn = row(local["mem_q_norm"]), row(local["mem_k_norm"]), row(local["ffn_norm"])

    inv = ROPE_THETA ** (-jnp.arange(0, QK_ROPE, 2, dtype=F32) / QK_ROPE)
    ang = positions[0].astype(F32)[:, None] * inv
    cosv, sinv = jnp.cos(ang), jnp.sin(ang)
    ones, zer = jnp.ones((s, QK_NOPE), F32), jnp.zeros((s, LANES - QK_HEAD), F32)
    cos_t = jnp.concatenate([ones, cosv, cosv, zer + 1.0], axis=1)
    sin_t = jnp.concatenate([ones * 0.0, -sinv, sinv, zer], axis=1)

    xr, yg, cq, ckv, krp, hb_in = _in_proj(x2d, g_attn, w_in_p, tm)
    h_f = _lru_scan_fwd(xr, cw[0], pv[0], wa[0], wi[0], False, t_scan)
    h_b = _lru_scan_fwd(xr, cw[1], pv[1], wa[1], wi[1], True, t_scan)
    q, k, v = _mla_qkv(cq, ckv, krp, cos_t, sin_t, g_qa, g_kva, g_qn, g_kn, w_uq_p, w_uk_p, w_uv, tm)
    o, lse = _attn_fwd(q, k, jnp.transpose(v), tq_f, tk)
    x1, mixed = _mix_out(h_f, h_b, yg, o, x2d, g_lru, g_mla, full["w_out"], tm)
    km, vm = _mem_kv(mem2d, g_mem, full["w_mem_kv"], g_mk)
    x2, o_mem = _mem_attn(x1, g_memattn, full["w_mem_q"], g_mq, km, vm, full["w_mem_o"], tm)
    gu_pre, hb_ffn = _ffn_up(x2, g_ffn, full["w_up"], tm)
    dy, dyb, act, gu_conv, loss_acc = _ffn_down_loss(gu_pre, x2, tgt, ffn_cw, full["w_down"], tm_ffn)
    loss = lax.psum(loss_acc[0, 0] * (0.5 / D_MODEL), ("x", "y", "c"))

    grads = {}
    grads["w_down"] = _matmul_tn(act, dyb, "grad_w_down")
    (dgu,) = _ffn_bwd_act(dyb, gu_conv, full["w_down"], tm_ffn)
    dpre, g_conv = _ffn_bwd_conv(dgu, gu_pre, ffn_cw, tm_ffn)
    grads["ffn_conv_w"], grads["ffn_conv_b"] = g_conv[:3], g_conv[3]
    grads["w_up"] = _matmul_tn(hb_ffn, dpre, "grad_w_up", col_shards=True)
    dx2, dx2b, gg = _ffn_bwd_in(dpre, x2, dy, g_ffn, full["w_up"], tm)
    grads["ffn_norm"] = gg[0]
    grads["w_mem_o"] = _matmul_tn(o_mem, dx2b, "grad_w_mem_o", col_shards=True)
    dx1, dx1b, hm, dqr_mem, dkm, dvm, gg, ggq = _mem_attn_bwd(x1, dx2, dx2b, g_memattn, full["w_mem_q"], g_mq, km, vm,
                                                                 full["w_mem_o"], tm)
    grads["mem_attn_norm"], grads["mem_q_norm"] = gg[0], ggq[0]
    grads["w_mem_q"] = _matmul_tn(hm, dqr_mem, "grad_w_mem_q")
    grads["w_mem_kv"], gg, ggk, _ = _mem_kv_bwd(mem2d, g_mem, full["w_mem_kv"], g_mk, dkm, dvm)
    grads["mem_norm"], grads["mem_k_norm"] = gg[0], ggk[0]
    grads["w_out"] = _matmul_tn(mixed, dx1b, "grad_w_out")
    dh, dyg, dob, dl128, ggl, ggm = _mix_out_bwd(dx1b, h_f, h_b, yg, o, g_lru, g_mla, full["w_out"], tm)
    grads["lru_out_norm"], grads["mla_out_norm"] = ggl[0], ggm[0]
    delta_t = jnp.transpose(dl128[:, :HEADS]).reshape(HEADS // 2, 2, s)
    dq_t, dk, dv = _attn_bwd(q, k, jnp.transpose(k), v, dob, lse, delta_t, tq_b, tk)
    dq = jnp.transpose(dq_t)
    (dcq, dckv, dkrp, cqb, dqr, ckvb, dkn, dvb, ggqa, ggkva, ggqn, ggkn) = _mla_qkv_bwd(
        cq, ckv, krp, cos_t, sin_t, dq, dk, dv, g_qa, g_kva, g_qn, g_kn, w_uq_p, w_uk_p, w_uv, tm)
    grads["q_a_norm"], grads["kv_a_norm"] = ggqa[0], ggkva[0]
    grads["mla_q_norm"], grads["mla_k_norm"] = ggqn[0, :QK_HEAD], ggkn[0, :QK_HEAD]
    g_uq_p = _matmul_tn(cqb, dqr, "grad_w_uq")
    grads["w_uq"] = g_uq_p.reshape(Q_LORA, HEADS, LANES)[:, :, :QK_HEAD].reshape(Q_LORA, -1)
    g_uk_p = _matmul_tn(ckvb, dkn, "grad_w_uk").reshape(KV_LORA, HEADS, LANES)[:, :, :QK_NOPE]
    g_uv = _matmul_tn(ckvb, dvb, "grad_w_uv").reshape(KV_LORA, HEADS, V_DIM)
    grads["w_ukv"] = jnp.concatenate([g_uk_p, g_uv], axis=2).reshape(KV_LORA, -1)
    dxr, gwa, gwi, gvec = [], [], [], []
    for d, hd in enumerate((h_f, h_b)):
        r = _lru_scan_bwd(xr, hd, dh, cw[d], pv[d], wa[d], wi[d], d == 1, t_scan)
        dxr.append(r[0])
        gwa.append(_block_diag_t(r[1]))
        gwi.append(_block_diag_t(r[2]))
        gvec.append(r[3])
    grads["lru_w_a"], grads["lru_w_i"] = jnp.stack(gwa), jnp.stack(gwi)
    grads["lru_conv_w"] = jnp.stack([gv[:CONV_W] for gv in gvec])
    for r_i, name in ((4, "lru_conv_b"), (5, "lru_b_a"), (6, "lru_b_i"), (7, "lru_lambda")):
        grads[name] = jnp.stack([gv[r_i] for gv in gvec])
    grad_x, dproj, gg = _in_proj_bwd(x2d, dx1, dxr[0], dxr[1], dyg, dcq, dckv, dkrp, g_attn, w_in_p, tm)
    grads["attn_norm"] = gg[0]
    g_in_p = _matmul_tn(hb_in, dproj, "grad_w_in")
    grads["w_in"] = jnp.concatenate([g_in_p[:, :OFF_KR], g_in_p[:, OFF_KR + QK_NOPE:OFF_KR + QK_HEAD]], axis=1)

    small = _pack_small_grads(grads)
    length = small.shape[1]
    hrows = _round_up(-(-length // (2 * LANES)), 16)
    small = jnp.pad(small, ((0, 0), (0, 2 * hrows * LANES - length))).reshape(N_CHIPS, 2, hrows, LANES)
    parts = []
    for name, shape, axis in BIG:
        g4 = grads[name] if grads[name].ndim == 3 else _grad_shards(grads[name], shape, axis)
        parts.append(g4.reshape(N_CHIPS, 2, g4.shape[1] // 2, g4.shape[2]))
    parts.append(small)
    names = [name for name, _, _ in BIG] + ["small"]
    c_idx = lax.axis_index("c").astype(jnp.int32).reshape(1)
    sibling = _swap_halves(parts)
    chip_sums = [_add_halves(g, b, c_idx, F32 if n == "small" else BF16, "grad_add_halves_" + n)
                 for n, g, b in zip(names, parts, sibling)]
    arrived = _scatter_chips(chip_sums)
    reduced = _join_halves([_sum_chips(b, c_idx, "grad_sum_chips_" + n) for n, b in zip(names, arrived)])

    outs = [{}, {}, {}, {}]
    for (name, shape, axis), r in zip(BIG, reduced):
        g2 = r.reshape(_shard_shape(shape, axis))
        res = _adamw(local[name], g2, given["m_" + name][0], given["v_" + name][0], "adamw_" + name)
        for o_, a in zip(outs, (g2, *res)):
            o_[name] = a[None]
    pack = lambda prefix: _pad_rows(_pack_small_local({n: given[prefix + n] for n in WEIGHT_ORDER}), 2 * hrows)
    g_small = reduced[-1].reshape(2 * hrows, LANES)
    res = _adamw(pack(""), g_small, pack("m_"), pack("v_"), "adamw_small")
    for o_, a in zip(outs, (g_small, *res)):
        o_.update(_unpack_small_local(a.reshape(-1)))
    return (loss, grad_x[None], *[o_[n] for o_ in outs for n in WEIGHT_ORDER])
```

```python
import functools

import jax
import jax.numpy as jnp
from jax import lax
from jax.experimental import pallas as pl
from jax.experimental.pallas import tpu as pltpu

F32, BF16 = jnp.float32, jnp.bfloat16
MESH = pl.DeviceIdType.MESH

D_MODEL = 1024
EPS = 1e-6
LRU_W = 512
LRU_BLOCKS = 8
LRU_C = 8.0
CONV_W = 4
HEADS = 8
QK_NOPE, QK_ROPE, QK_HEAD, V_DIM = 64, 32, 96, 64
Q_LORA, KV_LORA = 256, 128
MLA_W = HEADS * V_DIM
ROPE_THETA = 10000.0
IN_COLS = 2 * LRU_W + Q_LORA + KV_LORA + QK_ROPE
OFF_KR = IN_COLS - QK_ROPE
IN_PAD = 1536
MEM_HEADS, MEM_HD = 4, 128
MEM_W = MEM_HEADS * MEM_HD
D_FF = 2816
N_CHIPS = 4
ADAM_LR, ADAM_B1, ADAM_B2, ADAM_EPS, ADAM_WD, ADAM_STEP = 0.001, 0.9, 0.999, 1e-08, 0.01, 10

LANES = 128
SUBLANES = 8
VMEM_LIMIT = 56 * 1024 * 1024
PACK_ROWS = 2048

SHARDED = (
    ("w_in", (D_MODEL, IN_COLS), 1, True),
    ("lru_conv_w", (2, CONV_W, LRU_W), 2, False),
    ("lru_conv_b", (2, LRU_W), 1, False),
    ("lru_b_a", (2, LRU_W), 1, False),
    ("lru_b_i", (2, LRU_W), 1, False),
    ("lru_lambda", (2, LRU_W), 1, False),
    ("w_uq", (Q_LORA, HEADS * QK_HEAD), 1, True),
    ("w_ukv", (KV_LORA, HEADS * (QK_NOPE + V_DIM)), 1, True),
    ("w_out", (2 * LRU_W, D_MODEL), 0, True),
    ("w_mem_q", (D_MODEL, MEM_W), 0, True),
    ("w_mem_kv", (D_MODEL, 2 * MEM_W), 0, True),
    ("w_mem_o", (MEM_W, D_MODEL), 1, True),
    ("w_up", (D_MODEL, 2 * D_FF), 1, True),
    ("ffn_conv_w", (3, 2 * D_FF), 1, False),
    ("w_down", (D_FF, D_MODEL), 0, True),
)
REPLICATED = (
    ("attn_norm", (D_MODEL,)), ("lru_w_a", (2, LRU_BLOCKS, 64, 64)), ("lru_w_i", (2, LRU_BLOCKS, 64, 64)),
    ("q_a_norm", (Q_LORA,)), ("kv_a_norm", (KV_LORA,)), ("mla_q_norm", (QK_HEAD,)), ("mla_k_norm", (QK_HEAD,)),
    ("lru_out_norm", (LRU_W,)), ("mla_out_norm", (MLA_W,)), ("mem_attn_norm", (D_MODEL,)), ("mem_norm", (D_MODEL,)),
    ("mem_q_norm", (MEM_HD,)), ("mem_k_norm", (MEM_HD,)), ("ffn_norm", (D_MODEL,)), ("ffn_conv_b", (2 * D_FF,)),
)
WEIGHT_ORDER = ('attn_norm', 'w_in', 'lru_conv_w', 'lru_conv_b', 'lru_w_a', 'lru_b_a', 'lru_w_i', 'lru_b_i', 'lru_lambda',
                'q_a_norm', 'w_uq', 'kv_a_norm', 'w_ukv', 'mla_q_norm', 'mla_k_norm', 'lru_out_norm', 'mla_out_norm', 'w_out',
                'mem_attn_norm', 'mem_norm', 'w_mem_q', 'w_mem_kv', 'mem_q_norm', 'mem_k_norm', 'w_mem_o', 'ffn_norm', 'w_up',
                'ffn_conv_w', 'ffn_conv_b', 'w_down')


def _numel(shape):
    n = 1
    for s in shape:
        n *= s
    return n


def _cparams(n_axes):
    return pltpu.CompilerParams(dimension_semantics=("arbitrary",) * n_axes, vmem_limit_bytes=VMEM_LIMIT)


def _bdot(a, b):
    return jnp.dot(a.astype(BF16), b.astype(BF16), preferred_element_type=F32)


def _bdot_nt(a, b):
    return lax.dot_general(a.astype(BF16), b.astype(BF16), (((1,), (1,)), ((), ())), preferred_element_type=F32)


def _bdot_tn(a, b):
    return lax.dot_general(a.astype(BF16), b.astype(BF16), (((0,), (0,)), ((), ())), preferred_element_type=F32)


def _rstd(x, n=None):
    n = x.shape[-1] if n is None else n
    return lax.rsqrt(jnp.sum(x * x, axis=-1, keepdims=True) * (1.0 / n) + EPS)


def _norm_bwd(x, rs, g, dy, n=None):
    n = x.shape[-1] if n is None else n
    xhat = x * rs
    dxh = dy * g
    dx = rs * (dxh - xhat * (jnp.sum(dxh * xhat, axis=-1, keepdims=True) * (1.0 / n)))
    return dx, dy * xhat


def _acc_row(ref, r, val):
    ref[r:r + 1, :] += jnp.sum(val, axis=0, keepdims=True)


def _zero_first(i, *refs):
    @pl.when(i == 0)
    def _():
        for r in refs:
            r[...] = jnp.zeros_like(r)


def _shift_down(x, j, halo):
    if j == 0:
        return x
    xs = pltpu.roll(x, j, 0)
    hs = pltpu.roll(halo, j, 0)
    row = lax.broadcasted_iota(jnp.int32, hs.shape, 0)
    top = jnp.where(row < j, hs, xs[:SUBLANES])
    return jnp.concatenate([top, xs[SUBLANES:]], axis=0)


def _shift_up(x, j, halo):
    if j == 0:
        return x
    t = x.shape[0]
    xs = pltpu.roll(x, t - j, 0)
    hs = pltpu.roll(halo, SUBLANES - j, 0)
    row = lax.broadcasted_iota(jnp.int32, hs.shape, 0)
    bot = jnp.where(row >= SUBLANES - j, hs, xs[t - SUBLANES:])
    return jnp.concatenate([xs[:t - SUBLANES], bot], axis=0)


def _shift(x, j, halo, down):
    return _shift_down(x, j, halo) if down else _shift_up(x, j, halo)


def _scan(a, b, down):
    t = a.shape[0]
    row = lax.broadcasted_iota(jnp.int32, a.shape, 0)
    d = 1
    while d < t:
        if down:
            keep = row >= d
            a_s = jnp.where(keep, pltpu.roll(a, d, 0), 1.0)
            b_s = jnp.where(keep, pltpu.roll(b, d, 0), 0.0)
        else:
            keep = row < t - d
            a_s = jnp.where(keep, pltpu.roll(a, t - d, 0), 1.0)
            b_s = jnp.where(keep, pltpu.roll(b, t - d, 0), 0.0)
        b = a * b_s + b
        a = a * a_s
        d *= 2
    return a, b


def _sigmoid(x):
    return 0.5 * jnp.tanh(0.5 * x) + 0.5


LOG2E = 1.4426950408889634
GELU_K = 0.7978845608028654
GELU_C = 0.044715


def _gelu(x):
    return 0.5 * x * (1.0 + jnp.tanh(GELU_K * (x + GELU_C * x * x * x)))


def _gelu_grad(x):
    t = jnp.tanh(GELU_K * (x + GELU_C * x * x * x))
    return 0.5 * (1.0 + t) + 0.5 * x * (1.0 - t * t) * GELU_K * (1.0 + 3.0 * GELU_C * x * x)


def _rope_partner(x):
    lane = lax.broadcasted_iota(jnp.int32, x.shape, 1)
    half = QK_ROPE // 2
    sw = jnp.where(lane < QK_NOPE + half, pltpu.roll(x, LANES - half, 1), pltpu.roll(x, half, 1))
    return jnp.where((lane >= QK_NOPE) & (lane < QK_HEAD), sw, 0.0)


def _rope(x, cos_t, sin_t):
    return x * cos_t + _rope_partner(x) * sin_t


def _rope_t(dy, cos_t, sin_t):
    return dy * cos_t + _rope_partner(dy * sin_t)


def _rowwise(body, name, s, tm, rows=(), halos=(), fulls=(), outs=(), accs=()):
    n = s // tm
    hb = tm // SUBLANES
    last8 = s // SUBLANES - 1
    in_specs, args = [], []
    for a in rows:
        in_specs.append(pl.BlockSpec((tm, a.shape[1]), lambda i: (i, 0)))
        args.append(a)
    for a in halos:
        in_specs.append(pl.BlockSpec((SUBLANES, a.shape[1]), lambda i: (jnp.maximum(i * hb - 1, 0), 0)))
        in_specs.append(pl.BlockSpec((SUBLANES, a.shape[1]), lambda i: (jnp.minimum((i + 1) * hb, last8), 0)))
        args += [a, a]
    for a in fulls:
        in_specs.append(pl.BlockSpec(a.shape, lambda i, nd=a.ndim: (0,) * nd))
        args.append(a)
    out_shape, out_specs = [], []
    for c, dt in outs:
        out_shape.append(jax.ShapeDtypeStruct((s, c), dt))
        out_specs.append(pl.BlockSpec((tm, c), lambda i: (i, 0)))
    for shp, dt in accs:
        out_shape.append(jax.ShapeDtypeStruct(shp, dt))
        out_specs.append(pl.BlockSpec(shp, lambda i, nd=len(shp): (0,) * nd))

    def kern(*refs):
        body(pl.program_id(0), n, *refs)

    return pl.pallas_call(kern, grid=(n,), in_specs=in_specs, out_specs=out_specs, out_shape=out_shape, name=name,
                          compiler_params=_cparams(1))(*args)


def _matmul_tn(a, b, name, col_shards=False, out_dtype=F32):
    t, m = a.shape
    n = b.shape[1]
    bm = m
    for cand in range(LANES, m + 1, LANES):
        if m % cand == 0 and cand * (n // N_CHIPS if col_shards else min(n, 2048)) * 4 <= 6 * 1024 * 1024:
            bm = cand
    bn = n // N_CHIPS if col_shards else (n if n <= 2048 else 1408)
    bt = min(512, t)
    nt = t // bt

    def kern(a_ref, b_ref, o_ref, acc_ref):
        k = pl.program_id(2)

        @pl.when(k == 0)
        def _():
            acc_ref[...] = jnp.zeros_like(acc_ref)
        acc_ref[...] += _bdot_tn(a_ref[...], b_ref[...])

        @pl.when(k == nt - 1)
        def _():
            o_ref[...] = acc_ref[...].astype(out_dtype)

    if col_shards:
        out_spec = pl.BlockSpec((None, bm, bn), lambda i, j, k: (j, i, 0))
        out_shape = jax.ShapeDtypeStruct((N_CHIPS, m, bn), out_dtype)
    else:
        out_spec = pl.BlockSpec((bm, bn), lambda i, j, k: (i, j))
        out_shape = jax.ShapeDtypeStruct((m, n), out_dtype)
    return pl.pallas_call(
        kern, grid=(m // bm, n // bn, nt),
        in_specs=[pl.BlockSpec((bt, bm), lambda i, j, k: (k, i)), pl.BlockSpec((bt, bn), lambda i, j, k: (k, j))],
        out_specs=out_spec, out_shape=out_shape, scratch_shapes=[pltpu.VMEM((bm, bn), F32)], name=name,
        compiler_params=_cparams(3))(a, b)


def _in_proj(x, g, w_in_p, tm):
    def body(i, n, x_ref, g_ref, w_ref, xr, yg, cq, ckv, krp, hb):
        xv = x_ref[...]
        h = (xv * _rstd(xv) * g_ref[...]).astype(BF16)
        hb[...] = h
        p = jnp.dot(h, w_ref[...], preferred_element_type=F32)
        xr[...] = p[:, :LRU_W]
        yg[...] = p[:, LRU_W:2 * LRU_W]
        cq[...] = p[:, 2 * LRU_W:2 * LRU_W + Q_LORA]
        ckv[...] = p[:, 2 * LRU_W + Q_LORA:OFF_KR]
        krp[...] = p[:, OFF_KR:IN_PAD]

    return _rowwise(body, "in_proj", x.shape[0], tm, rows=[x], fulls=[g, w_in_p],
                    outs=[(LRU_W, F32), (LRU_W, F32), (Q_LORA, F32), (KV_LORA, F32), (LANES, F32), (D_MODEL, BF16)])


def _lru_gates(x, halo, cw_ref, pv_ref, wa_ref, wi_ref, rev):
    down = not rev
    xc = pv_ref[0:1, :] + jnp.zeros_like(x)
    for j in range(CONV_W):
        k = j if rev else CONV_W - 1 - j
        xc = xc + cw_ref[k:k + 1, :] * _shift(x, j, halo, down)
    r = _sigmoid(_bdot(xc, wa_ref[...]) + pv_ref[1:2, :])
    ig = _sigmoid(_bdot(xc, wi_ref[...]) + pv_ref[2:3, :])
    lam = pv_ref[3:4, :]
    sp = jnp.maximum(-lam, 0.0) + jnp.log(1.0 + jnp.exp(-jnp.abs(lam)))
    log_a = (-LRU_C) * r * sp
    a = jnp.exp(log_a)
    z = 2.0 * log_a
    series = -(z * (1.0 + z * (0.5 + z * (1.0 / 6.0 + z * (1.0 / 24.0)))))
    om = jnp.where(z > -0.02, series, 1.0 - jnp.exp(z))
    mult = jnp.sqrt(om)
    return xc, r, ig, sp, a, mult


def _lru_scan_fwd(xr, cw, pv, wa, wi, rev, t):
    s = xr.shape[0]
    n = s // t
    hb = t // SUBLANES
    last8 = s // SUBLANES - 1
    down = not rev

    def kern(x_ref, halo_ref, cw_ref, pv_ref, wa_ref, wi_ref, h_ref, carry_ref):
        i = pl.program_id(0)
        _zero_first(i, carry_ref)
        halo = jnp.where(i == 0, 0.0, halo_ref[...])
        xc, r, ig, sp, a, mult = _lru_gates(x_ref[...], halo, cw_ref, pv_ref, wa_ref, wi_ref, rev)
        aa, bb = _scan(a, mult * ig * xc, down)
        h_ref[...] = aa * carry_ref[...] + bb
        carry_ref[...] = h_ref[pl.ds(t - 1 if down else 0, 1), :]

    if rev:
        blk = lambda i: (n - 1 - i, 0)
        hal = lambda i: (jnp.minimum((n - i) * hb, last8), 0)
    else:
        blk = lambda i: (i, 0)
        hal = lambda i: (jnp.maximum(i * hb - 1, 0), 0)
    full = lambda a: pl.BlockSpec(a.shape, lambda i: (0, 0))
    return pl.pallas_call(
        kern, grid=(n,),
        in_specs=[pl.BlockSpec((t, LRU_W), blk), pl.BlockSpec((SUBLANES, LRU_W), hal), full(cw), full(pv), full(wa), full(wi)],
        out_specs=pl.BlockSpec((t, LRU_W), blk), out_shape=jax.ShapeDtypeStruct((s, LRU_W), F32),
        scratch_shapes=[pltpu.VMEM((1, LRU_W), F32)], name="lru_scan_rev" if rev else "lru_scan_fwd",
        compiler_params=_cparams(1))(xr, xr, cw, pv, wa, wi)


def _lru_scan_bwd(xr, h, dh, cw, pv, wa, wi, rev, t):
    s = xr.shape[0]
    n = s // t
    hb = t // SUBLANES
    last8 = s // SUBLANES - 1
    down = not rev

    def kern(x_ref, xh_ref, h_ref, hh_ref, dh_ref, cw_ref, pv_ref, wa_ref, wi_ref,
             dx_ref, gwa_ref, gwi_ref, gv_ref, p_ref, dxc_halo_ref, tmp_ref):
        i = pl.program_id(0)
        _zero_first(i, gwa_ref, gwi_ref, gv_ref, p_ref, dxc_halo_ref)
        at_start = i == n - 1
        x = x_ref[...]
        xhalo = jnp.where(at_start, 0.0, xh_ref[...])
        hhalo = jnp.where(at_start, 0.0, hh_ref[...])
        xc, r, ig, sp, a, mult = _lru_gates(x, xhalo, cw_ref, pv_ref, wa_ref, wi_ref, rev)
        h_prev = _shift(h_ref[...], 1, hhalo, down)
        row = lax.broadcasted_iota(jnp.int32, x.shape, 0)
        edge = t - 1 if down else 0
        dh_mod = dh_ref[...] + jnp.where(row == edge, p_ref[...], 0.0)
        a_next = _shift(a, 1, jnp.zeros((SUBLANES, LRU_W), F32), not down)
        _, g = _scan(a_next, dh_mod, not down)
        tmp_ref[...] = a * g
        p_ref[...] = tmp_ref[pl.ds(0 if down else t - 1, 1), :]
        da = g * h_prev
        d_ig = g * mult * xc
        d_xc = g * mult * ig
        d_om = g * ig * xc * (0.5 / jnp.maximum(mult, 1e-30))
        d_log_a = da * a - 2.0 * d_om * a * a
        d_r = d_log_a * ((-LRU_C) * sp)
        d_sp = jnp.sum(d_log_a * ((-LRU_C) * r), axis=0, keepdims=True)
        lam = pv_ref[3:4, :]
        gv_ref[7:8, :] += d_sp * (-_sigmoid(-lam))
        d_ga = d_r * r * (1.0 - r)
        d_gi = d_ig * ig * (1.0 - ig)
        _acc_row(gv_ref, 5, d_ga)
        _acc_row(gv_ref, 6, d_gi)
        d_xc = d_xc + _bdot_nt(d_ga, wa_ref[...]) + _bdot_nt(d_gi, wi_ref[...])
        gwa_ref[...] += _bdot_tn(xc, d_ga)
        gwi_ref[...] += _bdot_tn(xc, d_gi)
        _acc_row(gv_ref, 4, d_xc)
        dx = jnp.zeros_like(x)
        dxc_halo = dxc_halo_ref[...]
        for j in range(CONV_W):
            k = j if rev else CONV_W - 1 - j
            _acc_row(gv_ref, k, d_xc * _shift(x, j, xhalo, down))
            dx = dx + cw_ref[k:k + 1, :] * _shift(d_xc, j, dxc_halo, not down)
        dx_ref[...] = dx
        dxc_halo_ref[...] = d_xc[:SUBLANES] if down else d_xc[t - SUBLANES:]

    if rev:
        blk = lambda i: (i, 0)
        hal = lambda i: (jnp.minimum((i + 1) * hb, last8), 0)
    else:
        blk = lambda i: (n - 1 - i, 0)
        hal = lambda i: (jnp.maximum((n - 1 - i) * hb - 1, 0), 0)
    full = lambda a: pl.BlockSpec(a.shape, lambda i: (0, 0))
    bs = pl.BlockSpec((t, LRU_W), blk)
    hs = pl.BlockSpec((SUBLANES, LRU_W), hal)
    return pl.pallas_call(
        kern, grid=(n,),
        in_specs=[bs, hs, bs, hs, bs, full(cw), full(pv), full(wa), full(wi)],
        out_specs=[bs, pl.BlockSpec((LRU_W, LRU_W), lambda i: (0, 0)), pl.BlockSpec((LRU_W, LRU_W), lambda i: (0, 0)),
                   pl.BlockSpec((SUBLANES, LRU_W), lambda i: (0, 0))],
        out_shape=[jax.ShapeDtypeStruct((s, LRU_W), F32), jax.ShapeDtypeStruct((LRU_W, LRU_W), F32),
                   jax.ShapeDtypeStruct((LRU_W, LRU_W), F32), jax.ShapeDtypeStruct((SUBLANES, LRU_W), F32)],
        scratch_shapes=[pltpu.VMEM((1, LRU_W), F32), pltpu.VMEM((SUBLANES, LRU_W), F32), pltpu.VMEM((t, LRU_W), F32)],
        name="lru_bwd_rev" if rev else "lru_bwd_fwd", compiler_params=_cparams(1))(xr, xr, h, h, dh, cw, pv, wa, wi)


def _mla_qkv(cq, ckv, krp, cos_t, sin_t, g_qa, g_kva, g_qn, g_kn, w_uq_p, w_uk_p, w_uv, tm):
    scale = QK_HEAD ** -0.5 * LOG2E

    def body(i, n, cq_ref, ckv_ref, kr_ref, c_ref, s_ref, gqa, gkva, gqn, gkn, wq, wk, wv, q_out, k_out, v_out):
        cosv, sinv = c_ref[...], s_ref[...]
        cqv = cq_ref[...]
        qr = _bdot(cqv * _rstd(cqv) * gqa[...], wq[...])
        ckvv = ckv_ref[...]
        c_kv = (ckvv * _rstd(ckvv) * gkva[...]).astype(BF16)
        kn = jnp.dot(c_kv, wk[...], preferred_element_type=F32)
        v_out[...] = jnp.dot(c_kv, wv[...], preferred_element_type=F32).astype(BF16)
        kr = kr_ref[...]
        for h in range(HEADS):
            sl = slice(h * LANES, (h + 1) * LANES)
            qh = qr[:, sl]
            qh = _rope(qh * _rstd(qh, QK_HEAD) * gqn[...], cosv, sinv) * scale
            q_out[:, sl] = qh.astype(BF16)
            kh = kn[:, sl] + kr
            kh = _rope(kh * _rstd(kh, QK_HEAD) * gkn[...], cosv, sinv)
            k_out[:, sl] = kh.astype(BF16)

    return _rowwise(body, "mla_qkv", cq.shape[0], tm, rows=[cq, ckv, krp, cos_t, sin_t],
                    fulls=[g_qa, g_kva, g_qn, g_kn, w_uq_p, w_uk_p, w_uv],
                    outs=[(HEADS * LANES, BF16), (HEADS * LANES, BF16), (MLA_W, BF16)])


NT_DIMS = (((1,), (1,)), ((), ()))


def _riding_exchange(copies_fn, first, last):
    @pl.when(first)
    def _():
        for cp in copies_fn():
            cp.start()

    def finish():
        @pl.when(last)
        def _():
            for cp in copies_fn():
                cp.wait()
    return finish


def _attn_fwd(q, k, vt, tq, tk, shards=()):
    s = q.shape[0]
    nq, nk = s // tq, s // tk
    n = len(shards)

    def kern(*refs):
        q_ref, k_ref, vt_ref = refs[:3]
        o_ref, lse_ref = refs[3 + n:5 + n]
        acc_ref = refs[5 + 2 * n]
        p_id, i_id = pl.program_id(0), pl.program_id(1)
        finish = _riding_exchange(lambda: _gather_copies(refs[3:3 + n], refs[5 + n:5 + 2 * n], *refs[6 + 2 * n:]),
                                  (p_id == 0) & (i_id == 0), (p_id == HEADS // 2 - 1) & (i_id == nq - 1)) if n else None
        qs = (q_ref[:, :LANES], q_ref[:, LANES:])
        acc_ref[...] = jnp.zeros_like(acc_ref)

        def step(j, carry):
            off = pl.multiple_of(j * tk, tk)
            vtc = vt_ref[:, pl.ds(off, tk)]
            out = []
            for h in range(2):
                m, l = carry[2 * h:2 * h + 2]
                st = lax.dot_general(k_ref[pl.ds(off, tk), h * LANES:(h + 1) * LANES], qs[h], NT_DIMS,
                                     preferred_element_type=F32)
                mn = jnp.maximum(m, jnp.max(st, axis=0, keepdims=True))
                al = jnp.exp2(m - mn)
                pt = jnp.exp2(st - mn)
                l = al * l + jnp.sum(pt, axis=0, keepdims=True)
                acc_ref[h] = al * acc_ref[h] + jnp.dot(vtc, pt.astype(BF16), preferred_element_type=F32)
                out += [mn, l]
            return tuple(out)

        init = (jnp.full((1, tq), -1e30, F32), jnp.zeros((1, tq), F32)) * 2
        m0, l0, m1, l1 = lax.fori_loop(0, nk, step, init)
        row = lax.broadcasted_iota(jnp.int32, (LANES, tq), 0)
        o_ref[...] = jnp.where(row < V_DIM, acc_ref[0] / l0, acc_ref[1] / l1).T
        lse_ref[0, 0:1, :] = m0 + jnp.log2(l0)
        lse_ref[0, 1:2, :] = m1 + jnp.log2(l1)
        if n:
            finish()

    return pl.pallas_call(
        kern, grid=(HEADS // 2, nq),
        in_specs=[pl.BlockSpec((tq, 2 * LANES), lambda p, i: (i, p)), pl.BlockSpec((s, 2 * LANES), lambda p, i: (0, p)),
                  pl.BlockSpec((LANES, s), lambda p, i: (p, 0))] + [ANY] * n,
        out_specs=[pl.BlockSpec((tq, LANES), lambda p, i: (i, p)), pl.BlockSpec((1, 2, tq), lambda p, i: (p, 0, i))]
        + [ANY] * n,
        out_shape=[jax.ShapeDtypeStruct((s, MLA_W), F32), jax.ShapeDtypeStruct((HEADS // 2, 2, s), F32)]
        + _gather_shapes(shards),
        scratch_shapes=[pltpu.VMEM((2, LANES, tq), F32)] + (_gather_sems(n) if n else []),
        name="attn_fwd", compiler_params=_cparams(2))(q, k, vt, *shards)


def _attn_bwd(q, k, kt, v, do, lse, delta, tq, tk, contributions=()):
    s = q.shape[0]
    nq, nk = s // tq, s // tk
    n = len(contributions)

    def kern(*refs):
        q_ref, do_ref, lse_ref, dl_ref, k_ref, kt_ref, v_ref = refs[:7]
        dqt_ref, dk_ref, dv_ref = refs[7 + n:10 + n]
        acc_ref = refs[10 + 2 * n]
        p_id, i_id = pl.program_id(0), pl.program_id(1)
        finish = _riding_exchange(lambda: _to_owner_copies(refs[7:7 + n], refs[10 + n:10 + 2 * n], *refs[11 + 2 * n:]),
                                  (p_id == 0) & (i_id == 0), (p_id == HEADS // 2 - 1) & (i_id == nq - 1)) if n else None
        _zero_first(pl.program_id(1), dk_ref, dv_ref)
        acc_ref[...] = jnp.zeros_like(acc_ref)
        qs = (q_ref[:, :LANES], q_ref[:, LANES:])
        doc = do_ref[...]
        lane_q = lax.broadcasted_iota(jnp.int32, (tq, LANES), 1)
        zq = jnp.zeros_like(doc)
        dos = (jnp.where(lane_q < V_DIM, doc, zq), jnp.where(lane_q >= V_DIM, doc, zq))
        lses = (lse_ref[0, 0:1, :], lse_ref[0, 1:2, :])
        dls = (dl_ref[0, 0:1, :], dl_ref[0, 1:2, :])

        def step(j, carry):
            off = pl.multiple_of(j * tk, tk)
            vp = v_ref[pl.ds(off, tk), :]
            lane_k = lax.broadcasted_iota(jnp.int32, (tk, LANES), 1)
            zero = jnp.zeros_like(vp)
            vs = (jnp.where(lane_k < V_DIM, vp, zero), jnp.where(lane_k >= V_DIM, vp, zero))
            for h in range(2):
                sl = slice(h * LANES, (h + 1) * LANES)
                st = lax.dot_general(k_ref[pl.ds(off, tk), sl], qs[h], NT_DIMS, preferred_element_type=F32)
                pt = jnp.exp2(st - lses[h])
                dpt = lax.dot_general(vs[h], doc, NT_DIMS, preferred_element_type=F32)
                dst = (pt * (dpt - dls[h])).astype(BF16)
                dv_ref[pl.ds(off, tk), :] += jnp.dot(pt.astype(BF16), dos[h], preferred_element_type=F32)
                dk_ref[pl.ds(off, tk), sl] += jnp.dot(dst, qs[h], preferred_element_type=F32)
                acc_ref[h] += jnp.dot(kt_ref[sl, pl.ds(off, tk)], dst, preferred_element_type=F32)
            return carry

        lax.fori_loop(0, nk, step, 0)
        dqt_ref[:LANES, :] = acc_ref[0]
        dqt_ref[LANES:, :] = acc_ref[1]
        if n:
            finish()

    return pl.pallas_call(
        kern, grid=(HEADS // 2, nq),
        in_specs=[pl.BlockSpec((tq, 2 * LANES), lambda p, i: (i, p)), pl.BlockSpec((tq, LANES), lambda p, i: (i, p)),
                  pl.BlockSpec((1, 2, tq), lambda p, i: (p, 0, i)), pl.BlockSpec((1, 2, tq), lambda p, i: (p, 0, i)),
                  pl.BlockSpec((s, 2 * LANES), lambda p, i: (0, p)), pl.BlockSpec((2 * LANES, s), lambda p, i: (p, 0)),
                  pl.BlockSpec((s, LANES), lambda p, i: (0, p))] + [ANY] * n,
        out_specs=[pl.BlockSpec((2 * LANES, tq), lambda p, i: (p, i)), pl.BlockSpec((s, 2 * LANES), lambda p, i: (0, p)),
                   pl.BlockSpec((s, LANES), lambda p, i: (0, p))] + [ANY] * n,
        out_shape=[jax.ShapeDtypeStruct((HEADS * LANES, s), F32), jax.ShapeDtypeStruct((s, HEADS * LANES), F32),
                   jax.ShapeDtypeStruct((s, MLA_W), F32)] + _to_owner_shapes(contributions),
        scratch_shapes=[pltpu.VMEM((2, LANES, tq), F32)] + (_to_owner_sems(n) if n else []),
        name="attn_bwd", compiler_params=_cparams(2))(q, do, lse, delta, k, kt, v, *contributions)


def _mix_out(hf, hb, yg, o, x, g_lru, g_mla, w_out, tm):
    def body(i, n, hf_ref, hb_ref, yg_ref, o_ref, x_ref, gl, gm, w_ref, x1_ref, mix_ref):
        lo = (hf_ref[...] + hb_ref[...]) * _gelu(yg_ref[...])
        ov = o_ref[...]
        mix_ref[:, :LRU_W] = (lo * _rstd(lo) * gl[...]).astype(BF16)
        mix_ref[:, LRU_W:] = (ov * _rstd(ov) * gm[...]).astype(BF16)
        x1_ref[...] = x_ref[...] + jnp.dot(mix_ref[...], w_ref[...], preferred_element_type=F32)

    return _rowwise(body, "mix_out", x.shape[0], tm, rows=[hf, hb, yg, o, x], fulls=[g_lru, g_mla, w_out],
                    outs=[(D_MODEL, F32), (2 * LRU_W, BF16)])


def _mem_kv(mem, g_mem, w_kv, g_k):
    m = mem.shape[0]

    def body(i, n, mem_ref, g_ref, w_ref, gk_ref, km_ref, vm_ref):
        mv = mem_ref[...]
        kv = _bdot(mv * _rstd(mv) * g_ref[...], w_ref[...])
        vm_ref[...] = kv[:, MEM_W:].astype(BF16)
        for h in range(MEM_HEADS):
            sl = slice(h * MEM_HD, (h + 1) * MEM_HD)
            kh = kv[:, sl]
            km_ref[:, sl] = (kh * _rstd(kh) * gk_ref[...]).astype(BF16)

    return _rowwise(body, "mem_kv", m, m, rows=[mem], fulls=[g_mem, w_kv, g_k], outs=[(MEM_W, BF16), (MEM_W, BF16)])


def _mem_attn_core(x1v, g_ref, wq_ref, gq_ref, km_ref, vm_ref):
    scale = MEM_HD ** -0.5
    hm = (x1v * _rstd(x1v) * g_ref[...]).astype(BF16)
    qr = jnp.dot(hm, wq_ref[...], preferred_element_type=F32)
    heads = []
    for h in range(MEM_HEADS):
        sl = slice(h * MEM_HD, (h + 1) * MEM_HD)
        qh = qr[:, sl]
        rs = _rstd(qh)
        qn = (qh * rs * gq_ref[...]).astype(BF16)
        sc = lax.dot_general(qn, km_ref[:, sl], (((1,), (1,)), ((), ())), preferred_element_type=F32) * scale
        e = jnp.exp(sc - jnp.max(sc, axis=-1, keepdims=True))
        p = e / jnp.sum(e, axis=-1, keepdims=True)
        oh = jnp.dot(p.astype(BF16), vm_ref[:, sl], preferred_element_type=F32)
        heads.append((qh, rs, qn, p, oh))
    return hm, heads


def _mem_attn(x1, g, w_q, g_q, km, vm, w_o, tm):
    cs = D_MODEL // N_CHIPS

    def body(i, n, x1_ref, g_ref, wq_ref, gq_ref, km_ref, vm_ref, wo_ref, x2_ref, ob_ref):
        x1v = x1_ref[...]
        _, heads = _mem_attn_core(x1v, g_ref, wq_ref, gq_ref, km_ref, vm_ref)
        for h in range(MEM_HEADS):
            ob_ref[:, h * MEM_HD:(h + 1) * MEM_HD] = heads[h][4].astype(BF16)
        for k in range(N_CHIPS):
            sl = slice(k * cs, (k + 1) * cs)
            x2_ref[:, sl] = x1v[:, sl] + jnp.dot(ob_ref[...], wo_ref[k], preferred_element_type=F32)

    return _rowwise(body, "mem_attn", x1.shape[0], tm, rows=[x1], fulls=[g, w_q, g_q, km, vm, w_o],
                    outs=[(D_MODEL, F32), (MEM_W, BF16)])


def _ffn_up(x2, g, w_up, tm):
    cs = 2 * D_FF // N_CHIPS

    def body(i, n, x_ref, g_ref, w_ref, gu_ref, hb_ref):
        xv = x_ref[...]
        hb_ref[...] = (xv * _rstd(xv) * g_ref[...]).astype(BF16)
        for k in range(N_CHIPS):
            gu_ref[:, k * cs:(k + 1) * cs] = jnp.dot(hb_ref[...], w_ref[k], preferred_element_type=F32)

    return _rowwise(body, "ffn_up", x2.shape[0], tm, rows=[x2], fulls=[g, w_up], outs=[(2 * D_FF, F32), (D_MODEL, BF16)])


def _ffn_conv(gu, prev, nxt, cw_ref, i, n):
    prev = jnp.where(i == 0, 0.0, prev)
    nxt = jnp.where(i == n - 1, 0.0, nxt)
    return (cw_ref[3:4, :] + cw_ref[0:1, :] * _shift_down(gu, 1, prev) + cw_ref[1:2, :] * gu
            + cw_ref[2:3, :] * _shift_up(gu, 1, nxt))


def _ffn_down_loss(gu_pre, x2, target, cw, w_down, tm):
    def body(i, n, gu_ref, x_ref, t_ref, pv_ref, nx_ref, cw_ref, w_ref, dy_ref, dyb_ref, act_ref, guc_ref, loss_ref):
        _zero_first(i, loss_ref)
        gu = _ffn_conv(gu_ref[...], pv_ref[...], nx_ref[...], cw_ref, i, n)
        guc_ref[...] = gu
        g, u = gu[:, :D_FF], gu[:, D_FF:]
        act_ref[...] = (g * _sigmoid(g) * u).astype(BF16)
        y = x_ref[...] + jnp.dot(act_ref[...], w_ref[...], preferred_element_type=F32)
        e = y - t_ref[...]
        loss_ref[...] += jnp.sum(e * e)
        dy = e * (1.0 / D_MODEL)
        dy_ref[...] = dy
        dyb_ref[...] = dy.astype(BF16)

    return _rowwise(body, "ffn_down_loss", x2.shape[0], tm, rows=[gu_pre, x2, target], halos=[gu_pre], fulls=[cw, w_down],
                    outs=[(D_MODEL, F32), (D_MODEL, BF16), (D_FF, BF16), (2 * D_FF, F32)], accs=[((SUBLANES, LANES), F32)])


def _ffn_bwd_act(dyb, gu, w_down, tm):
    def body(i, n, dy_ref, gu_ref, w_ref, dgu_ref):
        d_act = lax.dot_general(dy_ref[...], w_ref[...], (((1,), (1,)), ((), ())), preferred_element_type=F32)
        g, u = gu_ref[:, :D_FF], gu_ref[:, D_FF:]
        sg = _sigmoid(g)
        a = g * sg
        dgu_ref[:, :D_FF] = (d_act * u) * (sg + a - a * sg)
        dgu_ref[:, D_FF:] = d_act * a

    return _rowwise(body, "ffn_bwd_act", dyb.shape[0], tm, rows=[dyb, gu], fulls=[w_down], outs=[(2 * D_FF, F32)])


def _ffn_bwd_conv(dgu, gu_pre, cw, tm):
    def body(i, n, d_ref, g_ref, dp_ref, dn_ref, cw_ref, dpre_ref, gc_ref):
        _zero_first(i, gc_ref)
        d = d_ref[...]
        g = g_ref[...]
        d_next = _shift_up(d, 1, jnp.where(i == n - 1, 0.0, dn_ref[...]))
        d_prev = _shift_down(d, 1, jnp.where(i == 0, 0.0, dp_ref[...]))
        dpre_ref[...] = (cw_ref[0:1, :] * d_next + cw_ref[1:2, :] * d + cw_ref[2:3, :] * d_prev).astype(BF16)
        _acc_row(gc_ref, 0, d_next * g)
        _acc_row(gc_ref, 1, d * g)
        _acc_row(gc_ref, 2, d_prev * g)
        _acc_row(gc_ref, 3, d)

    return _rowwise(body, "ffn_bwd_conv", dgu.shape[0], tm, rows=[dgu, gu_pre], halos=[dgu], fulls=[cw],
                    outs=[(2 * D_FF, BF16)], accs=[((SUBLANES, 2 * D_FF), F32)])


def _ffn_bwd_in(dpre, x2, dy, g, w_up, tm):
    cs = 2 * D_FF // N_CHIPS

    def body(i, n, dp_ref, x_ref, dy_ref, g_ref, w_ref, dx_ref, dxb_ref, gg_ref):
        _zero_first(i, gg_ref)
        d_h = jnp.zeros(x_ref.shape, F32)
        for k in range(N_CHIPS):
            d_h = d_h + lax.dot_general(dp_ref[:, k * cs:(k + 1) * cs], w_ref[k], (((1,), (1,)), ((), ())),
                                        preferred_element_type=F32)
        xv = x_ref[...]
        dx, dg = _norm_bwd(xv, _rstd(xv), g_ref[...], d_h)
        _acc_row(gg_ref, 0, dg)
        dx = dx + dy_ref[...]
        dx_ref[...] = dx
        dxb_ref[...] = dx.astype(BF16)

    return _rowwise(body, "ffn_bwd_in", x2.shape[0], tm, rows=[dpre, x2, dy], fulls=[g, w_up],
                    outs=[(D_MODEL, F32), (D_MODEL, BF16)], accs=[((SUBLANES, D_MODEL), F32)])


def _mem_attn_bwd(x1, dx2, dx2b, g, w_q, g_q, km, vm, w_o, tm):
    scale = MEM_HD ** -0.5
    m = km.shape[0]

    def body(i, n, x1_ref, dx2_ref, dx2b_ref, g_ref, wq_ref, gq_ref, km_ref, vm_ref, wo_ref,
             dx1_ref, dx1b_ref, hm_ref, dqr_ref, dkm_ref, dvm_ref, gg_ref, ggq_ref):
        _zero_first(i, dkm_ref, dvm_ref, gg_ref, ggq_ref)
        x1v = x1_ref[...]
        hm, heads = _mem_attn_core(x1v, g_ref, wq_ref, gq_ref, km_ref, vm_ref)
        hm_ref[...] = hm
        cs = D_MODEL // N_CHIPS
        d_o = jnp.zeros((x1v.shape[0], MEM_W), F32)
        for k in range(N_CHIPS):
            d_o = d_o + lax.dot_general(dx2b_ref[:, k * cs:(k + 1) * cs], wo_ref[k], (((1,), (1,)), ((), ())),
                                        preferred_element_type=F32)
        for h in range(MEM_HEADS):
            sl = slice(h * MEM_HD, (h + 1) * MEM_HD)
            qh, rs, qn, p, _ = heads[h]
            d_oh = d_o[:, sl].astype(BF16)
            dp = lax.dot_general(d_oh, vm_ref[:, sl], (((1,), (1,)), ((), ())), preferred_element_type=F32)
            ds = (p * (dp - jnp.sum(dp * p, axis=-1, keepdims=True)) * scale).astype(BF16)
            dqn = jnp.dot(ds, km_ref[:, sl], preferred_element_type=F32)
            dkm_ref[:, sl] += lax.dot_general(ds, qn, (((0,), (0,)), ((), ())), preferred_element_type=F32)
            dvm_ref[:, sl] += lax.dot_general(p.astype(BF16), d_oh, (((0,), (0,)), ((), ())), preferred_element_type=F32)
            dqh, dgq = _norm_bwd(qh, rs, gq_ref[...], dqn)
            _acc_row(ggq_ref, 0, dgq)
            dqr_ref[:, sl] = dqh.astype(BF16)
        d_hm = lax.dot_general(dqr_ref[...], wq_ref[...], (((1,), (1,)), ((), ())), preferred_element_type=F32)
        dx, dg = _norm_bwd(x1v, _rstd(x1v), g_ref[...], d_hm)
        _acc_row(gg_ref, 0, dg)
        dx = dx + dx2_ref[...]
        dx1_ref[...] = dx
        dx1b_ref[...] = dx.astype(BF16)

    return _rowwise(body, "mem_attn_bwd", x1.shape[0], tm, rows=[x1, dx2, dx2b], fulls=[g, w_q, g_q, km, vm, w_o],
                    outs=[(D_MODEL, F32), (D_MODEL, BF16), (D_MODEL, BF16), (MEM_W, BF16)],
                    accs=[((m, MEM_W), F32), ((m, MEM_W), F32), ((SUBLANES, D_MODEL), F32), ((SUBLANES, MEM_HD), F32)])


def _mem_kv_bwd(mem, g_mem, w_kv, g_k, dkm, dvm):
    m = mem.shape[0]

    def body(i, n, mem_ref, dkm_ref, dvm_ref, g_ref, w_ref, gk_ref, gw_ref, gg_ref, ggk_ref, dkv_ref):
        gg_ref[...] = jnp.zeros_like(gg_ref)
        ggk_ref[...] = jnp.zeros_like(ggk_ref)
        mv = mem_ref[...]
        rs_m = _rstd(mv)
        mem_n = (mv * rs_m * g_ref[...]).astype(BF16)
        kv = jnp.dot(mem_n, w_ref[...], preferred_element_type=F32)
        for h in range(MEM_HEADS):
            sl = slice(h * MEM_HD, (h + 1) * MEM_HD)
            kh = kv[:, sl]
            dkh, dgk = _norm_bwd(kh, _rstd(kh), gk_ref[...], dkm_ref[:, sl])
            _acc_row(ggk_ref, 0, dgk)
            dkv_ref[:, sl] = dkh.astype(BF16)
        dkv_ref[:, MEM_W:] = dvm_ref[...].astype(BF16)
        gw_ref[...] = lax.dot_general(mem_n, dkv_ref[...], (((0,), (0,)), ((), ())), preferred_element_type=F32)
        d_mn = lax.dot_general(dkv_ref[...], w_ref[...], (((1,), (1,)), ((), ())), preferred_element_type=F32)
        _acc_row(gg_ref, 0, d_mn * (mv * rs_m))

    return _rowwise(body, "mem_kv_bwd", m, m, rows=[mem, dkm, dvm], fulls=[g_mem, w_kv, g_k],
                    accs=[((D_MODEL, 2 * MEM_W), F32), ((SUBLANES, D_MODEL), F32), ((SUBLANES, MEM_HD), F32),
                          ((m, 2 * MEM_W), BF16)])


def _mix_out_bwd(dx1b, hf, hb, yg, o, g_lru, g_mla, w_out, tm):
    def body(i, n, dx_ref, hf_ref, hb_ref, yg_ref, o_ref, gl, gm, w_ref, dh_ref, dyg_ref, dob_ref, dl_ref, ggl_ref, ggm_ref):
        _zero_first(i, ggl_ref, ggm_ref)
        dmix = lax.dot_general(dx_ref[...], w_ref[...], (((1,), (1,)), ((), ())), preferred_element_type=F32)
        hs = hf_ref[...] + hb_ref[...]
        ygv = yg_ref[...]
        ge = _gelu(ygv)
        lo = hs * ge
        d_lo, dgl = _norm_bwd(lo, _rstd(lo), gl[...], dmix[:, :LRU_W])
        _acc_row(ggl_ref, 0, dgl)
        dh_ref[...] = d_lo * ge
        dyg_ref[...] = d_lo * hs * _gelu_grad(ygv)
        ov = o_ref[...]
        d_o, dgm = _norm_bwd(ov, _rstd(ov), gm[...], dmix[:, LRU_W:])
        _acc_row(ggm_ref, 0, dgm)
        dob_ref[...] = d_o.astype(BF16)
        prod = d_o * ov
        lane_w = lax.broadcasted_iota(jnp.int32, prod.shape, 1)
        lane = lax.broadcasted_iota(jnp.int32, (prod.shape[0], LANES), 1)
        dl = jnp.zeros((prod.shape[0], LANES), F32)
        for h in range(HEADS):
            in_head = (lane_w >= h * V_DIM) & (lane_w < (h + 1) * V_DIM)
            dl = dl + jnp.where(lane == h, jnp.sum(jnp.where(in_head, prod, 0.0), axis=-1, keepdims=True), 0.0)
        dl_ref[...] = dl

    return _rowwise(body, "mix_out_bwd", dx1b.shape[0], tm, rows=[dx1b, hf, hb, yg, o], fulls=[g_lru, g_mla, w_out],
                    outs=[(LRU_W, F32), (LRU_W, F32), (MLA_W, BF16), (LANES, F32)],
                    accs=[((SUBLANES, LRU_W), F32), ((SUBLANES, MLA_W), F32)])


def _mla_qkv_bwd(cq, ckv, krp, cos_t, sin_t, dq, dk, dv, g_qa, g_kva, g_qn, g_kn, w_uq_p, w_uk_p, w_uv, tm):
    scale = QK_HEAD ** -0.5

    def body(i, n, cq_ref, ckv_ref, kr_ref, c_ref, s_ref, dq_ref, dk_ref, dv_ref, gqa, gkva, gqn, gkn, wq, wk, wv,
             dcq_ref, dckv_ref, dkr_ref, cqb_ref, dqr_ref, ckvb_ref, dkn_ref, dvb_ref, ggqa, ggkva, ggqn, ggkn):
        _zero_first(i, ggqa, ggkva, ggqn, ggkn)
        cosv, sinv = c_ref[...], s_ref[...]
        cqv = cq_ref[...]
        rs_q = _rstd(cqv)
        cqb_ref[...] = (cqv * rs_q * gqa[...]).astype(BF16)
        qr = jnp.dot(cqb_ref[...], wq[...], preferred_element_type=F32)
        ckvv = ckv_ref[...]
        rs_kv = _rstd(ckvv)
        ckvb_ref[...] = (ckvv * rs_kv * gkva[...]).astype(BF16)
        kn = jnp.dot(ckvb_ref[...], wk[...], preferred_element_type=F32)
        kr = kr_ref[...]
        dkr = jnp.zeros_like(kr)
        for h in range(HEADS):
            sl = slice(h * LANES, (h + 1) * LANES)
            qh = qr[:, sl]
            d_qn = _rope_t(dq_ref[:, sl] * scale, cosv, sinv)
            dqh, dgq = _norm_bwd(qh, _rstd(qh, QK_HEAD), gqn[...], d_qn, QK_HEAD)
            _acc_row(ggqn, 0, dgq)
            dqr_ref[:, sl] = dqh.astype(BF16)
            kh = kn[:, sl] + kr
            d_kn = _rope_t(dk_ref[:, sl] * (1.0 / LOG2E), cosv, sinv)
            dkh, dgk = _norm_bwd(kh, _rstd(kh, QK_HEAD), gkn[...], d_kn, QK_HEAD)
            _acc_row(ggkn, 0, dgk)
            dkn_ref[:, sl] = dkh.astype(BF16)
            dkr = dkr + dkh
        dkr_ref[...] = dkr
        dvb_ref[...] = dv_ref[...].astype(BF16)
        d_cq = lax.dot_general(dqr_ref[...], wq[...], (((1,), (1,)), ((), ())), preferred_element_type=F32)
        dcq, dg = _norm_bwd(cqv, rs_q, gqa[...], d_cq)
        _acc_row(ggqa, 0, dg)
        dcq_ref[...] = dcq
        d_ckv = (lax.dot_general(dkn_ref[...], wk[...], (((1,), (1,)), ((), ())), preferred_element_type=F32)
                 + lax.dot_general(dvb_ref[...], wv[...], (((1,), (1,)), ((), ())), preferred_element_type=F32))
        dckv, dg = _norm_bwd(ckvv, rs_kv, gkva[...], d_ckv)
        _acc_row(ggkva, 0, dg)
        dckv_ref[...] = dckv

    return _rowwise(body, "mla_qkv_bwd", cq.shape[0], tm, rows=[cq, ckv, krp, cos_t, sin_t, dq, dk, dv],
                    fulls=[g_qa, g_kva, g_qn, g_kn, w_uq_p, w_uk_p, w_uv],
                    outs=[(Q_LORA, F32), (KV_LORA, F32), (LANES, F32), (Q_LORA, BF16), (HEADS * LANES, BF16),
                          (KV_LORA, BF16), (HEADS * LANES, BF16), (MLA_W, BF16)],
                    accs=[((SUBLANES, Q_LORA), F32), ((SUBLANES, KV_LORA), F32), ((SUBLANES, LANES), F32),
                          ((SUBLANES, LANES), F32)])


def _in_proj_bwd(x, dx1, dxr_f, dxr_b, dyg, dcq, dckv, dkrp, g, w_in_p, tm):
    def body(i, n, x_ref, dx1_ref, df_ref, db_ref, dyg_ref, dcq_ref, dckv_ref, dkr_ref, g_ref, w_ref, gx_ref, dp_ref, gg_ref):
        _zero_first(i, gg_ref)
        dp_ref[:, :LRU_W] = (df_ref[...] + db_ref[...]).astype(BF16)
        dp_ref[:, LRU_W:2 * LRU_W] = dyg_ref[...].astype(BF16)
        dp_ref[:, 2 * LRU_W:2 * LRU_W + Q_LORA] = dcq_ref[...].astype(BF16)
        dp_ref[:, 2 * LRU_W + Q_LORA:OFF_KR] = dckv_ref[...].astype(BF16)
        dp_ref[:, OFF_KR:] = dkr_ref[...].astype(BF16)
        d_h = lax.dot_general(dp_ref[...], w_ref[...], (((1,), (1,)), ((), ())), preferred_element_type=F32)
        xv = x_ref[...]
        dx, dg = _norm_bwd(xv, _rstd(xv), g_ref[...], d_h)
        _acc_row(gg_ref, 0, dg)
        gx_ref[...] = dx + dx1_ref[...]

    return _rowwise(body, "in_proj_bwd", x.shape[0], tm, rows=[x, dx1, dxr_f, dxr_b, dyg, dcq, dckv, dkrp],
                    fulls=[g, w_in_p], outs=[(D_MODEL, F32), (IN_PAD, BF16)], accs=[((SUBLANES, D_MODEL), F32)])


ANY = pl.BlockSpec(memory_space=pl.ANY)


def _chip_peers(x, y):
    return ((1 - x, y), (x, 1 - y), (1 - x, 1 - y))


def _exchange_call(kern, name, ins, out_shapes, n_sems, aliases=None):
    return pl.pallas_call(
        kern, in_specs=[ANY] * len(ins), out_specs=[ANY] * len(out_shapes), out_shape=out_shapes,
        scratch_shapes=[pltpu.SemaphoreType.DMA((n,)) for n in n_sems], input_output_aliases=aliases or {},
        name=name)(*ins)


def _start_then_wait(copies):
    for cp in copies:
        cp.start()
    for cp in copies:
        cp.wait()


N_DEV = 8
RELATIONS = tuple((dx, dy, dc) for dx in (0, 1) for dy in (0, 1) for dc in (0, 1))[1:]


def _flip(v, d):
    return 1 - v if d else v


def _gather_copies(ins, outs, ssem, rsem, lsem):
    x, y, c = lax.axis_index("x"), lax.axis_index("y"), lax.axis_index("c")
    me = 2 * x + y
    cps = []
    for i, (a, o) in enumerate(zip(ins, outs)):
        cps.append(pltpu.make_async_copy(a, o.at[me], lsem.at[i]))
        for j, (px, py) in enumerate(_chip_peers(x, y)):
            cps.append(pltpu.make_async_remote_copy(a, o.at[me], ssem.at[3 * i + j], rsem.at[3 * i + j],
                                                    device_id=(px, py, c), device_id_type=MESH))
    return cps


def _gather_shapes(arrs):
    return [jax.ShapeDtypeStruct((N_CHIPS,) + a.shape, a.dtype) for a in arrs]


def _gather_sems(n):
    return [pltpu.SemaphoreType.DMA((3 * n,)), pltpu.SemaphoreType.DMA((3 * n,)), pltpu.SemaphoreType.DMA((n,))]


def _gather_chips(arrs):
    n = len(arrs)

    def kern(*refs):
        _start_then_wait(_gather_copies(refs[:n], refs[n:2 * n], *refs[2 * n:]))

    return _exchange_call(kern, "gather_weights", arrs, _gather_shapes(arrs), (3 * n, 3 * n, n))


def _to_owner_copies(ins, outs, ssem, rsem, lsem):
    x, y, c = lax.axis_index("x"), lax.axis_index("y"), lax.axis_index("c")
    me = 4 * x + 2 * y + c
    cps = []
    for i, (a, o) in enumerate(zip(ins, outs)):
        cps.append(pltpu.make_async_copy(a.at[2 * x + y, c], o.at[me], lsem.at[i]))
        for r, (dx, dy, dc) in enumerate(RELATIONS):
            tx, ty, tc = _flip(x, dx), _flip(y, dy), _flip(c, dc)
            cps.append(pltpu.make_async_remote_copy(a.at[2 * tx + ty, tc], o.at[me], ssem.at[7 * i + r], rsem.at[7 * i + r],
                                                    device_id=(tx, ty, tc), device_id_type=MESH))
    return cps


def _to_owner_shapes(arrs):
    return [jax.ShapeDtypeStruct((N_DEV,) + a.shape[2:], a.dtype) for a in arrs]


def _to_owner_sems(n):
    return [pltpu.SemaphoreType.DMA((7 * n,)), pltpu.SemaphoreType.DMA((7 * n,)), pltpu.SemaphoreType.DMA((n,))]


def _to_owner(arrs, name):
    n = len(arrs)

    def kern(*refs):
        _start_then_wait(_to_owner_copies(refs[:n], refs[n:2 * n], *refs[2 * n:]))

    return _exchange_call(kern, name, arrs, _to_owner_shapes(arrs), (7 * n, 7 * n, n))


def _join_halves(arrs):
    n = len(arrs)

    def kern(*refs):
        outs, (ssem, rsem) = refs[n:2 * n], refs[2 * n:]
        x, y, c = lax.axis_index("x"), lax.axis_index("y"), lax.axis_index("c")
        _start_then_wait([
            pltpu.make_async_remote_copy(outs[i].at[c], outs[i].at[c], ssem.at[i], rsem.at[i],
                                         device_id=(x, y, 1 - c), device_id_type=MESH) for i in range(n)])

    outs = [jax.ShapeDtypeStruct(a.shape, a.dtype) for a in arrs]
    return _exchange_call(kern, "grad_join_halves", arrs, outs, (n, n), aliases={i: i for i in range(n)})


def _row_block(rows, row_bytes, limit=1 << 20):
    best = None
    for d in range(16, rows + 1, 16):
        if rows % d == 0 and d * row_bytes <= limit:
            best = d
    return best if best is not None else rows


def _sum_devices(b, c, name):
    _, h, cols = b.shape
    hb = _row_block(h, cols * 4)

    def kern(c_ref, b_ref, o_ref):
        acc = b_ref[0].astype(F32)
        for j in range(1, N_DEV):
            acc = acc + b_ref[j].astype(F32)
        o_ref[...] = acc

    return pl.pallas_call(
        kern,
        grid_spec=pltpu.PrefetchScalarGridSpec(
            num_scalar_prefetch=1, grid=(h // hb,),
            in_specs=[pl.BlockSpec((N_DEV, hb, cols), lambda i, c_ref: (0, i, 0))],
            out_specs=pl.BlockSpec((None, hb, cols), lambda i, c_ref: (c_ref[0], i, 0))),
        out_shape=jax.ShapeDtypeStruct((2, h, cols), F32), name=name, compiler_params=_cparams(1))(c, b)


def _adamw(w, g, m, v, name):
    rows, cols = w.shape
    rb = _row_block(rows, cols * 4)
    c1 = 1.0 - ADAM_B1 ** ADAM_STEP
    c2 = 1.0 - ADAM_B2 ** ADAM_STEP

    def kern(w_ref, g_ref, m_ref, v_ref, d_ref, mo_ref, vo_ref):
        gv = g_ref[...]
        mn = ADAM_B1 * m_ref[...] + (1.0 - ADAM_B1) * gv
        vn = ADAM_B2 * v_ref[...] + (1.0 - ADAM_B2) * (gv * gv)
        mo_ref[...] = mn
        vo_ref[...] = vn
        d_ref[...] = (-ADAM_LR) * ((mn / c1) / (jnp.sqrt(vn / c2) + ADAM_EPS) + ADAM_WD * w_ref[...])

    spec = pl.BlockSpec((rb, cols), lambda i: (i, 0))
    return pl.pallas_call(
        kern, grid=(rows // rb,), in_specs=[spec] * 4, out_specs=[spec] * 3,
        out_shape=[jax.ShapeDtypeStruct(w.shape, F32)] * 3, name=name, compiler_params=_cparams(1))(w, g, m, v)


def _pad_rows(flat, rows):
    return jnp.pad(flat, (0, rows * LANES - flat.shape[0])).reshape(rows, LANES)


def _round_up(n, m):
    return (n + m - 1) // m * m


def _shard_shape(shape, axis):
    return tuple(s // N_CHIPS if a == axis else s for a, s in enumerate(shape))


def _to_shards(full, axis):
    shape = full.shape
    t = full.reshape(shape[:axis] + (N_CHIPS, shape[axis] // N_CHIPS) + shape[axis + 1:])
    return jnp.moveaxis(t, axis, 0).reshape(N_CHIPS, -1)


def _from_shards(sh, shape, axis):
    t = sh.reshape((N_CHIPS,) + _shard_shape(shape, axis))
    t = jnp.moveaxis(t, 0, axis)
    return t.reshape(shape)


BIG = tuple((name, shape, axis) for name, shape, axis, big in SHARDED if big)
EARLY_WEIGHTS = ("w_in", "w_uq", "w_ukv")
SMALL_SHARDED = tuple((name, shape, axis) for name, shape, axis, big in SHARDED if not big)


def _pack_small_weights(p):
    flat = jnp.concatenate([p[name].reshape(-1) for name, _, _ in SMALL_SHARDED])
    return _pad_rows(flat, _round_up(-(-flat.shape[0] // LANES), SUBLANES))


def _unpack_small_weights(gathered):
    flat = gathered.reshape(N_CHIPS, -1)
    out, off = {}, 0
    for name, shape, axis in SMALL_SHARDED:
        n = _numel(shape) // N_CHIPS
        out[name] = _from_shards(flat[:, off:off + n], shape, axis)
        off += n
    return out


def _pack_small_local(p, prefix=""):
    parts = [p[prefix + name].reshape(-1) for name, _, _ in SMALL_SHARDED]
    parts += [p[prefix + name].reshape(-1) for name, _ in REPLICATED]
    return jnp.concatenate(parts)


def _pack_small_grads(g):
    parts = [_to_shards(g[name], axis) for name, _, axis in SMALL_SHARDED]
    rep = jnp.concatenate([g[name].reshape(-1) for name, _ in REPLICATED])
    parts.append(jnp.broadcast_to(rep[None], (N_CHIPS, rep.shape[0])))
    return jnp.concatenate(parts, axis=1)


def _unpack_small_local(flat):
    out, off = {}, 0
    for name, shape, axis in SMALL_SHARDED:
        n = _numel(shape) // N_CHIPS
        out[name] = flat[off:off + n].reshape((1,) + _shard_shape(shape, axis))
        off += n
    for name, shape in REPLICATED:
        n = _numel(shape)
        out[name] = flat[off:off + n].reshape((1,) + shape)
        off += n
    return out


def _grad_shards(g, shape, axis):
    if axis == 0:
        return g.reshape((N_CHIPS,) + _shard_shape(shape, axis))
    return jnp.transpose(g.reshape(shape[0], N_CHIPS, shape[1] // N_CHIPS), (1, 0, 2))


def _cols_from_shards(w4):
    return jnp.transpose(w4, (1, 0, 2)).reshape(w4.shape[1], -1)


def _block_diag(w):
    eye = jnp.eye(LRU_BLOCKS, dtype=w.dtype)
    return jnp.einsum("ncd,nm->ncmd", w, eye).reshape(LRU_W, LRU_W)


def _block_diag_t(g):
    g4 = g.reshape(LRU_BLOCKS, 64, LRU_BLOCKS, 64)
    return jnp.stack([g4[n, :, n, :] for n in range(LRU_BLOCKS)])


def _pad8(a):
    return jnp.pad(a, ((0, SUBLANES - a.shape[0]), (0, 0)))


def kernel(x, mem, positions, attn_norm, w_in, lru_conv_w, lru_conv_b, lru_w_a, lru_b_a, lru_w_i, lru_b_i, lru_lambda, q_a_norm, w_uq, kv_a_norm, w_ukv, mla_q_norm, mla_k_norm, lru_out_norm, mla_out_norm, w_out, mem_attn_norm, mem_norm, w_mem_q, w_mem_kv, mem_q_norm, mem_k_norm, w_mem_o, ffn_norm, w_up, ffn_conv_w, ffn_conv_b, w_down, loss_target, m_attn_norm, m_w_in, m_lru_conv_w, m_lru_conv_b, m_lru_w_a, m_lru_b_a, m_lru_w_i, m_lru_b_i, m_lru_lambda, m_q_a_norm, m_w_uq, m_kv_a_norm, m_w_ukv, m_mla_q_norm, m_mla_k_norm, m_lru_out_norm, m_mla_out_norm, m_w_out, m_mem_attn_norm, m_mem_norm, m_w_mem_q, m_w_mem_kv, m_mem_q_norm, m_mem_k_norm, m_w_mem_o, m_ffn_norm, m_w_up, m_ffn_conv_w, m_ffn_conv_b, m_w_down, v_attn_norm, v_w_in, v_lru_conv_w, v_lru_conv_b, v_lru_w_a, v_lru_b_a, v_lru_w_i, v_lru_b_i, v_lru_lambda, v_q_a_norm, v_w_uq, v_kv_a_norm, v_w_ukv, v_mla_q_norm, v_mla_k_norm, v_lru_out_norm, v_mla_out_norm, v_w_out, v_mem_attn_norm, v_mem_norm, v_w_mem_q, v_w_mem_kv, v_mem_q_norm, v_mem_k_norm, v_w_mem_o, v_ffn_norm, v_w_up, v_ffn_conv_w, v_ffn_conv_b, v_w_down):
    given = dict(locals())
    local = {name: given[name][0] for name in WEIGHT_ORDER}
    s = x.shape[1]
    x2d, mem2d, tgt = x[0], mem[0], loss_target[0]
    tm = min(256, s)
    tm_ffn = min(128, s)
    t_scan = min(256, s)
    tq_f, tq_b, tk = min(2048, s), min(1024, s), min(512, s)

    early = [b for b in BIG if b[0] in EARLY_WEIGHTS]
    late = [b for b in BIG if b[0] not in EARLY_WEIGHTS]
    got = _gather_chips([local[name].astype(BF16) for name, _, _ in early] + [_pack_small_weights(local)])
    full = _unpack_small_weights(got[-1])

    def take_gathered(entries, arrays):
        for (name, shape, axis), w4 in zip(entries, arrays):
            if axis == 0:
                full[name] = w4.reshape(shape)
            elif name in ("w_up", "w_mem_o"):
                full[name] = w4
            else:
                full[name] = _cols_from_shards(w4)

    take_gathered(early, got)
    row = lambda a: a.reshape(1, -1)
    b16 = lambda a: a.astype(BF16)
    zeros = lambda r, c: jnp.zeros((r, c), BF16)
    w_in_f = full["w_in"]
    w_in_p = jnp.concatenate([w_in_f[:, :OFF_KR], zeros(D_MODEL, QK_NOPE), w_in_f[:, OFF_KR:],
                              zeros(D_MODEL, LANES - QK_HEAD)], axis=1)
    w_uq_p = jnp.pad(full["w_uq"].reshape(Q_LORA, HEADS, QK_HEAD), ((0, 0), (0, 0), (0, LANES - QK_HEAD))).reshape(Q_LORA, -1)
    ukv = full["w_ukv"].reshape(KV_LORA, HEADS, QK_NOPE + V_DIM)
    w_uk_p = jnp.pad(ukv[:, :, :QK_NOPE], ((0, 0), (0, 0), (0, LANES - QK_NOPE))).reshape(KV_LORA, -1)
    w_uv = ukv[:, :, QK_NOPE:].reshape(KV_LORA, MLA_W)
    wa = [b16(_block_diag(local["lru_w_a"][d])) for d in range(2)]
    wi = [b16(_block_diag(local["lru_w_i"][d])) for d in range(2)]
    cw = [_pad8(full["lru_conv_w"][d]) for d in range(2)]
    pv = [_pad8(jnp.stack([full["lru_conv_b"][d], full["lru_b_a"][d], full["lru_b_i"][d], full["lru_lambda"][d]]))
          for d in range(2)]
    ffn_cw = _pad8(jnp.concatenate([full["ffn_conv_w"], row(local["ffn_conv_b"])], axis=0))
    g_attn, g_qa, g_kva = row(local["attn_norm"]), row(local["q_a_norm"]), row(local["kv_a_norm"])
    g_qn = jnp.pad(row(local["mla_q_norm"]), ((0, 0), (0, LANES - QK_HEAD)))
    g_kn = jnp.pad(row(local["mla_k_norm"]), ((0, 0), (0, LANES - QK_HEAD)))
    g_lru, g_mla = row(local["lru_out_norm"]), row(local["mla_out_norm"])
    g_memattn, g_mem = row(local["mem_attn_norm"]), row(local["mem_norm"])
    g_mq, g_mk, g_ffn = row(local["mem_q_norm"]), row(local["mem_k_norm"]), row(local["ffn_norm"])

    inv = ROPE_THETA ** (-jnp.arange(0, QK_ROPE, 2, dtype=F32) / QK_ROPE)
    ang = positions[0].astype(F32)[:, None] * inv
    cosv, sinv = jnp.cos(ang), jnp.sin(ang)
    ones, zer = jnp.ones((s, QK_NOPE), F32), jnp.zeros((s, LANES - QK_HEAD), F32)
    cos_t = jnp.concatenate([ones, cosv, cosv, zer + 1.0], axis=1)
    sin_t = jnp.concatenate([ones * 0.0, -sinv, sinv, zer], axis=1)

    xr, yg, cq, ckv, krp, hb_in = _in_proj(x2d, g_attn, w_in_p, tm)
    h_f = _lru_scan_fwd(xr, cw[0], pv[0], wa[0], wi[0], False, t_scan)
    h_b = _lru_scan_fwd(xr, cw[1], pv[1], wa[1], wi[1], True, t_scan)
    q, k, v = _mla_qkv(cq, ckv, krp, cos_t, sin_t, g_qa, g_kva, g_qn, g_kn, w_uq_p, w_uk_p, w_uv, tm)
    o, lse, *got = _attn_fwd(q, k, jnp.transpose(v), tq_f, tk, shards=[local[name].astype(BF16) for name, _, _ in late])
    take_gathered(late, got)
    x1, mixed = _mix_out(h_f, h_b, yg, o, x2d, g_lru, g_mla, full["w_out"], tm)
    km, vm = _mem_kv(mem2d, g_mem, full["w_mem_kv"], g_mk)
    x2, o_mem = _mem_attn(x1, g_memattn, full["w_mem_q"], g_mq, km, vm, full["w_mem_o"], tm)
    gu_pre, hb_ffn = _ffn_up(x2, g_ffn, full["w_up"], tm)
    dy, dyb, act, gu_conv, loss_acc = _ffn_down_loss(gu_pre, x2, tgt, ffn_cw, full["w_down"], tm_ffn)
    loss = lax.psum(loss_acc[0, 0] * (0.5 / D_MODEL), ("x", "y", "c"))

    grads = {}
    grads["w_down"] = _matmul_tn(act, dyb, "grad_w_down", out_dtype=BF16)
    (dgu,) = _ffn_bwd_act(dyb, gu_conv, full["w_down"], tm_ffn)
    dpre, g_conv = _ffn_bwd_conv(dgu, gu_pre, ffn_cw, tm_ffn)
    grads["ffn_conv_w"], grads["ffn_conv_b"] = g_conv[:3], g_conv[3]
    grads["w_up"] = _matmul_tn(hb_ffn, dpre, "grad_w_up", col_shards=True, out_dtype=BF16)
    dx2, dx2b, gg = _ffn_bwd_in(dpre, x2, dy, g_ffn, full["w_up"], tm)
    grads["ffn_norm"] = gg[0]
    grads["w_mem_o"] = _matmul_tn(o_mem, dx2b, "grad_w_mem_o", col_shards=True, out_dtype=BF16)
    dx1, dx1b, hm, dqr_mem, dkm, dvm, gg, ggq = _mem_attn_bwd(x1, dx2, dx2b, g_memattn, full["w_mem_q"], g_mq, km, vm,
                                                                 full["w_mem_o"], tm)
    grads["mem_attn_norm"], grads["mem_q_norm"] = gg[0], ggq[0]
    grads["w_mem_q"] = _matmul_tn(hm, dqr_mem, "grad_w_mem_q", out_dtype=BF16)
    g_mem_kv, gg, ggk, _ = _mem_kv_bwd(mem2d, g_mem, full["w_mem_kv"], g_mk, dkm, dvm)
    grads["w_mem_kv"] = g_mem_kv.astype(BF16)
    grads["mem_norm"], grads["mem_k_norm"] = gg[0], ggk[0]
    grads["w_out"] = _matmul_tn(mixed, dx1b, "grad_w_out", out_dtype=BF16)
    dh, dyg, dob, dl128, ggl, ggm = _mix_out_bwd(dx1b, h_f, h_b, yg, o, g_lru, g_mla, full["w_out"], tm)
    grads["lru_out_norm"], grads["mla_out_norm"] = ggl[0], ggm[0]
    delta_t = jnp.transpose(dl128[:, :HEADS]).reshape(HEADS // 2, 2, s)
    def halves(name, shape, axis):
        g4 = grads[name] if grads[name].ndim == 3 else _grad_shards(grads[name], shape, axis)
        return g4.reshape(N_CHIPS, 2, g4.shape[1] // 2, g4.shape[2])

    dq_t, dk, dv, *arrived_late = _attn_bwd(q, k, jnp.transpose(k), v, dob, lse, delta_t, tq_b, tk,
                                            contributions=[halves(*e) for e in late])
    dq = jnp.transpose(dq_t)
    (dcq, dckv, dkrp, cqb, dqr, ckvb, dkn, dvb, ggqa, ggkva, ggqn, ggkn) = _mla_qkv_bwd(
        cq, ckv, krp, cos_t, sin_t, dq, dk, dv, g_qa, g_kva, g_qn, g_kn, w_uq_p, w_uk_p, w_uv, tm)
    grads["q_a_norm"], grads["kv_a_norm"] = ggqa[0], ggkva[0]
    grads["mla_q_norm"], grads["mla_k_norm"] = ggqn[0, :QK_HEAD], ggkn[0, :QK_HEAD]
    g_uq_p = _matmul_tn(cqb, dqr, "grad_w_uq")
    grads["w_uq"] = g_uq_p.reshape(Q_LORA, HEADS, LANES)[:, :, :QK_HEAD].reshape(Q_LORA, -1)
    g_uk_p = _matmul_tn(ckvb, dkn, "grad_w_uk").reshape(KV_LORA, HEADS, LANES)[:, :, :QK_NOPE]
    g_uv = _matmul_tn(ckvb, dvb, "grad_w_uv").reshape(KV_LORA, HEADS, V_DIM)
    grads["w_ukv"] = jnp.concatenate([g_uk_p, g_uv], axis=2).reshape(KV_LORA, -1)
    dxr, gwa, gwi, gvec = [], [], [], []
    for d, hd in enumerate((h_f, h_b)):
        r = _lru_scan_bwd(xr, hd, dh, cw[d], pv[d], wa[d], wi[d], d == 1, t_scan)
        dxr.append(r[0])
        gwa.append(_block_diag_t(r[1]))
        gwi.append(_block_diag_t(r[2]))
        gvec.append(r[3])
    grads["lru_w_a"], grads["lru_w_i"] = jnp.stack(gwa), jnp.stack(gwi)
    grads["lru_conv_w"] = jnp.stack([gv[:CONV_W] for gv in gvec])
    for r_i, name in ((4, "lru_conv_b"), (5, "lru_b_a"), (6, "lru_b_i"), (7, "lru_lambda")):
        grads[name] = jnp.stack([gv[r_i] for gv in gvec])
    grad_x, dproj, gg = _in_proj_bwd(x2d, dx1, dxr[0], dxr[1], dyg, dcq, dckv, dkrp, g_attn, w_in_p, tm)
    grads["attn_norm"] = gg[0]
    g_in_p = _matmul_tn(hb_in, dproj, "grad_w_in")
    grads["w_in"] = jnp.concatenate([g_in_p[:, :OFF_KR], g_in_p[:, OFF_KR + QK_NOPE:OFF_KR + QK_HEAD]], axis=1)

    small = _pack_small_grads(grads)
    length = small.shape[1]
    hrows = _round_up(-(-length // (2 * LANES)), 16)
    small = jnp.pad(small, ((0, 0), (0, 2 * hrows * LANES - length))).reshape(N_CHIPS, 2, hrows, LANES)
    for name, _, _ in early:
        grads[name] = grads[name].astype(BF16)
    arrived_early = _to_owner([halves(*e) for e in early] + [small], "grad_to_owner")
    names = [name for name, _, _ in late + early] + ["small"]
    c_idx = lax.axis_index("c").astype(jnp.int32).reshape(1)
    reduced = _join_halves([_sum_devices(b, c_idx, "grad_sum_" + n)
                            for n, b in zip(names, list(arrived_late) + list(arrived_early))])

    outs = [{}, {}, {}, {}]
    for (name, shape, axis), r in zip(late + early, reduced):
        g2 = r.reshape(_shard_shape(shape, axis))
        res = _adamw(local[name], g2, given["m_" + name][0], given["v_" + name][0], "adamw_" + name)
        for o_, a in zip(outs, (g2, *res)):
            o_[name] = a[None]
    pack = lambda prefix: _pad_rows(_pack_small_local({n: given[prefix + n] for n in WEIGHT_ORDER}), 2 * hrows)
    g_small = reduced[-1].reshape(2 * hrows, LANES)
    res = _adamw(pack(""), g_small, pack("m_"), pack("v_"), "adamw_small")
    for o_, a in zip(outs, (g_small, *res)):
        o_.update(_unpack_small_local(a.reshape(-1)))
    return (loss, grad_x[None], *[o_[n] for o_ in outs for n in WEIGHT_ORDER])
```

```python
import functools

import jax
import jax.numpy as jnp
from jax import lax
from jax.experimental import pallas as pl
from jax.experimental.pallas import tpu as pltpu

F32, BF16 = jnp.float32, jnp.bfloat16
MESH = pl.DeviceIdType.MESH

D_MODEL = 1024
EPS = 1e-6
LRU_W = 512
LRU_BLOCKS = 8
LRU_C = 8.0
CONV_W = 4
HEADS = 8
QK_NOPE, QK_ROPE, QK_HEAD, V_DIM = 64, 32, 96, 64
Q_LORA, KV_LORA = 256, 128
MLA_W = HEADS * V_DIM
ROPE_THETA = 10000.0
IN_COLS = 2 * LRU_W + Q_LORA + KV_LORA + QK_ROPE
OFF_KR = IN_COLS - QK_ROPE
IN_PAD = 1536
MEM_HEADS, MEM_HD = 4, 128
MEM_W = MEM_HEADS * MEM_HD
D_FF = 2816
N_CHIPS = 4
ADAM_LR, ADAM_B1, ADAM_B2, ADAM_EPS, ADAM_WD, ADAM_STEP = 0.001, 0.9, 0.999, 1e-08, 0.01, 10

LANES = 128
SUBLANES = 8
VMEM_LIMIT = 56 * 1024 * 1024
PACK_ROWS = 2048

SHARDED = (
    ("w_in", (D_MODEL, IN_COLS), 1, True),
    ("lru_conv_w", (2, CONV_W, LRU_W), 2, False),
    ("lru_conv_b", (2, LRU_W), 1, False),
    ("lru_b_a", (2, LRU_W), 1, False),
    ("lru_b_i", (2, LRU_W), 1, False),
    ("lru_lambda", (2, LRU_W), 1, False),
    ("w_uq", (Q_LORA, HEADS * QK_HEAD), 1, True),
    ("w_ukv", (KV_LORA, HEADS * (QK_NOPE + V_DIM)), 1, True),
    ("w_out", (2 * LRU_W, D_MODEL), 0, True),
    ("w_mem_q", (D_MODEL, MEM_W), 0, True),
    ("w_mem_kv", (D_MODEL, 2 * MEM_W), 0, True),
    ("w_mem_o", (MEM_W, D_MODEL), 1, True),
    ("w_up", (D_MODEL, 2 * D_FF), 1, True),
    ("ffn_conv_w", (3, 2 * D_FF), 1, False),
    ("w_down", (D_FF, D_MODEL), 0, True),
)
REPLICATED = (
    ("attn_norm", (D_MODEL,)), ("lru_w_a", (2, LRU_BLOCKS, 64, 64)), ("lru_w_i", (2, LRU_BLOCKS, 64, 64)),
    ("q_a_norm", (Q_LORA,)), ("kv_a_norm", (KV_LORA,)), ("mla_q_norm", (QK_HEAD,)), ("mla_k_norm", (QK_HEAD,)),
    ("lru_out_norm", (LRU_W,)), ("mla_out_norm", (MLA_W,)), ("mem_attn_norm", (D_MODEL,)), ("mem_norm", (D_MODEL,)),
    ("mem_q_norm", (MEM_HD,)), ("mem_k_norm", (MEM_HD,)), ("ffn_norm", (D_MODEL,)), ("ffn_conv_b", (2 * D_FF,)),
)
WEIGHT_ORDER = ('attn_norm', 'w_in', 'lru_conv_w', 'lru_conv_b', 'lru_w_a', 'lru_b_a', 'lru_w_i', 'lru_b_i', 'lru_lambda',
                'q_a_norm', 'w_uq', 'kv_a_norm', 'w_ukv', 'mla_q_norm', 'mla_k_norm', 'lru_out_norm', 'mla_out_norm', 'w_out',
                'mem_attn_norm', 'mem_norm', 'w_mem_q', 'w_mem_kv', 'mem_q_norm', 'mem_k_norm', 'w_mem_o', 'ffn_norm', 'w_up',
                'ffn_conv_w', 'ffn_conv_b', 'w_down')


def _numel(shape):
    n = 1
    for s in shape:
        n *= s
    return n


def _cparams(n_axes):
    return pltpu.CompilerParams(dimension_semantics=("arbitrary",) * n_axes, vmem_limit_bytes=VMEM_LIMIT)


def _bdot(a, b):
    return jnp.dot(a.astype(BF16), b.astype(BF16), preferred_element_type=F32)


def _bdot_nt(a, b):
    return lax.dot_general(a.astype(BF16), b.astype(BF16), (((1,), (1,)), ((), ())), preferred_element_type=F32)


def _bdot_tn(a, b):
    return lax.dot_general(a.astype(BF16), b.astype(BF16), (((0,), (0,)), ((), ())), preferred_element_type=F32)


def _rstd(x, n=None):
    n = x.shape[-1] if n is None else n
    return lax.rsqrt(jnp.sum(x * x, axis=-1, keepdims=True) * (1.0 / n) + EPS)


def _norm_bwd(x, rs, g, dy, n=None):
    n = x.shape[-1] if n is None else n
    xhat = x * rs
    dxh = dy * g
    dx = rs * (dxh - xhat * (jnp.sum(dxh * xhat, axis=-1, keepdims=True) * (1.0 / n)))
    return dx, dy * xhat


def _acc_row(ref, r, val):
    ref[r:r + 1, :] += jnp.sum(val, axis=0, keepdims=True)


def _zero_first(i, *refs):
    @pl.when(i == 0)
    def _():
        for r in refs:
            r[...] = jnp.zeros_like(r)


def _shift_down(x, j, halo):
    if j == 0:
        return x
    xs = pltpu.roll(x, j, 0)
    hs = pltpu.roll(halo, j, 0)
    row = lax.broadcasted_iota(jnp.int32, hs.shape, 0)
    top = jnp.where(row < j, hs, xs[:SUBLANES])
    return jnp.concatenate([top, xs[SUBLANES:]], axis=0)


def _shift_up(x, j, halo):
    if j == 0:
        return x
    t = x.shape[0]
    xs = pltpu.roll(x, t - j, 0)
    hs = pltpu.roll(halo, SUBLANES - j, 0)
    row = lax.broadcasted_iota(jnp.int32, hs.shape, 0)
    bot = jnp.where(row >= SUBLANES - j, hs, xs[t - SUBLANES:])
    return jnp.concatenate([xs[:t - SUBLANES], bot], axis=0)


def _shift(x, j, halo, down):
    return _shift_down(x, j, halo) if down else _shift_up(x, j, halo)


def _scan(a, b, h_in, down):
    t, c = a.shape
    g = t // SUBLANES
    a3, b3 = a.reshape(g, SUBLANES, c), b.reshape(g, SUBLANES, c)
    sub = lax.broadcasted_iota(jnp.int32, a3.shape, 1)
    d = 1
    while d < SUBLANES:
        keep = (sub >= d) if down else (sub < SUBLANES - d)
        shift = d if down else SUBLANES - d
        a_s = jnp.where(keep, pltpu.roll(a3, shift, 1), 1.0)
        b_s = jnp.where(keep, pltpu.roll(b3, shift, 1), 0.0)
        b3 = a3 * b_s + b3
        a3 = a3 * a_s
        d *= 2
    hs = [None] * g
    carry = h_in
    for i in (range(g) if down else range(g - 1, -1, -1)):
        hs[i] = a3[i] * carry + b3[i]
        carry = hs[i][SUBLANES - 1:, :] if down else hs[i][:1, :]
    return jnp.concatenate(hs, axis=0)


def _sigmoid(x):
    return 0.5 * jnp.tanh(0.5 * x) + 0.5


LOG2E = 1.4426950408889634
GELU_K = 0.7978845608028654
GELU_C = 0.044715


def _gelu(x):
    return 0.5 * x * (1.0 + jnp.tanh(GELU_K * (x + GELU_C * x * x * x)))


def _gelu_grad(x):
    t = jnp.tanh(GELU_K * (x + GELU_C * x * x * x))
    return 0.5 * (1.0 + t) + 0.5 * x * (1.0 - t * t) * GELU_K * (1.0 + 3.0 * GELU_C * x * x)


def _rope_partner(x):
    lane = lax.broadcasted_iota(jnp.int32, x.shape, 1)
    half = QK_ROPE // 2
    sw = jnp.where(lane < QK_NOPE + half, pltpu.roll(x, LANES - half, 1), pltpu.roll(x, half, 1))
    return jnp.where((lane >= QK_NOPE) & (lane < QK_HEAD), sw, 0.0)


def _rope(x, cos_t, sin_t):
    return x * cos_t + _rope_partner(x) * sin_t


def _rope_t(dy, cos_t, sin_t):
    return dy * cos_t + _rope_partner(dy * sin_t)


def _rowwise(body, name, s, tm, rows=(), halos=(), fulls=(), outs=(), accs=()):
    n = s // tm
    hb = tm // SUBLANES
    last8 = s // SUBLANES - 1
    in_specs, args = [], []
    for a in rows:
        in_specs.append(pl.BlockSpec((tm, a.shape[1]), lambda i: (i, 0)))
        args.append(a)
    for a in halos:
        in_specs.append(pl.BlockSpec((SUBLANES, a.shape[1]), lambda i: (jnp.maximum(i * hb - 1, 0), 0)))
        in_specs.append(pl.BlockSpec((SUBLANES, a.shape[1]), lambda i: (jnp.minimum((i + 1) * hb, last8), 0)))
        args += [a, a]
    for a in fulls:
        in_specs.append(pl.BlockSpec(a.shape, lambda i, nd=a.ndim: (0,) * nd))
        args.append(a)
    out_shape, out_specs = [], []
    for c, dt in outs:
        out_shape.append(jax.ShapeDtypeStruct((s, c), dt))
        out_specs.append(pl.BlockSpec((tm, c), lambda i: (i, 0)))
    for shp, dt in accs:
        out_shape.append(jax.ShapeDtypeStruct(shp, dt))
        out_specs.append(pl.BlockSpec(shp, lambda i, nd=len(shp): (0,) * nd))

    def kern(*refs):
        body(pl.program_id(0), n, *refs)

    return pl.pallas_call(kern, grid=(n,), in_specs=in_specs, out_specs=out_specs, out_shape=out_shape, name=name,
                          compiler_params=_cparams(1))(*args)


def _matmul_tn(a, b, name, col_shards=False, out_dtype=F32):
    t, m = a.shape
    n = b.shape[1]
    bm = m
    for cand in range(LANES, m + 1, LANES):
        if m % cand == 0 and cand * (n // N_CHIPS if col_shards else min(n, 2048)) * 4 <= 6 * 1024 * 1024:
            bm = cand
    bn = n // N_CHIPS if col_shards else (n if n <= 2048 else 1408)
    bt = min(512, t)
    nt = t // bt

    def kern(a_ref, b_ref, o_ref, acc_ref):
        k = pl.program_id(2)

        @pl.when(k == 0)
        def _():
            acc_ref[...] = jnp.zeros_like(acc_ref)
        acc_ref[...] += _bdot_tn(a_ref[...], b_ref[...])

        @pl.when(k == nt - 1)
        def _():
            o_ref[...] = acc_ref[...].astype(out_dtype)

    if col_shards:
        out_spec = pl.BlockSpec((None, bm, bn), lambda i, j, k: (j, i, 0))
        out_shape = jax.ShapeDtypeStruct((N_CHIPS, m, bn), out_dtype)
    else:
        out_spec = pl.BlockSpec((bm, bn), lambda i, j, k: (i, j))
        out_shape = jax.ShapeDtypeStruct((m, n), out_dtype)
    return pl.pallas_call(
        kern, grid=(m // bm, n // bn, nt),
        in_specs=[pl.BlockSpec((bt, bm), lambda i, j, k: (k, i)), pl.BlockSpec((bt, bn), lambda i, j, k: (k, j))],
        out_specs=out_spec, out_shape=out_shape, scratch_shapes=[pltpu.VMEM((bm, bn), F32)], name=name,
        compiler_params=_cparams(3))(a, b)


def _in_proj(x, g, w_in_p, tm):
    def body(i, n, x_ref, g_ref, w_ref, xr, yg, cq, ckv, krp, hb):
        xv = x_ref[...]
        h = (xv * _rstd(xv) * g_ref[...]).astype(BF16)
        hb[...] = h
        p = jnp.dot(h, w_ref[...], preferred_element_type=F32)
        xr[...] = p[:, :LRU_W]
        yg[...] = p[:, LRU_W:2 * LRU_W]
        cq[...] = p[:, 2 * LRU_W:2 * LRU_W + Q_LORA]
        ckv[...] = p[:, 2 * LRU_W + Q_LORA:OFF_KR]
        krp[...] = p[:, OFF_KR:IN_PAD]

    return _rowwise(body, "in_proj", x.shape[0], tm, rows=[x], fulls=[g, w_in_p],
                    outs=[(LRU_W, F32), (LRU_W, F32), (Q_LORA, F32), (KV_LORA, F32), (LANES, F32), (D_MODEL, BF16)])


def _lru_gates(x, halo, cw_ref, pv_ref, wa_ref, wi_ref, rev):
    down = not rev
    xc = pv_ref[0:1, :] + jnp.zeros_like(x)
    for j in range(CONV_W):
        k = j if rev else CONV_W - 1 - j
        xc = xc + cw_ref[k:k + 1, :] * _shift(x, j, halo, down)
    r = _sigmoid(_bdot(xc, wa_ref[...]) + pv_ref[1:2, :])
    ig = _sigmoid(_bdot(xc, wi_ref[...]) + pv_ref[2:3, :])
    lam = pv_ref[3:4, :]
    sp = jnp.maximum(-lam, 0.0) + jnp.log(1.0 + jnp.exp(-jnp.abs(lam)))
    log_a = (-LRU_C) * r * sp
    a = jnp.exp(log_a)
    z = 2.0 * log_a
    series = -(z * (1.0 + z * (0.5 + z * (1.0 / 6.0 + z * (1.0 / 24.0)))))
    om = jnp.where(z > -0.02, series, 1.0 - jnp.exp(z))
    mult = jnp.sqrt(om)
    return xc, r, ig, sp, a, mult


def _lru_scan_fwd(xr, cw, pv, wa, wi, rev, t):
    s = xr.shape[0]
    n = s // t
    hb = t // SUBLANES
    last8 = s // SUBLANES - 1
    down = not rev

    def kern(x_ref, halo_ref, cw_ref, pv_ref, wa_ref, wi_ref, h_ref, carry_ref):
        i = pl.program_id(0)
        _zero_first(i, carry_ref)
        halo = jnp.where(i == 0, 0.0, halo_ref[...])
        xc, r, ig, sp, a, mult = _lru_gates(x_ref[...], halo, cw_ref, pv_ref, wa_ref, wi_ref, rev)
        h_ref[...] = _scan(a, mult * ig * xc, carry_ref[...], down)
        carry_ref[...] = h_ref[pl.ds(t - 1 if down else 0, 1), :]

    if rev:
        blk = lambda i: (n - 1 - i, 0)
        hal = lambda i: (jnp.minimum((n - i) * hb, last8), 0)
    else:
        blk = lambda i: (i, 0)
        hal = lambda i: (jnp.maximum(i * hb - 1, 0), 0)
    full = lambda a: pl.BlockSpec(a.shape, lambda i: (0, 0))
    return pl.pallas_call(
        kern, grid=(n,),
        in_specs=[pl.BlockSpec((t, LRU_W), blk), pl.BlockSpec((SUBLANES, LRU_W), hal), full(cw), full(pv), full(wa), full(wi)],
        out_specs=pl.BlockSpec((t, LRU_W), blk), out_shape=jax.ShapeDtypeStruct((s, LRU_W), F32),
        scratch_shapes=[pltpu.VMEM((1, LRU_W), F32)], name="lru_scan_rev" if rev else "lru_scan_fwd",
        compiler_params=_cparams(1))(xr, xr, cw, pv, wa, wi)


def _lru_scan_bwd(xr, h, dh, cw, pv, wa, wi, rev, t):
    s = xr.shape[0]
    n = s // t
    hb = t // SUBLANES
    last8 = s // SUBLANES - 1
    down = not rev

    def kern(x_ref, xh_ref, h_ref, hh_ref, dh_ref, cw_ref, pv_ref, wa_ref, wi_ref,
             dx_ref, gwa_ref, gwi_ref, gv_ref, p_ref, dxc_halo_ref, tmp_ref):
        i = pl.program_id(0)
        _zero_first(i, gwa_ref, gwi_ref, gv_ref, p_ref, dxc_halo_ref)
        at_start = i == n - 1
        x = x_ref[...]
        xhalo = jnp.where(at_start, 0.0, xh_ref[...])
        hhalo = jnp.where(at_start, 0.0, hh_ref[...])
        xc, r, ig, sp, a, mult = _lru_gates(x, xhalo, cw_ref, pv_ref, wa_ref, wi_ref, rev)
        h_prev = _shift(h_ref[...], 1, hhalo, down)
        row = lax.broadcasted_iota(jnp.int32, x.shape, 0)
        edge = t - 1 if down else 0
        dh_mod = dh_ref[...] + jnp.where(row == edge, p_ref[...], 0.0)
        a_next = _shift(a, 1, jnp.zeros((SUBLANES, LRU_W), F32), not down)
        g = _scan(a_next, dh_mod, jnp.zeros((1, LRU_W), F32), not down)
        tmp_ref[...] = a * g
        p_ref[...] = tmp_ref[pl.ds(0 if down else t - 1, 1), :]
        da = g * h_prev
        d_ig = g * mult * xc
        d_xc = g * mult * ig
        d_om = g * ig * xc * (0.5 / jnp.maximum(mult, 1e-30))
        d_log_a = da * a - 2.0 * d_om * a * a
        d_r = d_log_a * ((-LRU_C) * sp)
        d_sp = jnp.sum(d_log_a * ((-LRU_C) * r), axis=0, keepdims=True)
        lam = pv_ref[3:4, :]
        gv_ref[7:8, :] += d_sp * (-_sigmoid(-lam))
        d_ga = d_r * r * (1.0 - r)
        d_gi = d_ig * ig * (1.0 - ig)
        _acc_row(gv_ref, 5, d_ga)
        _acc_row(gv_ref, 6, d_gi)
        d_xc = d_xc + _bdot_nt(d_ga, wa_ref[...]) + _bdot_nt(d_gi, wi_ref[...])
        gwa_ref[...] += _bdot_tn(xc, d_ga)
        gwi_ref[...] += _bdot_tn(xc, d_gi)
        _acc_row(gv_ref, 4, d_xc)
        dx = jnp.zeros_like(x)
        dxc_halo = dxc_halo_ref[...]
        for j in range(CONV_W):
            k = j if rev else CONV_W - 1 - j
            _acc_row(gv_ref, k, d_xc * _shift(x, j, xhalo, down))
            dx = dx + cw_ref[k:k + 1, :] * _shift(d_xc, j, dxc_halo, not down)
        dx_ref[...] = dx
        dxc_halo_ref[...] = d_xc[:SUBLANES] if down else d_xc[t - SUBLANES:]

    if rev:
        blk = lambda i: (i, 0)
        hal = lambda i: (jnp.minimum((i + 1) * hb, last8), 0)
    else:
        blk = lambda i: (n - 1 - i, 0)
        hal = lambda i: (jnp.maximum((n - 1 - i) * hb - 1, 0), 0)
    full = lambda a: pl.BlockSpec(a.shape, lambda i: (0, 0))
    bs = pl.BlockSpec((t, LRU_W), blk)
    hs = pl.BlockSpec((SUBLANES, LRU_W), hal)
    return pl.pallas_call(
        kern, grid=(n,),
        in_specs=[bs, hs, bs, hs, bs, full(cw), full(pv), full(wa), full(wi)],
        out_specs=[bs, pl.BlockSpec((LRU_W, LRU_W), lambda i: (0, 0)), pl.BlockSpec((LRU_W, LRU_W), lambda i: (0, 0)),
                   pl.BlockSpec((SUBLANES, LRU_W), lambda i: (0, 0))],
        out_shape=[jax.ShapeDtypeStruct((s, LRU_W), F32), jax.ShapeDtypeStruct((LRU_W, LRU_W), F32),
                   jax.ShapeDtypeStruct((LRU_W, LRU_W), F32), jax.ShapeDtypeStruct((SUBLANES, LRU_W), F32)],
        scratch_shapes=[pltpu.VMEM((1, LRU_W), F32), pltpu.VMEM((SUBLANES, LRU_W), F32), pltpu.VMEM((t, LRU_W), F32)],
        name="lru_bwd_rev" if rev else "lru_bwd_fwd", compiler_params=_cparams(1))(xr, xr, h, h, dh, cw, pv, wa, wi)


def _mla_qkv(cq, ckv, krp, cos_t, sin_t, g_qa, g_kva, g_qn, g_kn, w_uq_p, w_uk_p, w_uv, tm):
    scale = QK_HEAD ** -0.5 * LOG2E

    def body(i, n, cq_ref, ckv_ref, kr_ref, c_ref, s_ref, gqa, gkva, gqn, gkn, wq, wk, wv, q_out, k_out, v_out):
        cosv, sinv = c_ref[...], s_ref[...]
        cqv = cq_ref[...]
        qr = _bdot(cqv * _rstd(cqv) * gqa[...], wq[...])
        ckvv = ckv_ref[...]
        c_kv = (ckvv * _rstd(ckvv) * gkva[...]).astype(BF16)
        kn = jnp.dot(c_kv, wk[...], preferred_element_type=F32)
        v_out[...] = jnp.dot(c_kv, wv[...], preferred_element_type=F32).astype(BF16)
        kr = kr_ref[...]
        kr_swapped = _rope_partner(kr * gkn[...]) * sinv
        for h in range(HEADS):
            sl = slice(h * LANES, (h + 1) * LANES)
            qh = qr[:, sl]
            qh = _rope(qh * _rstd(qh, QK_HEAD) * gqn[...], cosv, sinv) * scale
            q_out[:, sl] = qh.astype(BF16)
            kh = kn[:, sl] + kr
            rs = _rstd(kh, QK_HEAD)
            k_out[:, sl] = (kh * rs * gkn[...] * cosv + kr_swapped * rs).astype(BF16)

    return _rowwise(body, "mla_qkv", cq.shape[0], tm, rows=[cq, ckv, krp, cos_t, sin_t],
                    fulls=[g_qa, g_kva, g_qn, g_kn, w_uq_p, w_uk_p, w_uv],
                    outs=[(HEADS * LANES, BF16), (HEADS * LANES, BF16), (MLA_W, BF16)])


NT_DIMS = (((1,), (1,)), ((), ()))
TN_DIMS = (((0,), (0,)), ((), ()))


def _riding_exchange(copies_fn, first, last):
    @pl.when(first)
    def _():
        for cp in copies_fn():
            cp.start()

    def finish():
        @pl.when(last)
        def _():
            for cp in copies_fn():
                cp.wait()
    return finish


def _attn_fwd(q, k, v, tq, tk, shards=()):
    s = q.shape[0]
    nq, nk = s // tq, s // tk
    n = len(shards)

    def kern(*refs):
        q_ref, k_ref, v_ref = refs[:3]
        o_ref, lse_ref = refs[3 + n:5 + n]
        acc_ref = refs[5 + 2 * n]
        p_id, i_id = pl.program_id(0), pl.program_id(1)
        finish = _riding_exchange(lambda: _gather_copies(refs[3:3 + n], refs[5 + n:5 + 2 * n], *refs[6 + 2 * n:]),
                                  (p_id == 0) & (i_id == 0), (p_id == HEADS // 2 - 1) & (i_id == nq - 1)) if n else None
        qs = (q_ref[:, :LANES], q_ref[:, LANES:])
        acc_ref[...] = jnp.zeros_like(acc_ref)

        def step(j, carry):
            off = pl.multiple_of(j * tk, tk)
            vc = v_ref[pl.ds(off, tk), :]
            out = []
            for h in range(2):
                m, l = carry[2 * h:2 * h + 2]
                st = lax.dot_general(k_ref[pl.ds(off, tk), h * LANES:(h + 1) * LANES], qs[h], NT_DIMS,
                                     preferred_element_type=F32)
                mn = jnp.maximum(m, jnp.max(st, axis=0, keepdims=True))
                al = jnp.exp2(m - mn)
                pt = jnp.exp2(st - mn)
                l = al * l + jnp.sum(pt, axis=0, keepdims=True)
                acc_ref[h] = al * acc_ref[h] + lax.dot_general(vc, pt.astype(BF16), TN_DIMS, preferred_element_type=F32)
                out += [mn, l]
            return tuple(out)

        init = (jnp.full((1, tq), -1e30, F32), jnp.zeros((1, tq), F32)) * 2
        m0, l0, m1, l1 = lax.fori_loop(0, nk, step, init)
        row = lax.broadcasted_iota(jnp.int32, (LANES, tq), 0)
        o_ref[...] = jnp.where(row < V_DIM, acc_ref[0] / l0, acc_ref[1] / l1).T
        lse_ref[0, 0:1, :] = m0 + jnp.log2(l0)
        lse_ref[0, 1:2, :] = m1 + jnp.log2(l1)
        if n:
            finish()

    return pl.pallas_call(
        kern, grid=(HEADS // 2, nq),
        in_specs=[pl.BlockSpec((tq, 2 * LANES), lambda p, i: (i, p)), pl.BlockSpec((s, 2 * LANES), lambda p, i: (0, p)),
                  pl.BlockSpec((s, LANES), lambda p, i: (0, p))] + [ANY] * n,
        out_specs=[pl.BlockSpec((tq, LANES), lambda p, i: (i, p)), pl.BlockSpec((1, 2, tq), lambda p, i: (p, 0, i))]
        + [ANY] * n,
        out_shape=[jax.ShapeDtypeStruct((s, MLA_W), F32), jax.ShapeDtypeStruct((HEADS // 2, 2, s), F32)]
        + _gather_shapes(shards),
        scratch_shapes=[pltpu.VMEM((2, LANES, tq), F32)] + (_gather_sems(n) if n else []),
        name="attn_fwd", compiler_params=_cparams(2))(q, k, v, *shards)


def _attn_bwd(q, k, v, do, lse, delta, tq, tk, contributions=()):
    s = q.shape[0]
    nq, nk = s // tq, s // tk
    n = len(contributions)

    def kern(*refs):
        q_ref, do_ref, lse_ref, dl_ref, k_ref, v_ref = refs[:6]
        dq_ref, dk_ref, dv_ref = refs[6 + n:9 + n]
        acc_ref = refs[9 + 2 * n]
        p_id, i_id = pl.program_id(0), pl.program_id(1)
        finish = _riding_exchange(lambda: _to_owner_copies(refs[6:6 + n], refs[9 + n:9 + 2 * n], *refs[10 + 2 * n:]),
                                  (p_id == 0) & (i_id == 0), (p_id == HEADS // 2 - 1) & (i_id == nq - 1)) if n else None
        _zero_first(pl.program_id(1), dk_ref, dv_ref)
        acc_ref[...] = jnp.zeros_like(acc_ref)
        qs = (q_ref[:, :LANES], q_ref[:, LANES:])
        doc = do_ref[...]
        lane_q = lax.broadcasted_iota(jnp.int32, (tq, LANES), 1)
        zq = jnp.zeros_like(doc)
        dos = (jnp.where(lane_q < V_DIM, doc, zq), jnp.where(lane_q >= V_DIM, doc, zq))
        lses = (lse_ref[0, 0:1, :], lse_ref[0, 1:2, :])
        dls = (dl_ref[0, 0:1, :], dl_ref[0, 1:2, :])

        def step(j, carry):
            off = pl.multiple_of(j * tk, tk)
            vp = v_ref[pl.ds(off, tk), :]
            lane_k = lax.broadcasted_iota(jnp.int32, (tk, LANES), 1)
            zero = jnp.zeros_like(vp)
            vs = (jnp.where(lane_k < V_DIM, vp, zero), jnp.where(lane_k >= V_DIM, vp, zero))
            for h in range(2):
                sl = slice(h * LANES, (h + 1) * LANES)
                st = lax.dot_general(k_ref[pl.ds(off, tk), sl], qs[h], NT_DIMS, preferred_element_type=F32)
                pt = jnp.exp2(st - lses[h])
                dpt = lax.dot_general(vs[h], doc, NT_DIMS, preferred_element_type=F32)
                dst = (pt * (dpt - dls[h])).astype(BF16)
                dv_ref[pl.ds(off, tk), :] += jnp.dot(pt.astype(BF16), dos[h], preferred_element_type=F32)
                dk_ref[pl.ds(off, tk), sl] += jnp.dot(dst, qs[h], preferred_element_type=F32)
                acc_ref[h] += lax.dot_general(k_ref[pl.ds(off, tk), sl], dst, TN_DIMS, preferred_element_type=F32)
            return carry

        lax.fori_loop(0, nk, step, 0)
        dq_ref[:, :LANES] = acc_ref[0].T
        dq_ref[:, LANES:] = acc_ref[1].T
        if n:
            finish()

    return pl.pallas_call(
        kern, grid=(HEADS // 2, nq),
        in_specs=[pl.BlockSpec((tq, 2 * LANES), lambda p, i: (i, p)), pl.BlockSpec((tq, LANES), lambda p, i: (i, p)),
                  pl.BlockSpec((1, 2, tq), lambda p, i: (p, 0, i)), pl.BlockSpec((1, 2, tq), lambda p, i: (p, 0, i)),
                  pl.BlockSpec((s, 2 * LANES), lambda p, i: (0, p)), pl.BlockSpec((s, LANES), lambda p, i: (0, p))]
        + [ANY] * n,
        out_specs=[pl.BlockSpec((tq, 2 * LANES), lambda p, i: (i, p)), pl.BlockSpec((s, 2 * LANES), lambda p, i: (0, p)),
                   pl.BlockSpec((s, LANES), lambda p, i: (0, p))] + [ANY] * n,
        out_shape=[jax.ShapeDtypeStruct((s, HEADS * LANES), F32), jax.ShapeDtypeStruct((s, HEADS * LANES), F32),
                   jax.ShapeDtypeStruct((s, MLA_W), F32)] + _to_owner_shapes(contributions),
        scratch_shapes=[pltpu.VMEM((2, LANES, tq), F32)] + (_to_owner_sems(n) if n else []),
        name="attn_bwd", compiler_params=_cparams(2))(q, do, lse, delta, k, v, *contributions)


def _mix_out(hf, hb, yg, o, x, g_lru, g_mla, w_out, tm):
    def body(i, n, hf_ref, hb_ref, yg_ref, o_ref, x_ref, gl, gm, w_ref, x1_ref, mix_ref):
        lo = (hf_ref[...] + hb_ref[...]) * _gelu(yg_ref[...])
        ov = o_ref[...]
        mix_ref[:, :LRU_W] = (lo * _rstd(lo) * gl[...]).astype(BF16)
        mix_ref[:, LRU_W:] = (ov * _rstd(ov) * gm[...]).astype(BF16)
        x1_ref[...] = x_ref[...] + jnp.dot(mix_ref[...], w_ref[...], preferred_element_type=F32)

    return _rowwise(body, "mix_out", x.shape[0], tm, rows=[hf, hb, yg, o, x], fulls=[g_lru, g_mla, w_out],
                    outs=[(D_MODEL, F32), (2 * LRU_W, BF16)])


def _mem_kv(mem, g_mem, w_kv, g_k):
    m = mem.shape[0]

    def body(i, n, mem_ref, g_ref, w_ref, gk_ref, km_ref, vm_ref):
        mv = mem_ref[...]
        kv = _bdot(mv * _rstd(mv) * g_ref[...], w_ref[...])
        vm_ref[...] = kv[:, MEM_W:].astype(BF16)
        for h in range(MEM_HEADS):
            sl = slice(h * MEM_HD, (h + 1) * MEM_HD)
            kh = kv[:, sl]
            km_ref[:, sl] = (kh * _rstd(kh) * gk_ref[...]).astype(BF16)

    return _rowwise(body, "mem_kv", m, m, rows=[mem], fulls=[g_mem, w_kv, g_k], outs=[(MEM_W, BF16), (MEM_W, BF16)])


def _mem_attn_core(x1v, g_ref, wq_ref, gq_ref, km_ref, vm_ref):
    scale = MEM_HD ** -0.5
    hm = (x1v * _rstd(x1v) * g_ref[...]).astype(BF16)
    qr = jnp.dot(hm, wq_ref[...], preferred_element_type=F32)
    heads = []
    for h in range(MEM_HEADS):
        sl = slice(h * MEM_HD, (h + 1) * MEM_HD)
        qh = qr[:, sl]
        rs = _rstd(qh)
        qn = (qh * rs * gq_ref[...]).astype(BF16)
        sc = lax.dot_general(qn, km_ref[:, sl], (((1,), (1,)), ((), ())), preferred_element_type=F32) * scale
        e = jnp.exp(sc - jnp.max(sc, axis=-1, keepdims=True))
        p = e / jnp.sum(e, axis=-1, keepdims=True)
        oh = jnp.dot(p.astype(BF16), vm_ref[:, sl], preferred_element_type=F32)
        heads.append((qh, rs, qn, p, oh))
    return hm, heads


def _mem_attn(x1, g, w_q, g_q, km, vm, w_o, tm):
    cs = D_MODEL // N_CHIPS

    def body(i, n, x1_ref, g_ref, wq_ref, gq_ref, km_ref, vm_ref, wo_ref, x2_ref, ob_ref):
        x1v = x1_ref[...]
        _, heads = _mem_attn_core(x1v, g_ref, wq_ref, gq_ref, km_ref, vm_ref)
        for h in range(MEM_HEADS):
            ob_ref[:, h * MEM_HD:(h + 1) * MEM_HD] = heads[h][4].astype(BF16)
        for k in range(N_CHIPS):
            sl = slice(k * cs, (k + 1) * cs)
            x2_ref[:, sl] = x1v[:, sl] + jnp.dot(ob_ref[...], wo_ref[k], preferred_element_type=F32)

    return _rowwise(body, "mem_attn", x1.shape[0], tm, rows=[x1], fulls=[g, w_q, g_q, km, vm, w_o],
                    outs=[(D_MODEL, F32), (MEM_W, BF16)])


def _ffn_up(x2, g, w_up, tm):
    cs = 2 * D_FF // N_CHIPS

    def body(i, n, x_ref, g_ref, w_ref, gu_ref, hb_ref):
        xv = x_ref[...]
        hb_ref[...] = (xv * _rstd(xv) * g_ref[...]).astype(BF16)
        for k in range(N_CHIPS):
            gu_ref[:, k * cs:(k + 1) * cs] = jnp.dot(hb_ref[...], w_ref[k], preferred_element_type=F32)

    return _rowwise(body, "ffn_up", x2.shape[0], tm, rows=[x2], fulls=[g, w_up], outs=[(2 * D_FF, F32), (D_MODEL, BF16)])


def _ffn_conv(gu, prev, nxt, cw_ref, i, n):
    prev = jnp.where(i == 0, 0.0, prev)
    nxt = jnp.where(i == n - 1, 0.0, nxt)
    return (cw_ref[3:4, :] + cw_ref[0:1, :] * _shift_down(gu, 1, prev) + cw_ref[1:2, :] * gu
            + cw_ref[2:3, :] * _shift_up(gu, 1, nxt))


def _ffn_down_loss(gu_pre, x2, target, cw, w_down, tm):
    def body(i, n, gu_ref, x_ref, t_ref, pv_ref, nx_ref, cw_ref, w_ref, dy_ref, dyb_ref, act_ref, guc_ref, loss_ref):
        _zero_first(i, loss_ref)
        gu = _ffn_conv(gu_ref[...], pv_ref[...], nx_ref[...], cw_ref, i, n)
        guc_ref[...] = gu
        g, u = gu[:, :D_FF], gu[:, D_FF:]
        act_ref[...] = (g * _sigmoid(g) * u).astype(BF16)
        y = x_ref[...] + jnp.dot(act_ref[...], w_ref[...], preferred_element_type=F32)
        e = y - t_ref[...]
        loss_ref[...] += jnp.sum(e * e)
        dy = e * (1.0 / D_MODEL)
        dy_ref[...] = dy
        dyb_ref[...] = dy.astype(BF16)

    return _rowwise(body, "ffn_down_loss", x2.shape[0], tm, rows=[gu_pre, x2, target], halos=[gu_pre], fulls=[cw, w_down],
                    outs=[(D_MODEL, F32), (D_MODEL, BF16), (D_FF, BF16), (2 * D_FF, F32)], accs=[((SUBLANES, LANES), F32)])


def _ffn_bwd_act(dyb, gu, w_down, tm):
    def body(i, n, dy_ref, gu_ref, w_ref, dgu_ref):
        d_act = lax.dot_general(dy_ref[...], w_ref[...], (((1,), (1,)), ((), ())), preferred_element_type=F32)
        g, u = gu_ref[:, :D_FF], gu_ref[:, D_FF:]
        sg = _sigmoid(g)
        a = g * sg
        dgu_ref[:, :D_FF] = (d_act * u) * (sg + a - a * sg)
        dgu_ref[:, D_FF:] = d_act * a

    return _rowwise(body, "ffn_bwd_act", dyb.shape[0], tm, rows=[dyb, gu], fulls=[w_down], outs=[(2 * D_FF, F32)])


def _ffn_bwd_conv(dgu, gu_pre, cw, tm):
    def body(i, n, d_ref, g_ref, dp_ref, dn_ref, cw_ref, dpre_ref, gc_ref):
        _zero_first(i, gc_ref)
        d = d_ref[...]
        g = g_ref[...]
        d_next = _shift_up(d, 1, jnp.where(i == n - 1, 0.0, dn_ref[...]))
        d_prev = _shift_down(d, 1, jnp.where(i == 0, 0.0, dp_ref[...]))
        dpre_ref[...] = (cw_ref[0:1, :] * d_next + cw_ref[1:2, :] * d + cw_ref[2:3, :] * d_prev).astype(BF16)
        _acc_row(gc_ref, 0, d_next * g)
        _acc_row(gc_ref, 1, d * g)
        _acc_row(gc_ref, 2, d_prev * g)
        _acc_row(gc_ref, 3, d)

    return _rowwise(body, "ffn_bwd_conv", dgu.shape[0], tm, rows=[dgu, gu_pre], halos=[dgu], fulls=[cw],
                    outs=[(2 * D_FF, BF16)], accs=[((SUBLANES, 2 * D_FF), F32)])


def _ffn_bwd_in(dpre, x2, dy, g, w_up, tm):
    cs = 2 * D_FF // N_CHIPS

    def body(i, n, dp_ref, x_ref, dy_ref, g_ref, w_ref, dx_ref, dxb_ref, gg_ref):
        _zero_first(i, gg_ref)
        d_h = jnp.zeros(x_ref.shape, F32)
        for k in range(N_CHIPS):
            d_h = d_h + lax.dot_general(dp_ref[:, k * cs:(k + 1) * cs], w_ref[k], (((1,), (1,)), ((), ())),
                                        preferred_element_type=F32)
        xv = x_ref[...]
        dx, dg = _norm_bwd(xv, _rstd(xv), g_ref[...], d_h)
        _acc_row(gg_ref, 0, dg)
        dx = dx + dy_ref[...]
        dx_ref[...] = dx
        dxb_ref[...] = dx.astype(BF16)

    return _rowwise(body, "ffn_bwd_in", x2.shape[0], tm, rows=[dpre, x2, dy], fulls=[g, w_up],
                    outs=[(D_MODEL, F32), (D_MODEL, BF16)], accs=[((SUBLANES, D_MODEL), F32)])


def _mem_attn_bwd(x1, dx2, dx2b, g, w_q, g_q, km, vm, w_o, tm):
    scale = MEM_HD ** -0.5
    m = km.shape[0]

    def body(i, n, x1_ref, dx2_ref, dx2b_ref, g_ref, wq_ref, gq_ref, km_ref, vm_ref, wo_ref,
             dx1_ref, dx1b_ref, hm_ref, dqr_ref, dkm_ref, dvm_ref, gg_ref, ggq_ref):
        _zero_first(i, dkm_ref, dvm_ref, gg_ref, ggq_ref)
        x1v = x1_ref[...]
        hm, heads = _mem_attn_core(x1v, g_ref, wq_ref, gq_ref, km_ref, vm_ref)
        hm_ref[...] = hm
        cs = D_MODEL // N_CHIPS
        d_o = jnp.zeros((x1v.shape[0], MEM_W), F32)
        for k in range(N_CHIPS):
            d_o = d_o + lax.dot_general(dx2b_ref[:, k * cs:(k + 1) * cs], wo_ref[k], (((1,), (1,)), ((), ())),
                                        preferred_element_type=F32)
        for h in range(MEM_HEADS):
            sl = slice(h * MEM_HD, (h + 1) * MEM_HD)
            qh, rs, qn, p, _ = heads[h]
            d_oh = d_o[:, sl].astype(BF16)
            dp = lax.dot_general(d_oh, vm_ref[:, sl], (((1,), (1,)), ((), ())), preferred_element_type=F32)
            ds = (p * (dp - jnp.sum(dp * p, axis=-1, keepdims=True)) * scale).astype(BF16)
            dqn = jnp.dot(ds, km_ref[:, sl], preferred_element_type=F32)
            dkm_ref[:, sl] += lax.dot_general(ds, qn, (((0,), (0,)), ((), ())), preferred_element_type=F32)
            dvm_ref[:, sl] += lax.dot_general(p.astype(BF16), d_oh, (((0,), (0,)), ((), ())), preferred_element_type=F32)
            dqh, dgq = _norm_bwd(qh, rs, gq_ref[...], dqn)
            _acc_row(ggq_ref, 0, dgq)
            dqr_ref[:, sl] = dqh.astype(BF16)
        d_hm = lax.dot_general(dqr_ref[...], wq_ref[...], (((1,), (1,)), ((), ())), preferred_element_type=F32)
        dx, dg = _norm_bwd(x1v, _rstd(x1v), g_ref[...], d_hm)
        _acc_row(gg_ref, 0, dg)
        dx = dx + dx2_ref[...]
        dx1_ref[...] = dx
        dx1b_ref[...] = dx.astype(BF16)

    return _rowwise(body, "mem_attn_bwd", x1.shape[0], tm, rows=[x1, dx2, dx2b], fulls=[g, w_q, g_q, km, vm, w_o],
                    outs=[(D_MODEL, F32), (D_MODEL, BF16), (D_MODEL, BF16), (MEM_W, BF16)],
                    accs=[((m, MEM_W), F32), ((m, MEM_W), F32), ((SUBLANES, D_MODEL), F32), ((SUBLANES, MEM_HD), F32)])


def _mem_kv_bwd(mem, g_mem, w_kv, g_k, dkm, dvm):
    m = mem.shape[0]

    def body(i, n, mem_ref, dkm_ref, dvm_ref, g_ref, w_ref, gk_ref, gw_ref, gg_ref, ggk_ref, dkv_ref):
        gg_ref[...] = jnp.zeros_like(gg_ref)
        ggk_ref[...] = jnp.zeros_like(ggk_ref)
        mv = mem_ref[...]
        rs_m = _rstd(mv)
        mem_n = (mv * rs_m * g_ref[...]).astype(BF16)
        kv = jnp.dot(mem_n, w_ref[...], preferred_element_type=F32)
        for h in range(MEM_HEADS):
            sl = slice(h * MEM_HD, (h + 1) * MEM_HD)
            kh = kv[:, sl]
            dkh, dgk = _norm_bwd(kh, _rstd(kh), gk_ref[...], dkm_ref[:, sl])
            _acc_row(ggk_ref, 0, dgk)
            dkv_ref[:, sl] = dkh.astype(BF16)
        dkv_ref[:, MEM_W:] = dvm_ref[...].astype(BF16)
        gw_ref[...] = lax.dot_general(mem_n, dkv_ref[...], (((0,), (0,)), ((), ())), preferred_element_type=F32)
        d_mn = lax.dot_general(dkv_ref[...], w_ref[...], (((1,), (1,)), ((), ())), preferred_element_type=F32)
        _acc_row(gg_ref, 0, d_mn * (mv * rs_m))

    return _rowwise(body, "mem_kv_bwd", m, m, rows=[mem, dkm, dvm], fulls=[g_mem, w_kv, g_k],
                    accs=[((D_MODEL, 2 * MEM_W), F32), ((SUBLANES, D_MODEL), F32), ((SUBLANES, MEM_HD), F32),
                          ((m, 2 * MEM_W), BF16)])


def _mix_out_bwd(dx1b, hf, hb, yg, o, g_lru, g_mla, w_out, tm):
    def body(i, n, dx_ref, hf_ref, hb_ref, yg_ref, o_ref, gl, gm, w_ref, dh_ref, dyg_ref, dob_ref, dl_ref, ggl_ref, ggm_ref):
        _zero_first(i, ggl_ref, ggm_ref)
        dmix = lax.dot_general(dx_ref[...], w_ref[...], (((1,), (1,)), ((), ())), preferred_element_type=F32)
        hs = hf_ref[...] + hb_ref[...]
        ygv = yg_ref[...]
        ge = _gelu(ygv)
        lo = hs * ge
        d_lo, dgl = _norm_bwd(lo, _rstd(lo), gl[...], dmix[:, :LRU_W])
        _acc_row(ggl_ref, 0, dgl)
        dh_ref[...] = d_lo * ge
        dyg_ref[...] = d_lo * hs * _gelu_grad(ygv)
        ov = o_ref[...]
        d_o, dgm = _norm_bwd(ov, _rstd(ov), gm[...], dmix[:, LRU_W:])
        _acc_row(ggm_ref, 0, dgm)
        dob_ref[...] = d_o.astype(BF16)
        prod = d_o * ov
        lane_w = lax.broadcasted_iota(jnp.int32, prod.shape, 1)
        lane = lax.broadcasted_iota(jnp.int32, (prod.shape[0], LANES), 1)
        dl = jnp.zeros((prod.shape[0], LANES), F32)
        for h in range(HEADS):
            in_head = (lane_w >= h * V_DIM) & (lane_w < (h + 1) * V_DIM)
            dl = dl + jnp.where(lane == h, jnp.sum(jnp.where(in_head, prod, 0.0), axis=-1, keepdims=True), 0.0)
        dl_ref[...] = dl

    return _rowwise(body, "mix_out_bwd", dx1b.shape[0], tm, rows=[dx1b, hf, hb, yg, o], fulls=[g_lru, g_mla, w_out],
                    outs=[(LRU_W, F32), (LRU_W, F32), (MLA_W, BF16), (LANES, F32)],
                    accs=[((SUBLANES, LRU_W), F32), ((SUBLANES, MLA_W), F32)])


def _mla_qkv_bwd(cq, ckv, krp, cos_t, sin_t, dq, dk, dv, g_qa, g_kva, g_qn, g_kn, w_uq_p, w_uk_p, w_uv, tm):
    scale = QK_HEAD ** -0.5

    def body(i, n, cq_ref, ckv_ref, kr_ref, c_ref, s_ref, dq_ref, dk_ref, dv_ref, gqa, gkva, gqn, gkn, wq, wk, wv,
             dcq_ref, dckv_ref, dkr_ref, cqb_ref, dqr_ref, ckvb_ref, dkn_ref, dvb_ref, ggqa, ggkva, ggqn, ggkn):
        _zero_first(i, ggqa, ggkva, ggqn, ggkn)
        cosv, sinv = c_ref[...], s_ref[...]
        cqv = cq_ref[...]
        rs_q = _rstd(cqv)
        cqb_ref[...] = (cqv * rs_q * gqa[...]).astype(BF16)
        qr = jnp.dot(cqb_ref[...], wq[...], preferred_element_type=F32)
        ckvv = ckv_ref[...]
        rs_kv = _rstd(ckvv)
        ckvb_ref[...] = (ckvv * rs_kv * gkva[...]).astype(BF16)
        kn = jnp.dot(ckvb_ref[...], wk[...], preferred_element_type=F32)
        kr = kr_ref[...]
        dkr = jnp.zeros_like(kr)
        for h in range(HEADS):
            sl = slice(h * LANES, (h + 1) * LANES)
            qh = qr[:, sl]
            d_qn = _rope_t(dq_ref[:, sl] * scale, cosv, sinv)
            dqh, dgq = _norm_bwd(qh, _rstd(qh, QK_HEAD), gqn[...], d_qn, QK_HEAD)
            _acc_row(ggqn, 0, dgq)
            dqr_ref[:, sl] = dqh.astype(BF16)
            kh = kn[:, sl] + kr
            d_kn = _rope_t(dk_ref[:, sl] * (1.0 / LOG2E), cosv, sinv)
            dkh, dgk = _norm_bwd(kh, _rstd(kh, QK_HEAD), gkn[...], d_kn, QK_HEAD)
            _acc_row(ggkn, 0, dgk)
            dkn_ref[:, sl] = dkh.astype(BF16)
            dkr = dkr + dkh
        dkr_ref[...] = dkr
        dvb_ref[...] = dv_ref[...].astype(BF16)
        d_cq = lax.dot_general(dqr_ref[...], wq[...], (((1,), (1,)), ((), ())), preferred_element_type=F32)
        dcq, dg = _norm_bwd(cqv, rs_q, gqa[...], d_cq)
        _acc_row(ggqa, 0, dg)
        dcq_ref[...] = dcq
        d_ckv = (lax.dot_general(dkn_ref[...], wk[...], (((1,), (1,)), ((), ())), preferred_element_type=F32)
                 + lax.dot_general(dvb_ref[...], wv[...], (((1,), (1,)), ((), ())), preferred_element_type=F32))
        dckv, dg = _norm_bwd(ckvv, rs_kv, gkva[...], d_ckv)
        _acc_row(ggkva, 0, dg)
        dckv_ref[...] = dckv

    return _rowwise(body, "mla_qkv_bwd", cq.shape[0], tm, rows=[cq, ckv, krp, cos_t, sin_t, dq, dk, dv],
                    fulls=[g_qa, g_kva, g_qn, g_kn, w_uq_p, w_uk_p, w_uv],
                    outs=[(Q_LORA, F32), (KV_LORA, F32), (LANES, F32), (Q_LORA, BF16), (HEADS * LANES, BF16),
                          (KV_LORA, BF16), (HEADS * LANES, BF16), (MLA_W, BF16)],
                    accs=[((SUBLANES, Q_LORA), F32), ((SUBLANES, KV_LORA), F32), ((SUBLANES, LANES), F32),
                          ((SUBLANES, LANES), F32)])


def _in_proj_bwd(x, dx1, dxr_f, dxr_b, dyg, dcq, dckv, dkrp, g, w_in_p, tm):
    def body(i, n, x_ref, dx1_ref, df_ref, db_ref, dyg_ref, dcq_ref, dckv_ref, dkr_ref, g_ref, w_ref, gx_ref, dp_ref, gg_ref):
        _zero_first(i, gg_ref)
        dp_ref[:, :LRU_W] = (df_ref[...] + db_ref[...]).astype(BF16)
        dp_ref[:, LRU_W:2 * LRU_W] = dyg_ref[...].astype(BF16)
        dp_ref[:, 2 * LRU_W:2 * LRU_W + Q_LORA] = dcq_ref[...].astype(BF16)
        dp_ref[:, 2 * LRU_W + Q_LORA:OFF_KR] = dckv_ref[...].astype(BF16)
        dp_ref[:, OFF_KR:] = dkr_ref[...].astype(BF16)
        d_h = lax.dot_general(dp_ref[...], w_ref[...], (((1,), (1,)), ((), ())), preferred_element_type=F32)
        xv = x_ref[...]
        dx, dg = _norm_bwd(xv, _rstd(xv), g_ref[...], d_h)
        _acc_row(gg_ref, 0, dg)
        gx_ref[...] = dx + dx1_ref[...]

    return _rowwise(body, "in_proj_bwd", x.shape[0], tm, rows=[x, dx1, dxr_f, dxr_b, dyg, dcq, dckv, dkrp],
                    fulls=[g, w_in_p], outs=[(D_MODEL, F32), (IN_PAD, BF16)], accs=[((SUBLANES, D_MODEL), F32)])


ANY = pl.BlockSpec(memory_space=pl.ANY)


def _chip_peers(x, y):
    return ((1 - x, y), (x, 1 - y), (1 - x, 1 - y))


def _exchange_call(kern, name, ins, out_shapes, n_sems, aliases=None):
    return pl.pallas_call(
        kern, in_specs=[ANY] * len(ins), out_specs=[ANY] * len(out_shapes), out_shape=out_shapes,
        scratch_shapes=[pltpu.SemaphoreType.DMA((n,)) for n in n_sems], input_output_aliases=aliases or {},
        name=name)(*ins)


def _start_then_wait(copies):
    for cp in copies:
        cp.start()
    for cp in copies:
        cp.wait()


N_DEV = 8
RELATIONS = tuple((dx, dy, dc) for dx in (0, 1) for dy in (0, 1) for dc in (0, 1))[1:]


def _flip(v, d):
    return 1 - v if d else v


def _gather_copies(ins, outs, ssem, rsem, lsem):
    x, y, c = lax.axis_index("x"), lax.axis_index("y"), lax.axis_index("c")
    me = 2 * x + y
    cps = []
    for i, (a, o) in enumerate(zip(ins, outs)):
        cps.append(pltpu.make_async_copy(a, o.at[me], lsem.at[i]))
        for j, (px, py) in enumerate(_chip_peers(x, y)):
            cps.append(pltpu.make_async_remote_copy(a, o.at[me], ssem.at[3 * i + j], rsem.at[3 * i + j],
                                                    device_id=(px, py, c), device_id_type=MESH))
    return cps


def _gather_shapes(arrs):
    return [jax.ShapeDtypeStruct((N_CHIPS,) + a.shape, a.dtype) for a in arrs]


def _gather_sems(n):
    return [pltpu.SemaphoreType.DMA((3 * n,)), pltpu.SemaphoreType.DMA((3 * n,)), pltpu.SemaphoreType.DMA((n,))]


def _gather_chips(arrs):
    n = len(arrs)

    def kern(*refs):
        _start_then_wait(_gather_copies(refs[:n], refs[n:2 * n], *refs[2 * n:]))

    return _exchange_call(kern, "gather_weights", arrs, _gather_shapes(arrs), (3 * n, 3 * n, n))


def _to_owner_copies(ins, outs, ssem, rsem, lsem):
    x, y, c = lax.axis_index("x"), lax.axis_index("y"), lax.axis_index("c")
    me = 4 * x + 2 * y + c
    cps = []
    for i, (a, o) in enumerate(zip(ins, outs)):
        cps.append(pltpu.make_async_copy(a.at[2 * x + y, c], o.at[me], lsem.at[i]))
        for r, (dx, dy, dc) in enumerate(RELATIONS):
            tx, ty, tc = _flip(x, dx), _flip(y, dy), _flip(c, dc)
            cps.append(pltpu.make_async_remote_copy(a.at[2 * tx + ty, tc], o.at[me], ssem.at[7 * i + r], rsem.at[7 * i + r],
                                                    device_id=(tx, ty, tc), device_id_type=MESH))
    return cps


def _to_owner_shapes(arrs):
    return [jax.ShapeDtypeStruct((N_DEV,) + a.shape[2:], a.dtype) for a in arrs]


def _to_owner_sems(n):
    return [pltpu.SemaphoreType.DMA((7 * n,)), pltpu.SemaphoreType.DMA((7 * n,)), pltpu.SemaphoreType.DMA((n,))]


def _to_owner(arrs, name):
    n = len(arrs)

    def kern(*refs):
        _start_then_wait(_to_owner_copies(refs[:n], refs[n:2 * n], *refs[2 * n:]))

    return _exchange_call(kern, name, arrs, _to_owner_shapes(arrs), (7 * n, 7 * n, n))


def _join_halves(arrs):
    n = len(arrs)

    def kern(*refs):
        outs, (ssem, rsem) = refs[n:2 * n], refs[2 * n:]
        x, y, c = lax.axis_index("x"), lax.axis_index("y"), lax.axis_index("c")
        _start_then_wait([
            pltpu.make_async_remote_copy(outs[i].at[c], outs[i].at[c], ssem.at[i], rsem.at[i],
                                         device_id=(x, y, 1 - c), device_id_type=MESH) for i in range(n)])

    outs = [jax.ShapeDtypeStruct(a.shape, a.dtype) for a in arrs]
    return _exchange_call(kern, "grad_join_halves", arrs, outs, (n, n), aliases={i: i for i in range(n)})


def _row_block(rows, row_bytes, limit=1 << 20):
    best = None
    for d in range(16, rows + 1, 16):
        if rows % d == 0 and d * row_bytes <= limit:
            best = d
    return best if best is not None else rows


def _sum_devices(b, c, name):
    _, h, cols = b.shape
    hb = _row_block(h, cols * 4)

    def kern(c_ref, b_ref, o_ref):
        acc = b_ref[0].astype(F32)
        for j in range(1, N_DEV):
            acc = acc + b_ref[j].astype(F32)
        o_ref[...] = acc

    return pl.pallas_call(
        kern,
        grid_spec=pltpu.PrefetchScalarGridSpec(
            num_scalar_prefetch=1, grid=(h // hb,),
            in_specs=[pl.BlockSpec((N_DEV, hb, cols), lambda i, c_ref: (0, i, 0))],
            out_specs=pl.BlockSpec((None, hb, cols), lambda i, c_ref: (c_ref[0], i, 0))),
        out_shape=jax.ShapeDtypeStruct((2, h, cols), F32), name=name, compiler_params=_cparams(1))(c, b)


def _adamw(w, g, m, v, name):
    rows, cols = w.shape
    rb = _row_block(rows, cols * 4)
    c1 = 1.0 - ADAM_B1 ** ADAM_STEP
    c2 = 1.0 - ADAM_B2 ** ADAM_STEP

    def kern(w_ref, g_ref, m_ref, v_ref, d_ref, mo_ref, vo_ref):
        gv = g_ref[...]
        mn = ADAM_B1 * m_ref[...] + (1.0 - ADAM_B1) * gv
        vn = ADAM_B2 * v_ref[...] + (1.0 - ADAM_B2) * (gv * gv)
        mo_ref[...] = mn
        vo_ref[...] = vn
        d_ref[...] = (-ADAM_LR) * ((mn / c1) / (jnp.sqrt(vn / c2) + ADAM_EPS) + ADAM_WD * w_ref[...])

    spec = pl.BlockSpec((rb, cols), lambda i: (i, 0))
    return pl.pallas_call(
        kern, grid=(rows // rb,), in_specs=[spec] * 4, out_specs=[spec] * 3,
        out_shape=[jax.ShapeDtypeStruct(w.shape, F32)] * 3, name=name, compiler_params=_cparams(1))(w, g, m, v)


def _pad_rows(flat, rows):
    return jnp.pad(flat, (0, rows * LANES - flat.shape[0])).reshape(rows, LANES)


def _round_up(n, m):
    return (n + m - 1) // m * m


def _shard_shape(shape, axis):
    return tuple(s // N_CHIPS if a == axis else s for a, s in enumerate(shape))


def _to_shards(full, axis):
    shape = full.shape
    t = full.reshape(shape[:axis] + (N_CHIPS, shape[axis] // N_CHIPS) + shape[axis + 1:])
    return jnp.moveaxis(t, axis, 0).reshape(N_CHIPS, -1)


def _from_shards(sh, shape, axis):
    t = sh.reshape((N_CHIPS,) + _shard_shape(shape, axis))
    t = jnp.moveaxis(t, 0, axis)
    return t.reshape(shape)


BIG = tuple((name, shape, axis) for name, shape, axis, big in SHARDED if big)
EARLY_WEIGHTS = ("w_in", "w_uq", "w_ukv")
SMALL_SHARDED = tuple((name, shape, axis) for name, shape, axis, big in SHARDED if not big)


def _pack_small_weights(p):
    flat = jnp.concatenate([p[name].reshape(-1) for name, _, _ in SMALL_SHARDED])
    return _pad_rows(flat, _round_up(-(-flat.shape[0] // LANES), SUBLANES))


def _unpack_small_weights(gathered):
    flat = gathered.reshape(N_CHIPS, -1)
    out, off = {}, 0
    for name, shape, axis in SMALL_SHARDED:
        n = _numel(shape) // N_CHIPS
        out[name] = _from_shards(flat[:, off:off + n], shape, axis)
        off += n
    return out


def _pack_small_local(p, prefix=""):
    parts = [p[prefix + name].reshape(-1) for name, _, _ in SMALL_SHARDED]
    parts += [p[prefix + name].reshape(-1) for name, _ in REPLICATED]
    return jnp.concatenate(parts)


def _pack_small_grads(g):
    parts = [_to_shards(g[name], axis) for name, _, axis in SMALL_SHARDED]
    rep = jnp.concatenate([g[name].reshape(-1) for name, _ in REPLICATED])
    parts.append(jnp.broadcast_to(rep[None], (N_CHIPS, rep.shape[0])))
    return jnp.concatenate(parts, axis=1)


def _unpack_small_local(flat):
    out, off = {}, 0
    for name, shape, axis in SMALL_SHARDED:
        n = _numel(shape) // N_CHIPS
        out[name] = flat[off:off + n].reshape((1,) + _shard_shape(shape, axis))
        off += n
    for name, shape in REPLICATED:
        n = _numel(shape)
        out[name] = flat[off:off + n].reshape((1,) + shape)
        off += n
    return out


def _grad_shards(g, shape, axis):
    if axis == 0:
        return g.reshape((N_CHIPS,) + _shard_shape(shape, axis))
    return jnp.transpose(g.reshape(shape[0], N_CHIPS, shape[1] // N_CHIPS), (1, 0, 2))


def _cols_from_shards(w4):
    return jnp.transpose(w4, (1, 0, 2)).reshape(w4.shape[1], -1)


def _block_diag(w):
    eye = jnp.eye(LRU_BLOCKS, dtype=w.dtype)
    return jnp.einsum("ncd,nm->ncmd", w, eye).reshape(LRU_W, LRU_W)


def _block_diag_t(g):
    g4 = g.reshape(LRU_BLOCKS, 64, LRU_BLOCKS, 64)
    return jnp.stack([g4[n, :, n, :] for n in range(LRU_BLOCKS)])


def _pad8(a):
    return jnp.pad(a, ((0, SUBLANES - a.shape[0]), (0, 0)))


def kernel(x, mem, positions, attn_norm, w_in, lru_conv_w, lru_conv_b, lru_w_a, lru_b_a, lru_w_i, lru_b_i, lru_lambda, q_a_norm, w_uq, kv_a_norm, w_ukv, mla_q_norm, mla_k_norm, lru_out_norm, mla_out_norm, w_out, mem_attn_norm, mem_norm, w_mem_q, w_mem_kv, mem_q_norm, mem_k_norm, w_mem_o, ffn_norm, w_up, ffn_conv_w, ffn_conv_b, w_down, loss_target, m_attn_norm, m_w_in, m_lru_conv_w, m_lru_conv_b, m_lru_w_a, m_lru_b_a, m_lru_w_i, m_lru_b_i, m_lru_lambda, m_q_a_norm, m_w_uq, m_kv_a_norm, m_w_ukv, m_mla_q_norm, m_mla_k_norm, m_lru_out_norm, m_mla_out_norm, m_w_out, m_mem_attn_norm, m_mem_norm, m_w_mem_q, m_w_mem_kv, m_mem_q_norm, m_mem_k_norm, m_w_mem_o, m_ffn_norm, m_w_up, m_ffn_conv_w, m_ffn_conv_b, m_w_down, v_attn_norm, v_w_in, v_lru_conv_w, v_lru_conv_b, v_lru_w_a, v_lru_b_a, v_lru_w_i, v_lru_b_i, v_lru_lambda, v_q_a_norm, v_w_uq, v_kv_a_norm, v_w_ukv, v_mla_q_norm, v_mla_k_norm, v_lru_out_norm, v_mla_out_norm, v_w_out, v_mem_attn_norm, v_mem_norm, v_w_mem_q, v_w_mem_kv, v_mem_q_norm, v_mem_k_norm, v_w_mem_o, v_ffn_norm, v_w_up, v_ffn_conv_w, v_ffn_conv_b, v_w_down):
    given = dict(locals())
    local = {name: given[name][0] for name in WEIGHT_ORDER}
    s = x.shape[1]
    x2d, mem2d, tgt = x[0], mem[0], loss_target[0]
    tm = min(256, s)
    tm_ffn = min(128, s)
    t_scan = min(256, s)
    tq_f, tq_b, tk = min(4096, s), min(1024, s), min(512, s)

    early = [b for b in BIG if b[0] in EARLY_WEIGHTS]
    late = [b for b in BIG if b[0] not in EARLY_WEIGHTS]
    got = _gather_chips([local[name].astype(BF16) for name, _, _ in early] + [_pack_small_weights(local)])
    full = _unpack_small_weights(got[-1])

    def take_gathered(entries, arrays):
        for (name, shape, axis), w4 in zip(entries, arrays):
            if axis == 0:
                full[name] = w4.reshape(shape)
            elif name in ("w_up", "w_mem_o"):
                full[name] = w4
            else:
                full[name] = _cols_from_shards(w4)

    take_gathered(early, got)
    row = lambda a: a.reshape(1, -1)
    b16 = lambda a: a.astype(BF16)
    zeros = lambda r, c: jnp.zeros((r, c), BF16)
    w_in_f = full["w_in"]
    w_in_p = jnp.concatenate([w_in_f[:, :OFF_KR], zeros(D_MODEL, QK_NOPE), w_in_f[:, OFF_KR:],
                              zeros(D_MODEL, LANES - QK_HEAD)], axis=1)
    w_uq_p = jnp.pad(full["w_uq"].reshape(Q_LORA, HEADS, QK_HEAD), ((0, 0), (0, 0), (0, LANES - QK_HEAD))).reshape(Q_LORA, -1)
    ukv = full["w_ukv"].reshape(KV_LORA, HEADS, QK_NOPE + V_DIM)
    w_uk_p = jnp.pad(ukv[:, :, :QK_NOPE], ((0, 0), (0, 0), (0, LANES - QK_NOPE))).reshape(KV_LORA, -1)
    w_uv = ukv[:, :, QK_NOPE:].reshape(KV_LORA, MLA_W)
    wa = [b16(_block_diag(local["lru_w_a"][d])) for d in range(2)]
    wi = [b16(_block_diag(local["lru_w_i"][d])) for d in range(2)]
    cw = [_pad8(full["lru_conv_w"][d]) for d in range(2)]
    pv = [_pad8(jnp.stack([full["lru_conv_b"][d], full["lru_b_a"][d], full["lru_b_i"][d], full["lru_lambda"][d]]))
          for d in range(2)]
    ffn_cw = _pad8(jnp.concatenate([full["ffn_conv_w"], row(local["ffn_conv_b"])], axis=0))
    g_attn, g_qa, g_kva = row(local["attn_norm"]), row(local["q_a_norm"]), row(local["kv_a_norm"])
    g_qn = jnp.pad(row(local["mla_q_norm"]), ((0, 0), (0, LANES - QK_HEAD)))
    g_kn = jnp.pad(row(local["mla_k_norm"]), ((0, 0), (0, LANES - QK_HEAD)))
    g_lru, g_mla = row(local["lru_out_norm"]), row(local["mla_out_norm"])
    g_memattn, g_mem = row(local["mem_attn_norm"]), row(local["mem_norm"])
    g_mq, g_mk, g_ffn = row(local["mem_q_norm"]), row(local["mem_k_norm"]), row(local["ffn_norm"])

    inv = ROPE_THETA ** (-jnp.arange(0, QK_ROPE, 2, dtype=F32) / QK_ROPE)
    ang = positions[0].astype(F32)[:, None] * inv
    cosv, sinv = jnp.cos(ang), jnp.sin(ang)
    ones, zer = jnp.ones((s, QK_NOPE), F32), jnp.zeros((s, LANES - QK_HEAD), F32)
    cos_t = jnp.concatenate([ones, cosv, cosv, zer + 1.0], axis=1)
    sin_t = jnp.concatenate([ones * 0.0, -sinv, sinv, zer], axis=1)

    xr, yg, cq, ckv, krp, hb_in = _in_proj(x2d, g_attn, w_in_p, tm)
    h_f = _lru_scan_fwd(xr, cw[0], pv[0], wa[0], wi[0], False, t_scan)
    h_b = _lru_scan_fwd(xr, cw[1], pv[1], wa[1], wi[1], True, t_scan)
    q, k, v = _mla_qkv(cq, ckv, krp, cos_t, sin_t, g_qa, g_kva, g_qn, g_kn, w_uq_p, w_uk_p, w_uv, tm)
    o, lse, *got = _attn_fwd(q, k, v, tq_f, tk, shards=[local[name].astype(BF16) for name, _, _ in late])
    take_gathered(late, got)
    x1, mixed = _mix_out(h_f, h_b, yg, o, x2d, g_lru, g_mla, full["w_out"], tm)
    km, vm = _mem_kv(mem2d, g_mem, full["w_mem_kv"], g_mk)
    x2, o_mem = _mem_attn(x1, g_memattn, full["w_mem_q"], g_mq, km, vm, full["w_mem_o"], tm)
    gu_pre, hb_ffn = _ffn_up(x2, g_ffn, full["w_up"], tm)
    dy, dyb, act, gu_conv, loss_acc = _ffn_down_loss(gu_pre, x2, tgt, ffn_cw, full["w_down"], tm_ffn)
    loss = lax.psum(loss_acc[0, 0] * (0.5 / D_MODEL), ("x", "y", "c"))

    grads = {}
    grads["w_down"] = _matmul_tn(act, dyb, "grad_w_down", out_dtype=BF16)
    (dgu,) = _ffn_bwd_act(dyb, gu_conv, full["w_down"], tm_ffn)
    dpre, g_conv = _ffn_bwd_conv(dgu, gu_pre, ffn_cw, tm_ffn)
    grads["ffn_conv_w"], grads["ffn_conv_b"] = g_conv[:3], g_conv[3]
    grads["w_up"] = _matmul_tn(hb_ffn, dpre, "grad_w_up", col_shards=True, out_dtype=BF16)
    dx2, dx2b, gg = _ffn_bwd_in(dpre, x2, dy, g_ffn, full["w_up"], tm)
    grads["ffn_norm"] = gg[0]
    grads["w_mem_o"] = _matmul_tn(o_mem, dx2b, "grad_w_mem_o", col_shards=True, out_dtype=BF16)
    dx1, dx1b, hm, dqr_mem, dkm, dvm, gg, ggq = _mem_attn_bwd(x1, dx2, dx2b, g_memattn, full["w_mem_q"], g_mq, km, vm,
                                                                 full["w_mem_o"], tm)
    grads["mem_attn_norm"], grads["mem_q_norm"] = gg[0], ggq[0]
    grads["w_mem_q"] = _matmul_tn(hm, dqr_mem, "grad_w_mem_q", out_dtype=BF16)
    g_mem_kv, gg, ggk, _ = _mem_kv_bwd(mem2d, g_mem, full["w_mem_kv"], g_mk, dkm, dvm)
    grads["w_mem_kv"] = g_mem_kv.astype(BF16)
    grads["mem_norm"], grads["mem_k_norm"] = gg[0], ggk[0]
    grads["w_out"] = _matmul_tn(mixed, dx1b, "grad_w_out", out_dtype=BF16)
    dh, dyg, dob, dl128, ggl, ggm = _mix_out_bwd(dx1b, h_f, h_b, yg, o, g_lru, g_mla, full["w_out"], tm)
    grads["lru_out_norm"], grads["mla_out_norm"] = ggl[0], ggm[0]
    delta_t = jnp.transpose(dl128[:, :HEADS]).reshape(HEADS // 2, 2, s)
    def halves(name, shape, axis):
        g4 = grads[name] if grads[name].ndim == 3 else _grad_shards(grads[name], shape, axis)
        return g4.reshape(N_CHIPS, 2, g4.shape[1] // 2, g4.shape[2])

    dq, dk, dv, *arrived_late = _attn_bwd(q, k, v, dob, lse, delta_t, tq_b, tk,
                                          contributions=[halves(*e) for e in late])
    (dcq, dckv, dkrp, cqb, dqr, ckvb, dkn, dvb, ggqa, ggkva, ggqn, ggkn) = _mla_qkv_bwd(
        cq, ckv, krp, cos_t, sin_t, dq, dk, dv, g_qa, g_kva, g_qn, g_kn, w_uq_p, w_uk_p, w_uv, tm)
    grads["q_a_norm"], grads["kv_a_norm"] = ggqa[0], ggkva[0]
    grads["mla_q_norm"], grads["mla_k_norm"] = ggqn[0, :QK_HEAD], ggkn[0, :QK_HEAD]
    g_uq_p = _matmul_tn(cqb, dqr, "grad_w_uq")
    grads["w_uq"] = g_uq_p.reshape(Q_LORA, HEADS, LANES)[:, :, :QK_HEAD].reshape(Q_LORA, -1)
    g_uk_p = _matmul_tn(ckvb, dkn, "grad_w_uk").reshape(KV_LORA, HEADS, LANES)[:, :, :QK_NOPE]
    g_uv = _matmul_tn(ckvb, dvb, "grad_w_uv").reshape(KV_LORA, HEADS, V_DIM)
    grads["w_ukv"] = jnp.concatenate([g_uk_p, g_uv], axis=2).reshape(KV_LORA, -1)
    dxr, gwa, gwi, gvec = [], [], [], []
    for d, hd in enumerate((h_f, h_b)):
        r = _lru_scan_bwd(xr, hd, dh, cw[d], pv[d], wa[d], wi[d], d == 1, t_scan)
        dxr.append(r[0])
        gwa.append(_block_diag_t(r[1]))
        gwi.append(_block_diag_t(r[2]))
        gvec.append(r[3])
    grads["lru_w_a"], grads["lru_w_i"] = jnp.stack(gwa), jnp.stack(gwi)
    grads["lru_conv_w"] = jnp.stack([gv[:CONV_W] for gv in gvec])
    for r_i, name in ((4, "lru_conv_b"), (5, "lru_b_a"), (6, "lru_b_i"), (7, "lru_lambda")):
        grads[name] = jnp.stack([gv[r_i] for gv in gvec])
    grad_x, dproj, gg = _in_proj_bwd(x2d, dx1, dxr[0], dxr[1], dyg, dcq, dckv, dkrp, g_attn, w_in_p, tm)
    grads["attn_norm"] = gg[0]
    g_in_p = _matmul_tn(hb_in, dproj, "grad_w_in")
    grads["w_in"] = jnp.concatenate([g_in_p[:, :OFF_KR], g_in_p[:, OFF_KR + QK_NOPE:OFF_KR + QK_HEAD]], axis=1)

    small = _pack_small_grads(grads)
    length = small.shape[1]
    hrows = _round_up(-(-length // (2 * LANES)), 16)
    small = jnp.pad(small, ((0, 0), (0, 2 * hrows * LANES - length))).reshape(N_CHIPS, 2, hrows, LANES)
    for name, _, _ in early:
        grads[name] = grads[name].astype(BF16)
    arrived_early = _to_owner([halves(*e) for e in early] + [small], "grad_to_owner")
    names = [name for name, _, _ in late + early] + ["small"]
    c_idx = lax.axis_index("c").astype(jnp.int32).reshape(1)
    reduced = _join_halves([_sum_devices(b, c_idx, "grad_sum_" + n)
                            for n, b in zip(names, list(arrived_late) + list(arrived_early))])

    outs = [{}, {}, {}, {}]
    for (name, shape, axis), r in zip(late + early, reduced):
        g2 = r.reshape(_shard_shape(shape, axis))
        res = _adamw(local[name], g2, given["m_" + name][0], given["v_" + name][0], "adamw_" + name)
        for o_, a in zip(outs, (g2, *res)):
            o_[name] = a[None]
    pack = lambda prefix: _pad_rows(_pack_small_local({n: given[prefix + n] for n in WEIGHT_ORDER}), 2 * hrows)
    g_small = reduced[-1].reshape(2 * hrows, LANES)
    res = _adamw(pack(""), g_small, pack("m_"), pack("v_"), "adamw_small")
    for o_, a in zip(outs, (g_small, *res)):
        o_.update(_unpack_small_local(a.reshape(-1)))
    return (loss, grad_x[None], *[o_[n] for o_ in outs for n in WEIGHT_ORDER])
```

```python
import functools

import jax
import jax.numpy as jnp
from jax import lax
from jax.experimental import pallas as pl
from jax.experimental.pallas import tpu as pltpu

F32, BF16 = jnp.float32, jnp.bfloat16
MESH = pl.DeviceIdType.MESH

D_MODEL = 1024
EPS = 1e-6
LRU_W = 512
LRU_BLOCKS = 8
LRU_C = 8.0
CONV_W = 4
HEADS = 8
QK_NOPE, QK_ROPE, QK_HEAD, V_DIM = 64, 32, 96, 64
Q_LORA, KV_LORA = 256, 128
MLA_W = HEADS * V_DIM
ROPE_THETA = 10000.0
IN_COLS = 2 * LRU_W + Q_LORA + KV_LORA + QK_ROPE
OFF_KR = IN_COLS - QK_ROPE
IN_PAD = 1536
MEM_HEADS, MEM_HD = 4, 128
MEM_W = MEM_HEADS * MEM_HD
D_FF = 2816
N_CHIPS = 4
ADAM_LR, ADAM_B1, ADAM_B2, ADAM_EPS, ADAM_WD, ADAM_STEP = 0.001, 0.9, 0.999, 1e-08, 0.01, 10

LANES = 128
SUBLANES = 8
VMEM_LIMIT = 56 * 1024 * 1024
PACK_ROWS = 2048

SHARDED = (
    ("w_in", (D_MODEL, IN_COLS), 1, True),
    ("lru_conv_w", (2, CONV_W, LRU_W), 2, False),
    ("lru_conv_b", (2, LRU_W), 1, False),
    ("lru_b_a", (2, LRU_W), 1, False),
    ("lru_b_i", (2, LRU_W), 1, False),
    ("lru_lambda", (2, LRU_W), 1, False),
    ("w_uq", (Q_LORA, HEADS * QK_HEAD), 1, True),
    ("w_ukv", (KV_LORA, HEADS * (QK_NOPE + V_DIM)), 1, True),
    ("w_out", (2 * LRU_W, D_MODEL), 0, True),
    ("w_mem_q", (D_MODEL, MEM_W), 0, True),
    ("w_mem_kv", (D_MODEL, 2 * MEM_W), 0, True),
    ("w_mem_o", (MEM_W, D_MODEL), 1, True),
    ("w_up", (D_MODEL, 2 * D_FF), 1, True),
    ("ffn_conv_w", (3, 2 * D_FF), 1, False),
    ("w_down", (D_FF, D_MODEL), 0, True),
)
REPLICATED = (
    ("attn_norm", (D_MODEL,)), ("lru_w_a", (2, LRU_BLOCKS, 64, 64)), ("lru_w_i", (2, LRU_BLOCKS, 64, 64)),
    ("q_a_norm", (Q_LORA,)), ("kv_a_norm", (KV_LORA,)), ("mla_q_norm", (QK_HEAD,)), ("mla_k_norm", (QK_HEAD,)),
    ("lru_out_norm", (LRU_W,)), ("mla_out_norm", (MLA_W,)), ("mem_attn_norm", (D_MODEL,)), ("mem_norm", (D_MODEL,)),
    ("mem_q_norm", (MEM_HD,)), ("mem_k_norm", (MEM_HD,)), ("ffn_norm", (D_MODEL,)), ("ffn_conv_b", (2 * D_FF,)),
)
WEIGHT_ORDER = ('attn_norm', 'w_in', 'lru_conv_w', 'lru_conv_b', 'lru_w_a', 'lru_b_a', 'lru_w_i', 'lru_b_i', 'lru_lambda',
                'q_a_norm', 'w_uq', 'kv_a_norm', 'w_ukv', 'mla_q_norm', 'mla_k_norm', 'lru_out_norm', 'mla_out_norm', 'w_out',
                'mem_attn_norm', 'mem_norm', 'w_mem_q', 'w_mem_kv', 'mem_q_norm', 'mem_k_norm', 'w_mem_o', 'ffn_norm', 'w_up',
                'ffn_conv_w', 'ffn_conv_b', 'w_down')


def _numel(shape):
    n = 1
    for s in shape:
        n *= s
    return n


def _cparams(n_axes):
    return pltpu.CompilerParams(dimension_semantics=("arbitrary",) * n_axes, vmem_limit_bytes=VMEM_LIMIT)


def _bdot(a, b):
    return jnp.dot(a.astype(BF16), b.astype(BF16), preferred_element_type=F32)


def _bdot_nt(a, b):
    return lax.dot_general(a.astype(BF16), b.astype(BF16), (((1,), (1,)), ((), ())), preferred_element_type=F32)


def _bdot_tn(a, b):
    return lax.dot_general(a.astype(BF16), b.astype(BF16), (((0,), (0,)), ((), ())), preferred_element_type=F32)


def _rstd(x, n=None):
    n = x.shape[-1] if n is None else n
    return lax.rsqrt(jnp.sum(x * x, axis=-1, keepdims=True) * (1.0 / n) + EPS)


def _norm_bwd(x, rs, g, dy, n=None):
    n = x.shape[-1] if n is None else n
    xhat = x * rs
    dxh = dy * g
    dx = rs * (dxh - xhat * (jnp.sum(dxh * xhat, axis=-1, keepdims=True) * (1.0 / n)))
    return dx, dy * xhat


def _acc_row(ref, r, val):
    ref[r:r + 1, :] += jnp.sum(val, axis=0, keepdims=True)


def _zero_first(i, *refs):
    @pl.when(i == 0)
    def _():
        for r in refs:
            r[...] = jnp.zeros_like(r)


def _shift_down(x, j, halo):
    if j == 0:
        return x
    xs = pltpu.roll(x, j, 0)
    hs = pltpu.roll(halo, j, 0)
    row = lax.broadcasted_iota(jnp.int32, hs.shape, 0)
    top = jnp.where(row < j, hs, xs[:SUBLANES])
    return jnp.concatenate([top, xs[SUBLANES:]], axis=0)


def _shift_up(x, j, halo):
    if j == 0:
        return x
    t = x.shape[0]
    xs = pltpu.roll(x, t - j, 0)
    hs = pltpu.roll(halo, SUBLANES - j, 0)
    row = lax.broadcasted_iota(jnp.int32, hs.shape, 0)
    bot = jnp.where(row >= SUBLANES - j, hs, xs[t - SUBLANES:])
    return jnp.concatenate([xs[:t - SUBLANES], bot], axis=0)


def _shift(x, j, halo, down):
    return _shift_down(x, j, halo) if down else _shift_up(x, j, halo)


def _scan(a, b, h_in, down):
    t, c = a.shape
    g = t // SUBLANES
    a3, b3 = a.reshape(g, SUBLANES, c), b.reshape(g, SUBLANES, c)
    sub = lax.broadcasted_iota(jnp.int32, a3.shape, 1)
    d = 1
    while d < SUBLANES:
        keep = (sub >= d) if down else (sub < SUBLANES - d)
        shift = d if down else SUBLANES - d
        a_s = jnp.where(keep, pltpu.roll(a3, shift, 1), 1.0)
        b_s = jnp.where(keep, pltpu.roll(b3, shift, 1), 0.0)
        b3 = a3 * b_s + b3
        a3 = a3 * a_s
        d *= 2
    hs = [None] * g
    carry = h_in
    for i in (range(g) if down else range(g - 1, -1, -1)):
        hs[i] = a3[i] * carry + b3[i]
        carry = hs[i][SUBLANES - 1:, :] if down else hs[i][:1, :]
    return jnp.concatenate(hs, axis=0)


def _sigmoid(x):
    return 0.5 * jnp.tanh(0.5 * x) + 0.5


LOG2E = 1.4426950408889634
GELU_K = 0.7978845608028654
GELU_C = 0.044715


def _gelu(x):
    return 0.5 * x * (1.0 + jnp.tanh(GELU_K * (x + GELU_C * x * x * x)))


def _gelu_grad(x):
    t = jnp.tanh(GELU_K * (x + GELU_C * x * x * x))
    return 0.5 * (1.0 + t) + 0.5 * x * (1.0 - t * t) * GELU_K * (1.0 + 3.0 * GELU_C * x * x)


def _rope_partner(x):
    lane = lax.broadcasted_iota(jnp.int32, x.shape, 1)
    half = QK_ROPE // 2
    sw = jnp.where(lane < QK_NOPE + half, pltpu.roll(x, LANES - half, 1), pltpu.roll(x, half, 1))
    return jnp.where((lane >= QK_NOPE) & (lane < QK_HEAD), sw, 0.0)


def _rope(x, cos_t, sin_t):
    return x * cos_t + _rope_partner(x) * sin_t


def _rope_t(dy, cos_t, sin_t):
    return dy * cos_t + _rope_partner(dy * sin_t)


def _rowwise(body, name, s, tm, rows=(), halos=(), fulls=(), outs=(), accs=()):
    n = s // tm
    hb = tm // SUBLANES
    last8 = s // SUBLANES - 1
    in_specs, args = [], []
    for a in rows:
        in_specs.append(pl.BlockSpec((tm, a.shape[1]), lambda i: (i, 0)))
        args.append(a)
    for a in halos:
        in_specs.append(pl.BlockSpec((SUBLANES, a.shape[1]), lambda i: (jnp.maximum(i * hb - 1, 0), 0)))
        in_specs.append(pl.BlockSpec((SUBLANES, a.shape[1]), lambda i: (jnp.minimum((i + 1) * hb, last8), 0)))
        args += [a, a]
    for a in fulls:
        in_specs.append(pl.BlockSpec(a.shape, lambda i, nd=a.ndim: (0,) * nd))
        args.append(a)
    out_shape, out_specs = [], []
    for c, dt in outs:
        out_shape.append(jax.ShapeDtypeStruct((s, c), dt))
        out_specs.append(pl.BlockSpec((tm, c), lambda i: (i, 0)))
    for shp, dt in accs:
        out_shape.append(jax.ShapeDtypeStruct(shp, dt))
        out_specs.append(pl.BlockSpec(shp, lambda i, nd=len(shp): (0,) * nd))

    def kern(*refs):
        body(pl.program_id(0), n, *refs)

    return pl.pallas_call(kern, grid=(n,), in_specs=in_specs, out_specs=out_specs, out_shape=out_shape, name=name,
                          compiler_params=_cparams(1))(*args)


def _matmul_tn(a, b, name, col_shards=False, out_dtype=F32):
    t, m = a.shape
    n = b.shape[1]
    bm = m
    for cand in range(LANES, m + 1, LANES):
        if m % cand == 0 and cand * (n // N_CHIPS if col_shards else min(n, 2048)) * 4 <= 6 * 1024 * 1024:
            bm = cand
    bn = n // N_CHIPS if col_shards else (n if n <= 2048 else 1408)
    bt = min(512, t)
    nt = t // bt

    def kern(a_ref, b_ref, o_ref, acc_ref):
        k = pl.program_id(2)

        @pl.when(k == 0)
        def _():
            acc_ref[...] = jnp.zeros_like(acc_ref)
        acc_ref[...] += _bdot_tn(a_ref[...], b_ref[...])

        @pl.when(k == nt - 1)
        def _():
            o_ref[...] = acc_ref[...].astype(out_dtype)

    if col_shards:
        out_spec = pl.BlockSpec((None, bm, bn), lambda i, j, k: (j, i, 0))
        out_shape = jax.ShapeDtypeStruct((N_CHIPS, m, bn), out_dtype)
    else:
        out_spec = pl.BlockSpec((bm, bn), lambda i, j, k: (i, j))
        out_shape = jax.ShapeDtypeStruct((m, n), out_dtype)
    return pl.pallas_call(
        kern, grid=(m // bm, n // bn, nt),
        in_specs=[pl.BlockSpec((bt, bm), lambda i, j, k: (k, i)), pl.BlockSpec((bt, bn), lambda i, j, k: (k, j))],
        out_specs=out_spec, out_shape=out_shape, scratch_shapes=[pltpu.VMEM((bm, bn), F32)], name=name,
        compiler_params=_cparams(3))(a, b)


def _in_proj(x, g, w_in_p, tm):
    def body(i, n, x_ref, g_ref, w_ref, xr, yg, cq, ckv, krp, hb):
        xv = x_ref[...]
        h = (xv * _rstd(xv) * g_ref[...]).astype(BF16)
        hb[...] = h
        p = jnp.dot(h, w_ref[...], preferred_element_type=F32)
        xr[...] = p[:, :LRU_W]
        yg[...] = p[:, LRU_W:2 * LRU_W]
        cq[...] = p[:, 2 * LRU_W:2 * LRU_W + Q_LORA]
        ckv[...] = p[:, 2 * LRU_W + Q_LORA:OFF_KR]
        krp[...] = p[:, OFF_KR:IN_PAD]

    return _rowwise(body, "in_proj", x.shape[0], tm, rows=[x], fulls=[g, w_in_p],
                    outs=[(LRU_W, F32), (LRU_W, F32), (Q_LORA, F32), (KV_LORA, F32), (LANES, F32), (D_MODEL, BF16)])


def _lru_gates(x, halo, cw_ref, pv_ref, wa_ref, wi_ref, rev):
    down = not rev
    xc = pv_ref[0:1, :] + jnp.zeros_like(x)
    for j in range(CONV_W):
        k = j if rev else CONV_W - 1 - j
        xc = xc + cw_ref[k:k + 1, :] * _shift(x, j, halo, down)
    r = _sigmoid(_bdot(xc, wa_ref[...]) + pv_ref[1:2, :])
    ig = _sigmoid(_bdot(xc, wi_ref[...]) + pv_ref[2:3, :])
    lam = pv_ref[3:4, :]
    sp = jnp.maximum(-lam, 0.0) + jnp.log(1.0 + jnp.exp(-jnp.abs(lam)))
    log_a = (-LRU_C) * r * sp
    a = jnp.exp(log_a)
    z = 2.0 * log_a
    series = -(z * (1.0 + z * (0.5 + z * (1.0 / 6.0 + z * (1.0 / 24.0)))))
    om = jnp.where(z > -0.02, series, 1.0 - jnp.exp(z))
    mult = jnp.sqrt(om)
    return xc, r, ig, sp, a, mult


def _lru_scan_fwd(xr, cw, pv, wa, wi, rev, t):
    s = xr.shape[0]
    n = s // t
    hb = t // SUBLANES
    last8 = s // SUBLANES - 1
    down = not rev

    def kern(x_ref, halo_ref, cw_ref, pv_ref, wa_ref, wi_ref, h_ref, carry_ref):
        i = pl.program_id(0)
        _zero_first(i, carry_ref)
        halo = jnp.where(i == 0, 0.0, halo_ref[...])
        xc, r, ig, sp, a, mult = _lru_gates(x_ref[...], halo, cw_ref, pv_ref, wa_ref, wi_ref, rev)
        h_ref[...] = _scan(a, mult * ig * xc, carry_ref[...], down)
        carry_ref[...] = h_ref[pl.ds(t - 1 if down else 0, 1), :]

    if rev:
        blk = lambda i: (n - 1 - i, 0)
        hal = lambda i: (jnp.minimum((n - i) * hb, last8), 0)
    else:
        blk = lambda i: (i, 0)
        hal = lambda i: (jnp.maximum(i * hb - 1, 0), 0)
    full = lambda a: pl.BlockSpec(a.shape, lambda i: (0, 0))
    return pl.pallas_call(
        kern, grid=(n,),
        in_specs=[pl.BlockSpec((t, LRU_W), blk), pl.BlockSpec((SUBLANES, LRU_W), hal), full(cw), full(pv), full(wa), full(wi)],
        out_specs=pl.BlockSpec((t, LRU_W), blk), out_shape=jax.ShapeDtypeStruct((s, LRU_W), F32),
        scratch_shapes=[pltpu.VMEM((1, LRU_W), F32)], name="lru_scan_rev" if rev else "lru_scan_fwd",
        compiler_params=_cparams(1))(xr, xr, cw, pv, wa, wi)


def _lru_scan_bwd(xr, h, dh, cw, pv, wa, wi, rev, t):
    s = xr.shape[0]
    n = s // t
    hb = t // SUBLANES
    last8 = s // SUBLANES - 1
    down = not rev

    def kern(x_ref, xh_ref, h_ref, hh_ref, dh_ref, cw_ref, pv_ref, wa_ref, wi_ref,
             dx_ref, gwa_ref, gwi_ref, gv_ref, p_ref, dxc_halo_ref, tmp_ref):
        i = pl.program_id(0)
        _zero_first(i, gwa_ref, gwi_ref, gv_ref, p_ref, dxc_halo_ref)
        at_start = i == n - 1
        x = x_ref[...]
        xhalo = jnp.where(at_start, 0.0, xh_ref[...])
        hhalo = jnp.where(at_start, 0.0, hh_ref[...])
        xc, r, ig, sp, a, mult = _lru_gates(x, xhalo, cw_ref, pv_ref, wa_ref, wi_ref, rev)
        h_prev = _shift(h_ref[...], 1, hhalo, down)
        row = lax.broadcasted_iota(jnp.int32, x.shape, 0)
        edge = t - 1 if down else 0
        dh_mod = dh_ref[...] + jnp.where(row == edge, p_ref[...], 0.0)
        a_next = _shift(a, 1, jnp.zeros((SUBLANES, LRU_W), F32), not down)
        g = _scan(a_next, dh_mod, jnp.zeros((1, LRU_W), F32), not down)
        tmp_ref[...] = a * g
        p_ref[...] = tmp_ref[pl.ds(0 if down else t - 1, 1), :]
        da = g * h_prev
        d_ig = g * mult * xc
        d_xc = g * mult * ig
        d_om = g * ig * xc * (0.5 / jnp.maximum(mult, 1e-30))
        d_log_a = da * a - 2.0 * d_om * a * a
        d_r = d_log_a * ((-LRU_C) * sp)
        d_sp = jnp.sum(d_log_a * ((-LRU_C) * r), axis=0, keepdims=True)
        lam = pv_ref[3:4, :]
        gv_ref[7:8, :] += d_sp * (-_sigmoid(-lam))
        d_ga = d_r * r * (1.0 - r)
        d_gi = d_ig * ig * (1.0 - ig)
        _acc_row(gv_ref, 5, d_ga)
        _acc_row(gv_ref, 6, d_gi)
        d_xc = d_xc + _bdot_nt(d_ga, wa_ref[...]) + _bdot_nt(d_gi, wi_ref[...])
        gwa_ref[...] += _bdot_tn(xc, d_ga)
        gwi_ref[...] += _bdot_tn(xc, d_gi)
        _acc_row(gv_ref, 4, d_xc)
        dx = jnp.zeros_like(x)
        dxc_halo = dxc_halo_ref[...]
        for j in range(CONV_W):
            k = j if rev else CONV_W - 1 - j
            _acc_row(gv_ref, k, d_xc * _shift(x, j, xhalo, down))
            dx = dx + cw_ref[k:k + 1, :] * _shift(d_xc, j, dxc_halo, not down)
        dx_ref[...] = dx
        dxc_halo_ref[...] = d_xc[:SUBLANES] if down else d_xc[t - SUBLANES:]

    if rev:
        blk = lambda i: (i, 0)
        hal = lambda i: (jnp.minimum((i + 1) * hb, last8), 0)
    else:
        blk = lambda i: (n - 1 - i, 0)
        hal = lambda i: (jnp.maximum((n - 1 - i) * hb - 1, 0), 0)
    full = lambda a: pl.BlockSpec(a.shape, lambda i: (0, 0))
    bs = pl.BlockSpec((t, LRU_W), blk)
    hs = pl.BlockSpec((SUBLANES, LRU_W), hal)
    return pl.pallas_call(
        kern, grid=(n,),
        in_specs=[bs, hs, bs, hs, bs, full(cw), full(pv), full(wa), full(wi)],
        out_specs=[bs, pl.BlockSpec((LRU_W, LRU_W), lambda i: (0, 0)), pl.BlockSpec((LRU_W, LRU_W), lambda i: (0, 0)),
                   pl.BlockSpec((SUBLANES, LRU_W), lambda i: (0, 0))],
        out_shape=[jax.ShapeDtypeStruct((s, LRU_W), F32), jax.ShapeDtypeStruct((LRU_W, LRU_W), F32),
                   jax.ShapeDtypeStruct((LRU_W, LRU_W), F32), jax.ShapeDtypeStruct((SUBLANES, LRU_W), F32)],
        scratch_shapes=[pltpu.VMEM((1, LRU_W), F32), pltpu.VMEM((SUBLANES, LRU_W), F32), pltpu.VMEM((t, LRU_W), F32)],
        name="lru_bwd_rev" if rev else "lru_bwd_fwd", compiler_params=_cparams(1))(xr, xr, h, h, dh, cw, pv, wa, wi)


def _mla_qkv(cq, ckv, krp, cos_t, sin_t, g_qa, g_kva, g_qn, g_kn, w_uq_p, w_uk_p, w_uv, tm):
    scale = QK_HEAD ** -0.5 * LOG2E

    def body(i, n, cq_ref, ckv_ref, kr_ref, c_ref, s_ref, gqa, gkva, gqn, gkn, wq, wk, wv, q_out, k_out, v_out):
        cosv, sinv = c_ref[...], s_ref[...]
        cqv = cq_ref[...]
        qr = _bdot(cqv * _rstd(cqv) * gqa[...], wq[...])
        ckvv = ckv_ref[...]
        c_kv = (ckvv * _rstd(ckvv) * gkva[...]).astype(BF16)
        kn = jnp.dot(c_kv, wk[...], preferred_element_type=F32)
        v_out[...] = jnp.dot(c_kv, wv[...], preferred_element_type=F32).astype(BF16)
        kr = kr_ref[...]
        kr_swapped = _rope_partner(kr * gkn[...]) * sinv
        for h in range(HEADS):
            sl = slice(h * LANES, (h + 1) * LANES)
            qh = qr[:, sl]
            qh = _rope(qh * _rstd(qh, QK_HEAD) * gqn[...], cosv, sinv) * scale
            q_out[:, sl] = qh.astype(BF16)
            kh = kn[:, sl] + kr
            rs = _rstd(kh, QK_HEAD)
            k_out[:, sl] = (kh * rs * gkn[...] * cosv + kr_swapped * rs).astype(BF16)

    return _rowwise(body, "mla_qkv", cq.shape[0], tm, rows=[cq, ckv, krp, cos_t, sin_t],
                    fulls=[g_qa, g_kva, g_qn, g_kn, w_uq_p, w_uk_p, w_uv],
                    outs=[(HEADS * LANES, BF16), (HEADS * LANES, BF16), (MLA_W, BF16)])


NT_DIMS = (((1,), (1,)), ((), ()))
TN_DIMS = (((0,), (0,)), ((), ()))


def _riding_exchange(copies_fn, first, last):
    @pl.when(first)
    def _():
        for cp in copies_fn():
            cp.start()

    def finish():
        @pl.when(last)
        def _():
            for cp in copies_fn():
                cp.wait()
    return finish


def _attn_fwd(q, k, v, tq, tk, shards=()):
    s = q.shape[0]
    nq, nk = s // tq, s // tk
    n = len(shards)

    def kern(*refs):
        q_ref, k_ref, v_ref = refs[:3]
        o_ref, lse_ref = refs[3 + n:5 + n]
        acc_ref = refs[5 + 2 * n]
        p_id, i_id = pl.program_id(0), pl.program_id(1)
        finish = _riding_exchange(lambda: _gather_copies(refs[3:3 + n], refs[5 + n:5 + 2 * n], *refs[6 + 2 * n:]),
                                  (p_id == 0) & (i_id == 0), (p_id == HEADS // 2 - 1) & (i_id == nq - 1)) if n else None
        qs = (q_ref[:, :LANES], q_ref[:, LANES:])
        acc_ref[...] = jnp.zeros_like(acc_ref)

        def step(j, carry):
            off = pl.multiple_of(j * tk, tk)
            vc = v_ref[pl.ds(off, tk), :]
            out = []
            for h in range(2):
                m, l = carry[2 * h:2 * h + 2]
                st = lax.dot_general(k_ref[pl.ds(off, tk), h * LANES:(h + 1) * LANES], qs[h], NT_DIMS,
                                     preferred_element_type=F32)
                mn = jnp.maximum(m, jnp.max(st, axis=0, keepdims=True))
                al = jnp.exp2(m - mn)
                pt = jnp.exp2(st - mn)
                l = al * l + jnp.sum(pt, axis=0, keepdims=True)
                acc_ref[h] = al * acc_ref[h] + lax.dot_general(vc, pt.astype(BF16), TN_DIMS, preferred_element_type=F32)
                out += [mn, l]
            return tuple(out)

        init = (jnp.full((1, tq), -1e30, F32), jnp.zeros((1, tq), F32)) * 2
        m0, l0, m1, l1 = lax.fori_loop(0, nk, step, init)
        row = lax.broadcasted_iota(jnp.int32, (LANES, tq), 0)
        o_ref[...] = jnp.where(row < V_DIM, acc_ref[0] / l0, acc_ref[1] / l1).T
        lse_ref[0, 0:1, :] = m0 + jnp.log2(l0)
        lse_ref[0, 1:2, :] = m1 + jnp.log2(l1)
        if n:
            finish()

    return pl.pallas_call(
        kern, grid=(HEADS // 2, nq),
        in_specs=[pl.BlockSpec((tq, 2 * LANES), lambda p, i: (i, p)), pl.BlockSpec((s, 2 * LANES), lambda p, i: (0, p)),
                  pl.BlockSpec((s, LANES), lambda p, i: (0, p))] + [ANY] * n,
        out_specs=[pl.BlockSpec((tq, LANES), lambda p, i: (i, p)), pl.BlockSpec((1, 2, tq), lambda p, i: (p, 0, i))]
        + [ANY] * n,
        out_shape=[jax.ShapeDtypeStruct((s, MLA_W), F32), jax.ShapeDtypeStruct((HEADS // 2, 2, s), F32)]
        + _gather_shapes(shards),
        scratch_shapes=[pltpu.VMEM((2, LANES, tq), F32)] + (_gather_sems(n) if n else []),
        name="attn_fwd", compiler_params=_cparams(2))(q, k, v, *shards)


def _attn_bwd(q, k, v, do, lse, delta, tq, tk, contributions=()):
    s = q.shape[0]
    nq, nk = s // tq, s // tk
    n = len(contributions)

    def kern(*refs):
        q_ref, do_ref, lse_ref, dl_ref, k_ref, v_ref = refs[:6]
        dq_ref, dk_ref, dv_ref = refs[6 + n:9 + n]
        acc_ref = refs[9 + 2 * n]
        p_id, i_id = pl.program_id(0), pl.program_id(1)
        finish = _riding_exchange(lambda: _to_owner_copies(refs[6:6 + n], refs[9 + n:9 + 2 * n], *refs[10 + 2 * n:]),
                                  (p_id == 0) & (i_id == 0), (p_id == HEADS // 2 - 1) & (i_id == nq - 1)) if n else None
        _zero_first(pl.program_id(1), dk_ref, dv_ref)
        acc_ref[...] = jnp.zeros_like(acc_ref)
        qs = (q_ref[:, :LANES], q_ref[:, LANES:])
        doc = do_ref[...]
        lane_q = lax.broadcasted_iota(jnp.int32, (tq, LANES), 1)
        zq = jnp.zeros_like(doc)
        dos = (jnp.where(lane_q < V_DIM, doc, zq), jnp.where(lane_q >= V_DIM, doc, zq))
        lses = (lse_ref[0, 0:1, :], lse_ref[0, 1:2, :])
        dls = (dl_ref[0, 0:1, :], dl_ref[0, 1:2, :])

        def step(j, carry):
            off = pl.multiple_of(j * tk, tk)
            vp = v_ref[pl.ds(off, tk), :]
            lane_k = lax.broadcasted_iota(jnp.int32, (tk, LANES), 1)
            zero = jnp.zeros_like(vp)
            vs = (jnp.where(lane_k < V_DIM, vp, zero), jnp.where(lane_k >= V_DIM, vp, zero))
            for h in range(2):
                sl = slice(h * LANES, (h + 1) * LANES)
                st = lax.dot_general(k_ref[pl.ds(off, tk), sl], qs[h], NT_DIMS, preferred_element_type=F32)
                pt = jnp.exp2(st - lses[h])
                dpt = lax.dot_general(vs[h], doc, NT_DIMS, preferred_element_type=F32)
                dst = (pt * (dpt - dls[h])).astype(BF16)
                dv_ref[pl.ds(off, tk), :] += jnp.dot(pt.astype(BF16), dos[h], preferred_element_type=F32)
                dk_ref[pl.ds(off, tk), sl] += jnp.dot(dst, qs[h], preferred_element_type=F32)
                acc_ref[h] += lax.dot_general(k_ref[pl.ds(off, tk), sl], dst, TN_DIMS, preferred_element_type=F32)
            return carry

        lax.fori_loop(0, nk, step, 0)
        dq_ref[:, :LANES] = acc_ref[0].T
        dq_ref[:, LANES:] = acc_ref[1].T
        if n:
            finish()

    return pl.pallas_call(
        kern, grid=(HEADS // 2, nq),
        in_specs=[pl.BlockSpec((tq, 2 * LANES), lambda p, i: (i, p)), pl.BlockSpec((tq, LANES), lambda p, i: (i, p)),
                  pl.BlockSpec((1, 2, tq), lambda p, i: (p, 0, i)), pl.BlockSpec((1, 2, tq), lambda p, i: (p, 0, i)),
                  pl.BlockSpec((s, 2 * LANES), lambda p, i: (0, p)), pl.BlockSpec((s, LANES), lambda p, i: (0, p))]
        + [ANY] * n,
        out_specs=[pl.BlockSpec((tq, 2 * LANES), lambda p, i: (i, p)), pl.BlockSpec((s, 2 * LANES), lambda p, i: (0, p)),
                   pl.BlockSpec((s, LANES), lambda p, i: (0, p))] + [ANY] * n,
        out_shape=[jax.ShapeDtypeStruct((s, HEADS * LANES), F32), jax.ShapeDtypeStruct((s, HEADS * LANES), F32),
                   jax.ShapeDtypeStruct((s, MLA_W), F32)] + _to_owner_shapes(contributions),
        scratch_shapes=[pltpu.VMEM((2, LANES, tq), F32)] + (_to_owner_sems(n) if n else []),
        name="attn_bwd", compiler_params=_cparams(2))(q, do, lse, delta, k, v, *contributions)


def _mix_out(hf, hb, yg, o, x, g_lru, g_mla, w_out, tm):
    def body(i, n, hf_ref, hb_ref, yg_ref, o_ref, x_ref, gl, gm, w_ref, x1_ref, mix_ref):
        lo = (hf_ref[...] + hb_ref[...]) * _gelu(yg_ref[...])
        ov = o_ref[...]
        mix_ref[:, :LRU_W] = (lo * _rstd(lo) * gl[...]).astype(BF16)
        mix_ref[:, LRU_W:] = (ov * _rstd(ov) * gm[...]).astype(BF16)
        x1_ref[...] = x_ref[...] + jnp.dot(mix_ref[...], w_ref[...], preferred_element_type=F32)

    return _rowwise(body, "mix_out", x.shape[0], tm, rows=[hf, hb, yg, o, x], fulls=[g_lru, g_mla, w_out],
                    outs=[(D_MODEL, F32), (2 * LRU_W, BF16)])


def _mem_kv(mem, g_mem, w_kv, g_k):
    m = mem.shape[0]

    def body(i, n, mem_ref, g_ref, w_ref, gk_ref, km_ref, vm_ref):
        mv = mem_ref[...]
        kv = _bdot(mv * _rstd(mv) * g_ref[...], w_ref[...])
        vm_ref[...] = kv[:, MEM_W:].astype(BF16)
        for h in range(MEM_HEADS):
            sl = slice(h * MEM_HD, (h + 1) * MEM_HD)
            kh = kv[:, sl]
            km_ref[:, sl] = (kh * _rstd(kh) * gk_ref[...]).astype(BF16)

    return _rowwise(body, "mem_kv", m, m, rows=[mem], fulls=[g_mem, w_kv, g_k], outs=[(MEM_W, BF16), (MEM_W, BF16)])


def _mem_attn_core(x1v, g_ref, wq_ref, gq_ref, km_ref, vm_ref):
    scale = MEM_HD ** -0.5
    hm = (x1v * _rstd(x1v) * g_ref[...]).astype(BF16)
    qr = jnp.dot(hm, wq_ref[...], preferred_element_type=F32)
    heads = []
    for h in range(MEM_HEADS):
        sl = slice(h * MEM_HD, (h + 1) * MEM_HD)
        qh = qr[:, sl]
        rs = _rstd(qh)
        qn = (qh * rs * gq_ref[...]).astype(BF16)
        sc = lax.dot_general(qn, km_ref[:, sl], (((1,), (1,)), ((), ())), preferred_element_type=F32) * scale
        e = jnp.exp(sc - jnp.max(sc, axis=-1, keepdims=True))
        p = e / jnp.sum(e, axis=-1, keepdims=True)
        oh = jnp.dot(p.astype(BF16), vm_ref[:, sl], preferred_element_type=F32)
        heads.append((qh, rs, qn, p, oh))
    return hm, heads


def _mem_attn(x1, g, w_q, g_q, km, vm, w_o, tm):
    cs = D_MODEL // N_CHIPS

    def body(i, n, x1_ref, g_ref, wq_ref, gq_ref, km_ref, vm_ref, wo_ref, x2_ref, ob_ref):
        x1v = x1_ref[...]
        _, heads = _mem_attn_core(x1v, g_ref, wq_ref, gq_ref, km_ref, vm_ref)
        for h in range(MEM_HEADS):
            ob_ref[:, h * MEM_HD:(h + 1) * MEM_HD] = heads[h][4].astype(BF16)
        for k in range(N_CHIPS):
            sl = slice(k * cs, (k + 1) * cs)
            x2_ref[:, sl] = x1v[:, sl] + jnp.dot(ob_ref[...], wo_ref[k], preferred_element_type=F32)

    return _rowwise(body, "mem_attn", x1.shape[0], tm, rows=[x1], fulls=[g, w_q, g_q, km, vm, w_o],
                    outs=[(D_MODEL, F32), (MEM_W, BF16)])


def _ffn_up(x2, g, w_up, tm):
    cs = 2 * D_FF // N_CHIPS

    def body(i, n, x_ref, g_ref, w_ref, gu_ref, hb_ref):
        xv = x_ref[...]
        hb_ref[...] = (xv * _rstd(xv) * g_ref[...]).astype(BF16)
        for k in range(N_CHIPS):
            gu_ref[:, k * cs:(k + 1) * cs] = jnp.dot(hb_ref[...], w_ref[k], preferred_element_type=F32)

    return _rowwise(body, "ffn_up", x2.shape[0], tm, rows=[x2], fulls=[g, w_up], outs=[(2 * D_FF, F32), (D_MODEL, BF16)])


def _ffn_conv(gu, prev, nxt, cw_ref, i, n):
    prev = jnp.where(i == 0, 0.0, prev)
    nxt = jnp.where(i == n - 1, 0.0, nxt)
    return (cw_ref[3:4, :] + cw_ref[0:1, :] * _shift_down(gu, 1, prev) + cw_ref[1:2, :] * gu
            + cw_ref[2:3, :] * _shift_up(gu, 1, nxt))


def _ffn_down_loss(gu_pre, x2, target, cw, w_down, tm):
    def body(i, n, gu_ref, x_ref, t_ref, pv_ref, nx_ref, cw_ref, w_ref, dy_ref, dyb_ref, act_ref, guc_ref, loss_ref):
        _zero_first(i, loss_ref)
        gu = _ffn_conv(gu_ref[...], pv_ref[...], nx_ref[...], cw_ref, i, n)
        guc_ref[...] = gu
        g, u = gu[:, :D_FF], gu[:, D_FF:]
        act_ref[...] = (g * _sigmoid(g) * u).astype(BF16)
        y = x_ref[...] + jnp.dot(act_ref[...], w_ref[...], preferred_element_type=F32)
        e = y - t_ref[...]
        loss_ref[...] += jnp.sum(e * e)
        dy = e * (1.0 / D_MODEL)
        dy_ref[...] = dy
        dyb_ref[...] = dy.astype(BF16)

    return _rowwise(body, "ffn_down_loss", x2.shape[0], tm, rows=[gu_pre, x2, target], halos=[gu_pre], fulls=[cw, w_down],
                    outs=[(D_MODEL, F32), (D_MODEL, BF16), (D_FF, BF16), (2 * D_FF, F32)], accs=[((SUBLANES, LANES), F32)])


def _ffn_bwd_act(dyb, gu, w_down, tm):
    def body(i, n, dy_ref, gu_ref, w_ref, dgu_ref):
        d_act = lax.dot_general(dy_ref[...], w_ref[...], (((1,), (1,)), ((), ())), preferred_element_type=F32)
        g, u = gu_ref[:, :D_FF], gu_ref[:, D_FF:]
        sg = _sigmoid(g)
        a = g * sg
        dgu_ref[:, :D_FF] = (d_act * u) * (sg + a - a * sg)
        dgu_ref[:, D_FF:] = d_act * a

    return _rowwise(body, "ffn_bwd_act", dyb.shape[0], tm, rows=[dyb, gu], fulls=[w_down], outs=[(2 * D_FF, F32)])


def _ffn_bwd_conv(dgu, gu_pre, cw, tm):
    def body(i, n, d_ref, g_ref, dp_ref, dn_ref, cw_ref, dpre_ref, gc_ref):
        _zero_first(i, gc_ref)
        d = d_ref[...]
        g = g_ref[...]
        d_next = _shift_up(d, 1, jnp.where(i == n - 1, 0.0, dn_ref[...]))
        d_prev = _shift_down(d, 1, jnp.where(i == 0, 0.0, dp_ref[...]))
        dpre_ref[...] = (cw_ref[0:1, :] * d_next + cw_ref[1:2, :] * d + cw_ref[2:3, :] * d_prev).astype(BF16)
        _acc_row(gc_ref, 0, d_next * g)
        _acc_row(gc_ref, 1, d * g)
        _acc_row(gc_ref, 2, d_prev * g)
        _acc_row(gc_ref, 3, d)

    return _rowwise(body, "ffn_bwd_conv", dgu.shape[0], tm, rows=[dgu, gu_pre], halos=[dgu], fulls=[cw],
                    outs=[(2 * D_FF, BF16)], accs=[((SUBLANES, 2 * D_FF), F32)])


def _ffn_bwd_in(dpre, x2, dy, g, w_up, tm):
    cs = 2 * D_FF // N_CHIPS

    def body(i, n, dp_ref, x_ref, dy_ref, g_ref, w_ref, dx_ref, dxb_ref, gg_ref):
        _zero_first(i, gg_ref)
        d_h = jnp.zeros(x_ref.shape, F32)
        for k in range(N_CHIPS):
            d_h = d_h + lax.dot_general(dp_ref[:, k * cs:(k + 1) * cs], w_ref[k], (((1,), (1,)), ((), ())),
                                        preferred_element_type=F32)
        xv = x_ref[...]
        dx, dg = _norm_bwd(xv, _rstd(xv), g_ref[...], d_h)
        _acc_row(gg_ref, 0, dg)
        dx = dx + dy_ref[...]
        dx_ref[...] = dx
        dxb_ref[...] = dx.astype(BF16)

    return _rowwise(body, "ffn_bwd_in", x2.shape[0], tm, rows=[dpre, x2, dy], fulls=[g, w_up],
                    outs=[(D_MODEL, F32), (D_MODEL, BF16)], accs=[((SUBLANES, D_MODEL), F32)])


def _mem_attn_bwd(x1, dx2, dx2b, g, w_q, g_q, km, vm, w_o, tm):
    scale = MEM_HD ** -0.5
    m = km.shape[0]

    def body(i, n, x1_ref, dx2_ref, dx2b_ref, g_ref, wq_ref, gq_ref, km_ref, vm_ref, wo_ref,
             dx1_ref, dx1b_ref, hm_ref, dqr_ref, dkm_ref, dvm_ref, gg_ref, ggq_ref):
        _zero_first(i, dkm_ref, dvm_ref, gg_ref, ggq_ref)
        x1v = x1_ref[...]
        hm, heads = _mem_attn_core(x1v, g_ref, wq_ref, gq_ref, km_ref, vm_ref)
        hm_ref[...] = hm
        cs = D_MODEL // N_CHIPS
        d_o = jnp.zeros((x1v.shape[0], MEM_W), F32)
        for k in range(N_CHIPS):
            d_o = d_o + lax.dot_general(dx2b_ref[:, k * cs:(k + 1) * cs], wo_ref[k], (((1,), (1,)), ((), ())),
                                        preferred_element_type=F32)
        for h in range(MEM_HEADS):
            sl = slice(h * MEM_HD, (h + 1) * MEM_HD)
            qh, rs, qn, p, _ = heads[h]
            d_oh = d_o[:, sl].astype(BF16)
            dp = lax.dot_general(d_oh, vm_ref[:, sl], (((1,), (1,)), ((), ())), preferred_element_type=F32)
            ds = (p * (dp - jnp.sum(dp * p, axis=-1, keepdims=True)) * scale).astype(BF16)
            dqn = jnp.dot(ds, km_ref[:, sl], preferred_element_type=F32)
            dkm_ref[:, sl] += lax.dot_general(ds, qn, (((0,), (0,)), ((), ())), preferred_element_type=F32)
            dvm_ref[:, sl] += lax.dot_general(p.astype(BF16), d_oh, (((0,), (0,)), ((), ())), preferred_element_type=F32)
            dqh, dgq = _norm_bwd(qh, rs, gq_ref[...], dqn)
            _acc_row(ggq_ref, 0, dgq)
            dqr_ref[:, sl] = dqh.astype(BF16)
        d_hm = lax.dot_general(dqr_ref[...], wq_ref[...], (((1,), (1,)), ((), ())), preferred_element_type=F32)
        dx, dg = _norm_bwd(x1v, _rstd(x1v), g_ref[...], d_hm)
        _acc_row(gg_ref, 0, dg)
        dx = dx + dx2_ref[...]
        dx1_ref[...] = dx
        dx1b_ref[...] = dx.astype(BF16)

    return _rowwise(body, "mem_attn_bwd", x1.shape[0], tm, rows=[x1, dx2, dx2b], fulls=[g, w_q, g_q, km, vm, w_o],
                    outs=[(D_MODEL, F32), (D_MODEL, BF16), (D_MODEL, BF16), (MEM_W, BF16)],
                    accs=[((m, MEM_W), F32), ((m, MEM_W), F32), ((SUBLANES, D_MODEL), F32), ((SUBLANES, MEM_HD), F32)])


def _mem_kv_bwd(mem, g_mem, w_kv, g_k, dkm, dvm):
    m = mem.shape[0]

    def body(i, n, mem_ref, dkm_ref, dvm_ref, g_ref, w_ref, gk_ref, gw_ref, gg_ref, ggk_ref, dkv_ref):
        gg_ref[...] = jnp.zeros_like(gg_ref)
        ggk_ref[...] = jnp.zeros_like(ggk_ref)
        mv = mem_ref[...]
        rs_m = _rstd(mv)
        mem_n = (mv * rs_m * g_ref[...]).astype(BF16)
        kv = jnp.dot(mem_n, w_ref[...], preferred_element_type=F32)
        for h in range(MEM_HEADS):
            sl = slice(h * MEM_HD, (h + 1) * MEM_HD)
            kh = kv[:, sl]
            dkh, dgk = _norm_bwd(kh, _rstd(kh), gk_ref[...], dkm_ref[:, sl])
            _acc_row(ggk_ref, 0, dgk)
            dkv_ref[:, sl] = dkh.astype(BF16)
        dkv_ref[:, MEM_W:] = dvm_ref[...].astype(BF16)
        gw_ref[...] = lax.dot_general(mem_n, dkv_ref[...], (((0,), (0,)), ((), ())), preferred_element_type=F32)
        d_mn = lax.dot_general(dkv_ref[...], w_ref[...], (((1,), (1,)), ((), ())), preferred_element_type=F32)
        _acc_row(gg_ref, 0, d_mn * (mv * rs_m))

    return _rowwise(body, "mem_kv_bwd", m, m, rows=[mem, dkm, dvm], fulls=[g_mem, w_kv, g_k],
                    accs=[((D_MODEL, 2 * MEM_W), F32), ((SUBLANES, D_MODEL), F32), ((SUBLANES, MEM_HD), F32),
                          ((m, 2 * MEM_W), BF16)])


def _mix_out_bwd(dx1b, hf, hb, yg, o, g_lru, g_mla, w_out, tm):
    def body(i, n, dx_ref, hf_ref, hb_ref, yg_ref, o_ref, gl, gm, w_ref, dh_ref, dyg_ref, dob_ref, dl_ref, ggl_ref, ggm_ref):
        _zero_first(i, ggl_ref, ggm_ref)
        dmix = lax.dot_general(dx_ref[...], w_ref[...], (((1,), (1,)), ((), ())), preferred_element_type=F32)
        hs = hf_ref[...] + hb_ref[...]
        ygv = yg_ref[...]
        ge = _gelu(ygv)
        lo = hs * ge
        d_lo, dgl = _norm_bwd(lo, _rstd(lo), gl[...], dmix[:, :LRU_W])
        _acc_row(ggl_ref, 0, dgl)
        dh_ref[...] = d_lo * ge
        dyg_ref[...] = d_lo * hs * _gelu_grad(ygv)
        ov = o_ref[...]
        d_o, dgm = _norm_bwd(ov, _rstd(ov), gm[...], dmix[:, LRU_W:])
        _acc_row(ggm_ref, 0, dgm)
        dob_ref[...] = d_o.astype(BF16)
        prod = d_o * ov
        lane_w = lax.broadcasted_iota(jnp.int32, prod.shape, 1)
        lane = lax.broadcasted_iota(jnp.int32, (prod.shape[0], LANES), 1)
        dl = jnp.zeros((prod.shape[0], LANES), F32)
        for h in range(HEADS):
            in_head = (lane_w >= h * V_DIM) & (lane_w < (h + 1) * V_DIM)
            dl = dl + jnp.where(lane == h, jnp.sum(jnp.where(in_head, prod, 0.0), axis=-1, keepdims=True), 0.0)
        dl_ref[...] = dl

    return _rowwise(body, "mix_out_bwd", dx1b.shape[0], tm, rows=[dx1b, hf, hb, yg, o], fulls=[g_lru, g_mla, w_out],
                    outs=[(LRU_W, F32), (LRU_W, F32), (MLA_W, BF16), (LANES, F32)],
                    accs=[((SUBLANES, LRU_W), F32), ((SUBLANES, MLA_W), F32)])


def _mla_qkv_bwd(cq, ckv, krp, cos_t, sin_t, dq, dk, dv, g_qa, g_kva, g_qn, g_kn, w_uq_p, w_uk_p, w_uv, tm):
    scale = QK_HEAD ** -0.5

    def body(i, n, cq_ref, ckv_ref, kr_ref, c_ref, s_ref, dq_ref, dk_ref, dv_ref, gqa, gkva, gqn, gkn, wq, wk, wv,
             dcq_ref, dckv_ref, dkr_ref, cqb_ref, dqr_ref, ckvb_ref, dkn_ref, dvb_ref, ggqa, ggkva, ggqn, ggkn):
        _zero_first(i, ggqa, ggkva, ggqn, ggkn)
        cosv, sinv = c_ref[...], s_ref[...]
        cqv = cq_ref[...]
        rs_q = _rstd(cqv)
        cqb_ref[...] = (cqv * rs_q * gqa[...]).astype(BF16)
        qr = jnp.dot(cqb_ref[...], wq[...], preferred_element_type=F32)
        ckvv = ckv_ref[...]
        rs_kv = _rstd(ckvv)
        ckvb_ref[...] = (ckvv * rs_kv * gkva[...]).astype(BF16)
        kn = jnp.dot(ckvb_ref[...], wk[...], preferred_element_type=F32)
        kr = kr_ref[...]
        dkr = jnp.zeros_like(kr)
        for h in range(HEADS):
            sl = slice(h * LANES, (h + 1) * LANES)
            qh = qr[:, sl]
            d_qn = _rope_t(dq_ref[:, sl] * scale, cosv, sinv)
            dqh, dgq = _norm_bwd(qh, _rstd(qh, QK_HEAD), gqn[...], d_qn, QK_HEAD)
            _acc_row(ggqn, 0, dgq)
            dqr_ref[:, sl] = dqh.astype(BF16)
            kh = kn[:, sl] + kr
            d_kn = _rope_t(dk_ref[:, sl] * (1.0 / LOG2E), cosv, sinv)
            dkh, dgk = _norm_bwd(kh, _rstd(kh, QK_HEAD), gkn[...], d_kn, QK_HEAD)
            _acc_row(ggkn, 0, dgk)
            dkn_ref[:, sl] = dkh.astype(BF16)
            dkr = dkr + dkh
        dkr_ref[...] = dkr
        dvb_ref[...] = dv_ref[...].astype(BF16)
        d_cq = lax.dot_general(dqr_ref[...], wq[...], (((1,), (1,)), ((), ())), preferred_element_type=F32)
        dcq, dg = _norm_bwd(cqv, rs_q, gqa[...], d_cq)
        _acc_row(ggqa, 0, dg)
        dcq_ref[...] = dcq
        d_ckv = (lax.dot_general(dkn_ref[...], wk[...], (((1,), (1,)), ((), ())), preferred_element_type=F32)
                 + lax.dot_general(dvb_ref[...], wv[...], (((1,), (1,)), ((), ())), preferred_element_type=F32))
        dckv, dg = _norm_bwd(ckvv, rs_kv, gkva[...], d_ckv)
        _acc_row(ggkva, 0, dg)
        dckv_ref[...] = dckv

    return _rowwise(body, "mla_qkv_bwd", cq.shape[0], tm, rows=[cq, ckv, krp, cos_t, sin_t, dq, dk, dv],
                    fulls=[g_qa, g_kva, g_qn, g_kn, w_uq_p, w_uk_p, w_uv],
                    outs=[(Q_LORA, F32), (KV_LORA, F32), (LANES, F32), (Q_LORA, BF16), (HEADS * LANES, BF16),
                          (KV_LORA, BF16), (HEADS * LANES, BF16), (MLA_W, BF16)],
                    accs=[((SUBLANES, Q_LORA), F32), ((SUBLANES, KV_LORA), F32), ((SUBLANES, LANES), F32),
                          ((SUBLANES, LANES), F32)])


def _in_proj_bwd(x, dx1, dxr_f, dxr_b, dyg, dcq, dckv, dkrp, g, w_in_p, tm):
    def body(i, n, x_ref, dx1_ref, df_ref, db_ref, dyg_ref, dcq_ref, dckv_ref, dkr_ref, g_ref, w_ref, gx_ref, dp_ref, gg_ref):
        _zero_first(i, gg_ref)
        dp_ref[:, :LRU_W] = (df_ref[...] + db_ref[...]).astype(BF16)
        dp_ref[:, LRU_W:2 * LRU_W] = dyg_ref[...].astype(BF16)
        dp_ref[:, 2 * LRU_W:2 * LRU_W + Q_LORA] = dcq_ref[...].astype(BF16)
        dp_ref[:, 2 * LRU_W + Q_LORA:OFF_KR] = dckv_ref[...].astype(BF16)
        dp_ref[:, OFF_KR:] = dkr_ref[...].astype(BF16)
        d_h = lax.dot_general(dp_ref[...], w_ref[...], (((1,), (1,)), ((), ())), preferred_element_type=F32)
        xv = x_ref[...]
        dx, dg = _norm_bwd(xv, _rstd(xv), g_ref[...], d_h)
        _acc_row(gg_ref, 0, dg)
        gx_ref[...] = dx + dx1_ref[...]

    return _rowwise(body, "in_proj_bwd", x.shape[0], tm, rows=[x, dx1, dxr_f, dxr_b, dyg, dcq, dckv, dkrp],
                    fulls=[g, w_in_p], outs=[(D_MODEL, F32), (IN_PAD, BF16)], accs=[((SUBLANES, D_MODEL), F32)])


ANY = pl.BlockSpec(memory_space=pl.ANY)


def _chip_peers(x, y):
    return ((1 - x, y), (x, 1 - y), (1 - x, 1 - y))


def _exchange_call(kern, name, ins, out_shapes, n_sems, aliases=None):
    return pl.pallas_call(
        kern, in_specs=[ANY] * len(ins), out_specs=[ANY] * len(out_shapes), out_shape=out_shapes,
        scratch_shapes=[pltpu.SemaphoreType.DMA((n,)) for n in n_sems], input_output_aliases=aliases or {},
        name=name)(*ins)


def _start_then_wait(copies):
    for cp in copies:
        cp.start()
    for cp in copies:
        cp.wait()


N_DEV = 8
RELATIONS = tuple((dx, dy, dc) for dx in (0, 1) for dy in (0, 1) for dc in (0, 1))[1:]


def _flip(v, d):
    return 1 - v if d else v


def _gather_copies(ins, outs, ssem, rsem, lsem):
    x, y, c = lax.axis_index("x"), lax.axis_index("y"), lax.axis_index("c")
    me = 2 * x + y
    cps = []
    for i, (a, o) in enumerate(zip(ins, outs)):
        cps.append(pltpu.make_async_copy(a, o.at[me], lsem.at[i]))
        for j, (px, py) in enumerate(_chip_peers(x, y)):
            cps.append(pltpu.make_async_remote_copy(a, o.at[me], ssem.at[3 * i + j], rsem.at[3 * i + j],
                                                    device_id=(px, py, c), device_id_type=MESH))
    return cps


def _gather_shapes(arrs):
    return [jax.ShapeDtypeStruct((N_CHIPS,) + a.shape, a.dtype) for a in arrs]


def _gather_sems(n):
    return [pltpu.SemaphoreType.DMA((3 * n,)), pltpu.SemaphoreType.DMA((3 * n,)), pltpu.SemaphoreType.DMA((n,))]


def _gather_chips(arrs):
    n = len(arrs)

    def kern(*refs):
        _start_then_wait(_gather_copies(refs[:n], refs[n:2 * n], *refs[2 * n:]))

    return _exchange_call(kern, "gather_weights", arrs, _gather_shapes(arrs), (3 * n, 3 * n, n))


def _to_owner_copies(ins, outs, ssem, rsem, lsem):
    x, y, c = lax.axis_index("x"), lax.axis_index("y"), lax.axis_index("c")
    me = 4 * x + 2 * y + c
    cps = []
    for i, (a, o) in enumerate(zip(ins, outs)):
        cps.append(pltpu.make_async_copy(a.at[2 * x + y, c], o.at[me], lsem.at[i]))
        for r, (dx, dy, dc) in enumerate(RELATIONS):
            tx, ty, tc = _flip(x, dx), _flip(y, dy), _flip(c, dc)
            cps.append(pltpu.make_async_remote_copy(a.at[2 * tx + ty, tc], o.at[me], ssem.at[7 * i + r], rsem.at[7 * i + r],
                                                    device_id=(tx, ty, tc), device_id_type=MESH))
    return cps


def _to_owner_shapes(arrs):
    return [jax.ShapeDtypeStruct((N_DEV,) + a.shape[2:], a.dtype) for a in arrs]


def _to_owner_sems(n):
    return [pltpu.SemaphoreType.DMA((7 * n,)), pltpu.SemaphoreType.DMA((7 * n,)), pltpu.SemaphoreType.DMA((n,))]


def _to_owner(arrs, name):
    n = len(arrs)

    def kern(*refs):
        _start_then_wait(_to_owner_copies(refs[:n], refs[n:2 * n], *refs[2 * n:]))

    return _exchange_call(kern, name, arrs, _to_owner_shapes(arrs), (7 * n, 7 * n, n))


def _join_halves(arrs):
    n = len(arrs)

    def kern(*refs):
        outs, (ssem, rsem) = refs[n:2 * n], refs[2 * n:]
        x, y, c = lax.axis_index("x"), lax.axis_index("y"), lax.axis_index("c")
        _start_then_wait([
            pltpu.make_async_remote_copy(outs[i].at[c], outs[i].at[c], ssem.at[i], rsem.at[i],
                                         device_id=(x, y, 1 - c), device_id_type=MESH) for i in range(n)])

    outs = [jax.ShapeDtypeStruct(a.shape, a.dtype) for a in arrs]
    return _exchange_call(kern, "grad_join_halves", arrs, outs, (n, n), aliases={i: i for i in range(n)})


def _row_block(rows, row_bytes, limit=1 << 20):
    best = None
    for d in range(16, rows + 1, 16):
        if rows % d == 0 and d * row_bytes <= limit:
            best = d
    return best if best is not None else rows


def _sum_devices(b, c, name):
    _, h, cols = b.shape
    hb = _row_block(h, cols * 4)

    def kern(c_ref, b_ref, o_ref):
        acc = b_ref[0].astype(F32)
        for j in range(1, N_DEV):
            acc = acc + b_ref[j].astype(F32)
        o_ref[...] = acc

    return pl.pallas_call(
        kern,
        grid_spec=pltpu.PrefetchScalarGridSpec(
            num_scalar_prefetch=1, grid=(h // hb,),
            in_specs=[pl.BlockSpec((N_DEV, hb, cols), lambda i, c_ref: (0, i, 0))],
            out_specs=pl.BlockSpec((None, hb, cols), lambda i, c_ref: (c_ref[0], i, 0))),
        out_shape=jax.ShapeDtypeStruct((2, h, cols), F32), name=name, compiler_params=_cparams(1))(c, b)


def _adamw(w, g, m, v, name):
    rows, cols = w.shape
    rb = _row_block(rows, cols * 4)
    c1 = 1.0 - ADAM_B1 ** ADAM_STEP
    c2 = 1.0 - ADAM_B2 ** ADAM_STEP

    def kern(w_ref, g_ref, m_ref, v_ref, d_ref, mo_ref, vo_ref):
        gv = g_ref[...]
        mn = ADAM_B1 * m_ref[...] + (1.0 - ADAM_B1) * gv
        vn = ADAM_B2 * v_ref[...] + (1.0 - ADAM_B2) * (gv * gv)
        mo_ref[...] = mn
        vo_ref[...] = vn
        d_ref[...] = (-ADAM_LR) * ((mn / c1) / (jnp.sqrt(vn / c2) + ADAM_EPS) + ADAM_WD * w_ref[...])

    spec = pl.BlockSpec((rb, cols), lambda i: (i, 0))
    return pl.pallas_call(
        kern, grid=(rows // rb,), in_specs=[spec] * 4, out_specs=[spec] * 3,
        out_shape=[jax.ShapeDtypeStruct(w.shape, F32)] * 3, name=name, compiler_params=_cparams(1))(w, g, m, v)


def _pad_rows(flat, rows):
    return jnp.pad(flat, (0, rows * LANES - flat.shape[0])).reshape(rows, LANES)


def _round_up(n, m):
    return (n + m - 1) // m * m


def _shard_shape(shape, axis):
    return tuple(s // N_CHIPS if a == axis else s for a, s in enumerate(shape))


def _to_shards(full, axis):
    shape = full.shape
    t = full.reshape(shape[:axis] + (N_CHIPS, shape[axis] // N_CHIPS) + shape[axis + 1:])
    return jnp.moveaxis(t, axis, 0).reshape(N_CHIPS, -1)


def _from_shards(sh, shape, axis):
    t = sh.reshape((N_CHIPS,) + _shard_shape(shape, axis))
    t = jnp.moveaxis(t, 0, axis)
    return t.reshape(shape)


BIG = tuple((name, shape, axis) for name, shape, axis, big in SHARDED if big)
EARLY_WEIGHTS = ("w_in", "w_uq", "w_ukv")
SMALL_SHARDED = tuple((name, shape, axis) for name, shape, axis, big in SHARDED if not big)


def _pack_small_weights(p):
    flat = jnp.concatenate([p[name].reshape(-1) for name, _, _ in SMALL_SHARDED])
    return _pad_rows(flat, _round_up(-(-flat.shape[0] // LANES), SUBLANES))


def _unpack_small_weights(gathered):
    flat = gathered.reshape(N_CHIPS, -1)
    out, off = {}, 0
    for name, shape, axis in SMALL_SHARDED:
        n = _numel(shape) // N_CHIPS
        out[name] = _from_shards(flat[:, off:off + n], shape, axis)
        off += n
    return out


def _pack_small_local(p, prefix=""):
    parts = [p[prefix + name].reshape(-1) for name, _, _ in SMALL_SHARDED]
    parts += [p[prefix + name].reshape(-1) for name, _ in REPLICATED]
    return jnp.concatenate(parts)


def _pack_small_grads(g):
    parts = [_to_shards(g[name], axis) for name, _, axis in SMALL_SHARDED]
    rep = jnp.concatenate([g[name].reshape(-1) for name, _ in REPLICATED])
    parts.append(jnp.broadcast_to(rep[None], (N_CHIPS, rep.shape[0])))
    return jnp.concatenate(parts, axis=1)


def _unpack_small_local(flat):
    out, off = {}, 0
    for name, shape, axis in SMALL_SHARDED:
        n = _numel(shape) // N_CHIPS
        out[name] = flat[off:off + n].reshape((1,) + _shard_shape(shape, axis))
        off += n
    for name, shape in REPLICATED:
        n = _numel(shape)
        out[name] = flat[off:off + n].reshape((1,) + shape)
        off += n
    return out


def _grad_shards(g, shape, axis):
    if axis == 0:
        return g.reshape((N_CHIPS,) + _shard_shape(shape, axis))
    return jnp.transpose(g.reshape(shape[0], N_CHIPS, shape[1] // N_CHIPS), (1, 0, 2))


def _cols_from_shards(w4):
    return jnp.transpose(w4, (1, 0, 2)).reshape(w4.shape[1], -1)


def _block_diag(w):
    eye = jnp.eye(LRU_BLOCKS, dtype=w.dtype)
    return jnp.einsum("ncd,nm->ncmd", w, eye).reshape(LRU_W, LRU_W)


def _block_diag_t(g):
    g4 = g.reshape(LRU_BLOCKS, 64, LRU_BLOCKS, 64)
    return jnp.stack([g4[n, :, n, :] for n in range(LRU_BLOCKS)])


def _pad8(a):
    return jnp.pad(a, ((0, SUBLANES - a.shape[0]), (0, 0)))


def kernel(x, mem, positions, attn_norm, w_in, lru_conv_w, lru_conv_b, lru_w_a, lru_b_a, lru_w_i, lru_b_i, lru_lambda, q_a_norm, w_uq, kv_a_norm, w_ukv, mla_q_norm, mla_k_norm, lru_out_norm, mla_out_norm, w_out, mem_attn_norm, mem_norm, w_mem_q, w_mem_kv, mem_q_norm, mem_k_norm, w_mem_o, ffn_norm, w_up, ffn_conv_w, ffn_conv_b, w_down, loss_target, m_attn_norm, m_w_in, m_lru_conv_w, m_lru_conv_b, m_lru_w_a, m_lru_b_a, m_lru_w_i, m_lru_b_i, m_lru_lambda, m_q_a_norm, m_w_uq, m_kv_a_norm, m_w_ukv, m_mla_q_norm, m_mla_k_norm, m_lru_out_norm, m_mla_out_norm, m_w_out, m_mem_attn_norm, m_mem_norm, m_w_mem_q, m_w_mem_kv, m_mem_q_norm, m_mem_k_norm, m_w_mem_o, m_ffn_norm, m_w_up, m_ffn_conv_w, m_ffn_conv_b, m_w_down, v_attn_norm, v_w_in, v_lru_conv_w, v_lru_conv_b, v_lru_w_a, v_lru_b_a, v_lru_w_i, v_lru_b_i, v_lru_lambda, v_q_a_norm, v_w_uq, v_kv_a_norm, v_w_ukv, v_mla_q_norm, v_mla_k_norm, v_lru_out_norm, v_mla_out_norm, v_w_out, v_mem_attn_norm, v_mem_norm, v_w_mem_q, v_w_mem_kv, v_mem_q_norm, v_mem_k_norm, v_w_mem_o, v_ffn_norm, v_w_up, v_ffn_conv_w, v_ffn_conv_b, v_w_down):
    given = dict(locals())
    local = {name: given[name][0] for name in WEIGHT_ORDER}
    s = x.shape[1]
    x2d, mem2d, tgt = x[0], mem[0], loss_target[0]
    tm = min(512, s)
    tm_wide = min(1024, s)
    tm_ffn = min(256, s)
    t_scan = min(512, s)
    tq_f, tq_b, tk = min(4096, s), min(1024, s), min(512, s)

    early = [b for b in BIG if b[0] in EARLY_WEIGHTS]
    late = [b for b in BIG if b[0] not in EARLY_WEIGHTS]
    got = _gather_chips([local[name].astype(BF16) for name, _, _ in early] + [_pack_small_weights(local)])
    full = _unpack_small_weights(got[-1])

    def take_gathered(entries, arrays):
        for (name, shape, axis), w4 in zip(entries, arrays):
            if axis == 0:
                full[name] = w4.reshape(shape)
            elif name in ("w_up", "w_mem_o"):
                full[name] = w4
            else:
                full[name] = _cols_from_shards(w4)

    take_gathered(early, got)
    row = lambda a: a.reshape(1, -1)
    b16 = lambda a: a.astype(BF16)
    zeros = lambda r, c: jnp.zeros((r, c), BF16)
    w_in_f = full["w_in"]
    w_in_p = jnp.concatenate([w_in_f[:, :OFF_KR], zeros(D_MODEL, QK_NOPE), w_in_f[:, OFF_KR:],
                              zeros(D_MODEL, LANES - QK_HEAD)], axis=1)
    w_uq_p = jnp.pad(full["w_uq"].reshape(Q_LORA, HEADS, QK_HEAD), ((0, 0), (0, 0), (0, LANES - QK_HEAD))).reshape(Q_LORA, -1)
    ukv = full["w_ukv"].reshape(KV_LORA, HEADS, QK_NOPE + V_DIM)
    w_uk_p = jnp.pad(ukv[:, :, :QK_NOPE], ((0, 0), (0, 0), (0, LANES - QK_NOPE))).reshape(KV_LORA, -1)
    w_uv = ukv[:, :, QK_NOPE:].reshape(KV_LORA, MLA_W)
    wa = [b16(_block_diag(local["lru_w_a"][d])) for d in range(2)]
    wi = [b16(_block_diag(local["lru_w_i"][d])) for d in range(2)]
    cw = [_pad8(full["lru_conv_w"][d]) for d in range(2)]
    pv = [_pad8(jnp.stack([full["lru_conv_b"][d], full["lru_b_a"][d], full["lru_b_i"][d], full["lru_lambda"][d]]))
          for d in range(2)]
    ffn_cw = _pad8(jnp.concatenate([full["ffn_conv_w"], row(local["ffn_conv_b"])], axis=0))
    g_attn, g_qa, g_kva = row(local["attn_norm"]), row(local["q_a_norm"]), row(local["kv_a_norm"])
    g_qn = jnp.pad(row(local["mla_q_norm"]), ((0, 0), (0, LANES - QK_HEAD)))
    g_kn = jnp.pad(row(local["mla_k_norm"]), ((0, 0), (0, LANES - QK_HEAD)))
    g_lru, g_mla = row(local["lru_out_norm"]), row(local["mla_out_norm"])
    g_memattn, g_mem = row(local["mem_attn_norm"]), row(local["mem_norm"])
    g_mq, g_mk, g_ffn = row(local["mem_q_norm"]), row(local["mem_k_norm"]), row(local["ffn_norm"])

    inv = ROPE_THETA ** (-jnp.arange(0, QK_ROPE, 2, dtype=F32) / QK_ROPE)
    ang = positions[0].astype(F32)[:, None] * inv
    cosv, sinv = jnp.cos(ang), jnp.sin(ang)
    ones, zer = jnp.ones((s, QK_NOPE), F32), jnp.zeros((s, LANES - QK_HEAD), F32)
    cos_t = jnp.concatenate([ones, cosv, cosv, zer + 1.0], axis=1)
    sin_t = jnp.concatenate([ones * 0.0, -sinv, sinv, zer], axis=1)

    xr, yg, cq, ckv, krp, hb_in = _in_proj(x2d, g_attn, w_in_p, tm)
    h_f = _lru_scan_fwd(xr, cw[0], pv[0], wa[0], wi[0], False, t_scan)
    h_b = _lru_scan_fwd(xr, cw[1], pv[1], wa[1], wi[1], True, t_scan)
    q, k, v = _mla_qkv(cq, ckv, krp, cos_t, sin_t, g_qa, g_kva, g_qn, g_kn, w_uq_p, w_uk_p, w_uv, tm_wide)
    o, lse, *got = _attn_fwd(q, k, v, tq_f, tk, shards=[local[name].astype(BF16) for name, _, _ in late])
    take_gathered(late, got)
    x1, mixed = _mix_out(h_f, h_b, yg, o, x2d, g_lru, g_mla, full["w_out"], tm)
    km, vm = _mem_kv(mem2d, g_mem, full["w_mem_kv"], g_mk)
    x2, o_mem = _mem_attn(x1, g_memattn, full["w_mem_q"], g_mq, km, vm, full["w_mem_o"], tm_wide)
    gu_pre, hb_ffn = _ffn_up(x2, g_ffn, full["w_up"], tm)
    dy, dyb, act, gu_conv, loss_acc = _ffn_down_loss(gu_pre, x2, tgt, ffn_cw, full["w_down"], tm_ffn)
    loss = lax.psum(loss_acc[0, 0] * (0.5 / D_MODEL), ("x", "y", "c"))

    grads = {}
    grads["w_down"] = _matmul_tn(act, dyb, "grad_w_down", out_dtype=BF16)
    (dgu,) = _ffn_bwd_act(dyb, gu_conv, full["w_down"], tm_ffn)
    dpre, g_conv = _ffn_bwd_conv(dgu, gu_pre, ffn_cw, tm_ffn)
    grads["ffn_conv_w"], grads["ffn_conv_b"] = g_conv[:3], g_conv[3]
    grads["w_up"] = _matmul_tn(hb_ffn, dpre, "grad_w_up", col_shards=True, out_dtype=BF16)
    dx2, dx2b, gg = _ffn_bwd_in(dpre, x2, dy, g_ffn, full["w_up"], tm)
    grads["ffn_norm"] = gg[0]
    grads["w_mem_o"] = _matmul_tn(o_mem, dx2b, "grad_w_mem_o", col_shards=True, out_dtype=BF16)
    dx1, dx1b, hm, dqr_mem, dkm, dvm, gg, ggq = _mem_attn_bwd(x1, dx2, dx2b, g_memattn, full["w_mem_q"], g_mq, km, vm,
                                                                 full["w_mem_o"], tm)
    grads["mem_attn_norm"], grads["mem_q_norm"] = gg[0], ggq[0]
    grads["w_mem_q"] = _matmul_tn(hm, dqr_mem, "grad_w_mem_q", out_dtype=BF16)
    g_mem_kv, gg, ggk, _ = _mem_kv_bwd(mem2d, g_mem, full["w_mem_kv"], g_mk, dkm, dvm)
    grads["w_mem_kv"] = g_mem_kv.astype(BF16)
    grads["mem_norm"], grads["mem_k_norm"] = gg[0], ggk[0]
    grads["w_out"] = _matmul_tn(mixed, dx1b, "grad_w_out", out_dtype=BF16)
    dh, dyg, dob, dl128, ggl, ggm = _mix_out_bwd(dx1b, h_f, h_b, yg, o, g_lru, g_mla, full["w_out"], tm)
    grads["lru_out_norm"], grads["mla_out_norm"] = ggl[0], ggm[0]
    delta_t = jnp.transpose(dl128[:, :HEADS]).reshape(HEADS // 2, 2, s)
    def halves(name, shape, axis):
        g4 = grads[name] if grads[name].ndim == 3 else _grad_shards(grads[name], shape, axis)
        return g4.reshape(N_CHIPS, 2, g4.shape[1] // 2, g4.shape[2])

    dq, dk, dv, *arrived_late = _attn_bwd(q, k, v, dob, lse, delta_t, tq_b, tk,
                                          contributions=[halves(*e) for e in late])
    (dcq, dckv, dkrp, cqb, dqr, ckvb, dkn, dvb, ggqa, ggkva, ggqn, ggkn) = _mla_qkv_bwd(
        cq, ckv, krp, cos_t, sin_t, dq, dk, dv, g_qa, g_kva, g_qn, g_kn, w_uq_p, w_uk_p, w_uv, tm_wide)
    grads["q_a_norm"], grads["kv_a_norm"] = ggqa[0], ggkva[0]
    grads["mla_q_norm"], grads["mla_k_norm"] = ggqn[0, :QK_HEAD], ggkn[0, :QK_HEAD]
    g_uq_p = _matmul_tn(cqb, dqr, "grad_w_uq")
    grads["w_uq"] = g_uq_p.reshape(Q_LORA, HEADS, LANES)[:, :, :QK_HEAD].reshape(Q_LORA, -1)
    g_uk_p = _matmul_tn(ckvb, dkn, "grad_w_uk").reshape(KV_LORA, HEADS, LANES)[:, :, :QK_NOPE]
    g_uv = _matmul_tn(ckvb, dvb, "grad_w_uv").reshape(KV_LORA, HEADS, V_DIM)
    grads["w_ukv"] = jnp.concatenate([g_uk_p, g_uv], axis=2).reshape(KV_LORA, -1)
    dxr, gwa, gwi, gvec = [], [], [], []
    for d, hd in enumerate((h_f, h_b)):
        r = _lru_scan_bwd(xr, hd, dh, cw[d], pv[d], wa[d], wi[d], d == 1, t_scan)
        dxr.append(r[0])
        gwa.append(_block_diag_t(r[1]))
        gwi.append(_block_diag_t(r[2]))
        gvec.append(r[3])
    grads["lru_w_a"], grads["lru_w_i"] = jnp.stack(gwa), jnp.stack(gwi)
    grads["lru_conv_w"] = jnp.stack([gv[:CONV_W] for gv in gvec])
    for r_i, name in ((4, "lru_conv_b"), (5, "lru_b_a"), (6, "lru_b_i"), (7, "lru_lambda")):
        grads[name] = jnp.stack([gv[r_i] for gv in gvec])
    grad_x, dproj, gg = _in_proj_bwd(x2d, dx1, dxr[0], dxr[1], dyg, dcq, dckv, dkrp, g_attn, w_in_p, tm)
    grads["attn_norm"] = gg[0]
    g_in_p = _matmul_tn(hb_in, dproj, "grad_w_in")
    grads["w_in"] = jnp.concatenate([g_in_p[:, :OFF_KR], g_in_p[:, OFF_KR + QK_NOPE:OFF_KR + QK_HEAD]], axis=1)

    small = _pack_small_grads(grads)
    length = small.shape[1]
    hrows = _round_up(-(-length // (2 * LANES)), 16)
    small = jnp.pad(small, ((0, 0), (0, 2 * hrows * LANES - length))).reshape(N_CHIPS, 2, hrows, LANES)
    for name, _, _ in early:
        grads[name] = grads[name].astype(BF16)
    arrived_early = _to_owner([halves(*e) for e in early] + [small], "grad_to_owner")
    names = [name for name, _, _ in late + early] + ["small"]
    c_idx = lax.axis_index("c").astype(jnp.int32).reshape(1)
    reduced = _join_halves([_sum_devices(b, c_idx, "grad_sum_" + n)
                            for n, b in zip(names, list(arrived_late) + list(arrived_early))])

    outs = [{}, {}, {}, {}]
    for (name, shape, axis), r in zip(late + early, reduced):
        g2 = r.reshape(_shard_shape(shape, axis))
        res = _adamw(local[name], g2, given["m_" + name][0], given["v_" + name][0], "adamw_" + name)
        for o_, a in zip(outs, (g2, *res)):
            o_[name] = a[None]
    pack = lambda prefix: _pad_rows(_pack_small_local({n: given[prefix + n] for n in WEIGHT_ORDER}), 2 * hrows)
    g_small = reduced[-1].reshape(2 * hrows, LANES)
    res = _adamw(pack(""), g_small, pack("m_"), pack("v_"), "adamw_small")
    for o_, a in zip(outs, (g_small, *res)):
        o_.update(_unpack_small_local(a.reshape(-1)))
    return (loss, grad_x[None], *[o_[n] for o_ in outs for n in WEIGHT_ORDER])
```

```python
import functools

import jax
import jax.numpy as jnp
from jax import lax
from jax.experimental import pallas as pl
from jax.experimental.pallas import tpu as pltpu

F32, BF16 = jnp.float32, jnp.bfloat16
MESH = pl.DeviceIdType.MESH

D_MODEL = 1024
EPS = 1e-6
LRU_W = 512
LRU_BLOCKS = 8
LRU_C = 8.0
CONV_W = 4
HEADS = 8
QK_NOPE, QK_ROPE, QK_HEAD, V_DIM = 64, 32, 96, 64
Q_LORA, KV_LORA = 256, 128
MLA_W = HEADS * V_DIM
ROPE_THETA = 10000.0
IN_COLS = 2 * LRU_W + Q_LORA + KV_LORA + QK_ROPE
OFF_KR = IN_COLS - QK_ROPE
IN_PAD = 1536
MEM_HEADS, MEM_HD = 4, 128
MEM_W = MEM_HEADS * MEM_HD
D_FF = 2816
N_CHIPS = 4
ADAM_LR, ADAM_B1, ADAM_B2, ADAM_EPS, ADAM_WD, ADAM_STEP = 0.001, 0.9, 0.999, 1e-08, 0.01, 10

LANES = 128
SUBLANES = 8
VMEM_LIMIT = 56 * 1024 * 1024
PACK_ROWS = 2048

SHARDED = (
    ("w_in", (D_MODEL, IN_COLS), 1, True),
    ("lru_conv_w", (2, CONV_W, LRU_W), 2, False),
    ("lru_conv_b", (2, LRU_W), 1, False),
    ("lru_b_a", (2, LRU_W), 1, False),
    ("lru_b_i", (2, LRU_W), 1, False),
    ("lru_lambda", (2, LRU_W), 1, False),
    ("w_uq", (Q_LORA, HEADS * QK_HEAD), 1, True),
    ("w_ukv", (KV_LORA, HEADS * (QK_NOPE + V_DIM)), 1, True),
    ("w_out", (2 * LRU_W, D_MODEL), 0, True),
    ("w_mem_q", (D_MODEL, MEM_W), 0, True),
    ("w_mem_kv", (D_MODEL, 2 * MEM_W), 0, True),
    ("w_mem_o", (MEM_W, D_MODEL), 1, True),
    ("w_up", (D_MODEL, 2 * D_FF), 1, True),
    ("ffn_conv_w", (3, 2 * D_FF), 1, False),
    ("w_down", (D_FF, D_MODEL), 0, True),
)
REPLICATED = (
    ("attn_norm", (D_MODEL,)), ("lru_w_a", (2, LRU_BLOCKS, 64, 64)), ("lru_w_i", (2, LRU_BLOCKS, 64, 64)),
    ("q_a_norm", (Q_LORA,)), ("kv_a_norm", (KV_LORA,)), ("mla_q_norm", (QK_HEAD,)), ("mla_k_norm", (QK_HEAD,)),
    ("lru_out_norm", (LRU_W,)), ("mla_out_norm", (MLA_W,)), ("mem_attn_norm", (D_MODEL,)), ("mem_norm", (D_MODEL,)),
    ("mem_q_norm", (MEM_HD,)), ("mem_k_norm", (MEM_HD,)), ("ffn_norm", (D_MODEL,)), ("ffn_conv_b", (2 * D_FF,)),
)
WEIGHT_ORDER = ('attn_norm', 'w_in', 'lru_conv_w', 'lru_conv_b', 'lru_w_a', 'lru_b_a', 'lru_w_i', 'lru_b_i', 'lru_lambda',
                'q_a_norm', 'w_uq', 'kv_a_norm', 'w_ukv', 'mla_q_norm', 'mla_k_norm', 'lru_out_norm', 'mla_out_norm', 'w_out',
                'mem_attn_norm', 'mem_norm', 'w_mem_q', 'w_mem_kv', 'mem_q_norm', 'mem_k_norm', 'w_mem_o', 'ffn_norm', 'w_up',
                'ffn_conv_w', 'ffn_conv_b', 'w_down')


def _numel(shape):
    n = 1
    for s in shape:
        n *= s
    return n


def _cparams(n_axes):
    return pltpu.CompilerParams(dimension_semantics=("arbitrary",) * n_axes, vmem_limit_bytes=VMEM_LIMIT)


def _bdot(a, b):
    return jnp.dot(a.astype(BF16), b.astype(BF16), preferred_element_type=F32)


def _bdot_nt(a, b):
    return lax.dot_general(a.astype(BF16), b.astype(BF16), (((1,), (1,)), ((), ())), preferred_element_type=F32)


def _bdot_tn(a, b):
    return lax.dot_general(a.astype(BF16), b.astype(BF16), (((0,), (0,)), ((), ())), preferred_element_type=F32)


def _rstd(x, n=None):
    n = x.shape[-1] if n is None else n
    return lax.rsqrt(jnp.sum(x * x, axis=-1, keepdims=True) * (1.0 / n) + EPS)


def _norm_bwd(x, rs, g, dy, n=None):
    n = x.shape[-1] if n is None else n
    xhat = x * rs
    dxh = dy * g
    dx = rs * (dxh - xhat * (jnp.sum(dxh * xhat, axis=-1, keepdims=True) * (1.0 / n)))
    return dx, dy * xhat


def _acc_row(ref, r, val):
    ref[r:r + 1, :] += jnp.sum(val, axis=0, keepdims=True)


def _zero_first(i, *refs):
    @pl.when(i == 0)
    def _():
        for r in refs:
            r[...] = jnp.zeros_like(r)


def _shift_down(x, j, halo):
    if j == 0:
        return x
    xs = pltpu.roll(x, j, 0)
    hs = pltpu.roll(halo, j, 0)
    row = lax.broadcasted_iota(jnp.int32, hs.shape, 0)
    top = jnp.where(row < j, hs, xs[:SUBLANES])
    return jnp.concatenate([top, xs[SUBLANES:]], axis=0)


def _shift_up(x, j, halo):
    if j == 0:
        return x
    t = x.shape[0]
    xs = pltpu.roll(x, t - j, 0)
    hs = pltpu.roll(halo, SUBLANES - j, 0)
    row = lax.broadcasted_iota(jnp.int32, hs.shape, 0)
    bot = jnp.where(row >= SUBLANES - j, hs, xs[t - SUBLANES:])
    return jnp.concatenate([xs[:t - SUBLANES], bot], axis=0)


def _shift(x, j, halo, down):
    return _shift_down(x, j, halo) if down else _shift_up(x, j, halo)


def _scan(a, b, h_in, down):
    t, c = a.shape
    g = t // SUBLANES
    a3, b3 = a.reshape(g, SUBLANES, c), b.reshape(g, SUBLANES, c)
    sub = lax.broadcasted_iota(jnp.int32, a3.shape, 1)
    d = 1
    while d < SUBLANES:
        keep = (sub >= d) if down else (sub < SUBLANES - d)
        shift = d if down else SUBLANES - d
        a_s = jnp.where(keep, pltpu.roll(a3, shift, 1), 1.0)
        b_s = jnp.where(keep, pltpu.roll(b3, shift, 1), 0.0)
        b3 = a3 * b_s + b3
        a3 = a3 * a_s
        d *= 2
    hs = [None] * g
    carry = h_in
    for i in (range(g) if down else range(g - 1, -1, -1)):
        hs[i] = a3[i] * carry + b3[i]
        carry = hs[i][SUBLANES - 1:, :] if down else hs[i][:1, :]
    return jnp.concatenate(hs, axis=0)


def _sigmoid(x):
    return 0.5 * jnp.tanh(0.5 * x) + 0.5


LOG2E = 1.4426950408889634
GELU_K = 0.7978845608028654
GELU_C = 0.044715


def _gelu(x):
    return 0.5 * x * (1.0 + jnp.tanh(GELU_K * (x + GELU_C * x * x * x)))


def _gelu_grad(x):
    t = jnp.tanh(GELU_K * (x + GELU_C * x * x * x))
    return 0.5 * (1.0 + t) + 0.5 * x * (1.0 - t * t) * GELU_K * (1.0 + 3.0 * GELU_C * x * x)


def _rope_partner(x):
    lane = lax.broadcasted_iota(jnp.int32, x.shape, 1)
    half = QK_ROPE // 2
    sw = jnp.where(lane < QK_NOPE + half, pltpu.roll(x, LANES - half, 1), pltpu.roll(x, half, 1))
    return jnp.where((lane >= QK_NOPE) & (lane < QK_HEAD), sw, 0.0)


def _rope(x, cos_t, sin_t):
    return x * cos_t + _rope_partner(x) * sin_t


def _rope_t(dy, cos_t, sin_t):
    return dy * cos_t + _rope_partner(dy * sin_t)


def _rowwise(body, name, s, tm, rows=(), halos=(), fulls=(), outs=(), accs=()):
    n = s // tm
    hb = tm // SUBLANES
    last8 = s // SUBLANES - 1
    in_specs, args = [], []
    for a in rows:
        in_specs.append(pl.BlockSpec((tm, a.shape[1]), lambda i: (i, 0)))
        args.append(a)
    for a in halos:
        in_specs.append(pl.BlockSpec((SUBLANES, a.shape[1]), lambda i: (jnp.maximum(i * hb - 1, 0), 0)))
        in_specs.append(pl.BlockSpec((SUBLANES, a.shape[1]), lambda i: (jnp.minimum((i + 1) * hb, last8), 0)))
        args += [a, a]
    for a in fulls:
        in_specs.append(pl.BlockSpec(a.shape, lambda i, nd=a.ndim: (0,) * nd))
        args.append(a)
    out_shape, out_specs = [], []
    for c, dt in outs:
        out_shape.append(jax.ShapeDtypeStruct((s, c), dt))
        out_specs.append(pl.BlockSpec((tm, c), lambda i: (i, 0)))
    for shp, dt in accs:
        out_shape.append(jax.ShapeDtypeStruct(shp, dt))
        out_specs.append(pl.BlockSpec(shp, lambda i, nd=len(shp): (0,) * nd))

    def kern(*refs):
        body(pl.program_id(0), n, *refs)

    return pl.pallas_call(kern, grid=(n,), in_specs=in_specs, out_specs=out_specs, out_shape=out_shape, name=name,
                          compiler_params=_cparams(1))(*args)


def _matmul_tn(a, b, name, col_shards=False, out_dtype=F32):
    t, m = a.shape
    n = b.shape[1]
    bm = m
    for cand in range(LANES, m + 1, LANES):
        if m % cand == 0 and cand * (n // N_CHIPS if col_shards else min(n, 2048)) * 4 <= 6 * 1024 * 1024:
            bm = cand
    bn = n // N_CHIPS if col_shards else (n if n <= 2048 else 1408)
    bt = min(512, t)
    nt = t // bt

    def kern(a_ref, b_ref, o_ref, acc_ref):
        k = pl.program_id(2)

        @pl.when(k == 0)
        def _():
            acc_ref[...] = jnp.zeros_like(acc_ref)
        acc_ref[...] += _bdot_tn(a_ref[...], b_ref[...])

        @pl.when(k == nt - 1)
        def _():
            o_ref[...] = acc_ref[...].astype(out_dtype)

    if col_shards:
        out_spec = pl.BlockSpec((None, bm, bn), lambda i, j, k: (j, i, 0))
        out_shape = jax.ShapeDtypeStruct((N_CHIPS, m, bn), out_dtype)
    else:
        out_spec = pl.BlockSpec((bm, bn), lambda i, j, k: (i, j))
        out_shape = jax.ShapeDtypeStruct((m, n), out_dtype)
    return pl.pallas_call(
        kern, grid=(m // bm, n // bn, nt),
        in_specs=[pl.BlockSpec((bt, bm), lambda i, j, k: (k, i)), pl.BlockSpec((bt, bn), lambda i, j, k: (k, j))],
        out_specs=out_spec, out_shape=out_shape, scratch_shapes=[pltpu.VMEM((bm, bn), F32)], name=name,
        compiler_params=_cparams(3))(a, b)


def _in_proj(x, g, w_in_p, tm):
    def body(i, n, x_ref, g_ref, w_ref, xr, yg, cq, ckv, krp, hb):
        xv = x_ref[...]
        h = (xv * _rstd(xv) * g_ref[...]).astype(BF16)
        hb[...] = h
        p = jnp.dot(h, w_ref[...], preferred_element_type=F32)
        xr[...] = p[:, :LRU_W]
        yg[...] = p[:, LRU_W:2 * LRU_W]
        cq[...] = p[:, 2 * LRU_W:2 * LRU_W + Q_LORA]
        ckv[...] = p[:, 2 * LRU_W + Q_LORA:OFF_KR]
        krp[...] = p[:, OFF_KR:IN_PAD]

    return _rowwise(body, "in_proj", x.shape[0], tm, rows=[x], fulls=[g, w_in_p],
                    outs=[(LRU_W, F32), (LRU_W, F32), (Q_LORA, F32), (KV_LORA, F32), (LANES, F32), (D_MODEL, BF16)])


def _lru_gates(x, halo, cw_ref, pv_ref, wa_ref, wi_ref, rev):
    down = not rev
    xc = pv_ref[0:1, :] + jnp.zeros_like(x)
    for j in range(CONV_W):
        k = j if rev else CONV_W - 1 - j
        xc = xc + cw_ref[k:k + 1, :] * _shift(x, j, halo, down)
    r = _sigmoid(_bdot(xc, wa_ref[...]) + pv_ref[1:2, :])
    ig = _sigmoid(_bdot(xc, wi_ref[...]) + pv_ref[2:3, :])
    lam = pv_ref[3:4, :]
    sp = jnp.maximum(-lam, 0.0) + jnp.log(1.0 + jnp.exp(-jnp.abs(lam)))
    log_a = (-LRU_C) * r * sp
    a = jnp.exp(log_a)
    z = 2.0 * log_a
    series = -(z * (1.0 + z * (0.5 + z * (1.0 / 6.0 + z * (1.0 / 24.0)))))
    om = jnp.where(z > -0.02, series, 1.0 - jnp.exp(z))
    mult = jnp.sqrt(om)
    return xc, r, ig, sp, a, mult


def _lru_scan_fwd(xr, cw, pv, wa, wi, rev, t):
    s = xr.shape[0]
    n = s // t
    hb = t // SUBLANES
    last8 = s // SUBLANES - 1
    down = not rev

    def kern(x_ref, halo_ref, cw_ref, pv_ref, wa_ref, wi_ref, h_ref, xc_ref, r_ref, ig_ref, a_ref, mult_ref, carry_ref):
        i = pl.program_id(0)
        _zero_first(i, carry_ref)
        halo = jnp.where(i == 0, 0.0, halo_ref[...])
        xc, r, ig, sp, a, mult = _lru_gates(x_ref[...], halo, cw_ref, pv_ref, wa_ref, wi_ref, rev)
        xc_ref[...], r_ref[...], ig_ref[...], a_ref[...], mult_ref[...] = xc, r, ig, a, mult
        h_ref[...] = _scan(a, mult * ig * xc, carry_ref[...], down)
        carry_ref[...] = h_ref[pl.ds(t - 1 if down else 0, 1), :]

    if rev:
        blk = lambda i: (n - 1 - i, 0)
        hal = lambda i: (jnp.minimum((n - i) * hb, last8), 0)
    else:
        blk = lambda i: (i, 0)
        hal = lambda i: (jnp.maximum(i * hb - 1, 0), 0)
    full = lambda a: pl.BlockSpec(a.shape, lambda i: (0, 0))
    return pl.pallas_call(
        kern, grid=(n,),
        in_specs=[pl.BlockSpec((t, LRU_W), blk), pl.BlockSpec((SUBLANES, LRU_W), hal), full(cw), full(pv), full(wa), full(wi)],
        out_specs=[pl.BlockSpec((t, LRU_W), blk)] * 6, out_shape=[jax.ShapeDtypeStruct((s, LRU_W), F32)] * 6,
        scratch_shapes=[pltpu.VMEM((1, LRU_W), F32)], name="lru_scan_rev" if rev else "lru_scan_fwd",
        compiler_params=_cparams(1))(xr, xr, cw, pv, wa, wi)


def _lru_scan_bwd(xr, saved, h, dh, cw, pv, wa, wi, rev, t):
    s = xr.shape[0]
    n = s // t
    hb = t // SUBLANES
    last8 = s // SUBLANES - 1
    down = not rev

    def kern(x_ref, xc_ref, r_ref, ig_ref, a_ref, mult_ref, h_ref, hh_ref, dh_ref, cw_ref, pv_ref, wa_ref, wi_ref,
             dx_ref, gwa_ref, gwi_ref, gv_ref, p_ref, dxc_halo_ref, tmp_ref):
        i = pl.program_id(0)
        _zero_first(i, gwa_ref, gwi_ref, gv_ref, p_ref, dxc_halo_ref)
        at_start = i == n - 1
        x = x_ref[...]
        hhalo = jnp.where(at_start, 0.0, hh_ref[...])
        xc, r, ig, a, mult = xc_ref[...], r_ref[...], ig_ref[...], a_ref[...], mult_ref[...]
        lam = pv_ref[3:4, :]
        sp = jnp.maximum(-lam, 0.0) + jnp.log(1.0 + jnp.exp(-jnp.abs(lam)))
        h_prev = _shift(h_ref[...], 1, hhalo, down)
        row = lax.broadcasted_iota(jnp.int32, x.shape, 0)
        edge = t - 1 if down else 0
        dh_mod = dh_ref[...] + jnp.where(row == edge, p_ref[...], 0.0)
        a_next = _shift(a, 1, jnp.zeros((SUBLANES, LRU_W), F32), not down)
        g = _scan(a_next, dh_mod, jnp.zeros((1, LRU_W), F32), not down)
        tmp_ref[...] = a * g
        p_ref[...] = tmp_ref[pl.ds(0 if down else t - 1, 1), :]
        da = g * h_prev
        d_ig = g * mult * xc
        d_xc = g * mult * ig
        d_om = g * ig * xc * (0.5 / jnp.maximum(mult, 1e-30))
        d_log_a = da * a - 2.0 * d_om * a * a
        d_r = d_log_a * ((-LRU_C) * sp)
        d_sp = jnp.sum(d_log_a * ((-LRU_C) * r), axis=0, keepdims=True)
        gv_ref[7:8, :] += d_sp * (-_sigmoid(-lam))
        d_ga = d_r * r * (1.0 - r)
        d_gi = d_ig * ig * (1.0 - ig)
        _acc_row(gv_ref, 5, d_ga)
        _acc_row(gv_ref, 6, d_gi)
        d_xc = d_xc + _bdot_nt(d_ga, wa_ref[...]) + _bdot_nt(d_gi, wi_ref[...])
        gwa_ref[...] += _bdot_tn(xc, d_ga)
        gwi_ref[...] += _bdot_tn(xc, d_gi)
        _acc_row(gv_ref, 4, d_xc)
        dx = jnp.zeros_like(x)
        dxc_halo = dxc_halo_ref[...]
        for j in range(CONV_W):
            k = j if rev else CONV_W - 1 - j
            d_shift = _shift(d_xc, j, dxc_halo, not down)
            _acc_row(gv_ref, k, d_shift * x)
            dx = dx + cw_ref[k:k + 1, :] * d_shift
        dx_ref[...] = dx
        dxc_halo_ref[...] = d_xc[:SUBLANES] if down else d_xc[t - SUBLANES:]

    if rev:
        blk = lambda i: (i, 0)
        hal = lambda i: (jnp.minimum((i + 1) * hb, last8), 0)
    else:
        blk = lambda i: (n - 1 - i, 0)
        hal = lambda i: (jnp.maximum((n - 1 - i) * hb - 1, 0), 0)
    full = lambda a: pl.BlockSpec(a.shape, lambda i: (0, 0))
    bs = pl.BlockSpec((t, LRU_W), blk)
    hs = pl.BlockSpec((SUBLANES, LRU_W), hal)
    return pl.pallas_call(
        kern, grid=(n,),
        in_specs=[bs] * 7 + [hs, bs, full(cw), full(pv), full(wa), full(wi)],
        out_specs=[bs, pl.BlockSpec((LRU_W, LRU_W), lambda i: (0, 0)), pl.BlockSpec((LRU_W, LRU_W), lambda i: (0, 0)),
                   pl.BlockSpec((SUBLANES, LRU_W), lambda i: (0, 0))],
        out_shape=[jax.ShapeDtypeStruct((s, LRU_W), F32), jax.ShapeDtypeStruct((LRU_W, LRU_W), F32),
                   jax.ShapeDtypeStruct((LRU_W, LRU_W), F32), jax.ShapeDtypeStruct((SUBLANES, LRU_W), F32)],
        scratch_shapes=[pltpu.VMEM((1, LRU_W), F32), pltpu.VMEM((SUBLANES, LRU_W), F32), pltpu.VMEM((t, LRU_W), F32)],
        name="lru_bwd_rev" if rev else "lru_bwd_fwd", compiler_params=_cparams(1))(xr, *saved, h, h, dh, cw, pv, wa, wi)


def _mla_qkv(cq, ckv, krp, cos_t, sin_t, g_qa, g_kva, g_qn, g_kn, w_uq_p, w_uk_p, w_uv, tm):
    scale = QK_HEAD ** -0.5 * LOG2E

    def body(i, n, cq_ref, ckv_ref, kr_ref, c_ref, s_ref, gqa, gkva, gqn, gkn, wq, wk, wv, q_out, k_out, v_out):
        cosv, sinv = c_ref[...], s_ref[...]
        cqv = cq_ref[...]
        qr = _bdot(cqv * _rstd(cqv) * gqa[...], wq[...])
        ckvv = ckv_ref[...]
        c_kv = (ckvv * _rstd(ckvv) * gkva[...]).astype(BF16)
        kn = jnp.dot(c_kv, wk[...], preferred_element_type=F32)
        v_out[...] = jnp.dot(c_kv, wv[...], preferred_element_type=F32).astype(BF16)
        kr = kr_ref[...]
        kr_swapped = _rope_partner(kr * gkn[...]) * sinv
        for h in range(HEADS):
            sl = slice(h * LANES, (h + 1) * LANES)
            qh = qr[:, sl]
            qh = _rope(qh * _rstd(qh, QK_HEAD) * gqn[...], cosv, sinv) * scale
            q_out[:, sl] = qh.astype(BF16)
            kh = kn[:, sl] + kr
            rs = _rstd(kh, QK_HEAD)
            k_out[:, sl] = (kh * rs * gkn[...] * cosv + kr_swapped * rs).astype(BF16)

    return _rowwise(body, "mla_qkv", cq.shape[0], tm, rows=[cq, ckv, krp, cos_t, sin_t],
                    fulls=[g_qa, g_kva, g_qn, g_kn, w_uq_p, w_uk_p, w_uv],
                    outs=[(HEADS * LANES, BF16), (HEADS * LANES, BF16), (MLA_W, BF16)])


NT_DIMS = (((1,), (1,)), ((), ()))
TN_DIMS = (((0,), (0,)), ((), ()))


def _riding_exchange(copies_fn, first, last):
    @pl.when(first)
    def _():
        for cp in copies_fn():
            cp.start()

    def finish():
        @pl.when(last)
        def _():
            for cp in copies_fn():
                cp.wait()
    return finish


def _attn_fwd(q, k, v, tq, tk, shards=()):
    s = q.shape[0]
    nq, nk = s // tq, s // tk
    n = len(shards)

    def kern(*refs):
        q_ref, k_ref, v_ref = refs[:3]
        o_ref, lse_ref = refs[3 + n:5 + n]
        acc_ref = refs[5 + 2 * n]
        p_id, i_id = pl.program_id(0), pl.program_id(1)
        finish = _riding_exchange(lambda: _gather_copies(refs[3:3 + n], refs[5 + n:5 + 2 * n], *refs[6 + 2 * n:]),
                                  (p_id == 0) & (i_id == 0), (p_id == HEADS // 2 - 1) & (i_id == nq - 1)) if n else None
        qs = (q_ref[:, :LANES], q_ref[:, LANES:])
        acc_ref[...] = jnp.zeros_like(acc_ref)

        def step(j, carry):
            off = pl.multiple_of(j * tk, tk)
            vc = v_ref[pl.ds(off, tk), :]
            out = []
            for h in range(2):
                m, l = carry[2 * h:2 * h + 2]
                st = lax.dot_general(k_ref[pl.ds(off, tk), h * LANES:(h + 1) * LANES], qs[h], NT_DIMS,
                                     preferred_element_type=F32)
                mn = jnp.maximum(m, jnp.max(st, axis=0, keepdims=True))
                al = jnp.exp2(m - mn)
                pt = jnp.exp2(st - mn)
                l = al * l + jnp.sum(pt, axis=0, keepdims=True)
                acc_ref[h] = al * acc_ref[h] + lax.dot_general(vc, pt.astype(BF16), TN_DIMS, preferred_element_type=F32)
                out += [mn, l]
            return tuple(out)

        init = (jnp.full((1, tq), -1e30, F32), jnp.zeros((1, tq), F32)) * 2
        m0, l0, m1, l1 = lax.fori_loop(0, nk, step, init)
        row = lax.broadcasted_iota(jnp.int32, (LANES, tq), 0)
        o_ref[...] = jnp.where(row < V_DIM, acc_ref[0] / l0, acc_ref[1] / l1).T
        lse_ref[0, 0:1, :] = m0 + jnp.log2(l0)
        lse_ref[0, 1:2, :] = m1 + jnp.log2(l1)
        if n:
            finish()

    return pl.pallas_call(
        kern, grid=(HEADS // 2, nq),
        in_specs=[pl.BlockSpec((tq, 2 * LANES), lambda p, i: (i, p)), pl.BlockSpec((s, 2 * LANES), lambda p, i: (0, p)),
                  pl.BlockSpec((s, LANES), lambda p, i: (0, p))] + [ANY] * n,
        out_specs=[pl.BlockSpec((tq, LANES), lambda p, i: (i, p)), pl.BlockSpec((1, 2, tq), lambda p, i: (p, 0, i))]
        + [ANY] * n,
        out_shape=[jax.ShapeDtypeStruct((s, MLA_W), F32), jax.ShapeDtypeStruct((HEADS // 2, 2, s), F32)]
        + _gather_shapes(shards),
        scratch_shapes=[pltpu.VMEM((2, LANES, tq), F32)] + (_gather_sems(n) if n else []),
        name="attn_fwd", compiler_params=_cparams(2))(q, k, v, *shards)


def _attn_bwd(q, k, v, do, lse, delta, tq, tk, contributions=()):
    s = q.shape[0]
    nq, nk = s // tq, s // tk
    n = len(contributions)

    def kern(*refs):
        q_ref, do_ref, lse_ref, dl_ref, k_ref, v_ref = refs[:6]
        dq_ref, dk_ref, dv_ref = refs[6 + n:9 + n]
        acc_ref = refs[9 + 2 * n]
        p_id, i_id = pl.program_id(0), pl.program_id(1)
        finish = _riding_exchange(lambda: _to_owner_copies(refs[6:6 + n], refs[9 + n:9 + 2 * n], *refs[10 + 2 * n:]),
                                  (p_id == 0) & (i_id == 0), (p_id == HEADS // 2 - 1) & (i_id == nq - 1)) if n else None
        _zero_first(pl.program_id(1), dk_ref, dv_ref)
        acc_ref[...] = jnp.zeros_like(acc_ref)
        qs = (q_ref[:, :LANES], q_ref[:, LANES:])
        doc = do_ref[...]
        lane_q = lax.broadcasted_iota(jnp.int32, (tq, LANES), 1)
        zq = jnp.zeros_like(doc)
        dos = (jnp.where(lane_q < V_DIM, doc, zq), jnp.where(lane_q >= V_DIM, doc, zq))
        lses = (lse_ref[0, 0:1, :], lse_ref[0, 1:2, :])
        dls = (dl_ref[0, 0:1, :], dl_ref[0, 1:2, :])

        def step(j, carry):
            off = pl.multiple_of(j * tk, tk)
            vp = v_ref[pl.ds(off, tk), :]
            lane_k = lax.broadcasted_iota(jnp.int32, (tk, LANES), 1)
            zero = jnp.zeros_like(vp)
            vs = (jnp.where(lane_k < V_DIM, vp, zero), jnp.where(lane_k >= V_DIM, vp, zero))
            for h in range(2):
                sl = slice(h * LANES, (h + 1) * LANES)
                st = lax.dot_general(k_ref[pl.ds(off, tk), sl], qs[h], NT_DIMS, preferred_element_type=F32)
                pt = jnp.exp2(st - lses[h])
                dpt = lax.dot_general(vs[h], doc, NT_DIMS, preferred_element_type=F32)
                dst = (pt * (dpt - dls[h])).astype(BF16)
                dv_ref[pl.ds(off, tk), :] += jnp.dot(pt.astype(BF16), dos[h], preferred_element_type=F32)
                dk_ref[pl.ds(off, tk), sl] += jnp.dot(dst, qs[h], preferred_element_type=F32)
                acc_ref[h] += lax.dot_general(k_ref[pl.ds(off, tk), sl], dst, TN_DIMS, preferred_element_type=F32)
            return carry

        lax.fori_loop(0, nk, step, 0)
        dq_ref[:, :LANES] = acc_ref[0].T
        dq_ref[:, LANES:] = acc_ref[1].T
        if n:
            finish()

    return pl.pallas_call(
        kern, grid=(HEADS // 2, nq),
        in_specs=[pl.BlockSpec((tq, 2 * LANES), lambda p, i: (i, p)), pl.BlockSpec((tq, LANES), lambda p, i: (i, p)),
                  pl.BlockSpec((1, 2, tq), lambda p, i: (p, 0, i)), pl.BlockSpec((1, 2, tq), lambda p, i: (p, 0, i)),
                  pl.BlockSpec((s, 2 * LANES), lambda p, i: (0, p)), pl.BlockSpec((s, LANES), lambda p, i: (0, p))]
        + [ANY] * n,
        out_specs=[pl.BlockSpec((tq, 2 * LANES), lambda p, i: (i, p)), pl.BlockSpec((s, 2 * LANES), lambda p, i: (0, p)),
                   pl.BlockSpec((s, LANES), lambda p, i: (0, p))] + [ANY] * n,
        out_shape=[jax.ShapeDtypeStruct((s, HEADS * LANES), F32), jax.ShapeDtypeStruct((s, HEADS * LANES), F32),
                   jax.ShapeDtypeStruct((s, MLA_W), F32)] + _to_owner_shapes(contributions),
        scratch_shapes=[pltpu.VMEM((2, LANES, tq), F32)] + (_to_owner_sems(n) if n else []),
        name="attn_bwd", compiler_params=_cparams(2))(q, do, lse, delta, k, v, *contributions)


def _mix_out(hf, hb, yg, o, x, g_lru, g_mla, w_out, tm):
    def body(i, n, hf_ref, hb_ref, yg_ref, o_ref, x_ref, gl, gm, w_ref, x1_ref, mix_ref):
        lo = (hf_ref[...] + hb_ref[...]) * _gelu(yg_ref[...])
        ov = o_ref[...]
        mix_ref[:, :LRU_W] = (lo * _rstd(lo) * gl[...]).astype(BF16)
        mix_ref[:, LRU_W:] = (ov * _rstd(ov) * gm[...]).astype(BF16)
        x1_ref[...] = x_ref[...] + jnp.dot(mix_ref[...], w_ref[...], preferred_element_type=F32)

    return _rowwise(body, "mix_out", x.shape[0], tm, rows=[hf, hb, yg, o, x], fulls=[g_lru, g_mla, w_out],
                    outs=[(D_MODEL, F32), (2 * LRU_W, BF16)])


def _mem_kv(mem, g_mem, w_kv, g_k):
    m = mem.shape[0]

    def body(i, n, mem_ref, g_ref, w_ref, gk_ref, km_ref, vm_ref):
        mv = mem_ref[...]
        kv = _bdot(mv * _rstd(mv) * g_ref[...], w_ref[...])
        vm_ref[...] = kv[:, MEM_W:].astype(BF16)
        for h in range(MEM_HEADS):
            sl = slice(h * MEM_HD, (h + 1) * MEM_HD)
            kh = kv[:, sl]
            km_ref[:, sl] = (kh * _rstd(kh) * gk_ref[...]).astype(BF16)

    return _rowwise(body, "mem_kv", m, m, rows=[mem], fulls=[g_mem, w_kv, g_k], outs=[(MEM_W, BF16), (MEM_W, BF16)])


def _mem_attn_core(x1v, g_ref, wq_ref, gq_ref, km_ref, vm_ref):
    scale = MEM_HD ** -0.5
    hm = (x1v * _rstd(x1v) * g_ref[...]).astype(BF16)
    qr = jnp.dot(hm, wq_ref[...], preferred_element_type=F32)
    heads = []
    for h in range(MEM_HEADS):
        sl = slice(h * MEM_HD, (h + 1) * MEM_HD)
        qh = qr[:, sl]
        rs = _rstd(qh)
        qn = (qh * rs * gq_ref[...]).astype(BF16)
        sc = lax.dot_general(qn, km_ref[:, sl], (((1,), (1,)), ((), ())), preferred_element_type=F32) * scale
        e = jnp.exp(sc - jnp.max(sc, axis=-1, keepdims=True))
        p = e / jnp.sum(e, axis=-1, keepdims=True)
        oh = jnp.dot(p.astype(BF16), vm_ref[:, sl], preferred_element_type=F32)
        heads.append((qh, rs, qn, p, oh))
    return hm, heads


def _mem_attn(x1, g, w_q, g_q, km, vm, w_o, tm):
    cs = D_MODEL // N_CHIPS

    def body(i, n, x1_ref, g_ref, wq_ref, gq_ref, km_ref, vm_ref, wo_ref, x2_ref, ob_ref):
        x1v = x1_ref[...]
        _, heads = _mem_attn_core(x1v, g_ref, wq_ref, gq_ref, km_ref, vm_ref)
        for h in range(MEM_HEADS):
            ob_ref[:, h * MEM_HD:(h + 1) * MEM_HD] = heads[h][4].astype(BF16)
        for k in range(N_CHIPS):
            sl = slice(k * cs, (k + 1) * cs)
            x2_ref[:, sl] = x1v[:, sl] + jnp.dot(ob_ref[...], wo_ref[k], preferred_element_type=F32)

    return _rowwise(body, "mem_attn", x1.shape[0], tm, rows=[x1], fulls=[g, w_q, g_q, km, vm, w_o],
                    outs=[(D_MODEL, F32), (MEM_W, BF16)])


def _ffn_up(x2, g, w_up, tm):
    cs = 2 * D_FF // N_CHIPS

    def body(i, n, x_ref, g_ref, w_ref, gu_ref, hb_ref):
        xv = x_ref[...]
        hb_ref[...] = (xv * _rstd(xv) * g_ref[...]).astype(BF16)
        for k in range(N_CHIPS):
            gu_ref[:, k * cs:(k + 1) * cs] = jnp.dot(hb_ref[...], w_ref[k], preferred_element_type=F32)

    return _rowwise(body, "ffn_up", x2.shape[0], tm, rows=[x2], fulls=[g, w_up], outs=[(2 * D_FF, F32), (D_MODEL, BF16)])


def _ffn_conv(gu, prev, nxt, cw_ref, i, n):
    prev = jnp.where(i == 0, 0.0, prev)
    nxt = jnp.where(i == n - 1, 0.0, nxt)
    return (cw_ref[3:4, :] + cw_ref[0:1, :] * _shift_down(gu, 1, prev) + cw_ref[1:2, :] * gu
            + cw_ref[2:3, :] * _shift_up(gu, 1, nxt))


def _ffn_down_loss(gu_pre, x2, target, cw, w_down, tm):
    def body(i, n, gu_ref, x_ref, t_ref, pv_ref, nx_ref, cw_ref, w_ref, dy_ref, dyb_ref, act_ref, dgu_ref, loss_ref):
        _zero_first(i, loss_ref)
        gu = _ffn_conv(gu_ref[...], pv_ref[...], nx_ref[...], cw_ref, i, n)
        g, u = gu[:, :D_FF], gu[:, D_FF:]
        sg = _sigmoid(g)
        a = g * sg
        act_ref[...] = (a * u).astype(BF16)
        y = x_ref[...] + jnp.dot(act_ref[...], w_ref[...], preferred_element_type=F32)
        e = y - t_ref[...]
        loss_ref[...] += jnp.sum(e * e)
        dy = e * (1.0 / D_MODEL)
        dy_ref[...] = dy
        dyb_ref[...] = dy.astype(BF16)
        d_act = lax.dot_general(dyb_ref[...], w_ref[...], NT_DIMS, preferred_element_type=F32)
        dgu_ref[:, :D_FF] = (d_act * u) * (sg + a - a * sg)
        dgu_ref[:, D_FF:] = d_act * a

    return _rowwise(body, "ffn_down_loss", x2.shape[0], tm, rows=[gu_pre, x2, target], halos=[gu_pre], fulls=[cw, w_down],
                    outs=[(D_MODEL, F32), (D_MODEL, BF16), (D_FF, BF16), (2 * D_FF, F32)], accs=[((SUBLANES, LANES), F32)])


def _ffn_bwd_conv(dgu, gu_pre, cw, tm):
    def body(i, n, d_ref, g_ref, dp_ref, dn_ref, cw_ref, dpre_ref, gc_ref):
        _zero_first(i, gc_ref)
        d = d_ref[...]
        g = g_ref[...]
        d_next = _shift_up(d, 1, jnp.where(i == n - 1, 0.0, dn_ref[...]))
        d_prev = _shift_down(d, 1, jnp.where(i == 0, 0.0, dp_ref[...]))
        dpre_ref[...] = (cw_ref[0:1, :] * d_next + cw_ref[1:2, :] * d + cw_ref[2:3, :] * d_prev).astype(BF16)
        _acc_row(gc_ref, 0, d_next * g)
        _acc_row(gc_ref, 1, d * g)
        _acc_row(gc_ref, 2, d_prev * g)
        _acc_row(gc_ref, 3, d)

    return _rowwise(body, "ffn_bwd_conv", dgu.shape[0], tm, rows=[dgu, gu_pre], halos=[dgu], fulls=[cw],
                    outs=[(2 * D_FF, BF16)], accs=[((SUBLANES, 2 * D_FF), F32)])


def _ffn_bwd_in(dpre, x2, dy, g, w_up, tm):
    cs = 2 * D_FF // N_CHIPS

    def body(i, n, dp_ref, x_ref, dy_ref, g_ref, w_ref, dx_ref, dxb_ref, gg_ref):
        _zero_first(i, gg_ref)
        d_h = jnp.zeros(x_ref.shape, F32)
        for k in range(N_CHIPS):
            d_h = d_h + lax.dot_general(dp_ref[:, k * cs:(k + 1) * cs], w_ref[k], (((1,), (1,)), ((), ())),
                                        preferred_element_type=F32)
        xv = x_ref[...]
        dx, dg = _norm_bwd(xv, _rstd(xv), g_ref[...], d_h)
        _acc_row(gg_ref, 0, dg)
        dx = dx + dy_ref[...]
        dx_ref[...] = dx
        dxb_ref[...] = dx.astype(BF16)

    return _rowwise(body, "ffn_bwd_in", x2.shape[0], tm, rows=[dpre, x2, dy], fulls=[g, w_up],
                    outs=[(D_MODEL, F32), (D_MODEL, BF16)], accs=[((SUBLANES, D_MODEL), F32)])


def _mem_attn_bwd(x1, dx2, dx2b, g, w_q, g_q, km, vm, w_o, tm):
    scale = MEM_HD ** -0.5
    m = km.shape[0]

    def body(i, n, x1_ref, dx2_ref, dx2b_ref, g_ref, wq_ref, gq_ref, km_ref, vm_ref, wo_ref,
             dx1_ref, dx1b_ref, hm_ref, dqr_ref, dkm_ref, dvm_ref, gg_ref, ggq_ref):
        _zero_first(i, dkm_ref, dvm_ref, gg_ref, ggq_ref)
        x1v = x1_ref[...]
        hm, heads = _mem_attn_core(x1v, g_ref, wq_ref, gq_ref, km_ref, vm_ref)
        hm_ref[...] = hm
        cs = D_MODEL // N_CHIPS
        d_o = jnp.zeros((x1v.shape[0], MEM_W), F32)
        for k in range(N_CHIPS):
            d_o = d_o + lax.dot_general(dx2b_ref[:, k * cs:(k + 1) * cs], wo_ref[k], (((1,), (1,)), ((), ())),
                                        preferred_element_type=F32)
        for h in range(MEM_HEADS):
            sl = slice(h * MEM_HD, (h + 1) * MEM_HD)
            qh, rs, qn, p, _ = heads[h]
            d_oh = d_o[:, sl].astype(BF16)
            dp = lax.dot_general(d_oh, vm_ref[:, sl], (((1,), (1,)), ((), ())), preferred_element_type=F32)
            ds = (p * (dp - jnp.sum(dp * p, axis=-1, keepdims=True)) * scale).astype(BF16)
            dqn = jnp.dot(ds, km_ref[:, sl], preferred_element_type=F32)
            dkm_ref[:, sl] += lax.dot_general(ds, qn, (((0,), (0,)), ((), ())), preferred_element_type=F32)
            dvm_ref[:, sl] += lax.dot_general(p.astype(BF16), d_oh, (((0,), (0,)), ((), ())), preferred_element_type=F32)
            dqh, dgq = _norm_bwd(qh, rs, gq_ref[...], dqn)
            _acc_row(ggq_ref, 0, dgq)
            dqr_ref[:, sl] = dqh.astype(BF16)
        d_hm = lax.dot_general(dqr_ref[...], wq_ref[...], (((1,), (1,)), ((), ())), preferred_element_type=F32)
        dx, dg = _norm_bwd(x1v, _rstd(x1v), g_ref[...], d_hm)
        _acc_row(gg_ref, 0, dg)
        dx = dx + dx2_ref[...]
        dx1_ref[...] = dx
        dx1b_ref[...] = dx.astype(BF16)

    return _rowwise(body, "mem_attn_bwd", x1.shape[0], tm, rows=[x1, dx2, dx2b], fulls=[g, w_q, g_q, km, vm, w_o],
                    outs=[(D_MODEL, F32), (D_MODEL, BF16), (D_MODEL, BF16), (MEM_W, BF16)],
                    accs=[((m, MEM_W), F32), ((m, MEM_W), F32), ((SUBLANES, D_MODEL), F32), ((SUBLANES, MEM_HD), F32)])


def _mem_kv_bwd(mem, g_mem, w_kv, g_k, dkm, dvm):
    m = mem.shape[0]

    def body(i, n, mem_ref, dkm_ref, dvm_ref, g_ref, w_ref, gk_ref, gw_ref, gg_ref, ggk_ref, dkv_ref):
        gg_ref[...] = jnp.zeros_like(gg_ref)
        ggk_ref[...] = jnp.zeros_like(ggk_ref)
        mv = mem_ref[...]
        rs_m = _rstd(mv)
        mem_n = (mv * rs_m * g_ref[...]).astype(BF16)
        kv = jnp.dot(mem_n, w_ref[...], preferred_element_type=F32)
        for h in range(MEM_HEADS):
            sl = slice(h * MEM_HD, (h + 1) * MEM_HD)
            kh = kv[:, sl]
            dkh, dgk = _norm_bwd(kh, _rstd(kh), gk_ref[...], dkm_ref[:, sl])
            _acc_row(ggk_ref, 0, dgk)
            dkv_ref[:, sl] = dkh.astype(BF16)
        dkv_ref[:, MEM_W:] = dvm_ref[...].astype(BF16)
        gw_ref[...] = lax.dot_general(mem_n, dkv_ref[...], (((0,), (0,)), ((), ())), preferred_element_type=F32)
        d_mn = lax.dot_general(dkv_ref[...], w_ref[...], (((1,), (1,)), ((), ())), preferred_element_type=F32)
        _acc_row(gg_ref, 0, d_mn * (mv * rs_m))

    return _rowwise(body, "mem_kv_bwd", m, m, rows=[mem, dkm, dvm], fulls=[g_mem, w_kv, g_k],
                    accs=[((D_MODEL, 2 * MEM_W), F32), ((SUBLANES, D_MODEL), F32), ((SUBLANES, MEM_HD), F32),
                          ((m, 2 * MEM_W), BF16)])


def _mix_out_bwd(dx1b, hf, hb, yg, o, g_lru, g_mla, w_out, tm):
    def body(i, n, dx_ref, hf_ref, hb_ref, yg_ref, o_ref, gl, gm, w_ref, dh_ref, dyg_ref, dob_ref, dl_ref, ggl_ref, ggm_ref):
        _zero_first(i, ggl_ref, ggm_ref)
        dmix = lax.dot_general(dx_ref[...], w_ref[...], (((1,), (1,)), ((), ())), preferred_element_type=F32)
        hs = hf_ref[...] + hb_ref[...]
        ygv = yg_ref[...]
        ge = _gelu(ygv)
        lo = hs * ge
        d_lo, dgl = _norm_bwd(lo, _rstd(lo), gl[...], dmix[:, :LRU_W])
        _acc_row(ggl_ref, 0, dgl)
        dh_ref[...] = d_lo * ge
        dyg_ref[...] = d_lo * hs * _gelu_grad(ygv)
        ov = o_ref[...]
        d_o, dgm = _norm_bwd(ov, _rstd(ov), gm[...], dmix[:, LRU_W:])
        _acc_row(ggm_ref, 0, dgm)
        dob_ref[...] = d_o.astype(BF16)
        prod = d_o * ov
        lane_w = lax.broadcasted_iota(jnp.int32, prod.shape, 1)
        lane = lax.broadcasted_iota(jnp.int32, (prod.shape[0], LANES), 1)
        dl = jnp.zeros((prod.shape[0], LANES), F32)
        for h in range(HEADS):
            in_head = (lane_w >= h * V_DIM) & (lane_w < (h + 1) * V_DIM)
            dl = dl + jnp.where(lane == h, jnp.sum(jnp.where(in_head, prod, 0.0), axis=-1, keepdims=True), 0.0)
        dl_ref[...] = dl

    return _rowwise(body, "mix_out_bwd", dx1b.shape[0], tm, rows=[dx1b, hf, hb, yg, o], fulls=[g_lru, g_mla, w_out],
                    outs=[(LRU_W, F32), (LRU_W, F32), (MLA_W, BF16), (LANES, F32)],
                    accs=[((SUBLANES, LRU_W), F32), ((SUBLANES, MLA_W), F32)])


def _mla_qkv_bwd(cq, ckv, krp, cos_t, sin_t, dq, dk, dv, g_qa, g_kva, g_qn, g_kn, w_uq_p, w_uk_p, w_uv, tm):
    scale = QK_HEAD ** -0.5

    def body(i, n, cq_ref, ckv_ref, kr_ref, c_ref, s_ref, dq_ref, dk_ref, dv_ref, gqa, gkva, gqn, gkn, wq, wk, wv,
             dcq_ref, dckv_ref, dkr_ref, cqb_ref, dqr_ref, ckvb_ref, dkn_ref, dvb_ref, ggqa, ggkva, ggqn, ggkn):
        _zero_first(i, ggqa, ggkva, ggqn, ggkn)
        cosv, sinv = c_ref[...], s_ref[...]
        cqv = cq_ref[...]
        rs_q = _rstd(cqv)
        cqb_ref[...] = (cqv * rs_q * gqa[...]).astype(BF16)
        qr = jnp.dot(cqb_ref[...], wq[...], preferred_element_type=F32)
        ckvv = ckv_ref[...]
        rs_kv = _rstd(ckvv)
        ckvb_ref[...] = (ckvv * rs_kv * gkva[...]).astype(BF16)
        kn = jnp.dot(ckvb_ref[...], wk[...], preferred_element_type=F32)
        kr = kr_ref[...]
        dkr = jnp.zeros_like(kr)
        for h in range(HEADS):
            sl = slice(h * LANES, (h + 1) * LANES)
            qh = qr[:, sl]
            d_qn = _rope_t(dq_ref[:, sl] * scale, cosv, sinv)
            dqh, dgq = _norm_bwd(qh, _rstd(qh, QK_HEAD), gqn[...], d_qn, QK_HEAD)
            _acc_row(ggqn, 0, dgq)
            dqr_ref[:, sl] = dqh.astype(BF16)
            kh = kn[:, sl] + kr
            d_kn = _rope_t(dk_ref[:, sl] * (1.0 / LOG2E), cosv, sinv)
            dkh, dgk = _norm_bwd(kh, _rstd(kh, QK_HEAD), gkn[...], d_kn, QK_HEAD)
            _acc_row(ggkn, 0, dgk)
            dkn_ref[:, sl] = dkh.astype(BF16)
            dkr = dkr + dkh
        dkr_ref[...] = dkr
        dvb_ref[...] = dv_ref[...].astype(BF16)
        d_cq = lax.dot_general(dqr_ref[...], wq[...], (((1,), (1,)), ((), ())), preferred_element_type=F32)
        dcq, dg = _norm_bwd(cqv, rs_q, gqa[...], d_cq)
        _acc_row(ggqa, 0, dg)
        dcq_ref[...] = dcq
        d_ckv = (lax.dot_general(dkn_ref[...], wk[...], (((1,), (1,)), ((), ())), preferred_element_type=F32)
                 + lax.dot_general(dvb_ref[...], wv[...], (((1,), (1,)), ((), ())), preferred_element_type=F32))
        dckv, dg = _norm_bwd(ckvv, rs_kv, gkva[...], d_ckv)
        _acc_row(ggkva, 0, dg)
        dckv_ref[...] = dckv

    return _rowwise(body, "mla_qkv_bwd", cq.shape[0], tm, rows=[cq, ckv, krp, cos_t, sin_t, dq, dk, dv],
                    fulls=[g_qa, g_kva, g_qn, g_kn, w_uq_p, w_uk_p, w_uv],
                    outs=[(Q_LORA, F32), (KV_LORA, F32), (LANES, F32), (Q_LORA, BF16), (HEADS * LANES, BF16),
                          (KV_LORA, BF16), (HEADS * LANES, BF16), (MLA_W, BF16)],
                    accs=[((SUBLANES, Q_LORA), F32), ((SUBLANES, KV_LORA), F32), ((SUBLANES, LANES), F32),
                          ((SUBLANES, LANES), F32)])


def _in_proj_bwd(x, dx1, dxr_f, dxr_b, dyg, dcq, dckv, dkrp, g, w_in_p, tm):
    def body(i, n, x_ref, dx1_ref, df_ref, db_ref, dyg_ref, dcq_ref, dckv_ref, dkr_ref, g_ref, w_ref, gx_ref, dp_ref, gg_ref):
        _zero_first(i, gg_ref)
        dp_ref[:, :LRU_W] = (df_ref[...] + db_ref[...]).astype(BF16)
        dp_ref[:, LRU_W:2 * LRU_W] = dyg_ref[...].astype(BF16)
        dp_ref[:, 2 * LRU_W:2 * LRU_W + Q_LORA] = dcq_ref[...].astype(BF16)
        dp_ref[:, 2 * LRU_W + Q_LORA:OFF_KR] = dckv_ref[...].astype(BF16)
        dp_ref[:, OFF_KR:] = dkr_ref[...].astype(BF16)
        d_h = lax.dot_general(dp_ref[...], w_ref[...], (((1,), (1,)), ((), ())), preferred_element_type=F32)
        xv = x_ref[...]
        dx, dg = _norm_bwd(xv, _rstd(xv), g_ref[...], d_h)
        _acc_row(gg_ref, 0, dg)
        gx_ref[...] = dx + dx1_ref[...]

    return _rowwise(body, "in_proj_bwd", x.shape[0], tm, rows=[x, dx1, dxr_f, dxr_b, dyg, dcq, dckv, dkrp],
                    fulls=[g, w_in_p], outs=[(D_MODEL, F32), (IN_PAD, BF16)], accs=[((SUBLANES, D_MODEL), F32)])


ANY = pl.BlockSpec(memory_space=pl.ANY)


def _chip_peers(x, y):
    return ((1 - x, y), (x, 1 - y), (1 - x, 1 - y))


def _exchange_call(kern, name, ins, out_shapes, n_sems, aliases=None):
    return pl.pallas_call(
        kern, in_specs=[ANY] * len(ins), out_specs=[ANY] * len(out_shapes), out_shape=out_shapes,
        scratch_shapes=[pltpu.SemaphoreType.DMA((n,)) for n in n_sems], input_output_aliases=aliases or {},
        name=name)(*ins)


def _start_then_wait(copies):
    for cp in copies:
        cp.start()
    for cp in copies:
        cp.wait()


N_DEV = 8
RELATIONS = tuple((dx, dy, dc) for dx in (0, 1) for dy in (0, 1) for dc in (0, 1))[1:]


def _flip(v, d):
    return 1 - v if d else v


def _gather_copies(ins, outs, ssem, rsem, lsem):
    x, y, c = lax.axis_index("x"), lax.axis_index("y"), lax.axis_index("c")
    me = 2 * x + y
    cps = []
    for i, (a, o) in enumerate(zip(ins, outs)):
        cps.append(pltpu.make_async_copy(a, o.at[me], lsem.at[i]))
        for j, (px, py) in enumerate(_chip_peers(x, y)):
            cps.append(pltpu.make_async_remote_copy(a, o.at[me], ssem.at[3 * i + j], rsem.at[3 * i + j],
                                                    device_id=(px, py, c), device_id_type=MESH))
    return cps


def _gather_shapes(arrs):
    return [jax.ShapeDtypeStruct((N_CHIPS,) + a.shape, a.dtype) for a in arrs]


def _gather_sems(n):
    return [pltpu.SemaphoreType.DMA((3 * n,)), pltpu.SemaphoreType.DMA((3 * n,)), pltpu.SemaphoreType.DMA((n,))]


def _gather_chips(arrs):
    n = len(arrs)

    def kern(*refs):
        _start_then_wait(_gather_copies(refs[:n], refs[n:2 * n], *refs[2 * n:]))

    return _exchange_call(kern, "gather_weights", arrs, _gather_shapes(arrs), (3 * n, 3 * n, n))


def _to_owner_copies(ins, outs, ssem, rsem, lsem):
    x, y, c = lax.axis_index("x"), lax.axis_index("y"), lax.axis_index("c")
    me = 4 * x + 2 * y + c
    cps = []
    for i, (a, o) in enumerate(zip(ins, outs)):
        cps.append(pltpu.make_async_copy(a.at[2 * x + y, c], o.at[me], lsem.at[i]))
        for r, (dx, dy, dc) in enumerate(RELATIONS):
            tx, ty, tc = _flip(x, dx), _flip(y, dy), _flip(c, dc)
            cps.append(pltpu.make_async_remote_copy(a.at[2 * tx + ty, tc], o.at[me], ssem.at[7 * i + r], rsem.at[7 * i + r],
                                                    device_id=(tx, ty, tc), device_id_type=MESH))
    return cps


def _to_owner_shapes(arrs):
    return [jax.ShapeDtypeStruct((N_DEV,) + a.shape[2:], a.dtype) for a in arrs]


def _to_owner_sems(n):
    return [pltpu.SemaphoreType.DMA((7 * n,)), pltpu.SemaphoreType.DMA((7 * n,)), pltpu.SemaphoreType.DMA((n,))]


def _to_owner(arrs, name):
    n = len(arrs)

    def kern(*refs):
        _start_then_wait(_to_owner_copies(refs[:n], refs[n:2 * n], *refs[2 * n:]))

    return _exchange_call(kern, name, arrs, _to_owner_shapes(arrs), (7 * n, 7 * n, n))


def _join_halves(arrs):
    n = len(arrs)

    def kern(*refs):
        outs, (ssem, rsem) = refs[n:2 * n], refs[2 * n:]
        x, y, c = lax.axis_index("x"), lax.axis_index("y"), lax.axis_index("c")
        _start_then_wait([
            pltpu.make_async_remote_copy(outs[i].at[c], outs[i].at[c], ssem.at[i], rsem.at[i],
                                         device_id=(x, y, 1 - c), device_id_type=MESH) for i in range(n)])

    outs = [jax.ShapeDtypeStruct(a.shape, a.dtype) for a in arrs]
    return _exchange_call(kern, "grad_join_halves", arrs, outs, (n, n), aliases={i: i for i in range(n)})


def _row_block(rows, row_bytes, limit=1 << 20):
    best = None
    for d in range(16, rows + 1, 16):
        if rows % d == 0 and d * row_bytes <= limit:
            best = d
    return best if best is not None else rows


def _sum_devices(b, c, name):
    _, h, cols = b.shape
    hb = _row_block(h, cols * 4)

    def kern(c_ref, b_ref, o_ref):
        acc = b_ref[0].astype(F32)
        for j in range(1, N_DEV):
            acc = acc + b_ref[j].astype(F32)
        o_ref[...] = acc

    return pl.pallas_call(
        kern,
        grid_spec=pltpu.PrefetchScalarGridSpec(
            num_scalar_prefetch=1, grid=(h // hb,),
            in_specs=[pl.BlockSpec((N_DEV, hb, cols), lambda i, c_ref: (0, i, 0))],
            out_specs=pl.BlockSpec((None, hb, cols), lambda i, c_ref: (c_ref[0], i, 0))),
        out_shape=jax.ShapeDtypeStruct((2, h, cols), F32), name=name, compiler_params=_cparams(1))(c, b)


def _adamw(w, g, m, v, name):
    rows, cols = w.shape
    rb = _row_block(rows, cols * 4)
    c1 = 1.0 - ADAM_B1 ** ADAM_STEP
    c2 = 1.0 - ADAM_B2 ** ADAM_STEP

    def kern(w_ref, g_ref, m_ref, v_ref, d_ref, mo_ref, vo_ref):
        gv = g_ref[...]
        mn = ADAM_B1 * m_ref[...] + (1.0 - ADAM_B1) * gv
        vn = ADAM_B2 * v_ref[...] + (1.0 - ADAM_B2) * (gv * gv)
        mo_ref[...] = mn
        vo_ref[...] = vn
        d_ref[...] = (-ADAM_LR) * ((mn / c1) / (jnp.sqrt(vn / c2) + ADAM_EPS) + ADAM_WD * w_ref[...])

    spec = pl.BlockSpec((rb, cols), lambda i: (i, 0))
    return pl.pallas_call(
        kern, grid=(rows // rb,), in_specs=[spec] * 4, out_specs=[spec] * 3,
        out_shape=[jax.ShapeDtypeStruct(w.shape, F32)] * 3, name=name, compiler_params=_cparams(1))(w, g, m, v)


def _pad_rows(flat, rows):
    return jnp.pad(flat, (0, rows * LANES - flat.shape[0])).reshape(rows, LANES)


def _round_up(n, m):
    return (n + m - 1) // m * m


def _shard_shape(shape, axis):
    return tuple(s // N_CHIPS if a == axis else s for a, s in enumerate(shape))


def _to_shards(full, axis):
    shape = full.shape
    t = full.reshape(shape[:axis] + (N_CHIPS, shape[axis] // N_CHIPS) + shape[axis + 1:])
    return jnp.moveaxis(t, axis, 0).reshape(N_CHIPS, -1)


def _from_shards(sh, shape, axis):
    t = sh.reshape((N_CHIPS,) + _shard_shape(shape, axis))
    t = jnp.moveaxis(t, 0, axis)
    return t.reshape(shape)


BIG = tuple((name, shape, axis) for name, shape, axis, big in SHARDED if big)
EARLY_WEIGHTS = ("w_in", "w_uq", "w_ukv")
SMALL_SHARDED = tuple((name, shape, axis) for name, shape, axis, big in SHARDED if not big)


def _pack_small_weights(p):
    flat = jnp.concatenate([p[name].reshape(-1) for name, _, _ in SMALL_SHARDED])
    return _pad_rows(flat, _round_up(-(-flat.shape[0] // LANES), SUBLANES))


def _unpack_small_weights(gathered):
    flat = gathered.reshape(N_CHIPS, -1)
    out, off = {}, 0
    for name, shape, axis in SMALL_SHARDED:
        n = _numel(shape) // N_CHIPS
        out[name] = _from_shards(flat[:, off:off + n], shape, axis)
        off += n
    return out


def _pack_small_local(p, prefix=""):
    parts = [p[prefix + name].reshape(-1) for name, _, _ in SMALL_SHARDED]
    parts += [p[prefix + name].reshape(-1) for name, _ in REPLICATED]
    return jnp.concatenate(parts)


def _pack_small_grads(g):
    parts = [_to_shards(g[name], axis) for name, _, axis in SMALL_SHARDED]
    rep = jnp.concatenate([g[name].reshape(-1) for name, _ in REPLICATED])
    parts.append(jnp.broadcast_to(rep[None], (N_CHIPS, rep.shape[0])))
    return jnp.concatenate(parts, axis=1)


def _unpack_small_local(flat):
    out, off = {}, 0
    for name, shape, axis in SMALL_SHARDED:
        n = _numel(shape) // N_CHIPS
        out[name] = flat[off:off + n].reshape((1,) + _shard_shape(shape, axis))
        off += n
    for name, shape in REPLICATED:
        n = _numel(shape)
        out[name] = flat[off:off + n].reshape((1,) + shape)
        off += n
    return out


def _grad_shards(g, shape, axis):
    if axis == 0:
        return g.reshape((N_CHIPS,) + _shard_shape(shape, axis))
    return jnp.transpose(g.reshape(shape[0], N_CHIPS, shape[1] // N_CHIPS), (1, 0, 2))


def _cols_from_shards(w4):
    return jnp.transpose(w4, (1, 0, 2)).reshape(w4.shape[1], -1)


def _block_diag(w):
    eye = jnp.eye(LRU_BLOCKS, dtype=w.dtype)
    return jnp.einsum("ncd,nm->ncmd", w, eye).reshape(LRU_W, LRU_W)


def _block_diag_t(g):
    g4 = g.reshape(LRU_BLOCKS, 64, LRU_BLOCKS, 64)
    return jnp.stack([g4[n, :, n, :] for n in range(LRU_BLOCKS)])


def _pad8(a):
    return jnp.pad(a, ((0, SUBLANES - a.shape[0]), (0, 0)))


def kernel(x, mem, positions, attn_norm, w_in, lru_conv_w, lru_conv_b, lru_w_a, lru_b_a, lru_w_i, lru_b_i, lru_lambda, q_a_norm, w_uq, kv_a_norm, w_ukv, mla_q_norm, mla_k_norm, lru_out_norm, mla_out_norm, w_out, mem_attn_norm, mem_norm, w_mem_q, w_mem_kv, mem_q_norm, mem_k_norm, w_mem_o, ffn_norm, w_up, ffn_conv_w, ffn_conv_b, w_down, loss_target, m_attn_norm, m_w_in, m_lru_conv_w, m_lru_conv_b, m_lru_w_a, m_lru_b_a, m_lru_w_i, m_lru_b_i, m_lru_lambda, m_q_a_norm, m_w_uq, m_kv_a_norm, m_w_ukv, m_mla_q_norm, m_mla_k_norm, m_lru_out_norm, m_mla_out_norm, m_w_out, m_mem_attn_norm, m_mem_norm, m_w_mem_q, m_w_mem_kv, m_mem_q_norm, m_mem_k_norm, m_w_mem_o, m_ffn_norm, m_w_up, m_ffn_conv_w, m_ffn_conv_b, m_w_down, v_attn_norm, v_w_in, v_lru_conv_w, v_lru_conv_b, v_lru_w_a, v_lru_b_a, v_lru_w_i, v_lru_b_i, v_lru_lambda, v_q_a_norm, v_w_uq, v_kv_a_norm, v_w_ukv, v_mla_q_norm, v_mla_k_norm, v_lru_out_norm, v_mla_out_norm, v_w_out, v_mem_attn_norm, v_mem_norm, v_w_mem_q, v_w_mem_kv, v_mem_q_norm, v_mem_k_norm, v_w_mem_o, v_ffn_norm, v_w_up, v_ffn_conv_w, v_ffn_conv_b, v_w_down):
    given = dict(locals())
    local = {name: given[name][0] for name in WEIGHT_ORDER}
    s = x.shape[1]
    x2d, mem2d, tgt = x[0], mem[0], loss_target[0]
    tm = min(512, s)
    tm_wide = min(1024, s)
    tm_ffn = min(256, s)
    t_scan = min(512, s)
    tq_f, tq_b, tk = min(4096, s), min(1024, s), min(512, s)

    early = [b for b in BIG if b[0] in EARLY_WEIGHTS]
    late = [b for b in BIG if b[0] not in EARLY_WEIGHTS]
    got = _gather_chips([local[name].astype(BF16) for name, _, _ in early] + [_pack_small_weights(local)])
    full = _unpack_small_weights(got[-1])

    def take_gathered(entries, arrays):
        for (name, shape, axis), w4 in zip(entries, arrays):
            if axis == 0:
                full[name] = w4.reshape(shape)
            elif name in ("w_up", "w_mem_o"):
                full[name] = w4
            else:
                full[name] = _cols_from_shards(w4)

    take_gathered(early, got)
    row = lambda a: a.reshape(1, -1)
    b16 = lambda a: a.astype(BF16)
    zeros = lambda r, c: jnp.zeros((r, c), BF16)
    w_in_f = full["w_in"]
    w_in_p = jnp.concatenate([w_in_f[:, :OFF_KR], zeros(D_MODEL, QK_NOPE), w_in_f[:, OFF_KR:],
                              zeros(D_MODEL, LANES - QK_HEAD)], axis=1)
    w_uq_p = jnp.pad(full["w_uq"].reshape(Q_LORA, HEADS, QK_HEAD), ((0, 0), (0, 0), (0, LANES - QK_HEAD))).reshape(Q_LORA, -1)
    ukv = full["w_ukv"].reshape(KV_LORA, HEADS, QK_NOPE + V_DIM)
    w_uk_p = jnp.pad(ukv[:, :, :QK_NOPE], ((0, 0), (0, 0), (0, LANES - QK_NOPE))).reshape(KV_LORA, -1)
    w_uv = ukv[:, :, QK_NOPE:].reshape(KV_LORA, MLA_W)
    wa = [b16(_block_diag(local["lru_w_a"][d])) for d in range(2)]
    wi = [b16(_block_diag(local["lru_w_i"][d])) for d in range(2)]
    cw = [_pad8(full["lru_conv_w"][d]) for d in range(2)]
    pv = [_pad8(jnp.stack([full["lru_conv_b"][d], full["lru_b_a"][d], full["lru_b_i"][d], full["lru_lambda"][d]]))
          for d in range(2)]
    ffn_cw = _pad8(jnp.concatenate([full["ffn_conv_w"], row(local["ffn_conv_b"])], axis=0))
    g_attn, g_qa, g_kva = row(local["attn_norm"]), row(local["q_a_norm"]), row(local["kv_a_norm"])
    g_qn = jnp.pad(row(local["mla_q_norm"]), ((0, 0), (0, LANES - QK_HEAD)))
    g_kn = jnp.pad(row(local["mla_k_norm"]), ((0, 0), (0, LANES - QK_HEAD)))
    g_lru, g_mla = row(local["lru_out_norm"]), row(local["mla_out_norm"])
    g_memattn, g_mem = row(local["mem_attn_norm"]), row(local["mem_norm"])
    g_mq, g_mk, g_ffn = row(local["mem_q_norm"]), row(local["mem_k_norm"]), row(local["ffn_norm"])

    inv = ROPE_THETA ** (-jnp.arange(0, QK_ROPE, 2, dtype=F32) / QK_ROPE)
    ang = positions[0].astype(F32)[:, None] * inv
    cosv, sinv = jnp.cos(ang), jnp.sin(ang)
    ones, zer = jnp.ones((s, QK_NOPE), F32), jnp.zeros((s, LANES - QK_HEAD), F32)
    cos_t = jnp.concatenate([ones, cosv, cosv, zer + 1.0], axis=1)
    sin_t = jnp.concatenate([ones * 0.0, -sinv, sinv, zer], axis=1)

    xr, yg, cq, ckv, krp, hb_in = _in_proj(x2d, g_attn, w_in_p, tm)
    h_f, *saved_f = _lru_scan_fwd(xr, cw[0], pv[0], wa[0], wi[0], False, t_scan)
    h_b, *saved_b = _lru_scan_fwd(xr, cw[1], pv[1], wa[1], wi[1], True, t_scan)
    q, k, v = _mla_qkv(cq, ckv, krp, cos_t, sin_t, g_qa, g_kva, g_qn, g_kn, w_uq_p, w_uk_p, w_uv, tm_wide)
    o, lse, *got = _attn_fwd(q, k, v, tq_f, tk, shards=[local[name].astype(BF16) for name, _, _ in late])
    take_gathered(late, got)
    x1, mixed = _mix_out(h_f, h_b, yg, o, x2d, g_lru, g_mla, full["w_out"], tm)
    km, vm = _mem_kv(mem2d, g_mem, full["w_mem_kv"], g_mk)
    x2, o_mem = _mem_attn(x1, g_memattn, full["w_mem_q"], g_mq, km, vm, full["w_mem_o"], tm_wide)
    gu_pre, hb_ffn = _ffn_up(x2, g_ffn, full["w_up"], tm)
    dy, dyb, act, dgu, loss_acc = _ffn_down_loss(gu_pre, x2, tgt, ffn_cw, full["w_down"], tm_ffn)
    loss = lax.psum(loss_acc[0, 0] * (0.5 / D_MODEL), ("x", "y", "c"))

    grads = {}
    grads["w_down"] = _matmul_tn(act, dyb, "grad_w_down", out_dtype=BF16)
    dpre, g_conv = _ffn_bwd_conv(dgu, gu_pre, ffn_cw, tm_ffn)
    grads["ffn_conv_w"], grads["ffn_conv_b"] = g_conv[:3], g_conv[3]
    grads["w_up"] = _matmul_tn(hb_ffn, dpre, "grad_w_up", col_shards=True, out_dtype=BF16)
    dx2, dx2b, gg = _ffn_bwd_in(dpre, x2, dy, g_ffn, full["w_up"], tm)
    grads["ffn_norm"] = gg[0]
    grads["w_mem_o"] = _matmul_tn(o_mem, dx2b, "grad_w_mem_o", col_shards=True, out_dtype=BF16)
    dx1, dx1b, hm, dqr_mem, dkm, dvm, gg, ggq = _mem_attn_bwd(x1, dx2, dx2b, g_memattn, full["w_mem_q"], g_mq, km, vm,
                                                                 full["w_mem_o"], tm)
    grads["mem_attn_norm"], grads["mem_q_norm"] = gg[0], ggq[0]
    grads["w_mem_q"] = _matmul_tn(hm, dqr_mem, "grad_w_mem_q", out_dtype=BF16)
    g_mem_kv, gg, ggk, _ = _mem_kv_bwd(mem2d, g_mem, full["w_mem_kv"], g_mk, dkm, dvm)
    grads["w_mem_kv"] = g_mem_kv.astype(BF16)
    grads["mem_norm"], grads["mem_k_norm"] = gg[0], ggk[0]
    grads["w_out"] = _matmul_tn(mixed, dx1b, "grad_w_out", out_dtype=BF16)
    dh, dyg, dob, dl128, ggl, ggm = _mix_out_bwd(dx1b, h_f, h_b, yg, o, g_lru, g_mla, full["w_out"], tm)
    grads["lru_out_norm"], grads["mla_out_norm"] = ggl[0], ggm[0]
    delta_t = jnp.transpose(dl128[:, :HEADS]).reshape(HEADS // 2, 2, s)
    def halves(name, shape, axis):
        g4 = grads[name] if grads[name].ndim == 3 else _grad_shards(grads[name], shape, axis)
        return g4.reshape(N_CHIPS, 2, g4.shape[1] // 2, g4.shape[2])

    dq, dk, dv, *arrived_late = _attn_bwd(q, k, v, dob, lse, delta_t, tq_b, tk,
                                          contributions=[halves(*e) for e in late])
    (dcq, dckv, dkrp, cqb, dqr, ckvb, dkn, dvb, ggqa, ggkva, ggqn, ggkn) = _mla_qkv_bwd(
        cq, ckv, krp, cos_t, sin_t, dq, dk, dv, g_qa, g_kva, g_qn, g_kn, w_uq_p, w_uk_p, w_uv, tm_wide)
    grads["q_a_norm"], grads["kv_a_norm"] = ggqa[0], ggkva[0]
    grads["mla_q_norm"], grads["mla_k_norm"] = ggqn[0, :QK_HEAD], ggkn[0, :QK_HEAD]
    g_uq_p = _matmul_tn(cqb, dqr, "grad_w_uq")
    grads["w_uq"] = g_uq_p.reshape(Q_LORA, HEADS, LANES)[:, :, :QK_HEAD].reshape(Q_LORA, -1)
    g_uk_p = _matmul_tn(ckvb, dkn, "grad_w_uk").reshape(KV_LORA, HEADS, LANES)[:, :, :QK_NOPE]
    g_uv = _matmul_tn(ckvb, dvb, "grad_w_uv").reshape(KV_LORA, HEADS, V_DIM)
    grads["w_ukv"] = jnp.concatenate([g_uk_p, g_uv], axis=2).reshape(KV_LORA, -1)
    dxr, gwa, gwi, gvec = [], [], [], []
    for d, (hd, saved) in enumerate(((h_f, saved_f), (h_b, saved_b))):
        r = _lru_scan_bwd(xr, saved, hd, dh, cw[d], pv[d], wa[d], wi[d], d == 1, t_scan)
        dxr.append(r[0])
        gwa.append(_block_diag_t(r[1]))
        gwi.append(_block_diag_t(r[2]))
        gvec.append(r[3])
    grads["lru_w_a"], grads["lru_w_i"] = jnp.stack(gwa), jnp.stack(gwi)
    grads["lru_conv_w"] = jnp.stack([gv[:CONV_W] for gv in gvec])
    for r_i, name in ((4, "lru_conv_b"), (5, "lru_b_a"), (6, "lru_b_i"), (7, "lru_lambda")):
        grads[name] = jnp.stack([gv[r_i] for gv in gvec])
    grad_x, dproj, gg = _in_proj_bwd(x2d, dx1, dxr[0], dxr[1], dyg, dcq, dckv, dkrp, g_attn, w_in_p, tm)
    grads["attn_norm"] = gg[0]
    g_in_p = _matmul_tn(hb_in, dproj, "grad_w_in")
    grads["w_in"] = jnp.concatenate([g_in_p[:, :OFF_KR], g_in_p[:, OFF_KR + QK_NOPE:OFF_KR + QK_HEAD]], axis=1)

    small = _pack_small_grads(grads)
    length = small.shape[1]
    hrows = _round_up(-(-length // (2 * LANES)), 16)
    small = jnp.pad(small, ((0, 0), (0, 2 * hrows * LANES - length))).reshape(N_CHIPS, 2, hrows, LANES)
    for name, _, _ in early:
        grads[name] = grads[name].astype(BF16)
    arrived_early = _to_owner([halves(*e) for e in early] + [small], "grad_to_owner")
    names = [name for name, _, _ in late + early] + ["small"]
    c_idx = lax.axis_index("c").astype(jnp.int32).reshape(1)
    reduced = _join_halves([_sum_devices(b, c_idx, "grad_sum_" + n)
                            for n, b in zip(names, list(arrived_late) + list(arrived_early))])

    outs = [{}, {}, {}, {}]
    for (name, shape, axis), r in zip(late + early, reduced):
        g2 = r.reshape(_shard_shape(shape, axis))
        res = _adamw(local[name], g2, given["m_" + name][0], given["v_" + name][0], "adamw_" + name)
        for o_, a in zip(outs, (g2, *res)):
            o_[name] = a[None]
    pack = lambda prefix: _pad_rows(_pack_small_local({n: given[prefix + n] for n in WEIGHT_ORDER}), 2 * hrows)
    g_small = reduced[-1].reshape(2 * hrows, LANES)
    res = _adamw(pack(""), g_small, pack("m_"), pack("v_"), "adamw_small")
    for o_, a in zip(outs, (g_small, *res)):
        o_.update(_unpack_small_local(a.reshape(-1)))
    return (loss, grad_x[None], *[o_[n] for o_ in outs for n in WEIGHT_ORDER])
```

```python
import jax
import jax.numpy as jnp
from jax import lax
from jax.experimental import pallas as pl
from jax.experimental.pallas import tpu as pltpu

F32, BF16 = jnp.float32, jnp.bfloat16
MESH = pl.DeviceIdType.MESH

D_MODEL = 1024
EPS = 1e-6
LRU_W = 512
LRU_BLOCKS = 8
LRU_C = 8.0
CONV_W = 4
HEADS = 8
QK_NOPE, QK_ROPE, QK_HEAD, V_DIM = 64, 32, 96, 64
Q_LORA, KV_LORA = 256, 128
MLA_W = HEADS * V_DIM
ROPE_THETA = 10000.0
IN_COLS = 2 * LRU_W + Q_LORA + KV_LORA + QK_ROPE
OFF_KR = IN_COLS - QK_ROPE
IN_PAD = 1536
MEM_HEADS, MEM_HD = 4, 128
MEM_W = MEM_HEADS * MEM_HD
D_FF = 2816
N_CHIPS = 4
ADAM_LR, ADAM_B1, ADAM_B2, ADAM_EPS, ADAM_WD, ADAM_STEP = 0.001, 0.9, 0.999, 1e-08, 0.01, 10

LANES = 128
SUBLANES = 8
V7X_VMEM_BYTES = 64 * 1024 * 1024
VMEM_LIMIT = V7X_VMEM_BYTES * 7 // 8

SHARDED = (
    ("w_in", (D_MODEL, IN_COLS), 1, True),
    ("lru_conv_w", (2, CONV_W, LRU_W), 2, False),
    ("lru_conv_b", (2, LRU_W), 1, False),
    ("lru_b_a", (2, LRU_W), 1, False),
    ("lru_b_i", (2, LRU_W), 1, False),
    ("lru_lambda", (2, LRU_W), 1, False),
    ("w_uq", (Q_LORA, HEADS * QK_HEAD), 1, True),
    ("w_ukv", (KV_LORA, HEADS * (QK_NOPE + V_DIM)), 1, True),
    ("w_out", (2 * LRU_W, D_MODEL), 0, True),
    ("w_mem_q", (D_MODEL, MEM_W), 0, True),
    ("w_mem_kv", (D_MODEL, 2 * MEM_W), 0, True),
    ("w_mem_o", (MEM_W, D_MODEL), 1, True),
    ("w_up", (D_MODEL, 2 * D_FF), 1, True),
    ("ffn_conv_w", (3, 2 * D_FF), 1, False),
    ("w_down", (D_FF, D_MODEL), 0, True),
)
REPLICATED = (
    ("attn_norm", (D_MODEL,)), ("lru_w_a", (2, LRU_BLOCKS, 64, 64)), ("lru_w_i", (2, LRU_BLOCKS, 64, 64)),
    ("q_a_norm", (Q_LORA,)), ("kv_a_norm", (KV_LORA,)), ("mla_q_norm", (QK_HEAD,)), ("mla_k_norm", (QK_HEAD,)),
    ("lru_out_norm", (LRU_W,)), ("mla_out_norm", (MLA_W,)), ("mem_attn_norm", (D_MODEL,)), ("mem_norm", (D_MODEL,)),
    ("mem_q_norm", (MEM_HD,)), ("mem_k_norm", (MEM_HD,)), ("ffn_norm", (D_MODEL,)), ("ffn_conv_b", (2 * D_FF,)),
)
WEIGHT_ORDER = ('attn_norm', 'w_in', 'lru_conv_w', 'lru_conv_b', 'lru_w_a', 'lru_b_a', 'lru_w_i', 'lru_b_i', 'lru_lambda',
                'q_a_norm', 'w_uq', 'kv_a_norm', 'w_ukv', 'mla_q_norm', 'mla_k_norm', 'lru_out_norm', 'mla_out_norm', 'w_out',
                'mem_attn_norm', 'mem_norm', 'w_mem_q', 'w_mem_kv', 'mem_q_norm', 'mem_k_norm', 'w_mem_o', 'ffn_norm', 'w_up',
                'ffn_conv_w', 'ffn_conv_b', 'w_down')


def _numel(shape):
    n = 1
    for s in shape:
        n *= s
    return n


def _cparams(n_axes):
    return pltpu.CompilerParams(dimension_semantics=("arbitrary",) * n_axes, vmem_limit_bytes=VMEM_LIMIT)


def _bdot(a, b):
    return jnp.dot(a.astype(BF16), b.astype(BF16), preferred_element_type=F32)


def _bdot_nt(a, b):
    return lax.dot_general(a.astype(BF16), b.astype(BF16), (((1,), (1,)), ((), ())), preferred_element_type=F32)


def _bdot_tn(a, b):
    return lax.dot_general(a.astype(BF16), b.astype(BF16), (((0,), (0,)), ((), ())), preferred_element_type=F32)


def _rstd(x, n=None):
    n = x.shape[-1] if n is None else n
    return lax.rsqrt(jnp.sum(x * x, axis=-1, keepdims=True) * (1.0 / n) + EPS)


def _norm_bwd(x, rs, g, dy, n=None):
    n = x.shape[-1] if n is None else n
    xhat = x * rs
    dxh = dy * g
    dx = rs * (dxh - xhat * (jnp.sum(dxh * xhat, axis=-1, keepdims=True) * (1.0 / n)))
    return dx, dy * xhat


def _acc_row(ref, r, val):
    ref[r:r + 1, :] += jnp.sum(val, axis=0, keepdims=True)


def _zero_first(i, *refs):
    @pl.when(i == 0)
    def _():
        for r in refs:
            r[...] = jnp.zeros_like(r)


def _shift_down(x, j, halo):
    if j == 0:
        return x
    xs = pltpu.roll(x, j, 0)
    hs = pltpu.roll(halo, j, 0)
    row = lax.broadcasted_iota(jnp.int32, hs.shape, 0)
    top = jnp.where(row < j, hs, xs[:SUBLANES])
    return jnp.concatenate([top, xs[SUBLANES:]], axis=0)


def _shift_up(x, j, halo):
    if j == 0:
        return x
    t = x.shape[0]
    xs = pltpu.roll(x, t - j, 0)
    hs = pltpu.roll(halo, SUBLANES - j, 0)
    row = lax.broadcasted_iota(jnp.int32, hs.shape, 0)
    bot = jnp.where(row >= SUBLANES - j, hs, xs[t - SUBLANES:])
    return jnp.concatenate([xs[:t - SUBLANES], bot], axis=0)


def _shift(x, j, halo, down):
    return _shift_down(x, j, halo) if down else _shift_up(x, j, halo)


def _scan(a, b, h_in, down):
    t, c = a.shape
    g = t // SUBLANES
    a3, b3 = a.reshape(g, SUBLANES, c), b.reshape(g, SUBLANES, c)
    sub = lax.broadcasted_iota(jnp.int32, a3.shape, 1)
    d = 1
    while d < SUBLANES:
        keep = (sub >= d) if down else (sub < SUBLANES - d)
        shift = d if down else SUBLANES - d
        a_s = jnp.where(keep, pltpu.roll(a3, shift, 1), 1.0)
        b_s = jnp.where(keep, pltpu.roll(b3, shift, 1), 0.0)
        b3 = a3 * b_s + b3
        a3 = a3 * a_s
        d *= 2
    hs = [None] * g
    carry = h_in
    for i in (range(g) if down else range(g - 1, -1, -1)):
        hs[i] = a3[i] * carry + b3[i]
        carry = hs[i][SUBLANES - 1:, :] if down else hs[i][:1, :]
    return jnp.concatenate(hs, axis=0)


def _sigmoid(x):
    return 0.5 * jnp.tanh(0.5 * x) + 0.5


LOG2E = 1.4426950408889634
GELU_K = 0.7978845608028654
GELU_C = 0.044715


def _gelu(x):
    return 0.5 * x * (1.0 + jnp.tanh(GELU_K * (x + GELU_C * x * x * x)))


def _gelu_grad(x):
    t = jnp.tanh(GELU_K * (x + GELU_C * x * x * x))
    return 0.5 * (1.0 + t) + 0.5 * x * (1.0 - t * t) * GELU_K * (1.0 + 3.0 * GELU_C * x * x)


def _rope_partner(x):
    lane = lax.broadcasted_iota(jnp.int32, x.shape, 1)
    half = QK_ROPE // 2
    sw = jnp.where(lane < QK_NOPE + half, pltpu.roll(x, LANES - half, 1), pltpu.roll(x, half, 1))
    return jnp.where((lane >= QK_NOPE) & (lane < QK_HEAD), sw, 0.0)


def _rope(x, cos_t, sin_t):
    return x * cos_t + _rope_partner(x) * sin_t


def _rope_t(dy, cos_t, sin_t):
    return dy * cos_t + _rope_partner(dy * sin_t)


def _rowwise(body, name, s, tm, rows=(), halos=(), fulls=(), outs=(), accs=()):
    n = s // tm
    hb = tm // SUBLANES
    last8 = s // SUBLANES - 1
    in_specs, args = [], []
    for a in rows:
        in_specs.append(pl.BlockSpec((tm, a.shape[1]), lambda i: (i, 0)))
        args.append(a)
    for a in halos:
        in_specs.append(pl.BlockSpec((SUBLANES, a.shape[1]), lambda i: (jnp.maximum(i * hb - 1, 0), 0)))
        in_specs.append(pl.BlockSpec((SUBLANES, a.shape[1]), lambda i: (jnp.minimum((i + 1) * hb, last8), 0)))
        args += [a, a]
    for a in fulls:
        in_specs.append(pl.BlockSpec(a.shape, lambda i, nd=a.ndim: (0,) * nd))
        args.append(a)
    out_shape, out_specs = [], []
    for c, dt in outs:
        out_shape.append(jax.ShapeDtypeStruct((s, c), dt))
        out_specs.append(pl.BlockSpec((tm, c), lambda i: (i, 0)))
    for shp, dt in accs:
        out_shape.append(jax.ShapeDtypeStruct(shp, dt))
        out_specs.append(pl.BlockSpec(shp, lambda i, nd=len(shp): (0,) * nd))

    def kern(*refs):
        body(pl.program_id(0), n, *refs)

    return pl.pallas_call(kern, grid=(n,), in_specs=in_specs, out_specs=out_specs, out_shape=out_shape, name=name,
                          compiler_params=_cparams(1))(*args)


def _matmul_tn(a, b, name, col_shards=False, out_dtype=F32):
    t, m = a.shape
    n = b.shape[1]
    bm = m
    for cand in range(LANES, m + 1, LANES):
        if m % cand == 0 and cand * (n // N_CHIPS if col_shards else min(n, 2048)) * 4 <= 6 * 1024 * 1024:
            bm = cand
    bn = n // N_CHIPS if col_shards else (n if n <= 2048 else 1408)
    bt = min(512, t)
    nt = t // bt

    def kern(a_ref, b_ref, o_ref, acc_ref):
        k = pl.program_id(2)

        @pl.when(k == 0)
        def _():
            acc_ref[...] = jnp.zeros_like(acc_ref)
        acc_ref[...] += _bdot_tn(a_ref[...], b_ref[...])

        @pl.when(k == nt - 1)
        def _():
            o_ref[...] = acc_ref[...].astype(out_dtype)

    if col_shards:
        out_spec = pl.BlockSpec((None, bm, bn), lambda i, j, k: (j, i, 0))
        out_shape = jax.ShapeDtypeStruct((N_CHIPS, m, bn), out_dtype)
    else:
        out_spec = pl.BlockSpec((bm, bn), lambda i, j, k: (i, j))
        out_shape = jax.ShapeDtypeStruct((m, n), out_dtype)
    return pl.pallas_call(
        kern, grid=(m // bm, n // bn, nt),
        in_specs=[pl.BlockSpec((bt, bm), lambda i, j, k: (k, i)), pl.BlockSpec((bt, bn), lambda i, j, k: (k, j))],
        out_specs=out_spec, out_shape=out_shape, scratch_shapes=[pltpu.VMEM((bm, bn), F32)], name=name,
        compiler_params=_cparams(3))(a, b)


def _in_proj(x, g, w_in_p, tm):
    def body(i, n, x_ref, g_ref, w_ref, xr, yg, cq, ckv, krp, hb):
        xv = x_ref[...]
        h = (xv * _rstd(xv) * g_ref[...]).astype(BF16)
        hb[...] = h
        p = jnp.dot(h, w_ref[...], preferred_element_type=F32)
        xr[...] = p[:, :LRU_W]
        yg[...] = p[:, LRU_W:2 * LRU_W]
        cq[...] = p[:, 2 * LRU_W:2 * LRU_W + Q_LORA]
        ckv[...] = p[:, 2 * LRU_W + Q_LORA:OFF_KR]
        krp[...] = p[:, OFF_KR:IN_PAD]

    return _rowwise(body, "in_proj", x.shape[0], tm, rows=[x], fulls=[g, w_in_p],
                    outs=[(LRU_W, F32), (LRU_W, F32), (Q_LORA, F32), (KV_LORA, F32), (LANES, F32), (D_MODEL, BF16)])


def _softplus_neg(lam):
    e = jnp.exp(-jnp.abs(lam))
    return jnp.maximum(-lam, 0.0) + jnp.where(e < 1e-2, e * (1.0 - e * (0.5 - e * (1.0 / 3.0))), jnp.log(1.0 + e))


def _lru_gates(x, halo, cw_ref, pv_ref, wa_ref, wi_ref, rev):
    down = not rev
    xc = pv_ref[0:1, :] + jnp.zeros_like(x)
    for j in range(CONV_W):
        k = j if rev else CONV_W - 1 - j
        xc = xc + cw_ref[k:k + 1, :] * _shift(x, j, halo, down)
    r = _sigmoid(_bdot(xc, wa_ref[...]) + pv_ref[1:2, :])
    ig = _sigmoid(_bdot(xc, wi_ref[...]) + pv_ref[2:3, :])
    lam = pv_ref[3:4, :]
    sp = _softplus_neg(lam)
    log_a = (-LRU_C) * r * sp
    a = jnp.exp(log_a)
    z = 2.0 * log_a
    series = -(z * (1.0 + z * (0.5 + z * (1.0 / 6.0 + z * (1.0 / 24.0)))))
    om = jnp.where(z > -0.02, series, 1.0 - a * a)
    mult = jnp.sqrt(om)
    return xc, r, ig, sp, a, mult


def _lru_scan_fwd(xr, cw, pv, wa, wi, rev, t):
    s = xr.shape[0]
    n = s // t
    hb = t // SUBLANES
    last8 = s // SUBLANES - 1
    down = not rev

    def kern(x_ref, halo_ref, cw_ref, pv_ref, wa_ref, wi_ref, h_ref, xc_ref, r_ref, ig_ref, a_ref, mult_ref, carry_ref):
        i = pl.program_id(0)
        _zero_first(i, carry_ref)
        halo = jnp.where(i == 0, 0.0, halo_ref[...])
        xc, r, ig, sp, a, mult = _lru_gates(x_ref[...], halo, cw_ref, pv_ref, wa_ref, wi_ref, rev)
        xc_ref[...], r_ref[...], ig_ref[...], a_ref[...], mult_ref[...] = xc, r, ig, a, mult
        h_ref[...] = _scan(a, mult * ig * xc, carry_ref[...], down)
        carry_ref[...] = h_ref[pl.ds(t - 1 if down else 0, 1), :]

    if rev:
        blk = lambda i: (n - 1 - i, 0)
        hal = lambda i: (jnp.minimum((n - i) * hb, last8), 0)
    else:
        blk = lambda i: (i, 0)
        hal = lambda i: (jnp.maximum(i * hb - 1, 0), 0)
    full = lambda a: pl.BlockSpec(a.shape, lambda i: (0, 0))
    return pl.pallas_call(
        kern, grid=(n,),
        in_specs=[pl.BlockSpec((t, LRU_W), blk), pl.BlockSpec((SUBLANES, LRU_W), hal), full(cw), full(pv), full(wa), full(wi)],
        out_specs=[pl.BlockSpec((t, LRU_W), blk)] * 6, out_shape=[jax.ShapeDtypeStruct((s, LRU_W), F32)] * 6,
        scratch_shapes=[pltpu.VMEM((1, LRU_W), F32)], name="lru_scan_rev" if rev else "lru_scan_fwd",
        compiler_params=_cparams(1))(xr, xr, cw, pv, wa, wi)


def _lru_scan_bwd(xr, saved, h, dh, cw, pv, wa, wi, rev, t):
    s = xr.shape[0]
    n = s // t
    hb = t // SUBLANES
    last8 = s // SUBLANES - 1
    down = not rev

    def kern(x_ref, xc_ref, r_ref, ig_ref, a_ref, mult_ref, h_ref, hh_ref, dh_ref, cw_ref, pv_ref, wa_ref, wi_ref,
             dx_ref, gwa_ref, gwi_ref, gv_ref, p_ref, dxc_halo_ref, tmp_ref):
        i = pl.program_id(0)
        _zero_first(i, gwa_ref, gwi_ref, gv_ref, p_ref, dxc_halo_ref)
        at_start = i == n - 1
        x = x_ref[...]
        hhalo = jnp.where(at_start, 0.0, hh_ref[...])
        xc, r, ig, a, mult = xc_ref[...], r_ref[...], ig_ref[...], a_ref[...], mult_ref[...]
        lam = pv_ref[3:4, :]
        sp = _softplus_neg(lam)
        h_prev = _shift(h_ref[...], 1, hhalo, down)
        row = lax.broadcasted_iota(jnp.int32, x.shape, 0)
        edge = t - 1 if down else 0
        dh_mod = dh_ref[...] + jnp.where(row == edge, p_ref[...], 0.0)
        a_next = _shift(a, 1, jnp.zeros((SUBLANES, LRU_W), F32), not down)
        g = _scan(a_next, dh_mod, jnp.zeros((1, LRU_W), F32), not down)
        tmp_ref[...] = a * g
        p_ref[...] = tmp_ref[pl.ds(0 if down else t - 1, 1), :]
        da = g * h_prev
        d_ig = g * mult * xc
        d_xc = g * mult * ig
        d_om = g * ig * xc * (0.5 / jnp.maximum(mult, 1e-30))
        d_log_a = da * a - 2.0 * d_om * a * a
        d_r = d_log_a * ((-LRU_C) * sp)
        d_sp = jnp.sum(d_log_a * ((-LRU_C) * r), axis=0, keepdims=True)
        gv_ref[7:8, :] += d_sp * (-_sigmoid(-lam))
        d_ga = d_r * r * (1.0 - r)
        d_gi = d_ig * ig * (1.0 - ig)
        _acc_row(gv_ref, 5, d_ga)
        _acc_row(gv_ref, 6, d_gi)
        d_xc = d_xc + _bdot_nt(d_ga, wa_ref[...]) + _bdot_nt(d_gi, wi_ref[...])
        gwa_ref[...] += _bdot_tn(xc, d_ga)
        gwi_ref[...] += _bdot_tn(xc, d_gi)
        _acc_row(gv_ref, 4, d_xc)
        dx = jnp.zeros_like(x)
        dxc_halo = dxc_halo_ref[...]
        for j in range(CONV_W):
            k = j if rev else CONV_W - 1 - j
            d_shift = _shift(d_xc, j, dxc_halo, not down)
            _acc_row(gv_ref, k, d_shift * x)
            dx = dx + cw_ref[k:k + 1, :] * d_shift
        dx_ref[...] = dx
        dxc_halo_ref[...] = d_xc[:SUBLANES] if down else d_xc[t - SUBLANES:]

    if rev:
        blk = lambda i: (i, 0)
        hal = lambda i: (jnp.minimum((i + 1) * hb, last8), 0)
    else:
        blk = lambda i: (n - 1 - i, 0)
        hal = lambda i: (jnp.maximum((n - 1 - i) * hb - 1, 0), 0)
    full = lambda a: pl.BlockSpec(a.shape, lambda i: (0, 0))
    bs = pl.BlockSpec((t, LRU_W), blk)
    hs = pl.BlockSpec((SUBLANES, LRU_W), hal)
    return pl.pallas_call(
        kern, grid=(n,),
        in_specs=[bs] * 7 + [hs, bs, full(cw), full(pv), full(wa), full(wi)],
        out_specs=[bs, pl.BlockSpec((LRU_W, LRU_W), lambda i: (0, 0)), pl.BlockSpec((LRU_W, LRU_W), lambda i: (0, 0)),
                   pl.BlockSpec((SUBLANES, LRU_W), lambda i: (0, 0))],
        out_shape=[jax.ShapeDtypeStruct((s, LRU_W), F32), jax.ShapeDtypeStruct((LRU_W, LRU_W), F32),
                   jax.ShapeDtypeStruct((LRU_W, LRU_W), F32), jax.ShapeDtypeStruct((SUBLANES, LRU_W), F32)],
        scratch_shapes=[pltpu.VMEM((1, LRU_W), F32), pltpu.VMEM((SUBLANES, LRU_W), F32), pltpu.VMEM((t, LRU_W), F32)],
        name="lru_bwd_rev" if rev else "lru_bwd_fwd", compiler_params=_cparams(1))(xr, *saved, h, h, dh, cw, pv, wa, wi)


def _mla_qkv(cq, ckv, krp, cos_t, sin_t, g_qa, g_kva, g_qn, g_kn, w_uq_p, w_uk_p, w_uv, tm):
    scale = QK_HEAD ** -0.5 * LOG2E

    def body(i, n, cq_ref, ckv_ref, kr_ref, c_ref, s_ref, gqa, gkva, gqn, gkn, wq, wk, wv, q_out, k_out, v_out):
        cosv, sinv = c_ref[...], s_ref[...]
        cqv = cq_ref[...]
        qr = _bdot(cqv * _rstd(cqv) * gqa[...], wq[...])
        ckvv = ckv_ref[...]
        c_kv = (ckvv * _rstd(ckvv) * gkva[...]).astype(BF16)
        kn = jnp.dot(c_kv, wk[...], preferred_element_type=F32)
        v_out[...] = jnp.dot(c_kv, wv[...], preferred_element_type=F32).astype(BF16)
        kr = kr_ref[...]
        kr_swapped = _rope_partner(kr * gkn[...]) * sinv
        for h in range(HEADS):
            sl = slice(h * LANES, (h + 1) * LANES)
            qh = qr[:, sl]
            qh = _rope(qh * _rstd(qh, QK_HEAD) * gqn[...], cosv, sinv) * scale
            q_out[:, sl] = qh.astype(BF16)
            kh = kn[:, sl] + kr
            rs = _rstd(kh, QK_HEAD)
            k_out[:, sl] = (kh * rs * gkn[...] * cosv + kr_swapped * rs).astype(BF16)

    return _rowwise(body, "mla_qkv", cq.shape[0], tm, rows=[cq, ckv, krp, cos_t, sin_t],
                    fulls=[g_qa, g_kva, g_qn, g_kn, w_uq_p, w_uk_p, w_uv],
                    outs=[(HEADS * LANES, BF16), (HEADS * LANES, BF16), (MLA_W, BF16)])


NT_DIMS = (((1,), (1,)), ((), ()))
TN_DIMS = (((0,), (0,)), ((), ()))


def _riding_exchange(copies_fn, first, last):
    @pl.when(first)
    def _():
        for cp in copies_fn():
            cp.start()

    def finish():
        @pl.when(last)
        def _():
            for cp in copies_fn():
                cp.wait()
    return finish


def _attn_fwd(q, k, v, tq, tk, shards=()):
    s = q.shape[0]
    nq, nk = s // tq, s // tk
    n = len(shards)

    def kern(*refs):
        q_ref, k_ref, v_ref = refs[:3]
        o_ref, lse_ref = refs[3 + n:5 + n]
        acc_ref = refs[5 + 2 * n]
        p_id, i_id = pl.program_id(0), pl.program_id(1)
        finish = _riding_exchange(lambda: _gather_copies(refs[3:3 + n], refs[5 + n:5 + 2 * n], *refs[6 + 2 * n:]),
                                  (p_id == 0) & (i_id == 0), (p_id == HEADS // 2 - 1) & (i_id == nq - 1)) if n else None
        qs = (q_ref[:, :LANES], q_ref[:, LANES:])
        acc_ref[...] = jnp.zeros_like(acc_ref)

        def step(j, carry):
            off = pl.multiple_of(j * tk, tk)
            vc = v_ref[pl.ds(off, tk), :]
            out = []
            for h in range(2):
                m, l = carry[2 * h:2 * h + 2]
                st = lax.dot_general(k_ref[pl.ds(off, tk), h * LANES:(h + 1) * LANES], qs[h], NT_DIMS,
                                     preferred_element_type=F32)
                mn = jnp.maximum(m, jnp.max(st, axis=0, keepdims=True))
                al = jnp.exp2(m - mn)
                pt = jnp.exp2(st - mn)
                l = al * l + jnp.sum(pt, axis=0, keepdims=True)
                acc_ref[h] = al * acc_ref[h] + lax.dot_general(vc, pt.astype(BF16), TN_DIMS, preferred_element_type=F32)
                out += [mn, l]
            return tuple(out)

        init = (jnp.full((1, tq), -1e30, F32), jnp.zeros((1, tq), F32)) * 2
        m0, l0, m1, l1 = lax.fori_loop(0, nk, step, init)
        row = lax.broadcasted_iota(jnp.int32, (LANES, tq), 0)
        o_ref[...] = jnp.where(row < V_DIM, acc_ref[0] / l0, acc_ref[1] / l1).T
        lse_ref[0, 0:1, :] = m0 + jnp.log2(l0)
        lse_ref[0, 1:2, :] = m1 + jnp.log2(l1)
        if n:
            finish()

    return pl.pallas_call(
        kern, grid=(HEADS // 2, nq),
        in_specs=[pl.BlockSpec((tq, 2 * LANES), lambda p, i: (i, p)), pl.BlockSpec((s, 2 * LANES), lambda p, i: (0, p)),
                  pl.BlockSpec((s, LANES), lambda p, i: (0, p))] + [ANY] * n,
        out_specs=[pl.BlockSpec((tq, LANES), lambda p, i: (i, p)), pl.BlockSpec((1, 2, tq), lambda p, i: (p, 0, i))]
        + [ANY] * n,
        out_shape=[jax.ShapeDtypeStruct((s, MLA_W), F32), jax.ShapeDtypeStruct((HEADS // 2, 2, s), F32)]
        + _gather_shapes(shards),
        scratch_shapes=[pltpu.VMEM((2, LANES, tq), F32)] + (_gather_sems(n) if n else []),
        name="attn_fwd", compiler_params=_cparams(2))(q, k, v, *shards)


def _attn_bwd(q, k, v, do, lse, delta, tq, tk, contributions=()):
    s = q.shape[0]
    nq, nk = s // tq, s // tk
    n = len(contributions)

    def kern(*refs):
        q_ref, do_ref, lse_ref, dl_ref, k_ref, v_ref = refs[:6]
        dq_ref, dk_ref, dv_ref = refs[6 + n:9 + n]
        acc_ref = refs[9 + 2 * n]
        p_id, i_id = pl.program_id(0), pl.program_id(1)
        finish = _riding_exchange(lambda: _to_owner_copies(refs[6:6 + n], refs[9 + n:9 + 2 * n], *refs[10 + 2 * n:]),
                                  (p_id == 0) & (i_id == 0), (p_id == HEADS // 2 - 1) & (i_id == nq - 1)) if n else None
        _zero_first(pl.program_id(1), dk_ref, dv_ref)
        acc_ref[...] = jnp.zeros_like(acc_ref)
        qs = (q_ref[:, :LANES], q_ref[:, LANES:])
        doc = do_ref[...]
        lane_q = lax.broadcasted_iota(jnp.int32, (tq, LANES), 1)
        zq = jnp.zeros_like(doc)
        dos = (jnp.where(lane_q < V_DIM, doc, zq), jnp.where(lane_q >= V_DIM, doc, zq))
        lses = (lse_ref[0, 0:1, :], lse_ref[0, 1:2, :])
        dls = (dl_ref[0, 0:1, :], dl_ref[0, 1:2, :])

        def step(j, carry):
            off = pl.multiple_of(j * tk, tk)
            vp = v_ref[pl.ds(off, tk), :]
            lane_k = lax.broadcasted_iota(jnp.int32, (tk, LANES), 1)
            zero = jnp.zeros_like(vp)
            vs = (jnp.where(lane_k < V_DIM, vp, zero), jnp.where(lane_k >= V_DIM, vp, zero))
            for h in range(2):
                sl = slice(h * LANES, (h + 1) * LANES)
                st = lax.dot_general(k_ref[pl.ds(off, tk), sl], qs[h], NT_DIMS, preferred_element_type=F32)
                pt = jnp.exp2(st - lses[h])
                dpt = lax.dot_general(vs[h], doc, NT_DIMS, preferred_element_type=F32)
                dst = (pt * (dpt - dls[h])).astype(BF16)
                dv_ref[pl.ds(off, tk), :] += jnp.dot(pt.astype(BF16), dos[h], preferred_element_type=F32)
                dk_ref[pl.ds(off, tk), sl] += jnp.dot(dst, qs[h], preferred_element_type=F32)
                acc_ref[h] += lax.dot_general(k_ref[pl.ds(off, tk), sl], dst, TN_DIMS, preferred_element_type=F32)
            return carry

        lax.fori_loop(0, nk, step, 0)
        dq_ref[:, :LANES] = acc_ref[0].T
        dq_ref[:, LANES:] = acc_ref[1].T
        if n:
            finish()

    return pl.pallas_call(
        kern, grid=(HEADS // 2, nq),
        in_specs=[pl.BlockSpec((tq, 2 * LANES), lambda p, i: (i, p)), pl.BlockSpec((tq, LANES), lambda p, i: (i, p)),
                  pl.BlockSpec((1, 2, tq), lambda p, i: (p, 0, i)), pl.BlockSpec((1, 2, tq), lambda p, i: (p, 0, i)),
                  pl.BlockSpec((s, 2 * LANES), lambda p, i: (0, p)), pl.BlockSpec((s, LANES), lambda p, i: (0, p))]
        + [ANY] * n,
        out_specs=[pl.BlockSpec((tq, 2 * LANES), lambda p, i: (i, p)), pl.BlockSpec((s, 2 * LANES), lambda p, i: (0, p)),
                   pl.BlockSpec((s, LANES), lambda p, i: (0, p))] + [ANY] * n,
        out_shape=[jax.ShapeDtypeStruct((s, HEADS * LANES), F32), jax.ShapeDtypeStruct((s, HEADS * LANES), F32),
                   jax.ShapeDtypeStruct((s, MLA_W), F32)] + _to_owner_shapes(contributions),
        scratch_shapes=[pltpu.VMEM((2, LANES, tq), F32)] + (_to_owner_sems(n) if n else []),
        name="attn_bwd", compiler_params=_cparams(2))(q, do, lse, delta, k, v, *contributions)


def _mix_out(hf, hb, yg, o, x, g_lru, g_mla, w_out, tm):
    def body(i, n, hf_ref, hb_ref, yg_ref, o_ref, x_ref, gl, gm, w_ref, x1_ref, mix_ref):
        lo = (hf_ref[...] + hb_ref[...]) * _gelu(yg_ref[...])
        ov = o_ref[...]
        mix_ref[:, :LRU_W] = (lo * _rstd(lo) * gl[...]).astype(BF16)
        mix_ref[:, LRU_W:] = (ov * _rstd(ov) * gm[...]).astype(BF16)
        x1_ref[...] = x_ref[...] + jnp.dot(mix_ref[...], w_ref[...], preferred_element_type=F32)

    return _rowwise(body, "mix_out", x.shape[0], tm, rows=[hf, hb, yg, o, x], fulls=[g_lru, g_mla, w_out],
                    outs=[(D_MODEL, F32), (2 * LRU_W, BF16)])


def _mem_kv(mem, g_mem, w_kv, g_k):
    m = mem.shape[0]

    def body(i, n, mem_ref, g_ref, w_ref, gk_ref, km_ref, vm_ref):
        mv = mem_ref[...]
        kv = _bdot(mv * _rstd(mv) * g_ref[...], w_ref[...])
        vm_ref[...] = kv[:, MEM_W:].astype(BF16)
        for h in range(MEM_HEADS):
            sl = slice(h * MEM_HD, (h + 1) * MEM_HD)
            kh = kv[:, sl]
            km_ref[:, sl] = (kh * _rstd(kh) * gk_ref[...]).astype(BF16)

    return _rowwise(body, "mem_kv", m, m, rows=[mem], fulls=[g_mem, w_kv, g_k], outs=[(MEM_W, BF16), (MEM_W, BF16)])


def _mem_attn_core(x1v, g_ref, wq_ref, gq_ref, km_ref, vm_ref):
    scale = MEM_HD ** -0.5
    hm = (x1v * _rstd(x1v) * g_ref[...]).astype(BF16)
    qr = jnp.dot(hm, wq_ref[...], preferred_element_type=F32)
    heads = []
    for h in range(MEM_HEADS):
        sl = slice(h * MEM_HD, (h + 1) * MEM_HD)
        qh = qr[:, sl]
        rs = _rstd(qh)
        qn = (qh * rs * gq_ref[...]).astype(BF16)
        sc = lax.dot_general(qn, km_ref[:, sl], (((1,), (1,)), ((), ())), preferred_element_type=F32) * scale
        e = jnp.exp(sc - jnp.max(sc, axis=-1, keepdims=True))
        p = e / jnp.sum(e, axis=-1, keepdims=True)
        oh = jnp.dot(p.astype(BF16), vm_ref[:, sl], preferred_element_type=F32)
        heads.append((qh, rs, qn, p, oh))
    return hm, heads


def _mem_attn(x1, g, w_q, g_q, km, vm, w_o, tm):
    cs = D_MODEL // N_CHIPS

    def body(i, n, x1_ref, g_ref, wq_ref, gq_ref, km_ref, vm_ref, wo_ref, x2_ref, ob_ref):
        x1v = x1_ref[...]
        _, heads = _mem_attn_core(x1v, g_ref, wq_ref, gq_ref, km_ref, vm_ref)
        for h in range(MEM_HEADS):
            ob_ref[:, h * MEM_HD:(h + 1) * MEM_HD] = heads[h][4].astype(BF16)
        for k in range(N_CHIPS):
            sl = slice(k * cs, (k + 1) * cs)
            x2_ref[:, sl] = x1v[:, sl] + jnp.dot(ob_ref[...], wo_ref[k], preferred_element_type=F32)

    return _rowwise(body, "mem_attn", x1.shape[0], tm, rows=[x1], fulls=[g, w_q, g_q, km, vm, w_o],
                    outs=[(D_MODEL, F32), (MEM_W, BF16)])


def _ffn_up(x2, g, w_up, tm):
    cs = 2 * D_FF // N_CHIPS

    def body(i, n, x_ref, g_ref, w_ref, gu_ref, hb_ref):
        xv = x_ref[...]
        hb_ref[...] = (xv * _rstd(xv) * g_ref[...]).astype(BF16)
        for k in range(N_CHIPS):
            gu_ref[:, k * cs:(k + 1) * cs] = jnp.dot(hb_ref[...], w_ref[k], preferred_element_type=F32)

    return _rowwise(body, "ffn_up", x2.shape[0], tm, rows=[x2], fulls=[g, w_up], outs=[(2 * D_FF, F32), (D_MODEL, BF16)])


def _ffn_conv(gu, prev, nxt, cw_ref, i, n):
    prev = jnp.where(i == 0, 0.0, prev)
    nxt = jnp.where(i == n - 1, 0.0, nxt)
    return (cw_ref[3:4, :] + cw_ref[0:1, :] * _shift_down(gu, 1, prev) + cw_ref[1:2, :] * gu
            + cw_ref[2:3, :] * _shift_up(gu, 1, nxt))


def _ffn_down_loss(gu_pre, x2, target, cw, w_down, tm):
    def body(i, n, gu_ref, x_ref, t_ref, pv_ref, nx_ref, cw_ref, w_ref, dy_ref, dyb_ref, act_ref, dgu_ref, loss_ref):
        _zero_first(i, loss_ref)
        gu = _ffn_conv(gu_ref[...], pv_ref[...], nx_ref[...], cw_ref, i, n)
        g, u = gu[:, :D_FF], gu[:, D_FF:]
        sg = _sigmoid(g)
        a = g * sg
        act_ref[...] = (a * u).astype(BF16)
        y = x_ref[...] + jnp.dot(act_ref[...], w_ref[...], preferred_element_type=F32)
        e = y - t_ref[...]
        loss_ref[...] += jnp.sum(e * e)
        dy = e * (1.0 / D_MODEL)
        dy_ref[...] = dy
        dyb_ref[...] = dy.astype(BF16)
        d_act = lax.dot_general(dyb_ref[...], w_ref[...], NT_DIMS, preferred_element_type=F32)
        dgu_ref[:, :D_FF] = (d_act * u) * (sg + a - a * sg)
        dgu_ref[:, D_FF:] = d_act * a

    return _rowwise(body, "ffn_down_loss", x2.shape[0], tm, rows=[gu_pre, x2, target], halos=[gu_pre], fulls=[cw, w_down],
                    outs=[(D_MODEL, F32), (D_MODEL, BF16), (D_FF, BF16), (2 * D_FF, F32)], accs=[((SUBLANES, LANES), F32)])


def _ffn_bwd_conv(dgu, gu_pre, cw, tm):
    def body(i, n, d_ref, g_ref, dp_ref, dn_ref, cw_ref, dpre_ref, gc_ref):
        _zero_first(i, gc_ref)
        d = d_ref[...]
        g = g_ref[...]
        d_next = _shift_up(d, 1, jnp.where(i == n - 1, 0.0, dn_ref[...]))
        d_prev = _shift_down(d, 1, jnp.where(i == 0, 0.0, dp_ref[...]))
        dpre_ref[...] = (cw_ref[0:1, :] * d_next + cw_ref[1:2, :] * d + cw_ref[2:3, :] * d_prev).astype(BF16)
        _acc_row(gc_ref, 0, d_next * g)
        _acc_row(gc_ref, 1, d * g)
        _acc_row(gc_ref, 2, d_prev * g)
        _acc_row(gc_ref, 3, d)

    return _rowwise(body, "ffn_bwd_conv", dgu.shape[0], tm, rows=[dgu, gu_pre], halos=[dgu], fulls=[cw],
                    outs=[(2 * D_FF, BF16)], accs=[((SUBLANES, 2 * D_FF), F32)])


def _ffn_bwd_in(dpre, x2, dy, g, w_up, tm):
    cs = 2 * D_FF // N_CHIPS

    def body(i, n, dp_ref, x_ref, dy_ref, g_ref, w_ref, dx_ref, dxb_ref, gg_ref):
        _zero_first(i, gg_ref)
        d_h = jnp.zeros(x_ref.shape, F32)
        for k in range(N_CHIPS):
            d_h = d_h + lax.dot_general(dp_ref[:, k * cs:(k + 1) * cs], w_ref[k], (((1,), (1,)), ((), ())),
                                        preferred_element_type=F32)
        xv = x_ref[...]
        dx, dg = _norm_bwd(xv, _rstd(xv), g_ref[...], d_h)
        _acc_row(gg_ref, 0, dg)
        dx = dx + dy_ref[...]
        dx_ref[...] = dx
        dxb_ref[...] = dx.astype(BF16)

    return _rowwise(body, "ffn_bwd_in", x2.shape[0], tm, rows=[dpre, x2, dy], fulls=[g, w_up],
                    outs=[(D_MODEL, F32), (D_MODEL, BF16)], accs=[((SUBLANES, D_MODEL), F32)])


def _mem_attn_bwd(x1, dx2, dx2b, g, w_q, g_q, km, vm, w_o, tm):
    scale = MEM_HD ** -0.5
    m = km.shape[0]

    def body(i, n, x1_ref, dx2_ref, dx2b_ref, g_ref, wq_ref, gq_ref, km_ref, vm_ref, wo_ref,
             dx1_ref, dx1b_ref, hm_ref, dqr_ref, dkm_ref, dvm_ref, gg_ref, ggq_ref):
        _zero_first(i, dkm_ref, dvm_ref, gg_ref, ggq_ref)
        x1v = x1_ref[...]
        hm, heads = _mem_attn_core(x1v, g_ref, wq_ref, gq_ref, km_ref, vm_ref)
        hm_ref[...] = hm
        cs = D_MODEL // N_CHIPS
        d_o = jnp.zeros((x1v.shape[0], MEM_W), F32)
        for k in range(N_CHIPS):
            d_o = d_o + lax.dot_general(dx2b_ref[:, k * cs:(k + 1) * cs], wo_ref[k], (((1,), (1,)), ((), ())),
                                        preferred_element_type=F32)
        for h in range(MEM_HEADS):
            sl = slice(h * MEM_HD, (h + 1) * MEM_HD)
            qh, rs, qn, p, _ = heads[h]
            d_oh = d_o[:, sl].astype(BF16)
            dp = lax.dot_general(d_oh, vm_ref[:, sl], (((1,), (1,)), ((), ())), preferred_element_type=F32)
            ds = (p * (dp - jnp.sum(dp * p, axis=-1, keepdims=True)) * scale).astype(BF16)
            dqn = jnp.dot(ds, km_ref[:, sl], preferred_element_type=F32)
            dkm_ref[:, sl] += lax.dot_general(ds, qn, (((0,), (0,)), ((), ())), preferred_element_type=F32)
            dvm_ref[:, sl] += lax.dot_general(p.astype(BF16), d_oh, (((0,), (0,)), ((), ())), preferred_element_type=F32)
            dqh, dgq = _norm_bwd(qh, rs, gq_ref[...], dqn)
            _acc_row(ggq_ref, 0, dgq)
            dqr_ref[:, sl] = dqh.astype(BF16)
        d_hm = lax.dot_general(dqr_ref[...], wq_ref[...], (((1,), (1,)), ((), ())), preferred_element_type=F32)
        dx, dg = _norm_bwd(x1v, _rstd(x1v), g_ref[...], d_hm)
        _acc_row(gg_ref, 0, dg)
        dx = dx + dx2_ref[...]
        dx1_ref[...] = dx
        dx1b_ref[...] = dx.astype(BF16)

    return _rowwise(body, "mem_attn_bwd", x1.shape[0], tm, rows=[x1, dx2, dx2b], fulls=[g, w_q, g_q, km, vm, w_o],
                    outs=[(D_MODEL, F32), (D_MODEL, BF16), (D_MODEL, BF16), (MEM_W, BF16)],
                    accs=[((m, MEM_W), F32), ((m, MEM_W), F32), ((SUBLANES, D_MODEL), F32), ((SUBLANES, MEM_HD), F32)])


def _mem_kv_bwd(mem, g_mem, w_kv, g_k, dkm, dvm):
    m = mem.shape[0]

    def body(i, n, mem_ref, dkm_ref, dvm_ref, g_ref, w_ref, gk_ref, gw_ref, gg_ref, ggk_ref, dkv_ref):
        gg_ref[...] = jnp.zeros_like(gg_ref)
        ggk_ref[...] = jnp.zeros_like(ggk_ref)
        mv = mem_ref[...]
        rs_m = _rstd(mv)
        mem_n = (mv * rs_m * g_ref[...]).astype(BF16)
        kv = jnp.dot(mem_n, w_ref[...], preferred_element_type=F32)
        for h in range(MEM_HEADS):
            sl = slice(h * MEM_HD, (h + 1) * MEM_HD)
            kh = kv[:, sl]
            dkh, dgk = _norm_bwd(kh, _rstd(kh), gk_ref[...], dkm_ref[:, sl])
            _acc_row(ggk_ref, 0, dgk)
            dkv_ref[:, sl] = dkh.astype(BF16)
        dkv_ref[:, MEM_W:] = dvm_ref[...].astype(BF16)
        gw_ref[...] = lax.dot_general(mem_n, dkv_ref[...], (((0,), (0,)), ((), ())), preferred_element_type=F32)
        d_mn = lax.dot_general(dkv_ref[...], w_ref[...], (((1,), (1,)), ((), ())), preferred_element_type=F32)
        _acc_row(gg_ref, 0, d_mn * (mv * rs_m))

    return _rowwise(body, "mem_kv_bwd", m, m, rows=[mem, dkm, dvm], fulls=[g_mem, w_kv, g_k],
                    accs=[((D_MODEL, 2 * MEM_W), F32), ((SUBLANES, D_MODEL), F32), ((SUBLANES, MEM_HD), F32),
                          ((m, 2 * MEM_W), BF16)])


def _mix_out_bwd(dx1b, hf, hb, yg, o, g_lru, g_mla, w_out, tm):
    def body(i, n, dx_ref, hf_ref, hb_ref, yg_ref, o_ref, gl, gm, w_ref, dh_ref, dyg_ref, dob_ref, dl_ref, ggl_ref, ggm_ref):
        _zero_first(i, ggl_ref, ggm_ref)
        dmix = lax.dot_general(dx_ref[...], w_ref[...], (((1,), (1,)), ((), ())), preferred_element_type=F32)
        hs = hf_ref[...] + hb_ref[...]
        ygv = yg_ref[...]
        ge = _gelu(ygv)
        lo = hs * ge
        d_lo, dgl = _norm_bwd(lo, _rstd(lo), gl[...], dmix[:, :LRU_W])
        _acc_row(ggl_ref, 0, dgl)
        dh_ref[...] = d_lo * ge
        dyg_ref[...] = d_lo * hs * _gelu_grad(ygv)
        ov = o_ref[...]
        d_o, dgm = _norm_bwd(ov, _rstd(ov), gm[...], dmix[:, LRU_W:])
        _acc_row(ggm_ref, 0, dgm)
        dob_ref[...] = d_o.astype(BF16)
        prod = d_o * ov
        lane_w = lax.broadcasted_iota(jnp.int32, prod.shape, 1)
        lane = lax.broadcasted_iota(jnp.int32, (prod.shape[0], LANES), 1)
        dl = jnp.zeros((prod.shape[0], LANES), F32)
        for h in range(HEADS):
            in_head = (lane_w >= h * V_DIM) & (lane_w < (h + 1) * V_DIM)
            dl = dl + jnp.where(lane == h, jnp.sum(jnp.where(in_head, prod, 0.0), axis=-1, keepdims=True), 0.0)
        dl_ref[...] = dl

    return _rowwise(body, "mix_out_bwd", dx1b.shape[0], tm, rows=[dx1b, hf, hb, yg, o], fulls=[g_lru, g_mla, w_out],
                    outs=[(LRU_W, F32), (LRU_W, F32), (MLA_W, BF16), (LANES, F32)],
                    accs=[((SUBLANES, LRU_W), F32), ((SUBLANES, MLA_W), F32)])


def _mla_qkv_bwd(cq, ckv, krp, cos_t, sin_t, dq, dk, dv, g_qa, g_kva, g_qn, g_kn, w_uq_p, w_uk_p, w_uv, tm):
    scale = QK_HEAD ** -0.5

    def body(i, n, cq_ref, ckv_ref, kr_ref, c_ref, s_ref, dq_ref, dk_ref, dv_ref, gqa, gkva, gqn, gkn, wq, wk, wv,
             dcq_ref, dckv_ref, dkr_ref, cqb_ref, dqr_ref, ckvb_ref, dkn_ref, dvb_ref, ggqa, ggkva, ggqn, ggkn):
        _zero_first(i, ggqa, ggkva, ggqn, ggkn)
        cosv, sinv = c_ref[...], s_ref[...]
        cqv = cq_ref[...]
        rs_q = _rstd(cqv)
        cqb_ref[...] = (cqv * rs_q * gqa[...]).astype(BF16)
        qr = jnp.dot(cqb_ref[...], wq[...], preferred_element_type=F32)
        ckvv = ckv_ref[...]
        rs_kv = _rstd(ckvv)
        ckvb_ref[...] = (ckvv * rs_kv * gkva[...]).astype(BF16)
        kn = jnp.dot(ckvb_ref[...], wk[...], preferred_element_type=F32)
        kr = kr_ref[...]
        dkr = jnp.zeros_like(kr)
        for h in range(HEADS):
            sl = slice(h * LANES, (h + 1) * LANES)
            qh = qr[:, sl]
            d_qn = _rope_t(dq_ref[:, sl] * scale, cosv, sinv)
            dqh, dgq = _norm_bwd(qh, _rstd(qh, QK_HEAD), gqn[...], d_qn, QK_HEAD)
            _acc_row(ggqn, 0, dgq)
            dqr_ref[:, sl] = dqh.astype(BF16)
            kh = kn[:, sl] + kr
            d_kn = _rope_t(dk_ref[:, sl] * (1.0 / LOG2E), cosv, sinv)
            dkh, dgk = _norm_bwd(kh, _rstd(kh, QK_HEAD), gkn[...], d_kn, QK_HEAD)
            _acc_row(ggkn, 0, dgk)
            dkn_ref[:, sl] = dkh.astype(BF16)
            dkr = dkr + dkh
        dkr_ref[...] = dkr
        dvb_ref[...] = dv_ref[...].astype(BF16)
        d_cq = lax.dot_general(dqr_ref[...], wq[...], (((1,), (1,)), ((), ())), preferred_element_type=F32)
        dcq, dg = _norm_bwd(cqv, rs_q, gqa[...], d_cq)
        _acc_row(ggqa, 0, dg)
        dcq_ref[...] = dcq
        d_ckv = (lax.dot_general(dkn_ref[...], wk[...], (((1,), (1,)), ((), ())), preferred_element_type=F32)
                 + lax.dot_general(dvb_ref[...], wv[...], (((1,), (1,)), ((), ())), preferred_element_type=F32))
        dckv, dg = _norm_bwd(ckvv, rs_kv, gkva[...], d_ckv)
        _acc_row(ggkva, 0, dg)
        dckv_ref[...] = dckv

    return _rowwise(body, "mla_qkv_bwd", cq.shape[0], tm, rows=[cq, ckv, krp, cos_t, sin_t, dq, dk, dv],
                    fulls=[g_qa, g_kva, g_qn, g_kn, w_uq_p, w_uk_p, w_uv],
                    outs=[(Q_LORA, F32), (KV_LORA, F32), (LANES, F32), (Q_LORA, BF16), (HEADS * LANES, BF16),
                          (KV_LORA, BF16), (HEADS * LANES, BF16), (MLA_W, BF16)],
                    accs=[((SUBLANES, Q_LORA), F32), ((SUBLANES, KV_LORA), F32), ((SUBLANES, LANES), F32),
                          ((SUBLANES, LANES), F32)])


def _in_proj_bwd(x, dx1, dxr_f, dxr_b, dyg, dcq, dckv, dkrp, g, w_in_p, tm):
    def body(i, n, x_ref, dx1_ref, df_ref, db_ref, dyg_ref, dcq_ref, dckv_ref, dkr_ref, g_ref, w_ref, gx_ref, dp_ref, gg_ref):
        _zero_first(i, gg_ref)
        dp_ref[:, :LRU_W] = (df_ref[...] + db_ref[...]).astype(BF16)
        dp_ref[:, LRU_W:2 * LRU_W] = dyg_ref[...].astype(BF16)
        dp_ref[:, 2 * LRU_W:2 * LRU_W + Q_LORA] = dcq_ref[...].astype(BF16)
        dp_ref[:, 2 * LRU_W + Q_LORA:OFF_KR] = dckv_ref[...].astype(BF16)
        dp_ref[:, OFF_KR:] = dkr_ref[...].astype(BF16)
        d_h = lax.dot_general(dp_ref[...], w_ref[...], (((1,), (1,)), ((), ())), preferred_element_type=F32)
        xv = x_ref[...]
        dx, dg = _norm_bwd(xv, _rstd(xv), g_ref[...], d_h)
        _acc_row(gg_ref, 0, dg)
        gx_ref[...] = dx + dx1_ref[...]

    return _rowwise(body, "in_proj_bwd", x.shape[0], tm, rows=[x, dx1, dxr_f, dxr_b, dyg, dcq, dckv, dkrp],
                    fulls=[g, w_in_p], outs=[(D_MODEL, F32), (IN_PAD, BF16)], accs=[((SUBLANES, D_MODEL), F32)])


ANY = pl.BlockSpec(memory_space=pl.ANY)


def _chip_peers(x, y):
    return ((1 - x, y), (x, 1 - y), (1 - x, 1 - y))


def _exchange_call(kern, name, ins, out_shapes, n_sems, aliases=None):
    return pl.pallas_call(
        kern, in_specs=[ANY] * len(ins), out_specs=[ANY] * len(out_shapes), out_shape=out_shapes,
        scratch_shapes=[pltpu.SemaphoreType.DMA((n,)) for n in n_sems], input_output_aliases=aliases or {},
        name=name)(*ins)


def _start_then_wait(copies):
    for cp in copies:
        cp.start()
    for cp in copies:
        cp.wait()


N_DEV = 8
RELATIONS = tuple((dx, dy, dc) for dx in (0, 1) for dy in (0, 1) for dc in (0, 1))[1:]


def _flip(v, d):
    return 1 - v if d else v


def _gather_copies(ins, outs, ssem, rsem, lsem):
    x, y, c = lax.axis_index("x"), lax.axis_index("y"), lax.axis_index("c")
    me = 2 * x + y
    cps = []
    for i, (a, o) in enumerate(zip(ins, outs)):
        cps.append(pltpu.make_async_copy(a, o.at[me], lsem.at[i]))
        for j, (px, py) in enumerate(_chip_peers(x, y)):
            cps.append(pltpu.make_async_remote_copy(a, o.at[me], ssem.at[3 * i + j], rsem.at[3 * i + j],
                                                    device_id=(px, py, c), device_id_type=MESH))
    return cps


def _gather_shapes(arrs):
    return [jax.ShapeDtypeStruct((N_CHIPS,) + a.shape, a.dtype) for a in arrs]


def _gather_sems(n):
    return [pltpu.SemaphoreType.DMA((3 * n,)), pltpu.SemaphoreType.DMA((3 * n,)), pltpu.SemaphoreType.DMA((n,))]


def _gather_chips(arrs):
    n = len(arrs)

    def kern(*refs):
        _start_then_wait(_gather_copies(refs[:n], refs[n:2 * n], *refs[2 * n:]))

    return _exchange_call(kern, "gather_weights", arrs, _gather_shapes(arrs), (3 * n, 3 * n, n))


def _to_owner_copies(ins, outs, ssem, rsem, lsem):
    x, y, c = lax.axis_index("x"), lax.axis_index("y"), lax.axis_index("c")
    me = 4 * x + 2 * y + c
    cps = []
    for i, (a, o) in enumerate(zip(ins, outs)):
        cps.append(pltpu.make_async_copy(a.at[2 * x + y, c], o.at[me], lsem.at[i]))
        for r, (dx, dy, dc) in enumerate(RELATIONS):
            tx, ty, tc = _flip(x, dx), _flip(y, dy), _flip(c, dc)
            cps.append(pltpu.make_async_remote_copy(a.at[2 * tx + ty, tc], o.at[me], ssem.at[7 * i + r], rsem.at[7 * i + r],
                                                    device_id=(tx, ty, tc), device_id_type=MESH))
    return cps


def _to_owner_shapes(arrs):
    return [jax.ShapeDtypeStruct((N_DEV,) + a.shape[2:], a.dtype) for a in arrs]


def _to_owner_sems(n):
    return [pltpu.SemaphoreType.DMA((7 * n,)), pltpu.SemaphoreType.DMA((7 * n,)), pltpu.SemaphoreType.DMA((n,))]


def _to_owner(arrs, name):
    n = len(arrs)

    def kern(*refs):
        _start_then_wait(_to_owner_copies(refs[:n], refs[n:2 * n], *refs[2 * n:]))

    return _exchange_call(kern, name, arrs, _to_owner_shapes(arrs), (7 * n, 7 * n, n))


def _join_halves(arrs):
    n = len(arrs)

    def kern(*refs):
        outs, (ssem, rsem) = refs[n:2 * n], refs[2 * n:]
        x, y, c = lax.axis_index("x"), lax.axis_index("y"), lax.axis_index("c")
        _start_then_wait([
            pltpu.make_async_remote_copy(outs[i].at[c], outs[i].at[c], ssem.at[i], rsem.at[i],
                                         device_id=(x, y, 1 - c), device_id_type=MESH) for i in range(n)])

    outs = [jax.ShapeDtypeStruct(a.shape, a.dtype) for a in arrs]
    return _exchange_call(kern, "grad_join_halves", arrs, outs, (n, n), aliases={i: i for i in range(n)})


def _row_block(rows, row_bytes, limit=1 << 20):
    best = None
    for d in range(16, rows + 1, 16):
        if rows % d == 0 and d * row_bytes <= limit:
            best = d
    return best if best is not None else rows


def _sum_devices(b, c, name):
    _, h, cols = b.shape
    hb = _row_block(h, cols * 4)

    def kern(c_ref, b_ref, o_ref):
        acc = b_ref[0].astype(F32)
        for j in range(1, N_DEV):
            acc = acc + b_ref[j].astype(F32)
        o_ref[...] = acc

    return pl.pallas_call(
        kern,
        grid_spec=pltpu.PrefetchScalarGridSpec(
            num_scalar_prefetch=1, grid=(h // hb,),
            in_specs=[pl.BlockSpec((N_DEV, hb, cols), lambda i, c_ref: (0, i, 0))],
            out_specs=pl.BlockSpec((None, hb, cols), lambda i, c_ref: (c_ref[0], i, 0))),
        out_shape=jax.ShapeDtypeStruct((2, h, cols), F32), name=name, compiler_params=_cparams(1))(c, b)


def _adamw(w, g, m, v, name):
    rows, cols = w.shape
    rb = _row_block(rows, cols * 4)
    c1 = 1.0 - ADAM_B1 ** ADAM_STEP
    c2 = 1.0 - ADAM_B2 ** ADAM_STEP

    def kern(w_ref, g_ref, m_ref, v_ref, d_ref, mo_ref, vo_ref):
        gv = g_ref[...]
        mn = ADAM_B1 * m_ref[...] + (1.0 - ADAM_B1) * gv
        vn = ADAM_B2 * v_ref[...] + (1.0 - ADAM_B2) * (gv * gv)
        mo_ref[...] = mn
        vo_ref[...] = vn
        d_ref[...] = (-ADAM_LR) * ((mn / c1) / (jnp.sqrt(vn / c2) + ADAM_EPS) + ADAM_WD * w_ref[...])

    spec = pl.BlockSpec((rb, cols), lambda i: (i, 0))
    return pl.pallas_call(
        kern, grid=(rows // rb,), in_specs=[spec] * 4, out_specs=[spec] * 3,
        out_shape=[jax.ShapeDtypeStruct(w.shape, F32)] * 3, name=name, compiler_params=_cparams(1))(w, g, m, v)


def _pad_rows(flat, rows):
    return jnp.pad(flat, (0, rows * LANES - flat.shape[0])).reshape(rows, LANES)


def _round_up(n, m):
    return (n + m - 1) // m * m


def _shard_shape(shape, axis):
    return tuple(s // N_CHIPS if a == axis else s for a, s in enumerate(shape))


def _to_shards(full, axis):
    shape = full.shape
    t = full.reshape(shape[:axis] + (N_CHIPS, shape[axis] // N_CHIPS) + shape[axis + 1:])
    return jnp.moveaxis(t, axis, 0).reshape(N_CHIPS, -1)


def _from_shards(sh, shape, axis):
    t = sh.reshape((N_CHIPS,) + _shard_shape(shape, axis))
    t = jnp.moveaxis(t, 0, axis)
    return t.reshape(shape)


BIG = tuple((name, shape, axis) for name, shape, axis, big in SHARDED if big)
EARLY_WEIGHTS = ("w_in", "w_uq", "w_ukv")
SMALL_SHARDED = tuple((name, shape, axis) for name, shape, axis, big in SHARDED if not big)


def _pack_small_weights(p):
    flat = jnp.concatenate([p[name].reshape(-1) for name, _, _ in SMALL_SHARDED])
    return _pad_rows(flat, _round_up(-(-flat.shape[0] // LANES), SUBLANES))


def _unpack_small_weights(gathered):
    flat = gathered.reshape(N_CHIPS, -1)
    out, off = {}, 0
    for name, shape, axis in SMALL_SHARDED:
        n = _numel(shape) // N_CHIPS
        out[name] = _from_shards(flat[:, off:off + n], shape, axis)
        off += n
    return out


def _pack_small_local(p, prefix=""):
    parts = [p[prefix + name].reshape(-1) for name, _, _ in SMALL_SHARDED]
    parts += [p[prefix + name].reshape(-1) for name, _ in REPLICATED]
    return jnp.concatenate(parts)


def _pack_small_grads(g):
    parts = [_to_shards(g[name], axis) for name, _, axis in SMALL_SHARDED]
    rep = jnp.concatenate([g[name].reshape(-1) for name, _ in REPLICATED])
    parts.append(jnp.broadcast_to(rep[None], (N_CHIPS, rep.shape[0])))
    return jnp.concatenate(parts, axis=1)


def _unpack_small_local(flat):
    out, off = {}, 0
    for name, shape, axis in SMALL_SHARDED:
        n = _numel(shape) // N_CHIPS
        out[name] = flat[off:off + n].reshape((1,) + _shard_shape(shape, axis))
        off += n
    for name, shape in REPLICATED:
        n = _numel(shape)
        out[name] = flat[off:off + n].reshape((1,) + shape)
        off += n
    return out


def _grad_shards(g, shape, axis):
    if axis == 0:
        return g.reshape((N_CHIPS,) + _shard_shape(shape, axis))
    return jnp.transpose(g.reshape(shape[0], N_CHIPS, shape[1] // N_CHIPS), (1, 0, 2))


def _cols_from_shards(w4):
    return jnp.transpose(w4, (1, 0, 2)).reshape(w4.shape[1], -1)


def _block_diag(w):
    eye = jnp.eye(LRU_BLOCKS, dtype=w.dtype)
    return jnp.einsum("ncd,nm->ncmd", w, eye).reshape(LRU_W, LRU_W)


def _block_diag_t(g):
    g4 = g.reshape(LRU_BLOCKS, 64, LRU_BLOCKS, 64)
    return jnp.stack([g4[n, :, n, :] for n in range(LRU_BLOCKS)])


def _pad8(a):
    return jnp.pad(a, ((0, SUBLANES - a.shape[0]), (0, 0)))


def kernel(x, mem, positions, attn_norm, w_in, lru_conv_w, lru_conv_b, lru_w_a, lru_b_a, lru_w_i, lru_b_i, lru_lambda, q_a_norm, w_uq, kv_a_norm, w_ukv, mla_q_norm, mla_k_norm, lru_out_norm, mla_out_norm, w_out, mem_attn_norm, mem_norm, w_mem_q, w_mem_kv, mem_q_norm, mem_k_norm, w_mem_o, ffn_norm, w_up, ffn_conv_w, ffn_conv_b, w_down, loss_target, m_attn_norm, m_w_in, m_lru_conv_w, m_lru_conv_b, m_lru_w_a, m_lru_b_a, m_lru_w_i, m_lru_b_i, m_lru_lambda, m_q_a_norm, m_w_uq, m_kv_a_norm, m_w_ukv, m_mla_q_norm, m_mla_k_norm, m_lru_out_norm, m_mla_out_norm, m_w_out, m_mem_attn_norm, m_mem_norm, m_w_mem_q, m_w_mem_kv, m_mem_q_norm, m_mem_k_norm, m_w_mem_o, m_ffn_norm, m_w_up, m_ffn_conv_w, m_ffn_conv_b, m_w_down, v_attn_norm, v_w_in, v_lru_conv_w, v_lru_conv_b, v_lru_w_a, v_lru_b_a, v_lru_w_i, v_lru_b_i, v_lru_lambda, v_q_a_norm, v_w_uq, v_kv_a_norm, v_w_ukv, v_mla_q_norm, v_mla_k_norm, v_lru_out_norm, v_mla_out_norm, v_w_out, v_mem_attn_norm, v_mem_norm, v_w_mem_q, v_w_mem_kv, v_mem_q_norm, v_mem_k_norm, v_w_mem_o, v_ffn_norm, v_w_up, v_ffn_conv_w, v_ffn_conv_b, v_w_down):
    given = dict(locals())
    local = {name: given[name][0] for name in WEIGHT_ORDER}
    s = x.shape[1]
    x2d, mem2d, tgt = x[0], mem[0], loss_target[0]
    tm = min(512, s)
    tm_wide = min(1024, s)
    tm_ffn = min(256, s)
    t_scan = min(512, s)
    tq_f, tq_b, tk = min(4096, s), min(2048, s), min(512, s)

    early = [b for b in BIG if b[0] in EARLY_WEIGHTS]
    late = [b for b in BIG if b[0] not in EARLY_WEIGHTS]
    got = _gather_chips([local[name].astype(BF16) for name, _, _ in early] + [_pack_small_weights(local)])
    full = _unpack_small_weights(got[-1])

    def take_gathered(entries, arrays):
        for (name, shape, axis), w4 in zip(entries, arrays):
            if axis == 0:
                full[name] = w4.reshape(shape)
            elif name in ("w_up", "w_mem_o"):
                full[name] = w4
            else:
                full[name] = _cols_from_shards(w4)

    take_gathered(early, got)
    row = lambda a: a.reshape(1, -1)
    b16 = lambda a: a.astype(BF16)
    zeros = lambda r, c: jnp.zeros((r, c), BF16)
    w_in_f = full["w_in"]
    w_in_p = jnp.concatenate([w_in_f[:, :OFF_KR], zeros(D_MODEL, QK_NOPE), w_in_f[:, OFF_KR:],
                              zeros(D_MODEL, LANES - QK_HEAD)], axis=1)
    w_uq_p = jnp.pad(full["w_uq"].reshape(Q_LORA, HEADS, QK_HEAD), ((0, 0), (0, 0), (0, LANES - QK_HEAD))).reshape(Q_LORA, -1)
    ukv = full["w_ukv"].reshape(KV_LORA, HEADS, QK_NOPE + V_DIM)
    w_uk_p = jnp.pad(ukv[:, :, :QK_NOPE], ((0, 0), (0, 0), (0, LANES - QK_NOPE))).reshape(KV_LORA, -1)
    w_uv = ukv[:, :, QK_NOPE:].reshape(KV_LORA, MLA_W)
    wa = [b16(_block_diag(local["lru_w_a"][d])) for d in range(2)]
    wi = [b16(_block_diag(local["lru_w_i"][d])) for d in range(2)]
    cw = [_pad8(full["lru_conv_w"][d]) for d in range(2)]
    pv = [_pad8(jnp.stack([full["lru_conv_b"][d], full["lru_b_a"][d], full["lru_b_i"][d], full["lru_lambda"][d]]))
          for d in range(2)]
    ffn_cw = _pad8(jnp.concatenate([full["ffn_conv_w"], row(local["ffn_conv_b"])], axis=0))
    g_attn, g_qa, g_kva = row(local["attn_norm"]), row(local["q_a_norm"]), row(local["kv_a_norm"])
    g_qn = jnp.pad(row(local["mla_q_norm"]), ((0, 0), (0, LANES - QK_HEAD)))
    g_kn = jnp.pad(row(local["mla_k_norm"]), ((0, 0), (0, LANES - QK_HEAD)))
    g_lru, g_mla = row(local["lru_out_norm"]), row(local["mla_out_norm"])
    g_memattn, g_mem = row(local["mem_attn_norm"]), row(local["mem_norm"])
    g_mq, g_mk, g_ffn = row(local["mem_q_norm"]), row(local["mem_k_norm"]), row(local["ffn_norm"])

    inv = ROPE_THETA ** (-jnp.arange(0, QK_ROPE, 2, dtype=F32) / QK_ROPE)
    ang = positions[0].astype(F32)[:, None] * inv
    cosv, sinv = jnp.cos(ang), jnp.sin(ang)
    ones, zer = jnp.ones((s, QK_NOPE), F32), jnp.zeros((s, LANES - QK_HEAD), F32)
    cos_t = jnp.concatenate([ones, cosv, cosv, zer + 1.0], axis=1)
    sin_t = jnp.concatenate([ones * 0.0, -sinv, sinv, zer], axis=1)

    xr, yg, cq, ckv, krp, hb_in = _in_proj(x2d, g_attn, w_in_p, tm)
    h_f, *saved_f = _lru_scan_fwd(xr, cw[0], pv[0], wa[0], wi[0], False, t_scan)
    h_b, *saved_b = _lru_scan_fwd(xr, cw[1], pv[1], wa[1], wi[1], True, t_scan)
    q, k, v = _mla_qkv(cq, ckv, krp, cos_t, sin_t, g_qa, g_kva, g_qn, g_kn, w_uq_p, w_uk_p, w_uv, tm_wide)
    o, lse, *got = _attn_fwd(q, k, v, tq_f, tk, shards=[local[name].astype(BF16) for name, _, _ in late])
    take_gathered(late, got)
    x1, mixed = _mix_out(h_f, h_b, yg, o, x2d, g_lru, g_mla, full["w_out"], tm)
    km, vm = _mem_kv(mem2d, g_mem, full["w_mem_kv"], g_mk)
    x2, o_mem = _mem_attn(x1, g_memattn, full["w_mem_q"], g_mq, km, vm, full["w_mem_o"], tm_wide)
    gu_pre, hb_ffn = _ffn_up(x2, g_ffn, full["w_up"], tm)
    dy, dyb, act, dgu, loss_acc = _ffn_down_loss(gu_pre, x2, tgt, ffn_cw, full["w_down"], tm_ffn)
    loss = lax.psum(loss_acc[0, 0] * (0.5 / D_MODEL), ("x", "y", "c"))

    grads = {}
    grads["w_down"] = _matmul_tn(act, dyb, "grad_w_down", out_dtype=BF16)
    dpre, g_conv = _ffn_bwd_conv(dgu, gu_pre, ffn_cw, tm_ffn)
    grads["ffn_conv_w"], grads["ffn_conv_b"] = g_conv[:3], g_conv[3]
    grads["w_up"] = _matmul_tn(hb_ffn, dpre, "grad_w_up", col_shards=True, out_dtype=BF16)
    dx2, dx2b, gg = _ffn_bwd_in(dpre, x2, dy, g_ffn, full["w_up"], tm)
    grads["ffn_norm"] = gg[0]
    grads["w_mem_o"] = _matmul_tn(o_mem, dx2b, "grad_w_mem_o", col_shards=True, out_dtype=BF16)
    dx1, dx1b, hm, dqr_mem, dkm, dvm, gg, ggq = _mem_attn_bwd(x1, dx2, dx2b, g_memattn, full["w_mem_q"], g_mq, km, vm,
                                                                 full["w_mem_o"], tm)
    grads["mem_attn_norm"], grads["mem_q_norm"] = gg[0], ggq[0]
    grads["w_mem_q"] = _matmul_tn(hm, dqr_mem, "grad_w_mem_q", out_dtype=BF16)
    g_mem_kv, gg, ggk, _ = _mem_kv_bwd(mem2d, g_mem, full["w_mem_kv"], g_mk, dkm, dvm)
    grads["w_mem_kv"] = g_mem_kv.astype(BF16)
    grads["mem_norm"], grads["mem_k_norm"] = gg[0], ggk[0]
    grads["w_out"] = _matmul_tn(mixed, dx1b, "grad_w_out", out_dtype=BF16)
    dh, dyg, dob, dl128, ggl, ggm = _mix_out_bwd(dx1b, h_f, h_b, yg, o, g_lru, g_mla, full["w_out"], tm)
    grads["lru_out_norm"], grads["mla_out_norm"] = ggl[0], ggm[0]
    delta_t = jnp.transpose(dl128[:, :HEADS]).reshape(HEADS // 2, 2, s)
    def halves(name, shape, axis):
        g4 = grads[name] if grads[name].ndim == 3 else _grad_shards(grads[name], shape, axis)
        return g4.reshape(N_CHIPS, 2, g4.shape[1] // 2, g4.shape[2])

    dq, dk, dv, *arrived_late = _attn_bwd(q, k, v, dob, lse, delta_t, tq_b, tk,
                                          contributions=[halves(*e) for e in late])
    (dcq, dckv, dkrp, cqb, dqr, ckvb, dkn, dvb, ggqa, ggkva, ggqn, ggkn) = _mla_qkv_bwd(
        cq, ckv, krp, cos_t, sin_t, dq, dk, dv, g_qa, g_kva, g_qn, g_kn, w_uq_p, w_uk_p, w_uv, tm_wide)
    grads["q_a_norm"], grads["kv_a_norm"] = ggqa[0], ggkva[0]
    grads["mla_q_norm"], grads["mla_k_norm"] = ggqn[0, :QK_HEAD], ggkn[0, :QK_HEAD]
    g_uq_p = _matmul_tn(cqb, dqr, "grad_w_uq")
    grads["w_uq"] = g_uq_p.reshape(Q_LORA, HEADS, LANES)[:, :, :QK_HEAD].reshape(Q_LORA, -1)
    g_uk_p = _matmul_tn(ckvb, dkn, "grad_w_uk").reshape(KV_LORA, HEADS, LANES)[:, :, :QK_NOPE]
    g_uv = _matmul_tn(ckvb, dvb, "grad_w_uv").reshape(KV_LORA, HEADS, V_DIM)
    grads["w_ukv"] = jnp.concatenate([g_uk_p, g_uv], axis=2).reshape(KV_LORA, -1)
    dxr, gwa, gwi, gvec = [], [], [], []
    for d, (hd, saved) in enumerate(((h_f, saved_f), (h_b, saved_b))):
        r = _lru_scan_bwd(xr, saved, hd, dh, cw[d], pv[d], wa[d], wi[d], d == 1, t_scan)
        dxr.append(r[0])
        gwa.append(_block_diag_t(r[1]))
        gwi.append(_block_diag_t(r[2]))
        gvec.append(r[3])
    grads["lru_w_a"], grads["lru_w_i"] = jnp.stack(gwa), jnp.stack(gwi)
    grads["lru_conv_w"] = jnp.stack([gv[:CONV_W] for gv in gvec])
    for r_i, name in ((4, "lru_conv_b"), (5, "lru_b_a"), (6, "lru_b_i"), (7, "lru_lambda")):
        grads[name] = jnp.stack([gv[r_i] for gv in gvec])
    grad_x, dproj, gg = _in_proj_bwd(x2d, dx1, dxr[0], dxr[1], dyg, dcq, dckv, dkrp, g_attn, w_in_p, tm)
    grads["attn_norm"] = gg[0]
    g_in_p = _matmul_tn(hb_in, dproj, "grad_w_in")
    grads["w_in"] = jnp.concatenate([g_in_p[:, :OFF_KR], g_in_p[:, OFF_KR + QK_NOPE:OFF_KR + QK_HEAD]], axis=1)

    small = _pack_small_grads(grads)
    length = small.shape[1]
    hrows = _round_up(-(-length // (2 * LANES)), 16)
    small = jnp.pad(small, ((0, 0), (0, 2 * hrows * LANES - length))).reshape(N_CHIPS, 2, hrows, LANES)
    for name, _, _ in early:
        grads[name] = grads[name].astype(BF16)
    arrived_early = _to_owner([halves(*e) for e in early] + [small], "grad_to_owner")
    names = [name for name, _, _ in late + early] + ["small"]
    c_idx = lax.axis_index("c").astype(jnp.int32).reshape(1)
    reduced = _join_halves([_sum_devices(b, c_idx, "grad_sum_" + n)
                            for n, b in zip(names, list(arrived_late) + list(arrived_early))])

    outs = [{}, {}, {}, {}]
    for (name, shape, axis), r in zip(late + early, reduced):
        g2 = r.reshape(_shard_shape(shape, axis))
        res = _adamw(local[name], g2, given["m_" + name][0], given["v_" + name][0], "adamw_" + name)
        for o_, a in zip(outs, (g2, *res)):
            o_[name] = a[None]
    pack = lambda prefix: _pad_rows(_pack_small_local({n: given[prefix + n] for n in WEIGHT_ORDER}), 2 * hrows)
    g_small = reduced[-1].reshape(2 * hrows, LANES)
    res = _adamw(pack(""), g_small, pack("m_"), pack("v_"), "adamw_small")
    for o_, a in zip(outs, (g_small, *res)):
        o_.update(_unpack_small_local(a.reshape(-1)))
    return (loss, grad_x[None], *[o_[n] for o_ in outs for n in WEIGHT_ORDER])
```

```python
import jax
import jax.numpy as jnp
from jax import lax
from jax.experimental import pallas as pl
from jax.experimental.pallas import tpu as pltpu

F32, BF16 = jnp.float32, jnp.bfloat16
MESH = pl.DeviceIdType.MESH

D_MODEL = 1024
EPS = 1e-6
LRU_W = 512
LRU_BLOCKS = 8
LRU_C = 8.0
CONV_W = 4
HEADS = 8
QK_NOPE, QK_ROPE, QK_HEAD, V_DIM = 64, 32, 96, 64
Q_LORA, KV_LORA = 256, 128
MLA_W = HEADS * V_DIM
ROPE_THETA = 10000.0
IN_COLS = 2 * LRU_W + Q_LORA + KV_LORA + QK_ROPE
OFF_KR = IN_COLS - QK_ROPE
IN_PAD = 1536
MEM_HEADS, MEM_HD = 4, 128
MEM_W = MEM_HEADS * MEM_HD
D_FF = 2816
N_CHIPS = 4
ADAM_LR, ADAM_B1, ADAM_B2, ADAM_EPS, ADAM_WD, ADAM_STEP = 0.001, 0.9, 0.999, 1e-08, 0.01, 10

LANES = 128
SUBLANES = 8
V7X_VMEM_BYTES = 64 * 1024 * 1024
VMEM_LIMIT = V7X_VMEM_BYTES * 7 // 8

SHARDED = (
    ("w_in", (D_MODEL, IN_COLS), 1, True),
    ("lru_conv_w", (2, CONV_W, LRU_W), 2, False),
    ("lru_conv_b", (2, LRU_W), 1, False),
    ("lru_b_a", (2, LRU_W), 1, False),
    ("lru_b_i", (2, LRU_W), 1, False),
    ("lru_lambda", (2, LRU_W), 1, False),
    ("w_uq", (Q_LORA, HEADS * QK_HEAD), 1, True),
    ("w_ukv", (KV_LORA, HEADS * (QK_NOPE + V_DIM)), 1, True),
    ("w_out", (2 * LRU_W, D_MODEL), 0, True),
    ("w_mem_q", (D_MODEL, MEM_W), 0, True),
    ("w_mem_kv", (D_MODEL, 2 * MEM_W), 0, True),
    ("w_mem_o", (MEM_W, D_MODEL), 1, True),
    ("w_up", (D_MODEL, 2 * D_FF), 1, True),
    ("ffn_conv_w", (3, 2 * D_FF), 1, False),
    ("w_down", (D_FF, D_MODEL), 0, True),
)
REPLICATED = (
    ("attn_norm", (D_MODEL,)), ("lru_w_a", (2, LRU_BLOCKS, 64, 64)), ("lru_w_i", (2, LRU_BLOCKS, 64, 64)),
    ("q_a_norm", (Q_LORA,)), ("kv_a_norm", (KV_LORA,)), ("mla_q_norm", (QK_HEAD,)), ("mla_k_norm", (QK_HEAD,)),
    ("lru_out_norm", (LRU_W,)), ("mla_out_norm", (MLA_W,)), ("mem_attn_norm", (D_MODEL,)), ("mem_norm", (D_MODEL,)),
    ("mem_q_norm", (MEM_HD,)), ("mem_k_norm", (MEM_HD,)), ("ffn_norm", (D_MODEL,)), ("ffn_conv_b", (2 * D_FF,)),
)
WEIGHT_ORDER = ('attn_norm', 'w_in', 'lru_conv_w', 'lru_conv_b', 'lru_w_a', 'lru_b_a', 'lru_w_i', 'lru_b_i', 'lru_lambda',
                'q_a_norm', 'w_uq', 'kv_a_norm', 'w_ukv', 'mla_q_norm', 'mla_k_norm', 'lru_out_norm', 'mla_out_norm', 'w_out',
                'mem_attn_norm', 'mem_norm', 'w_mem_q', 'w_mem_kv', 'mem_q_norm', 'mem_k_norm', 'w_mem_o', 'ffn_norm', 'w_up',
                'ffn_conv_w', 'ffn_conv_b', 'w_down')


def _numel(shape):
    n = 1
    for s in shape:
        n *= s
    return n


def _cparams(n_axes):
    return pltpu.CompilerParams(dimension_semantics=("arbitrary",) * n_axes, vmem_limit_bytes=VMEM_LIMIT)


def _bdot(a, b):
    return jnp.dot(a.astype(BF16), b.astype(BF16), preferred_element_type=F32)


def _bdot_nt(a, b):
    return lax.dot_general(a.astype(BF16), b.astype(BF16), (((1,), (1,)), ((), ())), preferred_element_type=F32)


def _bdot_tn(a, b):
    return lax.dot_general(a.astype(BF16), b.astype(BF16), (((0,), (0,)), ((), ())), preferred_element_type=F32)


def _rstd(x, n=None):
    n = x.shape[-1] if n is None else n
    return lax.rsqrt(jnp.sum(x * x, axis=-1, keepdims=True) * (1.0 / n) + EPS)


def _norm_bwd(x, rs, g, dy, n=None):
    n = x.shape[-1] if n is None else n
    xhat = x * rs
    dxh = dy * g
    dx = rs * (dxh - xhat * (jnp.sum(dxh * xhat, axis=-1, keepdims=True) * (1.0 / n)))
    return dx, dy * xhat


def _acc_row(ref, r, val):
    ref[r:r + 1, :] += jnp.sum(val, axis=0, keepdims=True)


def _zero_first(i, *refs):
    @pl.when(i == 0)
    def _():
        for r in refs:
            r[...] = jnp.zeros_like(r)


def _shift_down(x, j, halo):
    if j == 0:
        return x
    xs = pltpu.roll(x, j, 0)
    hs = pltpu.roll(halo, j, 0)
    row = lax.broadcasted_iota(jnp.int32, hs.shape, 0)
    top = jnp.where(row < j, hs, xs[:SUBLANES])
    return jnp.concatenate([top, xs[SUBLANES:]], axis=0)


def _shift_up(x, j, halo):
    if j == 0:
        return x
    t = x.shape[0]
    xs = pltpu.roll(x, t - j, 0)
    hs = pltpu.roll(halo, SUBLANES - j, 0)
    row = lax.broadcasted_iota(jnp.int32, hs.shape, 0)
    bot = jnp.where(row >= SUBLANES - j, hs, xs[t - SUBLANES:])
    return jnp.concatenate([xs[:t - SUBLANES], bot], axis=0)


def _shift(x, j, halo, down):
    return _shift_down(x, j, halo) if down else _shift_up(x, j, halo)


def _scan(a, b, h_in, down):
    t, c = a.shape
    g = t // SUBLANES
    a3, b3 = a.reshape(g, SUBLANES, c), b.reshape(g, SUBLANES, c)
    sub = lax.broadcasted_iota(jnp.int32, a3.shape, 1)
    d = 1
    while d < SUBLANES:
        keep = (sub >= d) if down else (sub < SUBLANES - d)
        shift = d if down else SUBLANES - d
        a_s = jnp.where(keep, pltpu.roll(a3, shift, 1), 1.0)
        b_s = jnp.where(keep, pltpu.roll(b3, shift, 1), 0.0)
        b3 = a3 * b_s + b3
        a3 = a3 * a_s
        d *= 2
    hs = [None] * g
    carry = h_in
    for i in (range(g) if down else range(g - 1, -1, -1)):
        hs[i] = a3[i] * carry + b3[i]
        carry = hs[i][SUBLANES - 1:, :] if down else hs[i][:1, :]
    return jnp.concatenate(hs, axis=0)


def _sigmoid(x):
    return 0.5 * jnp.tanh(0.5 * x) + 0.5


LOG2E = 1.4426950408889634
GELU_K = 0.7978845608028654
GELU_C = 0.044715


def _gelu(x):
    return 0.5 * x * (1.0 + jnp.tanh(GELU_K * (x + GELU_C * x * x * x)))


def _gelu_grad(x):
    t = jnp.tanh(GELU_K * (x + GELU_C * x * x * x))
    return 0.5 * (1.0 + t) + 0.5 * x * (1.0 - t * t) * GELU_K * (1.0 + 3.0 * GELU_C * x * x)


ROPE_HALF = QK_ROPE // 2
ROPE_LANE = 32


def _head_tile(nope, rope):
    z = lambda n: jnp.zeros(nope.shape[:-1] + (n,), nope.dtype)
    r1, r2 = (z(ROPE_HALF), z(ROPE_HALF)) if rope is None else (rope[..., :ROPE_HALF], rope[..., ROPE_HALF:])
    return jnp.concatenate([nope[..., :ROPE_LANE], r1, nope[..., ROPE_LANE:], z(ROPE_HALF), r2, z(ROPE_HALF)], axis=-1)


def _from_head_tile(t):
    a, b = ROPE_LANE + ROPE_HALF, ROPE_LANE + LANES // 2
    return (jnp.concatenate([t[..., :ROPE_LANE], t[..., a:a + QK_NOPE - ROPE_LANE]], axis=-1),
            jnp.concatenate([t[..., ROPE_LANE:a], t[..., b:b + ROPE_HALF]], axis=-1))


def _rope_partner(x):
    lane = lax.broadcasted_iota(jnp.int32, x.shape, 1) & (LANES // 2 - 1)
    return jnp.where((lane >= ROPE_LANE) & (lane < ROPE_LANE + ROPE_HALF), pltpu.roll(x, LANES // 2, 1), 0.0)


def _rope(x, cos_t, sin_t):
    return x * cos_t + _rope_partner(x) * sin_t


def _rope_t(dy, cos_t, sin_t):
    return dy * cos_t + _rope_partner(dy * sin_t)


def _rowwise(body, name, s, tm, rows=(), halos=(), fulls=(), outs=(), accs=()):
    n = s // tm
    hb = tm // SUBLANES
    last8 = s // SUBLANES - 1
    in_specs, args = [], []
    for a in rows:
        in_specs.append(pl.BlockSpec((tm, a.shape[1]), lambda i: (i, 0)))
        args.append(a)
    for a in halos:
        in_specs.append(pl.BlockSpec((SUBLANES, a.shape[1]), lambda i: (jnp.maximum(i * hb - 1, 0), 0)))
        in_specs.append(pl.BlockSpec((SUBLANES, a.shape[1]), lambda i: (jnp.minimum((i + 1) * hb, last8), 0)))
        args += [a, a]
    for a in fulls:
        in_specs.append(pl.BlockSpec(a.shape, lambda i, nd=a.ndim: (0,) * nd))
        args.append(a)
    out_shape, out_specs = [], []
    for c, dt in outs:
        out_shape.append(jax.ShapeDtypeStruct((s, c), dt))
        out_specs.append(pl.BlockSpec((tm, c), lambda i: (i, 0)))
    for shp, dt in accs:
        out_shape.append(jax.ShapeDtypeStruct(shp, dt))
        out_specs.append(pl.BlockSpec(shp, lambda i, nd=len(shp): (0,) * nd))

    def kern(*refs):
        body(pl.program_id(0), n, *refs)

    return pl.pallas_call(kern, grid=(n,), in_specs=in_specs, out_specs=out_specs, out_shape=out_shape, name=name,
                          compiler_params=_cparams(1))(*args)


def _matmul_tn(a, b, name, col_shards=False, out_dtype=F32):
    t, m = a.shape
    n = b.shape[1]
    bm = m
    for cand in range(LANES, m + 1, LANES):
        if m % cand == 0 and cand * (n // N_CHIPS if col_shards else min(n, 2048)) * 4 <= 6 * 1024 * 1024:
            bm = cand
    bn = n // N_CHIPS if col_shards else (n if n <= 2048 else 1408)
    bt = min(t, 2048 if max(bm, bn) <= 512 else (1024 if max(bm, bn) <= 1024 else 512))
    nt = t // bt

    def kern(a_ref, b_ref, o_ref, acc_ref):
        k = pl.program_id(2)

        @pl.when(k == 0)
        def _():
            acc_ref[...] = jnp.zeros_like(acc_ref)
        acc_ref[...] += _bdot_tn(a_ref[...], b_ref[...])

        @pl.when(k == nt - 1)
        def _():
            o_ref[...] = acc_ref[...].astype(out_dtype)

    if col_shards:
        out_spec = pl.BlockSpec((None, bm, bn), lambda i, j, k: (j, i, 0))
        out_shape = jax.ShapeDtypeStruct((N_CHIPS, m, bn), out_dtype)
    else:
        out_spec = pl.BlockSpec((bm, bn), lambda i, j, k: (i, j))
        out_shape = jax.ShapeDtypeStruct((m, n), out_dtype)
    return pl.pallas_call(
        kern, grid=(m // bm, n // bn, nt),
        in_specs=[pl.BlockSpec((bt, bm), lambda i, j, k: (k, i)), pl.BlockSpec((bt, bn), lambda i, j, k: (k, j))],
        out_specs=out_spec, out_shape=out_shape, scratch_shapes=[pltpu.VMEM((bm, bn), F32)], name=name,
        compiler_params=_cparams(3))(a, b)


def _in_proj(x, g, w_in_p, tm):
    def body(i, n, x_ref, g_ref, w_ref, xr, yg, cq, ckv, krp, hb):
        xv = x_ref[...]
        h = (xv * _rstd(xv) * g_ref[...]).astype(BF16)
        hb[...] = h
        p = jnp.dot(h, w_ref[...], preferred_element_type=F32)
        xr[...] = p[:, :LRU_W]
        yg[...] = p[:, LRU_W:2 * LRU_W]
        cq[...] = p[:, 2 * LRU_W:2 * LRU_W + Q_LORA]
        ckv[...] = p[:, 2 * LRU_W + Q_LORA:OFF_KR]
        krp[...] = p[:, OFF_KR:IN_PAD]

    return _rowwise(body, "in_proj", x.shape[0], tm, rows=[x], fulls=[g, w_in_p],
                    outs=[(LRU_W, F32), (LRU_W, F32), (Q_LORA, F32), (KV_LORA, F32), (LANES, F32), (D_MODEL, BF16)])


def _softplus_neg(lam):
    e = jnp.exp(-jnp.abs(lam))
    return jnp.maximum(-lam, 0.0) + jnp.where(e < 1e-2, e * (1.0 - e * (0.5 - e * (1.0 / 3.0))), jnp.log(1.0 + e))


def _lru_gates(x, halo, cw_ref, pv_ref, wa_ref, wi_ref, rev):
    down = not rev
    xc = pv_ref[0:1, :] + jnp.zeros_like(x)
    for j in range(CONV_W):
        k = j if rev else CONV_W - 1 - j
        xc = xc + cw_ref[k:k + 1, :] * _shift(x, j, halo, down)
    r = _sigmoid(_bdot(xc, wa_ref[...]) + pv_ref[1:2, :])
    ig = _sigmoid(_bdot(xc, wi_ref[...]) + pv_ref[2:3, :])
    lam = pv_ref[3:4, :]
    sp = _softplus_neg(lam)
    log_a = (-LRU_C) * r * sp
    a = jnp.exp(log_a)
    z = 2.0 * log_a
    series = -(z * (1.0 + z * (0.5 + z * (1.0 / 6.0 + z * (1.0 / 24.0)))))
    om = jnp.where(z > -0.02, series, 1.0 - a * a)
    mult = jnp.sqrt(om)
    return xc, r, ig, sp, a, mult


def _lru_scan_fwd(xr, cw, pv, wa, wi, rev, t):
    s = xr.shape[0]
    n = s // t
    hb = t // SUBLANES
    last8 = s // SUBLANES - 1
    down = not rev

    def kern(x_ref, halo_ref, cw_ref, pv_ref, wa_ref, wi_ref, h_ref, xc_ref, r_ref, ig_ref, a_ref, mult_ref, carry_ref):
        i = pl.program_id(0)
        _zero_first(i, carry_ref)
        halo = jnp.where(i == 0, 0.0, halo_ref[...])
        xc, r, ig, sp, a, mult = _lru_gates(x_ref[...], halo, cw_ref, pv_ref, wa_ref, wi_ref, rev)
        xc_ref[...], r_ref[...], ig_ref[...], a_ref[...], mult_ref[...] = xc, r, ig, a, mult
        h_ref[...] = _scan(a, mult * ig * xc, carry_ref[...], down)
        carry_ref[...] = h_ref[pl.ds(t - 1 if down else 0, 1), :]

    if rev:
        blk = lambda i: (n - 1 - i, 0)
        hal = lambda i: (jnp.minimum((n - i) * hb, last8), 0)
    else:
        blk = lambda i: (i, 0)
        hal = lambda i: (jnp.maximum(i * hb - 1, 0), 0)
    full = lambda a: pl.BlockSpec(a.shape, lambda i: (0, 0))
    return pl.pallas_call(
        kern, grid=(n,),
        in_specs=[pl.BlockSpec((t, LRU_W), blk), pl.BlockSpec((SUBLANES, LRU_W), hal), full(cw), full(pv), full(wa), full(wi)],
        out_specs=[pl.BlockSpec((t, LRU_W), blk)] * 6, out_shape=[jax.ShapeDtypeStruct((s, LRU_W), F32)] * 6,
        scratch_shapes=[pltpu.VMEM((1, LRU_W), F32)], name="lru_scan_rev" if rev else "lru_scan_fwd",
        compiler_params=_cparams(1))(xr, xr, cw, pv, wa, wi)


def _lru_scan_bwd(xr, saved, h, dh, cw, pv, wa, wi, rev, t):
    s = xr.shape[0]
    n = s // t
    hb = t // SUBLANES
    last8 = s // SUBLANES - 1
    down = not rev

    def kern(x_ref, xc_ref, r_ref, ig_ref, a_ref, mult_ref, h_ref, hh_ref, dh_ref, cw_ref, pv_ref, wa_ref, wi_ref,
             dx_ref, gwa_ref, gwi_ref, gv_ref, p_ref, dxc_halo_ref, tmp_ref):
        i = pl.program_id(0)
        _zero_first(i, gwa_ref, gwi_ref, gv_ref, p_ref, dxc_halo_ref)
        at_start = i == n - 1
        x = x_ref[...]
        hhalo = jnp.where(at_start, 0.0, hh_ref[...])
        xc, r, ig, a, mult = xc_ref[...], r_ref[...], ig_ref[...], a_ref[...], mult_ref[...]
        lam = pv_ref[3:4, :]
        sp = _softplus_neg(lam)
        h_prev = _shift(h_ref[...], 1, hhalo, down)
        row = lax.broadcasted_iota(jnp.int32, x.shape, 0)
        edge = t - 1 if down else 0
        dh_mod = dh_ref[...] + jnp.where(row == edge, p_ref[...], 0.0)
        a_next = _shift(a, 1, jnp.zeros((SUBLANES, LRU_W), F32), not down)
        g = _scan(a_next, dh_mod, jnp.zeros((1, LRU_W), F32), not down)
        tmp_ref[...] = a * g
        p_ref[...] = tmp_ref[pl.ds(0 if down else t - 1, 1), :]
        da = g * h_prev
        d_ig = g * mult * xc
        d_xc = g * mult * ig
        d_om = g * ig * xc * (0.5 / jnp.maximum(mult, 1e-30))
        d_log_a = da * a - 2.0 * d_om * a * a
        d_r = d_log_a * ((-LRU_C) * sp)
        d_sp = jnp.sum(d_log_a * ((-LRU_C) * r), axis=0, keepdims=True)
        gv_ref[7:8, :] += d_sp * (-_sigmoid(-lam))
        d_ga = d_r * r * (1.0 - r)
        d_gi = d_ig * ig * (1.0 - ig)
        _acc_row(gv_ref, 5, d_ga)
        _acc_row(gv_ref, 6, d_gi)
        d_xc = d_xc + _bdot_nt(d_ga, wa_ref[...]) + _bdot_nt(d_gi, wi_ref[...])
        gwa_ref[...] += _bdot_tn(xc, d_ga)
        gwi_ref[...] += _bdot_tn(xc, d_gi)
        _acc_row(gv_ref, 4, d_xc)
        dx = jnp.zeros_like(x)
        dxc_halo = dxc_halo_ref[...]
        for j in range(CONV_W):
            k = j if rev else CONV_W - 1 - j
            d_shift = _shift(d_xc, j, dxc_halo, not down)
            _acc_row(gv_ref, k, d_shift * x)
            dx = dx + cw_ref[k:k + 1, :] * d_shift
        dx_ref[...] = dx
        dxc_halo_ref[...] = d_xc[:SUBLANES] if down else d_xc[t - SUBLANES:]

    if rev:
        blk = lambda i: (i, 0)
        hal = lambda i: (jnp.minimum((i + 1) * hb, last8), 0)
    else:
        blk = lambda i: (n - 1 - i, 0)
        hal = lambda i: (jnp.maximum((n - 1 - i) * hb - 1, 0), 0)
    full = lambda a: pl.BlockSpec(a.shape, lambda i: (0, 0))
    bs = pl.BlockSpec((t, LRU_W), blk)
    hs = pl.BlockSpec((SUBLANES, LRU_W), hal)
    return pl.pallas_call(
        kern, grid=(n,),
        in_specs=[bs] * 7 + [hs, bs, full(cw), full(pv), full(wa), full(wi)],
        out_specs=[bs, pl.BlockSpec((LRU_W, LRU_W), lambda i: (0, 0)), pl.BlockSpec((LRU_W, LRU_W), lambda i: (0, 0)),
                   pl.BlockSpec((SUBLANES, LRU_W), lambda i: (0, 0))],
        out_shape=[jax.ShapeDtypeStruct((s, LRU_W), F32), jax.ShapeDtypeStruct((LRU_W, LRU_W), F32),
                   jax.ShapeDtypeStruct((LRU_W, LRU_W), F32), jax.ShapeDtypeStruct((SUBLANES, LRU_W), F32)],
        scratch_shapes=[pltpu.VMEM((1, LRU_W), F32), pltpu.VMEM((SUBLANES, LRU_W), F32), pltpu.VMEM((t, LRU_W), F32)],
        name="lru_bwd_rev" if rev else "lru_bwd_fwd", compiler_params=_cparams(1))(xr, *saved, h, h, dh, cw, pv, wa, wi)


def _mla_qkv(cq, ckv, krp, cos_t, sin_t, g_qa, g_kva, g_qn, g_kn, w_uq_p, w_uk_p, w_uv, tm):
    scale = QK_HEAD ** -0.5 * LOG2E

    def body(i, n, cq_ref, ckv_ref, kr_ref, c_ref, s_ref, gqa, gkva, gqn, gkn, wq, wk, wv, q_out, k_out, v_out):
        cosv, sinv = c_ref[...], s_ref[...]
        cqv = cq_ref[...]
        qr = _bdot(cqv * _rstd(cqv) * gqa[...], wq[...])
        ckvv = ckv_ref[...]
        c_kv = (ckvv * _rstd(ckvv) * gkva[...]).astype(BF16)
        kn = jnp.dot(c_kv, wk[...], preferred_element_type=F32)
        v_out[...] = jnp.dot(c_kv, wv[...], preferred_element_type=F32).astype(BF16)
        kr = kr_ref[...]
        kr_swapped = _rope_partner(kr * gkn[...]) * sinv
        for h in range(HEADS):
            sl = slice(h * LANES, (h + 1) * LANES)
            qh = qr[:, sl]
            qh = _rope(qh * _rstd(qh, QK_HEAD) * gqn[...], cosv, sinv) * scale
            q_out[:, sl] = qh.astype(BF16)
            kh = kn[:, sl] + kr
            rs = _rstd(kh, QK_HEAD)
            k_out[:, sl] = (kh * rs * gkn[...] * cosv + kr_swapped * rs).astype(BF16)

    return _rowwise(body, "mla_qkv", cq.shape[0], tm, rows=[cq, ckv, krp, cos_t, sin_t],
                    fulls=[g_qa, g_kva, g_qn, g_kn, w_uq_p, w_uk_p, w_uv],
                    outs=[(HEADS * LANES, BF16), (HEADS * LANES, BF16), (MLA_W, BF16)])


NT_DIMS = (((1,), (1,)), ((), ()))
TN_DIMS = (((0,), (0,)), ((), ()))


def _riding_exchange(copies_fn, first, last):
    @pl.when(first)
    def _():
        for cp in copies_fn():
            cp.start()

    def finish():
        @pl.when(last)
        def _():
            for cp in copies_fn():
                cp.wait()
    return finish


def _attn_fwd(q, k, v, tq, tk, shards=()):
    s = q.shape[0]
    nq, nk = s // tq, s // tk
    n = len(shards)

    def kern(*refs):
        q_ref, k_ref, v_ref = refs[:3]
        o_ref, lse_ref = refs[3 + n:5 + n]
        acc_ref = refs[5 + 2 * n]
        p_id, i_id = pl.program_id(0), pl.program_id(1)
        finish = _riding_exchange(lambda: _gather_copies(refs[3:3 + n], refs[5 + n:5 + 2 * n], *refs[6 + 2 * n:]),
                                  (p_id == 0) & (i_id == 0), (p_id == HEADS // 2 - 1) & (i_id == nq - 1)) if n else None
        qs = (q_ref[:, :LANES], q_ref[:, LANES:])
        acc_ref[...] = jnp.zeros_like(acc_ref)

        def step(j, carry):
            off = pl.multiple_of(j * tk, tk)
            vc = v_ref[pl.ds(off, tk), :]
            out = []
            for h in range(2):
                m, l = carry[2 * h:2 * h + 2]
                st = lax.dot_general(k_ref[pl.ds(off, tk), h * LANES:(h + 1) * LANES], qs[h], NT_DIMS,
                                     preferred_element_type=F32)
                mn = jnp.maximum(m, jnp.max(st, axis=0, keepdims=True))
                al = jnp.exp2(m - mn)
                pt = jnp.exp2(st - mn)
                l = al * l + jnp.sum(pt, axis=0, keepdims=True)
                acc_ref[h] = al * acc_ref[h] + lax.dot_general(vc, pt.astype(BF16), TN_DIMS, preferred_element_type=F32)
                out += [mn, l]
            return tuple(out)

        init = (jnp.full((1, tq), -1e30, F32), jnp.zeros((1, tq), F32)) * 2
        m0, l0, m1, l1 = lax.fori_loop(0, nk, step, init)
        row = lax.broadcasted_iota(jnp.int32, (LANES, tq), 0)
        o_ref[...] = jnp.where(row < V_DIM, acc_ref[0] / l0, acc_ref[1] / l1).T
        lse_ref[0, 0:1, :] = m0 + jnp.log2(l0)
        lse_ref[0, 1:2, :] = m1 + jnp.log2(l1)
        if n:
            finish()

    return pl.pallas_call(
        kern, grid=(HEADS // 2, nq),
        in_specs=[pl.BlockSpec((tq, 2 * LANES), lambda p, i: (i, p)), pl.BlockSpec((s, 2 * LANES), lambda p, i: (0, p)),
                  pl.BlockSpec((s, LANES), lambda p, i: (0, p))] + [ANY] * n,
        out_specs=[pl.BlockSpec((tq, LANES), lambda p, i: (i, p)), pl.BlockSpec((1, 2, tq), lambda p, i: (p, 0, i))]
        + [ANY] * n,
        out_shape=[jax.ShapeDtypeStruct((s, MLA_W), F32), jax.ShapeDtypeStruct((HEADS // 2, 2, s), F32)]
        + _gather_shapes(shards),
        scratch_shapes=[pltpu.VMEM((2, LANES, tq), F32)] + (_gather_sems(n) if n else []),
        name="attn_fwd", compiler_params=_cparams(2))(q, k, v, *shards)


def _attn_bwd(q, k, v, do, lse, delta, tq, tk, contributions=()):
    s = q.shape[0]
    nq, nk = s // tq, s // tk
    n = len(contributions)

    def kern(*refs):
        q_ref, do_ref, lse_ref, dl_ref, k_ref, v_ref = refs[:6]
        dq_ref, dk_ref, dv_ref = refs[6 + n:9 + n]
        acc_ref = refs[9 + 2 * n]
        p_id, i_id = pl.program_id(0), pl.program_id(1)
        finish = _riding_exchange(lambda: _to_owner_copies(refs[6:6 + n], refs[9 + n:9 + 2 * n], *refs[10 + 2 * n:]),
                                  (p_id == 0) & (i_id == 0), (p_id == HEADS // 2 - 1) & (i_id == nq - 1)) if n else None
        _zero_first(pl.program_id(1), dk_ref, dv_ref)
        acc_ref[...] = jnp.zeros_like(acc_ref)
        qs = (q_ref[:, :LANES], q_ref[:, LANES:])
        doc = do_ref[...]
        lane_q = lax.broadcasted_iota(jnp.int32, (tq, LANES), 1)
        zq = jnp.zeros_like(doc)
        dos = (jnp.where(lane_q < V_DIM, doc, zq), jnp.where(lane_q >= V_DIM, doc, zq))
        lses = (lse_ref[0, 0:1, :], lse_ref[0, 1:2, :])
        dls = (dl_ref[0, 0:1, :], dl_ref[0, 1:2, :])

        def step(j, carry):
            off = pl.multiple_of(j * tk, tk)
            vp = v_ref[pl.ds(off, tk), :]
            lane_k = lax.broadcasted_iota(jnp.int32, (tk, LANES), 1)
            zero = jnp.zeros_like(vp)
            vs = (jnp.where(lane_k < V_DIM, vp, zero), jnp.where(lane_k >= V_DIM, vp, zero))
            for h in range(2):
                sl = slice(h * LANES, (h + 1) * LANES)
                st = lax.dot_general(k_ref[pl.ds(off, tk), sl], qs[h], NT_DIMS, preferred_element_type=F32)
                pt = jnp.exp2(st - lses[h])
                dpt = lax.dot_general(vs[h], doc, NT_DIMS, preferred_element_type=F32)
                dst = (pt * (dpt - dls[h])).astype(BF16)
                dv_ref[pl.ds(off, tk), :] += jnp.dot(pt.astype(BF16), dos[h], preferred_element_type=F32)
                dk_ref[pl.ds(off, tk), sl] += jnp.dot(dst, qs[h], preferred_element_type=F32)
                acc_ref[h] += lax.dot_general(k_ref[pl.ds(off, tk), sl], dst, TN_DIMS, preferred_element_type=F32)
            return carry

        lax.fori_loop(0, nk, step, 0)
        dq_ref[:, :LANES] = acc_ref[0].T
        dq_ref[:, LANES:] = acc_ref[1].T
        if n:
            finish()

    return pl.pallas_call(
        kern, grid=(HEADS // 2, nq),
        in_specs=[pl.BlockSpec((tq, 2 * LANES), lambda p, i: (i, p)), pl.BlockSpec((tq, LANES), lambda p, i: (i, p)),
                  pl.BlockSpec((1, 2, tq), lambda p, i: (p, 0, i)), pl.BlockSpec((1, 2, tq), lambda p, i: (p, 0, i)),
                  pl.BlockSpec((s, 2 * LANES), lambda p, i: (0, p)), pl.BlockSpec((s, LANES), lambda p, i: (0, p))]
        + [ANY] * n,
        out_specs=[pl.BlockSpec((tq, 2 * LANES), lambda p, i: (i, p)), pl.BlockSpec((s, 2 * LANES), lambda p, i: (0, p)),
                   pl.BlockSpec((s, LANES), lambda p, i: (0, p))] + [ANY] * n,
        out_shape=[jax.ShapeDtypeStruct((s, HEADS * LANES), F32), jax.ShapeDtypeStruct((s, HEADS * LANES), F32),
                   jax.ShapeDtypeStruct((s, MLA_W), F32)] + _to_owner_shapes(contributions),
        scratch_shapes=[pltpu.VMEM((2, LANES, tq), F32)] + (_to_owner_sems(n) if n else []),
        name="attn_bwd", compiler_params=_cparams(2))(q, do, lse, delta, k, v, *contributions)


def _mix_out(hf, hb, yg, o, x, g_lru, g_mla, w_out, tm):
    def body(i, n, hf_ref, hb_ref, yg_ref, o_ref, x_ref, gl, gm, w_ref, x1_ref, mix_ref):
        lo = (hf_ref[...] + hb_ref[...]) * _gelu(yg_ref[...])
        ov = o_ref[...]
        mix_ref[:, :LRU_W] = (lo * _rstd(lo) * gl[...]).astype(BF16)
        mix_ref[:, LRU_W:] = (ov * _rstd(ov) * gm[...]).astype(BF16)
        x1_ref[...] = x_ref[...] + jnp.dot(mix_ref[...], w_ref[...], preferred_element_type=F32)

    return _rowwise(body, "mix_out", x.shape[0], tm, rows=[hf, hb, yg, o, x], fulls=[g_lru, g_mla, w_out],
                    outs=[(D_MODEL, F32), (2 * LRU_W, BF16)])


def _mem_kv(mem, g_mem, w_kv, g_k):
    m = mem.shape[0]

    def body(i, n, mem_ref, g_ref, w_ref, gk_ref, km_ref, vm_ref):
        mv = mem_ref[...]
        kv = _bdot(mv * _rstd(mv) * g_ref[...], w_ref[...])
        vm_ref[...] = kv[:, MEM_W:].astype(BF16)
        for h in range(MEM_HEADS):
            sl = slice(h * MEM_HD, (h + 1) * MEM_HD)
            kh = kv[:, sl]
            km_ref[:, sl] = (kh * _rstd(kh) * gk_ref[...]).astype(BF16)

    return _rowwise(body, "mem_kv", m, m, rows=[mem], fulls=[g_mem, w_kv, g_k], outs=[(MEM_W, BF16), (MEM_W, BF16)])


def _mem_attn_core(x1v, g_ref, wq_ref, gq_ref, km_ref, vm_ref):
    scale = MEM_HD ** -0.5
    hm = (x1v * _rstd(x1v) * g_ref[...]).astype(BF16)
    qr = jnp.dot(hm, wq_ref[...], preferred_element_type=F32)
    heads = []
    for h in range(MEM_HEADS):
        sl = slice(h * MEM_HD, (h + 1) * MEM_HD)
        qh = qr[:, sl]
        rs = _rstd(qh)
        qn = (qh * rs * gq_ref[...]).astype(BF16)
        sc = lax.dot_general(qn, km_ref[:, sl], (((1,), (1,)), ((), ())), preferred_element_type=F32) * scale
        e = jnp.exp(sc - jnp.max(sc, axis=-1, keepdims=True))
        p = e / jnp.sum(e, axis=-1, keepdims=True)
        oh = jnp.dot(p.astype(BF16), vm_ref[:, sl], preferred_element_type=F32)
        heads.append((qh, rs, qn, p, oh))
    return hm, heads


def _mem_attn(x1, g, w_q, g_q, km, vm, w_o, tm):
    cs = D_MODEL // N_CHIPS

    def body(i, n, x1_ref, g_ref, wq_ref, gq_ref, km_ref, vm_ref, wo_ref, x2_ref, ob_ref):
        x1v = x1_ref[...]
        _, heads = _mem_attn_core(x1v, g_ref, wq_ref, gq_ref, km_ref, vm_ref)
        for h in range(MEM_HEADS):
            ob_ref[:, h * MEM_HD:(h + 1) * MEM_HD] = heads[h][4].astype(BF16)
        for k in range(N_CHIPS):
            sl = slice(k * cs, (k + 1) * cs)
            x2_ref[:, sl] = x1v[:, sl] + jnp.dot(ob_ref[...], wo_ref[k], preferred_element_type=F32)

    return _rowwise(body, "mem_attn", x1.shape[0], tm, rows=[x1], fulls=[g, w_q, g_q, km, vm, w_o],
                    outs=[(D_MODEL, F32), (MEM_W, BF16)])


def _ffn_up(x2, g, w_up, tm):
    cs = 2 * D_FF // N_CHIPS

    def body(i, n, x_ref, g_ref, w_ref, gu_ref, hb_ref):
        xv = x_ref[...]
        hb_ref[...] = (xv * _rstd(xv) * g_ref[...]).astype(BF16)
        for k in range(N_CHIPS):
            gu_ref[:, k * cs:(k + 1) * cs] = jnp.dot(hb_ref[...], w_ref[k], preferred_element_type=F32)

    return _rowwise(body, "ffn_up", x2.shape[0], tm, rows=[x2], fulls=[g, w_up], outs=[(2 * D_FF, F32), (D_MODEL, BF16)])


def _ffn_conv(gu, prev, nxt, cw_ref, i, n):
    prev = jnp.where(i == 0, 0.0, prev)
    nxt = jnp.where(i == n - 1, 0.0, nxt)
    return (cw_ref[3:4, :] + cw_ref[0:1, :] * _shift_down(gu, 1, prev) + cw_ref[1:2, :] * gu
            + cw_ref[2:3, :] * _shift_up(gu, 1, nxt))


def _ffn_down_loss(gu_pre, x2, target, cw, w_down, tm):
    def body(i, n, gu_ref, x_ref, t_ref, pv_ref, nx_ref, cw_ref, w_ref, dy_ref, dyb_ref, act_ref, dgu_ref, loss_ref):
        _zero_first(i, loss_ref)
        gu = _ffn_conv(gu_ref[...], pv_ref[...], nx_ref[...], cw_ref, i, n)
        g, u = gu[:, :D_FF], gu[:, D_FF:]
        sg = _sigmoid(g)
        a = g * sg
        act_ref[...] = (a * u).astype(BF16)
        y = x_ref[...] + jnp.dot(act_ref[...], w_ref[...], preferred_element_type=F32)
        e = y - t_ref[...]
        loss_ref[...] += jnp.sum(e * e)
        dy = e * (1.0 / D_MODEL)
        dy_ref[...] = dy
        dyb_ref[...] = dy.astype(BF16)
        d_act = lax.dot_general(dyb_ref[...], w_ref[...], NT_DIMS, preferred_element_type=F32)
        dgu_ref[:, :D_FF] = (d_act * u) * (sg + a - a * sg)
        dgu_ref[:, D_FF:] = d_act * a

    return _rowwise(body, "ffn_down_loss", x2.shape[0], tm, rows=[gu_pre, x2, target], halos=[gu_pre], fulls=[cw, w_down],
                    outs=[(D_MODEL, F32), (D_MODEL, BF16), (D_FF, BF16), (2 * D_FF, F32)], accs=[((SUBLANES, LANES), F32)])


def _ffn_bwd_conv(dgu, gu_pre, cw, tm):
    def body(i, n, d_ref, g_ref, dp_ref, dn_ref, cw_ref, dpre_ref, gc_ref):
        _zero_first(i, gc_ref)
        d = d_ref[...]
        g = g_ref[...]
        d_next = _shift_up(d, 1, jnp.where(i == n - 1, 0.0, dn_ref[...]))
        d_prev = _shift_down(d, 1, jnp.where(i == 0, 0.0, dp_ref[...]))
        dpre_ref[...] = (cw_ref[0:1, :] * d_next + cw_ref[1:2, :] * d + cw_ref[2:3, :] * d_prev).astype(BF16)
        _acc_row(gc_ref, 0, d_next * g)
        _acc_row(gc_ref, 1, d * g)
        _acc_row(gc_ref, 2, d_prev * g)
        _acc_row(gc_ref, 3, d)

    return _rowwise(body, "ffn_bwd_conv", dgu.shape[0], tm, rows=[dgu, gu_pre], halos=[dgu], fulls=[cw],
                    outs=[(2 * D_FF, BF16)], accs=[((SUBLANES, 2 * D_FF), F32)])


def _ffn_bwd_in(dpre, x2, dy, g, w_up, tm):
    cs = 2 * D_FF // N_CHIPS

    def body(i, n, dp_ref, x_ref, dy_ref, g_ref, w_ref, dx_ref, dxb_ref, gg_ref):
        _zero_first(i, gg_ref)
        d_h = jnp.zeros(x_ref.shape, F32)
        for k in range(N_CHIPS):
            d_h = d_h + lax.dot_general(dp_ref[:, k * cs:(k + 1) * cs], w_ref[k], (((1,), (1,)), ((), ())),
                                        preferred_element_type=F32)
        xv = x_ref[...]
        dx, dg = _norm_bwd(xv, _rstd(xv), g_ref[...], d_h)
        _acc_row(gg_ref, 0, dg)
        dx = dx + dy_ref[...]
        dx_ref[...] = dx
        dxb_ref[...] = dx.astype(BF16)

    return _rowwise(body, "ffn_bwd_in", x2.shape[0], tm, rows=[dpre, x2, dy], fulls=[g, w_up],
                    outs=[(D_MODEL, F32), (D_MODEL, BF16)], accs=[((SUBLANES, D_MODEL), F32)])


def _mem_attn_bwd(x1, dx2, dx2b, g, w_q, g_q, km, vm, w_o, tm):
    scale = MEM_HD ** -0.5
    m = km.shape[0]

    def body(i, n, x1_ref, dx2_ref, dx2b_ref, g_ref, wq_ref, gq_ref, km_ref, vm_ref, wo_ref,
             dx1_ref, dx1b_ref, hm_ref, dqr_ref, dkm_ref, dvm_ref, gg_ref, ggq_ref):
        _zero_first(i, dkm_ref, dvm_ref, gg_ref, ggq_ref)
        x1v = x1_ref[...]
        hm, heads = _mem_attn_core(x1v, g_ref, wq_ref, gq_ref, km_ref, vm_ref)
        hm_ref[...] = hm
        cs = D_MODEL // N_CHIPS
        d_o = jnp.zeros((x1v.shape[0], MEM_W), F32)
        for k in range(N_CHIPS):
            d_o = d_o + lax.dot_general(dx2b_ref[:, k * cs:(k + 1) * cs], wo_ref[k], (((1,), (1,)), ((), ())),
                                        preferred_element_type=F32)
        for h in range(MEM_HEADS):
            sl = slice(h * MEM_HD, (h + 1) * MEM_HD)
            qh, rs, qn, p, _ = heads[h]
            d_oh = d_o[:, sl].astype(BF16)
            dp = lax.dot_general(d_oh, vm_ref[:, sl], (((1,), (1,)), ((), ())), preferred_element_type=F32)
            ds = (p * (dp - jnp.sum(dp * p, axis=-1, keepdims=True)) * scale).astype(BF16)
            dqn = jnp.dot(ds, km_ref[:, sl], preferred_element_type=F32)
            dkm_ref[:, sl] += lax.dot_general(ds, qn, (((0,), (0,)), ((), ())), preferred_element_type=F32)
            dvm_ref[:, sl] += lax.dot_general(p.astype(BF16), d_oh, (((0,), (0,)), ((), ())), preferred_element_type=F32)
            dqh, dgq = _norm_bwd(qh, rs, gq_ref[...], dqn)
            _acc_row(ggq_ref, 0, dgq)
            dqr_ref[:, sl] = dqh.astype(BF16)
        d_hm = lax.dot_general(dqr_ref[...], wq_ref[...], (((1,), (1,)), ((), ())), preferred_element_type=F32)
        dx, dg = _norm_bwd(x1v, _rstd(x1v), g_ref[...], d_hm)
        _acc_row(gg_ref, 0, dg)
        dx = dx + dx2_ref[...]
        dx1_ref[...] = dx
        dx1b_ref[...] = dx.astype(BF16)

    return _rowwise(body, "mem_attn_bwd", x1.shape[0], tm, rows=[x1, dx2, dx2b], fulls=[g, w_q, g_q, km, vm, w_o],
                    outs=[(D_MODEL, F32), (D_MODEL, BF16), (D_MODEL, BF16), (MEM_W, BF16)],
                    accs=[((m, MEM_W), F32), ((m, MEM_W), F32), ((SUBLANES, D_MODEL), F32), ((SUBLANES, MEM_HD), F32)])


def _mem_kv_bwd(mem, g_mem, w_kv, g_k, dkm, dvm):
    m = mem.shape[0]

    def body(i, n, mem_ref, dkm_ref, dvm_ref, g_ref, w_ref, gk_ref, gw_ref, gg_ref, ggk_ref, dkv_ref):
        gg_ref[...] = jnp.zeros_like(gg_ref)
        ggk_ref[...] = jnp.zeros_like(ggk_ref)
        mv = mem_ref[...]
        rs_m = _rstd(mv)
        mem_n = (mv * rs_m * g_ref[...]).astype(BF16)
        kv = jnp.dot(mem_n, w_ref[...], preferred_element_type=F32)
        for h in range(MEM_HEADS):
            sl = slice(h * MEM_HD, (h + 1) * MEM_HD)
            kh = kv[:, sl]
            dkh, dgk = _norm_bwd(kh, _rstd(kh), gk_ref[...], dkm_ref[:, sl])
            _acc_row(ggk_ref, 0, dgk)
            dkv_ref[:, sl] = dkh.astype(BF16)
        dkv_ref[:, MEM_W:] = dvm_ref[...].astype(BF16)
        gw_ref[...] = lax.dot_general(mem_n, dkv_ref[...], (((0,), (0,)), ((), ())), preferred_element_type=F32)
        d_mn = lax.dot_general(dkv_ref[...], w_ref[...], (((1,), (1,)), ((), ())), preferred_element_type=F32)
        _acc_row(gg_ref, 0, d_mn * (mv * rs_m))

    return _rowwise(body, "mem_kv_bwd", m, m, rows=[mem, dkm, dvm], fulls=[g_mem, w_kv, g_k],
                    accs=[((D_MODEL, 2 * MEM_W), F32), ((SUBLANES, D_MODEL), F32), ((SUBLANES, MEM_HD), F32),
                          ((m, 2 * MEM_W), BF16)])


def _mix_out_bwd(dx1b, hf, hb, yg, o, g_lru, g_mla, w_out, tm):
    def body(i, n, dx_ref, hf_ref, hb_ref, yg_ref, o_ref, gl, gm, w_ref, dh_ref, dyg_ref, dob_ref, dl_ref, ggl_ref, ggm_ref):
        _zero_first(i, ggl_ref, ggm_ref)
        dmix = lax.dot_general(dx_ref[...], w_ref[...], (((1,), (1,)), ((), ())), preferred_element_type=F32)
        hs = hf_ref[...] + hb_ref[...]
        ygv = yg_ref[...]
        ge = _gelu(ygv)
        lo = hs * ge
        d_lo, dgl = _norm_bwd(lo, _rstd(lo), gl[...], dmix[:, :LRU_W])
        _acc_row(ggl_ref, 0, dgl)
        dh_ref[...] = d_lo * ge
        dyg_ref[...] = d_lo * hs * _gelu_grad(ygv)
        ov = o_ref[...]
        d_o, dgm = _norm_bwd(ov, _rstd(ov), gm[...], dmix[:, LRU_W:])
        _acc_row(ggm_ref, 0, dgm)
        dob_ref[...] = d_o.astype(BF16)
        prod = d_o * ov
        lane_w = lax.broadcasted_iota(jnp.int32, prod.shape, 1)
        lane = lax.broadcasted_iota(jnp.int32, (prod.shape[0], LANES), 1)
        dl = jnp.zeros((prod.shape[0], LANES), F32)
        for h in range(HEADS):
            in_head = (lane_w >= h * V_DIM) & (lane_w < (h + 1) * V_DIM)
            dl = dl + jnp.where(lane == h, jnp.sum(jnp.where(in_head, prod, 0.0), axis=-1, keepdims=True), 0.0)
        dl_ref[...] = dl

    return _rowwise(body, "mix_out_bwd", dx1b.shape[0], tm, rows=[dx1b, hf, hb, yg, o], fulls=[g_lru, g_mla, w_out],
                    outs=[(LRU_W, F32), (LRU_W, F32), (MLA_W, BF16), (LANES, F32)],
                    accs=[((SUBLANES, LRU_W), F32), ((SUBLANES, MLA_W), F32)])


def _mla_qkv_bwd(cq, ckv, krp, cos_t, sin_t, dq, dk, dv, g_qa, g_kva, g_qn, g_kn, w_uq_p, w_uk_p, w_uv, tm):
    scale = QK_HEAD ** -0.5

    def body(i, n, cq_ref, ckv_ref, kr_ref, c_ref, s_ref, dq_ref, dk_ref, dv_ref, gqa, gkva, gqn, gkn, wq, wk, wv,
             dcq_ref, dckv_ref, dkr_ref, cqb_ref, dqr_ref, ckvb_ref, dkn_ref, dvb_ref, ggqa, ggkva, ggqn, ggkn):
        _zero_first(i, ggqa, ggkva, ggqn, ggkn)
        cosv, sinv = c_ref[...], s_ref[...]
        cqv = cq_ref[...]
        rs_q = _rstd(cqv)
        cqb_ref[...] = (cqv * rs_q * gqa[...]).astype(BF16)
        qr = jnp.dot(cqb_ref[...], wq[...], preferred_element_type=F32)
        ckvv = ckv_ref[...]
        rs_kv = _rstd(ckvv)
        ckvb_ref[...] = (ckvv * rs_kv * gkva[...]).astype(BF16)
        kn = jnp.dot(ckvb_ref[...], wk[...], preferred_element_type=F32)
        kr = kr_ref[...]
        dkr = jnp.zeros_like(kr)
        for h in range(HEADS):
            sl = slice(h * LANES, (h + 1) * LANES)
            qh = qr[:, sl]
            d_qn = _rope_t(dq_ref[:, sl] * scale, cosv, sinv)
            dqh, dgq = _norm_bwd(qh, _rstd(qh, QK_HEAD), gqn[...], d_qn, QK_HEAD)
            _acc_row(ggqn, 0, dgq)
            dqr_ref[:, sl] = dqh.astype(BF16)
            kh = kn[:, sl] + kr
            d_kn = _rope_t(dk_ref[:, sl] * (1.0 / LOG2E), cosv, sinv)
            dkh, dgk = _norm_bwd(kh, _rstd(kh, QK_HEAD), gkn[...], d_kn, QK_HEAD)
            _acc_row(ggkn, 0, dgk)
            dkn_ref[:, sl] = dkh.astype(BF16)
            dkr = dkr + dkh
        dkr_ref[...] = dkr
        dvb_ref[...] = dv_ref[...].astype(BF16)
        d_cq = lax.dot_general(dqr_ref[...], wq[...], (((1,), (1,)), ((), ())), preferred_element_type=F32)
        dcq, dg = _norm_bwd(cqv, rs_q, gqa[...], d_cq)
        _acc_row(ggqa, 0, dg)
        dcq_ref[...] = dcq
        d_ckv = (lax.dot_general(dkn_ref[...], wk[...], (((1,), (1,)), ((), ())), preferred_element_type=F32)
                 + lax.dot_general(dvb_ref[...], wv[...], (((1,), (1,)), ((), ())), preferred_element_type=F32))
        dckv, dg = _norm_bwd(ckvv, rs_kv, gkva[...], d_ckv)
        _acc_row(ggkva, 0, dg)
        dckv_ref[...] = dckv

    return _rowwise(body, "mla_qkv_bwd", cq.shape[0], tm, rows=[cq, ckv, krp, cos_t, sin_t, dq, dk, dv],
                    fulls=[g_qa, g_kva, g_qn, g_kn, w_uq_p, w_uk_p, w_uv],
                    outs=[(Q_LORA, F32), (KV_LORA, F32), (LANES, F32), (Q_LORA, BF16), (HEADS * LANES, BF16),
                          (KV_LORA, BF16), (HEADS * LANES, BF16), (MLA_W, BF16)],
                    accs=[((SUBLANES, Q_LORA), F32), ((SUBLANES, KV_LORA), F32), ((SUBLANES, LANES), F32),
                          ((SUBLANES, LANES), F32)])


def _in_proj_bwd(x, dx1, dxr_f, dxr_b, dyg, dcq, dckv, dkrp, g, w_in_p, tm):
    def body(i, n, x_ref, dx1_ref, df_ref, db_ref, dyg_ref, dcq_ref, dckv_ref, dkr_ref, g_ref, w_ref, gx_ref, dp_ref, gg_ref):
        _zero_first(i, gg_ref)
        dp_ref[:, :LRU_W] = (df_ref[...] + db_ref[...]).astype(BF16)
        dp_ref[:, LRU_W:2 * LRU_W] = dyg_ref[...].astype(BF16)
        dp_ref[:, 2 * LRU_W:2 * LRU_W + Q_LORA] = dcq_ref[...].astype(BF16)
        dp_ref[:, 2 * LRU_W + Q_LORA:OFF_KR] = dckv_ref[...].astype(BF16)
        dp_ref[:, OFF_KR:] = dkr_ref[...].astype(BF16)
        d_h = lax.dot_general(dp_ref[...], w_ref[...], (((1,), (1,)), ((), ())), preferred_element_type=F32)
        xv = x_ref[...]
        dx, dg = _norm_bwd(xv, _rstd(xv), g_ref[...], d_h)
        _acc_row(gg_ref, 0, dg)
        gx_ref[...] = dx + dx1_ref[...]

    return _rowwise(body, "in_proj_bwd", x.shape[0], tm, rows=[x, dx1, dxr_f, dxr_b, dyg, dcq, dckv, dkrp],
                    fulls=[g, w_in_p], outs=[(D_MODEL, F32), (IN_PAD, BF16)], accs=[((SUBLANES, D_MODEL), F32)])


ANY = pl.BlockSpec(memory_space=pl.ANY)


def _chip_peers(x, y):
    return ((1 - x, y), (x, 1 - y), (1 - x, 1 - y))


def _exchange_call(kern, name, ins, out_shapes, n_sems, aliases=None):
    return pl.pallas_call(
        kern, in_specs=[ANY] * len(ins), out_specs=[ANY] * len(out_shapes), out_shape=out_shapes,
        scratch_shapes=[pltpu.SemaphoreType.DMA((n,)) for n in n_sems], input_output_aliases=aliases or {},
        name=name)(*ins)


def _start_then_wait(copies):
    for cp in copies:
        cp.start()
    for cp in copies:
        cp.wait()


N_DEV = 8
RELATIONS = tuple((dx, dy, dc) for dx in (0, 1) for dy in (0, 1) for dc in (0, 1))[1:]


def _flip(v, d):
    return 1 - v if d else v


def _gather_copies(ins, outs, ssem, rsem, lsem):
    x, y, c = lax.axis_index("x"), lax.axis_index("y"), lax.axis_index("c")
    me = 2 * x + y
    cps = []
    for i, (a, o) in enumerate(zip(ins, outs)):
        cps.append(pltpu.make_async_copy(a, o.at[me], lsem.at[i]))
        for j, (px, py) in enumerate(_chip_peers(x, y)):
            cps.append(pltpu.make_async_remote_copy(a, o.at[me], ssem.at[3 * i + j], rsem.at[3 * i + j],
                                                    device_id=(px, py, c), device_id_type=MESH))
    return cps


def _gather_shapes(arrs):
    return [jax.ShapeDtypeStruct((N_CHIPS,) + a.shape, a.dtype) for a in arrs]


def _gather_sems(n):
    return [pltpu.SemaphoreType.DMA((3 * n,)), pltpu.SemaphoreType.DMA((3 * n,)), pltpu.SemaphoreType.DMA((n,))]


def _gather_chips(arrs):
    n = len(arrs)

    def kern(*refs):
        _start_then_wait(_gather_copies(refs[:n], refs[n:2 * n], *refs[2 * n:]))

    return _exchange_call(kern, "gather_weights", arrs, _gather_shapes(arrs), (3 * n, 3 * n, n))


def _to_owner_copies(ins, outs, ssem, rsem, lsem):
    x, y, c = lax.axis_index("x"), lax.axis_index("y"), lax.axis_index("c")
    me = 4 * x + 2 * y + c
    cps = []
    for i, (a, o) in enumerate(zip(ins, outs)):
        cps.append(pltpu.make_async_copy(a.at[2 * x + y, c], o.at[me], lsem.at[i]))
        for r, (dx, dy, dc) in enumerate(RELATIONS):
            tx, ty, tc = _flip(x, dx), _flip(y, dy), _flip(c, dc)
            cps.append(pltpu.make_async_remote_copy(a.at[2 * tx + ty, tc], o.at[me], ssem.at[7 * i + r], rsem.at[7 * i + r],
                                                    device_id=(tx, ty, tc), device_id_type=MESH))
    return cps


def _to_owner_shapes(arrs):
    return [jax.ShapeDtypeStruct((N_DEV,) + a.shape[2:], a.dtype) for a in arrs]


def _to_owner_sems(n):
    return [pltpu.SemaphoreType.DMA((7 * n,)), pltpu.SemaphoreType.DMA((7 * n,)), pltpu.SemaphoreType.DMA((n,))]


def _to_owner(arrs, name):
    n = len(arrs)

    def kern(*refs):
        _start_then_wait(_to_owner_copies(refs[:n], refs[n:2 * n], *refs[2 * n:]))

    return _exchange_call(kern, name, arrs, _to_owner_shapes(arrs), (7 * n, 7 * n, n))


def _join_halves(arrs):
    n = len(arrs)

    def kern(*refs):
        outs, (ssem, rsem) = refs[n:2 * n], refs[2 * n:]
        x, y, c = lax.axis_index("x"), lax.axis_index("y"), lax.axis_index("c")
        _start_then_wait([
            pltpu.make_async_remote_copy(outs[i].at[c], outs[i].at[c], ssem.at[i], rsem.at[i],
                                         device_id=(x, y, 1 - c), device_id_type=MESH) for i in range(n)])

    outs = [jax.ShapeDtypeStruct(a.shape, a.dtype) for a in arrs]
    return _exchange_call(kern, "grad_join_halves", arrs, outs, (n, n), aliases={i: i for i in range(n)})


def _row_block(rows, row_bytes, limit=1 << 20):
    best = None
    for d in range(16, rows + 1, 16):
        if rows % d == 0 and d * row_bytes <= limit:
            best = d
    return best if best is not None else rows


def _sum_devices(b, c, name):
    _, h, cols = b.shape
    hb = _row_block(h, cols * 4)

    def kern(c_ref, b_ref, o_ref):
        acc = b_ref[0].astype(F32)
        for j in range(1, N_DEV):
            acc = acc + b_ref[j].astype(F32)
        o_ref[...] = acc

    return pl.pallas_call(
        kern,
        grid_spec=pltpu.PrefetchScalarGridSpec(
            num_scalar_prefetch=1, grid=(h // hb,),
            in_specs=[pl.BlockSpec((N_DEV, hb, cols), lambda i, c_ref: (0, i, 0))],
            out_specs=pl.BlockSpec((None, hb, cols), lambda i, c_ref: (c_ref[0], i, 0))),
        out_shape=jax.ShapeDtypeStruct((2, h, cols), F32), name=name, compiler_params=_cparams(1))(c, b)


def _adamw(w, g, m, v, name):
    rows, cols = w.shape
    rb = _row_block(rows, cols * 4)
    c1 = 1.0 - ADAM_B1 ** ADAM_STEP
    c2 = 1.0 - ADAM_B2 ** ADAM_STEP

    def kern(w_ref, g_ref, m_ref, v_ref, d_ref, mo_ref, vo_ref):
        gv = g_ref[...]
        mn = ADAM_B1 * m_ref[...] + (1.0 - ADAM_B1) * gv
        vn = ADAM_B2 * v_ref[...] + (1.0 - ADAM_B2) * (gv * gv)
        mo_ref[...] = mn
        vo_ref[...] = vn
        d_ref[...] = (-ADAM_LR) * ((mn / c1) / (jnp.sqrt(vn / c2) + ADAM_EPS) + ADAM_WD * w_ref[...])

    spec = pl.BlockSpec((rb, cols), lambda i: (i, 0))
    return pl.pallas_call(
        kern, grid=(rows // rb,), in_specs=[spec] * 4, out_specs=[spec] * 3,
        out_shape=[jax.ShapeDtypeStruct(w.shape, F32)] * 3, name=name, compiler_params=_cparams(1))(w, g, m, v)


def _pad_rows(flat, rows):
    return jnp.pad(flat, (0, rows * LANES - flat.shape[0])).reshape(rows, LANES)


def _round_up(n, m):
    return (n + m - 1) // m * m


def _shard_shape(shape, axis):
    return tuple(s // N_CHIPS if a == axis else s for a, s in enumerate(shape))


def _to_shards(full, axis):
    shape = full.shape
    t = full.reshape(shape[:axis] + (N_CHIPS, shape[axis] // N_CHIPS) + shape[axis + 1:])
    return jnp.moveaxis(t, axis, 0).reshape(N_CHIPS, -1)


def _from_shards(sh, shape, axis):
    t = sh.reshape((N_CHIPS,) + _shard_shape(shape, axis))
    t = jnp.moveaxis(t, 0, axis)
    return t.reshape(shape)


BIG = tuple((name, shape, axis) for name, shape, axis, big in SHARDED if big)
EARLY_WEIGHTS = ("w_in", "w_uq", "w_ukv")
SMALL_SHARDED = tuple((name, shape, axis) for name, shape, axis, big in SHARDED if not big)


def _pack_small_weights(p):
    flat = jnp.concatenate([p[name].reshape(-1) for name, _, _ in SMALL_SHARDED])
    return _pad_rows(flat, _round_up(-(-flat.shape[0] // LANES), SUBLANES))


def _unpack_small_weights(gathered):
    flat = gathered.reshape(N_CHIPS, -1)
    out, off = {}, 0
    for name, shape, axis in SMALL_SHARDED:
        n = _numel(shape) // N_CHIPS
        out[name] = _from_shards(flat[:, off:off + n], shape, axis)
        off += n
    return out


def _pack_small_local(p, prefix=""):
    parts = [p[prefix + name].reshape(-1) for name, _, _ in SMALL_SHARDED]
    parts += [p[prefix + name].reshape(-1) for name, _ in REPLICATED]
    return jnp.concatenate(parts)


def _pack_small_grads(g):
    parts = [_to_shards(g[name], axis) for name, _, axis in SMALL_SHARDED]
    rep = jnp.concatenate([g[name].reshape(-1) for name, _ in REPLICATED])
    parts.append(jnp.broadcast_to(rep[None], (N_CHIPS, rep.shape[0])))
    return jnp.concatenate(parts, axis=1)


def _unpack_small_local(flat):
    out, off = {}, 0
    for name, shape, axis in SMALL_SHARDED:
        n = _numel(shape) // N_CHIPS
        out[name] = flat[off:off + n].reshape((1,) + _shard_shape(shape, axis))
        off += n
    for name, shape in REPLICATED:
        n = _numel(shape)
        out[name] = flat[off:off + n].reshape((1,) + shape)
        off += n
    return out


def _grad_shards(g, shape, axis):
    if axis == 0:
        return g.reshape((N_CHIPS,) + _shard_shape(shape, axis))
    return jnp.transpose(g.reshape(shape[0], N_CHIPS, shape[1] // N_CHIPS), (1, 0, 2))


def _cols_from_shards(w4):
    return jnp.transpose(w4, (1, 0, 2)).reshape(w4.shape[1], -1)


def _block_diag(w):
    eye = jnp.eye(LRU_BLOCKS, dtype=w.dtype)
    return jnp.einsum("ncd,nm->ncmd", w, eye).reshape(LRU_W, LRU_W)


def _block_diag_t(g):
    g4 = g.reshape(LRU_BLOCKS, 64, LRU_BLOCKS, 64)
    return jnp.stack([g4[n, :, n, :] for n in range(LRU_BLOCKS)])


def _pad8(a):
    return jnp.pad(a, ((0, SUBLANES - a.shape[0]), (0, 0)))


def kernel(x, mem, positions, attn_norm, w_in, lru_conv_w, lru_conv_b, lru_w_a, lru_b_a, lru_w_i, lru_b_i, lru_lambda, q_a_norm, w_uq, kv_a_norm, w_ukv, mla_q_norm, mla_k_norm, lru_out_norm, mla_out_norm, w_out, mem_attn_norm, mem_norm, w_mem_q, w_mem_kv, mem_q_norm, mem_k_norm, w_mem_o, ffn_norm, w_up, ffn_conv_w, ffn_conv_b, w_down, loss_target, m_attn_norm, m_w_in, m_lru_conv_w, m_lru_conv_b, m_lru_w_a, m_lru_b_a, m_lru_w_i, m_lru_b_i, m_lru_lambda, m_q_a_norm, m_w_uq, m_kv_a_norm, m_w_ukv, m_mla_q_norm, m_mla_k_norm, m_lru_out_norm, m_mla_out_norm, m_w_out, m_mem_attn_norm, m_mem_norm, m_w_mem_q, m_w_mem_kv, m_mem_q_norm, m_mem_k_norm, m_w_mem_o, m_ffn_norm, m_w_up, m_ffn_conv_w, m_ffn_conv_b, m_w_down, v_attn_norm, v_w_in, v_lru_conv_w, v_lru_conv_b, v_lru_w_a, v_lru_b_a, v_lru_w_i, v_lru_b_i, v_lru_lambda, v_q_a_norm, v_w_uq, v_kv_a_norm, v_w_ukv, v_mla_q_norm, v_mla_k_norm, v_lru_out_norm, v_mla_out_norm, v_w_out, v_mem_attn_norm, v_mem_norm, v_w_mem_q, v_w_mem_kv, v_mem_q_norm, v_mem_k_norm, v_w_mem_o, v_ffn_norm, v_w_up, v_ffn_conv_w, v_ffn_conv_b, v_w_down):
    given = dict(locals())
    local = {name: given[name][0] for name in WEIGHT_ORDER}
    s = x.shape[1]
    x2d, mem2d, tgt = x[0], mem[0], loss_target[0]
    tm = min(512, s)
    tm_wide = min(1024, s)
    tm_ffn = min(256, s)
    t_scan = min(1024, s)
    tq_f, tq_b, tk = min(4096, s), min(2048, s), min(512, s)

    early = [b for b in BIG if b[0] in EARLY_WEIGHTS]
    late = [b for b in BIG if b[0] not in EARLY_WEIGHTS]
    got = _gather_chips([local[name].astype(BF16) for name, _, _ in early] + [_pack_small_weights(local)])
    full = _unpack_small_weights(got[-1])

    def take_gathered(entries, arrays):
        for (name, shape, axis), w4 in zip(entries, arrays):
            if axis == 0:
                full[name] = w4.reshape(shape)
            elif name in ("w_up", "w_mem_o"):
                full[name] = w4
            else:
                full[name] = _cols_from_shards(w4)

    take_gathered(early, got)
    row = lambda a: a.reshape(1, -1)
    b16 = lambda a: a.astype(BF16)
    zeros = lambda r, c: jnp.zeros((r, c), BF16)
    w_in_f = full["w_in"]
    w_in_p = jnp.concatenate([w_in_f[:, :OFF_KR], _head_tile(zeros(D_MODEL, QK_NOPE), w_in_f[:, OFF_KR:])], axis=1)
    uq = full["w_uq"].reshape(Q_LORA, HEADS, QK_HEAD)
    w_uq_p = _head_tile(uq[:, :, :QK_NOPE], uq[:, :, QK_NOPE:]).reshape(Q_LORA, -1)
    ukv = full["w_ukv"].reshape(KV_LORA, HEADS, QK_NOPE + V_DIM)
    w_uk_p = _head_tile(ukv[:, :, :QK_NOPE], None).reshape(KV_LORA, -1)
    w_uv = ukv[:, :, QK_NOPE:].reshape(KV_LORA, MLA_W)
    wa = [b16(_block_diag(local["lru_w_a"][d])) for d in range(2)]
    wi = [b16(_block_diag(local["lru_w_i"][d])) for d in range(2)]
    cw = [_pad8(full["lru_conv_w"][d]) for d in range(2)]
    pv = [_pad8(jnp.stack([full["lru_conv_b"][d], full["lru_b_a"][d], full["lru_b_i"][d], full["lru_lambda"][d]]))
          for d in range(2)]
    ffn_cw = _pad8(jnp.concatenate([full["ffn_conv_w"], row(local["ffn_conv_b"])], axis=0))
    g_attn, g_qa, g_kva = row(local["attn_norm"]), row(local["q_a_norm"]), row(local["kv_a_norm"])
    g_qn = _head_tile(row(local["mla_q_norm"])[:, :QK_NOPE], row(local["mla_q_norm"])[:, QK_NOPE:])
    g_kn = _head_tile(row(local["mla_k_norm"])[:, :QK_NOPE], row(local["mla_k_norm"])[:, QK_NOPE:])
    g_lru, g_mla = row(local["lru_out_norm"]), row(local["mla_out_norm"])
    g_memattn, g_mem = row(local["mem_attn_norm"]), row(local["mem_norm"])
    g_mq, g_mk, g_ffn = row(local["mem_q_norm"]), row(local["mem_k_norm"]), row(local["ffn_norm"])

    inv = ROPE_THETA ** (-jnp.arange(0, QK_ROPE, 2, dtype=F32) / QK_ROPE)
    ang = positions[0].astype(F32)[:, None] * inv
    cosv, sinv = jnp.cos(ang), jnp.sin(ang)
    cos_t = _head_tile(jnp.ones((s, QK_NOPE), F32), jnp.concatenate([cosv, cosv], axis=1))
    sin_t = _head_tile(jnp.zeros((s, QK_NOPE), F32), jnp.concatenate([-sinv, sinv], axis=1))

    xr, yg, cq, ckv, krp, hb_in = _in_proj(x2d, g_attn, w_in_p, tm_wide)
    h_f, *saved_f = _lru_scan_fwd(xr, cw[0], pv[0], wa[0], wi[0], False, t_scan)
    h_b, *saved_b = _lru_scan_fwd(xr, cw[1], pv[1], wa[1], wi[1], True, t_scan)
    q, k, v = _mla_qkv(cq, ckv, krp, cos_t, sin_t, g_qa, g_kva, g_qn, g_kn, w_uq_p, w_uk_p, w_uv, tm_wide)
    o, lse, *got = _attn_fwd(q, k, v, tq_f, tk, shards=[local[name].astype(BF16) for name, _, _ in late])
    take_gathered(late, got)
    x1, mixed = _mix_out(h_f, h_b, yg, o, x2d, g_lru, g_mla, full["w_out"], tm_wide)
    km, vm = _mem_kv(mem2d, g_mem, full["w_mem_kv"], g_mk)
    x2, o_mem = _mem_attn(x1, g_memattn, full["w_mem_q"], g_mq, km, vm, full["w_mem_o"], tm_wide)
    gu_pre, hb_ffn = _ffn_up(x2, g_ffn, full["w_up"], tm)
    dy, dyb, act, dgu, loss_acc = _ffn_down_loss(gu_pre, x2, tgt, ffn_cw, full["w_down"], tm_ffn)
    loss = lax.psum(loss_acc[0, 0] * (0.5 / D_MODEL), ("x", "y", "c"))

    grads = {}
    grads["w_down"] = _matmul_tn(act, dyb, "grad_w_down", out_dtype=BF16)
    dpre, g_conv = _ffn_bwd_conv(dgu, gu_pre, ffn_cw, tm_ffn)
    grads["ffn_conv_w"], grads["ffn_conv_b"] = g_conv[:3], g_conv[3]
    grads["w_up"] = _matmul_tn(hb_ffn, dpre, "grad_w_up", col_shards=True, out_dtype=BF16)
    dx2, dx2b, gg = _ffn_bwd_in(dpre, x2, dy, g_ffn, full["w_up"], tm)
    grads["ffn_norm"] = gg[0]
    grads["w_mem_o"] = _matmul_tn(o_mem, dx2b, "grad_w_mem_o", out_dtype=BF16)
    dx1, dx1b, hm, dqr_mem, dkm, dvm, gg, ggq = _mem_attn_bwd(x1, dx2, dx2b, g_memattn, full["w_mem_q"], g_mq, km, vm,
                                                                 full["w_mem_o"], tm)
    grads["mem_attn_norm"], grads["mem_q_norm"] = gg[0], ggq[0]
    grads["w_mem_q"] = _matmul_tn(hm, dqr_mem, "grad_w_mem_q", out_dtype=BF16)
    g_mem_kv, gg, ggk, _ = _mem_kv_bwd(mem2d, g_mem, full["w_mem_kv"], g_mk, dkm, dvm)
    grads["w_mem_kv"] = g_mem_kv.astype(BF16)
    grads["mem_norm"], grads["mem_k_norm"] = gg[0], ggk[0]
    grads["w_out"] = _matmul_tn(mixed, dx1b, "grad_w_out", out_dtype=BF16)
    dh, dyg, dob, dl128, ggl, ggm = _mix_out_bwd(dx1b, h_f, h_b, yg, o, g_lru, g_mla, full["w_out"], tm)
    grads["lru_out_norm"], grads["mla_out_norm"] = ggl[0], ggm[0]
    delta_t = jnp.transpose(dl128[:, :HEADS]).reshape(HEADS // 2, 2, s)
    def halves(name, shape, axis):
        g4 = grads[name] if grads[name].ndim == 3 else _grad_shards(grads[name], shape, axis)
        return g4.reshape(N_CHIPS, 2, g4.shape[1] // 2, g4.shape[2])

    dq, dk, dv, *arrived_late = _attn_bwd(q, k, v, dob, lse, delta_t, tq_b, tk,
                                          contributions=[halves(*e) for e in late])
    (dcq, dckv, dkrp, cqb, dqr, ckvb, dkn, dvb, ggqa, ggkva, ggqn, ggkn) = _mla_qkv_bwd(
        cq, ckv, krp, cos_t, sin_t, dq, dk, dv, g_qa, g_kva, g_qn, g_kn, w_uq_p, w_uk_p, w_uv, tm_wide)
    grads["q_a_norm"], grads["kv_a_norm"] = ggqa[0], ggkva[0]
    grads["mla_q_norm"] = jnp.concatenate(_from_head_tile(ggqn[0]))
    grads["mla_k_norm"] = jnp.concatenate(_from_head_tile(ggkn[0]))
    g_uq_p = _matmul_tn(cqb, dqr, "grad_w_uq")
    grads["w_uq"] = jnp.concatenate(_from_head_tile(g_uq_p.reshape(Q_LORA, HEADS, LANES)), axis=-1).reshape(Q_LORA, -1)
    g_uk_p = _from_head_tile(_matmul_tn(ckvb, dkn, "grad_w_uk").reshape(KV_LORA, HEADS, LANES))[0]
    g_uv = _matmul_tn(ckvb, dvb, "grad_w_uv").reshape(KV_LORA, HEADS, V_DIM)
    grads["w_ukv"] = jnp.concatenate([g_uk_p, g_uv], axis=2).reshape(KV_LORA, -1)
    dxr, gwa, gwi, gvec = [], [], [], []
    for d, (hd, saved) in enumerate(((h_f, saved_f), (h_b, saved_b))):
        r = _lru_scan_bwd(xr, saved, hd, dh, cw[d], pv[d], wa[d], wi[d], d == 1, t_scan)
        dxr.append(r[0])
        gwa.append(_block_diag_t(r[1]))
        gwi.append(_block_diag_t(r[2]))
        gvec.append(r[3])
    grads["lru_w_a"], grads["lru_w_i"] = jnp.stack(gwa), jnp.stack(gwi)
    grads["lru_conv_w"] = jnp.stack([gv[:CONV_W] for gv in gvec])
    for r_i, name in ((4, "lru_conv_b"), (5, "lru_b_a"), (6, "lru_b_i"), (7, "lru_lambda")):
        grads[name] = jnp.stack([gv[r_i] for gv in gvec])
    grad_x, dproj, gg = _in_proj_bwd(x2d, dx1, dxr[0], dxr[1], dyg, dcq, dckv, dkrp, g_attn, w_in_p, tm)
    grads["attn_norm"] = gg[0]
    g_in_p = _matmul_tn(hb_in, dproj, "grad_w_in")
    grads["w_in"] = jnp.concatenate([g_in_p[:, :OFF_KR], _from_head_tile(g_in_p[:, OFF_KR:])[1]], axis=1)

    small = _pack_small_grads(grads)
    length = small.shape[1]
    hrows = _round_up(-(-length // (2 * LANES)), 16)
    small = jnp.pad(small, ((0, 0), (0, 2 * hrows * LANES - length))).reshape(N_CHIPS, 2, hrows, LANES)
    for name, _, _ in early:
        grads[name] = grads[name].astype(BF16)
    arrived_early = _to_owner([halves(*e) for e in early] + [small], "grad_to_owner")
    names = [name for name, _, _ in late + early] + ["small"]
    c_idx = lax.axis_index("c").astype(jnp.int32).reshape(1)
    reduced = _join_halves([_sum_devices(b, c_idx, "grad_sum_" + n)
                            for n, b in zip(names, list(arrived_late) + list(arrived_early))])

    outs = [{}, {}, {}, {}]
    for (name, shape, axis), r in zip(late + early, reduced):
        g2 = r.reshape(_shard_shape(shape, axis))
        res = _adamw(local[name], g2, given["m_" + name][0], given["v_" + name][0], "adamw_" + name)
        for o_, a in zip(outs, (g2, *res)):
            o_[name] = a[None]
    pack = lambda prefix: _pad_rows(_pack_small_local({n: given[prefix + n] for n in WEIGHT_ORDER}), 2 * hrows)
    g_small = reduced[-1].reshape(2 * hrows, LANES)
    res = _adamw(pack(""), g_small, pack("m_"), pack("v_"), "adamw_small")
    for o_, a in zip(outs, (g_small, *res)):
        o_.update(_unpack_small_local(a.reshape(-1)))
    return (loss, grad_x[None], *[o_[n] for o_ in outs for n in WEIGHT_ORDER])
```

```python
import jax
import jax.numpy as jnp
from jax import lax
from jax.experimental import pallas as pl
from jax.experimental.pallas import tpu as pltpu

F32, BF16 = jnp.float32, jnp.bfloat16
MESH = pl.DeviceIdType.MESH

D_MODEL = 1024
EPS = 1e-6
LRU_W = 512
LRU_BLOCKS = 8
LRU_C = 8.0
CONV_W = 4
HEADS = 8
QK_NOPE, QK_ROPE, QK_HEAD, V_DIM = 64, 32, 96, 64
Q_LORA, KV_LORA = 256, 128
MLA_W = HEADS * V_DIM
ROPE_THETA = 10000.0
IN_COLS = 2 * LRU_W + Q_LORA + KV_LORA + QK_ROPE
OFF_KR = IN_COLS - QK_ROPE
IN_PAD = 1536
MEM_HEADS, MEM_HD = 4, 128
MEM_W = MEM_HEADS * MEM_HD
D_FF = 2816
N_CHIPS = 4
ADAM_LR, ADAM_B1, ADAM_B2, ADAM_EPS, ADAM_WD, ADAM_STEP = 0.001, 0.9, 0.999, 1e-08, 0.01, 10

LANES = 128
SUBLANES = 8
V7X_VMEM_BYTES = 64 * 1024 * 1024
VMEM_LIMIT = V7X_VMEM_BYTES * 7 // 8

SHARDED = (
    ("w_in", (D_MODEL, IN_COLS), 1, True),
    ("lru_conv_w", (2, CONV_W, LRU_W), 2, False),
    ("lru_conv_b", (2, LRU_W), 1, False),
    ("lru_b_a", (2, LRU_W), 1, False),
    ("lru_b_i", (2, LRU_W), 1, False),
    ("lru_lambda", (2, LRU_W), 1, False),
    ("w_uq", (Q_LORA, HEADS * QK_HEAD), 1, True),
    ("w_ukv", (KV_LORA, HEADS * (QK_NOPE + V_DIM)), 1, True),
    ("w_out", (2 * LRU_W, D_MODEL), 0, True),
    ("w_mem_q", (D_MODEL, MEM_W), 0, True),
    ("w_mem_kv", (D_MODEL, 2 * MEM_W), 0, True),
    ("w_mem_o", (MEM_W, D_MODEL), 1, True),
    ("w_up", (D_MODEL, 2 * D_FF), 1, True),
    ("ffn_conv_w", (3, 2 * D_FF), 1, False),
    ("w_down", (D_FF, D_MODEL), 0, True),
)
REPLICATED = (
    ("attn_norm", (D_MODEL,)), ("lru_w_a", (2, LRU_BLOCKS, 64, 64)), ("lru_w_i", (2, LRU_BLOCKS, 64, 64)),
    ("q_a_norm", (Q_LORA,)), ("kv_a_norm", (KV_LORA,)), ("mla_q_norm", (QK_HEAD,)), ("mla_k_norm", (QK_HEAD,)),
    ("lru_out_norm", (LRU_W,)), ("mla_out_norm", (MLA_W,)), ("mem_attn_norm", (D_MODEL,)), ("mem_norm", (D_MODEL,)),
    ("mem_q_norm", (MEM_HD,)), ("mem_k_norm", (MEM_HD,)), ("ffn_norm", (D_MODEL,)), ("ffn_conv_b", (2 * D_FF,)),
)
WEIGHT_ORDER = ('attn_norm', 'w_in', 'lru_conv_w', 'lru_conv_b', 'lru_w_a', 'lru_b_a', 'lru_w_i', 'lru_b_i', 'lru_lambda',
                'q_a_norm', 'w_uq', 'kv_a_norm', 'w_ukv', 'mla_q_norm', 'mla_k_norm', 'lru_out_norm', 'mla_out_norm', 'w_out',
                'mem_attn_norm', 'mem_norm', 'w_mem_q', 'w_mem_kv', 'mem_q_norm', 'mem_k_norm', 'w_mem_o', 'ffn_norm', 'w_up',
                'ffn_conv_w', 'ffn_conv_b', 'w_down')


def _numel(shape):
    n = 1
    for s in shape:
        n *= s
    return n


def _cparams(n_axes):
    return pltpu.CompilerParams(dimension_semantics=("arbitrary",) * n_axes, vmem_limit_bytes=VMEM_LIMIT)


def _bdot(a, b):
    return jnp.dot(a.astype(BF16), b.astype(BF16), preferred_element_type=F32)


def _bdot_nt(a, b):
    return lax.dot_general(a.astype(BF16), b.astype(BF16), (((1,), (1,)), ((), ())), preferred_element_type=F32)


def _bdot_tn(a, b):
    return lax.dot_general(a.astype(BF16), b.astype(BF16), (((0,), (0,)), ((), ())), preferred_element_type=F32)


def _rstd(x, n=None):
    n = x.shape[-1] if n is None else n
    return lax.rsqrt(jnp.sum(x * x, axis=-1, keepdims=True) * (1.0 / n) + EPS)


def _norm_bwd(x, rs, g, dy, n=None):
    n = x.shape[-1] if n is None else n
    xhat = x * rs
    dxh = dy * g
    dx = rs * (dxh - xhat * (jnp.sum(dxh * xhat, axis=-1, keepdims=True) * (1.0 / n)))
    return dx, dy * xhat


def _acc_row(ref, r, val):
    ref[r:r + 1, :] += jnp.sum(val, axis=0, keepdims=True)


def _zero_first(i, *refs):
    @pl.when(i == 0)
    def _():
        for r in refs:
            r[...] = jnp.zeros_like(r)


def _shift_down(x, j, halo):
    if j == 0:
        return x
    xs = pltpu.roll(x, j, 0)
    hs = pltpu.roll(halo, j, 0)
    row = lax.broadcasted_iota(jnp.int32, hs.shape, 0)
    top = jnp.where(row < j, hs, xs[:SUBLANES])
    return jnp.concatenate([top, xs[SUBLANES:]], axis=0)


def _shift_up(x, j, halo):
    if j == 0:
        return x
    t = x.shape[0]
    xs = pltpu.roll(x, t - j, 0)
    hs = pltpu.roll(halo, SUBLANES - j, 0)
    row = lax.broadcasted_iota(jnp.int32, hs.shape, 0)
    bot = jnp.where(row >= SUBLANES - j, hs, xs[t - SUBLANES:])
    return jnp.concatenate([xs[:t - SUBLANES], bot], axis=0)


def _shift(x, j, halo, down):
    return _shift_down(x, j, halo) if down else _shift_up(x, j, halo)


def _scan(a, b, h_in, down):
    t, c = a.shape
    g = t // SUBLANES
    a3, b3 = a.reshape(g, SUBLANES, c), b.reshape(g, SUBLANES, c)
    sub = lax.broadcasted_iota(jnp.int32, a3.shape, 1)
    d = 1
    while d < SUBLANES:
        keep = (sub >= d) if down else (sub < SUBLANES - d)
        shift = d if down else SUBLANES - d
        a_s = jnp.where(keep, pltpu.roll(a3, shift, 1), 1.0)
        b_s = jnp.where(keep, pltpu.roll(b3, shift, 1), 0.0)
        b3 = a3 * b_s + b3
        a3 = a3 * a_s
        d *= 2
    hs = [None] * g
    carry = h_in
    for i in (range(g) if down else range(g - 1, -1, -1)):
        hs[i] = a3[i] * carry + b3[i]
        carry = hs[i][SUBLANES - 1:, :] if down else hs[i][:1, :]
    return jnp.concatenate(hs, axis=0)


def _sigmoid(x):
    return 0.5 * jnp.tanh(0.5 * x) + 0.5


LOG2E = 1.4426950408889634
GELU_K = 0.7978845608028654
GELU_C = 0.044715


def _gelu(x):
    return 0.5 * x * (1.0 + jnp.tanh(GELU_K * (x + GELU_C * x * x * x)))


def _gelu_grad(x):
    t = jnp.tanh(GELU_K * (x + GELU_C * x * x * x))
    return 0.5 * (1.0 + t) + 0.5 * x * (1.0 - t * t) * GELU_K * (1.0 + 3.0 * GELU_C * x * x)


ROPE_HALF = QK_ROPE // 2
ROPE_LANE = 32


def _head_tile(nope, rope):
    z = lambda n: jnp.zeros(nope.shape[:-1] + (n,), nope.dtype)
    r1, r2 = (z(ROPE_HALF), z(ROPE_HALF)) if rope is None else (rope[..., :ROPE_HALF], rope[..., ROPE_HALF:])
    return jnp.concatenate([nope[..., :ROPE_LANE], r1, nope[..., ROPE_LANE:], z(ROPE_HALF), r2, z(ROPE_HALF)], axis=-1)


def _from_head_tile(t):
    a, b = ROPE_LANE + ROPE_HALF, ROPE_LANE + LANES // 2
    return (jnp.concatenate([t[..., :ROPE_LANE], t[..., a:a + QK_NOPE - ROPE_LANE]], axis=-1),
            jnp.concatenate([t[..., ROPE_LANE:a], t[..., b:b + ROPE_HALF]], axis=-1))


def _rope_partner(x):
    lane = lax.broadcasted_iota(jnp.int32, x.shape, 1) & (LANES // 2 - 1)
    return jnp.where((lane >= ROPE_LANE) & (lane < ROPE_LANE + ROPE_HALF), pltpu.roll(x, LANES // 2, 1), 0.0)


def _rope(x, cos_t, sin_t):
    return x * cos_t + _rope_partner(x) * sin_t


def _rope_t(dy, cos_t, sin_t):
    return dy * cos_t + _rope_partner(dy * sin_t)


def _rowwise(body, name, s, tm, rows=(), halos=(), fulls=(), outs=(), accs=()):
    n = s // tm
    hb = tm // SUBLANES
    last8 = s // SUBLANES - 1
    in_specs, args = [], []
    for a in rows:
        in_specs.append(pl.BlockSpec((tm, a.shape[1]), lambda i: (i, 0)))
        args.append(a)
    for a in halos:
        in_specs.append(pl.BlockSpec((SUBLANES, a.shape[1]), lambda i: (jnp.maximum(i * hb - 1, 0), 0)))
        in_specs.append(pl.BlockSpec((SUBLANES, a.shape[1]), lambda i: (jnp.minimum((i + 1) * hb, last8), 0)))
        args += [a, a]
    for a in fulls:
        in_specs.append(pl.BlockSpec(a.shape, lambda i, nd=a.ndim: (0,) * nd))
        args.append(a)
    out_shape, out_specs = [], []
    for c, dt in outs:
        out_shape.append(jax.ShapeDtypeStruct((s, c), dt))
        out_specs.append(pl.BlockSpec((tm, c), lambda i: (i, 0)))
    for shp, dt in accs:
        out_shape.append(jax.ShapeDtypeStruct(shp, dt))
        out_specs.append(pl.BlockSpec(shp, lambda i, nd=len(shp): (0,) * nd))

    def kern(*refs):
        body(pl.program_id(0), n, *refs)

    return pl.pallas_call(kern, grid=(n,), in_specs=in_specs, out_specs=out_specs, out_shape=out_shape, name=name,
                          compiler_params=_cparams(1))(*args)


def _matmul_tn(a, b, name, col_shards=False, out_dtype=F32):
    t, m = a.shape
    n = b.shape[1]
    bm = m
    for cand in range(LANES, m + 1, LANES):
        if m % cand == 0 and cand * (n // N_CHIPS if col_shards else min(n, 2048)) * 4 <= 6 * 1024 * 1024:
            bm = cand
    bn = n // N_CHIPS if col_shards else (n if n <= 2048 else 1408)
    bt = min(t, 2048 if max(bm, bn) <= 512 else (1024 if max(bm, bn) <= 1024 else 512))
    nt = t // bt

    def kern(a_ref, b_ref, o_ref, acc_ref):
        k = pl.program_id(2)

        @pl.when(k == 0)
        def _():
            acc_ref[...] = jnp.zeros_like(acc_ref)
        acc_ref[...] += _bdot_tn(a_ref[...], b_ref[...])

        @pl.when(k == nt - 1)
        def _():
            o_ref[...] = acc_ref[...].astype(out_dtype)

    if col_shards:
        out_spec = pl.BlockSpec((None, bm, bn), lambda i, j, k: (j, i, 0))
        out_shape = jax.ShapeDtypeStruct((N_CHIPS, m, bn), out_dtype)
    else:
        out_spec = pl.BlockSpec((bm, bn), lambda i, j, k: (i, j))
        out_shape = jax.ShapeDtypeStruct((m, n), out_dtype)
    return pl.pallas_call(
        kern, grid=(m // bm, n // bn, nt),
        in_specs=[pl.BlockSpec((bt, bm), lambda i, j, k: (k, i)), pl.BlockSpec((bt, bn), lambda i, j, k: (k, j))],
        out_specs=out_spec, out_shape=out_shape, scratch_shapes=[pltpu.VMEM((bm, bn), F32)], name=name,
        compiler_params=_cparams(3))(a, b)


def _in_proj(x, g, w_in_p, tm):
    def body(i, n, x_ref, g_ref, w_ref, xr, yg, cq, ckv, krp, hb):
        xv = x_ref[...]
        h = (xv * _rstd(xv) * g_ref[...]).astype(BF16)
        hb[...] = h
        p = jnp.dot(h, w_ref[...], preferred_element_type=F32)
        xr[...] = p[:, :LRU_W]
        yg[...] = p[:, LRU_W:2 * LRU_W]
        cq[...] = p[:, 2 * LRU_W:2 * LRU_W + Q_LORA]
        ckv[...] = p[:, 2 * LRU_W + Q_LORA:OFF_KR]
        krp[...] = p[:, OFF_KR:IN_PAD]

    return _rowwise(body, "in_proj", x.shape[0], tm, rows=[x], fulls=[g, w_in_p],
                    outs=[(LRU_W, F32), (LRU_W, F32), (Q_LORA, F32), (KV_LORA, F32), (LANES, F32), (D_MODEL, BF16)])


def _softplus_neg(lam):
    e = jnp.exp(-jnp.abs(lam))
    return jnp.maximum(-lam, 0.0) + jnp.where(e < 1e-2, e * (1.0 - e * (0.5 - e * (1.0 / 3.0))), jnp.log(1.0 + e))


def _lru_gates(x, halo, cw_ref, pv_ref, wa_ref, wi_ref, rev):
    down = not rev
    xc = pv_ref[0:1, :] + jnp.zeros_like(x)
    for j in range(CONV_W):
        k = j if rev else CONV_W - 1 - j
        xc = xc + cw_ref[k:k + 1, :] * _shift(x, j, halo, down)
    r = _sigmoid(_bdot(xc, wa_ref[...]) + pv_ref[1:2, :])
    ig = _sigmoid(_bdot(xc, wi_ref[...]) + pv_ref[2:3, :])
    lam = pv_ref[3:4, :]
    sp = _softplus_neg(lam)
    log_a = (-LRU_C) * r * sp
    a = jnp.exp(log_a)
    z = 2.0 * log_a
    series = -(z * (1.0 + z * (0.5 + z * (1.0 / 6.0 + z * (1.0 / 24.0)))))
    om = jnp.where(z > -0.02, series, 1.0 - a * a)
    mult = jnp.sqrt(om)
    return xc, r, ig, sp, a, mult


def _lru_scan_fwd(xr, cw, pv, wa, wi, rev, t):
    s = xr.shape[0]
    n = s // t
    hb = t // SUBLANES
    last8 = s // SUBLANES - 1
    down = not rev

    def kern(x_ref, halo_ref, cw_ref, pv_ref, wa_ref, wi_ref, h_ref, xc_ref, r_ref, ig_ref, a_ref, mult_ref, carry_ref):
        i = pl.program_id(0)
        _zero_first(i, carry_ref)
        halo = jnp.where(i == 0, 0.0, halo_ref[...])
        xc, r, ig, sp, a, mult = _lru_gates(x_ref[...], halo, cw_ref, pv_ref, wa_ref, wi_ref, rev)
        xc_ref[...], r_ref[...], ig_ref[...], a_ref[...], mult_ref[...] = xc, r, ig, a, mult
        h_ref[...] = _scan(a, mult * ig * xc, carry_ref[...], down)
        carry_ref[...] = h_ref[pl.ds(t - 1 if down else 0, 1), :]

    if rev:
        blk = lambda i: (n - 1 - i, 0)
        hal = lambda i: (jnp.minimum((n - i) * hb, last8), 0)
    else:
        blk = lambda i: (i, 0)
        hal = lambda i: (jnp.maximum(i * hb - 1, 0), 0)
    full = lambda a: pl.BlockSpec(a.shape, lambda i: (0, 0))
    return pl.pallas_call(
        kern, grid=(n,),
        in_specs=[pl.BlockSpec((t, LRU_W), blk), pl.BlockSpec((SUBLANES, LRU_W), hal), full(cw), full(pv), full(wa), full(wi)],
        out_specs=[pl.BlockSpec((t, LRU_W), blk)] * 6, out_shape=[jax.ShapeDtypeStruct((s, LRU_W), F32)] * 6,
        scratch_shapes=[pltpu.VMEM((1, LRU_W), F32)], name="lru_scan_rev" if rev else "lru_scan_fwd",
        compiler_params=_cparams(1))(xr, xr, cw, pv, wa, wi)


def _lru_scan_bwd(xr, saved, h, dh, cw, pv, wa, wi, rev, t):
    s = xr.shape[0]
    n = s // t
    hb = t // SUBLANES
    last8 = s // SUBLANES - 1
    down = not rev

    def kern(x_ref, xc_ref, r_ref, ig_ref, a_ref, mult_ref, h_ref, hh_ref, dh_ref, cw_ref, pv_ref, wa_ref, wi_ref,
             dx_ref, gwa_ref, gwi_ref, gv_ref, p_ref, dxc_halo_ref, tmp_ref):
        i = pl.program_id(0)
        _zero_first(i, gwa_ref, gwi_ref, gv_ref, p_ref, dxc_halo_ref)
        at_start = i == n - 1
        x = x_ref[...]
        hhalo = jnp.where(at_start, 0.0, hh_ref[...])
        xc, r, ig, a, mult = xc_ref[...], r_ref[...], ig_ref[...], a_ref[...], mult_ref[...]
        lam = pv_ref[3:4, :]
        sp = _softplus_neg(lam)
        h_prev = _shift(h_ref[...], 1, hhalo, down)
        row = lax.broadcasted_iota(jnp.int32, x.shape, 0)
        edge = t - 1 if down else 0
        dh_mod = dh_ref[...] + jnp.where(row == edge, p_ref[...], 0.0)
        a_next = _shift(a, 1, jnp.zeros((SUBLANES, LRU_W), F32), not down)
        g = _scan(a_next, dh_mod, jnp.zeros((1, LRU_W), F32), not down)
        tmp_ref[...] = a * g
        p_ref[...] = tmp_ref[pl.ds(0 if down else t - 1, 1), :]
        da = g * h_prev
        d_ig = g * mult * xc
        d_xc = g * mult * ig
        d_om = g * ig * xc * (0.5 / jnp.maximum(mult, 1e-30))
        d_log_a = da * a - 2.0 * d_om * a * a
        d_r = d_log_a * ((-LRU_C) * sp)
        d_sp = jnp.sum(d_log_a * ((-LRU_C) * r), axis=0, keepdims=True)
        gv_ref[7:8, :] += d_sp * (-_sigmoid(-lam))
        d_ga = d_r * r * (1.0 - r)
        d_gi = d_ig * ig * (1.0 - ig)
        _acc_row(gv_ref, 5, d_ga)
        _acc_row(gv_ref, 6, d_gi)
        d_xc = d_xc + _bdot_nt(d_ga, wa_ref[...]) + _bdot_nt(d_gi, wi_ref[...])
        gwa_ref[...] += _bdot_tn(xc, d_ga)
        gwi_ref[...] += _bdot_tn(xc, d_gi)
        _acc_row(gv_ref, 4, d_xc)
        dx = jnp.zeros_like(x)
        dxc_halo = dxc_halo_ref[...]
        for j in range(CONV_W):
            k = j if rev else CONV_W - 1 - j
            d_shift = _shift(d_xc, j, dxc_halo, not down)
            _acc_row(gv_ref, k, d_shift * x)
            dx = dx + cw_ref[k:k + 1, :] * d_shift
        dx_ref[...] = dx
        dxc_halo_ref[...] = d_xc[:SUBLANES] if down else d_xc[t - SUBLANES:]

    if rev:
        blk = lambda i: (i, 0)
        hal = lambda i: (jnp.minimum((i + 1) * hb, last8), 0)
    else:
        blk = lambda i: (n - 1 - i, 0)
        hal = lambda i: (jnp.maximum((n - 1 - i) * hb - 1, 0), 0)
    full = lambda a: pl.BlockSpec(a.shape, lambda i: (0, 0))
    bs = pl.BlockSpec((t, LRU_W), blk)
    hs = pl.BlockSpec((SUBLANES, LRU_W), hal)
    return pl.pallas_call(
        kern, grid=(n,),
        in_specs=[bs] * 7 + [hs, bs, full(cw), full(pv), full(wa), full(wi)],
        out_specs=[bs, pl.BlockSpec((LRU_W, LRU_W), lambda i: (0, 0)), pl.BlockSpec((LRU_W, LRU_W), lambda i: (0, 0)),
                   pl.BlockSpec((SUBLANES, LRU_W), lambda i: (0, 0))],
        out_shape=[jax.ShapeDtypeStruct((s, LRU_W), F32), jax.ShapeDtypeStruct((LRU_W, LRU_W), F32),
                   jax.ShapeDtypeStruct((LRU_W, LRU_W), F32), jax.ShapeDtypeStruct((SUBLANES, LRU_W), F32)],
        scratch_shapes=[pltpu.VMEM((1, LRU_W), F32), pltpu.VMEM((SUBLANES, LRU_W), F32), pltpu.VMEM((t, LRU_W), F32)],
        name="lru_bwd_rev" if rev else "lru_bwd_fwd", compiler_params=_cparams(1))(xr, *saved, h, h, dh, cw, pv, wa, wi)


def _mla_qkv(cq, ckv, krp, cos_t, sin_t, g_qa, g_kva, g_qn, g_kn, w_uq_p, w_uk_p, w_uv, tm):
    scale = QK_HEAD ** -0.5 * LOG2E

    def body(i, n, cq_ref, ckv_ref, kr_ref, c_ref, s_ref, gqa, gkva, gqn, gkn, wq, wk, wv, q_out, k_out, v_out):
        cosv, sinv = c_ref[...], s_ref[...]
        cqv = cq_ref[...]
        qr = _bdot(cqv * _rstd(cqv) * gqa[...], wq[...])
        ckvv = ckv_ref[...]
        c_kv = (ckvv * _rstd(ckvv) * gkva[...]).astype(BF16)
        kn = jnp.dot(c_kv, wk[...], preferred_element_type=F32)
        v_out[...] = jnp.dot(c_kv, wv[...], preferred_element_type=F32).astype(BF16)
        kr = kr_ref[...]
        kr_swapped = _rope_partner(kr * gkn[...]) * sinv
        for h in range(HEADS):
            sl = slice(h * LANES, (h + 1) * LANES)
            qh = qr[:, sl]
            qh = _rope(qh * _rstd(qh, QK_HEAD) * gqn[...], cosv, sinv) * scale
            q_out[:, sl] = qh.astype(BF16)
            kh = kn[:, sl] + kr
            rs = _rstd(kh, QK_HEAD)
            k_out[:, sl] = (kh * rs * gkn[...] * cosv + kr_swapped * rs).astype(BF16)

    return _rowwise(body, "mla_qkv", cq.shape[0], tm, rows=[cq, ckv, krp, cos_t, sin_t],
                    fulls=[g_qa, g_kva, g_qn, g_kn, w_uq_p, w_uk_p, w_uv],
                    outs=[(HEADS * LANES, BF16), (HEADS * LANES, BF16), (MLA_W, BF16)])


NT_DIMS = (((1,), (1,)), ((), ()))
TN_DIMS = (((0,), (0,)), ((), ()))


def _riding_exchange(copies_fn, first, last):
    @pl.when(first)
    def _():
        for cp in copies_fn():
            cp.start()

    def finish():
        @pl.when(last)
        def _():
            for cp in copies_fn():
                cp.wait()
    return finish


def _attn_fwd(q, k, v, tq, tk, shards=()):
    s = q.shape[0]
    nq, nk = s // tq, s // tk
    n = len(shards)

    def kern(*refs):
        q_ref, k_ref, v_ref = refs[:3]
        o_ref, lse_ref = refs[3 + n:5 + n]
        acc_ref = refs[5 + 2 * n]
        p_id, i_id = pl.program_id(0), pl.program_id(1)
        finish = _riding_exchange(lambda: _gather_copies(refs[3:3 + n], refs[5 + n:5 + 2 * n], *refs[6 + 2 * n:]),
                                  (p_id == 0) & (i_id == 0), (p_id == HEADS // 2 - 1) & (i_id == nq - 1)) if n else None
        qs = (q_ref[:, :LANES], q_ref[:, LANES:])
        acc_ref[...] = jnp.zeros_like(acc_ref)

        def step(j, carry):
            off = pl.multiple_of(j * tk, tk)
            vc = v_ref[pl.ds(off, tk), :]
            out = []
            for h in range(2):
                m, l = carry[2 * h:2 * h + 2]
                st = lax.dot_general(k_ref[pl.ds(off, tk), h * LANES:(h + 1) * LANES], qs[h], NT_DIMS,
                                     preferred_element_type=F32)
                mn = jnp.maximum(m, jnp.max(st, axis=0, keepdims=True))
                al = jnp.exp2(m - mn)
                pt = jnp.exp2(st - mn)
                l = al * l + jnp.sum(pt, axis=0, keepdims=True)
                acc_ref[h] = al * acc_ref[h] + lax.dot_general(vc, pt.astype(BF16), TN_DIMS, preferred_element_type=F32)
                out += [mn, l]
            return tuple(out)

        init = (jnp.full((1, tq), -1e30, F32), jnp.zeros((1, tq), F32)) * 2
        m0, l0, m1, l1 = lax.fori_loop(0, nk, step, init)
        row = lax.broadcasted_iota(jnp.int32, (LANES, tq), 0)
        o_ref[...] = jnp.where(row < V_DIM, acc_ref[0] / l0, acc_ref[1] / l1).T
        lse_ref[0, 0:1, :] = m0 + jnp.log2(l0)
        lse_ref[0, 1:2, :] = m1 + jnp.log2(l1)
        if n:
            finish()

    return pl.pallas_call(
        kern, grid=(HEADS // 2, nq),
        in_specs=[pl.BlockSpec((tq, 2 * LANES), lambda p, i: (i, p)), pl.BlockSpec((s, 2 * LANES), lambda p, i: (0, p)),
                  pl.BlockSpec((s, LANES), lambda p, i: (0, p))] + [ANY] * n,
        out_specs=[pl.BlockSpec((tq, LANES), lambda p, i: (i, p)), pl.BlockSpec((1, 2, tq), lambda p, i: (p, 0, i))]
        + [ANY] * n,
        out_shape=[jax.ShapeDtypeStruct((s, MLA_W), F32), jax.ShapeDtypeStruct((HEADS // 2, 2, s), F32)]
        + _gather_shapes(shards),
        scratch_shapes=[pltpu.VMEM((2, LANES, tq), F32)] + (_gather_sems(n) if n else []),
        name="attn_fwd", compiler_params=_cparams(2))(q, k, v, *shards)


def _attn_bwd(q, k, v, do, lse, delta, tq, tk, contributions=()):
    s = q.shape[0]
    nq, nk = s // tq, s // tk
    n = len(contributions)

    def kern(*refs):
        q_ref, do_ref, lse_ref, dl_ref, k_ref, v_ref = refs[:6]
        dq_ref, dk_ref, dv_ref = refs[6 + n:9 + n]
        acc_ref = refs[9 + 2 * n]
        p_id, i_id = pl.program_id(0), pl.program_id(1)
        finish = _riding_exchange(lambda: _to_owner_copies(refs[6:6 + n], refs[9 + n:9 + 2 * n], *refs[10 + 2 * n:]),
                                  (p_id == 0) & (i_id == 0), (p_id == HEADS // 2 - 1) & (i_id == nq - 1)) if n else None
        _zero_first(pl.program_id(1), dk_ref, dv_ref)
        acc_ref[...] = jnp.zeros_like(acc_ref)
        qs = (q_ref[:, :LANES], q_ref[:, LANES:])
        doc = do_ref[...]
        lane_q = lax.broadcasted_iota(jnp.int32, (tq, LANES), 1)
        zq = jnp.zeros_like(doc)
        dos = (jnp.where(lane_q < V_DIM, doc, zq), jnp.where(lane_q >= V_DIM, doc, zq))
        lses = (lse_ref[0, 0:1, :], lse_ref[0, 1:2, :])
        dls = (dl_ref[0, 0:1, :], dl_ref[0, 1:2, :])

        def step(j, carry):
            off = pl.multiple_of(j * tk, tk)
            vp = v_ref[pl.ds(off, tk), :]
            lane_k = lax.broadcasted_iota(jnp.int32, (tk, LANES), 1)
            zero = jnp.zeros_like(vp)
            vs = (jnp.where(lane_k < V_DIM, vp, zero), jnp.where(lane_k >= V_DIM, vp, zero))
            for h in range(2):
                sl = slice(h * LANES, (h + 1) * LANES)
                st = lax.dot_general(k_ref[pl.ds(off, tk), sl], qs[h], NT_DIMS, preferred_element_type=F32)
                pt = jnp.exp2(st - lses[h])
                dpt = lax.dot_general(vs[h], doc, NT_DIMS, preferred_element_type=F32)
                dst = (pt * (dpt - dls[h])).astype(BF16)
                dv_ref[pl.ds(off, tk), :] += jnp.dot(pt.astype(BF16), dos[h], preferred_element_type=F32)
                dk_ref[pl.ds(off, tk), sl] += jnp.dot(dst, qs[h], preferred_element_type=F32)
                acc_ref[h] += lax.dot_general(k_ref[pl.ds(off, tk), sl], dst, TN_DIMS, preferred_element_type=F32)
            return carry

        lax.fori_loop(0, nk, step, 0)
        dq_ref[:, :LANES] = acc_ref[0].T
        dq_ref[:, LANES:] = acc_ref[1].T
        if n:
            finish()

    return pl.pallas_call(
        kern, grid=(HEADS // 2, nq),
        in_specs=[pl.BlockSpec((tq, 2 * LANES), lambda p, i: (i, p)), pl.BlockSpec((tq, LANES), lambda p, i: (i, p)),
                  pl.BlockSpec((1, 2, tq), lambda p, i: (p, 0, i)), pl.BlockSpec((1, 2, tq), lambda p, i: (p, 0, i)),
                  pl.BlockSpec((s, 2 * LANES), lambda p, i: (0, p)), pl.BlockSpec((s, LANES), lambda p, i: (0, p))]
        + [ANY] * n,
        out_specs=[pl.BlockSpec((tq, 2 * LANES), lambda p, i: (i, p)), pl.BlockSpec((s, 2 * LANES), lambda p, i: (0, p)),
                   pl.BlockSpec((s, LANES), lambda p, i: (0, p))] + [ANY] * n,
        out_shape=[jax.ShapeDtypeStruct((s, HEADS * LANES), F32), jax.ShapeDtypeStruct((s, HEADS * LANES), F32),
                   jax.ShapeDtypeStruct((s, MLA_W), F32)] + _to_owner_shapes(contributions),
        scratch_shapes=[pltpu.VMEM((2, LANES, tq), F32)] + (_to_owner_sems(n) if n else []),
        name="attn_bwd", compiler_params=_cparams(2))(q, do, lse, delta, k, v, *contributions)


def _mix_out(hf, hb, yg, o, x, g_lru, g_mla, w_out, tm):
    def body(i, n, hf_ref, hb_ref, yg_ref, o_ref, x_ref, gl, gm, w_ref, x1_ref, mix_ref):
        lo = (hf_ref[...] + hb_ref[...]) * _gelu(yg_ref[...])
        ov = o_ref[...]
        mix_ref[:, :LRU_W] = (lo * _rstd(lo) * gl[...]).astype(BF16)
        mix_ref[:, LRU_W:] = (ov * _rstd(ov) * gm[...]).astype(BF16)
        x1_ref[...] = x_ref[...] + jnp.dot(mix_ref[...], w_ref[...], preferred_element_type=F32)

    return _rowwise(body, "mix_out", x.shape[0], tm, rows=[hf, hb, yg, o, x], fulls=[g_lru, g_mla, w_out],
                    outs=[(D_MODEL, F32), (2 * LRU_W, BF16)])


def _mem_kv(mem, g_mem, w_kv, g_k):
    m = mem.shape[0]

    def body(i, n, mem_ref, g_ref, w_ref, gk_ref, km_ref, vm_ref):
        mv = mem_ref[...]
        kv = _bdot(mv * _rstd(mv) * g_ref[...], w_ref[...])
        vm_ref[...] = kv[:, MEM_W:].astype(BF16)
        for h in range(MEM_HEADS):
            sl = slice(h * MEM_HD, (h + 1) * MEM_HD)
            kh = kv[:, sl]
            km_ref[:, sl] = (kh * _rstd(kh) * gk_ref[...]).astype(BF16)

    return _rowwise(body, "mem_kv", m, m, rows=[mem], fulls=[g_mem, w_kv, g_k], outs=[(MEM_W, BF16), (MEM_W, BF16)])


def _mem_attn_core(x1v, g_ref, wq_ref, gq_ref, km_ref, vm_ref):
    scale = MEM_HD ** -0.5
    hm = (x1v * _rstd(x1v) * g_ref[...]).astype(BF16)
    qr = jnp.dot(hm, wq_ref[...], preferred_element_type=F32)
    heads = []
    for h in range(MEM_HEADS):
        sl = slice(h * MEM_HD, (h + 1) * MEM_HD)
        qh = qr[:, sl]
        rs = _rstd(qh)
        qn = (qh * rs * gq_ref[...]).astype(BF16)
        sc = lax.dot_general(qn, km_ref[:, sl], (((1,), (1,)), ((), ())), preferred_element_type=F32) * scale
        e = jnp.exp(sc - jnp.max(sc, axis=-1, keepdims=True))
        p = e / jnp.sum(e, axis=-1, keepdims=True)
        oh = jnp.dot(p.astype(BF16), vm_ref[:, sl], preferred_element_type=F32)
        heads.append((qh, rs, qn, p, oh))
    return hm, heads


def _mem_attn(x1, g, w_q, g_q, km, vm, w_o, tm):
    cs = D_MODEL // N_CHIPS

    def body(i, n, x1_ref, g_ref, wq_ref, gq_ref, km_ref, vm_ref, wo_ref, x2_ref, ob_ref):
        x1v = x1_ref[...]
        _, heads = _mem_attn_core(x1v, g_ref, wq_ref, gq_ref, km_ref, vm_ref)
        for h in range(MEM_HEADS):
            ob_ref[:, h * MEM_HD:(h + 1) * MEM_HD] = heads[h][4].astype(BF16)
        for k in range(N_CHIPS):
            sl = slice(k * cs, (k + 1) * cs)
            x2_ref[:, sl] = x1v[:, sl] + jnp.dot(ob_ref[...], wo_ref[k], preferred_element_type=F32)

    return _rowwise(body, "mem_attn", x1.shape[0], tm, rows=[x1], fulls=[g, w_q, g_q, km, vm, w_o],
                    outs=[(D_MODEL, F32), (MEM_W, BF16)])


def _ffn_up(x2, g, w_up, tm):
    cs = 2 * D_FF // N_CHIPS

    def body(i, n, x_ref, g_ref, w_ref, gu_ref, hb_ref):
        xv = x_ref[...]
        hb_ref[...] = (xv * _rstd(xv) * g_ref[...]).astype(BF16)
        for k in range(N_CHIPS):
            gu_ref[:, k * cs:(k + 1) * cs] = jnp.dot(hb_ref[...], w_ref[k], preferred_element_type=F32)

    return _rowwise(body, "ffn_up", x2.shape[0], tm, rows=[x2], fulls=[g, w_up], outs=[(2 * D_FF, F32), (D_MODEL, BF16)])


def _ffn_conv(gu, prev, nxt, cw_ref, i, n):
    prev = jnp.where(i == 0, 0.0, prev)
    nxt = jnp.where(i == n - 1, 0.0, nxt)
    return (cw_ref[3:4, :] + cw_ref[0:1, :] * _shift_down(gu, 1, prev) + cw_ref[1:2, :] * gu
            + cw_ref[2:3, :] * _shift_up(gu, 1, nxt))


def _ffn_down_loss(gu_pre, x2, target, cw, w_down, tm):
    def body(i, n, gu_ref, x_ref, t_ref, pv_ref, nx_ref, cw_ref, w_ref, dy_ref, dyb_ref, act_ref, dgu_ref, loss_ref):
        _zero_first(i, loss_ref)
        gu = _ffn_conv(gu_ref[...], pv_ref[...], nx_ref[...], cw_ref, i, n)
        g, u = gu[:, :D_FF], gu[:, D_FF:]
        sg = _sigmoid(g)
        a = g * sg
        act_ref[...] = (a * u).astype(BF16)
        y = x_ref[...] + jnp.dot(act_ref[...], w_ref[...], preferred_element_type=F32)
        e = y - t_ref[...]
        loss_ref[...] += jnp.sum(e * e)
        dy = e * (1.0 / D_MODEL)
        dy_ref[...] = dy
        dyb_ref[...] = dy.astype(BF16)
        d_act = lax.dot_general(dyb_ref[...], w_ref[...], NT_DIMS, preferred_element_type=F32)
        dgu_ref[:, :D_FF] = (d_act * u) * (sg + a - a * sg)
        dgu_ref[:, D_FF:] = d_act * a

    return _rowwise(body, "ffn_down_loss", x2.shape[0], tm, rows=[gu_pre, x2, target], halos=[gu_pre], fulls=[cw, w_down],
                    outs=[(D_MODEL, F32), (D_MODEL, BF16), (D_FF, BF16), (2 * D_FF, F32)], accs=[((SUBLANES, LANES), F32)])


def _ffn_bwd_conv(dgu, gu_pre, cw, tm):
    def body(i, n, d_ref, g_ref, dp_ref, dn_ref, cw_ref, dpre_ref, gc_ref):
        _zero_first(i, gc_ref)
        d = d_ref[...]
        g = g_ref[...]
        d_next = _shift_up(d, 1, jnp.where(i == n - 1, 0.0, dn_ref[...]))
        d_prev = _shift_down(d, 1, jnp.where(i == 0, 0.0, dp_ref[...]))
        dpre_ref[...] = (cw_ref[0:1, :] * d_next + cw_ref[1:2, :] * d + cw_ref[2:3, :] * d_prev).astype(BF16)
        _acc_row(gc_ref, 0, d_next * g)
        _acc_row(gc_ref, 1, d * g)
        _acc_row(gc_ref, 2, d_prev * g)
        _acc_row(gc_ref, 3, d)

    return _rowwise(body, "ffn_bwd_conv", dgu.shape[0], tm, rows=[dgu, gu_pre], halos=[dgu], fulls=[cw],
                    outs=[(2 * D_FF, BF16)], accs=[((SUBLANES, 2 * D_FF), F32)])


def _ffn_bwd_in(dpre, x2, dy, g, w_up, tm):
    cs = 2 * D_FF // N_CHIPS

    def body(i, n, dp_ref, x_ref, dy_ref, g_ref, w_ref, dx_ref, dxb_ref, gg_ref):
        _zero_first(i, gg_ref)
        d_h = jnp.zeros(x_ref.shape, F32)
        for k in range(N_CHIPS):
            d_h = d_h + lax.dot_general(dp_ref[:, k * cs:(k + 1) * cs], w_ref[k], (((1,), (1,)), ((), ())),
                                        preferred_element_type=F32)
        xv = x_ref[...]
        dx, dg = _norm_bwd(xv, _rstd(xv), g_ref[...], d_h)
        _acc_row(gg_ref, 0, dg)
        dx = dx + dy_ref[...]
        dx_ref[...] = dx
        dxb_ref[...] = dx.astype(BF16)

    return _rowwise(body, "ffn_bwd_in", x2.shape[0], tm, rows=[dpre, x2, dy], fulls=[g, w_up],
                    outs=[(D_MODEL, F32), (D_MODEL, BF16)], accs=[((SUBLANES, D_MODEL), F32)])


def _mem_attn_bwd(x1, dx2, dx2b, g, w_q, g_q, km, vm, w_o, tm):
    scale = MEM_HD ** -0.5
    m = km.shape[0]

    def body(i, n, x1_ref, dx2_ref, dx2b_ref, g_ref, wq_ref, gq_ref, km_ref, vm_ref, wo_ref,
             dx1_ref, dx1b_ref, hm_ref, dqr_ref, dkm_ref, dvm_ref, gg_ref, ggq_ref):
        _zero_first(i, dkm_ref, dvm_ref, gg_ref, ggq_ref)
        x1v = x1_ref[...]
        hm, heads = _mem_attn_core(x1v, g_ref, wq_ref, gq_ref, km_ref, vm_ref)
        hm_ref[...] = hm
        cs = D_MODEL // N_CHIPS
        d_o = jnp.zeros((x1v.shape[0], MEM_W), F32)
        for k in range(N_CHIPS):
            d_o = d_o + lax.dot_general(dx2b_ref[:, k * cs:(k + 1) * cs], wo_ref[k], (((1,), (1,)), ((), ())),
                                        preferred_element_type=F32)
        for h in range(MEM_HEADS):
            sl = slice(h * MEM_HD, (h + 1) * MEM_HD)
            qh, rs, qn, p, _ = heads[h]
            d_oh = d_o[:, sl].astype(BF16)
            dp = lax.dot_general(d_oh, vm_ref[:, sl], (((1,), (1,)), ((), ())), preferred_element_type=F32)
            ds = (p * (dp - jnp.sum(dp * p, axis=-1, keepdims=True)) * scale).astype(BF16)
            dqn = jnp.dot(ds, km_ref[:, sl], preferred_element_type=F32)
            dkm_ref[:, sl] += lax.dot_general(ds, qn, (((0,), (0,)), ((), ())), preferred_element_type=F32)
            dvm_ref[:, sl] += lax.dot_general(p.astype(BF16), d_oh, (((0,), (0,)), ((), ())), preferred_element_type=F32)
            dqh, dgq = _norm_bwd(qh, rs, gq_ref[...], dqn)
            _acc_row(ggq_ref, 0, dgq)
            dqr_ref[:, sl] = dqh.astype(BF16)
        d_hm = lax.dot_general(dqr_ref[...], wq_ref[...], (((1,), (1,)), ((), ())), preferred_element_type=F32)
        dx, dg = _norm_bwd(x1v, _rstd(x1v), g_ref[...], d_hm)
        _acc_row(gg_ref, 0, dg)
        dx = dx + dx2_ref[...]
        dx1_ref[...] = dx
        dx1b_ref[...] = dx.astype(BF16)

    return _rowwise(body, "mem_attn_bwd", x1.shape[0], tm, rows=[x1, dx2, dx2b], fulls=[g, w_q, g_q, km, vm, w_o],
                    outs=[(D_MODEL, F32), (D_MODEL, BF16), (D_MODEL, BF16), (MEM_W, BF16)],
                    accs=[((m, MEM_W), F32), ((m, MEM_W), F32), ((SUBLANES, D_MODEL), F32), ((SUBLANES, MEM_HD), F32)])


def _mem_kv_bwd(mem, g_mem, w_kv, g_k, dkm, dvm):
    m = mem.shape[0]

    def body(i, n, mem_ref, dkm_ref, dvm_ref, g_ref, w_ref, gk_ref, gw_ref, gg_ref, ggk_ref, dkv_ref):
        gg_ref[...] = jnp.zeros_like(gg_ref)
        ggk_ref[...] = jnp.zeros_like(ggk_ref)
        mv = mem_ref[...]
        rs_m = _rstd(mv)
        mem_n = (mv * rs_m * g_ref[...]).astype(BF16)
        kv = jnp.dot(mem_n, w_ref[...], preferred_element_type=F32)
        for h in range(MEM_HEADS):
            sl = slice(h * MEM_HD, (h + 1) * MEM_HD)
            kh = kv[:, sl]
            dkh, dgk = _norm_bwd(kh, _rstd(kh), gk_ref[...], dkm_ref[:, sl])
            _acc_row(ggk_ref, 0, dgk)
            dkv_ref[:, sl] = dkh.astype(BF16)
        dkv_ref[:, MEM_W:] = dvm_ref[...].astype(BF16)
        gw_ref[...] = lax.dot_general(mem_n, dkv_ref[...], (((0,), (0,)), ((), ())), preferred_element_type=F32)
        d_mn = lax.dot_general(dkv_ref[...], w_ref[...], (((1,), (1,)), ((), ())), preferred_element_type=F32)
        _acc_row(gg_ref, 0, d_mn * (mv * rs_m))

    return _rowwise(body, "mem_kv_bwd", m, m, rows=[mem, dkm, dvm], fulls=[g_mem, w_kv, g_k],
                    accs=[((D_MODEL, 2 * MEM_W), F32), ((SUBLANES, D_MODEL), F32), ((SUBLANES, MEM_HD), F32),
                          ((m, 2 * MEM_W), BF16)])


def _mix_out_bwd(dx1b, hf, hb, yg, o, g_lru, g_mla, w_out, tm):
    def body(i, n, dx_ref, hf_ref, hb_ref, yg_ref, o_ref, gl, gm, w_ref, dh_ref, dyg_ref, dob_ref, dl_ref, ggl_ref, ggm_ref):
        _zero_first(i, ggl_ref, ggm_ref)
        dmix = lax.dot_general(dx_ref[...], w_ref[...], (((1,), (1,)), ((), ())), preferred_element_type=F32)
        hs = hf_ref[...] + hb_ref[...]
        ygv = yg_ref[...]
        ge = _gelu(ygv)
        lo = hs * ge
        d_lo, dgl = _norm_bwd(lo, _rstd(lo), gl[...], dmix[:, :LRU_W])
        _acc_row(ggl_ref, 0, dgl)
        dh_ref[...] = d_lo * ge
        dyg_ref[...] = d_lo * hs * _gelu_grad(ygv)
        ov = o_ref[...]
        d_o, dgm = _norm_bwd(ov, _rstd(ov), gm[...], dmix[:, LRU_W:])
        _acc_row(ggm_ref, 0, dgm)
        dob_ref[...] = d_o.astype(BF16)
        prod = d_o * ov
        lane_w = lax.broadcasted_iota(jnp.int32, prod.shape, 1)
        lane = lax.broadcasted_iota(jnp.int32, (prod.shape[0], LANES), 1)
        dl = jnp.zeros((prod.shape[0], LANES), F32)
        for h in range(HEADS):
            in_head = (lane_w >= h * V_DIM) & (lane_w < (h + 1) * V_DIM)
            dl = dl + jnp.where(lane == h, jnp.sum(jnp.where(in_head, prod, 0.0), axis=-1, keepdims=True), 0.0)
        dl_ref[...] = dl

    return _rowwise(body, "mix_out_bwd", dx1b.shape[0], tm, rows=[dx1b, hf, hb, yg, o], fulls=[g_lru, g_mla, w_out],
                    outs=[(LRU_W, F32), (LRU_W, F32), (MLA_W, BF16), (LANES, F32)],
                    accs=[((SUBLANES, LRU_W), F32), ((SUBLANES, MLA_W), F32)])


def _mla_qkv_bwd(cq, ckv, krp, cos_t, sin_t, dq, dk, dv, g_qa, g_kva, g_qn, g_kn, w_uq_p, w_uk_p, w_uv, tm):
    scale = QK_HEAD ** -0.5

    def body(i, n, cq_ref, ckv_ref, kr_ref, c_ref, s_ref, dq_ref, dk_ref, dv_ref, gqa, gkva, gqn, gkn, wq, wk, wv,
             dcq_ref, dckv_ref, dkr_ref, cqb_ref, dqr_ref, ckvb_ref, dkn_ref, dvb_ref, ggqa, ggkva, ggqn, ggkn):
        _zero_first(i, ggqa, ggkva, ggqn, ggkn)
        cosv, sinv = c_ref[...], s_ref[...]
        cqv = cq_ref[...]
        rs_q = _rstd(cqv)
        cqb_ref[...] = (cqv * rs_q * gqa[...]).astype(BF16)
        qr = jnp.dot(cqb_ref[...], wq[...], preferred_element_type=F32)
        ckvv = ckv_ref[...]
        rs_kv = _rstd(ckvv)
        ckvb_ref[...] = (ckvv * rs_kv * gkva[...]).astype(BF16)
        kn = jnp.dot(ckvb_ref[...], wk[...], preferred_element_type=F32)
        kr = kr_ref[...]
        dkr = jnp.zeros_like(kr)
        for h in range(HEADS):
            sl = slice(h * LANES, (h + 1) * LANES)
            qh = qr[:, sl]
            d_qn = _rope_t(dq_ref[:, sl] * scale, cosv, sinv)
            dqh, dgq = _norm_bwd(qh, _rstd(qh, QK_HEAD), gqn[...], d_qn, QK_HEAD)
            _acc_row(ggqn, 0, dgq)
            dqr_ref[:, sl] = dqh.astype(BF16)
            kh = kn[:, sl] + kr
            d_kn = _rope_t(dk_ref[:, sl] * (1.0 / LOG2E), cosv, sinv)
            dkh, dgk = _norm_bwd(kh, _rstd(kh, QK_HEAD), gkn[...], d_kn, QK_HEAD)
            _acc_row(ggkn, 0, dgk)
            dkn_ref[:, sl] = dkh.astype(BF16)
            dkr = dkr + dkh
        dkr_ref[...] = dkr
        dvb_ref[...] = dv_ref[...].astype(BF16)
        d_cq = lax.dot_general(dqr_ref[...], wq[...], (((1,), (1,)), ((), ())), preferred_element_type=F32)
        dcq, dg = _norm_bwd(cqv, rs_q, gqa[...], d_cq)
        _acc_row(ggqa, 0, dg)
        dcq_ref[...] = dcq
        d_ckv = (lax.dot_general(dkn_ref[...], wk[...], (((1,), (1,)), ((), ())), preferred_element_type=F32)
                 + lax.dot_general(dvb_ref[...], wv[...], (((1,), (1,)), ((), ())), preferred_element_type=F32))
        dckv, dg = _norm_bwd(ckvv, rs_kv, gkva[...], d_ckv)
        _acc_row(ggkva, 0, dg)
        dckv_ref[...] = dckv

    return _rowwise(body, "mla_qkv_bwd", cq.shape[0], tm, rows=[cq, ckv, krp, cos_t, sin_t, dq, dk, dv],
                    fulls=[g_qa, g_kva, g_qn, g_kn, w_uq_p, w_uk_p, w_uv],
                    outs=[(Q_LORA, F32), (KV_LORA, F32), (LANES, F32), (Q_LORA, BF16), (HEADS * LANES, BF16),
                          (KV_LORA, BF16), (HEADS * LANES, BF16), (MLA_W, BF16)],
                    accs=[((SUBLANES, Q_LORA), F32), ((SUBLANES, KV_LORA), F32), ((SUBLANES, LANES), F32),
                          ((SUBLANES, LANES), F32)])


def _in_proj_bwd(x, dx1, dxr_f, dxr_b, dyg, dcq, dckv, dkrp, g, w_in_p, tm):
    def body(i, n, x_ref, dx1_ref, df_ref, db_ref, dyg_ref, dcq_ref, dckv_ref, dkr_ref, g_ref, w_ref, gx_ref, dp_ref, gg_ref):
        _zero_first(i, gg_ref)
        dp_ref[:, :LRU_W] = (df_ref[...] + db_ref[...]).astype(BF16)
        dp_ref[:, LRU_W:2 * LRU_W] = dyg_ref[...].astype(BF16)
        dp_ref[:, 2 * LRU_W:2 * LRU_W + Q_LORA] = dcq_ref[...].astype(BF16)
        dp_ref[:, 2 * LRU_W + Q_LORA:OFF_KR] = dckv_ref[...].astype(BF16)
        dp_ref[:, OFF_KR:] = dkr_ref[...].astype(BF16)
        d_h = lax.dot_general(dp_ref[...], w_ref[...], (((1,), (1,)), ((), ())), preferred_element_type=F32)
        xv = x_ref[...]
        dx, dg = _norm_bwd(xv, _rstd(xv), g_ref[...], d_h)
        _acc_row(gg_ref, 0, dg)
        gx_ref[...] = dx + dx1_ref[...]

    return _rowwise(body, "in_proj_bwd", x.shape[0], tm, rows=[x, dx1, dxr_f, dxr_b, dyg, dcq, dckv, dkrp],
                    fulls=[g, w_in_p], outs=[(D_MODEL, F32), (IN_PAD, BF16)], accs=[((SUBLANES, D_MODEL), F32)])


ANY = pl.BlockSpec(memory_space=pl.ANY)


def _chip_peers(x, y):
    return ((1 - x, y), (x, 1 - y), (1 - x, 1 - y))


def _exchange_call(kern, name, ins, out_shapes, n_sems, aliases=None):
    return pl.pallas_call(
        kern, in_specs=[ANY] * len(ins), out_specs=[ANY] * len(out_shapes), out_shape=out_shapes,
        scratch_shapes=[pltpu.SemaphoreType.DMA((n,)) for n in n_sems], input_output_aliases=aliases or {},
        name=name)(*ins)


def _start_then_wait(copies):
    for cp in copies:
        cp.start()
    for cp in copies:
        cp.wait()


N_DEV = 8
RELATIONS = tuple((dx, dy, dc) for dx in (0, 1) for dy in (0, 1) for dc in (0, 1))[1:]


def _flip(v, d):
    return 1 - v if d else v


def _gather_copies(ins, outs, ssem, rsem, lsem):
    x, y, c = lax.axis_index("x"), lax.axis_index("y"), lax.axis_index("c")
    me = 2 * x + y
    cps = []
    for i, (a, o) in enumerate(zip(ins, outs)):
        cps.append(pltpu.make_async_copy(a, o.at[me], lsem.at[i]))
        for j, (px, py) in enumerate(_chip_peers(x, y)):
            cps.append(pltpu.make_async_remote_copy(a, o.at[me], ssem.at[3 * i + j], rsem.at[3 * i + j],
                                                    device_id=(px, py, c), device_id_type=MESH))
    return cps


def _gather_shapes(arrs):
    return [jax.ShapeDtypeStruct((N_CHIPS,) + a.shape, a.dtype) for a in arrs]


def _gather_sems(n):
    return [pltpu.SemaphoreType.DMA((3 * n,)), pltpu.SemaphoreType.DMA((3 * n,)), pltpu.SemaphoreType.DMA((n,))]


def _gather_chips(arrs):
    n = len(arrs)

    def kern(*refs):
        _start_then_wait(_gather_copies(refs[:n], refs[n:2 * n], *refs[2 * n:]))

    return _exchange_call(kern, "gather_weights", arrs, _gather_shapes(arrs), (3 * n, 3 * n, n))


def _to_owner_copies(ins, outs, ssem, rsem, lsem):
    x, y, c = lax.axis_index("x"), lax.axis_index("y"), lax.axis_index("c")
    me = 4 * x + 2 * y + c
    cps = []
    for i, (a, o) in enumerate(zip(ins, outs)):
        cps.append(pltpu.make_async_copy(a.at[2 * x + y, c], o.at[me], lsem.at[i]))
        for r, (dx, dy, dc) in enumerate(RELATIONS):
            tx, ty, tc = _flip(x, dx), _flip(y, dy), _flip(c, dc)
            cps.append(pltpu.make_async_remote_copy(a.at[2 * tx + ty, tc], o.at[me], ssem.at[7 * i + r], rsem.at[7 * i + r],
                                                    device_id=(tx, ty, tc), device_id_type=MESH))
    return cps


def _to_owner_shapes(arrs):
    return [jax.ShapeDtypeStruct((N_DEV,) + a.shape[2:], a.dtype) for a in arrs]


def _to_owner_sems(n):
    return [pltpu.SemaphoreType.DMA((7 * n,)), pltpu.SemaphoreType.DMA((7 * n,)), pltpu.SemaphoreType.DMA((n,))]


def _to_owner(arrs, name):
    n = len(arrs)

    def kern(*refs):
        _start_then_wait(_to_owner_copies(refs[:n], refs[n:2 * n], *refs[2 * n:]))

    return _exchange_call(kern, name, arrs, _to_owner_shapes(arrs), (7 * n, 7 * n, n))


def _join_halves(arrs):
    n = len(arrs)

    def kern(*refs):
        outs, (ssem, rsem) = refs[n:2 * n], refs[2 * n:]
        x, y, c = lax.axis_index("x"), lax.axis_index("y"), lax.axis_index("c")
        _start_then_wait([
            pltpu.make_async_remote_copy(outs[i].at[c], outs[i].at[c], ssem.at[i], rsem.at[i],
                                         device_id=(x, y, 1 - c), device_id_type=MESH) for i in range(n)])

    outs = [jax.ShapeDtypeStruct(a.shape, a.dtype) for a in arrs]
    return _exchange_call(kern, "grad_join_halves", arrs, outs, (n, n), aliases={i: i for i in range(n)})


def _row_block(rows, row_bytes, limit=1 << 20):
    best = None
    for d in range(16, rows + 1, 16):
        if rows % d == 0 and d * row_bytes <= limit:
            best = d
    return best if best is not None else rows


def _sum_devices(b, c, name):
    _, h, cols = b.shape
    hb = _row_block(h, cols * 4)

    def kern(c_ref, b_ref, o_ref):
        acc = b_ref[0].astype(F32)
        for j in range(1, N_DEV):
            acc = acc + b_ref[j].astype(F32)
        o_ref[...] = acc

    return pl.pallas_call(
        kern,
        grid_spec=pltpu.PrefetchScalarGridSpec(
            num_scalar_prefetch=1, grid=(h // hb,),
            in_specs=[pl.BlockSpec((N_DEV, hb, cols), lambda i, c_ref: (0, i, 0))],
            out_specs=pl.BlockSpec((None, hb, cols), lambda i, c_ref: (c_ref[0], i, 0))),
        out_shape=jax.ShapeDtypeStruct((2, h, cols), F32), name=name, compiler_params=_cparams(1))(c, b)


def _adamw(w, g, m, v, name):
    rows, cols = w.shape
    rb = _row_block(rows, cols * 4)
    c1 = 1.0 - ADAM_B1 ** ADAM_STEP
    c2 = 1.0 - ADAM_B2 ** ADAM_STEP

    def kern(w_ref, g_ref, m_ref, v_ref, d_ref, mo_ref, vo_ref):
        gv = g_ref[...]
        mn = ADAM_B1 * m_ref[...] + (1.0 - ADAM_B1) * gv
        vn = ADAM_B2 * v_ref[...] + (1.0 - ADAM_B2) * (gv * gv)
        mo_ref[...] = mn
        vo_ref[...] = vn
        d_ref[...] = (-ADAM_LR) * ((mn / c1) / (jnp.sqrt(vn / c2) + ADAM_EPS) + ADAM_WD * w_ref[...])

    spec = pl.BlockSpec((rb, cols), lambda i: (i, 0))
    return pl.pallas_call(
        kern, grid=(rows // rb,), in_specs=[spec] * 4, out_specs=[spec] * 3,
        out_shape=[jax.ShapeDtypeStruct(w.shape, F32)] * 3, name=name, compiler_params=_cparams(1))(w, g, m, v)


def _pad_rows(flat, rows):
    return jnp.pad(flat, (0, rows * LANES - flat.shape[0])).reshape(rows, LANES)


def _round_up(n, m):
    return (n + m - 1) // m * m


def _shard_shape(shape, axis):
    return tuple(s // N_CHIPS if a == axis else s for a, s in enumerate(shape))


def _to_shards(full, axis):
    shape = full.shape
    t = full.reshape(shape[:axis] + (N_CHIPS, shape[axis] // N_CHIPS) + shape[axis + 1:])
    return jnp.moveaxis(t, axis, 0).reshape(N_CHIPS, -1)


def _from_shards(sh, shape, axis):
    t = sh.reshape((N_CHIPS,) + _shard_shape(shape, axis))
    t = jnp.moveaxis(t, 0, axis)
    return t.reshape(shape)


BIG = tuple((name, shape, axis) for name, shape, axis, big in SHARDED if big)
EARLY_WEIGHTS = ("w_in", "w_uq", "w_ukv")
SMALL_SHARDED = tuple((name, shape, axis) for name, shape, axis, big in SHARDED if not big)


def _pack_small_weights(p):
    flat = jnp.concatenate([p[name].reshape(-1) for name, _, _ in SMALL_SHARDED])
    return _pad_rows(flat, _round_up(-(-flat.shape[0] // LANES), SUBLANES))


def _unpack_small_weights(gathered):
    flat = gathered.reshape(N_CHIPS, -1)
    out, off = {}, 0
    for name, shape, axis in SMALL_SHARDED:
        n = _numel(shape) // N_CHIPS
        out[name] = _from_shards(flat[:, off:off + n], shape, axis)
        off += n
    return out


def _pack_small_local(p, prefix=""):
    parts = [p[prefix + name].reshape(-1) for name, _, _ in SMALL_SHARDED]
    parts += [p[prefix + name].reshape(-1) for name, _ in REPLICATED]
    return jnp.concatenate(parts)


def _pack_small_grads(g):
    parts = [_to_shards(g[name], axis) for name, _, axis in SMALL_SHARDED]
    rep = jnp.concatenate([g[name].reshape(-1) for name, _ in REPLICATED])
    parts.append(jnp.broadcast_to(rep[None], (N_CHIPS, rep.shape[0])))
    return jnp.concatenate(parts, axis=1)


def _unpack_small_local(flat):
    out, off = {}, 0
    for name, shape, axis in SMALL_SHARDED:
        n = _numel(shape) // N_CHIPS
        out[name] = flat[off:off + n].reshape((1,) + _shard_shape(shape, axis))
        off += n
    for name, shape in REPLICATED:
        n = _numel(shape)
        out[name] = flat[off:off + n].reshape((1,) + shape)
        off += n
    return out


def _grad_shards(g, shape, axis):
    if axis == 0:
        return g.reshape((N_CHIPS,) + _shard_shape(shape, axis))
    return jnp.transpose(g.reshape(shape[0], N_CHIPS, shape[1] // N_CHIPS), (1, 0, 2))


def _cols_from_shards(w4):
    return jnp.transpose(w4, (1, 0, 2)).reshape(w4.shape[1], -1)


def _block_diag(w):
    eye = jnp.eye(LRU_BLOCKS, dtype=w.dtype)
    return jnp.einsum("ncd,nm->ncmd", w, eye).reshape(LRU_W, LRU_W)


def _block_diag_t(g):
    g4 = g.reshape(LRU_BLOCKS, 64, LRU_BLOCKS, 64)
    return jnp.stack([g4[n, :, n, :] for n in range(LRU_BLOCKS)])


def _pad8(a):
    return jnp.pad(a, ((0, SUBLANES - a.shape[0]), (0, 0)))


def kernel(x, mem, positions, attn_norm, w_in, lru_conv_w, lru_conv_b, lru_w_a, lru_b_a, lru_w_i, lru_b_i, lru_lambda, q_a_norm, w_uq, kv_a_norm, w_ukv, mla_q_norm, mla_k_norm, lru_out_norm, mla_out_norm, w_out, mem_attn_norm, mem_norm, w_mem_q, w_mem_kv, mem_q_norm, mem_k_norm, w_mem_o, ffn_norm, w_up, ffn_conv_w, ffn_conv_b, w_down, loss_target, m_attn_norm, m_w_in, m_lru_conv_w, m_lru_conv_b, m_lru_w_a, m_lru_b_a, m_lru_w_i, m_lru_b_i, m_lru_lambda, m_q_a_norm, m_w_uq, m_kv_a_norm, m_w_ukv, m_mla_q_norm, m_mla_k_norm, m_lru_out_norm, m_mla_out_norm, m_w_out, m_mem_attn_norm, m_mem_norm, m_w_mem_q, m_w_mem_kv, m_mem_q_norm, m_mem_k_norm, m_w_mem_o, m_ffn_norm, m_w_up, m_ffn_conv_w, m_ffn_conv_b, m_w_down, v_attn_norm, v_w_in, v_lru_conv_w, v_lru_conv_b, v_lru_w_a, v_lru_b_a, v_lru_w_i, v_lru_b_i, v_lru_lambda, v_q_a_norm, v_w_uq, v_kv_a_norm, v_w_ukv, v_mla_q_norm, v_mla_k_norm, v_lru_out_norm, v_mla_out_norm, v_w_out, v_mem_attn_norm, v_mem_norm, v_w_mem_q, v_w_mem_kv, v_mem_q_norm, v_mem_k_norm, v_w_mem_o, v_ffn_norm, v_w_up, v_ffn_conv_w, v_ffn_conv_b, v_w_down):
    given = dict(locals())
    local = {name: given[name][0] for name in WEIGHT_ORDER}
    s = x.shape[1]
    x2d, mem2d, tgt = x[0], mem[0], loss_target[0]
    tm = min(512, s)
    tm_wide = min(1024, s)
    tm_ffn = min(256, s)
    t_scan = min(1024, s)
    tq_f, tq_b, tk = min(4096, s), min(2048, s), min(512, s)

    early = [b for b in BIG if b[0] in EARLY_WEIGHTS]
    late = [b for b in BIG if b[0] not in EARLY_WEIGHTS]
    got = _gather_chips([local[name].astype(BF16) for name, _, _ in early] + [_pack_small_weights(local)])
    full = _unpack_small_weights(got[-1])

    def take_gathered(entries, arrays):
        for (name, shape, axis), w4 in zip(entries, arrays):
            if axis == 0:
                full[name] = w4.reshape(shape)
            elif name in ("w_up", "w_mem_o"):
                full[name] = w4
            else:
                full[name] = _cols_from_shards(w4)

    take_gathered(early, got)
    row = lambda a: a.reshape(1, -1)
    b16 = lambda a: a.astype(BF16)
    zeros = lambda r, c: jnp.zeros((r, c), BF16)
    w_in_f = full["w_in"]
    w_in_p = jnp.concatenate([w_in_f[:, :OFF_KR], _head_tile(zeros(D_MODEL, QK_NOPE), w_in_f[:, OFF_KR:])], axis=1)
    uq = full["w_uq"].reshape(Q_LORA, HEADS, QK_HEAD)
    w_uq_p = _head_tile(uq[:, :, :QK_NOPE], uq[:, :, QK_NOPE:]).reshape(Q_LORA, -1)
    ukv = full["w_ukv"].reshape(KV_LORA, HEADS, QK_NOPE + V_DIM)
    w_uk_p = _head_tile(ukv[:, :, :QK_NOPE], None).reshape(KV_LORA, -1)
    w_uv = ukv[:, :, QK_NOPE:].reshape(KV_LORA, MLA_W)
    wa = [b16(_block_diag(local["lru_w_a"][d])) for d in range(2)]
    wi = [b16(_block_diag(local["lru_w_i"][d])) for d in range(2)]
    cw = [_pad8(full["lru_conv_w"][d]) for d in range(2)]
    pv = [_pad8(jnp.stack([full["lru_conv_b"][d], full["lru_b_a"][d], full["lru_b_i"][d], full["lru_lambda"][d]]))
          for d in range(2)]
    ffn_cw = _pad8(jnp.concatenate([full["ffn_conv_w"], row(local["ffn_conv_b"])], axis=0))
    g_attn, g_qa, g_kva = row(local["attn_norm"]), row(local["q_a_norm"]), row(local["kv_a_norm"])
    g_qn = _head_tile(row(local["mla_q_norm"])[:, :QK_NOPE], row(local["mla_q_norm"])[:, QK_NOPE:])
    g_kn = _head_tile(row(local["mla_k_norm"])[:, :QK_NOPE], row(local["mla_k_norm"])[:, QK_NOPE:])
    g_lru, g_mla = row(local["lru_out_norm"]), row(local["mla_out_norm"])
    g_memattn, g_mem = row(local["mem_attn_norm"]), row(local["mem_norm"])
    g_mq, g_mk, g_ffn = row(local["mem_q_norm"]), row(local["mem_k_norm"]), row(local["ffn_norm"])

    inv = ROPE_THETA ** (-jnp.arange(0, QK_ROPE, 2, dtype=F32) / QK_ROPE)
    no_nope = jnp.zeros((1, QK_NOPE), F32)
    inv_tile = _head_tile(no_nope, jnp.concatenate([inv, inv])[None])
    sign_tile = _head_tile(no_nope, jnp.concatenate([-jnp.ones_like(inv), jnp.ones_like(inv)])[None])
    ang = positions[0].astype(F32)[:, None] * inv_tile
    cos_t, sin_t = jnp.cos(ang), jnp.sin(ang) * sign_tile

    xr, yg, cq, ckv, krp, hb_in = _in_proj(x2d, g_attn, w_in_p, tm_wide)
    h_f, *saved_f = _lru_scan_fwd(xr, cw[0], pv[0], wa[0], wi[0], False, t_scan)
    h_b, *saved_b = _lru_scan_fwd(xr, cw[1], pv[1], wa[1], wi[1], True, t_scan)
    q, k, v = _mla_qkv(cq, ckv, krp, cos_t, sin_t, g_qa, g_kva, g_qn, g_kn, w_uq_p, w_uk_p, w_uv, tm_wide)
    o, lse, *got = _attn_fwd(q, k, v, tq_f, tk, shards=[local[name].astype(BF16) for name, _, _ in late])
    take_gathered(late, got)
    x1, mixed = _mix_out(h_f, h_b, yg, o, x2d, g_lru, g_mla, full["w_out"], tm_wide)
    km, vm = _mem_kv(mem2d, g_mem, full["w_mem_kv"], g_mk)
    x2, o_mem = _mem_attn(x1, g_memattn, full["w_mem_q"], g_mq, km, vm, full["w_mem_o"], tm_wide)
    gu_pre, hb_ffn = _ffn_up(x2, g_ffn, full["w_up"], tm)
    dy, dyb, act, dgu, loss_acc = _ffn_down_loss(gu_pre, x2, tgt, ffn_cw, full["w_down"], tm_ffn)
    loss = lax.psum(loss_acc[0, 0] * (0.5 / D_MODEL), ("x", "y", "c"))

    grads = {}
    grads["w_down"] = _matmul_tn(act, dyb, "grad_w_down", out_dtype=BF16)
    dpre, g_conv = _ffn_bwd_conv(dgu, gu_pre, ffn_cw, tm_ffn)
    grads["ffn_conv_w"], grads["ffn_conv_b"] = g_conv[:3], g_conv[3]
    grads["w_up"] = _matmul_tn(hb_ffn, dpre, "grad_w_up", col_shards=True, out_dtype=BF16)
    dx2, dx2b, gg = _ffn_bwd_in(dpre, x2, dy, g_ffn, full["w_up"], tm)
    grads["ffn_norm"] = gg[0]
    grads["w_mem_o"] = _matmul_tn(o_mem, dx2b, "grad_w_mem_o", out_dtype=BF16)
    dx1, dx1b, hm, dqr_mem, dkm, dvm, gg, ggq = _mem_attn_bwd(x1, dx2, dx2b, g_memattn, full["w_mem_q"], g_mq, km, vm,
                                                                 full["w_mem_o"], tm)
    grads["mem_attn_norm"], grads["mem_q_norm"] = gg[0], ggq[0]
    grads["w_mem_q"] = _matmul_tn(hm, dqr_mem, "grad_w_mem_q", out_dtype=BF16)
    g_mem_kv, gg, ggk, _ = _mem_kv_bwd(mem2d, g_mem, full["w_mem_kv"], g_mk, dkm, dvm)
    grads["w_mem_kv"] = g_mem_kv.astype(BF16)
    grads["mem_norm"], grads["mem_k_norm"] = gg[0], ggk[0]
    grads["w_out"] = _matmul_tn(mixed, dx1b, "grad_w_out", out_dtype=BF16)
    dh, dyg, dob, dl128, ggl, ggm = _mix_out_bwd(dx1b, h_f, h_b, yg, o, g_lru, g_mla, full["w_out"], tm)
    grads["lru_out_norm"], grads["mla_out_norm"] = ggl[0], ggm[0]
    delta_t = jnp.transpose(dl128[:, :HEADS]).reshape(HEADS // 2, 2, s)
    def halves(name, shape, axis):
        g4 = grads[name] if grads[name].ndim == 3 else _grad_shards(grads[name], shape, axis)
        return g4.reshape(N_CHIPS, 2, g4.shape[1] // 2, g4.shape[2])

    dq, dk, dv, *arrived_late = _attn_bwd(q, k, v, dob, lse, delta_t, tq_b, tk,
                                          contributions=[halves(*e) for e in late])
    (dcq, dckv, dkrp, cqb, dqr, ckvb, dkn, dvb, ggqa, ggkva, ggqn, ggkn) = _mla_qkv_bwd(
        cq, ckv, krp, cos_t, sin_t, dq, dk, dv, g_qa, g_kva, g_qn, g_kn, w_uq_p, w_uk_p, w_uv, tm_wide)
    grads["q_a_norm"], grads["kv_a_norm"] = ggqa[0], ggkva[0]
    grads["mla_q_norm"] = jnp.concatenate(_from_head_tile(ggqn[0]))
    grads["mla_k_norm"] = jnp.concatenate(_from_head_tile(ggkn[0]))
    g_uq_p = _matmul_tn(cqb, dqr, "grad_w_uq")
    grads["w_uq"] = jnp.concatenate(_from_head_tile(g_uq_p.reshape(Q_LORA, HEADS, LANES)), axis=-1).reshape(Q_LORA, -1)
    g_uk_p = _from_head_tile(_matmul_tn(ckvb, dkn, "grad_w_uk").reshape(KV_LORA, HEADS, LANES))[0]
    g_uv = _matmul_tn(ckvb, dvb, "grad_w_uv").reshape(KV_LORA, HEADS, V_DIM)
    grads["w_ukv"] = jnp.concatenate([g_uk_p, g_uv], axis=2).reshape(KV_LORA, -1)
    dxr, gwa, gwi, gvec = [], [], [], []
    for d, (hd, saved) in enumerate(((h_f, saved_f), (h_b, saved_b))):
        r = _lru_scan_bwd(xr, saved, hd, dh, cw[d], pv[d], wa[d], wi[d], d == 1, t_scan)
        dxr.append(r[0])
        gwa.append(_block_diag_t(r[1]))
        gwi.append(_block_diag_t(r[2]))
        gvec.append(r[3])
    grads["lru_w_a"], grads["lru_w_i"] = jnp.stack(gwa), jnp.stack(gwi)
    grads["lru_conv_w"] = jnp.stack([gv[:CONV_W] for gv in gvec])
    for r_i, name in ((4, "lru_conv_b"), (5, "lru_b_a"), (6, "lru_b_i"), (7, "lru_lambda")):
        grads[name] = jnp.stack([gv[r_i] for gv in gvec])
    grad_x, dproj, gg = _in_proj_bwd(x2d, dx1, dxr[0], dxr[1], dyg, dcq, dckv, dkrp, g_attn, w_in_p, tm)
    grads["attn_norm"] = gg[0]
    g_in_p = _matmul_tn(hb_in, dproj, "grad_w_in")
    grads["w_in"] = jnp.concatenate([g_in_p[:, :OFF_KR], _from_head_tile(g_in_p[:, OFF_KR:])[1]], axis=1)

    small = _pack_small_grads(grads)
    length = small.shape[1]
    hrows = _round_up(-(-length // (2 * LANES)), 16)
    small = jnp.pad(small, ((0, 0), (0, 2 * hrows * LANES - length))).reshape(N_CHIPS, 2, hrows, LANES)
    for name, _, _ in early:
        grads[name] = grads[name].astype(BF16)
    arrived_early = _to_owner([halves(*e) for e in early] + [small], "grad_to_owner")
    names = [name for name, _, _ in late + early] + ["small"]
    c_idx = lax.axis_index("c").astype(jnp.int32).reshape(1)
    reduced = _join_halves([_sum_devices(b, c_idx, "grad_sum_" + n)
                            for n, b in zip(names, list(arrived_late) + list(arrived_early))])

    outs = [{}, {}, {}, {}]
    for (name, shape, axis), r in zip(late + early, reduced):
        g2 = r.reshape(_shard_shape(shape, axis))
        res = _adamw(local[name], g2, given["m_" + name][0], given["v_" + name][0], "adamw_" + name)
        for o_, a in zip(outs, (g2, *res)):
            o_[name] = a[None]
    pack = lambda prefix: _pad_rows(_pack_small_local({n: given[prefix + n] for n in WEIGHT_ORDER}), 2 * hrows)
    g_small = reduced[-1].reshape(2 * hrows, LANES)
    res = _adamw(pack(""), g_small, pack("m_"), pack("v_"), "adamw_small")
    for o_, a in zip(outs, (g_small, *res)):
        o_.update(_unpack_small_local(a.reshape(-1)))
    return (loss, grad_x[None], *[o_[n] for o_ in outs for n in WEIGHT_ORDER])
```

```python
import jax
import jax.numpy as jnp
from jax import lax
from jax.experimental import pallas as pl
from jax.experimental.pallas import tpu as pltpu

F32, BF16 = jnp.float32, jnp.bfloat16
MESH = pl.DeviceIdType.MESH

D_MODEL = 1024
EPS = 1e-6
LRU_W = 512
LRU_BLOCKS = 8
LRU_C = 8.0
CONV_W = 4
HEADS = 8
QK_NOPE, QK_ROPE, QK_HEAD, V_DIM = 64, 32, 96, 64
Q_LORA, KV_LORA = 256, 128
MLA_W = HEADS * V_DIM
ROPE_THETA = 10000.0
IN_COLS = 2 * LRU_W + Q_LORA + KV_LORA + QK_ROPE
OFF_KR = IN_COLS - QK_ROPE
IN_PAD = 1536
MEM_HEADS, MEM_HD = 4, 128
MEM_W = MEM_HEADS * MEM_HD
D_FF = 2816
N_CHIPS = 4
ADAM_LR, ADAM_B1, ADAM_B2, ADAM_EPS, ADAM_WD, ADAM_STEP = 0.001, 0.9, 0.999, 1e-08, 0.01, 10

LANES = 128
SUBLANES = 8
V7X_VMEM_BYTES = 64 * 1024 * 1024
VMEM_LIMIT = V7X_VMEM_BYTES * 7 // 8

SHARDED = (
    ("w_in", (D_MODEL, IN_COLS), 1, True),
    ("lru_conv_w", (2, CONV_W, LRU_W), 2, False),
    ("lru_conv_b", (2, LRU_W), 1, False),
    ("lru_b_a", (2, LRU_W), 1, False),
    ("lru_b_i", (2, LRU_W), 1, False),
    ("lru_lambda", (2, LRU_W), 1, False),
    ("w_uq", (Q_LORA, HEADS * QK_HEAD), 1, True),
    ("w_ukv", (KV_LORA, HEADS * (QK_NOPE + V_DIM)), 1, True),
    ("w_out", (2 * LRU_W, D_MODEL), 0, True),
    ("w_mem_q", (D_MODEL, MEM_W), 0, True),
    ("w_mem_kv", (D_MODEL, 2 * MEM_W), 0, True),
    ("w_mem_o", (MEM_W, D_MODEL), 1, True),
    ("w_up", (D_MODEL, 2 * D_FF), 1, True),
    ("ffn_conv_w", (3, 2 * D_FF), 1, False),
    ("w_down", (D_FF, D_MODEL), 0, True),
)
REPLICATED = (
    ("attn_norm", (D_MODEL,)), ("lru_w_a", (2, LRU_BLOCKS, 64, 64)), ("lru_w_i", (2, LRU_BLOCKS, 64, 64)),
    ("q_a_norm", (Q_LORA,)), ("kv_a_norm", (KV_LORA,)), ("mla_q_norm", (QK_HEAD,)), ("mla_k_norm", (QK_HEAD,)),
    ("lru_out_norm", (LRU_W,)), ("mla_out_norm", (MLA_W,)), ("mem_attn_norm", (D_MODEL,)), ("mem_norm", (D_MODEL,)),
    ("mem_q_norm", (MEM_HD,)), ("mem_k_norm", (MEM_HD,)), ("ffn_norm", (D_MODEL,)), ("ffn_conv_b", (2 * D_FF,)),
)
WEIGHT_ORDER = ('attn_norm', 'w_in', 'lru_conv_w', 'lru_conv_b', 'lru_w_a', 'lru_b_a', 'lru_w_i', 'lru_b_i', 'lru_lambda',
                'q_a_norm', 'w_uq', 'kv_a_norm', 'w_ukv', 'mla_q_norm', 'mla_k_norm', 'lru_out_norm', 'mla_out_norm', 'w_out',
                'mem_attn_norm', 'mem_norm', 'w_mem_q', 'w_mem_kv', 'mem_q_norm', 'mem_k_norm', 'w_mem_o', 'ffn_norm', 'w_up',
                'ffn_conv_w', 'ffn_conv_b', 'w_down')


def _numel(shape):
    n = 1
    for s in shape:
        n *= s
    return n


def _cparams(n_axes):
    return pltpu.CompilerParams(dimension_semantics=("arbitrary",) * n_axes, vmem_limit_bytes=VMEM_LIMIT)


def _bdot(a, b):
    return jnp.dot(a.astype(BF16), b.astype(BF16), preferred_element_type=F32)


def _bdot_nt(a, b):
    return lax.dot_general(a.astype(BF16), b.astype(BF16), (((1,), (1,)), ((), ())), preferred_element_type=F32)


def _bdot_tn(a, b):
    return lax.dot_general(a.astype(BF16), b.astype(BF16), (((0,), (0,)), ((), ())), preferred_element_type=F32)


def _rstd(x, n=None):
    n = x.shape[-1] if n is None else n
    return lax.rsqrt(jnp.sum(x * x, axis=-1, keepdims=True) * (1.0 / n) + EPS)


def _norm_bwd(x, rs, g, dy, n=None):
    n = x.shape[-1] if n is None else n
    xhat = x * rs
    dxh = dy * g
    dx = rs * (dxh - xhat * (jnp.sum(dxh * xhat, axis=-1, keepdims=True) * (1.0 / n)))
    return dx, dy * xhat


def _acc_row(ref, r, val):
    ref[r:r + 1, :] += jnp.sum(val, axis=0, keepdims=True)


def _zero_first(i, *refs):
    @pl.when(i == 0)
    def _():
        for r in refs:
            r[...] = jnp.zeros_like(r)


def _shift_down(x, j, halo):
    if j == 0:
        return x
    xs = pltpu.roll(x, j, 0)
    hs = pltpu.roll(halo, j, 0)
    row = lax.broadcasted_iota(jnp.int32, hs.shape, 0)
    top = jnp.where(row < j, hs, xs[:SUBLANES])
    return jnp.concatenate([top, xs[SUBLANES:]], axis=0)


def _shift_up(x, j, halo):
    if j == 0:
        return x
    t = x.shape[0]
    xs = pltpu.roll(x, t - j, 0)
    hs = pltpu.roll(halo, SUBLANES - j, 0)
    row = lax.broadcasted_iota(jnp.int32, hs.shape, 0)
    bot = jnp.where(row >= SUBLANES - j, hs, xs[t - SUBLANES:])
    return jnp.concatenate([xs[:t - SUBLANES], bot], axis=0)


def _shift(x, j, halo, down):
    return _shift_down(x, j, halo) if down else _shift_up(x, j, halo)


def _scan(a, b, h_in, down):
    t, c = a.shape
    g = t // SUBLANES
    a3, b3 = a.reshape(g, SUBLANES, c), b.reshape(g, SUBLANES, c)
    sub = lax.broadcasted_iota(jnp.int32, a3.shape, 1)
    d = 1
    while d < SUBLANES:
        keep = (sub >= d) if down else (sub < SUBLANES - d)
        shift = d if down else SUBLANES - d
        a_s = jnp.where(keep, pltpu.roll(a3, shift, 1), 1.0)
        b_s = jnp.where(keep, pltpu.roll(b3, shift, 1), 0.0)
        b3 = a3 * b_s + b3
        a3 = a3 * a_s
        d *= 2
    hs = [None] * g
    carry = h_in
    for i in (range(g) if down else range(g - 1, -1, -1)):
        hs[i] = a3[i] * carry + b3[i]
        carry = hs[i][SUBLANES - 1:, :] if down else hs[i][:1, :]
    return jnp.concatenate(hs, axis=0)


def _sigmoid(x):
    return 0.5 * jnp.tanh(0.5 * x) + 0.5


LOG2E = 1.4426950408889634
GELU_K = 0.7978845608028654
GELU_C = 0.044715


def _gelu(x):
    return 0.5 * x * (1.0 + jnp.tanh(GELU_K * (x + GELU_C * x * x * x)))


def _gelu_grad(x):
    t = jnp.tanh(GELU_K * (x + GELU_C * x * x * x))
    return 0.5 * (1.0 + t) + 0.5 * x * (1.0 - t * t) * GELU_K * (1.0 + 3.0 * GELU_C * x * x)


ROPE_HALF = QK_ROPE // 2
ROPE_LANE = 32


def _head_tile(nope, rope):
    z = lambda n: jnp.zeros(nope.shape[:-1] + (n,), nope.dtype)
    r1, r2 = (z(ROPE_HALF), z(ROPE_HALF)) if rope is None else (rope[..., :ROPE_HALF], rope[..., ROPE_HALF:])
    return jnp.concatenate([nope[..., :ROPE_LANE], r1, nope[..., ROPE_LANE:], z(ROPE_HALF), r2, z(ROPE_HALF)], axis=-1)


def _from_head_tile(t):
    a, b = ROPE_LANE + ROPE_HALF, ROPE_LANE + LANES // 2
    return (jnp.concatenate([t[..., :ROPE_LANE], t[..., a:a + QK_NOPE - ROPE_LANE]], axis=-1),
            jnp.concatenate([t[..., ROPE_LANE:a], t[..., b:b + ROPE_HALF]], axis=-1))


def _rope_partner(x):
    lane = lax.broadcasted_iota(jnp.int32, x.shape, 1) & (LANES // 2 - 1)
    return jnp.where((lane >= ROPE_LANE) & (lane < ROPE_LANE + ROPE_HALF), pltpu.roll(x, LANES // 2, 1), 0.0)


def _rope(x, cos_t, sin_t):
    return x * cos_t + _rope_partner(x) * sin_t


def _rope_t(dy, cos_t, sin_t):
    return dy * cos_t + _rope_partner(dy * sin_t)


def _rowwise(body, name, s, tm, rows=(), halos=(), fulls=(), outs=(), accs=()):
    n = s // tm
    hb = tm // SUBLANES
    last8 = s // SUBLANES - 1
    in_specs, args = [], []
    for a in rows:
        in_specs.append(pl.BlockSpec((tm, a.shape[1]), lambda i: (i, 0)))
        args.append(a)
    for a in halos:
        in_specs.append(pl.BlockSpec((SUBLANES, a.shape[1]), lambda i: (jnp.maximum(i * hb - 1, 0), 0)))
        in_specs.append(pl.BlockSpec((SUBLANES, a.shape[1]), lambda i: (jnp.minimum((i + 1) * hb, last8), 0)))
        args += [a, a]
    for a in fulls:
        in_specs.append(pl.BlockSpec(a.shape, lambda i, nd=a.ndim: (0,) * nd))
        args.append(a)
    out_shape, out_specs = [], []
    for c, dt in outs:
        out_shape.append(jax.ShapeDtypeStruct((s, c), dt))
        out_specs.append(pl.BlockSpec((tm, c), lambda i: (i, 0)))
    for shp, dt in accs:
        out_shape.append(jax.ShapeDtypeStruct(shp, dt))
        out_specs.append(pl.BlockSpec(shp, lambda i, nd=len(shp): (0,) * nd))

    def kern(*refs):
        body(pl.program_id(0), n, *refs)

    return pl.pallas_call(kern, grid=(n,), in_specs=in_specs, out_specs=out_specs, out_shape=out_shape, name=name,
                          compiler_params=_cparams(1))(*args)


def _matmul_tn(a, b, name, col_shards=False, out_dtype=F32):
    t, m = a.shape
    n = b.shape[1]
    bm = m
    for cand in range(LANES, m + 1, LANES):
        if m % cand == 0 and cand * (n // N_CHIPS if col_shards else min(n, 2048)) * 4 <= 6 * 1024 * 1024:
            bm = cand
    bn = n // N_CHIPS if col_shards else (n if n <= 2048 else 1408)
    bt = min(t, 2048 if max(bm, bn) <= 512 else (1024 if max(bm, bn) <= 1024 else 512))
    nt = t // bt

    def kern(a_ref, b_ref, o_ref, acc_ref):
        k = pl.program_id(2)

        @pl.when(k == 0)
        def _():
            acc_ref[...] = jnp.zeros_like(acc_ref)
        acc_ref[...] += _bdot_tn(a_ref[...], b_ref[...])

        @pl.when(k == nt - 1)
        def _():
            o_ref[...] = acc_ref[...].astype(out_dtype)

    if col_shards:
        out_spec = pl.BlockSpec((None, bm, bn), lambda i, j, k: (j, i, 0))
        out_shape = jax.ShapeDtypeStruct((N_CHIPS, m, bn), out_dtype)
    else:
        out_spec = pl.BlockSpec((bm, bn), lambda i, j, k: (i, j))
        out_shape = jax.ShapeDtypeStruct((m, n), out_dtype)
    return pl.pallas_call(
        kern, grid=(m // bm, n // bn, nt),
        in_specs=[pl.BlockSpec((bt, bm), lambda i, j, k: (k, i)), pl.BlockSpec((bt, bn), lambda i, j, k: (k, j))],
        out_specs=out_spec, out_shape=out_shape, scratch_shapes=[pltpu.VMEM((bm, bn), F32)], name=name,
        compiler_params=_cparams(3))(a, b)


def _in_proj(x, g, w_in_p, tm):
    def body(i, n, x_ref, g_ref, w_ref, xr, yg, cq, ckv, krp, hb):
        xv = x_ref[...]
        h = (xv * _rstd(xv) * g_ref[...]).astype(BF16)
        hb[...] = h
        p = jnp.dot(h, w_ref[...], preferred_element_type=F32)
        xr[...] = p[:, :LRU_W]
        yg[...] = p[:, LRU_W:2 * LRU_W]
        cq[...] = p[:, 2 * LRU_W:2 * LRU_W + Q_LORA]
        ckv[...] = p[:, 2 * LRU_W + Q_LORA:OFF_KR]
        krp[...] = p[:, OFF_KR:IN_PAD]

    return _rowwise(body, "in_proj", x.shape[0], tm, rows=[x], fulls=[g, w_in_p],
                    outs=[(LRU_W, F32), (LRU_W, F32), (Q_LORA, F32), (KV_LORA, F32), (LANES, F32), (D_MODEL, BF16)])


def _softplus_neg(lam):
    e = jnp.exp(-jnp.abs(lam))
    return jnp.maximum(-lam, 0.0) + jnp.where(e < 1e-2, e * (1.0 - e * (0.5 - e * (1.0 / 3.0))), jnp.log(1.0 + e))


def _lru_gates(x, halo, cw_ref, pv_ref, wa_ref, wi_ref, rev):
    down = not rev
    xc = pv_ref[0:1, :] + jnp.zeros_like(x)
    for j in range(CONV_W):
        k = j if rev else CONV_W - 1 - j
        xc = xc + cw_ref[k:k + 1, :] * _shift(x, j, halo, down)
    r = _sigmoid(_bdot(xc, wa_ref[...]) + pv_ref[1:2, :])
    ig = _sigmoid(_bdot(xc, wi_ref[...]) + pv_ref[2:3, :])
    lam = pv_ref[3:4, :]
    sp = _softplus_neg(lam)
    log_a = (-LRU_C) * r * sp
    a = jnp.exp(log_a)
    z = 2.0 * log_a
    series = -(z * (1.0 + z * (0.5 + z * (1.0 / 6.0 + z * (1.0 / 24.0)))))
    om = jnp.where(z > -0.02, series, 1.0 - a * a)
    mult = jnp.sqrt(om)
    return xc, r, ig, sp, a, mult


def _lru_scan_fwd(xr, cw, pv, wa, wi, rev, t):
    s = xr.shape[0]
    n = s // t
    hb = t // SUBLANES
    last8 = s // SUBLANES - 1
    down = not rev

    def kern(x_ref, halo_ref, cw_ref, pv_ref, wa_ref, wi_ref, h_ref, xc_ref, r_ref, ig_ref, a_ref, mult_ref, carry_ref):
        i = pl.program_id(0)
        _zero_first(i, carry_ref)
        halo = jnp.where(i == 0, 0.0, halo_ref[...])
        xc, r, ig, sp, a, mult = _lru_gates(x_ref[...], halo, cw_ref, pv_ref, wa_ref, wi_ref, rev)
        xc_ref[...], r_ref[...], ig_ref[...], a_ref[...], mult_ref[...] = xc, r, ig, a, mult
        h_ref[...] = _scan(a, mult * ig * xc, carry_ref[...], down)
        carry_ref[...] = h_ref[pl.ds(t - 1 if down else 0, 1), :]

    if rev:
        blk = lambda i: (n - 1 - i, 0)
        hal = lambda i: (jnp.minimum((n - i) * hb, last8), 0)
    else:
        blk = lambda i: (i, 0)
        hal = lambda i: (jnp.maximum(i * hb - 1, 0), 0)
    full = lambda a: pl.BlockSpec(a.shape, lambda i: (0, 0))
    return pl.pallas_call(
        kern, grid=(n,),
        in_specs=[pl.BlockSpec((t, LRU_W), blk), pl.BlockSpec((SUBLANES, LRU_W), hal), full(cw), full(pv), full(wa), full(wi)],
        out_specs=[pl.BlockSpec((t, LRU_W), blk)] * 6, out_shape=[jax.ShapeDtypeStruct((s, LRU_W), F32)] * 6,
        scratch_shapes=[pltpu.VMEM((1, LRU_W), F32)], name="lru_scan_rev" if rev else "lru_scan_fwd",
        compiler_params=_cparams(1))(xr, xr, cw, pv, wa, wi)


def _lru_scan_bwd(xr, saved, h, dh, cw, pv, wa, wi, rev, t):
    s = xr.shape[0]
    n = s // t
    hb = t // SUBLANES
    last8 = s // SUBLANES - 1
    down = not rev

    def kern(x_ref, xc_ref, r_ref, ig_ref, a_ref, mult_ref, h_ref, hh_ref, dh_ref, cw_ref, pv_ref, wa_ref, wi_ref,
             dx_ref, gwa_ref, gwi_ref, gv_ref, p_ref, dxc_halo_ref, tmp_ref):
        i = pl.program_id(0)
        _zero_first(i, gwa_ref, gwi_ref, gv_ref, p_ref, dxc_halo_ref)
        at_start = i == n - 1
        x = x_ref[...]
        hhalo = jnp.where(at_start, 0.0, hh_ref[...])
        xc, r, ig, a, mult = xc_ref[...], r_ref[...], ig_ref[...], a_ref[...], mult_ref[...]
        lam = pv_ref[3:4, :]
        sp = _softplus_neg(lam)
        h_prev = _shift(h_ref[...], 1, hhalo, down)
        row = lax.broadcasted_iota(jnp.int32, x.shape, 0)
        edge = t - 1 if down else 0
        dh_mod = dh_ref[...] + jnp.where(row == edge, p_ref[...], 0.0)
        a_next = _shift(a, 1, jnp.zeros((SUBLANES, LRU_W), F32), not down)
        g = _scan(a_next, dh_mod, jnp.zeros((1, LRU_W), F32), not down)
        tmp_ref[...] = a * g
        p_ref[...] = tmp_ref[pl.ds(0 if down else t - 1, 1), :]
        da = g * h_prev
        d_ig = g * mult * xc
        d_xc = g * mult * ig
        d_om = g * ig * xc * (0.5 / jnp.maximum(mult, 1e-30))
        d_log_a = da * a - 2.0 * d_om * a * a
        d_r = d_log_a * ((-LRU_C) * sp)
        d_sp = jnp.sum(d_log_a * ((-LRU_C) * r), axis=0, keepdims=True)
        gv_ref[7:8, :] += d_sp * (-_sigmoid(-lam))
        d_ga = d_r * r * (1.0 - r)
        d_gi = d_ig * ig * (1.0 - ig)
        _acc_row(gv_ref, 5, d_ga)
        _acc_row(gv_ref, 6, d_gi)
        d_xc = d_xc + _bdot_nt(d_ga, wa_ref[...]) + _bdot_nt(d_gi, wi_ref[...])
        gwa_ref[...] += _bdot_tn(xc, d_ga)
        gwi_ref[...] += _bdot_tn(xc, d_gi)
        _acc_row(gv_ref, 4, d_xc)
        dx = jnp.zeros_like(x)
        dxc_halo = dxc_halo_ref[...]
        for j in range(CONV_W):
            k = j if rev else CONV_W - 1 - j
            d_shift = _shift(d_xc, j, dxc_halo, not down)
            _acc_row(gv_ref, k, d_shift * x)
            dx = dx + cw_ref[k:k + 1, :] * d_shift
        dx_ref[...] = dx
        dxc_halo_ref[...] = d_xc[:SUBLANES] if down else d_xc[t - SUBLANES:]

    if rev:
        blk = lambda i: (i, 0)
        hal = lambda i: (jnp.minimum((i + 1) * hb, last8), 0)
    else:
        blk = lambda i: (n - 1 - i, 0)
        hal = lambda i: (jnp.maximum((n - 1 - i) * hb - 1, 0), 0)
    full = lambda a: pl.BlockSpec(a.shape, lambda i: (0, 0))
    bs = pl.BlockSpec((t, LRU_W), blk)
    hs = pl.BlockSpec((SUBLANES, LRU_W), hal)
    return pl.pallas_call(
        kern, grid=(n,),
        in_specs=[bs] * 7 + [hs, bs, full(cw), full(pv), full(wa), full(wi)],
        out_specs=[bs, pl.BlockSpec((LRU_W, LRU_W), lambda i: (0, 0)), pl.BlockSpec((LRU_W, LRU_W), lambda i: (0, 0)),
                   pl.BlockSpec((SUBLANES, LRU_W), lambda i: (0, 0))],
        out_shape=[jax.ShapeDtypeStruct((s, LRU_W), F32), jax.ShapeDtypeStruct((LRU_W, LRU_W), F32),
                   jax.ShapeDtypeStruct((LRU_W, LRU_W), F32), jax.ShapeDtypeStruct((SUBLANES, LRU_W), F32)],
        scratch_shapes=[pltpu.VMEM((1, LRU_W), F32), pltpu.VMEM((SUBLANES, LRU_W), F32), pltpu.VMEM((t, LRU_W), F32)],
        name="lru_bwd_rev" if rev else "lru_bwd_fwd", compiler_params=_cparams(1))(xr, *saved, h, h, dh, cw, pv, wa, wi)


def _mla_qkv(cq, ckv, krp, cos_t, sin_t, g_qa, g_kva, g_qn, g_kn, w_uq_p, w_uk_p, w_uv, tm):
    scale = QK_HEAD ** -0.5 * LOG2E

    def body(i, n, cq_ref, ckv_ref, kr_ref, c_ref, s_ref, gqa, gkva, gqn, gkn, wq, wk, wv, q_out, k_out, v_out):
        cosv, sinv = c_ref[...], s_ref[...]
        cqv = cq_ref[...]
        qr = _bdot(cqv * _rstd(cqv) * gqa[...], wq[...])
        ckvv = ckv_ref[...]
        c_kv = (ckvv * _rstd(ckvv) * gkva[...]).astype(BF16)
        kn = jnp.dot(c_kv, wk[...], preferred_element_type=F32)
        v_out[...] = jnp.dot(c_kv, wv[...], preferred_element_type=F32).astype(BF16)
        kr = kr_ref[...]
        kr_swapped = _rope_partner(kr * gkn[...]) * sinv
        for h in range(HEADS):
            sl = slice(h * LANES, (h + 1) * LANES)
            qh = qr[:, sl]
            qh = _rope(qh * _rstd(qh, QK_HEAD) * gqn[...], cosv, sinv) * scale
            q_out[:, sl] = qh.astype(BF16)
            kh = kn[:, sl] + kr
            rs = _rstd(kh, QK_HEAD)
            k_out[:, sl] = (kh * rs * gkn[...] * cosv + kr_swapped * rs).astype(BF16)

    return _rowwise(body, "mla_qkv", cq.shape[0], tm, rows=[cq, ckv, krp, cos_t, sin_t],
                    fulls=[g_qa, g_kva, g_qn, g_kn, w_uq_p, w_uk_p, w_uv],
                    outs=[(HEADS * LANES, BF16), (HEADS * LANES, BF16), (MLA_W, BF16)])


NT_DIMS = (((1,), (1,)), ((), ()))
TN_DIMS = (((0,), (0,)), ((), ()))


def _riding_exchange(copies_fn, first, last):
    @pl.when(first)
    def _():
        for cp in copies_fn():
            cp.start()

    def finish():
        @pl.when(last)
        def _():
            for cp in copies_fn():
                cp.wait()
    return finish


def _attn_fwd(q, k, v, tq, tk, shards=()):
    s = q.shape[0]
    nq, nk = s // tq, s // tk
    n = len(shards)

    def kern(*refs):
        q_ref, k_ref, v_ref = refs[:3]
        o_ref, lse_ref = refs[3 + n:5 + n]
        acc_ref = refs[5 + 2 * n]
        p_id, i_id = pl.program_id(0), pl.program_id(1)
        finish = _riding_exchange(lambda: _gather_copies(refs[3:3 + n], refs[5 + n:5 + 2 * n], *refs[6 + 2 * n:]),
                                  (p_id == 0) & (i_id == 0), (p_id == HEADS // 2 - 1) & (i_id == nq - 1)) if n else None
        qs = (q_ref[:, :LANES], q_ref[:, LANES:])
        acc_ref[...] = jnp.zeros_like(acc_ref)

        def step(j, carry):
            off = pl.multiple_of(j * tk, tk)
            vc = v_ref[pl.ds(off, tk), :]
            out = []
            for h in range(2):
                m, l = carry[2 * h:2 * h + 2]
                st = lax.dot_general(k_ref[pl.ds(off, tk), h * LANES:(h + 1) * LANES], qs[h], NT_DIMS,
                                     preferred_element_type=F32)
                mn = jnp.maximum(m, jnp.max(st, axis=0, keepdims=True))
                al = jnp.exp2(m - mn)
                pt = jnp.exp2(st - mn)
                l = al * l + jnp.sum(pt, axis=0, keepdims=True)
                acc_ref[h] = al * acc_ref[h] + lax.dot_general(vc, pt.astype(BF16), TN_DIMS, preferred_element_type=F32)
                out += [mn, l]
            return tuple(out)

        init = (jnp.full((1, tq), -1e30, F32), jnp.zeros((1, tq), F32)) * 2
        m0, l0, m1, l1 = lax.fori_loop(0, nk, step, init)
        row = lax.broadcasted_iota(jnp.int32, (LANES, tq), 0)
        o_ref[...] = jnp.where(row < V_DIM, acc_ref[0] / l0, acc_ref[1] / l1).T
        lse_ref[0, 0:1, :] = m0 + jnp.log2(l0)
        lse_ref[0, 1:2, :] = m1 + jnp.log2(l1)
        if n:
            finish()

    return pl.pallas_call(
        kern, grid=(HEADS // 2, nq),
        in_specs=[pl.BlockSpec((tq, 2 * LANES), lambda p, i: (i, p)), pl.BlockSpec((s, 2 * LANES), lambda p, i: (0, p)),
                  pl.BlockSpec((s, LANES), lambda p, i: (0, p))] + [ANY] * n,
        out_specs=[pl.BlockSpec((tq, LANES), lambda p, i: (i, p)), pl.BlockSpec((1, 2, tq), lambda p, i: (p, 0, i))]
        + [ANY] * n,
        out_shape=[jax.ShapeDtypeStruct((s, MLA_W), F32), jax.ShapeDtypeStruct((HEADS // 2, 2, s), F32)]
        + _gather_shapes(shards),
        scratch_shapes=[pltpu.VMEM((2, LANES, tq), F32)] + (_gather_sems(n) if n else []),
        name="attn_fwd", compiler_params=_cparams(2))(q, k, v, *shards)


def _attn_bwd(q, k, v, do, lse, delta, tq, tk, contributions=()):
    s = q.shape[0]
    nq, nk = s // tq, s // tk
    n = len(contributions)

    def kern(*refs):
        q_ref, do_ref, lse_ref, dl_ref, k_ref, v_ref = refs[:6]
        dq_ref, dk_ref, dv_ref = refs[6 + n:9 + n]
        acc_ref = refs[9 + 2 * n]
        p_id, i_id = pl.program_id(0), pl.program_id(1)
        finish = _riding_exchange(lambda: _to_owner_copies(refs[6:6 + n], refs[9 + n:9 + 2 * n], *refs[10 + 2 * n:]),
                                  (p_id == 0) & (i_id == 0), (p_id == HEADS // 2 - 1) & (i_id == nq - 1)) if n else None
        _zero_first(pl.program_id(1), dk_ref, dv_ref)
        acc_ref[...] = jnp.zeros_like(acc_ref)
        qs = (q_ref[:, :LANES], q_ref[:, LANES:])
        doc = do_ref[...]
        lane_q = lax.broadcasted_iota(jnp.int32, (tq, LANES), 1)
        zq = jnp.zeros_like(doc)
        dos = (jnp.where(lane_q < V_DIM, doc, zq), jnp.where(lane_q >= V_DIM, doc, zq))
        lses = (lse_ref[0, 0:1, :], lse_ref[0, 1:2, :])
        dls = (dl_ref[0, 0:1, :], dl_ref[0, 1:2, :])

        def step(j, carry):
            off = pl.multiple_of(j * tk, tk)
            vp = v_ref[pl.ds(off, tk), :]
            lane_k = lax.broadcasted_iota(jnp.int32, (tk, LANES), 1)
            zero = jnp.zeros_like(vp)
            vs = (jnp.where(lane_k < V_DIM, vp, zero), jnp.where(lane_k >= V_DIM, vp, zero))
            for h in range(2):
                sl = slice(h * LANES, (h + 1) * LANES)
                st = lax.dot_general(k_ref[pl.ds(off, tk), sl], qs[h], NT_DIMS, preferred_element_type=F32)
                pt = jnp.exp2(st - lses[h])
                dpt = lax.dot_general(vs[h], doc, NT_DIMS, preferred_element_type=F32)
                dst = (pt * (dpt - dls[h])).astype(BF16)
                dv_ref[pl.ds(off, tk), :] += jnp.dot(pt.astype(BF16), dos[h], preferred_element_type=F32)
                dk_ref[pl.ds(off, tk), sl] += jnp.dot(dst, qs[h], preferred_element_type=F32)
                acc_ref[h] += lax.dot_general(k_ref[pl.ds(off, tk), sl], dst, TN_DIMS, preferred_element_type=F32)
            return carry

        lax.fori_loop(0, nk, step, 0)
        dq_ref[:, :LANES] = acc_ref[0].T
        dq_ref[:, LANES:] = acc_ref[1].T
        if n:
            finish()

    return pl.pallas_call(
        kern, grid=(HEADS // 2, nq),
        in_specs=[pl.BlockSpec((tq, 2 * LANES), lambda p, i: (i, p)), pl.BlockSpec((tq, LANES), lambda p, i: (i, p)),
                  pl.BlockSpec((1, 2, tq), lambda p, i: (p, 0, i)), pl.BlockSpec((1, 2, tq), lambda p, i: (p, 0, i)),
                  pl.BlockSpec((s, 2 * LANES), lambda p, i: (0, p)), pl.BlockSpec((s, LANES), lambda p, i: (0, p))]
        + [ANY] * n,
        out_specs=[pl.BlockSpec((tq, 2 * LANES), lambda p, i: (i, p)), pl.BlockSpec((s, 2 * LANES), lambda p, i: (0, p)),
                   pl.BlockSpec((s, LANES), lambda p, i: (0, p))] + [ANY] * n,
        out_shape=[jax.ShapeDtypeStruct((s, HEADS * LANES), F32), jax.ShapeDtypeStruct((s, HEADS * LANES), F32),
                   jax.ShapeDtypeStruct((s, MLA_W), F32)] + _to_owner_shapes(contributions),
        scratch_shapes=[pltpu.VMEM((2, LANES, tq), F32)] + (_to_owner_sems(n) if n else []),
        name="attn_bwd", compiler_params=_cparams(2))(q, do, lse, delta, k, v, *contributions)


def _mix_out(hf, hb, yg, o, x, g_lru, g_mla, w_out, tm):
    def body(i, n, hf_ref, hb_ref, yg_ref, o_ref, x_ref, gl, gm, w_ref, x1_ref, mix_ref):
        lo = (hf_ref[...] + hb_ref[...]) * _gelu(yg_ref[...])
        ov = o_ref[...]
        mix_ref[:, :LRU_W] = (lo * _rstd(lo) * gl[...]).astype(BF16)
        mix_ref[:, LRU_W:] = (ov * _rstd(ov) * gm[...]).astype(BF16)
        x1_ref[...] = x_ref[...] + jnp.dot(mix_ref[...], w_ref[...], preferred_element_type=F32)

    return _rowwise(body, "mix_out", x.shape[0], tm, rows=[hf, hb, yg, o, x], fulls=[g_lru, g_mla, w_out],
                    outs=[(D_MODEL, F32), (2 * LRU_W, BF16)])


def _mem_kv(mem, g_mem, w_kv, g_k):
    m = mem.shape[0]

    def body(i, n, mem_ref, g_ref, w_ref, gk_ref, km_ref, vm_ref):
        mv = mem_ref[...]
        kv = _bdot(mv * _rstd(mv) * g_ref[...], w_ref[...])
        vm_ref[...] = kv[:, MEM_W:].astype(BF16)
        for h in range(MEM_HEADS):
            sl = slice(h * MEM_HD, (h + 1) * MEM_HD)
            kh = kv[:, sl]
            km_ref[:, sl] = (kh * _rstd(kh) * gk_ref[...]).astype(BF16)

    return _rowwise(body, "mem_kv", m, m, rows=[mem], fulls=[g_mem, w_kv, g_k], outs=[(MEM_W, BF16), (MEM_W, BF16)])


def _mem_attn_core(x1v, g_ref, wq_ref, gq_ref, km_ref, vm_ref):
    scale = MEM_HD ** -0.5
    hm = (x1v * _rstd(x1v) * g_ref[...]).astype(BF16)
    qr = jnp.dot(hm, wq_ref[...], preferred_element_type=F32)
    heads = []
    for h in range(MEM_HEADS):
        sl = slice(h * MEM_HD, (h + 1) * MEM_HD)
        qh = qr[:, sl]
        rs = _rstd(qh)
        qn = (qh * rs * gq_ref[...]).astype(BF16)
        sc = lax.dot_general(qn, km_ref[:, sl], (((1,), (1,)), ((), ())), preferred_element_type=F32) * scale
        e = jnp.exp(sc - jnp.max(sc, axis=-1, keepdims=True))
        p = e / jnp.sum(e, axis=-1, keepdims=True)
        oh = jnp.dot(p.astype(BF16), vm_ref[:, sl], preferred_element_type=F32)
        heads.append((qh, rs, qn, p, oh))
    return hm, heads


def _mem_attn(x1, g, w_q, g_q, km, vm, w_o, tm):
    cs = D_MODEL // N_CHIPS

    def body(i, n, x1_ref, g_ref, wq_ref, gq_ref, km_ref, vm_ref, wo_ref, x2_ref, ob_ref):
        x1v = x1_ref[...]
        _, heads = _mem_attn_core(x1v, g_ref, wq_ref, gq_ref, km_ref, vm_ref)
        for h in range(MEM_HEADS):
            ob_ref[:, h * MEM_HD:(h + 1) * MEM_HD] = heads[h][4].astype(BF16)
        for k in range(N_CHIPS):
            sl = slice(k * cs, (k + 1) * cs)
            x2_ref[:, sl] = x1v[:, sl] + jnp.dot(ob_ref[...], wo_ref[k], preferred_element_type=F32)

    return _rowwise(body, "mem_attn", x1.shape[0], tm, rows=[x1], fulls=[g, w_q, g_q, km, vm, w_o],
                    outs=[(D_MODEL, F32), (MEM_W, BF16)])


def _ffn_up(x2, g, w_up, cw, tm):
    s = x2.shape[0]
    n = s // tm
    cs = 2 * D_FF // N_CHIPS
    hb8 = tm // SUBLANES
    last8 = s // SUBLANES - 1

    def kern(x_ref, pv_ref, nx_ref, g_ref, w_ref, cw_ref, gu_ref, guc_ref, hb_ref, hext_ref):
        i = pl.program_id(0)
        xv = x_ref[...]
        hb = (xv * _rstd(xv) * g_ref[...]).astype(BF16)
        hb_ref[...] = hb
        xh = jnp.concatenate([jnp.where(i == 0, 0.0, pv_ref[...]), jnp.where(i == n - 1, 0.0, nx_ref[...])], axis=0)
        hext_ref[:tm, :] = hb
        hext_ref[tm:, :] = (xh * _rstd(xh) * g_ref[...]).astype(BF16)
        for k in range(N_CHIPS):
            sl = slice(k * cs, (k + 1) * cs)
            ge = jnp.dot(hext_ref[...], w_ref[k], preferred_element_type=F32)
            gm = ge[:tm]
            gu_ref[:, sl] = gm
            guc_ref[:, sl] = (cw_ref[3:4, sl] + cw_ref[0:1, sl] * _shift_down(gm, 1, ge[tm:tm + SUBLANES])
                              + cw_ref[1:2, sl] * gm + cw_ref[2:3, sl] * _shift_up(gm, 1, ge[tm + SUBLANES:]))

    row = lambda c: pl.BlockSpec((tm, c), lambda i: (i, 0))
    return pl.pallas_call(
        kern, grid=(n,),
        in_specs=[row(D_MODEL),
                  pl.BlockSpec((SUBLANES, D_MODEL), lambda i: (jnp.maximum(i * hb8 - 1, 0), 0)),
                  pl.BlockSpec((SUBLANES, D_MODEL), lambda i: (jnp.minimum((i + 1) * hb8, last8), 0)),
                  pl.BlockSpec((1, D_MODEL), lambda i: (0, 0)),
                  pl.BlockSpec((N_CHIPS, D_MODEL, cs), lambda i: (0, 0, 0)),
                  pl.BlockSpec((SUBLANES, 2 * D_FF), lambda i: (0, 0))],
        out_specs=[row(2 * D_FF), row(2 * D_FF), row(D_MODEL)],
        out_shape=[jax.ShapeDtypeStruct((s, 2 * D_FF), F32), jax.ShapeDtypeStruct((s, 2 * D_FF), F32),
                   jax.ShapeDtypeStruct((s, D_MODEL), BF16)],
        scratch_shapes=[pltpu.VMEM((tm + 2 * SUBLANES, D_MODEL), BF16)], name="ffn_up", compiler_params=_cparams(1))(
            x2, x2, x2, g, w_up, cw)


def _ffn_down_loss(gu, x2, target, w_down, tm):
    def body(i, n, gu_ref, x_ref, t_ref, w_ref, dy_ref, dyb_ref, act_ref, dgu_ref, loss_ref):
        _zero_first(i, loss_ref)
        g, u = gu_ref[:, :D_FF], gu_ref[:, D_FF:]
        sg = _sigmoid(g)
        a = g * sg
        act_ref[...] = (a * u).astype(BF16)
        y = x_ref[...] + jnp.dot(act_ref[...], w_ref[...], preferred_element_type=F32)
        e = y - t_ref[...]
        loss_ref[...] += jnp.sum(e * e)
        dy = e * (1.0 / D_MODEL)
        dy_ref[...] = dy
        dyb_ref[...] = dy.astype(BF16)
        d_act = lax.dot_general(dyb_ref[...], w_ref[...], NT_DIMS, preferred_element_type=F32)
        dgu_ref[:, :D_FF] = (d_act * u) * (sg + a - a * sg)
        dgu_ref[:, D_FF:] = d_act * a

    return _rowwise(body, "ffn_down_loss", x2.shape[0], tm, rows=[gu, x2, target], fulls=[w_down],
                    outs=[(D_MODEL, F32), (D_MODEL, BF16), (D_FF, BF16), (2 * D_FF, F32)], accs=[((SUBLANES, LANES), F32)])


def _ffn_bwd_conv(dgu, gu_pre, cw, tm):
    def body(i, n, d_ref, g_ref, dp_ref, dn_ref, cw_ref, dpre_ref, gc_ref):
        _zero_first(i, gc_ref)
        d = d_ref[...]
        g = g_ref[...]
        d_next = _shift_up(d, 1, jnp.where(i == n - 1, 0.0, dn_ref[...]))
        d_prev = _shift_down(d, 1, jnp.where(i == 0, 0.0, dp_ref[...]))
        dpre_ref[...] = (cw_ref[0:1, :] * d_next + cw_ref[1:2, :] * d + cw_ref[2:3, :] * d_prev).astype(BF16)
        _acc_row(gc_ref, 0, d_next * g)
        _acc_row(gc_ref, 1, d * g)
        _acc_row(gc_ref, 2, d_prev * g)
        _acc_row(gc_ref, 3, d)

    return _rowwise(body, "ffn_bwd_conv", dgu.shape[0], tm, rows=[dgu, gu_pre], halos=[dgu], fulls=[cw],
                    outs=[(2 * D_FF, BF16)], accs=[((SUBLANES, 2 * D_FF), F32)])


def _ffn_bwd_in(dpre, x2, dy, g, w_up, tm):
    cs = 2 * D_FF // N_CHIPS

    def body(i, n, dp_ref, x_ref, dy_ref, g_ref, w_ref, dx_ref, dxb_ref, gg_ref):
        _zero_first(i, gg_ref)
        d_h = jnp.zeros(x_ref.shape, F32)
        for k in range(N_CHIPS):
            d_h = d_h + lax.dot_general(dp_ref[:, k * cs:(k + 1) * cs], w_ref[k], (((1,), (1,)), ((), ())),
                                        preferred_element_type=F32)
        xv = x_ref[...]
        dx, dg = _norm_bwd(xv, _rstd(xv), g_ref[...], d_h)
        _acc_row(gg_ref, 0, dg)
        dx = dx + dy_ref[...]
        dx_ref[...] = dx
        dxb_ref[...] = dx.astype(BF16)

    return _rowwise(body, "ffn_bwd_in", x2.shape[0], tm, rows=[dpre, x2, dy], fulls=[g, w_up],
                    outs=[(D_MODEL, F32), (D_MODEL, BF16)], accs=[((SUBLANES, D_MODEL), F32)])


def _mem_attn_bwd(x1, dx2, dx2b, g, w_q, g_q, km, vm, w_o, tm):
    scale = MEM_HD ** -0.5
    m = km.shape[0]

    def body(i, n, x1_ref, dx2_ref, dx2b_ref, g_ref, wq_ref, gq_ref, km_ref, vm_ref, wo_ref,
             dx1_ref, dx1b_ref, hm_ref, dqr_ref, dkm_ref, dvm_ref, gg_ref, ggq_ref):
        _zero_first(i, dkm_ref, dvm_ref, gg_ref, ggq_ref)
        x1v = x1_ref[...]
        hm, heads = _mem_attn_core(x1v, g_ref, wq_ref, gq_ref, km_ref, vm_ref)
        hm_ref[...] = hm
        cs = D_MODEL // N_CHIPS
        d_o = jnp.zeros((x1v.shape[0], MEM_W), F32)
        for k in range(N_CHIPS):
            d_o = d_o + lax.dot_general(dx2b_ref[:, k * cs:(k + 1) * cs], wo_ref[k], (((1,), (1,)), ((), ())),
                                        preferred_element_type=F32)
        for h in range(MEM_HEADS):
            sl = slice(h * MEM_HD, (h + 1) * MEM_HD)
            qh, rs, qn, p, _ = heads[h]
            d_oh = d_o[:, sl].astype(BF16)
            dp = lax.dot_general(d_oh, vm_ref[:, sl], (((1,), (1,)), ((), ())), preferred_element_type=F32)
            ds = (p * (dp - jnp.sum(dp * p, axis=-1, keepdims=True)) * scale).astype(BF16)
            dqn = jnp.dot(ds, km_ref[:, sl], preferred_element_type=F32)
            dkm_ref[:, sl] += lax.dot_general(ds, qn, (((0,), (0,)), ((), ())), preferred_element_type=F32)
            dvm_ref[:, sl] += lax.dot_general(p.astype(BF16), d_oh, (((0,), (0,)), ((), ())), preferred_element_type=F32)
            dqh, dgq = _norm_bwd(qh, rs, gq_ref[...], dqn)
            _acc_row(ggq_ref, 0, dgq)
            dqr_ref[:, sl] = dqh.astype(BF16)
        d_hm = lax.dot_general(dqr_ref[...], wq_ref[...], (((1,), (1,)), ((), ())), preferred_element_type=F32)
        dx, dg = _norm_bwd(x1v, _rstd(x1v), g_ref[...], d_hm)
        _acc_row(gg_ref, 0, dg)
        dx = dx + dx2_ref[...]
        dx1_ref[...] = dx
        dx1b_ref[...] = dx.astype(BF16)

    return _rowwise(body, "mem_attn_bwd", x1.shape[0], tm, rows=[x1, dx2, dx2b], fulls=[g, w_q, g_q, km, vm, w_o],
                    outs=[(D_MODEL, F32), (D_MODEL, BF16), (D_MODEL, BF16), (MEM_W, BF16)],
                    accs=[((m, MEM_W), F32), ((m, MEM_W), F32), ((SUBLANES, D_MODEL), F32), ((SUBLANES, MEM_HD), F32)])


def _mem_kv_bwd(mem, g_mem, w_kv, g_k, dkm, dvm):
    m = mem.shape[0]

    def body(i, n, mem_ref, dkm_ref, dvm_ref, g_ref, w_ref, gk_ref, gw_ref, gg_ref, ggk_ref, dkv_ref):
        gg_ref[...] = jnp.zeros_like(gg_ref)
        ggk_ref[...] = jnp.zeros_like(ggk_ref)
        mv = mem_ref[...]
        rs_m = _rstd(mv)
        mem_n = (mv * rs_m * g_ref[...]).astype(BF16)
        kv = jnp.dot(mem_n, w_ref[...], preferred_element_type=F32)
        for h in range(MEM_HEADS):
            sl = slice(h * MEM_HD, (h + 1) * MEM_HD)
            kh = kv[:, sl]
            dkh, dgk = _norm_bwd(kh, _rstd(kh), gk_ref[...], dkm_ref[:, sl])
            _acc_row(ggk_ref, 0, dgk)
            dkv_ref[:, sl] = dkh.astype(BF16)
        dkv_ref[:, MEM_W:] = dvm_ref[...].astype(BF16)
        gw_ref[...] = lax.dot_general(mem_n, dkv_ref[...], (((0,), (0,)), ((), ())), preferred_element_type=F32)
        d_mn = lax.dot_general(dkv_ref[...], w_ref[...], (((1,), (1,)), ((), ())), preferred_element_type=F32)
        _acc_row(gg_ref, 0, d_mn * (mv * rs_m))

    return _rowwise(body, "mem_kv_bwd", m, m, rows=[mem, dkm, dvm], fulls=[g_mem, w_kv, g_k],
                    accs=[((D_MODEL, 2 * MEM_W), F32), ((SUBLANES, D_MODEL), F32), ((SUBLANES, MEM_HD), F32),
                          ((m, 2 * MEM_W), BF16)])


def _mix_out_bwd(dx1b, hf, hb, yg, o, g_lru, g_mla, w_out, tm):
    def body(i, n, dx_ref, hf_ref, hb_ref, yg_ref, o_ref, gl, gm, w_ref, dh_ref, dyg_ref, dob_ref, dl_ref, ggl_ref, ggm_ref):
        _zero_first(i, ggl_ref, ggm_ref)
        dmix = lax.dot_general(dx_ref[...], w_ref[...], (((1,), (1,)), ((), ())), preferred_element_type=F32)
        hs = hf_ref[...] + hb_ref[...]
        ygv = yg_ref[...]
        ge = _gelu(ygv)
        lo = hs * ge
        d_lo, dgl = _norm_bwd(lo, _rstd(lo), gl[...], dmix[:, :LRU_W])
        _acc_row(ggl_ref, 0, dgl)
        dh_ref[...] = d_lo * ge
        dyg_ref[...] = d_lo * hs * _gelu_grad(ygv)
        ov = o_ref[...]
        d_o, dgm = _norm_bwd(ov, _rstd(ov), gm[...], dmix[:, LRU_W:])
        _acc_row(ggm_ref, 0, dgm)
        dob_ref[...] = d_o.astype(BF16)
        prod = d_o * ov
        lane_w = lax.broadcasted_iota(jnp.int32, prod.shape, 1)
        lane = lax.broadcasted_iota(jnp.int32, (prod.shape[0], LANES), 1)
        dl = jnp.zeros((prod.shape[0], LANES), F32)
        for h in range(HEADS):
            in_head = (lane_w >= h * V_DIM) & (lane_w < (h + 1) * V_DIM)
            dl = dl + jnp.where(lane == h, jnp.sum(jnp.where(in_head, prod, 0.0), axis=-1, keepdims=True), 0.0)
        dl_ref[...] = dl

    return _rowwise(body, "mix_out_bwd", dx1b.shape[0], tm, rows=[dx1b, hf, hb, yg, o], fulls=[g_lru, g_mla, w_out],
                    outs=[(LRU_W, F32), (LRU_W, F32), (MLA_W, BF16), (LANES, F32)],
                    accs=[((SUBLANES, LRU_W), F32), ((SUBLANES, MLA_W), F32)])


def _mla_qkv_bwd(cq, ckv, krp, cos_t, sin_t, dq, dk, dv, g_qa, g_kva, g_qn, g_kn, w_uq_p, w_uk_p, w_uv, tm):
    scale = QK_HEAD ** -0.5

    def body(i, n, cq_ref, ckv_ref, kr_ref, c_ref, s_ref, dq_ref, dk_ref, dv_ref, gqa, gkva, gqn, gkn, wq, wk, wv,
             dcq_ref, dckv_ref, dkr_ref, cqb_ref, dqr_ref, ckvb_ref, dkn_ref, dvb_ref, ggqa, ggkva, ggqn, ggkn):
        _zero_first(i, ggqa, ggkva, ggqn, ggkn)
        cosv, sinv = c_ref[...], s_ref[...]
        cqv = cq_ref[...]
        rs_q = _rstd(cqv)
        cqb_ref[...] = (cqv * rs_q * gqa[...]).astype(BF16)
        qr = jnp.dot(cqb_ref[...], wq[...], preferred_element_type=F32)
        ckvv = ckv_ref[...]
        rs_kv = _rstd(ckvv)
        ckvb_ref[...] = (ckvv * rs_kv * gkva[...]).astype(BF16)
        kn = jnp.dot(ckvb_ref[...], wk[...], preferred_element_type=F32)
        kr = kr_ref[...]
        dkr = jnp.zeros_like(kr)
        for h in range(HEADS):
            sl = slice(h * LANES, (h + 1) * LANES)
            qh = qr[:, sl]
            d_qn = _rope_t(dq_ref[:, sl] * scale, cosv, sinv)
            dqh, dgq = _norm_bwd(qh, _rstd(qh, QK_HEAD), gqn[...], d_qn, QK_HEAD)
            _acc_row(ggqn, 0, dgq)
            dqr_ref[:, sl] = dqh.astype(BF16)
            kh = kn[:, sl] + kr
            d_kn = _rope_t(dk_ref[:, sl] * (1.0 / LOG2E), cosv, sinv)
            dkh, dgk = _norm_bwd(kh, _rstd(kh, QK_HEAD), gkn[...], d_kn, QK_HEAD)
            _acc_row(ggkn, 0, dgk)
            dkn_ref[:, sl] = dkh.astype(BF16)
            dkr = dkr + dkh
        dkr_ref[...] = dkr
        dvb_ref[...] = dv_ref[...].astype(BF16)
        d_cq = lax.dot_general(dqr_ref[...], wq[...], (((1,), (1,)), ((), ())), preferred_element_type=F32)
        dcq, dg = _norm_bwd(cqv, rs_q, gqa[...], d_cq)
        _acc_row(ggqa, 0, dg)
        dcq_ref[...] = dcq
        d_ckv = (lax.dot_general(dkn_ref[...], wk[...], (((1,), (1,)), ((), ())), preferred_element_type=F32)
                 + lax.dot_general(dvb_ref[...], wv[...], (((1,), (1,)), ((), ())), preferred_element_type=F32))
        dckv, dg = _norm_bwd(ckvv, rs_kv, gkva[...], d_ckv)
        _acc_row(ggkva, 0, dg)
        dckv_ref[...] = dckv

    return _rowwise(body, "mla_qkv_bwd", cq.shape[0], tm, rows=[cq, ckv, krp, cos_t, sin_t, dq, dk, dv],
                    fulls=[g_qa, g_kva, g_qn, g_kn, w_uq_p, w_uk_p, w_uv],
                    outs=[(Q_LORA, F32), (KV_LORA, F32), (LANES, F32), (Q_LORA, BF16), (HEADS * LANES, BF16),
                          (KV_LORA, BF16), (HEADS * LANES, BF16), (MLA_W, BF16)],
                    accs=[((SUBLANES, Q_LORA), F32), ((SUBLANES, KV_LORA), F32), ((SUBLANES, LANES), F32),
                          ((SUBLANES, LANES), F32)])


def _in_proj_bwd(x, dx1, dxr_f, dxr_b, dyg, dcq, dckv, dkrp, g, w_in_p, tm):
    def body(i, n, x_ref, dx1_ref, df_ref, db_ref, dyg_ref, dcq_ref, dckv_ref, dkr_ref, g_ref, w_ref, gx_ref, dp_ref, gg_ref):
        _zero_first(i, gg_ref)
        dp_ref[:, :LRU_W] = (df_ref[...] + db_ref[...]).astype(BF16)
        dp_ref[:, LRU_W:2 * LRU_W] = dyg_ref[...].astype(BF16)
        dp_ref[:, 2 * LRU_W:2 * LRU_W + Q_LORA] = dcq_ref[...].astype(BF16)
        dp_ref[:, 2 * LRU_W + Q_LORA:OFF_KR] = dckv_ref[...].astype(BF16)
        dp_ref[:, OFF_KR:] = dkr_ref[...].astype(BF16)
        d_h = lax.dot_general(dp_ref[...], w_ref[...], (((1,), (1,)), ((), ())), preferred_element_type=F32)
        xv = x_ref[...]
        dx, dg = _norm_bwd(xv, _rstd(xv), g_ref[...], d_h)
        _acc_row(gg_ref, 0, dg)
        gx_ref[...] = dx + dx1_ref[...]

    return _rowwise(body, "in_proj_bwd", x.shape[0], tm, rows=[x, dx1, dxr_f, dxr_b, dyg, dcq, dckv, dkrp],
                    fulls=[g, w_in_p], outs=[(D_MODEL, F32), (IN_PAD, BF16)], accs=[((SUBLANES, D_MODEL), F32)])


ANY = pl.BlockSpec(memory_space=pl.ANY)


def _chip_peers(x, y):
    return ((1 - x, y), (x, 1 - y), (1 - x, 1 - y))


def _exchange_call(kern, name, ins, out_shapes, n_sems, aliases=None):
    return pl.pallas_call(
        kern, in_specs=[ANY] * len(ins), out_specs=[ANY] * len(out_shapes), out_shape=out_shapes,
        scratch_shapes=[pltpu.SemaphoreType.DMA((n,)) for n in n_sems], input_output_aliases=aliases or {},
        name=name)(*ins)


def _start_then_wait(copies):
    for cp in copies:
        cp.start()
    for cp in copies:
        cp.wait()


N_DEV = 8
RELATIONS = tuple((dx, dy, dc) for dx in (0, 1) for dy in (0, 1) for dc in (0, 1))[1:]


def _flip(v, d):
    return 1 - v if d else v


def _gather_copies(ins, outs, ssem, rsem, lsem):
    x, y, c = lax.axis_index("x"), lax.axis_index("y"), lax.axis_index("c")
    me = 2 * x + y
    cps = []
    for i, (a, o) in enumerate(zip(ins, outs)):
        cps.append(pltpu.make_async_copy(a, o.at[me], lsem.at[i]))
        for j, (px, py) in enumerate(_chip_peers(x, y)):
            cps.append(pltpu.make_async_remote_copy(a, o.at[me], ssem.at[3 * i + j], rsem.at[3 * i + j],
                                                    device_id=(px, py, c), device_id_type=MESH))
    return cps


def _gather_shapes(arrs):
    return [jax.ShapeDtypeStruct((N_CHIPS,) + a.shape, a.dtype) for a in arrs]


def _gather_sems(n):
    return [pltpu.SemaphoreType.DMA((3 * n,)), pltpu.SemaphoreType.DMA((3 * n,)), pltpu.SemaphoreType.DMA((n,))]


def _gather_chips(arrs):
    n = len(arrs)

    def kern(*refs):
        _start_then_wait(_gather_copies(refs[:n], refs[n:2 * n], *refs[2 * n:]))

    return _exchange_call(kern, "gather_weights", arrs, _gather_shapes(arrs), (3 * n, 3 * n, n))


def _to_owner_copies(ins, outs, ssem, rsem, lsem):
    x, y, c = lax.axis_index("x"), lax.axis_index("y"), lax.axis_index("c")
    me = 4 * x + 2 * y + c
    cps = []
    for i, (a, o) in enumerate(zip(ins, outs)):
        cps.append(pltpu.make_async_copy(a.at[2 * x + y, c], o.at[me], lsem.at[i]))
        for r, (dx, dy, dc) in enumerate(RELATIONS):
            tx, ty, tc = _flip(x, dx), _flip(y, dy), _flip(c, dc)
            cps.append(pltpu.make_async_remote_copy(a.at[2 * tx + ty, tc], o.at[me], ssem.at[7 * i + r], rsem.at[7 * i + r],
                                                    device_id=(tx, ty, tc), device_id_type=MESH))
    return cps


def _to_owner_shapes(arrs):
    return [jax.ShapeDtypeStruct((N_DEV,) + a.shape[2:], a.dtype) for a in arrs]


def _to_owner_sems(n):
    return [pltpu.SemaphoreType.DMA((7 * n,)), pltpu.SemaphoreType.DMA((7 * n,)), pltpu.SemaphoreType.DMA((n,))]


def _to_owner(arrs, name):
    n = len(arrs)

    def kern(*refs):
        _start_then_wait(_to_owner_copies(refs[:n], refs[n:2 * n], *refs[2 * n:]))

    return _exchange_call(kern, name, arrs, _to_owner_shapes(arrs), (7 * n, 7 * n, n))


def _join_halves(arrs):
    n = len(arrs)

    def kern(*refs):
        outs, (ssem, rsem) = refs[n:2 * n], refs[2 * n:]
        x, y, c = lax.axis_index("x"), lax.axis_index("y"), lax.axis_index("c")
        _start_then_wait([
            pltpu.make_async_remote_copy(outs[i].at[c], outs[i].at[c], ssem.at[i], rsem.at[i],
                                         device_id=(x, y, 1 - c), device_id_type=MESH) for i in range(n)])

    outs = [jax.ShapeDtypeStruct(a.shape, a.dtype) for a in arrs]
    return _exchange_call(kern, "grad_join_halves", arrs, outs, (n, n), aliases={i: i for i in range(n)})


def _row_block(rows, row_bytes, limit=1 << 20):
    best = None
    for d in range(16, rows + 1, 16):
        if rows % d == 0 and d * row_bytes <= limit:
            best = d
    return best if best is not None else rows


def _sum_devices(b, c, name):
    _, h, cols = b.shape
    hb = _row_block(h, cols * 4)

    def kern(c_ref, b_ref, o_ref):
        acc = b_ref[0].astype(F32)
        for j in range(1, N_DEV):
            acc = acc + b_ref[j].astype(F32)
        o_ref[...] = acc

    return pl.pallas_call(
        kern,
        grid_spec=pltpu.PrefetchScalarGridSpec(
            num_scalar_prefetch=1, grid=(h // hb,),
            in_specs=[pl.BlockSpec((N_DEV, hb, cols), lambda i, c_ref: (0, i, 0))],
            out_specs=pl.BlockSpec((None, hb, cols), lambda i, c_ref: (c_ref[0], i, 0))),
        out_shape=jax.ShapeDtypeStruct((2, h, cols), F32), name=name, compiler_params=_cparams(1))(c, b)


def _adamw(w, g, m, v, name):
    rows, cols = w.shape
    rb = _row_block(rows, cols * 4)
    c1 = 1.0 - ADAM_B1 ** ADAM_STEP
    c2 = 1.0 - ADAM_B2 ** ADAM_STEP

    def kern(w_ref, g_ref, m_ref, v_ref, d_ref, mo_ref, vo_ref):
        gv = g_ref[...]
        mn = ADAM_B1 * m_ref[...] + (1.0 - ADAM_B1) * gv
        vn = ADAM_B2 * v_ref[...] + (1.0 - ADAM_B2) * (gv * gv)
        mo_ref[...] = mn
        vo_ref[...] = vn
        d_ref[...] = (-ADAM_LR) * ((mn / c1) / (jnp.sqrt(vn / c2) + ADAM_EPS) + ADAM_WD * w_ref[...])

    spec = pl.BlockSpec((rb, cols), lambda i: (i, 0))
    return pl.pallas_call(
        kern, grid=(rows // rb,), in_specs=[spec] * 4, out_specs=[spec] * 3,
        out_shape=[jax.ShapeDtypeStruct(w.shape, F32)] * 3, name=name, compiler_params=_cparams(1))(w, g, m, v)


def _pad_rows(flat, rows):
    return jnp.pad(flat, (0, rows * LANES - flat.shape[0])).reshape(rows, LANES)


def _round_up(n, m):
    return (n + m - 1) // m * m


def _shard_shape(shape, axis):
    return tuple(s // N_CHIPS if a == axis else s for a, s in enumerate(shape))


def _to_shards(full, axis):
    shape = full.shape
    t = full.reshape(shape[:axis] + (N_CHIPS, shape[axis] // N_CHIPS) + shape[axis + 1:])
    return jnp.moveaxis(t, axis, 0).reshape(N_CHIPS, -1)


def _from_shards(sh, shape, axis):
    t = sh.reshape((N_CHIPS,) + _shard_shape(shape, axis))
    t = jnp.moveaxis(t, 0, axis)
    return t.reshape(shape)


BIG = tuple((name, shape, axis) for name, shape, axis, big in SHARDED if big)
EARLY_WEIGHTS = ("w_in", "w_uq", "w_ukv")
SMALL_SHARDED = tuple((name, shape, axis) for name, shape, axis, big in SHARDED if not big)


def _pack_small_weights(p):
    flat = jnp.concatenate([p[name].reshape(-1) for name, _, _ in SMALL_SHARDED])
    return _pad_rows(flat, _round_up(-(-flat.shape[0] // LANES), SUBLANES))


def _unpack_small_weights(gathered):
    flat = gathered.reshape(N_CHIPS, -1)
    out, off = {}, 0
    for name, shape, axis in SMALL_SHARDED:
        n = _numel(shape) // N_CHIPS
        out[name] = _from_shards(flat[:, off:off + n], shape, axis)
        off += n
    return out


def _pack_small_local(p, prefix=""):
    parts = [p[prefix + name].reshape(-1) for name, _, _ in SMALL_SHARDED]
    parts += [p[prefix + name].reshape(-1) for name, _ in REPLICATED]
    return jnp.concatenate(parts)


def _pack_small_grads(g):
    parts = [_to_shards(g[name], axis) for name, _, axis in SMALL_SHARDED]
    rep = jnp.concatenate([g[name].reshape(-1) for name, _ in REPLICATED])
    parts.append(jnp.broadcast_to(rep[None], (N_CHIPS, rep.shape[0])))
    return jnp.concatenate(parts, axis=1)


def _unpack_small_local(flat):
    out, off = {}, 0
    for name, shape, axis in SMALL_SHARDED:
        n = _numel(shape) // N_CHIPS
        out[name] = flat[off:off + n].reshape((1,) + _shard_shape(shape, axis))
        off += n
    for name, shape in REPLICATED:
        n = _numel(shape)
        out[name] = flat[off:off + n].reshape((1,) + shape)
        off += n
    return out


def _grad_shards(g, shape, axis):
    if axis == 0:
        return g.reshape((N_CHIPS,) + _shard_shape(shape, axis))
    return jnp.transpose(g.reshape(shape[0], N_CHIPS, shape[1] // N_CHIPS), (1, 0, 2))


def _cols_from_shards(w4):
    return jnp.transpose(w4, (1, 0, 2)).reshape(w4.shape[1], -1)


def _block_diag(w):
    eye = jnp.eye(LRU_BLOCKS, dtype=w.dtype)
    return jnp.einsum("ncd,nm->ncmd", w, eye).reshape(LRU_W, LRU_W)


def _block_diag_t(g):
    g4 = g.reshape(LRU_BLOCKS, 64, LRU_BLOCKS, 64)
    return jnp.stack([g4[n, :, n, :] for n in range(LRU_BLOCKS)])


def _pad8(a):
    return jnp.pad(a, ((0, SUBLANES - a.shape[0]), (0, 0)))


def kernel(x, mem, positions, attn_norm, w_in, lru_conv_w, lru_conv_b, lru_w_a, lru_b_a, lru_w_i, lru_b_i, lru_lambda, q_a_norm, w_uq, kv_a_norm, w_ukv, mla_q_norm, mla_k_norm, lru_out_norm, mla_out_norm, w_out, mem_attn_norm, mem_norm, w_mem_q, w_mem_kv, mem_q_norm, mem_k_norm, w_mem_o, ffn_norm, w_up, ffn_conv_w, ffn_conv_b, w_down, loss_target, m_attn_norm, m_w_in, m_lru_conv_w, m_lru_conv_b, m_lru_w_a, m_lru_b_a, m_lru_w_i, m_lru_b_i, m_lru_lambda, m_q_a_norm, m_w_uq, m_kv_a_norm, m_w_ukv, m_mla_q_norm, m_mla_k_norm, m_lru_out_norm, m_mla_out_norm, m_w_out, m_mem_attn_norm, m_mem_norm, m_w_mem_q, m_w_mem_kv, m_mem_q_norm, m_mem_k_norm, m_w_mem_o, m_ffn_norm, m_w_up, m_ffn_conv_w, m_ffn_conv_b, m_w_down, v_attn_norm, v_w_in, v_lru_conv_w, v_lru_conv_b, v_lru_w_a, v_lru_b_a, v_lru_w_i, v_lru_b_i, v_lru_lambda, v_q_a_norm, v_w_uq, v_kv_a_norm, v_w_ukv, v_mla_q_norm, v_mla_k_norm, v_lru_out_norm, v_mla_out_norm, v_w_out, v_mem_attn_norm, v_mem_norm, v_w_mem_q, v_w_mem_kv, v_mem_q_norm, v_mem_k_norm, v_w_mem_o, v_ffn_norm, v_w_up, v_ffn_conv_w, v_ffn_conv_b, v_w_down):
    given = dict(locals())
    local = {name: given[name][0] for name in WEIGHT_ORDER}
    s = x.shape[1]
    x2d, mem2d, tgt = x[0], mem[0], loss_target[0]
    tm = min(512, s)
    tm_wide = min(1024, s)
    tm_ffn = min(256, s)
    t_scan = min(1024, s)
    tq_f, tq_b, tk = min(4096, s), min(2048, s), min(512, s)

    early = [b for b in BIG if b[0] in EARLY_WEIGHTS]
    late = [b for b in BIG if b[0] not in EARLY_WEIGHTS]
    got = _gather_chips([local[name].astype(BF16) for name, _, _ in early] + [_pack_small_weights(local)])
    full = _unpack_small_weights(got[-1])

    def take_gathered(entries, arrays):
        for (name, shape, axis), w4 in zip(entries, arrays):
            if axis == 0:
                full[name] = w4.reshape(shape)
            elif name in ("w_up", "w_mem_o"):
                full[name] = w4
            else:
                full[name] = _cols_from_shards(w4)

    take_gathered(early, got)
    row = lambda a: a.reshape(1, -1)
    b16 = lambda a: a.astype(BF16)
    zeros = lambda r, c: jnp.zeros((r, c), BF16)
    w_in_f = full["w_in"]
    w_in_p = jnp.concatenate([w_in_f[:, :OFF_KR], _head_tile(zeros(D_MODEL, QK_NOPE), w_in_f[:, OFF_KR:])], axis=1)
    uq = full["w_uq"].reshape(Q_LORA, HEADS, QK_HEAD)
    w_uq_p = _head_tile(uq[:, :, :QK_NOPE], uq[:, :, QK_NOPE:]).reshape(Q_LORA, -1)
    ukv = full["w_ukv"].reshape(KV_LORA, HEADS, QK_NOPE + V_DIM)
    w_uk_p = _head_tile(ukv[:, :, :QK_NOPE], None).reshape(KV_LORA, -1)
    w_uv = ukv[:, :, QK_NOPE:].reshape(KV_LORA, MLA_W)
    wa = [b16(_block_diag(local["lru_w_a"][d])) for d in range(2)]
    wi = [b16(_block_diag(local["lru_w_i"][d])) for d in range(2)]
    cw = [_pad8(full["lru_conv_w"][d]) for d in range(2)]
    pv = [_pad8(jnp.stack([full["lru_conv_b"][d], full["lru_b_a"][d], full["lru_b_i"][d], full["lru_lambda"][d]]))
          for d in range(2)]
    ffn_cw = _pad8(jnp.concatenate([full["ffn_conv_w"], row(local["ffn_conv_b"])], axis=0))
    g_attn, g_qa, g_kva = row(local["attn_norm"]), row(local["q_a_norm"]), row(local["kv_a_norm"])
    g_qn = _head_tile(row(local["mla_q_norm"])[:, :QK_NOPE], row(local["mla_q_norm"])[:, QK_NOPE:])
    g_kn = _head_tile(row(local["mla_k_norm"])[:, :QK_NOPE], row(local["mla_k_norm"])[:, QK_NOPE:])
    g_lru, g_mla = row(local["lru_out_norm"]), row(local["mla_out_norm"])
    g_memattn, g_mem = row(local["mem_attn_norm"]), row(local["mem_norm"])
    g_mq, g_mk, g_ffn = row(local["mem_q_norm"]), row(local["mem_k_norm"]), row(local["ffn_norm"])

    inv = ROPE_THETA ** (-jnp.arange(0, QK_ROPE, 2, dtype=F32) / QK_ROPE)
    no_nope = jnp.zeros((1, QK_NOPE), F32)
    inv_tile = _head_tile(no_nope, jnp.concatenate([inv, inv])[None])
    sign_tile = _head_tile(no_nope, jnp.concatenate([-jnp.ones_like(inv), jnp.ones_like(inv)])[None])
    ang = positions[0].astype(F32)[:, None] * inv_tile
    cos_t, sin_t = jnp.cos(ang), jnp.sin(ang) * sign_tile

    xr, yg, cq, ckv, krp, hb_in = _in_proj(x2d, g_attn, w_in_p, tm_wide)
    h_f, *saved_f = _lru_scan_fwd(xr, cw[0], pv[0], wa[0], wi[0], False, t_scan)
    h_b, *saved_b = _lru_scan_fwd(xr, cw[1], pv[1], wa[1], wi[1], True, t_scan)
    q, k, v = _mla_qkv(cq, ckv, krp, cos_t, sin_t, g_qa, g_kva, g_qn, g_kn, w_uq_p, w_uk_p, w_uv, tm_wide)
    o, lse, *got = _attn_fwd(q, k, v, tq_f, tk, shards=[local[name].astype(BF16) for name, _, _ in late])
    take_gathered(late, got)
    x1, mixed = _mix_out(h_f, h_b, yg, o, x2d, g_lru, g_mla, full["w_out"], tm_wide)
    km, vm = _mem_kv(mem2d, g_mem, full["w_mem_kv"], g_mk)
    x2, o_mem = _mem_attn(x1, g_memattn, full["w_mem_q"], g_mq, km, vm, full["w_mem_o"], tm_wide)
    gu_pre, gu_conv, hb_ffn = _ffn_up(x2, g_ffn, full["w_up"], ffn_cw, tm_ffn)
    dy, dyb, act, dgu, loss_acc = _ffn_down_loss(gu_conv, x2, tgt, full["w_down"], tm_ffn)
    loss = lax.psum(loss_acc[0, 0] * (0.5 / D_MODEL), ("x", "y", "c"))

    grads = {}
    grads["w_down"] = _matmul_tn(act, dyb, "grad_w_down", out_dtype=BF16)
    dpre, g_conv = _ffn_bwd_conv(dgu, gu_pre, ffn_cw, tm_ffn)
    grads["ffn_conv_w"], grads["ffn_conv_b"] = g_conv[:3], g_conv[3]
    grads["w_up"] = _matmul_tn(hb_ffn, dpre, "grad_w_up", col_shards=True, out_dtype=BF16)
    dx2, dx2b, gg = _ffn_bwd_in(dpre, x2, dy, g_ffn, full["w_up"], tm)
    grads["ffn_norm"] = gg[0]
    grads["w_mem_o"] = _matmul_tn(o_mem, dx2b, "grad_w_mem_o", out_dtype=BF16)
    dx1, dx1b, hm, dqr_mem, dkm, dvm, gg, ggq = _mem_attn_bwd(x1, dx2, dx2b, g_memattn, full["w_mem_q"], g_mq, km, vm,
                                                                 full["w_mem_o"], tm)
    grads["mem_attn_norm"], grads["mem_q_norm"] = gg[0], ggq[0]
    grads["w_mem_q"] = _matmul_tn(hm, dqr_mem, "grad_w_mem_q", out_dtype=BF16)
    g_mem_kv, gg, ggk, _ = _mem_kv_bwd(mem2d, g_mem, full["w_mem_kv"], g_mk, dkm, dvm)
    grads["w_mem_kv"] = g_mem_kv.astype(BF16)
    grads["mem_norm"], grads["mem_k_norm"] = gg[0], ggk[0]
    grads["w_out"] = _matmul_tn(mixed, dx1b, "grad_w_out", out_dtype=BF16)
    dh, dyg, dob, dl128, ggl, ggm = _mix_out_bwd(dx1b, h_f, h_b, yg, o, g_lru, g_mla, full["w_out"], tm)
    grads["lru_out_norm"], grads["mla_out_norm"] = ggl[0], ggm[0]
    delta_t = jnp.transpose(dl128[:, :HEADS]).reshape(HEADS // 2, 2, s)
    def halves(name, shape, axis):
        g4 = grads[name] if grads[name].ndim == 3 else _grad_shards(grads[name], shape, axis)
        return g4.reshape(N_CHIPS, 2, g4.shape[1] // 2, g4.shape[2])

    dq, dk, dv, *arrived_late = _attn_bwd(q, k, v, dob, lse, delta_t, tq_b, tk,
                                          contributions=[halves(*e) for e in late])
    (dcq, dckv, dkrp, cqb, dqr, ckvb, dkn, dvb, ggqa, ggkva, ggqn, ggkn) = _mla_qkv_bwd(
        cq, ckv, krp, cos_t, sin_t, dq, dk, dv, g_qa, g_kva, g_qn, g_kn, w_uq_p, w_uk_p, w_uv, tm_wide)
    grads["q_a_norm"], grads["kv_a_norm"] = ggqa[0], ggkva[0]
    grads["mla_q_norm"] = jnp.concatenate(_from_head_tile(ggqn[0]))
    grads["mla_k_norm"] = jnp.concatenate(_from_head_tile(ggkn[0]))
    g_uq_p = _matmul_tn(cqb, dqr, "grad_w_uq")
    grads["w_uq"] = jnp.concatenate(_from_head_tile(g_uq_p.reshape(Q_LORA, HEADS, LANES)), axis=-1).reshape(Q_LORA, -1)
    g_uk_p = _from_head_tile(_matmul_tn(ckvb, dkn, "grad_w_uk").reshape(KV_LORA, HEADS, LANES))[0]
    g_uv = _matmul_tn(ckvb, dvb, "grad_w_uv").reshape(KV_LORA, HEADS, V_DIM)
    grads["w_ukv"] = jnp.concatenate([g_uk_p, g_uv], axis=2).reshape(KV_LORA, -1)
    dxr, gwa, gwi, gvec = [], [], [], []
    for d, (hd, saved) in enumerate(((h_f, saved_f), (h_b, saved_b))):
        r = _lru_scan_bwd(xr, saved, hd, dh, cw[d], pv[d], wa[d], wi[d], d == 1, t_scan)
        dxr.append(r[0])
        gwa.append(_block_diag_t(r[1]))
        gwi.append(_block_diag_t(r[2]))
        gvec.append(r[3])
    grads["lru_w_a"], grads["lru_w_i"] = jnp.stack(gwa), jnp.stack(gwi)
    grads["lru_conv_w"] = jnp.stack([gv[:CONV_W] for gv in gvec])
    for r_i, name in ((4, "lru_conv_b"), (5, "lru_b_a"), (6, "lru_b_i"), (7, "lru_lambda")):
        grads[name] = jnp.stack([gv[r_i] for gv in gvec])
    grad_x, dproj, gg = _in_proj_bwd(x2d, dx1, dxr[0], dxr[1], dyg, dcq, dckv, dkrp, g_attn, w_in_p, tm)
    grads["attn_norm"] = gg[0]
    g_in_p = _matmul_tn(hb_in, dproj, "grad_w_in")
    grads["w_in"] = jnp.concatenate([g_in_p[:, :OFF_KR], _from_head_tile(g_in_p[:, OFF_KR:])[1]], axis=1)

    small = _pack_small_grads(grads)
    length = small.shape[1]
    hrows = _round_up(-(-length // (2 * LANES)), 16)
    small = jnp.pad(small, ((0, 0), (0, 2 * hrows * LANES - length))).reshape(N_CHIPS, 2, hrows, LANES)
    for name, _, _ in early:
        grads[name] = grads[name].astype(BF16)
    arrived_early = _to_owner([halves(*e) for e in early] + [small], "grad_to_owner")
    names = [name for name, _, _ in late + early] + ["small"]
    c_idx = lax.axis_index("c").astype(jnp.int32).reshape(1)
    reduced = _join_halves([_sum_devices(b, c_idx, "grad_sum_" + n)
                            for n, b in zip(names, list(arrived_late) + list(arrived_early))])

    outs = [{}, {}, {}, {}]
    for (name, shape, axis), r in zip(late + early, reduced):
        g2 = r.reshape(_shard_shape(shape, axis))
        res = _adamw(local[name], g2, given["m_" + name][0], given["v_" + name][0], "adamw_" + name)
        for o_, a in zip(outs, (g2, *res)):
            o_[name] = a[None]
    pack = lambda prefix: _pad_rows(_pack_small_local({n: given[prefix + n] for n in WEIGHT_ORDER}), 2 * hrows)
    g_small = reduced[-1].reshape(2 * hrows, LANES)
    res = _adamw(pack(""), g_small, pack("m_"), pack("v_"), "adamw_small")
    for o_, a in zip(outs, (g_small, *res)):
        o_.update(_unpack_small_local(a.reshape(-1)))
    return (loss, grad_x[None], *[o_[n] for o_ in outs for n in WEIGHT_ORDER])
```

```python
import jax
import jax.numpy as jnp
from jax import lax
from jax.experimental import pallas as pl
from jax.experimental.pallas import tpu as pltpu

F32, BF16 = jnp.float32, jnp.bfloat16
MESH = pl.DeviceIdType.MESH

D_MODEL = 1024
EPS = 1e-6
LRU_W = 512
LRU_BLOCKS = 8
LRU_C = 8.0
CONV_W = 4
HEADS = 8
QK_NOPE, QK_ROPE, QK_HEAD, V_DIM = 64, 32, 96, 64
Q_LORA, KV_LORA = 256, 128
MLA_W = HEADS * V_DIM
ROPE_THETA = 10000.0
IN_COLS = 2 * LRU_W + Q_LORA + KV_LORA + QK_ROPE
OFF_KR = IN_COLS - QK_ROPE
IN_PAD = 1536
MEM_HEADS, MEM_HD = 4, 128
MEM_W = MEM_HEADS * MEM_HD
D_FF = 2816
N_CHIPS = 4
ADAM_LR, ADAM_B1, ADAM_B2, ADAM_EPS, ADAM_WD, ADAM_STEP = 0.001, 0.9, 0.999, 1e-08, 0.01, 10

LANES = 128
SUBLANES = 8
V7X_VMEM_BYTES = 64 * 1024 * 1024
VMEM_LIMIT = V7X_VMEM_BYTES * 7 // 8

SHARDED = (
    ("w_in", (D_MODEL, IN_COLS), 1, True),
    ("lru_conv_w", (2, CONV_W, LRU_W), 2, False),
    ("lru_conv_b", (2, LRU_W), 1, False),
    ("lru_b_a", (2, LRU_W), 1, False),
    ("lru_b_i", (2, LRU_W), 1, False),
    ("lru_lambda", (2, LRU_W), 1, False),
    ("w_uq", (Q_LORA, HEADS * QK_HEAD), 1, True),
    ("w_ukv", (KV_LORA, HEADS * (QK_NOPE + V_DIM)), 1, True),
    ("w_out", (2 * LRU_W, D_MODEL), 0, True),
    ("w_mem_q", (D_MODEL, MEM_W), 0, True),
    ("w_mem_kv", (D_MODEL, 2 * MEM_W), 0, True),
    ("w_mem_o", (MEM_W, D_MODEL), 1, True),
    ("w_up", (D_MODEL, 2 * D_FF), 1, True),
    ("ffn_conv_w", (3, 2 * D_FF), 1, False),
    ("w_down", (D_FF, D_MODEL), 0, True),
)
REPLICATED = (
    ("attn_norm", (D_MODEL,)), ("lru_w_a", (2, LRU_BLOCKS, 64, 64)), ("lru_w_i", (2, LRU_BLOCKS, 64, 64)),
    ("q_a_norm", (Q_LORA,)), ("kv_a_norm", (KV_LORA,)), ("mla_q_norm", (QK_HEAD,)), ("mla_k_norm", (QK_HEAD,)),
    ("lru_out_norm", (LRU_W,)), ("mla_out_norm", (MLA_W,)), ("mem_attn_norm", (D_MODEL,)), ("mem_norm", (D_MODEL,)),
    ("mem_q_norm", (MEM_HD,)), ("mem_k_norm", (MEM_HD,)), ("ffn_norm", (D_MODEL,)), ("ffn_conv_b", (2 * D_FF,)),
)
WEIGHT_ORDER = ('attn_norm', 'w_in', 'lru_conv_w', 'lru_conv_b', 'lru_w_a', 'lru_b_a', 'lru_w_i', 'lru_b_i', 'lru_lambda',
                'q_a_norm', 'w_uq', 'kv_a_norm', 'w_ukv', 'mla_q_norm', 'mla_k_norm', 'lru_out_norm', 'mla_out_norm', 'w_out',
                'mem_attn_norm', 'mem_norm', 'w_mem_q', 'w_mem_kv', 'mem_q_norm', 'mem_k_norm', 'w_mem_o', 'ffn_norm', 'w_up',
                'ffn_conv_w', 'ffn_conv_b', 'w_down')


def _numel(shape):
    n = 1
    for s in shape:
        n *= s
    return n


def _cparams(n_axes):
    return pltpu.CompilerParams(dimension_semantics=("arbitrary",) * n_axes, vmem_limit_bytes=VMEM_LIMIT)


def _bdot(a, b):
    return jnp.dot(a.astype(BF16), b.astype(BF16), preferred_element_type=F32)


def _bdot_nt(a, b):
    return lax.dot_general(a.astype(BF16), b.astype(BF16), (((1,), (1,)), ((), ())), preferred_element_type=F32)


def _bdot_tn(a, b):
    return lax.dot_general(a.astype(BF16), b.astype(BF16), (((0,), (0,)), ((), ())), preferred_element_type=F32)


def _rstd(x, n=None):
    n = x.shape[-1] if n is None else n
    return lax.rsqrt(jnp.sum(x * x, axis=-1, keepdims=True) * (1.0 / n) + EPS)


def _norm_bwd(x, rs, g, dy, n=None):
    n = x.shape[-1] if n is None else n
    xhat = x * rs
    dxh = dy * g
    dx = rs * (dxh - xhat * (jnp.sum(dxh * xhat, axis=-1, keepdims=True) * (1.0 / n)))
    return dx, dy * xhat


def _acc_row(ref, r, val):
    ref[r:r + 1, :] += jnp.sum(val, axis=0, keepdims=True)


def _zero_first(i, *refs):
    @pl.when(i == 0)
    def _():
        for r in refs:
            r[...] = jnp.zeros_like(r)


def _shift_down(x, j, halo):
    if j == 0:
        return x
    xs = pltpu.roll(x, j, 0)
    hs = pltpu.roll(halo, j, 0)
    row = lax.broadcasted_iota(jnp.int32, hs.shape, 0)
    top = jnp.where(row < j, hs, xs[:SUBLANES])
    return jnp.concatenate([top, xs[SUBLANES:]], axis=0)


def _shift_up(x, j, halo):
    if j == 0:
        return x
    t = x.shape[0]
    xs = pltpu.roll(x, t - j, 0)
    hs = pltpu.roll(halo, SUBLANES - j, 0)
    row = lax.broadcasted_iota(jnp.int32, hs.shape, 0)
    bot = jnp.where(row >= SUBLANES - j, hs, xs[t - SUBLANES:])
    return jnp.concatenate([xs[:t - SUBLANES], bot], axis=0)


def _shift(x, j, halo, down):
    return _shift_down(x, j, halo) if down else _shift_up(x, j, halo)


def _scan(a, b, h_in, down):
    t, c = a.shape
    g = t // SUBLANES
    a3, b3 = a.reshape(g, SUBLANES, c), b.reshape(g, SUBLANES, c)
    sub = lax.broadcasted_iota(jnp.int32, a3.shape, 1)
    d = 1
    while d < SUBLANES:
        keep = (sub >= d) if down else (sub < SUBLANES - d)
        shift = d if down else SUBLANES - d
        a_s = jnp.where(keep, pltpu.roll(a3, shift, 1), 1.0)
        b_s = jnp.where(keep, pltpu.roll(b3, shift, 1), 0.0)
        b3 = a3 * b_s + b3
        a3 = a3 * a_s
        d *= 2
    hs = [None] * g
    carry = h_in
    for i in (range(g) if down else range(g - 1, -1, -1)):
        hs[i] = a3[i] * carry + b3[i]
        carry = hs[i][SUBLANES - 1:, :] if down else hs[i][:1, :]
    return jnp.concatenate(hs, axis=0)


def _sigmoid(x):
    return 0.5 * jnp.tanh(0.5 * x) + 0.5


LOG2E = 1.4426950408889634
GELU_K = 0.7978845608028654
GELU_C = 0.044715


def _gelu(x):
    return 0.5 * x * (1.0 + jnp.tanh(GELU_K * (x + GELU_C * x * x * x)))


def _gelu_grad(x):
    t = jnp.tanh(GELU_K * (x + GELU_C * x * x * x))
    return 0.5 * (1.0 + t) + 0.5 * x * (1.0 - t * t) * GELU_K * (1.0 + 3.0 * GELU_C * x * x)


ROPE_HALF = QK_ROPE // 2
ROPE_LANE = 32


def _head_tile(nope, rope):
    z = lambda n: jnp.zeros(nope.shape[:-1] + (n,), nope.dtype)
    r1, r2 = (z(ROPE_HALF), z(ROPE_HALF)) if rope is None else (rope[..., :ROPE_HALF], rope[..., ROPE_HALF:])
    return jnp.concatenate([nope[..., :ROPE_LANE], r1, nope[..., ROPE_LANE:], z(ROPE_HALF), r2, z(ROPE_HALF)], axis=-1)


def _from_head_tile(t):
    a, b = ROPE_LANE + ROPE_HALF, ROPE_LANE + LANES // 2
    return (jnp.concatenate([t[..., :ROPE_LANE], t[..., a:a + QK_NOPE - ROPE_LANE]], axis=-1),
            jnp.concatenate([t[..., ROPE_LANE:a], t[..., b:b + ROPE_HALF]], axis=-1))


def _rope_partner(x):
    lane = lax.broadcasted_iota(jnp.int32, x.shape, 1) & (LANES // 2 - 1)
    return jnp.where((lane >= ROPE_LANE) & (lane < ROPE_LANE + ROPE_HALF), pltpu.roll(x, LANES // 2, 1), 0.0)


def _rope(x, cos_t, sin_t):
    return x * cos_t + _rope_partner(x) * sin_t


def _rope_t(dy, cos_t, sin_t):
    return dy * cos_t + _rope_partner(dy * sin_t)


def _rowwise(body, name, s, tm, rows=(), halos=(), fulls=(), outs=(), accs=()):
    n = s // tm
    in_specs, args = [], []
    for a in rows:
        in_specs.append(pl.BlockSpec((tm, a.shape[1]), lambda i: (i, 0)))
        args.append(a)
    for a in halos:
        hr = 2 * SUBLANES if a.dtype == BF16 else SUBLANES
        in_specs.append(pl.BlockSpec((hr, a.shape[1]), lambda i, hr=hr: (jnp.maximum(i * (tm // hr) - 1, 0), 0)))
        in_specs.append(pl.BlockSpec((hr, a.shape[1]), lambda i, hr=hr: (jnp.minimum((i + 1) * (tm // hr), s // hr - 1), 0)))
        args += [a, a]
    for a in fulls:
        in_specs.append(pl.BlockSpec(a.shape, lambda i, nd=a.ndim: (0,) * nd))
        args.append(a)
    out_shape, out_specs = [], []
    for c, dt in outs:
        out_shape.append(jax.ShapeDtypeStruct((s, c), dt))
        out_specs.append(pl.BlockSpec((tm, c), lambda i: (i, 0)))
    for shp, dt in accs:
        out_shape.append(jax.ShapeDtypeStruct(shp, dt))
        out_specs.append(pl.BlockSpec(shp, lambda i, nd=len(shp): (0,) * nd))

    def kern(*refs):
        body(pl.program_id(0), n, *refs)

    return pl.pallas_call(kern, grid=(n,), in_specs=in_specs, out_specs=out_specs, out_shape=out_shape, name=name,
                          compiler_params=_cparams(1))(*args)


def _matmul_tn(a, b, name, col_shards=False, out_dtype=F32):
    t, m = a.shape
    n = b.shape[1]
    bm = m
    for cand in range(LANES, m + 1, LANES):
        if m % cand == 0 and cand * (n // N_CHIPS if col_shards else min(n, 2048)) * 4 <= 6 * 1024 * 1024:
            bm = cand
    bn = n // N_CHIPS if col_shards else (n if n <= 2048 else 1408)
    bt = min(t, 2048 if max(bm, bn) <= 512 else (1024 if max(bm, bn) <= 1024 else 512))
    nt = t // bt

    def kern(a_ref, b_ref, o_ref, acc_ref):
        k = pl.program_id(2)

        @pl.when(k == 0)
        def _():
            acc_ref[...] = jnp.zeros_like(acc_ref)
        acc_ref[...] += _bdot_tn(a_ref[...], b_ref[...])

        @pl.when(k == nt - 1)
        def _():
            o_ref[...] = acc_ref[...].astype(out_dtype)

    if col_shards:
        out_spec = pl.BlockSpec((None, bm, bn), lambda i, j, k: (j, i, 0))
        out_shape = jax.ShapeDtypeStruct((N_CHIPS, m, bn), out_dtype)
    else:
        out_spec = pl.BlockSpec((bm, bn), lambda i, j, k: (i, j))
        out_shape = jax.ShapeDtypeStruct((m, n), out_dtype)
    return pl.pallas_call(
        kern, grid=(m // bm, n // bn, nt),
        in_specs=[pl.BlockSpec((bt, bm), lambda i, j, k: (k, i)), pl.BlockSpec((bt, bn), lambda i, j, k: (k, j))],
        out_specs=out_spec, out_shape=out_shape, scratch_shapes=[pltpu.VMEM((bm, bn), F32)], name=name,
        compiler_params=_cparams(3))(a, b)


def _in_proj(x, g, w_in_p, tm):
    def body(i, n, x_ref, g_ref, w_ref, xr, yg, cq, ckv, krp, hb):
        xv = x_ref[...]
        h = (xv * _rstd(xv) * g_ref[...]).astype(BF16)
        hb[...] = h
        p = jnp.dot(h, w_ref[...], preferred_element_type=F32)
        xr[...] = p[:, :LRU_W]
        yg[...] = p[:, LRU_W:2 * LRU_W]
        cq[...] = p[:, 2 * LRU_W:2 * LRU_W + Q_LORA]
        ckv[...] = p[:, 2 * LRU_W + Q_LORA:OFF_KR]
        krp[...] = p[:, OFF_KR:IN_PAD]

    return _rowwise(body, "in_proj", x.shape[0], tm, rows=[x], fulls=[g, w_in_p],
                    outs=[(LRU_W, F32), (LRU_W, F32), (Q_LORA, F32), (KV_LORA, F32), (LANES, F32), (D_MODEL, BF16)])


def _softplus_neg(lam):
    e = jnp.exp(-jnp.abs(lam))
    return jnp.maximum(-lam, 0.0) + jnp.where(e < 1e-2, e * (1.0 - e * (0.5 - e * (1.0 / 3.0))), jnp.log(1.0 + e))


def _lru_gates(x, halo, cw_ref, pv_ref, wa_ref, wi_ref, rev):
    down = not rev
    xc = pv_ref[0:1, :] + jnp.zeros_like(x)
    for j in range(CONV_W):
        k = j if rev else CONV_W - 1 - j
        xc = xc + cw_ref[k:k + 1, :] * _shift(x, j, halo, down)
    r = _sigmoid(_bdot(xc, wa_ref[...]) + pv_ref[1:2, :])
    ig = _sigmoid(_bdot(xc, wi_ref[...]) + pv_ref[2:3, :])
    lam = pv_ref[3:4, :]
    sp = _softplus_neg(lam)
    log_a = (-LRU_C) * r * sp
    a = jnp.exp(log_a)
    z = 2.0 * log_a
    series = -(z * (1.0 + z * (0.5 + z * (1.0 / 6.0 + z * (1.0 / 24.0)))))
    om = jnp.where(z > -0.02, series, 1.0 - a * a)
    mult = jnp.sqrt(om)
    return xc, r, ig, sp, a, mult


def _lru_scan_fwd(xr, cw, pv, wa, wi, rev, t):
    s = xr.shape[0]
    n = s // t
    hb = t // SUBLANES
    last8 = s // SUBLANES - 1
    down = not rev

    def kern(x_ref, halo_ref, cw_ref, pv_ref, wa_ref, wi_ref, h_ref, xc_ref, r_ref, ig_ref, a_ref, mult_ref, carry_ref):
        i = pl.program_id(0)
        _zero_first(i, carry_ref)
        halo = jnp.where(i == 0, 0.0, halo_ref[...])
        xc, r, ig, sp, a, mult = _lru_gates(x_ref[...], halo, cw_ref, pv_ref, wa_ref, wi_ref, rev)
        xc_ref[...], r_ref[...], ig_ref[...], a_ref[...], mult_ref[...] = xc, r, ig, a, mult
        h_ref[...] = _scan(a, mult * ig * xc, carry_ref[...], down)
        carry_ref[...] = h_ref[pl.ds(t - 1 if down else 0, 1), :]

    if rev:
        blk = lambda i: (n - 1 - i, 0)
        hal = lambda i: (jnp.minimum((n - i) * hb, last8), 0)
    else:
        blk = lambda i: (i, 0)
        hal = lambda i: (jnp.maximum(i * hb - 1, 0), 0)
    full = lambda a: pl.BlockSpec(a.shape, lambda i: (0, 0))
    return pl.pallas_call(
        kern, grid=(n,),
        in_specs=[pl.BlockSpec((t, LRU_W), blk), pl.BlockSpec((SUBLANES, LRU_W), hal), full(cw), full(pv), full(wa), full(wi)],
        out_specs=[pl.BlockSpec((t, LRU_W), blk)] * 6, out_shape=[jax.ShapeDtypeStruct((s, LRU_W), F32)] * 6,
        scratch_shapes=[pltpu.VMEM((1, LRU_W), F32)], name="lru_scan_rev" if rev else "lru_scan_fwd",
        compiler_params=_cparams(1))(xr, xr, cw, pv, wa, wi)


def _lru_scan_bwd(xr, saved, h, dh, cw, pv, wa, wi, rev, t):
    s = xr.shape[0]
    n = s // t
    hb = t // SUBLANES
    last8 = s // SUBLANES - 1
    down = not rev

    def kern(x_ref, xc_ref, r_ref, ig_ref, a_ref, mult_ref, h_ref, hh_ref, dh_ref, cw_ref, pv_ref, wa_ref, wi_ref,
             dx_ref, gwa_ref, gwi_ref, gv_ref, p_ref, dxc_halo_ref, tmp_ref):
        i = pl.program_id(0)
        _zero_first(i, gwa_ref, gwi_ref, gv_ref, p_ref, dxc_halo_ref)
        at_start = i == n - 1
        x = x_ref[...]
        hhalo = jnp.where(at_start, 0.0, hh_ref[...])
        xc, r, ig, a, mult = xc_ref[...], r_ref[...], ig_ref[...], a_ref[...], mult_ref[...]
        lam = pv_ref[3:4, :]
        sp = _softplus_neg(lam)
        h_prev = _shift(h_ref[...], 1, hhalo, down)
        row = lax.broadcasted_iota(jnp.int32, x.shape, 0)
        edge = t - 1 if down else 0
        dh_mod = dh_ref[...] + jnp.where(row == edge, p_ref[...], 0.0)
        a_next = _shift(a, 1, jnp.zeros((SUBLANES, LRU_W), F32), not down)
        g = _scan(a_next, dh_mod, jnp.zeros((1, LRU_W), F32), not down)
        tmp_ref[...] = a * g
        p_ref[...] = tmp_ref[pl.ds(0 if down else t - 1, 1), :]
        da = g * h_prev
        d_ig = g * mult * xc
        d_xc = g * mult * ig
        d_om = g * ig * xc * (0.5 / jnp.maximum(mult, 1e-30))
        d_log_a = da * a - 2.0 * d_om * a * a
        d_r = d_log_a * ((-LRU_C) * sp)
        d_sp = jnp.sum(d_log_a * ((-LRU_C) * r), axis=0, keepdims=True)
        gv_ref[7:8, :] += d_sp * (-_sigmoid(-lam))
        d_ga = d_r * r * (1.0 - r)
        d_gi = d_ig * ig * (1.0 - ig)
        _acc_row(gv_ref, 5, d_ga)
        _acc_row(gv_ref, 6, d_gi)
        d_xc = d_xc + _bdot_nt(d_ga, wa_ref[...]) + _bdot_nt(d_gi, wi_ref[...])
        gwa_ref[...] += _bdot_tn(xc, d_ga)
        gwi_ref[...] += _bdot_tn(xc, d_gi)
        _acc_row(gv_ref, 4, d_xc)
        dx = jnp.zeros_like(x)
        dxc_halo = dxc_halo_ref[...]
        for j in range(CONV_W):
            k = j if rev else CONV_W - 1 - j
            d_shift = _shift(d_xc, j, dxc_halo, not down)
            _acc_row(gv_ref, k, d_shift * x)
            dx = dx + cw_ref[k:k + 1, :] * d_shift
        dx_ref[...] = dx
        dxc_halo_ref[...] = d_xc[:SUBLANES] if down else d_xc[t - SUBLANES:]

    if rev:
        blk = lambda i: (i, 0)
        hal = lambda i: (jnp.minimum((i + 1) * hb, last8), 0)
    else:
        blk = lambda i: (n - 1 - i, 0)
        hal = lambda i: (jnp.maximum((n - 1 - i) * hb - 1, 0), 0)
    full = lambda a: pl.BlockSpec(a.shape, lambda i: (0, 0))
    bs = pl.BlockSpec((t, LRU_W), blk)
    hs = pl.BlockSpec((SUBLANES, LRU_W), hal)
    return pl.pallas_call(
        kern, grid=(n,),
        in_specs=[bs] * 7 + [hs, bs, full(cw), full(pv), full(wa), full(wi)],
        out_specs=[bs, pl.BlockSpec((LRU_W, LRU_W), lambda i: (0, 0)), pl.BlockSpec((LRU_W, LRU_W), lambda i: (0, 0)),
                   pl.BlockSpec((SUBLANES, LRU_W), lambda i: (0, 0))],
        out_shape=[jax.ShapeDtypeStruct((s, LRU_W), F32), jax.ShapeDtypeStruct((LRU_W, LRU_W), F32),
                   jax.ShapeDtypeStruct((LRU_W, LRU_W), F32), jax.ShapeDtypeStruct((SUBLANES, LRU_W), F32)],
        scratch_shapes=[pltpu.VMEM((1, LRU_W), F32), pltpu.VMEM((SUBLANES, LRU_W), F32), pltpu.VMEM((t, LRU_W), F32)],
        name="lru_bwd_rev" if rev else "lru_bwd_fwd", compiler_params=_cparams(1))(xr, *saved, h, h, dh, cw, pv, wa, wi)


def _mla_qkv(cq, ckv, krp, cos_t, sin_t, g_qa, g_kva, g_qn, g_kn, w_uq_p, w_uk_p, w_uv, tm):
    scale = QK_HEAD ** -0.5 * LOG2E

    def body(i, n, cq_ref, ckv_ref, kr_ref, c_ref, s_ref, gqa, gkva, gqn, gkn, wq, wk, wv, q_out, k_out, v_out):
        cosv, sinv = c_ref[...], s_ref[...]
        cqv = cq_ref[...]
        qr = _bdot(cqv * _rstd(cqv) * gqa[...], wq[...])
        ckvv = ckv_ref[...]
        c_kv = (ckvv * _rstd(ckvv) * gkva[...]).astype(BF16)
        kn = jnp.dot(c_kv, wk[...], preferred_element_type=F32)
        v_out[...] = jnp.dot(c_kv, wv[...], preferred_element_type=F32).astype(BF16)
        kr = kr_ref[...]
        kr_swapped = _rope_partner(kr * gkn[...]) * sinv
        for h in range(HEADS):
            sl = slice(h * LANES, (h + 1) * LANES)
            qh = qr[:, sl]
            qh = _rope(qh * _rstd(qh, QK_HEAD) * gqn[...], cosv, sinv) * scale
            q_out[:, sl] = qh.astype(BF16)
            kh = kn[:, sl] + kr
            rs = _rstd(kh, QK_HEAD)
            k_out[:, sl] = (kh * rs * gkn[...] * cosv + kr_swapped * rs).astype(BF16)

    return _rowwise(body, "mla_qkv", cq.shape[0], tm, rows=[cq, ckv, krp, cos_t, sin_t],
                    fulls=[g_qa, g_kva, g_qn, g_kn, w_uq_p, w_uk_p, w_uv],
                    outs=[(HEADS * LANES, BF16), (HEADS * LANES, BF16), (MLA_W, BF16)])


NT_DIMS = (((1,), (1,)), ((), ()))
TN_DIMS = (((0,), (0,)), ((), ()))


def _riding_exchange(copies_fn, first, last):
    @pl.when(first)
    def _():
        for cp in copies_fn():
            cp.start()

    def finish():
        @pl.when(last)
        def _():
            for cp in copies_fn():
                cp.wait()
    return finish


def _attn_fwd(q, k, v, tq, tk, shards=()):
    s = q.shape[0]
    nq, nk = s // tq, s // tk
    n = len(shards)

    def kern(*refs):
        q_ref, k_ref, v_ref = refs[:3]
        o_ref, lse_ref = refs[3 + n:5 + n]
        acc_ref = refs[5 + 2 * n]
        p_id, i_id = pl.program_id(0), pl.program_id(1)
        finish = _riding_exchange(lambda: _gather_copies(refs[3:3 + n], refs[5 + n:5 + 2 * n], *refs[6 + 2 * n:]),
                                  (p_id == 0) & (i_id == 0), (p_id == HEADS // 2 - 1) & (i_id == nq - 1)) if n else None
        qs = (q_ref[:, :LANES], q_ref[:, LANES:])
        acc_ref[...] = jnp.zeros_like(acc_ref)

        def step(j, carry):
            off = pl.multiple_of(j * tk, tk)
            vc = v_ref[pl.ds(off, tk), :]
            out = []
            for h in range(2):
                m, l = carry[2 * h:2 * h + 2]
                st = lax.dot_general(k_ref[pl.ds(off, tk), h * LANES:(h + 1) * LANES], qs[h], NT_DIMS,
                                     preferred_element_type=F32)
                mn = jnp.maximum(m, jnp.max(st, axis=0, keepdims=True))
                al = jnp.exp2(m - mn)
                pt = jnp.exp2(st - mn)
                l = al * l + jnp.sum(pt, axis=0, keepdims=True)
                acc_ref[h] = al * acc_ref[h] + lax.dot_general(vc, pt.astype(BF16), TN_DIMS, preferred_element_type=F32)
                out += [mn, l]
            return tuple(out)

        init = (jnp.full((1, tq), -1e30, F32), jnp.zeros((1, tq), F32)) * 2
        m0, l0, m1, l1 = lax.fori_loop(0, nk, step, init)
        row = lax.broadcasted_iota(jnp.int32, (LANES, tq), 0)
        o_ref[...] = jnp.where(row < V_DIM, acc_ref[0] / l0, acc_ref[1] / l1).T
        lse_ref[0, 0:1, :] = m0 + jnp.log2(l0)
        lse_ref[0, 1:2, :] = m1 + jnp.log2(l1)
        if n:
            finish()

    return pl.pallas_call(
        kern, grid=(HEADS // 2, nq),
        in_specs=[pl.BlockSpec((tq, 2 * LANES), lambda p, i: (i, p)), pl.BlockSpec((s, 2 * LANES), lambda p, i: (0, p)),
                  pl.BlockSpec((s, LANES), lambda p, i: (0, p))] + [ANY] * n,
        out_specs=[pl.BlockSpec((tq, LANES), lambda p, i: (i, p)), pl.BlockSpec((1, 2, tq), lambda p, i: (p, 0, i))]
        + [ANY] * n,
        out_shape=[jax.ShapeDtypeStruct((s, MLA_W), F32), jax.ShapeDtypeStruct((HEADS // 2, 2, s), F32)]
        + _gather_shapes(shards),
        scratch_shapes=[pltpu.VMEM((2, LANES, tq), F32)] + (_gather_sems(n) if n else []),
        name="attn_fwd", compiler_params=_cparams(2))(q, k, v, *shards)


def _attn_bwd(q, k, v, do, lse, delta, tq, tk, contributions=()):
    s = q.shape[0]
    nq, nk = s // tq, s // tk
    n = len(contributions)

    def kern(*refs):
        q_ref, do_ref, lse_ref, dl_ref, k_ref, v_ref = refs[:6]
        dq_ref, dk_ref, dv_ref = refs[6 + n:9 + n]
        acc_ref = refs[9 + 2 * n]
        p_id, i_id = pl.program_id(0), pl.program_id(1)
        finish = _riding_exchange(lambda: _to_owner_copies(refs[6:6 + n], refs[9 + n:9 + 2 * n], *refs[10 + 2 * n:]),
                                  (p_id == 0) & (i_id == 0), (p_id == HEADS // 2 - 1) & (i_id == nq - 1)) if n else None
        _zero_first(pl.program_id(1), dk_ref, dv_ref)
        acc_ref[...] = jnp.zeros_like(acc_ref)
        qs = (q_ref[:, :LANES], q_ref[:, LANES:])
        doc = do_ref[...]
        lane_q = lax.broadcasted_iota(jnp.int32, (tq, LANES), 1)
        zq = jnp.zeros_like(doc)
        dos = (jnp.where(lane_q < V_DIM, doc, zq), jnp.where(lane_q >= V_DIM, doc, zq))
        lses = (lse_ref[0, 0:1, :], lse_ref[0, 1:2, :])
        dls = (dl_ref[0, 0:1, :], dl_ref[0, 1:2, :])

        def step(j, carry):
            off = pl.multiple_of(j * tk, tk)
            vp = v_ref[pl.ds(off, tk), :]
            lane_k = lax.broadcasted_iota(jnp.int32, (tk, LANES), 1)
            zero = jnp.zeros_like(vp)
            vs = (jnp.where(lane_k < V_DIM, vp, zero), jnp.where(lane_k >= V_DIM, vp, zero))
            for h in range(2):
                sl = slice(h * LANES, (h + 1) * LANES)
                st = lax.dot_general(k_ref[pl.ds(off, tk), sl], qs[h], NT_DIMS, preferred_element_type=F32)
                pt = jnp.exp2(st - lses[h])
                dpt = lax.dot_general(vs[h], doc, NT_DIMS, preferred_element_type=F32)
                dst = (pt * (dpt - dls[h])).astype(BF16)
                dv_ref[pl.ds(off, tk), :] += jnp.dot(pt.astype(BF16), dos[h], preferred_element_type=F32)
                dk_ref[pl.ds(off, tk), sl] += jnp.dot(dst, qs[h], preferred_element_type=F32)
                acc_ref[h] += lax.dot_general(k_ref[pl.ds(off, tk), sl], dst, TN_DIMS, preferred_element_type=F32)
            return carry

        lax.fori_loop(0, nk, step, 0)
        dq_ref[:, :LANES] = acc_ref[0].T
        dq_ref[:, LANES:] = acc_ref[1].T
        if n:
            finish()

    return pl.pallas_call(
        kern, grid=(HEADS // 2, nq),
        in_specs=[pl.BlockSpec((tq, 2 * LANES), lambda p, i: (i, p)), pl.BlockSpec((tq, LANES), lambda p, i: (i, p)),
                  pl.BlockSpec((1, 2, tq), lambda p, i: (p, 0, i)), pl.BlockSpec((1, 2, tq), lambda p, i: (p, 0, i)),
                  pl.BlockSpec((s, 2 * LANES), lambda p, i: (0, p)), pl.BlockSpec((s, LANES), lambda p, i: (0, p))]
        + [ANY] * n,
        out_specs=[pl.BlockSpec((tq, 2 * LANES), lambda p, i: (i, p)), pl.BlockSpec((s, 2 * LANES), lambda p, i: (0, p)),
                   pl.BlockSpec((s, LANES), lambda p, i: (0, p))] + [ANY] * n,
        out_shape=[jax.ShapeDtypeStruct((s, HEADS * LANES), F32), jax.ShapeDtypeStruct((s, HEADS * LANES), F32),
                   jax.ShapeDtypeStruct((s, MLA_W), F32)] + _to_owner_shapes(contributions),
        scratch_shapes=[pltpu.VMEM((2, LANES, tq), F32)] + (_to_owner_sems(n) if n else []),
        name="attn_bwd", compiler_params=_cparams(2))(q, do, lse, delta, k, v, *contributions)


def _mix_out(hf, hb, yg, o, x, g_lru, g_mla, w_out, tm):
    def body(i, n, hf_ref, hb_ref, yg_ref, o_ref, x_ref, gl, gm, w_ref, x1_ref, mix_ref):
        lo = (hf_ref[...] + hb_ref[...]) * _gelu(yg_ref[...])
        ov = o_ref[...]
        mix_ref[:, :LRU_W] = (lo * _rstd(lo) * gl[...]).astype(BF16)
        mix_ref[:, LRU_W:] = (ov * _rstd(ov) * gm[...]).astype(BF16)
        x1_ref[...] = x_ref[...] + jnp.dot(mix_ref[...], w_ref[...], preferred_element_type=F32)

    return _rowwise(body, "mix_out", x.shape[0], tm, rows=[hf, hb, yg, o, x], fulls=[g_lru, g_mla, w_out],
                    outs=[(D_MODEL, F32), (2 * LRU_W, BF16)])


def _mem_kv(mem, g_mem, w_kv, g_k):
    m = mem.shape[0]

    def body(i, n, mem_ref, g_ref, w_ref, gk_ref, km_ref, vm_ref):
        mv = mem_ref[...]
        kv = _bdot(mv * _rstd(mv) * g_ref[...], w_ref[...])
        vm_ref[...] = kv[:, MEM_W:].astype(BF16)
        for h in range(MEM_HEADS):
            sl = slice(h * MEM_HD, (h + 1) * MEM_HD)
            kh = kv[:, sl]
            km_ref[:, sl] = (kh * _rstd(kh) * gk_ref[...]).astype(BF16)

    return _rowwise(body, "mem_kv", m, m, rows=[mem], fulls=[g_mem, w_kv, g_k], outs=[(MEM_W, BF16), (MEM_W, BF16)])


def _mem_attn_core(x1v, g_ref, wq_ref, gq_ref, km_ref, vm_ref):
    scale = MEM_HD ** -0.5
    hm = (x1v * _rstd(x1v) * g_ref[...]).astype(BF16)
    qr = jnp.dot(hm, wq_ref[...], preferred_element_type=F32)
    heads = []
    for h in range(MEM_HEADS):
        sl = slice(h * MEM_HD, (h + 1) * MEM_HD)
        qh = qr[:, sl]
        rs = _rstd(qh)
        qn = (qh * rs * gq_ref[...]).astype(BF16)
        sc = lax.dot_general(qn, km_ref[:, sl], (((1,), (1,)), ((), ())), preferred_element_type=F32) * scale
        e = jnp.exp(sc - jnp.max(sc, axis=-1, keepdims=True))
        p = e / jnp.sum(e, axis=-1, keepdims=True)
        oh = jnp.dot(p.astype(BF16), vm_ref[:, sl], preferred_element_type=F32)
        heads.append((qh, rs, qn, p, oh))
    return hm, heads


def _mem_attn(x1, g, w_q, g_q, km, vm, w_o, tm):
    cs = D_MODEL // N_CHIPS

    def body(i, n, x1_ref, g_ref, wq_ref, gq_ref, km_ref, vm_ref, wo_ref, x2_ref, ob_ref):
        x1v = x1_ref[...]
        _, heads = _mem_attn_core(x1v, g_ref, wq_ref, gq_ref, km_ref, vm_ref)
        for h in range(MEM_HEADS):
            ob_ref[:, h * MEM_HD:(h + 1) * MEM_HD] = heads[h][4].astype(BF16)
        for k in range(N_CHIPS):
            sl = slice(k * cs, (k + 1) * cs)
            x2_ref[:, sl] = x1v[:, sl] + jnp.dot(ob_ref[...], wo_ref[k], preferred_element_type=F32)

    return _rowwise(body, "mem_attn", x1.shape[0], tm, rows=[x1], fulls=[g, w_q, g_q, km, vm, w_o],
                    outs=[(D_MODEL, F32), (MEM_W, BF16)])


def _ffn_up(x2, g, w_up, tm):
    cs = 2 * D_FF // N_CHIPS

    def body(i, n, x_ref, g_ref, w_ref, gu_ref, hb_ref):
        xv = x_ref[...]
        hb_ref[...] = (xv * _rstd(xv) * g_ref[...]).astype(BF16)
        for k in range(N_CHIPS):
            gu_ref[:, k * cs:(k + 1) * cs] = jnp.dot(hb_ref[...], w_ref[k], preferred_element_type=F32)

    return _rowwise(body, "ffn_up", x2.shape[0], tm, rows=[x2], fulls=[g, w_up], outs=[(2 * D_FF, F32), (D_MODEL, BF16)])


def _ffn_conv(gu, prev, nxt, cw_ref, i, n):
    prev = jnp.where(i == 0, 0.0, prev)
    nxt = jnp.where(i == n - 1, 0.0, nxt)
    return (cw_ref[3:4, :] + cw_ref[0:1, :] * _shift_down(gu, 1, prev) + cw_ref[1:2, :] * gu
            + cw_ref[2:3, :] * _shift_up(gu, 1, nxt))


def _ffn_down_loss(gu_pre, x2, target, cw, w_down, tm):
    def body(i, n, gu_ref, x_ref, t_ref, pv_ref, nx_ref, cw_ref, w_ref, dy_ref, dyb_ref, act_ref, dgu_ref, loss_ref):
        _zero_first(i, loss_ref)
        gu = _ffn_conv(gu_ref[...], pv_ref[...], nx_ref[...], cw_ref, i, n)
        g, u = gu[:, :D_FF], gu[:, D_FF:]
        sg = _sigmoid(g)
        a = g * sg
        act_ref[...] = (a * u).astype(BF16)
        y = x_ref[...] + jnp.dot(act_ref[...], w_ref[...], preferred_element_type=F32)
        e = y - t_ref[...]
        loss_ref[...] += jnp.sum(e * e)
        dy = e * (1.0 / D_MODEL)
        dy_ref[...] = dy
        dyb_ref[...] = dy.astype(BF16)
        d_act = lax.dot_general(dyb_ref[...], w_ref[...], NT_DIMS, preferred_element_type=F32)
        dgu_ref[:, :D_FF] = ((d_act * u) * (sg + a - a * sg)).astype(BF16)
        dgu_ref[:, D_FF:] = (d_act * a).astype(BF16)

    return _rowwise(body, "ffn_down_loss", x2.shape[0], tm, rows=[gu_pre, x2, target], halos=[gu_pre], fulls=[cw, w_down],
                    outs=[(D_MODEL, F32), (D_MODEL, BF16), (D_FF, BF16), (2 * D_FF, BF16)], accs=[((SUBLANES, LANES), F32)])


def _ffn_bwd_conv(dgu, gu_pre, cw, tm):
    def body(i, n, d_ref, g_ref, dp_ref, dn_ref, cw_ref, dpre_ref, gc_ref):
        _zero_first(i, gc_ref)
        d = d_ref[...].astype(F32)
        g = g_ref[...]
        d_next = _shift_up(d, 1, jnp.where(i == n - 1, 0.0, dn_ref[...].astype(F32)[:SUBLANES]))
        d_prev = _shift_down(d, 1, jnp.where(i == 0, 0.0, dp_ref[...].astype(F32)[SUBLANES:]))
        dpre_ref[...] = (cw_ref[0:1, :] * d_next + cw_ref[1:2, :] * d + cw_ref[2:3, :] * d_prev).astype(BF16)
        _acc_row(gc_ref, 0, d_next * g)
        _acc_row(gc_ref, 1, d * g)
        _acc_row(gc_ref, 2, d_prev * g)
        _acc_row(gc_ref, 3, d)

    return _rowwise(body, "ffn_bwd_conv", dgu.shape[0], tm, rows=[dgu, gu_pre], halos=[dgu], fulls=[cw],
                    outs=[(2 * D_FF, BF16)], accs=[((SUBLANES, 2 * D_FF), F32)])


def _ffn_bwd_in(dpre, x2, dy, g, w_up, tm):
    cs = 2 * D_FF // N_CHIPS

    def body(i, n, dp_ref, x_ref, dy_ref, g_ref, w_ref, dx_ref, dxb_ref, gg_ref):
        _zero_first(i, gg_ref)
        d_h = jnp.zeros(x_ref.shape, F32)
        for k in range(N_CHIPS):
            d_h = d_h + lax.dot_general(dp_ref[:, k * cs:(k + 1) * cs], w_ref[k], (((1,), (1,)), ((), ())),
                                        preferred_element_type=F32)
        xv = x_ref[...]
        dx, dg = _norm_bwd(xv, _rstd(xv), g_ref[...], d_h)
        _acc_row(gg_ref, 0, dg)
        dx = dx + dy_ref[...]
        dx_ref[...] = dx
        dxb_ref[...] = dx.astype(BF16)

    return _rowwise(body, "ffn_bwd_in", x2.shape[0], tm, rows=[dpre, x2, dy], fulls=[g, w_up],
                    outs=[(D_MODEL, F32), (D_MODEL, BF16)], accs=[((SUBLANES, D_MODEL), F32)])


def _mem_attn_bwd(x1, dx2, dx2b, g, w_q, g_q, km, vm, w_o, tm):
    scale = MEM_HD ** -0.5
    m = km.shape[0]

    def body(i, n, x1_ref, dx2_ref, dx2b_ref, g_ref, wq_ref, gq_ref, km_ref, vm_ref, wo_ref,
             dx1_ref, dx1b_ref, hm_ref, dqr_ref, dkm_ref, dvm_ref, gg_ref, ggq_ref):
        _zero_first(i, dkm_ref, dvm_ref, gg_ref, ggq_ref)
        x1v = x1_ref[...]
        hm, heads = _mem_attn_core(x1v, g_ref, wq_ref, gq_ref, km_ref, vm_ref)
        hm_ref[...] = hm
        cs = D_MODEL // N_CHIPS
        d_o = jnp.zeros((x1v.shape[0], MEM_W), F32)
        for k in range(N_CHIPS):
            d_o = d_o + lax.dot_general(dx2b_ref[:, k * cs:(k + 1) * cs], wo_ref[k], (((1,), (1,)), ((), ())),
                                        preferred_element_type=F32)
        for h in range(MEM_HEADS):
            sl = slice(h * MEM_HD, (h + 1) * MEM_HD)
            qh, rs, qn, p, _ = heads[h]
            d_oh = d_o[:, sl].astype(BF16)
            dp = lax.dot_general(d_oh, vm_ref[:, sl], (((1,), (1,)), ((), ())), preferred_element_type=F32)
            ds = (p * (dp - jnp.sum(dp * p, axis=-1, keepdims=True)) * scale).astype(BF16)
            dqn = jnp.dot(ds, km_ref[:, sl], preferred_element_type=F32)
            dkm_ref[:, sl] += lax.dot_general(ds, qn, (((0,), (0,)), ((), ())), preferred_element_type=F32)
            dvm_ref[:, sl] += lax.dot_general(p.astype(BF16), d_oh, (((0,), (0,)), ((), ())), preferred_element_type=F32)
            dqh, dgq = _norm_bwd(qh, rs, gq_ref[...], dqn)
            _acc_row(ggq_ref, 0, dgq)
            dqr_ref[:, sl] = dqh.astype(BF16)
        d_hm = lax.dot_general(dqr_ref[...], wq_ref[...], (((1,), (1,)), ((), ())), preferred_element_type=F32)
        dx, dg = _norm_bwd(x1v, _rstd(x1v), g_ref[...], d_hm)
        _acc_row(gg_ref, 0, dg)
        dx = dx + dx2_ref[...]
        dx1_ref[...] = dx
        dx1b_ref[...] = dx.astype(BF16)

    return _rowwise(body, "mem_attn_bwd", x1.shape[0], tm, rows=[x1, dx2, dx2b], fulls=[g, w_q, g_q, km, vm, w_o],
                    outs=[(D_MODEL, F32), (D_MODEL, BF16), (D_MODEL, BF16), (MEM_W, BF16)],
                    accs=[((m, MEM_W), F32), ((m, MEM_W), F32), ((SUBLANES, D_MODEL), F32), ((SUBLANES, MEM_HD), F32)])


def _mem_kv_bwd(mem, g_mem, w_kv, g_k, dkm, dvm):
    m = mem.shape[0]

    def body(i, n, mem_ref, dkm_ref, dvm_ref, g_ref, w_ref, gk_ref, gw_ref, gg_ref, ggk_ref, dkv_ref):
        gg_ref[...] = jnp.zeros_like(gg_ref)
        ggk_ref[...] = jnp.zeros_like(ggk_ref)
        mv = mem_ref[...]
        rs_m = _rstd(mv)
        mem_n = (mv * rs_m * g_ref[...]).astype(BF16)
        kv = jnp.dot(mem_n, w_ref[...], preferred_element_type=F32)
        for h in range(MEM_HEADS):
            sl = slice(h * MEM_HD, (h + 1) * MEM_HD)
            kh = kv[:, sl]
            dkh, dgk = _norm_bwd(kh, _rstd(kh), gk_ref[...], dkm_ref[:, sl])
            _acc_row(ggk_ref, 0, dgk)
            dkv_ref[:, sl] = dkh.astype(BF16)
        dkv_ref[:, MEM_W:] = dvm_ref[...].astype(BF16)
        gw_ref[...] = lax.dot_general(mem_n, dkv_ref[...], (((0,), (0,)), ((), ())), preferred_element_type=F32)
        d_mn = lax.dot_general(dkv_ref[...], w_ref[...], (((1,), (1,)), ((), ())), preferred_element_type=F32)
        _acc_row(gg_ref, 0, d_mn * (mv * rs_m))

    return _rowwise(body, "mem_kv_bwd", m, m, rows=[mem, dkm, dvm], fulls=[g_mem, w_kv, g_k],
                    accs=[((D_MODEL, 2 * MEM_W), F32), ((SUBLANES, D_MODEL), F32), ((SUBLANES, MEM_HD), F32),
                          ((m, 2 * MEM_W), BF16)])


def _mix_out_bwd(dx1b, hf, hb, yg, o, g_lru, g_mla, w_out, tm):
    def body(i, n, dx_ref, hf_ref, hb_ref, yg_ref, o_ref, gl, gm, w_ref, dh_ref, dyg_ref, dob_ref, dl_ref, ggl_ref, ggm_ref):
        _zero_first(i, ggl_ref, ggm_ref)
        dmix = lax.dot_general(dx_ref[...], w_ref[...], (((1,), (1,)), ((), ())), preferred_element_type=F32)
        hs = hf_ref[...] + hb_ref[...]
        ygv = yg_ref[...]
        ge = _gelu(ygv)
        lo = hs * ge
        d_lo, dgl = _norm_bwd(lo, _rstd(lo), gl[...], dmix[:, :LRU_W])
        _acc_row(ggl_ref, 0, dgl)
        dh_ref[...] = d_lo * ge
        dyg_ref[...] = d_lo * hs * _gelu_grad(ygv)
        ov = o_ref[...]
        d_o, dgm = _norm_bwd(ov, _rstd(ov), gm[...], dmix[:, LRU_W:])
        _acc_row(ggm_ref, 0, dgm)
        dob_ref[...] = d_o.astype(BF16)
        prod = d_o * ov
        lane_w = lax.broadcasted_iota(jnp.int32, prod.shape, 1)
        lane = lax.broadcasted_iota(jnp.int32, (prod.shape[0], LANES), 1)
        dl = jnp.zeros((prod.shape[0], LANES), F32)
        for h in range(HEADS):
            in_head = (lane_w >= h * V_DIM) & (lane_w < (h + 1) * V_DIM)
            dl = dl + jnp.where(lane == h, jnp.sum(jnp.where(in_head, prod, 0.0), axis=-1, keepdims=True), 0.0)
        dl_ref[...] = dl

    return _rowwise(body, "mix_out_bwd", dx1b.shape[0], tm, rows=[dx1b, hf, hb, yg, o], fulls=[g_lru, g_mla, w_out],
                    outs=[(LRU_W, F32), (LRU_W, F32), (MLA_W, BF16), (LANES, F32)],
                    accs=[((SUBLANES, LRU_W), F32), ((SUBLANES, MLA_W), F32)])


def _mla_qkv_bwd(cq, ckv, krp, cos_t, sin_t, dq, dk, dv, g_qa, g_kva, g_qn, g_kn, w_uq_p, w_uk_p, w_uv, tm):
    scale = QK_HEAD ** -0.5

    def body(i, n, cq_ref, ckv_ref, kr_ref, c_ref, s_ref, dq_ref, dk_ref, dv_ref, gqa, gkva, gqn, gkn, wq, wk, wv,
             dcq_ref, dckv_ref, dkr_ref, cqb_ref, dqr_ref, ckvb_ref, dkn_ref, dvb_ref, ggqa, ggkva, ggqn, ggkn):
        _zero_first(i, ggqa, ggkva, ggqn, ggkn)
        cosv, sinv = c_ref[...], s_ref[...]
        cqv = cq_ref[...]
        rs_q = _rstd(cqv)
        cqb_ref[...] = (cqv * rs_q * gqa[...]).astype(BF16)
        qr = jnp.dot(cqb_ref[...], wq[...], preferred_element_type=F32)
        ckvv = ckv_ref[...]
        rs_kv = _rstd(ckvv)
        ckvb_ref[...] = (ckvv * rs_kv * gkva[...]).astype(BF16)
        kn = jnp.dot(ckvb_ref[...], wk[...], preferred_element_type=F32)
        kr = kr_ref[...]
        dkr = jnp.zeros_like(kr)
        for h in range(HEADS):
            sl = slice(h * LANES, (h + 1) * LANES)
            qh = qr[:, sl]
            d_qn = _rope_t(dq_ref[:, sl] * scale, cosv, sinv)
            dqh, dgq = _norm_bwd(qh, _rstd(qh, QK_HEAD), gqn[...], d_qn, QK_HEAD)
            _acc_row(ggqn, 0, dgq)
            dqr_ref[:, sl] = dqh.astype(BF16)
            kh = kn[:, sl] + kr
            d_kn = _rope_t(dk_ref[:, sl] * (1.0 / LOG2E), cosv, sinv)
            dkh, dgk = _norm_bwd(kh, _rstd(kh, QK_HEAD), gkn[...], d_kn, QK_HEAD)
            _acc_row(ggkn, 0, dgk)
            dkn_ref[:, sl] = dkh.astype(BF16)
            dkr = dkr + dkh
        dkr_ref[...] = dkr
        dvb_ref[...] = dv_ref[...].astype(BF16)
        d_cq = lax.dot_general(dqr_ref[...], wq[...], (((1,), (1,)), ((), ())), preferred_element_type=F32)
        dcq, dg = _norm_bwd(cqv, rs_q, gqa[...], d_cq)
        _acc_row(ggqa, 0, dg)
        dcq_ref[...] = dcq
        d_ckv = (lax.dot_general(dkn_ref[...], wk[...], (((1,), (1,)), ((), ())), preferred_element_type=F32)
                 + lax.dot_general(dvb_ref[...], wv[...], (((1,), (1,)), ((), ())), preferred_element_type=F32))
        dckv, dg = _norm_bwd(ckvv, rs_kv, gkva[...], d_ckv)
        _acc_row(ggkva, 0, dg)
        dckv_ref[...] = dckv

    return _rowwise(body, "mla_qkv_bwd", cq.shape[0], tm, rows=[cq, ckv, krp, cos_t, sin_t, dq, dk, dv],
                    fulls=[g_qa, g_kva, g_qn, g_kn, w_uq_p, w_uk_p, w_uv],
                    outs=[(Q_LORA, F32), (KV_LORA, F32), (LANES, F32), (Q_LORA, BF16), (HEADS * LANES, BF16),
                          (KV_LORA, BF16), (HEADS * LANES, BF16), (MLA_W, BF16)],
                    accs=[((SUBLANES, Q_LORA), F32), ((SUBLANES, KV_LORA), F32), ((SUBLANES, LANES), F32),
                          ((SUBLANES, LANES), F32)])


def _in_proj_bwd(x, dx1, dxr_f, dxr_b, dyg, dcq, dckv, dkrp, g, w_in_p, tm):
    def body(i, n, x_ref, dx1_ref, df_ref, db_ref, dyg_ref, dcq_ref, dckv_ref, dkr_ref, g_ref, w_ref, gx_ref, dp_ref, gg_ref):
        _zero_first(i, gg_ref)
        dp_ref[:, :LRU_W] = (df_ref[...] + db_ref[...]).astype(BF16)
        dp_ref[:, LRU_W:2 * LRU_W] = dyg_ref[...].astype(BF16)
        dp_ref[:, 2 * LRU_W:2 * LRU_W + Q_LORA] = dcq_ref[...].astype(BF16)
        dp_ref[:, 2 * LRU_W + Q_LORA:OFF_KR] = dckv_ref[...].astype(BF16)
        dp_ref[:, OFF_KR:] = dkr_ref[...].astype(BF16)
        d_h = lax.dot_general(dp_ref[...], w_ref[...], (((1,), (1,)), ((), ())), preferred_element_type=F32)
        xv = x_ref[...]
        dx, dg = _norm_bwd(xv, _rstd(xv), g_ref[...], d_h)
        _acc_row(gg_ref, 0, dg)
        gx_ref[...] = dx + dx1_ref[...]

    return _rowwise(body, "in_proj_bwd", x.shape[0], tm, rows=[x, dx1, dxr_f, dxr_b, dyg, dcq, dckv, dkrp],
                    fulls=[g, w_in_p], outs=[(D_MODEL, F32), (IN_PAD, BF16)], accs=[((SUBLANES, D_MODEL), F32)])


ANY = pl.BlockSpec(memory_space=pl.ANY)


def _chip_peers(x, y):
    return ((1 - x, y), (x, 1 - y), (1 - x, 1 - y))


def _exchange_call(kern, name, ins, out_shapes, n_sems, aliases=None):
    return pl.pallas_call(
        kern, in_specs=[ANY] * len(ins), out_specs=[ANY] * len(out_shapes), out_shape=out_shapes,
        scratch_shapes=[pltpu.SemaphoreType.DMA((n,)) for n in n_sems], input_output_aliases=aliases or {},
        name=name)(*ins)


def _start_then_wait(copies):
    for cp in copies:
        cp.start()
    for cp in copies:
        cp.wait()


N_DEV = 8
RELATIONS = tuple((dx, dy, dc) for dx in (0, 1) for dy in (0, 1) for dc in (0, 1))[1:]


def _flip(v, d):
    return 1 - v if d else v


def _gather_copies(ins, outs, ssem, rsem, lsem):
    x, y, c = lax.axis_index("x"), lax.axis_index("y"), lax.axis_index("c")
    me = 2 * x + y
    cps = []
    for i, (a, o) in enumerate(zip(ins, outs)):
        cps.append(pltpu.make_async_copy(a, o.at[me], lsem.at[i]))
        for j, (px, py) in enumerate(_chip_peers(x, y)):
            cps.append(pltpu.make_async_remote_copy(a, o.at[me], ssem.at[3 * i + j], rsem.at[3 * i + j],
                                                    device_id=(px, py, c), device_id_type=MESH))
    return cps


def _gather_shapes(arrs):
    return [jax.ShapeDtypeStruct((N_CHIPS,) + a.shape, a.dtype) for a in arrs]


def _gather_sems(n):
    return [pltpu.SemaphoreType.DMA((3 * n,)), pltpu.SemaphoreType.DMA((3 * n,)), pltpu.SemaphoreType.DMA((n,))]


def _gather_chips(arrs):
    n = len(arrs)

    def kern(*refs):
        _start_then_wait(_gather_copies(refs[:n], refs[n:2 * n], *refs[2 * n:]))

    return _exchange_call(kern, "gather_weights", arrs, _gather_shapes(arrs), (3 * n, 3 * n, n))


def _to_owner_copies(ins, outs, ssem, rsem, lsem):
    x, y, c = lax.axis_index("x"), lax.axis_index("y"), lax.axis_index("c")
    me = 4 * x + 2 * y + c
    cps = []
    for i, (a, o) in enumerate(zip(ins, outs)):
        cps.append(pltpu.make_async_copy(a.at[2 * x + y, c], o.at[me], lsem.at[i]))
        for r, (dx, dy, dc) in enumerate(RELATIONS):
            tx, ty, tc = _flip(x, dx), _flip(y, dy), _flip(c, dc)
            cps.append(pltpu.make_async_remote_copy(a.at[2 * tx + ty, tc], o.at[me], ssem.at[7 * i + r], rsem.at[7 * i + r],
                                                    device_id=(tx, ty, tc), device_id_type=MESH))
    return cps


def _to_owner_shapes(arrs):
    return [jax.ShapeDtypeStruct((N_DEV,) + a.shape[2:], a.dtype) for a in arrs]


def _to_owner_sems(n):
    return [pltpu.SemaphoreType.DMA((7 * n,)), pltpu.SemaphoreType.DMA((7 * n,)), pltpu.SemaphoreType.DMA((n,))]


def _to_owner(arrs, name):
    n = len(arrs)

    def kern(*refs):
        _start_then_wait(_to_owner_copies(refs[:n], refs[n:2 * n], *refs[2 * n:]))

    return _exchange_call(kern, name, arrs, _to_owner_shapes(arrs), (7 * n, 7 * n, n))


def _join_halves(arrs):
    n = len(arrs)

    def kern(*refs):
        outs, (ssem, rsem) = refs[n:2 * n], refs[2 * n:]
        x, y, c = lax.axis_index("x"), lax.axis_index("y"), lax.axis_index("c")
        _start_then_wait([
            pltpu.make_async_remote_copy(outs[i].at[c], outs[i].at[c], ssem.at[i], rsem.at[i],
                                         device_id=(x, y, 1 - c), device_id_type=MESH) for i in range(n)])

    outs = [jax.ShapeDtypeStruct(a.shape, a.dtype) for a in arrs]
    return _exchange_call(kern, "grad_join_halves", arrs, outs, (n, n), aliases={i: i for i in range(n)})


def _row_block(rows, row_bytes, limit=1 << 20):
    best = None
    for d in range(16, rows + 1, 16):
        if rows % d == 0 and d * row_bytes <= limit:
            best = d
    return best if best is not None else rows


def _sum_devices(b, c, name):
    _, h, cols = b.shape
    hb = _row_block(h, cols * 4)

    def kern(c_ref, b_ref, o_ref):
        acc = b_ref[0].astype(F32)
        for j in range(1, N_DEV):
            acc = acc + b_ref[j].astype(F32)
        o_ref[...] = acc

    return pl.pallas_call(
        kern,
        grid_spec=pltpu.PrefetchScalarGridSpec(
            num_scalar_prefetch=1, grid=(h // hb,),
            in_specs=[pl.BlockSpec((N_DEV, hb, cols), lambda i, c_ref: (0, i, 0))],
            out_specs=pl.BlockSpec((None, hb, cols), lambda i, c_ref: (c_ref[0], i, 0))),
        out_shape=jax.ShapeDtypeStruct((2, h, cols), F32), name=name, compiler_params=_cparams(1))(c, b)


def _adamw(w, g, m, v, name):
    rows, cols = w.shape
    rb = _row_block(rows, cols * 4)
    c1 = 1.0 - ADAM_B1 ** ADAM_STEP
    c2 = 1.0 - ADAM_B2 ** ADAM_STEP

    def kern(w_ref, g_ref, m_ref, v_ref, d_ref, mo_ref, vo_ref):
        gv = g_ref[...]
        mn = ADAM_B1 * m_ref[...] + (1.0 - ADAM_B1) * gv
        vn = ADAM_B2 * v_ref[...] + (1.0 - ADAM_B2) * (gv * gv)
        mo_ref[...] = mn
        vo_ref[...] = vn
        d_ref[...] = (-ADAM_LR) * ((mn / c1) / (jnp.sqrt(vn / c2) + ADAM_EPS) + ADAM_WD * w_ref[...])

    spec = pl.BlockSpec((rb, cols), lambda i: (i, 0))
    return pl.pallas_call(
        kern, grid=(rows // rb,), in_specs=[spec] * 4, out_specs=[spec] * 3,
        out_shape=[jax.ShapeDtypeStruct(w.shape, F32)] * 3, name=name, compiler_params=_cparams(1))(w, g, m, v)


def _pad_rows(flat, rows):
    return jnp.pad(flat, (0, rows * LANES - flat.shape[0])).reshape(rows, LANES)


def _round_up(n, m):
    return (n + m - 1) // m * m


def _shard_shape(shape, axis):
    return tuple(s // N_CHIPS if a == axis else s for a, s in enumerate(shape))


def _to_shards(full, axis):
    shape = full.shape
    t = full.reshape(shape[:axis] + (N_CHIPS, shape[axis] // N_CHIPS) + shape[axis + 1:])
    return jnp.moveaxis(t, axis, 0).reshape(N_CHIPS, -1)


def _from_shards(sh, shape, axis):
    t = sh.reshape((N_CHIPS,) + _shard_shape(shape, axis))
    t = jnp.moveaxis(t, 0, axis)
    return t.reshape(shape)


BIG = tuple((name, shape, axis) for name, shape, axis, big in SHARDED if big)
EARLY_WEIGHTS = ("w_in", "w_uq", "w_ukv")
SMALL_SHARDED = tuple((name, shape, axis) for name, shape, axis, big in SHARDED if not big)


def _pack_small_weights(p):
    flat = jnp.concatenate([p[name].reshape(-1) for name, _, _ in SMALL_SHARDED])
    return _pad_rows(flat, _round_up(-(-flat.shape[0] // LANES), SUBLANES))


def _unpack_small_weights(gathered):
    flat = gathered.reshape(N_CHIPS, -1)
    out, off = {}, 0
    for name, shape, axis in SMALL_SHARDED:
        n = _numel(shape) // N_CHIPS
        out[name] = _from_shards(flat[:, off:off + n], shape, axis)
        off += n
    return out


def _pack_small_local(p, prefix=""):
    parts = [p[prefix + name].reshape(-1) for name, _, _ in SMALL_SHARDED]
    parts += [p[prefix + name].reshape(-1) for name, _ in REPLICATED]
    return jnp.concatenate(parts)


def _pack_small_grads(g):
    parts = [_to_shards(g[name], axis) for name, _, axis in SMALL_SHARDED]
    rep = jnp.concatenate([g[name].reshape(-1) for name, _ in REPLICATED])
    parts.append(jnp.broadcast_to(rep[None], (N_CHIPS, rep.shape[0])))
    return jnp.concatenate(parts, axis=1)


def _unpack_small_local(flat):
    out, off = {}, 0
    for name, shape, axis in SMALL_SHARDED:
        n = _numel(shape) // N_CHIPS
        out[name] = flat[off:off + n].reshape((1,) + _shard_shape(shape, axis))
        off += n
    for name, shape in REPLICATED:
        n = _numel(shape)
        out[name] = flat[off:off + n].reshape((1,) + shape)
        off += n
    return out


def _grad_shards(g, shape, axis):
    if axis == 0:
        return g.reshape((N_CHIPS,) + _shard_shape(shape, axis))
    return jnp.transpose(g.reshape(shape[0], N_CHIPS, shape[1] // N_CHIPS), (1, 0, 2))


def _cols_from_shards(w4):
    return jnp.transpose(w4, (1, 0, 2)).reshape(w4.shape[1], -1)


def _block_diag(w):
    eye = jnp.eye(LRU_BLOCKS, dtype=w.dtype)
    return jnp.einsum("ncd,nm->ncmd", w, eye).reshape(LRU_W, LRU_W)


def _block_diag_t(g):
    g4 = g.reshape(LRU_BLOCKS, 64, LRU_BLOCKS, 64)
    eye = jnp.eye(LRU_BLOCKS, dtype=g.dtype)[:, None, :, None]
    return jnp.sum(g4 * eye, axis=2)


def _pad8(a):
    return jnp.pad(a, ((0, SUBLANES - a.shape[0]), (0, 0)))


def kernel(x, mem, positions, attn_norm, w_in, lru_conv_w, lru_conv_b, lru_w_a, lru_b_a, lru_w_i, lru_b_i, lru_lambda, q_a_norm, w_uq, kv_a_norm, w_ukv, mla_q_norm, mla_k_norm, lru_out_norm, mla_out_norm, w_out, mem_attn_norm, mem_norm, w_mem_q, w_mem_kv, mem_q_norm, mem_k_norm, w_mem_o, ffn_norm, w_up, ffn_conv_w, ffn_conv_b, w_down, loss_target, m_attn_norm, m_w_in, m_lru_conv_w, m_lru_conv_b, m_lru_w_a, m_lru_b_a, m_lru_w_i, m_lru_b_i, m_lru_lambda, m_q_a_norm, m_w_uq, m_kv_a_norm, m_w_ukv, m_mla_q_norm, m_mla_k_norm, m_lru_out_norm, m_mla_out_norm, m_w_out, m_mem_attn_norm, m_mem_norm, m_w_mem_q, m_w_mem_kv, m_mem_q_norm, m_mem_k_norm, m_w_mem_o, m_ffn_norm, m_w_up, m_ffn_conv_w, m_ffn_conv_b, m_w_down, v_attn_norm, v_w_in, v_lru_conv_w, v_lru_conv_b, v_lru_w_a, v_lru_b_a, v_lru_w_i, v_lru_b_i, v_lru_lambda, v_q_a_norm, v_w_uq, v_kv_a_norm, v_w_ukv, v_mla_q_norm, v_mla_k_norm, v_lru_out_norm, v_mla_out_norm, v_w_out, v_mem_attn_norm, v_mem_norm, v_w_mem_q, v_w_mem_kv, v_mem_q_norm, v_mem_k_norm, v_w_mem_o, v_ffn_norm, v_w_up, v_ffn_conv_w, v_ffn_conv_b, v_w_down):
    given = dict(locals())
    local = {name: given[name][0] for name in WEIGHT_ORDER}
    s = x.shape[1]
    x2d, mem2d, tgt = x[0], mem[0], loss_target[0]
    tm = min(512, s)
    tm_wide = min(1024, s)
    tm_ffn = min(256, s)
    t_scan = min(1024, s)
    tq_f, tq_b, tk = min(4096, s), min(2048, s), min(512, s)

    early = [b for b in BIG if b[0] in EARLY_WEIGHTS]
    late = [b for b in BIG if b[0] not in EARLY_WEIGHTS]
    got = _gather_chips([local[name].astype(BF16) for name, _, _ in early] + [_pack_small_weights(local)])
    full = _unpack_small_weights(got[-1])

    def take_gathered(entries, arrays):
        for (name, shape, axis), w4 in zip(entries, arrays):
            if axis == 0:
                full[name] = w4.reshape(shape)
            elif name in ("w_up", "w_mem_o"):
                full[name] = w4
            else:
                full[name] = _cols_from_shards(w4)

    take_gathered(early, got)
    row = lambda a: a.reshape(1, -1)
    b16 = lambda a: a.astype(BF16)
    zeros = lambda r, c: jnp.zeros((r, c), BF16)
    w_in_f = full["w_in"]
    w_in_p = jnp.concatenate([w_in_f[:, :OFF_KR], _head_tile(zeros(D_MODEL, QK_NOPE), w_in_f[:, OFF_KR:])], axis=1)
    uq = full["w_uq"].reshape(Q_LORA, HEADS, QK_HEAD)
    w_uq_p = _head_tile(uq[:, :, :QK_NOPE], uq[:, :, QK_NOPE:]).reshape(Q_LORA, -1)
    ukv = full["w_ukv"].reshape(KV_LORA, HEADS, QK_NOPE + V_DIM)
    w_uk_p = _head_tile(ukv[:, :, :QK_NOPE], None).reshape(KV_LORA, -1)
    w_uv = ukv[:, :, QK_NOPE:].reshape(KV_LORA, MLA_W)
    wa = [b16(_block_diag(local["lru_w_a"][d])) for d in range(2)]
    wi = [b16(_block_diag(local["lru_w_i"][d])) for d in range(2)]
    cw = [_pad8(full["lru_conv_w"][d]) for d in range(2)]
    pv = [_pad8(jnp.stack([full["lru_conv_b"][d], full["lru_b_a"][d], full["lru_b_i"][d], full["lru_lambda"][d]]))
          for d in range(2)]
    ffn_cw = _pad8(jnp.concatenate([full["ffn_conv_w"], row(local["ffn_conv_b"])], axis=0))
    g_attn, g_qa, g_kva = row(local["attn_norm"]), row(local["q_a_norm"]), row(local["kv_a_norm"])
    g_qn = _head_tile(row(local["mla_q_norm"])[:, :QK_NOPE], row(local["mla_q_norm"])[:, QK_NOPE:])
    g_kn = _head_tile(row(local["mla_k_norm"])[:, :QK_NOPE], row(local["mla_k_norm"])[:, QK_NOPE:])
    g_lru, g_mla = row(local["lru_out_norm"]), row(local["mla_out_norm"])
    g_memattn, g_mem = row(local["mem_attn_norm"]), row(local["mem_norm"])
    g_mq, g_mk, g_ffn = row(local["mem_q_norm"]), row(local["mem_k_norm"]), row(local["ffn_norm"])

    inv = ROPE_THETA ** (-jnp.arange(0, QK_ROPE, 2, dtype=F32) / QK_ROPE)
    no_nope = jnp.zeros((1, QK_NOPE), F32)
    inv_tile = _head_tile(no_nope, jnp.concatenate([inv, inv])[None])
    sign_tile = _head_tile(no_nope, jnp.concatenate([-jnp.ones_like(inv), jnp.ones_like(inv)])[None])
    ang = positions[0].astype(F32)[:, None] * inv_tile
    cos_t, sin_t = jnp.cos(ang), jnp.sin(ang) * sign_tile

    xr, yg, cq, ckv, krp, hb_in = _in_proj(x2d, g_attn, w_in_p, tm_wide)
    h_f, *saved_f = _lru_scan_fwd(xr, cw[0], pv[0], wa[0], wi[0], False, t_scan)
    h_b, *saved_b = _lru_scan_fwd(xr, cw[1], pv[1], wa[1], wi[1], True, t_scan)
    q, k, v = _mla_qkv(cq, ckv, krp, cos_t, sin_t, g_qa, g_kva, g_qn, g_kn, w_uq_p, w_uk_p, w_uv, tm_wide)
    o, lse, *got = _attn_fwd(q, k, v, tq_f, tk, shards=[local[name].astype(BF16) for name, _, _ in late])
    take_gathered(late, got)
    x1, mixed = _mix_out(h_f, h_b, yg, o, x2d, g_lru, g_mla, full["w_out"], tm_wide)
    km, vm = _mem_kv(mem2d, g_mem, full["w_mem_kv"], g_mk)
    x2, o_mem = _mem_attn(x1, g_memattn, full["w_mem_q"], g_mq, km, vm, full["w_mem_o"], tm_wide)
    gu_pre, hb_ffn = _ffn_up(x2, g_ffn, full["w_up"], tm)
    dy, dyb, act, dgu, loss_acc = _ffn_down_loss(gu_pre, x2, tgt, ffn_cw, full["w_down"], tm_ffn)
    loss = lax.psum(loss_acc[0, 0] * (0.5 / D_MODEL), ("x", "y", "c"))

    grads = {}
    grads["w_down"] = _matmul_tn(act, dyb, "grad_w_down", out_dtype=BF16)
    dpre, g_conv = _ffn_bwd_conv(dgu, gu_pre, ffn_cw, tm_ffn)
    grads["ffn_conv_w"], grads["ffn_conv_b"] = g_conv[:3], g_conv[3]
    grads["w_up"] = _matmul_tn(hb_ffn, dpre, "grad_w_up", col_shards=True, out_dtype=BF16)
    dx2, dx2b, gg = _ffn_bwd_in(dpre, x2, dy, g_ffn, full["w_up"], tm)
    grads["ffn_norm"] = gg[0]
    grads["w_mem_o"] = _matmul_tn(o_mem, dx2b, "grad_w_mem_o", out_dtype=BF16)
    dx1, dx1b, hm, dqr_mem, dkm, dvm, gg, ggq = _mem_attn_bwd(x1, dx2, dx2b, g_memattn, full["w_mem_q"], g_mq, km, vm,
                                                                 full["w_mem_o"], tm)
    grads["mem_attn_norm"], grads["mem_q_norm"] = gg[0], ggq[0]
    grads["w_mem_q"] = _matmul_tn(hm, dqr_mem, "grad_w_mem_q", out_dtype=BF16)
    g_mem_kv, gg, ggk, _ = _mem_kv_bwd(mem2d, g_mem, full["w_mem_kv"], g_mk, dkm, dvm)
    grads["w_mem_kv"] = g_mem_kv.astype(BF16)
    grads["mem_norm"], grads["mem_k_norm"] = gg[0], ggk[0]
    grads["w_out"] = _matmul_tn(mixed, dx1b, "grad_w_out", out_dtype=BF16)
    dh, dyg, dob, dl128, ggl, ggm = _mix_out_bwd(dx1b, h_f, h_b, yg, o, g_lru, g_mla, full["w_out"], tm)
    grads["lru_out_norm"], grads["mla_out_norm"] = ggl[0], ggm[0]
    delta_t = jnp.transpose(dl128[:, :HEADS]).reshape(HEADS // 2, 2, s)
    def halves(name, shape, axis):
        g4 = grads[name] if grads[name].ndim == 3 else _grad_shards(grads[name], shape, axis)
        return g4.reshape(N_CHIPS, 2, g4.shape[1] // 2, g4.shape[2])

    dq, dk, dv, *arrived_late = _attn_bwd(q, k, v, dob, lse, delta_t, tq_b, tk,
                                          contributions=[halves(*e) for e in late])
    (dcq, dckv, dkrp, cqb, dqr, ckvb, dkn, dvb, ggqa, ggkva, ggqn, ggkn) = _mla_qkv_bwd(
        cq, ckv, krp, cos_t, sin_t, dq, dk, dv, g_qa, g_kva, g_qn, g_kn, w_uq_p, w_uk_p, w_uv, tm_wide)
    grads["q_a_norm"], grads["kv_a_norm"] = ggqa[0], ggkva[0]
    grads["mla_q_norm"] = jnp.concatenate(_from_head_tile(ggqn[0]))
    grads["mla_k_norm"] = jnp.concatenate(_from_head_tile(ggkn[0]))
    g_uq_p = _matmul_tn(cqb, dqr, "grad_w_uq")
    grads["w_uq"] = jnp.concatenate(_from_head_tile(g_uq_p.reshape(Q_LORA, HEADS, LANES)), axis=-1).reshape(Q_LORA, -1)
    g_uk_p = _from_head_tile(_matmul_tn(ckvb, dkn, "grad_w_uk").reshape(KV_LORA, HEADS, LANES))[0]
    g_uv = _matmul_tn(ckvb, dvb, "grad_w_uv").reshape(KV_LORA, HEADS, V_DIM)
    grads["w_ukv"] = jnp.concatenate([g_uk_p, g_uv], axis=2).reshape(KV_LORA, -1)
    dxr, gwa, gwi, gvec = [], [], [], []
    for d, (hd, saved) in enumerate(((h_f, saved_f), (h_b, saved_b))):
        r = _lru_scan_bwd(xr, saved, hd, dh, cw[d], pv[d], wa[d], wi[d], d == 1, t_scan)
        dxr.append(r[0])
        gwa.append(_block_diag_t(r[1]))
        gwi.append(_block_diag_t(r[2]))
        gvec.append(r[3])
    grads["lru_w_a"], grads["lru_w_i"] = jnp.stack(gwa), jnp.stack(gwi)
    grads["lru_conv_w"] = jnp.stack([gv[:CONV_W] for gv in gvec])
    for r_i, name in ((4, "lru_conv_b"), (5, "lru_b_a"), (6, "lru_b_i"), (7, "lru_lambda")):
        grads[name] = jnp.stack([gv[r_i] for gv in gvec])
    grad_x, dproj, gg = _in_proj_bwd(x2d, dx1, dxr[0], dxr[1], dyg, dcq, dckv, dkrp, g_attn, w_in_p, tm)
    grads["attn_norm"] = gg[0]
    g_in_p = _matmul_tn(hb_in, dproj, "grad_w_in")
    grads["w_in"] = jnp.concatenate([g_in_p[:, :OFF_KR], _from_head_tile(g_in_p[:, OFF_KR:])[1]], axis=1)

    small = _pack_small_grads(grads)
    length = small.shape[1]
    hrows = _round_up(-(-length // (2 * LANES)), 16)
    small = jnp.pad(small, ((0, 0), (0, 2 * hrows * LANES - length))).reshape(N_CHIPS, 2, hrows, LANES)
    for name, _, _ in early:
        grads[name] = grads[name].astype(BF16)
    arrived_early = _to_owner([halves(*e) for e in early] + [small], "grad_to_owner")
    names = [name for name, _, _ in late + early] + ["small"]
    c_idx = lax.axis_index("c").astype(jnp.int32).reshape(1)
    reduced = _join_halves([_sum_devices(b, c_idx, "grad_sum_" + n)
                            for n, b in zip(names, list(arrived_late) + list(arrived_early))])

    outs = [{}, {}, {}, {}]
    for (name, shape, axis), r in zip(late + early, reduced):
        g2 = r.reshape(_shard_shape(shape, axis))
        res = _adamw(local[name], g2, given["m_" + name][0], given["v_" + name][0], "adamw_" + name)
        for o_, a in zip(outs, (g2, *res)):
            o_[name] = a[None]
    pack = lambda prefix: _pad_rows(_pack_small_local({n: given[prefix + n] for n in WEIGHT_ORDER}), 2 * hrows)
    g_small = reduced[-1].reshape(2 * hrows, LANES)
    res = _adamw(pack(""), g_small, pack("m_"), pack("v_"), "adamw_small")
    for o_, a in zip(outs, (g_small, *res)):
        o_.update(_unpack_small_local(a.reshape(-1)))
    return (loss, grad_x[None], *[o_[n] for o_ in outs for n in WEIGHT_ORDER])
```

```python
import jax
import jax.numpy as jnp
from jax import lax
from jax.experimental import pallas as pl
from jax.experimental.pallas import tpu as pltpu

F32, BF16 = jnp.float32, jnp.bfloat16
MESH = pl.DeviceIdType.MESH

D_MODEL = 1024
EPS = 1e-6
LRU_W = 512
LRU_BLOCKS = 8
LRU_C = 8.0
CONV_W = 4
HEADS = 8
QK_NOPE, QK_ROPE, QK_HEAD, V_DIM = 64, 32, 96, 64
Q_LORA, KV_LORA = 256, 128
MLA_W = HEADS * V_DIM
ROPE_THETA = 10000.0
IN_COLS = 2 * LRU_W + Q_LORA + KV_LORA + QK_ROPE
OFF_KR = IN_COLS - QK_ROPE
IN_PAD = 1536
MEM_HEADS, MEM_HD = 4, 128
MEM_W = MEM_HEADS * MEM_HD
D_FF = 2816
N_CHIPS = 4
ADAM_LR, ADAM_B1, ADAM_B2, ADAM_EPS, ADAM_WD, ADAM_STEP = 0.001, 0.9, 0.999, 1e-08, 0.01, 10

LANES = 128
SUBLANES = 8
V7X_VMEM_BYTES = 64 * 1024 * 1024
VMEM_LIMIT = V7X_VMEM_BYTES * 7 // 8

SHARDED = (
    ("w_in", (D_MODEL, IN_COLS), 1, True),
    ("lru_conv_w", (2, CONV_W, LRU_W), 2, False),
    ("lru_conv_b", (2, LRU_W), 1, False),
    ("lru_b_a", (2, LRU_W), 1, False),
    ("lru_b_i", (2, LRU_W), 1, False),
    ("lru_lambda", (2, LRU_W), 1, False),
    ("w_uq", (Q_LORA, HEADS * QK_HEAD), 1, True),
    ("w_ukv", (KV_LORA, HEADS * (QK_NOPE + V_DIM)), 1, True),
    ("w_out", (2 * LRU_W, D_MODEL), 0, True),
    ("w_mem_q", (D_MODEL, MEM_W), 0, True),
    ("w_mem_kv", (D_MODEL, 2 * MEM_W), 0, True),
    ("w_mem_o", (MEM_W, D_MODEL), 1, True),
    ("w_up", (D_MODEL, 2 * D_FF), 1, True),
    ("ffn_conv_w", (3, 2 * D_FF), 1, False),
    ("w_down", (D_FF, D_MODEL), 0, True),
)
REPLICATED = (
    ("attn_norm", (D_MODEL,)), ("lru_w_a", (2, LRU_BLOCKS, 64, 64)), ("lru_w_i", (2, LRU_BLOCKS, 64, 64)),
    ("q_a_norm", (Q_LORA,)), ("kv_a_norm", (KV_LORA,)), ("mla_q_norm", (QK_HEAD,)), ("mla_k_norm", (QK_HEAD,)),
    ("lru_out_norm", (LRU_W,)), ("mla_out_norm", (MLA_W,)), ("mem_attn_norm", (D_MODEL,)), ("mem_norm", (D_MODEL,)),
    ("mem_q_norm", (MEM_HD,)), ("mem_k_norm", (MEM_HD,)), ("ffn_norm", (D_MODEL,)), ("ffn_conv_b", (2 * D_FF,)),
)
WEIGHT_ORDER = ('attn_norm', 'w_in', 'lru_conv_w', 'lru_conv_b', 'lru_w_a', 'lru_b_a', 'lru_w_i', 'lru_b_i', 'lru_lambda',
                'q_a_norm', 'w_uq', 'kv_a_norm', 'w_ukv', 'mla_q_norm', 'mla_k_norm', 'lru_out_norm', 'mla_out_norm', 'w_out',
                'mem_attn_norm', 'mem_norm', 'w_mem_q', 'w_mem_kv', 'mem_q_norm', 'mem_k_norm', 'w_mem_o', 'ffn_norm', 'w_up',
                'ffn_conv_w', 'ffn_conv_b', 'w_down')


def _numel(shape):
    n = 1
    for s in shape:
        n *= s
    return n


def _cparams(n_axes):
    return pltpu.CompilerParams(dimension_semantics=("arbitrary",) * n_axes, vmem_limit_bytes=VMEM_LIMIT)


def _bdot(a, b):
    return jnp.dot(a.astype(BF16), b.astype(BF16), preferred_element_type=F32)


def _bdot_nt(a, b):
    return lax.dot_general(a.astype(BF16), b.astype(BF16), (((1,), (1,)), ((), ())), preferred_element_type=F32)


def _bdot_tn(a, b):
    return lax.dot_general(a.astype(BF16), b.astype(BF16), (((0,), (0,)), ((), ())), preferred_element_type=F32)


def _rstd(x, n=None):
    n = x.shape[-1] if n is None else n
    return lax.rsqrt(jnp.sum(x * x, axis=-1, keepdims=True) * (1.0 / n) + EPS)


def _norm_bwd(x, rs, g, dy, n=None):
    n = x.shape[-1] if n is None else n
    xhat = x * rs
    dxh = dy * g
    dx = rs * (dxh - xhat * (jnp.sum(dxh * xhat, axis=-1, keepdims=True) * (1.0 / n)))
    return dx, dy * xhat


def _acc_row(ref, r, val):
    ref[r:r + 1, :] += jnp.sum(val, axis=0, keepdims=True)


def _zero_first(i, *refs):
    @pl.when(i == 0)
    def _():
        for r in refs:
            r[...] = jnp.zeros_like(r)


def _shift_down(x, j, halo):
    if j == 0:
        return x
    xs = pltpu.roll(x, j, 0)
    hs = pltpu.roll(halo, j, 0)
    row = lax.broadcasted_iota(jnp.int32, hs.shape, 0)
    top = jnp.where(row < j, hs, xs[:SUBLANES])
    return jnp.concatenate([top, xs[SUBLANES:]], axis=0)


def _shift_up(x, j, halo):
    if j == 0:
        return x
    t = x.shape[0]
    xs = pltpu.roll(x, t - j, 0)
    hs = pltpu.roll(halo, SUBLANES - j, 0)
    row = lax.broadcasted_iota(jnp.int32, hs.shape, 0)
    bot = jnp.where(row >= SUBLANES - j, hs, xs[t - SUBLANES:])
    return jnp.concatenate([xs[:t - SUBLANES], bot], axis=0)


def _shift(x, j, halo, down):
    return _shift_down(x, j, halo) if down else _shift_up(x, j, halo)


def _scan(a, b, h_in, down):
    t, c = a.shape
    g = t // SUBLANES
    a3, b3 = a.reshape(g, SUBLANES, c), b.reshape(g, SUBLANES, c)
    sub = lax.broadcasted_iota(jnp.int32, a3.shape, 1)
    d = 1
    while d < SUBLANES:
        keep = (sub >= d) if down else (sub < SUBLANES - d)
        shift = d if down else SUBLANES - d
        a_s = jnp.where(keep, pltpu.roll(a3, shift, 1), 1.0)
        b_s = jnp.where(keep, pltpu.roll(b3, shift, 1), 0.0)
        b3 = a3 * b_s + b3
        a3 = a3 * a_s
        d *= 2
    hs = [None] * g
    carry = h_in
    for i in (range(g) if down else range(g - 1, -1, -1)):
        hs[i] = a3[i] * carry + b3[i]
        carry = hs[i][SUBLANES - 1:, :] if down else hs[i][:1, :]
    return jnp.concatenate(hs, axis=0)


def _sigmoid(x):
    return 0.5 * jnp.tanh(0.5 * x) + 0.5


LOG2E = 1.4426950408889634
GELU_K = 0.7978845608028654
GELU_C = 0.044715


def _gelu(x):
    return 0.5 * x * (1.0 + jnp.tanh(GELU_K * (x + GELU_C * x * x * x)))


def _gelu_grad(x):
    t = jnp.tanh(GELU_K * (x + GELU_C * x * x * x))
    return 0.5 * (1.0 + t) + 0.5 * x * (1.0 - t * t) * GELU_K * (1.0 + 3.0 * GELU_C * x * x)


ROPE_HALF = QK_ROPE // 2
ROPE_LANE = 32


def _head_tile(nope, rope):
    z = lambda n: jnp.zeros(nope.shape[:-1] + (n,), nope.dtype)
    r1, r2 = (z(ROPE_HALF), z(ROPE_HALF)) if rope is None else (rope[..., :ROPE_HALF], rope[..., ROPE_HALF:])
    return jnp.concatenate([nope[..., :ROPE_LANE], r1, nope[..., ROPE_LANE:], z(ROPE_HALF), r2, z(ROPE_HALF)], axis=-1)


def _from_head_tile(t):
    a, b = ROPE_LANE + ROPE_HALF, ROPE_LANE + LANES // 2
    return (jnp.concatenate([t[..., :ROPE_LANE], t[..., a:a + QK_NOPE - ROPE_LANE]], axis=-1),
            jnp.concatenate([t[..., ROPE_LANE:a], t[..., b:b + ROPE_HALF]], axis=-1))


def _rope_partner(x):
    lane = lax.broadcasted_iota(jnp.int32, x.shape, 1) & (LANES // 2 - 1)
    return jnp.where((lane >= ROPE_LANE) & (lane < ROPE_LANE + ROPE_HALF), pltpu.roll(x, LANES // 2, 1), 0.0)


def _rope(x, cos_t, sin_t):
    return x * cos_t + _rope_partner(x) * sin_t


def _rope_t(dy, cos_t, sin_t):
    return dy * cos_t + _rope_partner(dy * sin_t)


def _rowwise(body, name, s, tm, rows=(), halos=(), fulls=(), outs=(), accs=()):
    n = s // tm
    in_specs, args = [], []
    for a in rows:
        in_specs.append(pl.BlockSpec((tm, a.shape[1]), lambda i: (i, 0)))
        args.append(a)
    for a in halos:
        hr = 2 * SUBLANES if a.dtype == BF16 else SUBLANES
        in_specs.append(pl.BlockSpec((hr, a.shape[1]), lambda i, hr=hr: (jnp.maximum(i * (tm // hr) - 1, 0), 0)))
        in_specs.append(pl.BlockSpec((hr, a.shape[1]), lambda i, hr=hr: (jnp.minimum((i + 1) * (tm // hr), s // hr - 1), 0)))
        args += [a, a]
    for a in fulls:
        in_specs.append(pl.BlockSpec(a.shape, lambda i, nd=a.ndim: (0,) * nd))
        args.append(a)
    out_shape, out_specs = [], []
    for c, dt in outs:
        out_shape.append(jax.ShapeDtypeStruct((s, c), dt))
        out_specs.append(pl.BlockSpec((tm, c), lambda i: (i, 0)))
    for shp, dt in accs:
        out_shape.append(jax.ShapeDtypeStruct(shp, dt))
        out_specs.append(pl.BlockSpec(shp, lambda i, nd=len(shp): (0,) * nd))

    def kern(*refs):
        body(pl.program_id(0), n, *refs)

    return pl.pallas_call(kern, grid=(n,), in_specs=in_specs, out_specs=out_specs, out_shape=out_shape, name=name,
                          compiler_params=_cparams(1))(*args)


def _matmul_tn(a, b, name, col_shards=False, out_dtype=F32):
    t, m = a.shape
    n = b.shape[1]
    bm = m
    for cand in range(LANES, m + 1, LANES):
        if m % cand == 0 and cand * (n // N_CHIPS if col_shards else min(n, 2048)) * 4 <= 6 * 1024 * 1024:
            bm = cand
    bn = n // N_CHIPS if col_shards else (n if n <= 2048 else 1408)
    bt = min(t, 2048 if max(bm, bn) <= 512 else (1024 if max(bm, bn) <= 1024 else 512))
    nt = t // bt

    def kern(a_ref, b_ref, o_ref, acc_ref):
        k = pl.program_id(2)

        @pl.when(k == 0)
        def _():
            acc_ref[...] = jnp.zeros_like(acc_ref)
        acc_ref[...] += _bdot_tn(a_ref[...], b_ref[...])

        @pl.when(k == nt - 1)
        def _():
            o_ref[...] = acc_ref[...].astype(out_dtype)

    if col_shards:
        out_spec = pl.BlockSpec((None, bm, bn), lambda i, j, k: (j, i, 0))
        out_shape = jax.ShapeDtypeStruct((N_CHIPS, m, bn), out_dtype)
    else:
        out_spec = pl.BlockSpec((bm, bn), lambda i, j, k: (i, j))
        out_shape = jax.ShapeDtypeStruct((m, n), out_dtype)
    return pl.pallas_call(
        kern, grid=(m // bm, n // bn, nt),
        in_specs=[pl.BlockSpec((bt, bm), lambda i, j, k: (k, i)), pl.BlockSpec((bt, bn), lambda i, j, k: (k, j))],
        out_specs=out_spec, out_shape=out_shape, scratch_shapes=[pltpu.VMEM((bm, bn), F32)], name=name,
        compiler_params=_cparams(3))(a, b)


def _in_proj(x, g, w_in_p, tm):
    def body(i, n, x_ref, g_ref, w_ref, xr, yg, cq, ckv, krp, hb):
        xv = x_ref[...]
        h = (xv * _rstd(xv) * g_ref[...]).astype(BF16)
        hb[...] = h
        p = jnp.dot(h, w_ref[...], preferred_element_type=F32)
        xr[...] = p[:, :LRU_W]
        yg[...] = p[:, LRU_W:2 * LRU_W]
        cq[...] = p[:, 2 * LRU_W:2 * LRU_W + Q_LORA]
        ckv[...] = p[:, 2 * LRU_W + Q_LORA:OFF_KR]
        krp[...] = p[:, OFF_KR:IN_PAD]

    return _rowwise(body, "in_proj", x.shape[0], tm, rows=[x], fulls=[g, w_in_p],
                    outs=[(LRU_W, F32), (LRU_W, F32), (Q_LORA, F32), (KV_LORA, F32), (LANES, F32), (D_MODEL, BF16)])


def _softplus_neg(lam):
    e = jnp.exp(-jnp.abs(lam))
    return jnp.maximum(-lam, 0.0) + jnp.where(e < 1e-2, e * (1.0 - e * (0.5 - e * (1.0 / 3.0))), jnp.log(1.0 + e))


def _lru_gates(x, halo, cw_ref, pv_ref, wa_ref, wi_ref, rev):
    down = not rev
    xc = pv_ref[0:1, :] + jnp.zeros_like(x)
    for j in range(CONV_W):
        k = j if rev else CONV_W - 1 - j
        xc = xc + cw_ref[k:k + 1, :] * _shift(x, j, halo, down)
    r = _sigmoid(_bdot(xc, wa_ref[...]) + pv_ref[1:2, :])
    ig = _sigmoid(_bdot(xc, wi_ref[...]) + pv_ref[2:3, :])
    lam = pv_ref[3:4, :]
    sp = _softplus_neg(lam)
    log_a = (-LRU_C) * r * sp
    a = jnp.exp(log_a)
    z = 2.0 * log_a
    series = -(z * (1.0 + z * (0.5 + z * (1.0 / 6.0 + z * (1.0 / 24.0)))))
    om = jnp.where(z > -0.02, series, 1.0 - a * a)
    mult = jnp.sqrt(om)
    return xc, r, ig, sp, a, mult


def _lru_scan_fwd(xr, cw, pv, wa, wi, rev, t):
    s = xr.shape[0]
    n = s // t
    hb = t // SUBLANES
    last8 = s // SUBLANES - 1
    down = not rev

    def kern(x_ref, halo_ref, cw_ref, pv_ref, wa_ref, wi_ref, h_ref, xc_ref, r_ref, ig_ref, a_ref, mult_ref, carry_ref):
        i = pl.program_id(0)
        _zero_first(i, carry_ref)
        halo = jnp.where(i == 0, 0.0, halo_ref[...])
        xc, r, ig, sp, a, mult = _lru_gates(x_ref[...], halo, cw_ref, pv_ref, wa_ref, wi_ref, rev)
        xc_ref[...], r_ref[...], ig_ref[...], a_ref[...], mult_ref[...] = xc, r, ig, a, mult
        h_ref[...] = _scan(a, mult * ig * xc, carry_ref[...], down)
        carry_ref[...] = h_ref[pl.ds(t - 1 if down else 0, 1), :]

    if rev:
        blk = lambda i: (n - 1 - i, 0)
        hal = lambda i: (jnp.minimum((n - i) * hb, last8), 0)
    else:
        blk = lambda i: (i, 0)
        hal = lambda i: (jnp.maximum(i * hb - 1, 0), 0)
    full = lambda a: pl.BlockSpec(a.shape, lambda i: (0, 0))
    return pl.pallas_call(
        kern, grid=(n,),
        in_specs=[pl.BlockSpec((t, LRU_W), blk), pl.BlockSpec((SUBLANES, LRU_W), hal), full(cw), full(pv), full(wa), full(wi)],
        out_specs=[pl.BlockSpec((t, LRU_W), blk)] * 6, out_shape=[jax.ShapeDtypeStruct((s, LRU_W), F32)] * 6,
        scratch_shapes=[pltpu.VMEM((1, LRU_W), F32)], name="lru_scan_rev" if rev else "lru_scan_fwd",
        compiler_params=_cparams(1))(xr, xr, cw, pv, wa, wi)


def _lru_scan_bwd(xr, saved, h, dh, cw, pv, wa, wi, rev, t):
    s = xr.shape[0]
    n = s // t
    hb = t // SUBLANES
    last8 = s // SUBLANES - 1
    down = not rev

    def kern(x_ref, xc_ref, r_ref, ig_ref, a_ref, mult_ref, h_ref, hh_ref, dh_ref, cw_ref, pv_ref, wa_ref, wi_ref,
             dx_ref, gwa_ref, gwi_ref, gv_ref, p_ref, dxc_halo_ref, tmp_ref):
        i = pl.program_id(0)
        _zero_first(i, gwa_ref, gwi_ref, gv_ref, p_ref, dxc_halo_ref)
        at_start = i == n - 1
        x = x_ref[...]
        hhalo = jnp.where(at_start, 0.0, hh_ref[...])
        xc, r, ig, a, mult = xc_ref[...], r_ref[...], ig_ref[...], a_ref[...], mult_ref[...]
        lam = pv_ref[3:4, :]
        sp = _softplus_neg(lam)
        h_prev = _shift(h_ref[...], 1, hhalo, down)
        row = lax.broadcasted_iota(jnp.int32, x.shape, 0)
        edge = t - 1 if down else 0
        dh_mod = dh_ref[...] + jnp.where(row == edge, p_ref[...], 0.0)
        a_next = _shift(a, 1, jnp.zeros((SUBLANES, LRU_W), F32), not down)
        g = _scan(a_next, dh_mod, jnp.zeros((1, LRU_W), F32), not down)
        tmp_ref[...] = a * g
        p_ref[...] = tmp_ref[pl.ds(0 if down else t - 1, 1), :]
        da = g * h_prev
        d_ig = g * mult * xc
        d_xc = g * mult * ig
        d_om = g * ig * xc * (0.5 / jnp.maximum(mult, 1e-30))
        d_log_a = da * a - 2.0 * d_om * a * a
        d_r = d_log_a * ((-LRU_C) * sp)
        d_sp = jnp.sum(d_log_a * ((-LRU_C) * r), axis=0, keepdims=True)
        gv_ref[7:8, :] += d_sp * (-_sigmoid(-lam))
        d_ga = d_r * r * (1.0 - r)
        d_gi = d_ig * ig * (1.0 - ig)
        _acc_row(gv_ref, 5, d_ga)
        _acc_row(gv_ref, 6, d_gi)
        d_xc = d_xc + _bdot_nt(d_ga, wa_ref[...]) + _bdot_nt(d_gi, wi_ref[...])
        gwa_ref[...] += _bdot_tn(xc, d_ga)
        gwi_ref[...] += _bdot_tn(xc, d_gi)
        _acc_row(gv_ref, 4, d_xc)
        dx = jnp.zeros_like(x)
        dxc_halo = dxc_halo_ref[...]
        for j in range(CONV_W):
            k = j if rev else CONV_W - 1 - j
            d_shift = _shift(d_xc, j, dxc_halo, not down)
            _acc_row(gv_ref, k, d_shift * x)
            dx = dx + cw_ref[k:k + 1, :] * d_shift
        dx_ref[...] = dx.astype(BF16)
        dxc_halo_ref[...] = d_xc[:SUBLANES] if down else d_xc[t - SUBLANES:]

    if rev:
        blk = lambda i: (i, 0)
        hal = lambda i: (jnp.minimum((i + 1) * hb, last8), 0)
    else:
        blk = lambda i: (n - 1 - i, 0)
        hal = lambda i: (jnp.maximum((n - 1 - i) * hb - 1, 0), 0)
    full = lambda a: pl.BlockSpec(a.shape, lambda i: (0, 0))
    bs = pl.BlockSpec((t, LRU_W), blk)
    hs = pl.BlockSpec((SUBLANES, LRU_W), hal)
    return pl.pallas_call(
        kern, grid=(n,),
        in_specs=[bs] * 7 + [hs, bs, full(cw), full(pv), full(wa), full(wi)],
        out_specs=[bs, pl.BlockSpec((LRU_W, LRU_W), lambda i: (0, 0)), pl.BlockSpec((LRU_W, LRU_W), lambda i: (0, 0)),
                   pl.BlockSpec((SUBLANES, LRU_W), lambda i: (0, 0))],
        out_shape=[jax.ShapeDtypeStruct((s, LRU_W), BF16), jax.ShapeDtypeStruct((LRU_W, LRU_W), F32),
                   jax.ShapeDtypeStruct((LRU_W, LRU_W), F32), jax.ShapeDtypeStruct((SUBLANES, LRU_W), F32)],
        scratch_shapes=[pltpu.VMEM((1, LRU_W), F32), pltpu.VMEM((SUBLANES, LRU_W), F32), pltpu.VMEM((t, LRU_W), F32)],
        name="lru_bwd_rev" if rev else "lru_bwd_fwd", compiler_params=_cparams(1))(xr, *saved, h, h, dh, cw, pv, wa, wi)


def _mla_qkv(cq, ckv, krp, cos_t, sin_t, g_qa, g_kva, g_qn, g_kn, w_uq_p, w_uk_p, w_uv, tm):
    scale = QK_HEAD ** -0.5 * LOG2E

    def body(i, n, cq_ref, ckv_ref, kr_ref, c_ref, s_ref, gqa, gkva, gqn, gkn, wq, wk, wv, q_out, k_out, v_out):
        cosv, sinv = c_ref[...], s_ref[...]
        cqv = cq_ref[...]
        qr = _bdot(cqv * _rstd(cqv) * gqa[...], wq[...])
        ckvv = ckv_ref[...]
        c_kv = (ckvv * _rstd(ckvv) * gkva[...]).astype(BF16)
        kn = jnp.dot(c_kv, wk[...], preferred_element_type=F32)
        v_out[...] = jnp.dot(c_kv, wv[...], preferred_element_type=F32).astype(BF16)
        kr = kr_ref[...]
        kr_swapped = _rope_partner(kr * gkn[...]) * sinv
        for h in range(HEADS):
            sl = slice(h * LANES, (h + 1) * LANES)
            qh = qr[:, sl]
            qh = _rope(qh * _rstd(qh, QK_HEAD) * gqn[...], cosv, sinv) * scale
            q_out[:, sl] = qh.astype(BF16)
            kh = kn[:, sl] + kr
            rs = _rstd(kh, QK_HEAD)
            k_out[:, sl] = (kh * rs * gkn[...] * cosv + kr_swapped * rs).astype(BF16)

    return _rowwise(body, "mla_qkv", cq.shape[0], tm, rows=[cq, ckv, krp, cos_t, sin_t],
                    fulls=[g_qa, g_kva, g_qn, g_kn, w_uq_p, w_uk_p, w_uv],
                    outs=[(HEADS * LANES, BF16), (HEADS * LANES, BF16), (MLA_W, BF16)])


NT_DIMS = (((1,), (1,)), ((), ()))
TN_DIMS = (((0,), (0,)), ((), ()))


def _riding_exchange(copies_fn, first, last):
    @pl.when(first)
    def _():
        for cp in copies_fn():
            cp.start()

    def finish():
        @pl.when(last)
        def _():
            for cp in copies_fn():
                cp.wait()
    return finish


def _attn_fwd(q, k, v, tq, tk, shards=()):
    s = q.shape[0]
    nq, nk = s // tq, s // tk
    n = len(shards)

    def kern(*refs):
        q_ref, k_ref, v_ref = refs[:3]
        o_ref, lse_ref = refs[3 + n:5 + n]
        acc_ref = refs[5 + 2 * n]
        p_id, i_id = pl.program_id(0), pl.program_id(1)
        finish = _riding_exchange(lambda: _gather_copies(refs[3:3 + n], refs[5 + n:5 + 2 * n], *refs[6 + 2 * n:]),
                                  (p_id == 0) & (i_id == 0), (p_id == HEADS // 2 - 1) & (i_id == nq - 1)) if n else None
        qs = (q_ref[:, :LANES], q_ref[:, LANES:])
        acc_ref[...] = jnp.zeros_like(acc_ref)

        def step(j, carry):
            off = pl.multiple_of(j * tk, tk)
            vc = v_ref[pl.ds(off, tk), :]
            out = []
            for h in range(2):
                m, l = carry[2 * h:2 * h + 2]
                st = lax.dot_general(k_ref[pl.ds(off, tk), h * LANES:(h + 1) * LANES], qs[h], NT_DIMS,
                                     preferred_element_type=F32)
                mn = jnp.maximum(m, jnp.max(st, axis=0, keepdims=True))
                al = jnp.exp2(m - mn)
                pt = jnp.exp2(st - mn)
                l = al * l + jnp.sum(pt, axis=0, keepdims=True)
                acc_ref[h] = al * acc_ref[h] + lax.dot_general(vc, pt.astype(BF16), TN_DIMS, preferred_element_type=F32)
                out += [mn, l]
            return tuple(out)

        init = (jnp.full((1, tq), -1e30, F32), jnp.zeros((1, tq), F32)) * 2
        m0, l0, m1, l1 = lax.fori_loop(0, nk, step, init)
        row = lax.broadcasted_iota(jnp.int32, (LANES, tq), 0)
        o_ref[...] = jnp.where(row < V_DIM, acc_ref[0] / l0, acc_ref[1] / l1).T
        lse_ref[0, 0:1, :] = m0 + jnp.log2(l0)
        lse_ref[0, 1:2, :] = m1 + jnp.log2(l1)
        if n:
            finish()

    return pl.pallas_call(
        kern, grid=(HEADS // 2, nq),
        in_specs=[pl.BlockSpec((tq, 2 * LANES), lambda p, i: (i, p)), pl.BlockSpec((s, 2 * LANES), lambda p, i: (0, p)),
                  pl.BlockSpec((s, LANES), lambda p, i: (0, p))] + [ANY] * n,
        out_specs=[pl.BlockSpec((tq, LANES), lambda p, i: (i, p)), pl.BlockSpec((1, 2, tq), lambda p, i: (p, 0, i))]
        + [ANY] * n,
        out_shape=[jax.ShapeDtypeStruct((s, MLA_W), F32), jax.ShapeDtypeStruct((HEADS // 2, 2, s), F32)]
        + _gather_shapes(shards),
        scratch_shapes=[pltpu.VMEM((2, LANES, tq), F32)] + (_gather_sems(n) if n else []),
        name="attn_fwd", compiler_params=_cparams(2))(q, k, v, *shards)


def _attn_bwd(q, k, v, do, lse, delta, tq, tk, contributions=()):
    s = q.shape[0]
    nq, nk = s // tq, s // tk
    n = len(contributions)

    def kern(*refs):
        q_ref, do_ref, lse_ref, dl_ref, k_ref, v_ref = refs[:6]
        dq_ref, dk_ref, dv_ref = refs[6 + n:9 + n]
        acc_ref = refs[9 + 2 * n]
        p_id, i_id = pl.program_id(0), pl.program_id(1)
        finish = _riding_exchange(lambda: _to_owner_copies(refs[6:6 + n], refs[9 + n:9 + 2 * n], *refs[10 + 2 * n:]),
                                  (p_id == 0) & (i_id == 0), (p_id == HEADS // 2 - 1) & (i_id == nq - 1)) if n else None
        _zero_first(pl.program_id(1), dk_ref, dv_ref)
        acc_ref[...] = jnp.zeros_like(acc_ref)
        qs = (q_ref[:, :LANES], q_ref[:, LANES:])
        doc = do_ref[...]
        lane_q = lax.broadcasted_iota(jnp.int32, (tq, LANES), 1)
        zq = jnp.zeros_like(doc)
        dos = (jnp.where(lane_q < V_DIM, doc, zq), jnp.where(lane_q >= V_DIM, doc, zq))
        lses = (lse_ref[0, 0:1, :], lse_ref[0, 1:2, :])
        dls = (dl_ref[0, 0:1, :], dl_ref[0, 1:2, :])

        def step(j, carry):
            off = pl.multiple_of(j * tk, tk)
            vp = v_ref[pl.ds(off, tk), :]
            lane_k = lax.broadcasted_iota(jnp.int32, (tk, LANES), 1)
            zero = jnp.zeros_like(vp)
            vs = (jnp.where(lane_k < V_DIM, vp, zero), jnp.where(lane_k >= V_DIM, vp, zero))
            for h in range(2):
                sl = slice(h * LANES, (h + 1) * LANES)
                st = lax.dot_general(k_ref[pl.ds(off, tk), sl], qs[h], NT_DIMS, preferred_element_type=F32)
                pt = jnp.exp2(st - lses[h])
                dpt = lax.dot_general(vs[h], doc, NT_DIMS, preferred_element_type=F32)
                dst = (pt * (dpt - dls[h])).astype(BF16)
                dv_ref[pl.ds(off, tk), :] += jnp.dot(pt.astype(BF16), dos[h], preferred_element_type=F32)
                dk_ref[pl.ds(off, tk), sl] += jnp.dot(dst, qs[h], preferred_element_type=F32)
                acc_ref[h] += lax.dot_general(k_ref[pl.ds(off, tk), sl], dst, TN_DIMS, preferred_element_type=F32)
            return carry

        lax.fori_loop(0, nk, step, 0)
        dq_ref[:, :LANES] = acc_ref[0].T
        dq_ref[:, LANES:] = acc_ref[1].T
        if n:
            finish()

    return pl.pallas_call(
        kern, grid=(HEADS // 2, nq),
        in_specs=[pl.BlockSpec((tq, 2 * LANES), lambda p, i: (i, p)), pl.BlockSpec((tq, LANES), lambda p, i: (i, p)),
                  pl.BlockSpec((1, 2, tq), lambda p, i: (p, 0, i)), pl.BlockSpec((1, 2, tq), lambda p, i: (p, 0, i)),
                  pl.BlockSpec((s, 2 * LANES), lambda p, i: (0, p)), pl.BlockSpec((s, LANES), lambda p, i: (0, p))]
        + [ANY] * n,
        out_specs=[pl.BlockSpec((tq, 2 * LANES), lambda p, i: (i, p)), pl.BlockSpec((s, 2 * LANES), lambda p, i: (0, p)),
                   pl.BlockSpec((s, LANES), lambda p, i: (0, p))] + [ANY] * n,
        out_shape=[jax.ShapeDtypeStruct((s, HEADS * LANES), F32), jax.ShapeDtypeStruct((s, HEADS * LANES), F32),
                   jax.ShapeDtypeStruct((s, MLA_W), F32)] + _to_owner_shapes(contributions),
        scratch_shapes=[pltpu.VMEM((2, LANES, tq), F32)] + (_to_owner_sems(n) if n else []),
        name="attn_bwd", compiler_params=_cparams(2))(q, do, lse, delta, k, v, *contributions)


def _mix_out(hf, hb, yg, o, x, g_lru, g_mla, w_out, tm):
    def body(i, n, hf_ref, hb_ref, yg_ref, o_ref, x_ref, gl, gm, w_ref, x1_ref, mix_ref):
        lo = (hf_ref[...] + hb_ref[...]) * _gelu(yg_ref[...])
        ov = o_ref[...]
        mix_ref[:, :LRU_W] = (lo * _rstd(lo) * gl[...]).astype(BF16)
        mix_ref[:, LRU_W:] = (ov * _rstd(ov) * gm[...]).astype(BF16)
        x1_ref[...] = x_ref[...] + jnp.dot(mix_ref[...], w_ref[...], preferred_element_type=F32)

    return _rowwise(body, "mix_out", x.shape[0], tm, rows=[hf, hb, yg, o, x], fulls=[g_lru, g_mla, w_out],
                    outs=[(D_MODEL, F32), (2 * LRU_W, BF16)])


def _mem_kv(mem, g_mem, w_kv, g_k):
    m = mem.shape[0]

    def body(i, n, mem_ref, g_ref, w_ref, gk_ref, km_ref, vm_ref):
        mv = mem_ref[...]
        kv = _bdot(mv * _rstd(mv) * g_ref[...], w_ref[...])
        vm_ref[...] = kv[:, MEM_W:].astype(BF16)
        for h in range(MEM_HEADS):
            sl = slice(h * MEM_HD, (h + 1) * MEM_HD)
            kh = kv[:, sl]
            km_ref[:, sl] = (kh * _rstd(kh) * gk_ref[...]).astype(BF16)

    return _rowwise(body, "mem_kv", m, m, rows=[mem], fulls=[g_mem, w_kv, g_k], outs=[(MEM_W, BF16), (MEM_W, BF16)])


def _mem_attn_core(x1v, g_ref, wq_ref, gq_ref, km_ref, vm_ref):
    scale = MEM_HD ** -0.5
    hm = (x1v * _rstd(x1v) * g_ref[...]).astype(BF16)
    qr = jnp.dot(hm, wq_ref[...], preferred_element_type=F32)
    heads = []
    for h in range(MEM_HEADS):
        sl = slice(h * MEM_HD, (h + 1) * MEM_HD)
        qh = qr[:, sl]
        rs = _rstd(qh)
        qn = (qh * rs * gq_ref[...]).astype(BF16)
        sc = lax.dot_general(qn, km_ref[:, sl], (((1,), (1,)), ((), ())), preferred_element_type=F32) * scale
        e = jnp.exp(sc - jnp.max(sc, axis=-1, keepdims=True))
        p = e / jnp.sum(e, axis=-1, keepdims=True)
        oh = jnp.dot(p.astype(BF16), vm_ref[:, sl], preferred_element_type=F32)
        heads.append((qh, rs, qn, p, oh))
    return hm, heads


def _mem_attn(x1, g, w_q, g_q, km, vm, w_o, tm):
    cs = D_MODEL // N_CHIPS

    def body(i, n, x1_ref, g_ref, wq_ref, gq_ref, km_ref, vm_ref, wo_ref, x2_ref, ob_ref):
        x1v = x1_ref[...]
        _, heads = _mem_attn_core(x1v, g_ref, wq_ref, gq_ref, km_ref, vm_ref)
        for h in range(MEM_HEADS):
            ob_ref[:, h * MEM_HD:(h + 1) * MEM_HD] = heads[h][4].astype(BF16)
        for k in range(N_CHIPS):
            sl = slice(k * cs, (k + 1) * cs)
            x2_ref[:, sl] = x1v[:, sl] + jnp.dot(ob_ref[...], wo_ref[k], preferred_element_type=F32)

    return _rowwise(body, "mem_attn", x1.shape[0], tm, rows=[x1], fulls=[g, w_q, g_q, km, vm, w_o],
                    outs=[(D_MODEL, F32), (MEM_W, BF16)])


def _ffn_up(x2, g, w_up, tm):
    cs = 2 * D_FF // N_CHIPS

    def body(i, n, x_ref, g_ref, w_ref, gu_ref, hb_ref):
        xv = x_ref[...]
        hb_ref[...] = (xv * _rstd(xv) * g_ref[...]).astype(BF16)
        for k in range(N_CHIPS):
            gu_ref[:, k * cs:(k + 1) * cs] = jnp.dot(hb_ref[...], w_ref[k], preferred_element_type=F32)

    return _rowwise(body, "ffn_up", x2.shape[0], tm, rows=[x2], fulls=[g, w_up], outs=[(2 * D_FF, F32), (D_MODEL, BF16)])


def _ffn_conv(gu, prev, nxt, cw_ref, i, n):
    prev = jnp.where(i == 0, 0.0, prev)
    nxt = jnp.where(i == n - 1, 0.0, nxt)
    return (cw_ref[3:4, :] + cw_ref[0:1, :] * _shift_down(gu, 1, prev) + cw_ref[1:2, :] * gu
            + cw_ref[2:3, :] * _shift_up(gu, 1, nxt))


def _ffn_down_loss(gu_pre, x2, target, cw, w_down, tm):
    def body(i, n, gu_ref, x_ref, t_ref, pv_ref, nx_ref, cw_ref, w_ref, dy_ref, dyb_ref, act_ref, dgu_ref, loss_ref):
        _zero_first(i, loss_ref)
        gu = _ffn_conv(gu_ref[...], pv_ref[...], nx_ref[...], cw_ref, i, n)
        g, u = gu[:, :D_FF], gu[:, D_FF:]
        sg = _sigmoid(g)
        a = g * sg
        act_ref[...] = (a * u).astype(BF16)
        y = x_ref[...] + jnp.dot(act_ref[...], w_ref[...], preferred_element_type=F32)
        e = y - t_ref[...]
        loss_ref[...] += jnp.sum(e * e)
        dy = e * (1.0 / D_MODEL)
        dy_ref[...] = dy
        dyb_ref[...] = dy.astype(BF16)
        d_act = lax.dot_general(dyb_ref[...], w_ref[...], NT_DIMS, preferred_element_type=F32)
        dgu_ref[:, :D_FF] = ((d_act * u) * (sg + a - a * sg)).astype(BF16)
        dgu_ref[:, D_FF:] = (d_act * a).astype(BF16)

    return _rowwise(body, "ffn_down_loss", x2.shape[0], tm, rows=[gu_pre, x2, target], halos=[gu_pre], fulls=[cw, w_down],
                    outs=[(D_MODEL, F32), (D_MODEL, BF16), (D_FF, BF16), (2 * D_FF, BF16)], accs=[((SUBLANES, LANES), F32)])


def _ffn_bwd_conv(dgu, gu_pre, cw, tm):
    def body(i, n, d_ref, g_ref, dp_ref, dn_ref, cw_ref, dpre_ref, gc_ref):
        _zero_first(i, gc_ref)
        d = d_ref[...].astype(F32)
        g = g_ref[...]
        d_next = _shift_up(d, 1, jnp.where(i == n - 1, 0.0, dn_ref[...].astype(F32)[:SUBLANES]))
        d_prev = _shift_down(d, 1, jnp.where(i == 0, 0.0, dp_ref[...].astype(F32)[SUBLANES:]))
        dpre_ref[...] = (cw_ref[0:1, :] * d_next + cw_ref[1:2, :] * d + cw_ref[2:3, :] * d_prev).astype(BF16)
        _acc_row(gc_ref, 0, d_next * g)
        _acc_row(gc_ref, 1, d * g)
        _acc_row(gc_ref, 2, d_prev * g)
        _acc_row(gc_ref, 3, d)

    return _rowwise(body, "ffn_bwd_conv", dgu.shape[0], tm, rows=[dgu, gu_pre], halos=[dgu], fulls=[cw],
                    outs=[(2 * D_FF, BF16)], accs=[((SUBLANES, 2 * D_FF), F32)])


def _ffn_bwd_in(dpre, x2, dy, g, w_up, tm):
    cs = 2 * D_FF // N_CHIPS

    def body(i, n, dp_ref, x_ref, dy_ref, g_ref, w_ref, dx_ref, dxb_ref, gg_ref):
        _zero_first(i, gg_ref)
        d_h = jnp.zeros(x_ref.shape, F32)
        for k in range(N_CHIPS):
            d_h = d_h + lax.dot_general(dp_ref[:, k * cs:(k + 1) * cs], w_ref[k], (((1,), (1,)), ((), ())),
                                        preferred_element_type=F32)
        xv = x_ref[...]
        dx, dg = _norm_bwd(xv, _rstd(xv), g_ref[...], d_h)
        _acc_row(gg_ref, 0, dg)
        dx = dx + dy_ref[...]
        dx_ref[...] = dx
        dxb_ref[...] = dx.astype(BF16)

    return _rowwise(body, "ffn_bwd_in", x2.shape[0], tm, rows=[dpre, x2, dy], fulls=[g, w_up],
                    outs=[(D_MODEL, F32), (D_MODEL, BF16)], accs=[((SUBLANES, D_MODEL), F32)])


def _mem_attn_bwd(x1, dx2, dx2b, g, w_q, g_q, km, vm, w_o, tm):
    scale = MEM_HD ** -0.5
    m = km.shape[0]

    def body(i, n, x1_ref, dx2_ref, dx2b_ref, g_ref, wq_ref, gq_ref, km_ref, vm_ref, wo_ref,
             dx1_ref, dx1b_ref, hm_ref, dqr_ref, dkm_ref, dvm_ref, gg_ref, ggq_ref):
        _zero_first(i, dkm_ref, dvm_ref, gg_ref, ggq_ref)
        x1v = x1_ref[...]
        hm, heads = _mem_attn_core(x1v, g_ref, wq_ref, gq_ref, km_ref, vm_ref)
        hm_ref[...] = hm
        cs = D_MODEL // N_CHIPS
        d_o = jnp.zeros((x1v.shape[0], MEM_W), F32)
        for k in range(N_CHIPS):
            d_o = d_o + lax.dot_general(dx2b_ref[:, k * cs:(k + 1) * cs], wo_ref[k], (((1,), (1,)), ((), ())),
                                        preferred_element_type=F32)
        for h in range(MEM_HEADS):
            sl = slice(h * MEM_HD, (h + 1) * MEM_HD)
            qh, rs, qn, p, _ = heads[h]
            d_oh = d_o[:, sl].astype(BF16)
            dp = lax.dot_general(d_oh, vm_ref[:, sl], (((1,), (1,)), ((), ())), preferred_element_type=F32)
            ds = (p * (dp - jnp.sum(dp * p, axis=-1, keepdims=True)) * scale).astype(BF16)
            dqn = jnp.dot(ds, km_ref[:, sl], preferred_element_type=F32)
            dkm_ref[:, sl] += lax.dot_general(ds, qn, (((0,), (0,)), ((), ())), preferred_element_type=F32)
            dvm_ref[:, sl] += lax.dot_general(p.astype(BF16), d_oh, (((0,), (0,)), ((), ())), preferred_element_type=F32)
            dqh, dgq = _norm_bwd(qh, rs, gq_ref[...], dqn)
            _acc_row(ggq_ref, 0, dgq)
            dqr_ref[:, sl] = dqh.astype(BF16)
        d_hm = lax.dot_general(dqr_ref[...], wq_ref[...], (((1,), (1,)), ((), ())), preferred_element_type=F32)
        dx, dg = _norm_bwd(x1v, _rstd(x1v), g_ref[...], d_hm)
        _acc_row(gg_ref, 0, dg)
        dx = dx + dx2_ref[...]
        dx1_ref[...] = dx
        dx1b_ref[...] = dx.astype(BF16)

    return _rowwise(body, "mem_attn_bwd", x1.shape[0], tm, rows=[x1, dx2, dx2b], fulls=[g, w_q, g_q, km, vm, w_o],
                    outs=[(D_MODEL, F32), (D_MODEL, BF16), (D_MODEL, BF16), (MEM_W, BF16)],
                    accs=[((m, MEM_W), F32), ((m, MEM_W), F32), ((SUBLANES, D_MODEL), F32), ((SUBLANES, MEM_HD), F32)])


def _mem_kv_bwd(mem, g_mem, w_kv, g_k, dkm, dvm):
    m = mem.shape[0]

    def body(i, n, mem_ref, dkm_ref, dvm_ref, g_ref, w_ref, gk_ref, gw_ref, gg_ref, ggk_ref, dkv_ref):
        gg_ref[...] = jnp.zeros_like(gg_ref)
        ggk_ref[...] = jnp.zeros_like(ggk_ref)
        mv = mem_ref[...]
        rs_m = _rstd(mv)
        mem_n = (mv * rs_m * g_ref[...]).astype(BF16)
        kv = jnp.dot(mem_n, w_ref[...], preferred_element_type=F32)
        for h in range(MEM_HEADS):
            sl = slice(h * MEM_HD, (h + 1) * MEM_HD)
            kh = kv[:, sl]
            dkh, dgk = _norm_bwd(kh, _rstd(kh), gk_ref[...], dkm_ref[:, sl])
            _acc_row(ggk_ref, 0, dgk)
            dkv_ref[:, sl] = dkh.astype(BF16)
        dkv_ref[:, MEM_W:] = dvm_ref[...].astype(BF16)
        gw_ref[...] = lax.dot_general(mem_n, dkv_ref[...], (((0,), (0,)), ((), ())), preferred_element_type=F32)
        d_mn = lax.dot_general(dkv_ref[...], w_ref[...], (((1,), (1,)), ((), ())), preferred_element_type=F32)
        _acc_row(gg_ref, 0, d_mn * (mv * rs_m))

    return _rowwise(body, "mem_kv_bwd", m, m, rows=[mem, dkm, dvm], fulls=[g_mem, w_kv, g_k],
                    accs=[((D_MODEL, 2 * MEM_W), F32), ((SUBLANES, D_MODEL), F32), ((SUBLANES, MEM_HD), F32),
                          ((m, 2 * MEM_W), BF16)])


def _mix_out_bwd(dx1b, hf, hb, yg, o, g_lru, g_mla, w_out, tm):
    def body(i, n, dx_ref, hf_ref, hb_ref, yg_ref, o_ref, gl, gm, w_ref, dh_ref, dyg_ref, dob_ref, dl_ref, ggl_ref, ggm_ref):
        _zero_first(i, ggl_ref, ggm_ref)
        dmix = lax.dot_general(dx_ref[...], w_ref[...], (((1,), (1,)), ((), ())), preferred_element_type=F32)
        hs = hf_ref[...] + hb_ref[...]
        ygv = yg_ref[...]
        ge = _gelu(ygv)
        lo = hs * ge
        d_lo, dgl = _norm_bwd(lo, _rstd(lo), gl[...], dmix[:, :LRU_W])
        _acc_row(ggl_ref, 0, dgl)
        dh_ref[...] = d_lo * ge
        dyg_ref[...] = (d_lo * hs * _gelu_grad(ygv)).astype(BF16)
        ov = o_ref[...]
        d_o, dgm = _norm_bwd(ov, _rstd(ov), gm[...], dmix[:, LRU_W:])
        _acc_row(ggm_ref, 0, dgm)
        dob_ref[...] = d_o.astype(BF16)
        prod = d_o * ov
        lane_w = lax.broadcasted_iota(jnp.int32, prod.shape, 1)
        lane = lax.broadcasted_iota(jnp.int32, (prod.shape[0], LANES), 1)
        dl = jnp.zeros((prod.shape[0], LANES), F32)
        for h in range(HEADS):
            in_head = (lane_w >= h * V_DIM) & (lane_w < (h + 1) * V_DIM)
            dl = dl + jnp.where(lane == h, jnp.sum(jnp.where(in_head, prod, 0.0), axis=-1, keepdims=True), 0.0)
        dl_ref[...] = dl

    return _rowwise(body, "mix_out_bwd", dx1b.shape[0], tm, rows=[dx1b, hf, hb, yg, o], fulls=[g_lru, g_mla, w_out],
                    outs=[(LRU_W, F32), (LRU_W, BF16), (MLA_W, BF16), (LANES, F32)],
                    accs=[((SUBLANES, LRU_W), F32), ((SUBLANES, MLA_W), F32)])


def _mla_qkv_bwd(cq, ckv, krp, cos_t, sin_t, dq, dk, dv, g_qa, g_kva, g_qn, g_kn, w_uq_p, w_uk_p, w_uv, tm):
    scale = QK_HEAD ** -0.5

    def body(i, n, cq_ref, ckv_ref, kr_ref, c_ref, s_ref, dq_ref, dk_ref, dv_ref, gqa, gkva, gqn, gkn, wq, wk, wv,
             dcq_ref, dckv_ref, dkr_ref, cqb_ref, dqr_ref, ckvb_ref, dkn_ref, dvb_ref, ggqa, ggkva, ggqn, ggkn):
        _zero_first(i, ggqa, ggkva, ggqn, ggkn)
        cosv, sinv = c_ref[...], s_ref[...]
        cqv = cq_ref[...]
        rs_q = _rstd(cqv)
        cqb_ref[...] = (cqv * rs_q * gqa[...]).astype(BF16)
        qr = jnp.dot(cqb_ref[...], wq[...], preferred_element_type=F32)
        ckvv = ckv_ref[...]
        rs_kv = _rstd(ckvv)
        ckvb_ref[...] = (ckvv * rs_kv * gkva[...]).astype(BF16)
        kn = jnp.dot(ckvb_ref[...], wk[...], preferred_element_type=F32)
        kr = kr_ref[...]
        dkr = jnp.zeros_like(kr)
        for h in range(HEADS):
            sl = slice(h * LANES, (h + 1) * LANES)
            qh = qr[:, sl]
            d_qn = _rope_t(dq_ref[:, sl] * scale, cosv, sinv)
            dqh, dgq = _norm_bwd(qh, _rstd(qh, QK_HEAD), gqn[...], d_qn, QK_HEAD)
            _acc_row(ggqn, 0, dgq)
            dqr_ref[:, sl] = dqh.astype(BF16)
            kh = kn[:, sl] + kr
            d_kn = _rope_t(dk_ref[:, sl] * (1.0 / LOG2E), cosv, sinv)
            dkh, dgk = _norm_bwd(kh, _rstd(kh, QK_HEAD), gkn[...], d_kn, QK_HEAD)
            _acc_row(ggkn, 0, dgk)
            dkn_ref[:, sl] = dkh.astype(BF16)
            dkr = dkr + dkh
        dkr_ref[...] = dkr.astype(BF16)
        dvb_ref[...] = dv_ref[...].astype(BF16)
        d_cq = lax.dot_general(dqr_ref[...], wq[...], (((1,), (1,)), ((), ())), preferred_element_type=F32)
        dcq, dg = _norm_bwd(cqv, rs_q, gqa[...], d_cq)
        _acc_row(ggqa, 0, dg)
        dcq_ref[...] = dcq.astype(BF16)
        d_ckv = (lax.dot_general(dkn_ref[...], wk[...], (((1,), (1,)), ((), ())), preferred_element_type=F32)
                 + lax.dot_general(dvb_ref[...], wv[...], (((1,), (1,)), ((), ())), preferred_element_type=F32))
        dckv, dg = _norm_bwd(ckvv, rs_kv, gkva[...], d_ckv)
        _acc_row(ggkva, 0, dg)
        dckv_ref[...] = dckv.astype(BF16)

    return _rowwise(body, "mla_qkv_bwd", cq.shape[0], tm, rows=[cq, ckv, krp, cos_t, sin_t, dq, dk, dv],
                    fulls=[g_qa, g_kva, g_qn, g_kn, w_uq_p, w_uk_p, w_uv],
                    outs=[(Q_LORA, BF16), (KV_LORA, BF16), (LANES, BF16), (Q_LORA, BF16), (HEADS * LANES, BF16),
                          (KV_LORA, BF16), (HEADS * LANES, BF16), (MLA_W, BF16)],
                    accs=[((SUBLANES, Q_LORA), F32), ((SUBLANES, KV_LORA), F32), ((SUBLANES, LANES), F32),
                          ((SUBLANES, LANES), F32)])


def _in_proj_bwd(x, dx1, dxr_f, dxr_b, dyg, dcq, dckv, dkrp, g, w_in_p, tm):
    def body(i, n, x_ref, dx1_ref, df_ref, db_ref, dyg_ref, dcq_ref, dckv_ref, dkr_ref, g_ref, w_ref, gx_ref, dp_ref, gg_ref):
        _zero_first(i, gg_ref)
        dp_ref[:, :LRU_W] = (df_ref[...].astype(F32) + db_ref[...].astype(F32)).astype(BF16)
        dp_ref[:, LRU_W:2 * LRU_W] = dyg_ref[...].astype(BF16)
        dp_ref[:, 2 * LRU_W:2 * LRU_W + Q_LORA] = dcq_ref[...].astype(BF16)
        dp_ref[:, 2 * LRU_W + Q_LORA:OFF_KR] = dckv_ref[...].astype(BF16)
        dp_ref[:, OFF_KR:] = dkr_ref[...].astype(BF16)
        d_h = lax.dot_general(dp_ref[...], w_ref[...], (((1,), (1,)), ((), ())), preferred_element_type=F32)
        xv = x_ref[...]
        dx, dg = _norm_bwd(xv, _rstd(xv), g_ref[...], d_h)
        _acc_row(gg_ref, 0, dg)
        gx_ref[...] = dx + dx1_ref[...]

    return _rowwise(body, "in_proj_bwd", x.shape[0], tm, rows=[x, dx1, dxr_f, dxr_b, dyg, dcq, dckv, dkrp],
                    fulls=[g, w_in_p], outs=[(D_MODEL, F32), (IN_PAD, BF16)], accs=[((SUBLANES, D_MODEL), F32)])


ANY = pl.BlockSpec(memory_space=pl.ANY)


def _chip_peers(x, y):
    return ((1 - x, y), (x, 1 - y), (1 - x, 1 - y))


def _exchange_call(kern, name, ins, out_shapes, n_sems, aliases=None):
    return pl.pallas_call(
        kern, in_specs=[ANY] * len(ins), out_specs=[ANY] * len(out_shapes), out_shape=out_shapes,
        scratch_shapes=[pltpu.SemaphoreType.DMA((n,)) for n in n_sems], input_output_aliases=aliases or {},
        name=name)(*ins)


def _start_then_wait(copies):
    for cp in copies:
        cp.start()
    for cp in copies:
        cp.wait()


N_DEV = 8
RELATIONS = tuple((dx, dy, dc) for dx in (0, 1) for dy in (0, 1) for dc in (0, 1))[1:]


def _flip(v, d):
    return 1 - v if d else v


def _gather_copies(ins, outs, ssem, rsem, lsem):
    x, y, c = lax.axis_index("x"), lax.axis_index("y"), lax.axis_index("c")
    me = 2 * x + y
    cps = []
    for i, (a, o) in enumerate(zip(ins, outs)):
        cps.append(pltpu.make_async_copy(a, o.at[me], lsem.at[i]))
        for j, (px, py) in enumerate(_chip_peers(x, y)):
            cps.append(pltpu.make_async_remote_copy(a, o.at[me], ssem.at[3 * i + j], rsem.at[3 * i + j],
                                                    device_id=(px, py, c), device_id_type=MESH))
    return cps


def _gather_shapes(arrs):
    return [jax.ShapeDtypeStruct((N_CHIPS,) + a.shape, a.dtype) for a in arrs]


def _gather_sems(n):
    return [pltpu.SemaphoreType.DMA((3 * n,)), pltpu.SemaphoreType.DMA((3 * n,)), pltpu.SemaphoreType.DMA((n,))]


def _gather_chips(arrs):
    n = len(arrs)

    def kern(*refs):
        _start_then_wait(_gather_copies(refs[:n], refs[n:2 * n], *refs[2 * n:]))

    return _exchange_call(kern, "gather_weights", arrs, _gather_shapes(arrs), (3 * n, 3 * n, n))


def _to_owner_copies(ins, outs, ssem, rsem, lsem):
    x, y, c = lax.axis_index("x"), lax.axis_index("y"), lax.axis_index("c")
    me = 4 * x + 2 * y + c
    cps = []
    for i, (a, o) in enumerate(zip(ins, outs)):
        cps.append(pltpu.make_async_copy(a.at[2 * x + y, c], o.at[me], lsem.at[i]))
        for r, (dx, dy, dc) in enumerate(RELATIONS):
            tx, ty, tc = _flip(x, dx), _flip(y, dy), _flip(c, dc)
            cps.append(pltpu.make_async_remote_copy(a.at[2 * tx + ty, tc], o.at[me], ssem.at[7 * i + r], rsem.at[7 * i + r],
                                                    device_id=(tx, ty, tc), device_id_type=MESH))
    return cps


def _to_owner_shapes(arrs):
    return [jax.ShapeDtypeStruct((N_DEV,) + a.shape[2:], a.dtype) for a in arrs]


def _to_owner_sems(n):
    return [pltpu.SemaphoreType.DMA((7 * n,)), pltpu.SemaphoreType.DMA((7 * n,)), pltpu.SemaphoreType.DMA((n,))]


def _to_owner(arrs, name):
    n = len(arrs)

    def kern(*refs):
        _start_then_wait(_to_owner_copies(refs[:n], refs[n:2 * n], *refs[2 * n:]))

    return _exchange_call(kern, name, arrs, _to_owner_shapes(arrs), (7 * n, 7 * n, n))


def _join_halves(arrs):
    n = len(arrs)

    def kern(*refs):
        outs, (ssem, rsem) = refs[n:2 * n], refs[2 * n:]
        x, y, c = lax.axis_index("x"), lax.axis_index("y"), lax.axis_index("c")
        _start_then_wait([
            pltpu.make_async_remote_copy(outs[i].at[c], outs[i].at[c], ssem.at[i], rsem.at[i],
                                         device_id=(x, y, 1 - c), device_id_type=MESH) for i in range(n)])

    outs = [jax.ShapeDtypeStruct(a.shape, a.dtype) for a in arrs]
    return _exchange_call(kern, "grad_join_halves", arrs, outs, (n, n), aliases={i: i for i in range(n)})


def _row_block(rows, row_bytes, limit=1 << 20):
    best = None
    for d in range(16, rows + 1, 16):
        if rows % d == 0 and d * row_bytes <= limit:
            best = d
    return best if best is not None else rows


def _sum_devices(b, c, name):
    _, h, cols = b.shape
    hb = _row_block(h, cols * 4)

    def kern(c_ref, b_ref, o_ref):
        acc = b_ref[0].astype(F32)
        for j in range(1, N_DEV):
            acc = acc + b_ref[j].astype(F32)
        o_ref[...] = acc

    return pl.pallas_call(
        kern,
        grid_spec=pltpu.PrefetchScalarGridSpec(
            num_scalar_prefetch=1, grid=(h // hb,),
            in_specs=[pl.BlockSpec((N_DEV, hb, cols), lambda i, c_ref: (0, i, 0))],
            out_specs=pl.BlockSpec((None, hb, cols), lambda i, c_ref: (c_ref[0], i, 0))),
        out_shape=jax.ShapeDtypeStruct((2, h, cols), F32), name=name, compiler_params=_cparams(1))(c, b)


def _adamw(w, g, m, v, name):
    rows, cols = w.shape
    rb = _row_block(rows, cols * 4)
    c1 = 1.0 - ADAM_B1 ** ADAM_STEP
    c2 = 1.0 - ADAM_B2 ** ADAM_STEP

    def kern(w_ref, g_ref, m_ref, v_ref, d_ref, mo_ref, vo_ref):
        gv = g_ref[...]
        mn = ADAM_B1 * m_ref[...] + (1.0 - ADAM_B1) * gv
        vn = ADAM_B2 * v_ref[...] + (1.0 - ADAM_B2) * (gv * gv)
        mo_ref[...] = mn
        vo_ref[...] = vn
        d_ref[...] = (-ADAM_LR) * ((mn / c1) / (jnp.sqrt(vn / c2) + ADAM_EPS) + ADAM_WD * w_ref[...])

    spec = pl.BlockSpec((rb, cols), lambda i: (i, 0))
    return pl.pallas_call(
        kern, grid=(rows // rb,), in_specs=[spec] * 4, out_specs=[spec] * 3,
        out_shape=[jax.ShapeDtypeStruct(w.shape, F32)] * 3, name=name, compiler_params=_cparams(1))(w, g, m, v)


def _pad_rows(flat, rows):
    return jnp.pad(flat, (0, rows * LANES - flat.shape[0])).reshape(rows, LANES)


def _round_up(n, m):
    return (n + m - 1) // m * m


def _shard_shape(shape, axis):
    return tuple(s // N_CHIPS if a == axis else s for a, s in enumerate(shape))


def _to_shards(full, axis):
    shape = full.shape
    t = full.reshape(shape[:axis] + (N_CHIPS, shape[axis] // N_CHIPS) + shape[axis + 1:])
    return jnp.moveaxis(t, axis, 0).reshape(N_CHIPS, -1)


def _from_shards(sh, shape, axis):
    t = sh.reshape((N_CHIPS,) + _shard_shape(shape, axis))
    t = jnp.moveaxis(t, 0, axis)
    return t.reshape(shape)


BIG = tuple((name, shape, axis) for name, shape, axis, big in SHARDED if big)
EARLY_WEIGHTS = ("w_in", "w_uq", "w_ukv")
SMALL_SHARDED = tuple((name, shape, axis) for name, shape, axis, big in SHARDED if not big)


def _pack_small_weights(p):
    flat = jnp.concatenate([p[name].reshape(-1) for name, _, _ in SMALL_SHARDED])
    return _pad_rows(flat, _round_up(-(-flat.shape[0] // LANES), SUBLANES))


def _unpack_small_weights(gathered):
    flat = gathered.reshape(N_CHIPS, -1)
    out, off = {}, 0
    for name, shape, axis in SMALL_SHARDED:
        n = _numel(shape) // N_CHIPS
        out[name] = _from_shards(flat[:, off:off + n], shape, axis)
        off += n
    return out


def _pack_small_local(p, prefix=""):
    parts = [p[prefix + name].reshape(-1) for name, _, _ in SMALL_SHARDED]
    parts += [p[prefix + name].reshape(-1) for name, _ in REPLICATED]
    return jnp.concatenate(parts)


def _pack_small_grads(g, loss_part):
    parts = [_to_shards(g[name], axis) for name, _, axis in SMALL_SHARDED]
    rep = jnp.concatenate([g[name].reshape(-1) for name, _ in REPLICATED] + [loss_part.reshape(1)])
    parts.append(jnp.broadcast_to(rep[None], (N_CHIPS, rep.shape[0])))
    return jnp.concatenate(parts, axis=1)


def _unpack_small_local(flat):
    out, off = {}, 0
    for name, shape, axis in SMALL_SHARDED:
        n = _numel(shape) // N_CHIPS
        out[name] = flat[off:off + n].reshape((1,) + _shard_shape(shape, axis))
        off += n
    for name, shape in REPLICATED:
        n = _numel(shape)
        out[name] = flat[off:off + n].reshape((1,) + shape)
        off += n
    return out


def _grad_shards(g, shape, axis):
    if axis == 0:
        return g.reshape((N_CHIPS,) + _shard_shape(shape, axis))
    return jnp.transpose(g.reshape(shape[0], N_CHIPS, shape[1] // N_CHIPS), (1, 0, 2))


def _cols_from_shards(w4):
    return jnp.transpose(w4, (1, 0, 2)).reshape(w4.shape[1], -1)


def _block_diag(w):
    eye = jnp.eye(LRU_BLOCKS, dtype=w.dtype)
    return jnp.einsum("ncd,nm->ncmd", w, eye).reshape(LRU_W, LRU_W)


def _block_diag_t(g):
    g4 = g.reshape(LRU_BLOCKS, 64, LRU_BLOCKS, 64)
    eye = jnp.eye(LRU_BLOCKS, dtype=g.dtype)[:, None, :, None]
    return jnp.sum(g4 * eye, axis=2)


def _pad8(a):
    return jnp.pad(a, ((0, SUBLANES - a.shape[0]), (0, 0)))


def kernel(x, mem, positions, attn_norm, w_in, lru_conv_w, lru_conv_b, lru_w_a, lru_b_a, lru_w_i, lru_b_i, lru_lambda, q_a_norm, w_uq, kv_a_norm, w_ukv, mla_q_norm, mla_k_norm, lru_out_norm, mla_out_norm, w_out, mem_attn_norm, mem_norm, w_mem_q, w_mem_kv, mem_q_norm, mem_k_norm, w_mem_o, ffn_norm, w_up, ffn_conv_w, ffn_conv_b, w_down, loss_target, m_attn_norm, m_w_in, m_lru_conv_w, m_lru_conv_b, m_lru_w_a, m_lru_b_a, m_lru_w_i, m_lru_b_i, m_lru_lambda, m_q_a_norm, m_w_uq, m_kv_a_norm, m_w_ukv, m_mla_q_norm, m_mla_k_norm, m_lru_out_norm, m_mla_out_norm, m_w_out, m_mem_attn_norm, m_mem_norm, m_w_mem_q, m_w_mem_kv, m_mem_q_norm, m_mem_k_norm, m_w_mem_o, m_ffn_norm, m_w_up, m_ffn_conv_w, m_ffn_conv_b, m_w_down, v_attn_norm, v_w_in, v_lru_conv_w, v_lru_conv_b, v_lru_w_a, v_lru_b_a, v_lru_w_i, v_lru_b_i, v_lru_lambda, v_q_a_norm, v_w_uq, v_kv_a_norm, v_w_ukv, v_mla_q_norm, v_mla_k_norm, v_lru_out_norm, v_mla_out_norm, v_w_out, v_mem_attn_norm, v_mem_norm, v_w_mem_q, v_w_mem_kv, v_mem_q_norm, v_mem_k_norm, v_w_mem_o, v_ffn_norm, v_w_up, v_ffn_conv_w, v_ffn_conv_b, v_w_down):
    given = dict(locals())
    local = {name: given[name][0] for name in WEIGHT_ORDER}
    s = x.shape[1]
    x2d, mem2d, tgt = x[0], mem[0], loss_target[0]
    tm = min(512, s)
    tm_wide = min(1024, s)
    tm_ffn = min(256, s)
    t_scan = min(1024, s)
    tq_f, tq_b, tk = min(4096, s), min(2048, s), min(512, s)

    early = [b for b in BIG if b[0] in EARLY_WEIGHTS]
    late = [b for b in BIG if b[0] not in EARLY_WEIGHTS]
    got = _gather_chips([local[name].astype(BF16) for name, _, _ in early] + [_pack_small_weights(local)])
    full = _unpack_small_weights(got[-1])

    def take_gathered(entries, arrays):
        for (name, shape, axis), w4 in zip(entries, arrays):
            if axis == 0:
                full[name] = w4.reshape(shape)
            elif name in ("w_up", "w_mem_o"):
                full[name] = w4
            else:
                full[name] = _cols_from_shards(w4)

    take_gathered(early, got)
    row = lambda a: a.reshape(1, -1)
    b16 = lambda a: a.astype(BF16)
    zeros = lambda r, c: jnp.zeros((r, c), BF16)
    w_in_f = full["w_in"]
    w_in_p = jnp.concatenate([w_in_f[:, :OFF_KR], _head_tile(zeros(D_MODEL, QK_NOPE), w_in_f[:, OFF_KR:])], axis=1)
    uq = full["w_uq"].reshape(Q_LORA, HEADS, QK_HEAD)
    w_uq_p = _head_tile(uq[:, :, :QK_NOPE], uq[:, :, QK_NOPE:]).reshape(Q_LORA, -1)
    ukv = full["w_ukv"].reshape(KV_LORA, HEADS, QK_NOPE + V_DIM)
    w_uk_p = _head_tile(ukv[:, :, :QK_NOPE], None).reshape(KV_LORA, -1)
    w_uv = ukv[:, :, QK_NOPE:].reshape(KV_LORA, MLA_W)
    wa = [b16(_block_diag(local["lru_w_a"][d])) for d in range(2)]
    wi = [b16(_block_diag(local["lru_w_i"][d])) for d in range(2)]
    cw = [_pad8(full["lru_conv_w"][d]) for d in range(2)]
    pv = [_pad8(jnp.stack([full["lru_conv_b"][d], full["lru_b_a"][d], full["lru_b_i"][d], full["lru_lambda"][d]]))
          for d in range(2)]
    ffn_cw = _pad8(jnp.concatenate([full["ffn_conv_w"], row(local["ffn_conv_b"])], axis=0))
    g_attn, g_qa, g_kva = row(local["attn_norm"]), row(local["q_a_norm"]), row(local["kv_a_norm"])
    g_qn = _head_tile(row(local["mla_q_norm"])[:, :QK_NOPE], row(local["mla_q_norm"])[:, QK_NOPE:])
    g_kn = _head_tile(row(local["mla_k_norm"])[:, :QK_NOPE], row(local["mla_k_norm"])[:, QK_NOPE:])
    g_lru, g_mla = row(local["lru_out_norm"]), row(local["mla_out_norm"])
    g_memattn, g_mem = row(local["mem_attn_norm"]), row(local["mem_norm"])
    g_mq, g_mk, g_ffn = row(local["mem_q_norm"]), row(local["mem_k_norm"]), row(local["ffn_norm"])

    inv = ROPE_THETA ** (-jnp.arange(0, QK_ROPE, 2, dtype=F32) / QK_ROPE)
    no_nope = jnp.zeros((1, QK_NOPE), F32)
    inv_tile = _head_tile(no_nope, jnp.concatenate([inv, inv])[None])
    sign_tile = _head_tile(no_nope, jnp.concatenate([-jnp.ones_like(inv), jnp.ones_like(inv)])[None])
    ang = positions[0].astype(F32)[:, None] * inv_tile
    cos_t, sin_t = jnp.cos(ang), jnp.sin(ang) * sign_tile

    xr, yg, cq, ckv, krp, hb_in = _in_proj(x2d, g_attn, w_in_p, tm_wide)
    h_f, *saved_f = _lru_scan_fwd(xr, cw[0], pv[0], wa[0], wi[0], False, t_scan)
    h_b, *saved_b = _lru_scan_fwd(xr, cw[1], pv[1], wa[1], wi[1], True, t_scan)
    q, k, v = _mla_qkv(cq, ckv, krp, cos_t, sin_t, g_qa, g_kva, g_qn, g_kn, w_uq_p, w_uk_p, w_uv, tm_wide)
    o, lse, *got = _attn_fwd(q, k, v, tq_f, tk, shards=[local[name].astype(BF16) for name, _, _ in late])
    take_gathered(late, got)
    x1, mixed = _mix_out(h_f, h_b, yg, o, x2d, g_lru, g_mla, full["w_out"], tm_wide)
    km, vm = _mem_kv(mem2d, g_mem, full["w_mem_kv"], g_mk)
    x2, o_mem = _mem_attn(x1, g_memattn, full["w_mem_q"], g_mq, km, vm, full["w_mem_o"], tm_wide)
    gu_pre, hb_ffn = _ffn_up(x2, g_ffn, full["w_up"], tm)
    dy, dyb, act, dgu, loss_acc = _ffn_down_loss(gu_pre, x2, tgt, ffn_cw, full["w_down"], tm_ffn)

    grads = {}
    grads["w_down"] = _matmul_tn(act, dyb, "grad_w_down", out_dtype=BF16)
    dpre, g_conv = _ffn_bwd_conv(dgu, gu_pre, ffn_cw, tm_ffn)
    grads["ffn_conv_w"], grads["ffn_conv_b"] = g_conv[:3], g_conv[3]
    grads["w_up"] = _matmul_tn(hb_ffn, dpre, "grad_w_up", col_shards=True, out_dtype=BF16)
    dx2, dx2b, gg = _ffn_bwd_in(dpre, x2, dy, g_ffn, full["w_up"], tm)
    grads["ffn_norm"] = gg[0]
    grads["w_mem_o"] = _matmul_tn(o_mem, dx2b, "grad_w_mem_o", out_dtype=BF16)
    dx1, dx1b, hm, dqr_mem, dkm, dvm, gg, ggq = _mem_attn_bwd(x1, dx2, dx2b, g_memattn, full["w_mem_q"], g_mq, km, vm,
                                                                 full["w_mem_o"], tm)
    grads["mem_attn_norm"], grads["mem_q_norm"] = gg[0], ggq[0]
    grads["w_mem_q"] = _matmul_tn(hm, dqr_mem, "grad_w_mem_q", out_dtype=BF16)
    g_mem_kv, gg, ggk, _ = _mem_kv_bwd(mem2d, g_mem, full["w_mem_kv"], g_mk, dkm, dvm)
    grads["w_mem_kv"] = g_mem_kv.astype(BF16)
    grads["mem_norm"], grads["mem_k_norm"] = gg[0], ggk[0]
    grads["w_out"] = _matmul_tn(mixed, dx1b, "grad_w_out", out_dtype=BF16)
    dh, dyg, dob, dl128, ggl, ggm = _mix_out_bwd(dx1b, h_f, h_b, yg, o, g_lru, g_mla, full["w_out"], tm)
    grads["lru_out_norm"], grads["mla_out_norm"] = ggl[0], ggm[0]
    delta_t = jnp.transpose(dl128[:, :HEADS]).reshape(HEADS // 2, 2, s)
    def halves(name, shape, axis):
        g4 = grads[name] if grads[name].ndim == 3 else _grad_shards(grads[name], shape, axis)
        return g4.reshape(N_CHIPS, 2, g4.shape[1] // 2, g4.shape[2])

    dq, dk, dv, *arrived_late = _attn_bwd(q, k, v, dob, lse, delta_t, tq_b, tk,
                                          contributions=[halves(*e) for e in late])
    (dcq, dckv, dkrp, cqb, dqr, ckvb, dkn, dvb, ggqa, ggkva, ggqn, ggkn) = _mla_qkv_bwd(
        cq, ckv, krp, cos_t, sin_t, dq, dk, dv, g_qa, g_kva, g_qn, g_kn, w_uq_p, w_uk_p, w_uv, tm_wide)
    grads["q_a_norm"], grads["kv_a_norm"] = ggqa[0], ggkva[0]
    grads["mla_q_norm"] = jnp.concatenate(_from_head_tile(ggqn[0]))
    grads["mla_k_norm"] = jnp.concatenate(_from_head_tile(ggkn[0]))
    g_uq_p = _matmul_tn(cqb, dqr, "grad_w_uq")
    grads["w_uq"] = jnp.concatenate(_from_head_tile(g_uq_p.reshape(Q_LORA, HEADS, LANES)), axis=-1).reshape(Q_LORA, -1)
    g_uk_p = _from_head_tile(_matmul_tn(ckvb, dkn, "grad_w_uk").reshape(KV_LORA, HEADS, LANES))[0]
    g_uv = _matmul_tn(ckvb, dvb, "grad_w_uv").reshape(KV_LORA, HEADS, V_DIM)
    grads["w_ukv"] = jnp.concatenate([g_uk_p, g_uv], axis=2).reshape(KV_LORA, -1)
    dxr, gwa, gwi, gvec = [], [], [], []
    for d, (hd, saved) in enumerate(((h_f, saved_f), (h_b, saved_b))):
        r = _lru_scan_bwd(xr, saved, hd, dh, cw[d], pv[d], wa[d], wi[d], d == 1, t_scan)
        dxr.append(r[0])
        gwa.append(_block_diag_t(r[1]))
        gwi.append(_block_diag_t(r[2]))
        gvec.append(r[3])
    grads["lru_w_a"], grads["lru_w_i"] = jnp.stack(gwa), jnp.stack(gwi)
    grads["lru_conv_w"] = jnp.stack([gv[:CONV_W] for gv in gvec])
    for r_i, name in ((4, "lru_conv_b"), (5, "lru_b_a"), (6, "lru_b_i"), (7, "lru_lambda")):
        grads[name] = jnp.stack([gv[r_i] for gv in gvec])
    grad_x, dproj, gg = _in_proj_bwd(x2d, dx1, dxr[0], dxr[1], dyg, dcq, dckv, dkrp, g_attn, w_in_p, tm)
    grads["attn_norm"] = gg[0]
    g_in_p = _matmul_tn(hb_in, dproj, "grad_w_in")
    grads["w_in"] = jnp.concatenate([g_in_p[:, :OFF_KR], _from_head_tile(g_in_p[:, OFF_KR:])[1]], axis=1)

    small = _pack_small_grads(grads, loss_acc[0, 0] * (0.5 / D_MODEL))
    length = small.shape[1]
    hrows = _round_up(-(-length // (2 * LANES)), 16)
    small = jnp.pad(small, ((0, 0), (0, 2 * hrows * LANES - length))).reshape(N_CHIPS, 2, hrows, LANES)
    for name, _, _ in early:
        grads[name] = grads[name].astype(BF16)
    arrived_early = _to_owner([halves(*e) for e in early] + [small], "grad_to_owner")
    names = [name for name, _, _ in late + early] + ["small"]
    c_idx = lax.axis_index("c").astype(jnp.int32).reshape(1)
    reduced = _join_halves([_sum_devices(b, c_idx, "grad_sum_" + n)
                            for n, b in zip(names, list(arrived_late) + list(arrived_early))])

    outs = [{}, {}, {}, {}]
    for (name, shape, axis), r in zip(late + early, reduced):
        g2 = r.reshape(_shard_shape(shape, axis))
        res = _adamw(local[name], g2, given["m_" + name][0], given["v_" + name][0], "adamw_" + name)
        for o_, a in zip(outs, (g2, *res)):
            o_[name] = a[None]
    pack = lambda prefix: _pad_rows(_pack_small_local({n: given[prefix + n] for n in WEIGHT_ORDER}), 2 * hrows)
    g_small = reduced[-1].reshape(2 * hrows, LANES)
    res = _adamw(pack(""), g_small, pack("m_"), pack("v_"), "adamw_small")
    for o_, a in zip(outs, (g_small, *res)):
        o_.update(_unpack_small_local(a.reshape(-1)))
    loss = g_small.reshape(-1)[length - 1]
    return (loss, grad_x[None], *[o_[n] for o_ in outs for n in WEIGHT_ORDER])
```

```python
import jax
import jax.numpy as jnp
from jax import lax
from jax.experimental import pallas as pl
from jax.experimental.pallas import tpu as pltpu

F32, BF16 = jnp.float32, jnp.bfloat16
MESH = pl.DeviceIdType.MESH

D_MODEL = 1024
EPS = 1e-6
LRU_W = 512
LRU_BLOCKS = 8
LRU_C = 8.0
CONV_W = 4
HEADS = 8
QK_NOPE, QK_ROPE, QK_HEAD, V_DIM = 64, 32, 96, 64
Q_LORA, KV_LORA = 256, 128
MLA_W = HEADS * V_DIM
ROPE_THETA = 10000.0
IN_COLS = 2 * LRU_W + Q_LORA + KV_LORA + QK_ROPE
OFF_KR = IN_COLS - QK_ROPE
IN_PAD = 1536
MEM_HEADS, MEM_HD = 4, 128
MEM_W = MEM_HEADS * MEM_HD
D_FF = 2816
N_CHIPS = 4
ADAM_LR, ADAM_B1, ADAM_B2, ADAM_EPS, ADAM_WD, ADAM_STEP = 0.001, 0.9, 0.999, 1e-08, 0.01, 10

LANES = 128
SUBLANES = 8
V7X_VMEM_BYTES = 64 * 1024 * 1024
VMEM_LIMIT = V7X_VMEM_BYTES * 7 // 8

SHARDED = (
    ("w_in", (D_MODEL, IN_COLS), 1, True),
    ("lru_conv_w", (2, CONV_W, LRU_W), 2, False),
    ("lru_conv_b", (2, LRU_W), 1, False),
    ("lru_b_a", (2, LRU_W), 1, False),
    ("lru_b_i", (2, LRU_W), 1, False),
    ("lru_lambda", (2, LRU_W), 1, False),
    ("w_uq", (Q_LORA, HEADS * QK_HEAD), 1, True),
    ("w_ukv", (KV_LORA, HEADS * (QK_NOPE + V_DIM)), 1, True),
    ("w_out", (2 * LRU_W, D_MODEL), 0, True),
    ("w_mem_q", (D_MODEL, MEM_W), 0, True),
    ("w_mem_kv", (D_MODEL, 2 * MEM_W), 0, True),
    ("w_mem_o", (MEM_W, D_MODEL), 1, True),
    ("w_up", (D_MODEL, 2 * D_FF), 1, True),
    ("ffn_conv_w", (3, 2 * D_FF), 1, False),
    ("w_down", (D_FF, D_MODEL), 0, True),
)
REPLICATED = (
    ("attn_norm", (D_MODEL,)), ("lru_w_a", (2, LRU_BLOCKS, 64, 64)), ("lru_w_i", (2, LRU_BLOCKS, 64, 64)),
    ("q_a_norm", (Q_LORA,)), ("kv_a_norm", (KV_LORA,)), ("mla_q_norm", (QK_HEAD,)), ("mla_k_norm", (QK_HEAD,)),
    ("lru_out_norm", (LRU_W,)), ("mla_out_norm", (MLA_W,)), ("mem_attn_norm", (D_MODEL,)), ("mem_norm", (D_MODEL,)),
    ("mem_q_norm", (MEM_HD,)), ("mem_k_norm", (MEM_HD,)), ("ffn_norm", (D_MODEL,)), ("ffn_conv_b", (2 * D_FF,)),
)
WEIGHT_ORDER = ('attn_norm', 'w_in', 'lru_conv_w', 'lru_conv_b', 'lru_w_a', 'lru_b_a', 'lru_w_i', 'lru_b_i', 'lru_lambda',
                'q_a_norm', 'w_uq', 'kv_a_norm', 'w_ukv', 'mla_q_norm', 'mla_k_norm', 'lru_out_norm', 'mla_out_norm', 'w_out',
                'mem_attn_norm', 'mem_norm', 'w_mem_q', 'w_mem_kv', 'mem_q_norm', 'mem_k_norm', 'w_mem_o', 'ffn_norm', 'w_up',
                'ffn_conv_w', 'ffn_conv_b', 'w_down')


def _numel(shape):
    n = 1
    for s in shape:
        n *= s
    return n


def _cparams(n_axes):
    return pltpu.CompilerParams(dimension_semantics=("arbitrary",) * n_axes, vmem_limit_bytes=VMEM_LIMIT)


def _bdot(a, b):
    return jnp.dot(a.astype(BF16), b.astype(BF16), preferred_element_type=F32)


def _bdot_nt(a, b):
    return lax.dot_general(a.astype(BF16), b.astype(BF16), (((1,), (1,)), ((), ())), preferred_element_type=F32)


def _bdot_tn(a, b):
    return lax.dot_general(a.astype(BF16), b.astype(BF16), (((0,), (0,)), ((), ())), preferred_element_type=F32)


def _rstd(x, n=None):
    n = x.shape[-1] if n is None else n
    return lax.rsqrt(jnp.sum(x * x, axis=-1, keepdims=True) * (1.0 / n) + EPS)


def _norm_bwd(x, rs, g, dy, n=None):
    n = x.shape[-1] if n is None else n
    xhat = x * rs
    dxh = dy * g
    dx = rs * (dxh - xhat * (jnp.sum(dxh * xhat, axis=-1, keepdims=True) * (1.0 / n)))
    return dx, dy * xhat


def _acc_row(ref, r, val):
    ref[r:r + 1, :] += jnp.sum(val, axis=0, keepdims=True)


def _zero_first(i, *refs):
    @pl.when(i == 0)
    def _():
        for r in refs:
            r[...] = jnp.zeros_like(r)


def _shift_down(x, j, halo):
    if j == 0:
        return x
    xs = pltpu.roll(x, j, 0)
    hs = pltpu.roll(halo, j, 0)
    row = lax.broadcasted_iota(jnp.int32, hs.shape, 0)
    top = jnp.where(row < j, hs, xs[:SUBLANES])
    return jnp.concatenate([top, xs[SUBLANES:]], axis=0)


def _shift_up(x, j, halo):
    if j == 0:
        return x
    t = x.shape[0]
    xs = pltpu.roll(x, t - j, 0)
    hs = pltpu.roll(halo, SUBLANES - j, 0)
    row = lax.broadcasted_iota(jnp.int32, hs.shape, 0)
    bot = jnp.where(row >= SUBLANES - j, hs, xs[t - SUBLANES:])
    return jnp.concatenate([xs[:t - SUBLANES], bot], axis=0)


def _shift(x, j, halo, down):
    return _shift_down(x, j, halo) if down else _shift_up(x, j, halo)


def _scan(a, b, h_in, down):
    t, c = a.shape
    g = t // SUBLANES
    a3, b3 = a.reshape(g, SUBLANES, c), b.reshape(g, SUBLANES, c)
    sub = lax.broadcasted_iota(jnp.int32, a3.shape, 1)
    d = 1
    while d < SUBLANES:
        keep = (sub >= d) if down else (sub < SUBLANES - d)
        shift = d if down else SUBLANES - d
        a_s = jnp.where(keep, pltpu.roll(a3, shift, 1), 1.0)
        b_s = jnp.where(keep, pltpu.roll(b3, shift, 1), 0.0)
        b3 = a3 * b_s + b3
        a3 = a3 * a_s
        d *= 2
    hs = [None] * g
    carry = h_in
    for i in (range(g) if down else range(g - 1, -1, -1)):
        hs[i] = a3[i] * carry + b3[i]
        carry = hs[i][SUBLANES - 1:, :] if down else hs[i][:1, :]
    return jnp.concatenate(hs, axis=0)


def _sigmoid(x):
    return 0.5 * jnp.tanh(0.5 * x) + 0.5


LOG2E = 1.4426950408889634
GELU_K = 0.7978845608028654
GELU_C = 0.044715


def _gelu(x):
    return 0.5 * x * (1.0 + jnp.tanh(GELU_K * (x + GELU_C * x * x * x)))


def _gelu_grad(x):
    t = jnp.tanh(GELU_K * (x + GELU_C * x * x * x))
    return 0.5 * (1.0 + t) + 0.5 * x * (1.0 - t * t) * GELU_K * (1.0 + 3.0 * GELU_C * x * x)


ROPE_HALF = QK_ROPE // 2
ROPE_LANE = 32


def _head_tile(nope, rope):
    z = lambda n: jnp.zeros(nope.shape[:-1] + (n,), nope.dtype)
    r1, r2 = (z(ROPE_HALF), z(ROPE_HALF)) if rope is None else (rope[..., :ROPE_HALF], rope[..., ROPE_HALF:])
    return jnp.concatenate([nope[..., :ROPE_LANE], r1, nope[..., ROPE_LANE:], z(ROPE_HALF), r2, z(ROPE_HALF)], axis=-1)


def _from_head_tile(t):
    a, b = ROPE_LANE + ROPE_HALF, ROPE_LANE + LANES // 2
    return (jnp.concatenate([t[..., :ROPE_LANE], t[..., a:a + QK_NOPE - ROPE_LANE]], axis=-1),
            jnp.concatenate([t[..., ROPE_LANE:a], t[..., b:b + ROPE_HALF]], axis=-1))


def _rope_partner(x):
    lane = lax.broadcasted_iota(jnp.int32, x.shape, 1) & (LANES // 2 - 1)
    return jnp.where((lane >= ROPE_LANE) & (lane < ROPE_LANE + ROPE_HALF), pltpu.roll(x, LANES // 2, 1), 0.0)


def _rope(x, cos_t, sin_t):
    return x * cos_t + _rope_partner(x) * sin_t


def _rope_t(dy, cos_t, sin_t):
    return dy * cos_t + _rope_partner(dy * sin_t)


def _rowwise(body, name, s, tm, rows=(), halos=(), fulls=(), outs=(), accs=()):
    n = s // tm
    in_specs, args = [], []
    for a in rows:
        in_specs.append(pl.BlockSpec((tm, a.shape[1]), lambda i: (i, 0)))
        args.append(a)
    for a in halos:
        hr = 2 * SUBLANES if a.dtype == BF16 else SUBLANES
        in_specs.append(pl.BlockSpec((hr, a.shape[1]), lambda i, hr=hr: (jnp.maximum(i * (tm // hr) - 1, 0), 0)))
        in_specs.append(pl.BlockSpec((hr, a.shape[1]), lambda i, hr=hr: (jnp.minimum((i + 1) * (tm // hr), s // hr - 1), 0)))
        args += [a, a]
    for a in fulls:
        in_specs.append(pl.BlockSpec(a.shape, lambda i, nd=a.ndim: (0,) * nd))
        args.append(a)
    out_shape, out_specs = [], []
    for c, dt in outs:
        out_shape.append(jax.ShapeDtypeStruct((s, c), dt))
        out_specs.append(pl.BlockSpec((tm, c), lambda i: (i, 0)))
    for shp, dt in accs:
        out_shape.append(jax.ShapeDtypeStruct(shp, dt))
        out_specs.append(pl.BlockSpec(shp, lambda i, nd=len(shp): (0,) * nd))

    def kern(*refs):
        body(pl.program_id(0), n, *refs)

    return pl.pallas_call(kern, grid=(n,), in_specs=in_specs, out_specs=out_specs, out_shape=out_shape, name=name,
                          compiler_params=_cparams(1))(*args)


def _matmul_tn(a, b, name, col_shards=False, out_dtype=F32):
    t, m = a.shape
    n = b.shape[1]
    bm = m
    for cand in range(LANES, m + 1, LANES):
        if m % cand == 0 and cand * (n // N_CHIPS if col_shards else min(n, 2048)) * 4 <= 6 * 1024 * 1024:
            bm = cand
    bn = n // N_CHIPS if col_shards else (n if n <= 2048 else 1408)
    bt = min(t, 2048 if max(bm, bn) <= 512 else 1024)
    nt = t // bt

    def kern(a_ref, b_ref, o_ref, acc_ref):
        k = pl.program_id(2)

        @pl.when(k == 0)
        def _():
            acc_ref[...] = jnp.zeros_like(acc_ref)
        acc_ref[...] += _bdot_tn(a_ref[...], b_ref[...])

        @pl.when(k == nt - 1)
        def _():
            o_ref[...] = acc_ref[...].astype(out_dtype)

    if col_shards:
        out_spec = pl.BlockSpec((None, bm, bn), lambda i, j, k: (j, i, 0))
        out_shape = jax.ShapeDtypeStruct((N_CHIPS, m, bn), out_dtype)
    else:
        out_spec = pl.BlockSpec((bm, bn), lambda i, j, k: (i, j))
        out_shape = jax.ShapeDtypeStruct((m, n), out_dtype)
    return pl.pallas_call(
        kern, grid=(m // bm, n // bn, nt),
        in_specs=[pl.BlockSpec((bt, bm), lambda i, j, k: (k, i)), pl.BlockSpec((bt, bn), lambda i, j, k: (k, j))],
        out_specs=out_spec, out_shape=out_shape, scratch_shapes=[pltpu.VMEM((bm, bn), F32)], name=name,
        compiler_params=_cparams(3))(a, b)


def _in_proj(x, g, w_in_p, tm):
    def body(i, n, x_ref, g_ref, w_ref, xr, yg, cq, ckv, krp, hb):
        xv = x_ref[...]
        h = (xv * _rstd(xv) * g_ref[...]).astype(BF16)
        hb[...] = h
        p = jnp.dot(h, w_ref[...], preferred_element_type=F32)
        xr[...] = p[:, :LRU_W]
        yg[...] = p[:, LRU_W:2 * LRU_W]
        cq[...] = p[:, 2 * LRU_W:2 * LRU_W + Q_LORA]
        ckv[...] = p[:, 2 * LRU_W + Q_LORA:OFF_KR]
        krp[...] = p[:, OFF_KR:IN_PAD]

    return _rowwise(body, "in_proj", x.shape[0], tm, rows=[x], fulls=[g, w_in_p],
                    outs=[(LRU_W, F32), (LRU_W, F32), (Q_LORA, F32), (KV_LORA, F32), (LANES, F32), (D_MODEL, BF16)])


def _softplus_neg(lam):
    e = jnp.exp(-jnp.abs(lam))
    return jnp.maximum(-lam, 0.0) + jnp.where(e < 1e-2, e * (1.0 - e * (0.5 - e * (1.0 / 3.0))), jnp.log(1.0 + e))


def _lru_gates(x, halo, cw_ref, pv_ref, wa_ref, wi_ref, rev):
    down = not rev
    xc = pv_ref[0:1, :] + jnp.zeros_like(x)
    for j in range(CONV_W):
        k = j if rev else CONV_W - 1 - j
        xc = xc + cw_ref[k:k + 1, :] * _shift(x, j, halo, down)
    r = _sigmoid(_bdot(xc, wa_ref[...]) + pv_ref[1:2, :])
    ig = _sigmoid(_bdot(xc, wi_ref[...]) + pv_ref[2:3, :])
    lam = pv_ref[3:4, :]
    sp = _softplus_neg(lam)
    log_a = (-LRU_C) * r * sp
    a = jnp.exp(log_a)
    z = 2.0 * log_a
    series = -(z * (1.0 + z * (0.5 + z * (1.0 / 6.0 + z * (1.0 / 24.0)))))
    om = jnp.where(z > -0.02, series, 1.0 - a * a)
    mult = jnp.sqrt(om)
    return xc, r, ig, sp, a, mult


def _lru_scan_fwd(xr, cw, pv, wa, wi, rev, t):
    s = xr.shape[0]
    n = s // t
    hb = t // SUBLANES
    last8 = s // SUBLANES - 1
    down = not rev

    def kern(x_ref, halo_ref, cw_ref, pv_ref, wa_ref, wi_ref, h_ref, xc_ref, r_ref, ig_ref, a_ref, mult_ref, carry_ref):
        i = pl.program_id(0)
        _zero_first(i, carry_ref)
        halo = jnp.where(i == 0, 0.0, halo_ref[...])
        xc, r, ig, sp, a, mult = _lru_gates(x_ref[...], halo, cw_ref, pv_ref, wa_ref, wi_ref, rev)
        xc_ref[...], r_ref[...], ig_ref[...], a_ref[...], mult_ref[...] = xc, r, ig, a, mult
        h_ref[...] = _scan(a, mult * ig * xc, carry_ref[...], down)
        carry_ref[...] = h_ref[pl.ds(t - 1 if down else 0, 1), :]

    if rev:
        blk = lambda i: (n - 1 - i, 0)
        hal = lambda i: (jnp.minimum((n - i) * hb, last8), 0)
    else:
        blk = lambda i: (i, 0)
        hal = lambda i: (jnp.maximum(i * hb - 1, 0), 0)
    full = lambda a: pl.BlockSpec(a.shape, lambda i: (0, 0))
    return pl.pallas_call(
        kern, grid=(n,),
        in_specs=[pl.BlockSpec((t, LRU_W), blk), pl.BlockSpec((SUBLANES, LRU_W), hal), full(cw), full(pv), full(wa), full(wi)],
        out_specs=[pl.BlockSpec((t, LRU_W), blk)] * 6, out_shape=[jax.ShapeDtypeStruct((s, LRU_W), F32)] * 6,
        scratch_shapes=[pltpu.VMEM((1, LRU_W), F32)], name="lru_scan_rev" if rev else "lru_scan_fwd",
        compiler_params=_cparams(1))(xr, xr, cw, pv, wa, wi)


def _lru_scan_bwd(xr, saved, h, dh, cw, pv, wa, wi, rev, t):
    s = xr.shape[0]
    n = s // t
    hb = t // SUBLANES
    last8 = s // SUBLANES - 1
    down = not rev

    def kern(x_ref, xc_ref, r_ref, ig_ref, a_ref, mult_ref, h_ref, hh_ref, dh_ref, cw_ref, pv_ref, wa_ref, wi_ref,
             dx_ref, gwa_ref, gwi_ref, gv_ref, p_ref, dxc_halo_ref, tmp_ref):
        i = pl.program_id(0)
        _zero_first(i, gwa_ref, gwi_ref, gv_ref, p_ref, dxc_halo_ref)
        at_start = i == n - 1
        x = x_ref[...]
        hhalo = jnp.where(at_start, 0.0, hh_ref[...])
        xc, r, ig, a, mult = xc_ref[...], r_ref[...], ig_ref[...], a_ref[...], mult_ref[...]
        lam = pv_ref[3:4, :]
        sp = _softplus_neg(lam)
        h_prev = _shift(h_ref[...], 1, hhalo, down)
        row = lax.broadcasted_iota(jnp.int32, x.shape, 0)
        edge = t - 1 if down else 0
        dh_mod = dh_ref[...] + jnp.where(row == edge, p_ref[...], 0.0)
        a_next = _shift(a, 1, jnp.zeros((SUBLANES, LRU_W), F32), not down)
        g = _scan(a_next, dh_mod, jnp.zeros((1, LRU_W), F32), not down)
        tmp_ref[...] = a * g
        p_ref[...] = tmp_ref[pl.ds(0 if down else t - 1, 1), :]
        da = g * h_prev
        d_ig = g * mult * xc
        d_xc = g * mult * ig
        d_om = g * ig * xc * (0.5 / jnp.maximum(mult, 1e-30))
        d_log_a = da * a - 2.0 * d_om * a * a
        d_r = d_log_a * ((-LRU_C) * sp)
        d_sp = jnp.sum(d_log_a * ((-LRU_C) * r), axis=0, keepdims=True)
        gv_ref[7:8, :] += d_sp * (-_sigmoid(-lam))
        d_ga = d_r * r * (1.0 - r)
        d_gi = d_ig * ig * (1.0 - ig)
        _acc_row(gv_ref, 5, d_ga)
        _acc_row(gv_ref, 6, d_gi)
        d_xc = d_xc + _bdot_nt(d_ga, wa_ref[...]) + _bdot_nt(d_gi, wi_ref[...])
        gwa_ref[...] += _bdot_tn(xc, d_ga)
        gwi_ref[...] += _bdot_tn(xc, d_gi)
        _acc_row(gv_ref, 4, d_xc)
        dx = jnp.zeros_like(x)
        dxc_halo = dxc_halo_ref[...]
        for j in range(CONV_W):
            k = j if rev else CONV_W - 1 - j
            d_shift = _shift(d_xc, j, dxc_halo, not down)
            _acc_row(gv_ref, k, d_shift * x)
            dx = dx + cw_ref[k:k + 1, :] * d_shift
        dx_ref[...] = dx.astype(BF16)
        dxc_halo_ref[...] = d_xc[:SUBLANES] if down else d_xc[t - SUBLANES:]

    if rev:
        blk = lambda i: (i, 0)
        hal = lambda i: (jnp.minimum((i + 1) * hb, last8), 0)
    else:
        blk = lambda i: (n - 1 - i, 0)
        hal = lambda i: (jnp.maximum((n - 1 - i) * hb - 1, 0), 0)
    full = lambda a: pl.BlockSpec(a.shape, lambda i: (0, 0))
    bs = pl.BlockSpec((t, LRU_W), blk)
    hs = pl.BlockSpec((SUBLANES, LRU_W), hal)
    return pl.pallas_call(
        kern, grid=(n,),
        in_specs=[bs] * 7 + [hs, bs, full(cw), full(pv), full(wa), full(wi)],
        out_specs=[bs, pl.BlockSpec((LRU_W, LRU_W), lambda i: (0, 0)), pl.BlockSpec((LRU_W, LRU_W), lambda i: (0, 0)),
                   pl.BlockSpec((SUBLANES, LRU_W), lambda i: (0, 0))],
        out_shape=[jax.ShapeDtypeStruct((s, LRU_W), BF16), jax.ShapeDtypeStruct((LRU_W, LRU_W), F32),
                   jax.ShapeDtypeStruct((LRU_W, LRU_W), F32), jax.ShapeDtypeStruct((SUBLANES, LRU_W), F32)],
        scratch_shapes=[pltpu.VMEM((1, LRU_W), F32), pltpu.VMEM((SUBLANES, LRU_W), F32), pltpu.VMEM((t, LRU_W), F32)],
        name="lru_bwd_rev" if rev else "lru_bwd_fwd", compiler_params=_cparams(1))(xr, *saved, h, h, dh, cw, pv, wa, wi)


def _mla_qkv(cq, ckv, krp, cos_t, sin_t, g_qa, g_kva, g_qn, g_kn, w_uq_p, w_uk_p, w_uv, tm):
    scale = QK_HEAD ** -0.5 * LOG2E

    def body(i, n, cq_ref, ckv_ref, kr_ref, c_ref, s_ref, gqa, gkva, gqn, gkn, wq, wk, wv, q_out, k_out, v_out):
        cosv, sinv = c_ref[...], s_ref[...]
        cqv = cq_ref[...]
        qr = _bdot(cqv * _rstd(cqv) * gqa[...], wq[...])
        ckvv = ckv_ref[...]
        c_kv = (ckvv * _rstd(ckvv) * gkva[...]).astype(BF16)
        kn = jnp.dot(c_kv, wk[...], preferred_element_type=F32)
        v_out[...] = jnp.dot(c_kv, wv[...], preferred_element_type=F32).astype(BF16)
        kr = kr_ref[...]
        kr_swapped = _rope_partner(kr * gkn[...]) * sinv
        for h in range(HEADS):
            sl = slice(h * LANES, (h + 1) * LANES)
            qh = qr[:, sl]
            qh = _rope(qh * _rstd(qh, QK_HEAD) * gqn[...], cosv, sinv) * scale
            q_out[:, sl] = qh.astype(BF16)
            kh = kn[:, sl] + kr
            rs = _rstd(kh, QK_HEAD)
            k_out[:, sl] = (kh * rs * gkn[...] * cosv + kr_swapped * rs).astype(BF16)

    return _rowwise(body, "mla_qkv", cq.shape[0], tm, rows=[cq, ckv, krp, cos_t, sin_t],
                    fulls=[g_qa, g_kva, g_qn, g_kn, w_uq_p, w_uk_p, w_uv],
                    outs=[(HEADS * LANES, BF16), (HEADS * LANES, BF16), (MLA_W, BF16)])


NT_DIMS = (((1,), (1,)), ((), ()))
TN_DIMS = (((0,), (0,)), ((), ()))


def _riding_exchange(copies_fn, first, last):
    @pl.when(first)
    def _():
        for cp in copies_fn():
            cp.start()

    def finish():
        @pl.when(last)
        def _():
            for cp in copies_fn():
                cp.wait()
    return finish


def _attn_fwd(q, k, v, tq, tk, shards=()):
    s = q.shape[0]
    nq, nk = s // tq, s // tk
    n = len(shards)

    def kern(*refs):
        q_ref, k_ref, v_ref = refs[:3]
        o_ref, lse_ref = refs[3 + n:5 + n]
        acc_ref = refs[5 + 2 * n]
        p_id, i_id = pl.program_id(0), pl.program_id(1)
        finish = _riding_exchange(lambda: _gather_copies(refs[3:3 + n], refs[5 + n:5 + 2 * n], *refs[6 + 2 * n:]),
                                  (p_id == 0) & (i_id == 0), (p_id == HEADS // 2 - 1) & (i_id == nq - 1)) if n else None
        qs = (q_ref[:, :LANES], q_ref[:, LANES:])
        acc_ref[...] = jnp.zeros_like(acc_ref)

        def step(j, carry):
            off = pl.multiple_of(j * tk, tk)
            vc = v_ref[pl.ds(off, tk), :]
            out = []
            for h in range(2):
                m, l = carry[2 * h:2 * h + 2]
                st = lax.dot_general(k_ref[pl.ds(off, tk), h * LANES:(h + 1) * LANES], qs[h], NT_DIMS,
                                     preferred_element_type=F32)
                mn = jnp.maximum(m, jnp.max(st, axis=0, keepdims=True))
                al = jnp.exp2(m - mn)
                pt = jnp.exp2(st - mn)
                l = al * l + jnp.sum(pt, axis=0, keepdims=True)
                acc_ref[h] = al * acc_ref[h] + lax.dot_general(vc, pt.astype(BF16), TN_DIMS, preferred_element_type=F32)
                out += [mn, l]
            return tuple(out)

        init = (jnp.full((1, tq), -1e30, F32), jnp.zeros((1, tq), F32)) * 2
        m0, l0, m1, l1 = lax.fori_loop(0, nk, step, init)
        row = lax.broadcasted_iota(jnp.int32, (LANES, tq), 0)
        o_ref[...] = jnp.where(row < V_DIM, acc_ref[0] / l0, acc_ref[1] / l1).T
        lse_ref[0, 0:1, :] = m0 + jnp.log2(l0)
        lse_ref[0, 1:2, :] = m1 + jnp.log2(l1)
        if n:
            finish()

    return pl.pallas_call(
        kern, grid=(HEADS // 2, nq),
        in_specs=[pl.BlockSpec((tq, 2 * LANES), lambda p, i: (i, p)), pl.BlockSpec((s, 2 * LANES), lambda p, i: (0, p)),
                  pl.BlockSpec((s, LANES), lambda p, i: (0, p))] + [ANY] * n,
        out_specs=[pl.BlockSpec((tq, LANES), lambda p, i: (i, p)), pl.BlockSpec((1, 2, tq), lambda p, i: (p, 0, i))]
        + [ANY] * n,
        out_shape=[jax.ShapeDtypeStruct((s, MLA_W), F32), jax.ShapeDtypeStruct((HEADS // 2, 2, s), F32)]
        + _gather_shapes(shards),
        scratch_shapes=[pltpu.VMEM((2, LANES, tq), F32)] + (_gather_sems(n) if n else []),
        name="attn_fwd", compiler_params=_cparams(2))(q, k, v, *shards)


def _attn_bwd(q, k, v, do, lse, delta, tq, tk, contributions=()):
    s = q.shape[0]
    nq, nk = s // tq, s // tk
    n = len(contributions)

    def kern(*refs):
        q_ref, do_ref, lse_ref, dl_ref, k_ref, v_ref = refs[:6]
        dq_ref, dk_ref, dv_ref = refs[6 + n:9 + n]
        acc_ref = refs[9 + 2 * n]
        p_id, i_id = pl.program_id(0), pl.program_id(1)
        finish = _riding_exchange(lambda: _to_owner_copies(refs[6:6 + n], refs[9 + n:9 + 2 * n], *refs[10 + 2 * n:]),
                                  (p_id == 0) & (i_id == 0), (p_id == HEADS // 2 - 1) & (i_id == nq - 1)) if n else None
        _zero_first(pl.program_id(1), dk_ref, dv_ref)
        acc_ref[...] = jnp.zeros_like(acc_ref)
        qs = (q_ref[:, :LANES], q_ref[:, LANES:])
        doc = do_ref[...]
        lane_q = lax.broadcasted_iota(jnp.int32, (tq, LANES), 1)
        zq = jnp.zeros_like(doc)
        dos = (jnp.where(lane_q < V_DIM, doc, zq), jnp.where(lane_q >= V_DIM, doc, zq))
        lses = (lse_ref[0, 0:1, :], lse_ref[0, 1:2, :])
        dls = (dl_ref[0, 0:1, :], dl_ref[0, 1:2, :])

        def step(j, carry):
            off = pl.multiple_of(j * tk, tk)
            vp = v_ref[pl.ds(off, tk), :]
            lane_k = lax.broadcasted_iota(jnp.int32, (tk, LANES), 1)
            zero = jnp.zeros_like(vp)
            vs = (jnp.where(lane_k < V_DIM, vp, zero), jnp.where(lane_k >= V_DIM, vp, zero))
            for h in range(2):
                sl = slice(h * LANES, (h + 1) * LANES)
                st = lax.dot_general(k_ref[pl.ds(off, tk), sl], qs[h], NT_DIMS, preferred_element_type=F32)
                pt = jnp.exp2(st - lses[h])
                dpt = lax.dot_general(vs[h], doc, NT_DIMS, preferred_element_type=F32)
                dst = (pt * (dpt - dls[h])).astype(BF16)
                dv_ref[pl.ds(off, tk), :] += jnp.dot(pt.astype(BF16), dos[h], preferred_element_type=F32)
                dk_ref[pl.ds(off, tk), sl] += jnp.dot(dst, qs[h], preferred_element_type=F32)
                acc_ref[h] += lax.dot_general(k_ref[pl.ds(off, tk), sl], dst, TN_DIMS, preferred_element_type=F32)
            return carry

        lax.fori_loop(0, nk, step, 0)
        dq_ref[:, :LANES] = acc_ref[0].T
        dq_ref[:, LANES:] = acc_ref[1].T
        if n:
            finish()

    return pl.pallas_call(
        kern, grid=(HEADS // 2, nq),
        in_specs=[pl.BlockSpec((tq, 2 * LANES), lambda p, i: (i, p)), pl.BlockSpec((tq, LANES), lambda p, i: (i, p)),
                  pl.BlockSpec((1, 2, tq), lambda p, i: (p, 0, i)), pl.BlockSpec((1, 2, tq), lambda p, i: (p, 0, i)),
                  pl.BlockSpec((s, 2 * LANES), lambda p, i: (0, p)), pl.BlockSpec((s, LANES), lambda p, i: (0, p))]
        + [ANY] * n,
        out_specs=[pl.BlockSpec((tq, 2 * LANES), lambda p, i: (i, p)), pl.BlockSpec((s, 2 * LANES), lambda p, i: (0, p)),
                   pl.BlockSpec((s, LANES), lambda p, i: (0, p))] + [ANY] * n,
        out_shape=[jax.ShapeDtypeStruct((s, HEADS * LANES), F32), jax.ShapeDtypeStruct((s, HEADS * LANES), F32),
                   jax.ShapeDtypeStruct((s, MLA_W), F32)] + _to_owner_shapes(contributions),
        scratch_shapes=[pltpu.VMEM((2, LANES, tq), F32)] + (_to_owner_sems(n) if n else []),
        name="attn_bwd", compiler_params=_cparams(2))(q, do, lse, delta, k, v, *contributions)


def _mix_out(hf, hb, yg, o, x, g_lru, g_mla, w_out, tm):
    def body(i, n, hf_ref, hb_ref, yg_ref, o_ref, x_ref, gl, gm, w_ref, x1_ref, mix_ref):
        lo = (hf_ref[...] + hb_ref[...]) * _gelu(yg_ref[...])
        ov = o_ref[...]
        mix_ref[:, :LRU_W] = (lo * _rstd(lo) * gl[...]).astype(BF16)
        mix_ref[:, LRU_W:] = (ov * _rstd(ov) * gm[...]).astype(BF16)
        x1_ref[...] = x_ref[...] + jnp.dot(mix_ref[...], w_ref[...], preferred_element_type=F32)

    return _rowwise(body, "mix_out", x.shape[0], tm, rows=[hf, hb, yg, o, x], fulls=[g_lru, g_mla, w_out],
                    outs=[(D_MODEL, F32), (2 * LRU_W, BF16)])


def _mem_kv(mem, g_mem, w_kv, g_k):
    m = mem.shape[0]

    def body(i, n, mem_ref, g_ref, w_ref, gk_ref, km_ref, vm_ref):
        mv = mem_ref[...]
        kv = _bdot(mv * _rstd(mv) * g_ref[...], w_ref[...])
        vm_ref[...] = kv[:, MEM_W:].astype(BF16)
        for h in range(MEM_HEADS):
            sl = slice(h * MEM_HD, (h + 1) * MEM_HD)
            kh = kv[:, sl]
            km_ref[:, sl] = (kh * _rstd(kh) * gk_ref[...]).astype(BF16)

    return _rowwise(body, "mem_kv", m, m, rows=[mem], fulls=[g_mem, w_kv, g_k], outs=[(MEM_W, BF16), (MEM_W, BF16)])


def _mem_attn_core(x1v, g_ref, wq_ref, gq_ref, km_ref, vm_ref):
    scale = MEM_HD ** -0.5
    hm = (x1v * _rstd(x1v) * g_ref[...]).astype(BF16)
    qr = jnp.dot(hm, wq_ref[...], preferred_element_type=F32)
    heads = []
    for h in range(MEM_HEADS):
        sl = slice(h * MEM_HD, (h + 1) * MEM_HD)
        qh = qr[:, sl]
        rs = _rstd(qh)
        qn = (qh * rs * gq_ref[...]).astype(BF16)
        sc = lax.dot_general(qn, km_ref[:, sl], (((1,), (1,)), ((), ())), preferred_element_type=F32) * scale
        e = jnp.exp(sc - jnp.max(sc, axis=-1, keepdims=True))
        p = e / jnp.sum(e, axis=-1, keepdims=True)
        oh = jnp.dot(p.astype(BF16), vm_ref[:, sl], preferred_element_type=F32)
        heads.append((qh, rs, qn, p, oh))
    return hm, heads


def _mem_attn(x1, g, w_q, g_q, km, vm, w_o, tm):
    cs = D_MODEL // N_CHIPS

    def body(i, n, x1_ref, g_ref, wq_ref, gq_ref, km_ref, vm_ref, wo_ref, x2_ref, ob_ref):
        x1v = x1_ref[...]
        _, heads = _mem_attn_core(x1v, g_ref, wq_ref, gq_ref, km_ref, vm_ref)
        for h in range(MEM_HEADS):
            ob_ref[:, h * MEM_HD:(h + 1) * MEM_HD] = heads[h][4].astype(BF16)
        for k in range(N_CHIPS):
            sl = slice(k * cs, (k + 1) * cs)
            x2_ref[:, sl] = x1v[:, sl] + jnp.dot(ob_ref[...], wo_ref[k], preferred_element_type=F32)

    return _rowwise(body, "mem_attn", x1.shape[0], tm, rows=[x1], fulls=[g, w_q, g_q, km, vm, w_o],
                    outs=[(D_MODEL, F32), (MEM_W, BF16)])


def _ffn_up(x2, g, w_up, tm):
    cs = 2 * D_FF // N_CHIPS

    def body(i, n, x_ref, g_ref, w_ref, gu_ref, hb_ref):
        xv = x_ref[...]
        hb_ref[...] = (xv * _rstd(xv) * g_ref[...]).astype(BF16)
        for k in range(N_CHIPS):
            gu_ref[:, k * cs:(k + 1) * cs] = jnp.dot(hb_ref[...], w_ref[k], preferred_element_type=F32)

    return _rowwise(body, "ffn_up", x2.shape[0], tm, rows=[x2], fulls=[g, w_up], outs=[(2 * D_FF, F32), (D_MODEL, BF16)])


def _ffn_conv(gu, prev, nxt, cw_ref, i, n):
    prev = jnp.where(i == 0, 0.0, prev)
    nxt = jnp.where(i == n - 1, 0.0, nxt)
    return (cw_ref[3:4, :] + cw_ref[0:1, :] * _shift_down(gu, 1, prev) + cw_ref[1:2, :] * gu
            + cw_ref[2:3, :] * _shift_up(gu, 1, nxt))


def _ffn_down_loss(gu_pre, x2, target, cw, w_down, tm):
    def body(i, n, gu_ref, x_ref, t_ref, pv_ref, nx_ref, cw_ref, w_ref, dy_ref, dyb_ref, act_ref, dgu_ref, loss_ref):
        _zero_first(i, loss_ref)
        gu = _ffn_conv(gu_ref[...], pv_ref[...], nx_ref[...], cw_ref, i, n)
        g, u = gu[:, :D_FF], gu[:, D_FF:]
        sg = _sigmoid(g)
        a = g * sg
        act_ref[...] = (a * u).astype(BF16)
        y = x_ref[...] + jnp.dot(act_ref[...], w_ref[...], preferred_element_type=F32)
        e = y - t_ref[...]
        loss_ref[...] += jnp.sum(e * e)
        dy = e * (1.0 / D_MODEL)
        dy_ref[...] = dy
        dyb_ref[...] = dy.astype(BF16)
        d_act = lax.dot_general(dyb_ref[...], w_ref[...], NT_DIMS, preferred_element_type=F32)
        dgu_ref[:, :D_FF] = ((d_act * u) * (sg + a - a * sg)).astype(BF16)
        dgu_ref[:, D_FF:] = (d_act * a).astype(BF16)

    return _rowwise(body, "ffn_down_loss", x2.shape[0], tm, rows=[gu_pre, x2, target], halos=[gu_pre], fulls=[cw, w_down],
                    outs=[(D_MODEL, F32), (D_MODEL, BF16), (D_FF, BF16), (2 * D_FF, BF16)], accs=[((SUBLANES, LANES), F32)])


def _ffn_bwd_conv(dgu, gu_pre, cw, tm):
    def body(i, n, d_ref, g_ref, dp_ref, dn_ref, cw_ref, dpre_ref, gc_ref):
        _zero_first(i, gc_ref)
        d = d_ref[...].astype(F32)
        g = g_ref[...]
        d_next = _shift_up(d, 1, jnp.where(i == n - 1, 0.0, dn_ref[...].astype(F32)[:SUBLANES]))
        d_prev = _shift_down(d, 1, jnp.where(i == 0, 0.0, dp_ref[...].astype(F32)[SUBLANES:]))
        dpre_ref[...] = (cw_ref[0:1, :] * d_next + cw_ref[1:2, :] * d + cw_ref[2:3, :] * d_prev).astype(BF16)
        _acc_row(gc_ref, 0, d_next * g)
        _acc_row(gc_ref, 1, d * g)
        _acc_row(gc_ref, 2, d_prev * g)
        _acc_row(gc_ref, 3, d)

    return _rowwise(body, "ffn_bwd_conv", dgu.shape[0], tm, rows=[dgu, gu_pre], halos=[dgu], fulls=[cw],
                    outs=[(2 * D_FF, BF16)], accs=[((SUBLANES, 2 * D_FF), F32)])


def _ffn_bwd_in(dpre, x2, dy, g, w_up, tm):
    cs = 2 * D_FF // N_CHIPS

    def body(i, n, dp_ref, x_ref, dy_ref, g_ref, w_ref, dx_ref, dxb_ref, gg_ref):
        _zero_first(i, gg_ref)
        d_h = jnp.zeros(x_ref.shape, F32)
        for k in range(N_CHIPS):
            d_h = d_h + lax.dot_general(dp_ref[:, k * cs:(k + 1) * cs], w_ref[k], (((1,), (1,)), ((), ())),
                                        preferred_element_type=F32)
        xv = x_ref[...]
        dx, dg = _norm_bwd(xv, _rstd(xv), g_ref[...], d_h)
        _acc_row(gg_ref, 0, dg)
        dx = dx + dy_ref[...]
        dx_ref[...] = dx
        dxb_ref[...] = dx.astype(BF16)

    return _rowwise(body, "ffn_bwd_in", x2.shape[0], tm, rows=[dpre, x2, dy], fulls=[g, w_up],
                    outs=[(D_MODEL, F32), (D_MODEL, BF16)], accs=[((SUBLANES, D_MODEL), F32)])


def _mem_attn_bwd(x1, dx2, dx2b, g, w_q, g_q, km, vm, w_o, tm):
    scale = MEM_HD ** -0.5
    m = km.shape[0]

    def body(i, n, x1_ref, dx2_ref, dx2b_ref, g_ref, wq_ref, gq_ref, km_ref, vm_ref, wo_ref,
             dx1_ref, dx1b_ref, hm_ref, dqr_ref, dkm_ref, dvm_ref, gg_ref, ggq_ref):
        _zero_first(i, dkm_ref, dvm_ref, gg_ref, ggq_ref)
        x1v = x1_ref[...]
        hm, heads = _mem_attn_core(x1v, g_ref, wq_ref, gq_ref, km_ref, vm_ref)
        hm_ref[...] = hm
        cs = D_MODEL // N_CHIPS
        d_o = jnp.zeros((x1v.shape[0], MEM_W), F32)
        for k in range(N_CHIPS):
            d_o = d_o + lax.dot_general(dx2b_ref[:, k * cs:(k + 1) * cs], wo_ref[k], (((1,), (1,)), ((), ())),
                                        preferred_element_type=F32)
        for h in range(MEM_HEADS):
            sl = slice(h * MEM_HD, (h + 1) * MEM_HD)
            qh, rs, qn, p, _ = heads[h]
            d_oh = d_o[:, sl].astype(BF16)
            dp = lax.dot_general(d_oh, vm_ref[:, sl], (((1,), (1,)), ((), ())), preferred_element_type=F32)
            ds = (p * (dp - jnp.sum(dp * p, axis=-1, keepdims=True)) * scale).astype(BF16)
            dqn = jnp.dot(ds, km_ref[:, sl], preferred_element_type=F32)
            dkm_ref[:, sl] += lax.dot_general(ds, qn, (((0,), (0,)), ((), ())), preferred_element_type=F32)
            dvm_ref[:, sl] += lax.dot_general(p.astype(BF16), d_oh, (((0,), (0,)), ((), ())), preferred_element_type=F32)
            dqh, dgq = _norm_bwd(qh, rs, gq_ref[...], dqn)
            _acc_row(ggq_ref, 0, dgq)
            dqr_ref[:, sl] = dqh.astype(BF16)
        d_hm = lax.dot_general(dqr_ref[...], wq_ref[...], (((1,), (1,)), ((), ())), preferred_element_type=F32)
        dx, dg = _norm_bwd(x1v, _rstd(x1v), g_ref[...], d_hm)
        _acc_row(gg_ref, 0, dg)
        dx = dx + dx2_ref[...]
        dx1_ref[...] = dx
        dx1b_ref[...] = dx.astype(BF16)

    return _rowwise(body, "mem_attn_bwd", x1.shape[0], tm, rows=[x1, dx2, dx2b], fulls=[g, w_q, g_q, km, vm, w_o],
                    outs=[(D_MODEL, F32), (D_MODEL, BF16), (D_MODEL, BF16), (MEM_W, BF16)],
                    accs=[((m, MEM_W), F32), ((m, MEM_W), F32), ((SUBLANES, D_MODEL), F32), ((SUBLANES, MEM_HD), F32)])


def _mem_kv_bwd(mem, g_mem, w_kv, g_k, dkm, dvm):
    m = mem.shape[0]

    def body(i, n, mem_ref, dkm_ref, dvm_ref, g_ref, w_ref, gk_ref, gw_ref, gg_ref, ggk_ref, dkv_ref):
        gg_ref[...] = jnp.zeros_like(gg_ref)
        ggk_ref[...] = jnp.zeros_like(ggk_ref)
        mv = mem_ref[...]
        rs_m = _rstd(mv)
        mem_n = (mv * rs_m * g_ref[...]).astype(BF16)
        kv = jnp.dot(mem_n, w_ref[...], preferred_element_type=F32)
        for h in range(MEM_HEADS):
            sl = slice(h * MEM_HD, (h + 1) * MEM_HD)
            kh = kv[:, sl]
            dkh, dgk = _norm_bwd(kh, _rstd(kh), gk_ref[...], dkm_ref[:, sl])
            _acc_row(ggk_ref, 0, dgk)
            dkv_ref[:, sl] = dkh.astype(BF16)
        dkv_ref[:, MEM_W:] = dvm_ref[...].astype(BF16)
        gw_ref[...] = lax.dot_general(mem_n, dkv_ref[...], (((0,), (0,)), ((), ())), preferred_element_type=F32)
        d_mn = lax.dot_general(dkv_ref[...], w_ref[...], (((1,), (1,)), ((), ())), preferred_element_type=F32)
        _acc_row(gg_ref, 0, d_mn * (mv * rs_m))

    return _rowwise(body, "mem_kv_bwd", m, m, rows=[mem, dkm, dvm], fulls=[g_mem, w_kv, g_k],
                    accs=[((D_MODEL, 2 * MEM_W), F32), ((SUBLANES, D_MODEL), F32), ((SUBLANES, MEM_HD), F32),
                          ((m, 2 * MEM_W), BF16)])


def _mix_out_bwd(dx1b, hf, hb, yg, o, g_lru, g_mla, w_out, tm):
    def body(i, n, dx_ref, hf_ref, hb_ref, yg_ref, o_ref, gl, gm, w_ref, dh_ref, dyg_ref, dob_ref, dl_ref, ggl_ref, ggm_ref):
        _zero_first(i, ggl_ref, ggm_ref)
        dmix = lax.dot_general(dx_ref[...], w_ref[...], (((1,), (1,)), ((), ())), preferred_element_type=F32)
        hs = hf_ref[...] + hb_ref[...]
        ygv = yg_ref[...]
        ge = _gelu(ygv)
        lo = hs * ge
        d_lo, dgl = _norm_bwd(lo, _rstd(lo), gl[...], dmix[:, :LRU_W])
        _acc_row(ggl_ref, 0, dgl)
        dh_ref[...] = d_lo * ge
        dyg_ref[...] = (d_lo * hs * _gelu_grad(ygv)).astype(BF16)
        ov = o_ref[...]
        d_o, dgm = _norm_bwd(ov, _rstd(ov), gm[...], dmix[:, LRU_W:])
        _acc_row(ggm_ref, 0, dgm)
        dob_ref[...] = d_o.astype(BF16)
        prod = d_o * ov
        lane_w = lax.broadcasted_iota(jnp.int32, prod.shape, 1)
        lane = lax.broadcasted_iota(jnp.int32, (prod.shape[0], LANES), 1)
        dl = jnp.zeros((prod.shape[0], LANES), F32)
        for h in range(HEADS):
            in_head = (lane_w >= h * V_DIM) & (lane_w < (h + 1) * V_DIM)
            dl = dl + jnp.where(lane == h, jnp.sum(jnp.where(in_head, prod, 0.0), axis=-1, keepdims=True), 0.0)
        dl_ref[...] = dl

    return _rowwise(body, "mix_out_bwd", dx1b.shape[0], tm, rows=[dx1b, hf, hb, yg, o], fulls=[g_lru, g_mla, w_out],
                    outs=[(LRU_W, F32), (LRU_W, BF16), (MLA_W, BF16), (LANES, F32)],
                    accs=[((SUBLANES, LRU_W), F32), ((SUBLANES, MLA_W), F32)])


def _mla_qkv_bwd(cq, ckv, krp, cos_t, sin_t, dq, dk, dv, g_qa, g_kva, g_qn, g_kn, w_uq_p, w_uk_p, w_uv, tm):
    scale = QK_HEAD ** -0.5

    def body(i, n, cq_ref, ckv_ref, kr_ref, c_ref, s_ref, dq_ref, dk_ref, dv_ref, gqa, gkva, gqn, gkn, wq, wk, wv,
             dcq_ref, dckv_ref, dkr_ref, cqb_ref, dqr_ref, ckvb_ref, dkn_ref, dvb_ref, ggqa, ggkva, ggqn, ggkn):
        _zero_first(i, ggqa, ggkva, ggqn, ggkn)
        cosv, sinv = c_ref[...], s_ref[...]
        cqv = cq_ref[...]
        rs_q = _rstd(cqv)
        cqb_ref[...] = (cqv * rs_q * gqa[...]).astype(BF16)
        qr = jnp.dot(cqb_ref[...], wq[...], preferred_element_type=F32)
        ckvv = ckv_ref[...]
        rs_kv = _rstd(ckvv)
        ckvb_ref[...] = (ckvv * rs_kv * gkva[...]).astype(BF16)
        kn = jnp.dot(ckvb_ref[...], wk[...], preferred_element_type=F32)
        kr = kr_ref[...]
        dkr = jnp.zeros_like(kr)
        for h in range(HEADS):
            sl = slice(h * LANES, (h + 1) * LANES)
            qh = qr[:, sl]
            d_qn = _rope_t(dq_ref[:, sl] * scale, cosv, sinv)
            dqh, dgq = _norm_bwd(qh, _rstd(qh, QK_HEAD), gqn[...], d_qn, QK_HEAD)
            _acc_row(ggqn, 0, dgq)
            dqr_ref[:, sl] = dqh.astype(BF16)
            kh = kn[:, sl] + kr
            d_kn = _rope_t(dk_ref[:, sl] * (1.0 / LOG2E), cosv, sinv)
            dkh, dgk = _norm_bwd(kh, _rstd(kh, QK_HEAD), gkn[...], d_kn, QK_HEAD)
            _acc_row(ggkn, 0, dgk)
            dkn_ref[:, sl] = dkh.astype(BF16)
            dkr = dkr + dkh
        dkr_ref[...] = dkr.astype(BF16)
        dvb_ref[...] = dv_ref[...].astype(BF16)
        d_cq = lax.dot_general(dqr_ref[...], wq[...], (((1,), (1,)), ((), ())), preferred_element_type=F32)
        dcq, dg = _norm_bwd(cqv, rs_q, gqa[...], d_cq)
        _acc_row(ggqa, 0, dg)
        dcq_ref[...] = dcq.astype(BF16)
        d_ckv = (lax.dot_general(dkn_ref[...], wk[...], (((1,), (1,)), ((), ())), preferred_element_type=F32)
                 + lax.dot_general(dvb_ref[...], wv[...], (((1,), (1,)), ((), ())), preferred_element_type=F32))
        dckv, dg = _norm_bwd(ckvv, rs_kv, gkva[...], d_ckv)
        _acc_row(ggkva, 0, dg)
        dckv_ref[...] = dckv.astype(BF16)

    return _rowwise(body, "mla_qkv_bwd", cq.shape[0], tm, rows=[cq, ckv, krp, cos_t, sin_t, dq, dk, dv],
                    fulls=[g_qa, g_kva, g_qn, g_kn, w_uq_p, w_uk_p, w_uv],
                    outs=[(Q_LORA, BF16), (KV_LORA, BF16), (LANES, BF16), (Q_LORA, BF16), (HEADS * LANES, BF16),
                          (KV_LORA, BF16), (HEADS * LANES, BF16), (MLA_W, BF16)],
                    accs=[((SUBLANES, Q_LORA), F32), ((SUBLANES, KV_LORA), F32), ((SUBLANES, LANES), F32),
                          ((SUBLANES, LANES), F32)])


def _in_proj_bwd(x, dx1, dxr_f, dxr_b, dyg, dcq, dckv, dkrp, g, w_in_p, tm):
    def body(i, n, x_ref, dx1_ref, df_ref, db_ref, dyg_ref, dcq_ref, dckv_ref, dkr_ref, g_ref, w_ref, gx_ref, dp_ref, gg_ref):
        _zero_first(i, gg_ref)
        dp_ref[:, :LRU_W] = (df_ref[...].astype(F32) + db_ref[...].astype(F32)).astype(BF16)
        dp_ref[:, LRU_W:2 * LRU_W] = dyg_ref[...].astype(BF16)
        dp_ref[:, 2 * LRU_W:2 * LRU_W + Q_LORA] = dcq_ref[...].astype(BF16)
        dp_ref[:, 2 * LRU_W + Q_LORA:OFF_KR] = dckv_ref[...].astype(BF16)
        dp_ref[:, OFF_KR:] = dkr_ref[...].astype(BF16)
        d_h = lax.dot_general(dp_ref[...], w_ref[...], (((1,), (1,)), ((), ())), preferred_element_type=F32)
        xv = x_ref[...]
        dx, dg = _norm_bwd(xv, _rstd(xv), g_ref[...], d_h)
        _acc_row(gg_ref, 0, dg)
        gx_ref[...] = dx + dx1_ref[...]

    return _rowwise(body, "in_proj_bwd", x.shape[0], tm, rows=[x, dx1, dxr_f, dxr_b, dyg, dcq, dckv, dkrp],
                    fulls=[g, w_in_p], outs=[(D_MODEL, F32), (IN_PAD, BF16)], accs=[((SUBLANES, D_MODEL), F32)])


ANY = pl.BlockSpec(memory_space=pl.ANY)


def _chip_peers(x, y):
    return ((1 - x, y), (x, 1 - y), (1 - x, 1 - y))


def _exchange_call(kern, name, ins, out_shapes, n_sems, aliases=None):
    return pl.pallas_call(
        kern, in_specs=[ANY] * len(ins), out_specs=[ANY] * len(out_shapes), out_shape=out_shapes,
        scratch_shapes=[pltpu.SemaphoreType.DMA((n,)) for n in n_sems], input_output_aliases=aliases or {},
        name=name)(*ins)


def _start_then_wait(copies):
    for cp in copies:
        cp.start()
    for cp in copies:
        cp.wait()


N_DEV = 8
RELATIONS = tuple((dx, dy, dc) for dx in (0, 1) for dy in (0, 1) for dc in (0, 1))[1:]


def _flip(v, d):
    return 1 - v if d else v


def _gather_copies(ins, outs, ssem, rsem, lsem):
    x, y, c = lax.axis_index("x"), lax.axis_index("y"), lax.axis_index("c")
    me = 2 * x + y
    cps = []
    for i, (a, o) in enumerate(zip(ins, outs)):
        cps.append(pltpu.make_async_copy(a, o.at[me], lsem.at[i]))
        for j, (px, py) in enumerate(_chip_peers(x, y)):
            cps.append(pltpu.make_async_remote_copy(a, o.at[me], ssem.at[3 * i + j], rsem.at[3 * i + j],
                                                    device_id=(px, py, c), device_id_type=MESH))
    return cps


def _gather_shapes(arrs):
    return [jax.ShapeDtypeStruct((N_CHIPS,) + a.shape, a.dtype) for a in arrs]


def _gather_sems(n):
    return [pltpu.SemaphoreType.DMA((3 * n,)), pltpu.SemaphoreType.DMA((3 * n,)), pltpu.SemaphoreType.DMA((n,))]


def _gather_chips(arrs):
    n = len(arrs)

    def kern(*refs):
        _start_then_wait(_gather_copies(refs[:n], refs[n:2 * n], *refs[2 * n:]))

    return _exchange_call(kern, "gather_weights", arrs, _gather_shapes(arrs), (3 * n, 3 * n, n))


def _to_owner_copies(ins, outs, ssem, rsem, lsem):
    x, y, c = lax.axis_index("x"), lax.axis_index("y"), lax.axis_index("c")
    me = 4 * x + 2 * y + c
    cps = []
    for i, (a, o) in enumerate(zip(ins, outs)):
        cps.append(pltpu.make_async_copy(a.at[2 * x + y, c], o.at[me], lsem.at[i]))
        for r, (dx, dy, dc) in enumerate(RELATIONS):
            tx, ty, tc = _flip(x, dx), _flip(y, dy), _flip(c, dc)
            cps.append(pltpu.make_async_remote_copy(a.at[2 * tx + ty, tc], o.at[me], ssem.at[7 * i + r], rsem.at[7 * i + r],
                                                    device_id=(tx, ty, tc), device_id_type=MESH))
    return cps


def _to_owner_shapes(arrs):
    return [jax.ShapeDtypeStruct((N_DEV,) + a.shape[2:], a.dtype) for a in arrs]


def _to_owner_sems(n):
    return [pltpu.SemaphoreType.DMA((7 * n,)), pltpu.SemaphoreType.DMA((7 * n,)), pltpu.SemaphoreType.DMA((n,))]


def _to_owner(arrs, name):
    n = len(arrs)

    def kern(*refs):
        _start_then_wait(_to_owner_copies(refs[:n], refs[n:2 * n], *refs[2 * n:]))

    return _exchange_call(kern, name, arrs, _to_owner_shapes(arrs), (7 * n, 7 * n, n))


def _join_halves(arrs):
    n = len(arrs)

    def kern(*refs):
        outs, (ssem, rsem) = refs[n:2 * n], refs[2 * n:]
        x, y, c = lax.axis_index("x"), lax.axis_index("y"), lax.axis_index("c")
        _start_then_wait([
            pltpu.make_async_remote_copy(outs[i].at[c], outs[i].at[c], ssem.at[i], rsem.at[i],
                                         device_id=(x, y, 1 - c), device_id_type=MESH) for i in range(n)])

    outs = [jax.ShapeDtypeStruct(a.shape, a.dtype) for a in arrs]
    return _exchange_call(kern, "grad_join_halves", arrs, outs, (n, n), aliases={i: i for i in range(n)})


def _row_block(rows, row_bytes, limit=1 << 20):
    best = None
    for d in range(16, rows + 1, 16):
        if rows % d == 0 and d * row_bytes <= limit:
            best = d
    return best if best is not None else rows


def _sum_devices(b, c, name):
    _, h, cols = b.shape
    hb = _row_block(h, cols * 4)

    def kern(c_ref, b_ref, o_ref):
        acc = b_ref[0].astype(F32)
        for j in range(1, N_DEV):
            acc = acc + b_ref[j].astype(F32)
        o_ref[...] = acc

    return pl.pallas_call(
        kern,
        grid_spec=pltpu.PrefetchScalarGridSpec(
            num_scalar_prefetch=1, grid=(h // hb,),
            in_specs=[pl.BlockSpec((N_DEV, hb, cols), lambda i, c_ref: (0, i, 0))],
            out_specs=pl.BlockSpec((None, hb, cols), lambda i, c_ref: (c_ref[0], i, 0))),
        out_shape=jax.ShapeDtypeStruct((2, h, cols), F32), name=name, compiler_params=_cparams(1))(c, b)


def _adamw(w, g, m, v, name):
    rows, cols = w.shape
    rb = _row_block(rows, cols * 4)
    c1 = 1.0 - ADAM_B1 ** ADAM_STEP
    c2 = 1.0 - ADAM_B2 ** ADAM_STEP

    def kern(w_ref, g_ref, m_ref, v_ref, d_ref, mo_ref, vo_ref):
        gv = g_ref[...]
        mn = ADAM_B1 * m_ref[...] + (1.0 - ADAM_B1) * gv
        vn = ADAM_B2 * v_ref[...] + (1.0 - ADAM_B2) * (gv * gv)
        mo_ref[...] = mn
        vo_ref[...] = vn
        d_ref[...] = (-ADAM_LR) * ((mn / c1) / (jnp.sqrt(vn / c2) + ADAM_EPS) + ADAM_WD * w_ref[...])

    spec = pl.BlockSpec((rb, cols), lambda i: (i, 0))
    return pl.pallas_call(
        kern, grid=(rows // rb,), in_specs=[spec] * 4, out_specs=[spec] * 3,
        out_shape=[jax.ShapeDtypeStruct(w.shape, F32)] * 3, name=name, compiler_params=_cparams(1))(w, g, m, v)


def _pad_rows(flat, rows):
    return jnp.pad(flat, (0, rows * LANES - flat.shape[0])).reshape(rows, LANES)


def _round_up(n, m):
    return (n + m - 1) // m * m


def _shard_shape(shape, axis):
    return tuple(s // N_CHIPS if a == axis else s for a, s in enumerate(shape))


def _to_shards(full, axis):
    shape = full.shape
    t = full.reshape(shape[:axis] + (N_CHIPS, shape[axis] // N_CHIPS) + shape[axis + 1:])
    return jnp.moveaxis(t, axis, 0).reshape(N_CHIPS, -1)


def _from_shards(sh, shape, axis):
    t = sh.reshape((N_CHIPS,) + _shard_shape(shape, axis))
    t = jnp.moveaxis(t, 0, axis)
    return t.reshape(shape)


BIG = tuple((name, shape, axis) for name, shape, axis, big in SHARDED if big)
EARLY_WEIGHTS = ("w_in", "w_uq", "w_ukv")
SMALL_SHARDED = tuple((name, shape, axis) for name, shape, axis, big in SHARDED if not big)


def _pack_small_weights(p):
    flat = jnp.concatenate([p[name].reshape(-1) for name, _, _ in SMALL_SHARDED])
    return _pad_rows(flat, _round_up(-(-flat.shape[0] // LANES), SUBLANES))


def _unpack_small_weights(gathered):
    flat = gathered.reshape(N_CHIPS, -1)
    out, off = {}, 0
    for name, shape, axis in SMALL_SHARDED:
        n = _numel(shape) // N_CHIPS
        out[name] = _from_shards(flat[:, off:off + n], shape, axis)
        off += n
    return out


def _pack_small_local(p, prefix=""):
    parts = [p[prefix + name].reshape(-1) for name, _, _ in SMALL_SHARDED]
    parts += [p[prefix + name].reshape(-1) for name, _ in REPLICATED]
    return jnp.concatenate(parts)


def _pack_small_grads(g, loss_part):
    parts = [_to_shards(g[name], axis) for name, _, axis in SMALL_SHARDED]
    rep = jnp.concatenate([g[name].reshape(-1) for name, _ in REPLICATED] + [loss_part.reshape(1)])
    parts.append(jnp.broadcast_to(rep[None], (N_CHIPS, rep.shape[0])))
    return jnp.concatenate(parts, axis=1)


def _unpack_small_local(flat):
    out, off = {}, 0
    for name, shape, axis in SMALL_SHARDED:
        n = _numel(shape) // N_CHIPS
        out[name] = flat[off:off + n].reshape((1,) + _shard_shape(shape, axis))
        off += n
    for name, shape in REPLICATED:
        n = _numel(shape)
        out[name] = flat[off:off + n].reshape((1,) + shape)
        off += n
    return out


def _grad_shards(g, shape, axis):
    if axis == 0:
        return g.reshape((N_CHIPS,) + _shard_shape(shape, axis))
    return jnp.transpose(g.reshape(shape[0], N_CHIPS, shape[1] // N_CHIPS), (1, 0, 2))


def _cols_from_shards(w4):
    return jnp.transpose(w4, (1, 0, 2)).reshape(w4.shape[1], -1)


def _block_diag(w):
    eye = jnp.eye(LRU_BLOCKS, dtype=w.dtype)
    return jnp.einsum("ncd,nm->ncmd", w, eye).reshape(LRU_W, LRU_W)


def _block_diag_t(g):
    g4 = g.reshape(LRU_BLOCKS, 64, LRU_BLOCKS, 64)
    eye = jnp.eye(LRU_BLOCKS, dtype=g.dtype)[:, None, :, None]
    return jnp.sum(g4 * eye, axis=2)


def _pad8(a):
    return jnp.pad(a, ((0, SUBLANES - a.shape[0]), (0, 0)))


def kernel(x, mem, positions, attn_norm, w_in, lru_conv_w, lru_conv_b, lru_w_a, lru_b_a, lru_w_i, lru_b_i, lru_lambda, q_a_norm, w_uq, kv_a_norm, w_ukv, mla_q_norm, mla_k_norm, lru_out_norm, mla_out_norm, w_out, mem_attn_norm, mem_norm, w_mem_q, w_mem_kv, mem_q_norm, mem_k_norm, w_mem_o, ffn_norm, w_up, ffn_conv_w, ffn_conv_b, w_down, loss_target, m_attn_norm, m_w_in, m_lru_conv_w, m_lru_conv_b, m_lru_w_a, m_lru_b_a, m_lru_w_i, m_lru_b_i, m_lru_lambda, m_q_a_norm, m_w_uq, m_kv_a_norm, m_w_ukv, m_mla_q_norm, m_mla_k_norm, m_lru_out_norm, m_mla_out_norm, m_w_out, m_mem_attn_norm, m_mem_norm, m_w_mem_q, m_w_mem_kv, m_mem_q_norm, m_mem_k_norm, m_w_mem_o, m_ffn_norm, m_w_up, m_ffn_conv_w, m_ffn_conv_b, m_w_down, v_attn_norm, v_w_in, v_lru_conv_w, v_lru_conv_b, v_lru_w_a, v_lru_b_a, v_lru_w_i, v_lru_b_i, v_lru_lambda, v_q_a_norm, v_w_uq, v_kv_a_norm, v_w_ukv, v_mla_q_norm, v_mla_k_norm, v_lru_out_norm, v_mla_out_norm, v_w_out, v_mem_attn_norm, v_mem_norm, v_w_mem_q, v_w_mem_kv, v_mem_q_norm, v_mem_k_norm, v_w_mem_o, v_ffn_norm, v_w_up, v_ffn_conv_w, v_ffn_conv_b, v_w_down):
    given = dict(locals())
    local = {name: given[name][0] for name in WEIGHT_ORDER}
    s = x.shape[1]
    x2d, mem2d, tgt = x[0], mem[0], loss_target[0]
    tm = min(512, s)
    tm_wide = min(1024, s)
    tm_ffn = min(256, s)
    t_scan = min(1024, s)
    tq_f, tq_b, tk = min(4096, s), min(2048, s), min(512, s)

    early = [b for b in BIG if b[0] in EARLY_WEIGHTS]
    late = [b for b in BIG if b[0] not in EARLY_WEIGHTS]
    got = _gather_chips([local[name].astype(BF16) for name, _, _ in early] + [_pack_small_weights(local)])
    full = _unpack_small_weights(got[-1])

    def take_gathered(entries, arrays):
        for (name, shape, axis), w4 in zip(entries, arrays):
            if axis == 0:
                full[name] = w4.reshape(shape)
            elif name in ("w_up", "w_mem_o"):
                full[name] = w4
            else:
                full[name] = _cols_from_shards(w4)

    take_gathered(early, got)
    row = lambda a: a.reshape(1, -1)
    b16 = lambda a: a.astype(BF16)
    zeros = lambda r, c: jnp.zeros((r, c), BF16)
    w_in_f = full["w_in"]
    w_in_p = jnp.concatenate([w_in_f[:, :OFF_KR], _head_tile(zeros(D_MODEL, QK_NOPE), w_in_f[:, OFF_KR:])], axis=1)
    uq = full["w_uq"].reshape(Q_LORA, HEADS, QK_HEAD)
    w_uq_p = _head_tile(uq[:, :, :QK_NOPE], uq[:, :, QK_NOPE:]).reshape(Q_LORA, -1)
    ukv = full["w_ukv"].reshape(KV_LORA, HEADS, QK_NOPE + V_DIM)
    w_uk_p = _head_tile(ukv[:, :, :QK_NOPE], None).reshape(KV_LORA, -1)
    w_uv = ukv[:, :, QK_NOPE:].reshape(KV_LORA, MLA_W)
    wa = [b16(_block_diag(local["lru_w_a"][d])) for d in range(2)]
    wi = [b16(_block_diag(local["lru_w_i"][d])) for d in range(2)]
    cw = [_pad8(full["lru_conv_w"][d]) for d in range(2)]
    pv = [_pad8(jnp.stack([full["lru_conv_b"][d], full["lru_b_a"][d], full["lru_b_i"][d], full["lru_lambda"][d]]))
          for d in range(2)]
    ffn_cw = _pad8(jnp.concatenate([full["ffn_conv_w"], row(local["ffn_conv_b"])], axis=0))
    g_attn, g_qa, g_kva = row(local["attn_norm"]), row(local["q_a_norm"]), row(local["kv_a_norm"])
    g_qn = _head_tile(row(local["mla_q_norm"])[:, :QK_NOPE], row(local["mla_q_norm"])[:, QK_NOPE:])
    g_kn = _head_tile(row(local["mla_k_norm"])[:, :QK_NOPE], row(local["mla_k_norm"])[:, QK_NOPE:])
    g_lru, g_mla = row(local["lru_out_norm"]), row(local["mla_out_norm"])
    g_memattn, g_mem = row(local["mem_attn_norm"]), row(local["mem_norm"])
    g_mq, g_mk, g_ffn = row(local["mem_q_norm"]), row(local["mem_k_norm"]), row(local["ffn_norm"])

    inv = ROPE_THETA ** (-jnp.arange(0, QK_ROPE, 2, dtype=F32) / QK_ROPE)
    no_nope = jnp.zeros((1, QK_NOPE), F32)
    inv_tile = _head_tile(no_nope, jnp.concatenate([inv, inv])[None])
    sign_tile = _head_tile(no_nope, jnp.concatenate([-jnp.ones_like(inv), jnp.ones_like(inv)])[None])
    ang = positions[0].astype(F32)[:, None] * inv_tile
    cos_t, sin_t = jnp.cos(ang), jnp.sin(ang) * sign_tile

    xr, yg, cq, ckv, krp, hb_in = _in_proj(x2d, g_attn, w_in_p, tm_wide)
    h_f, *saved_f = _lru_scan_fwd(xr, cw[0], pv[0], wa[0], wi[0], False, t_scan)
    h_b, *saved_b = _lru_scan_fwd(xr, cw[1], pv[1], wa[1], wi[1], True, t_scan)
    q, k, v = _mla_qkv(cq, ckv, krp, cos_t, sin_t, g_qa, g_kva, g_qn, g_kn, w_uq_p, w_uk_p, w_uv, tm_wide)
    o, lse, *got = _attn_fwd(q, k, v, tq_f, tk, shards=[local[name].astype(BF16) for name, _, _ in late])
    take_gathered(late, got)
    x1, mixed = _mix_out(h_f, h_b, yg, o, x2d, g_lru, g_mla, full["w_out"], tm_wide)
    km, vm = _mem_kv(mem2d, g_mem, full["w_mem_kv"], g_mk)
    x2, o_mem = _mem_attn(x1, g_memattn, full["w_mem_q"], g_mq, km, vm, full["w_mem_o"], tm_wide)
    gu_pre, hb_ffn = _ffn_up(x2, g_ffn, full["w_up"], tm)
    dy, dyb, act, dgu, loss_acc = _ffn_down_loss(gu_pre, x2, tgt, ffn_cw, full["w_down"], tm_ffn)

    grads = {}
    grads["w_down"] = _matmul_tn(act, dyb, "grad_w_down", out_dtype=BF16)
    dpre, g_conv = _ffn_bwd_conv(dgu, gu_pre, ffn_cw, tm_ffn)
    grads["ffn_conv_w"], grads["ffn_conv_b"] = g_conv[:3], g_conv[3]
    grads["w_up"] = _matmul_tn(hb_ffn, dpre, "grad_w_up", col_shards=True, out_dtype=BF16)
    dx2, dx2b, gg = _ffn_bwd_in(dpre, x2, dy, g_ffn, full["w_up"], tm)
    grads["ffn_norm"] = gg[0]
    grads["w_mem_o"] = _matmul_tn(o_mem, dx2b, "grad_w_mem_o", out_dtype=BF16)
    dx1, dx1b, hm, dqr_mem, dkm, dvm, gg, ggq = _mem_attn_bwd(x1, dx2, dx2b, g_memattn, full["w_mem_q"], g_mq, km, vm,
                                                                 full["w_mem_o"], tm)
    grads["mem_attn_norm"], grads["mem_q_norm"] = gg[0], ggq[0]
    grads["w_mem_q"] = _matmul_tn(hm, dqr_mem, "grad_w_mem_q", out_dtype=BF16)
    g_mem_kv, gg, ggk, _ = _mem_kv_bwd(mem2d, g_mem, full["w_mem_kv"], g_mk, dkm, dvm)
    grads["w_mem_kv"] = g_mem_kv.astype(BF16)
    grads["mem_norm"], grads["mem_k_norm"] = gg[0], ggk[0]
    grads["w_out"] = _matmul_tn(mixed, dx1b, "grad_w_out", out_dtype=BF16)
    dh, dyg, dob, dl128, ggl, ggm = _mix_out_bwd(dx1b, h_f, h_b, yg, o, g_lru, g_mla, full["w_out"], tm)
    grads["lru_out_norm"], grads["mla_out_norm"] = ggl[0], ggm[0]
    delta_t = jnp.transpose(dl128[:, :HEADS]).reshape(HEADS // 2, 2, s)
    def halves(name, shape, axis):
        g4 = grads[name] if grads[name].ndim == 3 else _grad_shards(grads[name], shape, axis)
        return g4.reshape(N_CHIPS, 2, g4.shape[1] // 2, g4.shape[2])

    dq, dk, dv, *arrived_late = _attn_bwd(q, k, v, dob, lse, delta_t, tq_b, tk,
                                          contributions=[halves(*e) for e in late])
    (dcq, dckv, dkrp, cqb, dqr, ckvb, dkn, dvb, ggqa, ggkva, ggqn, ggkn) = _mla_qkv_bwd(
        cq, ckv, krp, cos_t, sin_t, dq, dk, dv, g_qa, g_kva, g_qn, g_kn, w_uq_p, w_uk_p, w_uv, tm_wide)
    grads["q_a_norm"], grads["kv_a_norm"] = ggqa[0], ggkva[0]
    grads["mla_q_norm"] = jnp.concatenate(_from_head_tile(ggqn[0]))
    grads["mla_k_norm"] = jnp.concatenate(_from_head_tile(ggkn[0]))
    g_uq_p = _matmul_tn(cqb, dqr, "grad_w_uq")
    grads["w_uq"] = jnp.concatenate(_from_head_tile(g_uq_p.reshape(Q_LORA, HEADS, LANES)), axis=-1).reshape(Q_LORA, -1)
    g_uk_p = _from_head_tile(_matmul_tn(ckvb, dkn, "grad_w_uk").reshape(KV_LORA, HEADS, LANES))[0]
    g_uv = _matmul_tn(ckvb, dvb, "grad_w_uv").reshape(KV_LORA, HEADS, V_DIM)
    grads["w_ukv"] = jnp.concatenate([g_uk_p, g_uv], axis=2).reshape(KV_LORA, -1)
    dxr, gwa, gwi, gvec = [], [], [], []
    for d, (hd, saved) in enumerate(((h_f, saved_f), (h_b, saved_b))):
        r = _lru_scan_bwd(xr, saved, hd, dh, cw[d], pv[d], wa[d], wi[d], d == 1, t_scan)
        dxr.append(r[0])
        gwa.append(_block_diag_t(r[1]))
        gwi.append(_block_diag_t(r[2]))
        gvec.append(r[3])
    grads["lru_w_a"], grads["lru_w_i"] = jnp.stack(gwa), jnp.stack(gwi)
    grads["lru_conv_w"] = jnp.stack([gv[:CONV_W] for gv in gvec])
    for r_i, name in ((4, "lru_conv_b"), (5, "lru_b_a"), (6, "lru_b_i"), (7, "lru_lambda")):
        grads[name] = jnp.stack([gv[r_i] for gv in gvec])
    grad_x, dproj, gg = _in_proj_bwd(x2d, dx1, dxr[0], dxr[1], dyg, dcq, dckv, dkrp, g_attn, w_in_p, tm)
    grads["attn_norm"] = gg[0]
    g_in_p = _matmul_tn(hb_in, dproj, "grad_w_in")
    grads["w_in"] = jnp.concatenate([g_in_p[:, :OFF_KR], _from_head_tile(g_in_p[:, OFF_KR:])[1]], axis=1)

    small = _pack_small_grads(grads, loss_acc[0, 0] * (0.5 / D_MODEL))
    length = small.shape[1]
    hrows = _round_up(-(-length // (2 * LANES)), 16)
    small = jnp.pad(small, ((0, 0), (0, 2 * hrows * LANES - length))).reshape(N_CHIPS, 2, hrows, LANES)
    for name, _, _ in early:
        grads[name] = grads[name].astype(BF16)
    arrived_early = _to_owner([halves(*e) for e in early] + [small], "grad_to_owner")
    names = [name for name, _, _ in late + early] + ["small"]
    c_idx = lax.axis_index("c").astype(jnp.int32).reshape(1)
    reduced = _join_halves([_sum_devices(b, c_idx, "grad_sum_" + n)
                            for n, b in zip(names, list(arrived_late) + list(arrived_early))])

    outs = [{}, {}, {}, {}]
    for (name, shape, axis), r in zip(late + early, reduced):
        g2 = r.reshape(_shard_shape(shape, axis))
        res = _adamw(local[name], g2, given["m_" + name][0], given["v_" + name][0], "adamw_" + name)
        for o_, a in zip(outs, (g2, *res)):
            o_[name] = a[None]
    pack = lambda prefix: _pad_rows(_pack_small_local({n: given[prefix + n] for n in WEIGHT_ORDER}), 2 * hrows)
    g_small = reduced[-1].reshape(2 * hrows, LANES)
    res = _adamw(pack(""), g_small, pack("m_"), pack("v_"), "adamw_small")
    for o_, a in zip(outs, (g_small, *res)):
        o_.update(_unpack_small_local(a.reshape(-1)))
    loss = g_small.reshape(-1)[length - 1]
    return (loss, grad_x[None], *[o_[n] for o_ in outs for n in WEIGHT_ORDER])
```

```python
import jax
import jax.numpy as jnp
from jax import lax
from jax.experimental import pallas as pl
from jax.experimental.pallas import tpu as pltpu

F32, BF16 = jnp.float32, jnp.bfloat16
MESH = pl.DeviceIdType.MESH

D_MODEL = 1024
EPS = 1e-6
LRU_W = 512
LRU_BLOCKS = 8
LRU_C = 8.0
CONV_W = 4
HEADS = 8
QK_NOPE, QK_ROPE, QK_HEAD, V_DIM = 64, 32, 96, 64
Q_LORA, KV_LORA = 256, 128
MLA_W = HEADS * V_DIM
ROPE_THETA = 10000.0
IN_COLS = 2 * LRU_W + Q_LORA + KV_LORA + QK_ROPE
OFF_KR = IN_COLS - QK_ROPE
IN_PAD = 1536
MEM_HEADS, MEM_HD = 4, 128
MEM_W = MEM_HEADS * MEM_HD
D_FF = 2816
N_CHIPS = 4
ADAM_LR, ADAM_B1, ADAM_B2, ADAM_EPS, ADAM_WD, ADAM_STEP = 0.001, 0.9, 0.999, 1e-08, 0.01, 10

LANES = 128
SUBLANES = 8
V7X_VMEM_BYTES = 64 * 1024 * 1024
VMEM_LIMIT = V7X_VMEM_BYTES * 7 // 8

SHARDED = (
    ("w_in", (D_MODEL, IN_COLS), 1, True),
    ("lru_conv_w", (2, CONV_W, LRU_W), 2, False),
    ("lru_conv_b", (2, LRU_W), 1, False),
    ("lru_b_a", (2, LRU_W), 1, False),
    ("lru_b_i", (2, LRU_W), 1, False),
    ("lru_lambda", (2, LRU_W), 1, False),
    ("w_uq", (Q_LORA, HEADS * QK_HEAD), 1, True),
    ("w_ukv", (KV_LORA, HEADS * (QK_NOPE + V_DIM)), 1, True),
    ("w_out", (2 * LRU_W, D_MODEL), 0, True),
    ("w_mem_q", (D_MODEL, MEM_W), 0, True),
    ("w_mem_kv", (D_MODEL, 2 * MEM_W), 0, True),
    ("w_mem_o", (MEM_W, D_MODEL), 1, True),
    ("w_up", (D_MODEL, 2 * D_FF), 1, True),
    ("ffn_conv_w", (3, 2 * D_FF), 1, False),
    ("w_down", (D_FF, D_MODEL), 0, True),
)
REPLICATED = (
    ("attn_norm", (D_MODEL,)), ("lru_w_a", (2, LRU_BLOCKS, 64, 64)), ("lru_w_i", (2, LRU_BLOCKS, 64, 64)),
    ("q_a_norm", (Q_LORA,)), ("kv_a_norm", (KV_LORA,)), ("mla_q_norm", (QK_HEAD,)), ("mla_k_norm", (QK_HEAD,)),
    ("lru_out_norm", (LRU_W,)), ("mla_out_norm", (MLA_W,)), ("mem_attn_norm", (D_MODEL,)), ("mem_norm", (D_MODEL,)),
    ("mem_q_norm", (MEM_HD,)), ("mem_k_norm", (MEM_HD,)), ("ffn_norm", (D_MODEL,)), ("ffn_conv_b", (2 * D_FF,)),
)
WEIGHT_ORDER = ('attn_norm', 'w_in', 'lru_conv_w', 'lru_conv_b', 'lru_w_a', 'lru_b_a', 'lru_w_i', 'lru_b_i', 'lru_lambda',
                'q_a_norm', 'w_uq', 'kv_a_norm', 'w_ukv', 'mla_q_norm', 'mla_k_norm', 'lru_out_norm', 'mla_out_norm', 'w_out',
                'mem_attn_norm', 'mem_norm', 'w_mem_q', 'w_mem_kv', 'mem_q_norm', 'mem_k_norm', 'w_mem_o', 'ffn_norm', 'w_up',
                'ffn_conv_w', 'ffn_conv_b', 'w_down')


def _numel(shape):
    n = 1
    for s in shape:
        n *= s
    return n


def _cparams(n_axes):
    return pltpu.CompilerParams(dimension_semantics=("arbitrary",) * n_axes, vmem_limit_bytes=VMEM_LIMIT)


def _bdot(a, b):
    return jnp.dot(a.astype(BF16), b.astype(BF16), preferred_element_type=F32)


def _bdot_nt(a, b):
    return lax.dot_general(a.astype(BF16), b.astype(BF16), (((1,), (1,)), ((), ())), preferred_element_type=F32)


def _bdot_tn(a, b):
    return lax.dot_general(a.astype(BF16), b.astype(BF16), (((0,), (0,)), ((), ())), preferred_element_type=F32)


def _rstd(x, n=None):
    n = x.shape[-1] if n is None else n
    return lax.rsqrt(jnp.sum(x * x, axis=-1, keepdims=True) * (1.0 / n) + EPS)


def _norm_bwd(x, rs, g, dy, n=None):
    n = x.shape[-1] if n is None else n
    xhat = x * rs
    dxh = dy * g
    dx = rs * (dxh - xhat * (jnp.sum(dxh * xhat, axis=-1, keepdims=True) * (1.0 / n)))
    return dx, dy * xhat


def _acc_row(ref, r, val):
    ref[r:r + 1, :] += jnp.sum(val, axis=0, keepdims=True)


def _zero_first(i, *refs):
    @pl.when(i == 0)
    def _():
        for r in refs:
            r[...] = jnp.zeros_like(r)


def _shift_down(x, j, halo):
    if j == 0:
        return x
    xs = pltpu.roll(x, j, 0)
    hs = pltpu.roll(halo, j, 0)
    row = lax.broadcasted_iota(jnp.int32, hs.shape, 0)
    top = jnp.where(row < j, hs, xs[:SUBLANES])
    return jnp.concatenate([top, xs[SUBLANES:]], axis=0)


def _shift_up(x, j, halo):
    if j == 0:
        return x
    t = x.shape[0]
    xs = pltpu.roll(x, t - j, 0)
    hs = pltpu.roll(halo, SUBLANES - j, 0)
    row = lax.broadcasted_iota(jnp.int32, hs.shape, 0)
    bot = jnp.where(row >= SUBLANES - j, hs, xs[t - SUBLANES:])
    return jnp.concatenate([xs[:t - SUBLANES], bot], axis=0)


def _shift(x, j, halo, down):
    return _shift_down(x, j, halo) if down else _shift_up(x, j, halo)


def _scan(a, b, h_in, down):
    t, c = a.shape
    g = t // SUBLANES
    a3, b3 = a.reshape(g, SUBLANES, c), b.reshape(g, SUBLANES, c)
    sub = lax.broadcasted_iota(jnp.int32, a3.shape, 1)
    d = 1
    while d < SUBLANES:
        keep = (sub >= d) if down else (sub < SUBLANES - d)
        shift = d if down else SUBLANES - d
        a_s = jnp.where(keep, pltpu.roll(a3, shift, 1), 1.0)
        b_s = jnp.where(keep, pltpu.roll(b3, shift, 1), 0.0)
        b3 = a3 * b_s + b3
        a3 = a3 * a_s
        d *= 2
    hs = [None] * g
    carry = h_in
    for i in (range(g) if down else range(g - 1, -1, -1)):
        hs[i] = a3[i] * carry + b3[i]
        carry = hs[i][SUBLANES - 1:, :] if down else hs[i][:1, :]
    return jnp.concatenate(hs, axis=0)


def _sigmoid(x):
    return 0.5 * jnp.tanh(0.5 * x) + 0.5


LOG2E = 1.4426950408889634
GELU_K = 0.7978845608028654
GELU_C = 0.044715


def _gelu(x):
    return 0.5 * x * (1.0 + jnp.tanh(GELU_K * (x + GELU_C * x * x * x)))


def _gelu_grad(x):
    t = jnp.tanh(GELU_K * (x + GELU_C * x * x * x))
    return 0.5 * (1.0 + t) + 0.5 * x * (1.0 - t * t) * GELU_K * (1.0 + 3.0 * GELU_C * x * x)


ROPE_HALF = QK_ROPE // 2
ROPE_LANE = 32


def _head_tile(nope, rope):
    z = lambda n: jnp.zeros(nope.shape[:-1] + (n,), nope.dtype)
    r1, r2 = (z(ROPE_HALF), z(ROPE_HALF)) if rope is None else (rope[..., :ROPE_HALF], rope[..., ROPE_HALF:])
    return jnp.concatenate([nope[..., :ROPE_LANE], r1, nope[..., ROPE_LANE:], z(ROPE_HALF), r2, z(ROPE_HALF)], axis=-1)


def _from_head_tile(t):
    a, b = ROPE_LANE + ROPE_HALF, ROPE_LANE + LANES // 2
    return (jnp.concatenate([t[..., :ROPE_LANE], t[..., a:a + QK_NOPE - ROPE_LANE]], axis=-1),
            jnp.concatenate([t[..., ROPE_LANE:a], t[..., b:b + ROPE_HALF]], axis=-1))


def _rope_partner(x):
    lane = lax.broadcasted_iota(jnp.int32, x.shape, 1) & (LANES // 2 - 1)
    return jnp.where((lane >= ROPE_LANE) & (lane < ROPE_LANE + ROPE_HALF), pltpu.roll(x, LANES // 2, 1), 0.0)


def _rope(x, cos_t, sin_t):
    return x * cos_t + _rope_partner(x) * sin_t


def _rope_t(dy, cos_t, sin_t):
    return dy * cos_t + _rope_partner(dy * sin_t)


def _rowwise(body, name, s, tm, rows=(), halos=(), fulls=(), outs=(), accs=()):
    n = s // tm
    in_specs, args = [], []
    for a in rows:
        in_specs.append(pl.BlockSpec((tm, a.shape[1]), lambda i: (i, 0)))
        args.append(a)
    for a in halos:
        hr = 2 * SUBLANES if a.dtype == BF16 else SUBLANES
        in_specs.append(pl.BlockSpec((hr, a.shape[1]), lambda i, hr=hr: (jnp.maximum(i * (tm // hr) - 1, 0), 0)))
        in_specs.append(pl.BlockSpec((hr, a.shape[1]), lambda i, hr=hr: (jnp.minimum((i + 1) * (tm // hr), s // hr - 1), 0)))
        args += [a, a]
    for a in fulls:
        in_specs.append(pl.BlockSpec(a.shape, lambda i, nd=a.ndim: (0,) * nd))
        args.append(a)
    out_shape, out_specs = [], []
    for c, dt in outs:
        out_shape.append(jax.ShapeDtypeStruct((s, c), dt))
        out_specs.append(pl.BlockSpec((tm, c), lambda i: (i, 0)))
    for shp, dt in accs:
        out_shape.append(jax.ShapeDtypeStruct(shp, dt))
        out_specs.append(pl.BlockSpec(shp, lambda i, nd=len(shp): (0,) * nd))

    def kern(*refs):
        body(pl.program_id(0), n, *refs)

    return pl.pallas_call(kern, grid=(n,), in_specs=in_specs, out_specs=out_specs, out_shape=out_shape, name=name,
                          compiler_params=_cparams(1))(*args)


def _matmul_tn(a, b, name, col_shards=False, out_dtype=F32):
    t, m = a.shape
    n = b.shape[1]
    bm = m
    for cand in range(LANES, m + 1, LANES):
        if m % cand == 0 and cand * (n // N_CHIPS if col_shards else min(n, 2048)) * 4 <= 6 * 1024 * 1024:
            bm = cand
    bn = n // N_CHIPS if col_shards else (n if n <= 2048 else 1408)
    bt = min(t, 2048)
    nt = t // bt

    def kern(a_ref, b_ref, o_ref, acc_ref):
        k = pl.program_id(2)

        @pl.when(k == 0)
        def _():
            acc_ref[...] = jnp.zeros_like(acc_ref)
        acc_ref[...] += _bdot_tn(a_ref[...], b_ref[...])

        @pl.when(k == nt - 1)
        def _():
            o_ref[...] = acc_ref[...].astype(out_dtype)

    if col_shards:
        out_spec = pl.BlockSpec((None, bm, bn), lambda i, j, k: (j, i, 0))
        out_shape = jax.ShapeDtypeStruct((N_CHIPS, m, bn), out_dtype)
    else:
        out_spec = pl.BlockSpec((bm, bn), lambda i, j, k: (i, j))
        out_shape = jax.ShapeDtypeStruct((m, n), out_dtype)
    return pl.pallas_call(
        kern, grid=(m // bm, n // bn, nt),
        in_specs=[pl.BlockSpec((bt, bm), lambda i, j, k: (k, i)), pl.BlockSpec((bt, bn), lambda i, j, k: (k, j))],
        out_specs=out_spec, out_shape=out_shape, scratch_shapes=[pltpu.VMEM((bm, bn), F32)], name=name,
        compiler_params=_cparams(3))(a, b)


def _in_proj(x, g, w_in_p, tm):
    def body(i, n, x_ref, g_ref, w_ref, xr, yg, cq, ckv, krp, hb):
        xv = x_ref[...]
        h = (xv * _rstd(xv) * g_ref[...]).astype(BF16)
        hb[...] = h
        p = jnp.dot(h, w_ref[...], preferred_element_type=F32)
        xr[...] = p[:, :LRU_W]
        yg[...] = p[:, LRU_W:2 * LRU_W]
        cq[...] = p[:, 2 * LRU_W:2 * LRU_W + Q_LORA]
        ckv[...] = p[:, 2 * LRU_W + Q_LORA:OFF_KR]
        krp[...] = p[:, OFF_KR:IN_PAD]

    return _rowwise(body, "in_proj", x.shape[0], tm, rows=[x], fulls=[g, w_in_p],
                    outs=[(LRU_W, F32), (LRU_W, F32), (Q_LORA, F32), (KV_LORA, F32), (LANES, F32), (D_MODEL, BF16)])


def _softplus_neg(lam):
    e = jnp.exp(-jnp.abs(lam))
    return jnp.maximum(-lam, 0.0) + jnp.where(e < 1e-2, e * (1.0 - e * (0.5 - e * (1.0 / 3.0))), jnp.log(1.0 + e))


def _lru_gates(x, halo, cw_ref, pv_ref, wa_ref, wi_ref, rev):
    down = not rev
    xc = pv_ref[0:1, :] + jnp.zeros_like(x)
    for j in range(CONV_W):
        k = j if rev else CONV_W - 1 - j
        xc = xc + cw_ref[k:k + 1, :] * _shift(x, j, halo, down)
    r = _sigmoid(_bdot(xc, wa_ref[...]) + pv_ref[1:2, :])
    ig = _sigmoid(_bdot(xc, wi_ref[...]) + pv_ref[2:3, :])
    lam = pv_ref[3:4, :]
    sp = _softplus_neg(lam)
    log_a = (-LRU_C) * r * sp
    a = jnp.exp(log_a)
    z = 2.0 * log_a
    series = -(z * (1.0 + z * (0.5 + z * (1.0 / 6.0 + z * (1.0 / 24.0)))))
    om = jnp.where(z > -0.02, series, 1.0 - a * a)
    mult = jnp.sqrt(om)
    return xc, r, ig, sp, a, mult


def _lru_scan_fwd(xr, cw, pv, wa, wi, rev, t):
    s = xr.shape[0]
    n = s // t
    hb = t // SUBLANES
    last8 = s // SUBLANES - 1
    down = not rev

    def kern(x_ref, halo_ref, cw_ref, pv_ref, wa_ref, wi_ref, h_ref, xc_ref, r_ref, ig_ref, a_ref, mult_ref, carry_ref):
        i = pl.program_id(0)
        _zero_first(i, carry_ref)
        halo = jnp.where(i == 0, 0.0, halo_ref[...])
        xc, r, ig, sp, a, mult = _lru_gates(x_ref[...], halo, cw_ref, pv_ref, wa_ref, wi_ref, rev)
        xc_ref[...], r_ref[...], ig_ref[...], a_ref[...], mult_ref[...] = xc, r, ig, a, mult
        h_ref[...] = _scan(a, mult * ig * xc, carry_ref[...], down)
        carry_ref[...] = h_ref[pl.ds(t - 1 if down else 0, 1), :]

    if rev:
        blk = lambda i: (n - 1 - i, 0)
        hal = lambda i: (jnp.minimum((n - i) * hb, last8), 0)
    else:
        blk = lambda i: (i, 0)
        hal = lambda i: (jnp.maximum(i * hb - 1, 0), 0)
    full = lambda a: pl.BlockSpec(a.shape, lambda i: (0, 0))
    return pl.pallas_call(
        kern, grid=(n,),
        in_specs=[pl.BlockSpec((t, LRU_W), blk), pl.BlockSpec((SUBLANES, LRU_W), hal), full(cw), full(pv), full(wa), full(wi)],
        out_specs=[pl.BlockSpec((t, LRU_W), blk)] * 6, out_shape=[jax.ShapeDtypeStruct((s, LRU_W), F32)] * 6,
        scratch_shapes=[pltpu.VMEM((1, LRU_W), F32)], name="lru_scan_rev" if rev else "lru_scan_fwd",
        compiler_params=_cparams(1))(xr, xr, cw, pv, wa, wi)


def _lru_scan_bwd(xr, saved, h, dh, cw, pv, wa, wi, rev, t):
    s = xr.shape[0]
    n = s // t
    hb = t // SUBLANES
    last8 = s // SUBLANES - 1
    down = not rev

    def kern(x_ref, xc_ref, r_ref, ig_ref, a_ref, mult_ref, h_ref, hh_ref, dh_ref, cw_ref, pv_ref, wa_ref, wi_ref,
             dx_ref, gwa_ref, gwi_ref, gv_ref, p_ref, dxc_halo_ref, tmp_ref):
        i = pl.program_id(0)
        _zero_first(i, gwa_ref, gwi_ref, gv_ref, p_ref, dxc_halo_ref)
        at_start = i == n - 1
        x = x_ref[...]
        hhalo = jnp.where(at_start, 0.0, hh_ref[...])
        xc, r, ig, a, mult = xc_ref[...], r_ref[...], ig_ref[...], a_ref[...], mult_ref[...]
        lam = pv_ref[3:4, :]
        sp = _softplus_neg(lam)
        h_prev = _shift(h_ref[...], 1, hhalo, down)
        row = lax.broadcasted_iota(jnp.int32, x.shape, 0)
        edge = t - 1 if down else 0
        dh_mod = dh_ref[...] + jnp.where(row == edge, p_ref[...], 0.0)
        a_next = _shift(a, 1, jnp.zeros((SUBLANES, LRU_W), F32), not down)
        g = _scan(a_next, dh_mod, jnp.zeros((1, LRU_W), F32), not down)
        tmp_ref[...] = a * g
        p_ref[...] = tmp_ref[pl.ds(0 if down else t - 1, 1), :]
        da = g * h_prev
        d_ig = g * mult * xc
        d_xc = g * mult * ig
        d_om = g * ig * xc * (0.5 / jnp.maximum(mult, 1e-30))
        d_log_a = da * a - 2.0 * d_om * a * a
        d_r = d_log_a * ((-LRU_C) * sp)
        d_sp = jnp.sum(d_log_a * ((-LRU_C) * r), axis=0, keepdims=True)
        gv_ref[7:8, :] += d_sp * (-_sigmoid(-lam))
        d_ga = d_r * r * (1.0 - r)
        d_gi = d_ig * ig * (1.0 - ig)
        _acc_row(gv_ref, 5, d_ga)
        _acc_row(gv_ref, 6, d_gi)
        d_xc = d_xc + _bdot_nt(d_ga, wa_ref[...]) + _bdot_nt(d_gi, wi_ref[...])
        gwa_ref[...] += _bdot_tn(xc, d_ga)
        gwi_ref[...] += _bdot_tn(xc, d_gi)
        _acc_row(gv_ref, 4, d_xc)
        dx = jnp.zeros_like(x)
        dxc_halo = dxc_halo_ref[...]
        for j in range(CONV_W):
            k = j if rev else CONV_W - 1 - j
            d_shift = _shift(d_xc, j, dxc_halo, not down)
            _acc_row(gv_ref, k, d_shift * x)
            dx = dx + cw_ref[k:k + 1, :] * d_shift
        dx_ref[...] = dx.astype(BF16)
        dxc_halo_ref[...] = d_xc[:SUBLANES] if down else d_xc[t - SUBLANES:]

    if rev:
        blk = lambda i: (i, 0)
        hal = lambda i: (jnp.minimum((i + 1) * hb, last8), 0)
    else:
        blk = lambda i: (n - 1 - i, 0)
        hal = lambda i: (jnp.maximum((n - 1 - i) * hb - 1, 0), 0)
    full = lambda a: pl.BlockSpec(a.shape, lambda i: (0, 0))
    bs = pl.BlockSpec((t, LRU_W), blk)
    hs = pl.BlockSpec((SUBLANES, LRU_W), hal)
    return pl.pallas_call(
        kern, grid=(n,),
        in_specs=[bs] * 7 + [hs, bs, full(cw), full(pv), full(wa), full(wi)],
        out_specs=[bs, pl.BlockSpec((LRU_W, LRU_W), lambda i: (0, 0)), pl.BlockSpec((LRU_W, LRU_W), lambda i: (0, 0)),
                   pl.BlockSpec((SUBLANES, LRU_W), lambda i: (0, 0))],
        out_shape=[jax.ShapeDtypeStruct((s, LRU_W), BF16), jax.ShapeDtypeStruct((LRU_W, LRU_W), F32),
                   jax.ShapeDtypeStruct((LRU_W, LRU_W), F32), jax.ShapeDtypeStruct((SUBLANES, LRU_W), F32)],
        scratch_shapes=[pltpu.VMEM((1, LRU_W), F32), pltpu.VMEM((SUBLANES, LRU_W), F32), pltpu.VMEM((t, LRU_W), F32)],
        name="lru_bwd_rev" if rev else "lru_bwd_fwd", compiler_params=_cparams(1))(xr, *saved, h, h, dh, cw, pv, wa, wi)


def _mla_qkv(cq, ckv, krp, cos_t, sin_t, g_qa, g_kva, g_qn, g_kn, w_uq_p, w_uk_p, w_uv, tm):
    scale = QK_HEAD ** -0.5 * LOG2E

    def body(i, n, cq_ref, ckv_ref, kr_ref, c_ref, s_ref, gqa, gkva, gqn, gkn, wq, wk, wv, q_out, k_out, v_out):
        cosv, sinv = c_ref[...], s_ref[...]
        cqv = cq_ref[...]
        qr = _bdot(cqv * _rstd(cqv) * gqa[...], wq[...])
        ckvv = ckv_ref[...]
        c_kv = (ckvv * _rstd(ckvv) * gkva[...]).astype(BF16)
        kn = jnp.dot(c_kv, wk[...], preferred_element_type=F32)
        v_out[...] = jnp.dot(c_kv, wv[...], preferred_element_type=F32).astype(BF16)
        kr = kr_ref[...]
        kr_swapped = _rope_partner(kr * gkn[...]) * sinv
        for h in range(HEADS):
            sl = slice(h * LANES, (h + 1) * LANES)
            qh = qr[:, sl]
            qh = _rope(qh * _rstd(qh, QK_HEAD) * gqn[...], cosv, sinv) * scale
            q_out[:, sl] = qh.astype(BF16)
            kh = kn[:, sl] + kr
            rs = _rstd(kh, QK_HEAD)
            k_out[:, sl] = (kh * rs * gkn[...] * cosv + kr_swapped * rs).astype(BF16)

    return _rowwise(body, "mla_qkv", cq.shape[0], tm, rows=[cq, ckv, krp, cos_t, sin_t],
                    fulls=[g_qa, g_kva, g_qn, g_kn, w_uq_p, w_uk_p, w_uv],
                    outs=[(HEADS * LANES, BF16), (HEADS * LANES, BF16), (MLA_W, BF16)])


NT_DIMS = (((1,), (1,)), ((), ()))
TN_DIMS = (((0,), (0,)), ((), ()))


def _riding_exchange(copies_fn, first, last):
    @pl.when(first)
    def _():
        for cp in copies_fn():
            cp.start()

    def finish():
        @pl.when(last)
        def _():
            for cp in copies_fn():
                cp.wait()
    return finish


def _attn_fwd(q, k, v, tq, tk, shards=()):
    s = q.shape[0]
    nq, nk = s // tq, s // tk
    n = len(shards)

    def kern(*refs):
        q_ref, k_ref, v_ref = refs[:3]
        o_ref, lse_ref = refs[3 + n:5 + n]
        acc_ref = refs[5 + 2 * n]
        p_id, i_id = pl.program_id(0), pl.program_id(1)
        finish = _riding_exchange(lambda: _gather_copies(refs[3:3 + n], refs[5 + n:5 + 2 * n], *refs[6 + 2 * n:]),
                                  (p_id == 0) & (i_id == 0), (p_id == HEADS // 2 - 1) & (i_id == nq - 1)) if n else None
        qs = (q_ref[:, :LANES], q_ref[:, LANES:])
        acc_ref[...] = jnp.zeros_like(acc_ref)

        def step(j, carry):
            off = pl.multiple_of(j * tk, tk)
            vc = v_ref[pl.ds(off, tk), :]
            out = []
            for h in range(2):
                m, l = carry[2 * h:2 * h + 2]
                st = lax.dot_general(k_ref[pl.ds(off, tk), h * LANES:(h + 1) * LANES], qs[h], NT_DIMS,
                                     preferred_element_type=F32)
                mn = jnp.maximum(m, jnp.max(st, axis=0, keepdims=True))
                al = jnp.exp2(m - mn)
                pt = jnp.exp2(st - mn)
                l = al * l + jnp.sum(pt, axis=0, keepdims=True)
                acc_ref[h] = al * acc_ref[h] + lax.dot_general(vc, pt.astype(BF16), TN_DIMS, preferred_element_type=F32)
                out += [mn, l]
            return tuple(out)

        init = (jnp.full((1, tq), -1e30, F32), jnp.zeros((1, tq), F32)) * 2
        m0, l0, m1, l1 = lax.fori_loop(0, nk, step, init)
        row = lax.broadcasted_iota(jnp.int32, (LANES, tq), 0)
        o_ref[...] = jnp.where(row < V_DIM, acc_ref[0] / l0, acc_ref[1] / l1).T
        lse_ref[0, 0:1, :] = m0 + jnp.log2(l0)
        lse_ref[0, 1:2, :] = m1 + jnp.log2(l1)
        if n:
            finish()

    return pl.pallas_call(
        kern, grid=(HEADS // 2, nq),
        in_specs=[pl.BlockSpec((tq, 2 * LANES), lambda p, i: (i, p)), pl.BlockSpec((s, 2 * LANES), lambda p, i: (0, p)),
                  pl.BlockSpec((s, LANES), lambda p, i: (0, p))] + [ANY] * n,
        out_specs=[pl.BlockSpec((tq, LANES), lambda p, i: (i, p)), pl.BlockSpec((1, 2, tq), lambda p, i: (p, 0, i))]
        + [ANY] * n,
        out_shape=[jax.ShapeDtypeStruct((s, MLA_W), F32), jax.ShapeDtypeStruct((HEADS // 2, 2, s), F32)]
        + _gather_shapes(shards),
        scratch_shapes=[pltpu.VMEM((2, LANES, tq), F32)] + (_gather_sems(n) if n else []),
        name="attn_fwd", compiler_params=_cparams(2))(q, k, v, *shards)


def _attn_bwd(q, k, v, do, lse, delta, tq, tk, contributions=()):
    s = q.shape[0]
    nq, nk = s // tq, s // tk
    n = len(contributions)

    def kern(*refs):
        q_ref, do_ref, lse_ref, dl_ref, k_ref, v_ref = refs[:6]
        dq_ref, dk_ref, dv_ref = refs[6 + n:9 + n]
        acc_ref = refs[9 + 2 * n]
        p_id, i_id = pl.program_id(0), pl.program_id(1)
        finish = _riding_exchange(lambda: _to_owner_copies(refs[6:6 + n], refs[9 + n:9 + 2 * n], *refs[10 + 2 * n:]),
                                  (p_id == 0) & (i_id == 0), (p_id == HEADS // 2 - 1) & (i_id == nq - 1)) if n else None
        _zero_first(pl.program_id(1), dk_ref, dv_ref)
        acc_ref[...] = jnp.zeros_like(acc_ref)
        qs = (q_ref[:, :LANES], q_ref[:, LANES:])
        doc = do_ref[...]
        lane_q = lax.broadcasted_iota(jnp.int32, (tq, LANES), 1)
        zq = jnp.zeros_like(doc)
        dos = (jnp.where(lane_q < V_DIM, doc, zq), jnp.where(lane_q >= V_DIM, doc, zq))
        lses = (lse_ref[0, 0:1, :], lse_ref[0, 1:2, :])
        dls = (dl_ref[0, 0:1, :], dl_ref[0, 1:2, :])

        def step(j, carry):
            off = pl.multiple_of(j * tk, tk)
            vp = v_ref[pl.ds(off, tk), :]
            lane_k = lax.broadcasted_iota(jnp.int32, (tk, LANES), 1)
            zero = jnp.zeros_like(vp)
            vs = (jnp.where(lane_k < V_DIM, vp, zero), jnp.where(lane_k >= V_DIM, vp, zero))
            for h in range(2):
                sl = slice(h * LANES, (h + 1) * LANES)
                st = lax.dot_general(k_ref[pl.ds(off, tk), sl], qs[h], NT_DIMS, preferred_element_type=F32)
                pt = jnp.exp2(st - lses[h])
                dpt = lax.dot_general(vs[h], doc, NT_DIMS, preferred_element_type=F32)
                dst = (pt * (dpt - dls[h])).astype(BF16)
                dv_ref[pl.ds(off, tk), :] += jnp.dot(pt.astype(BF16), dos[h], preferred_element_type=F32)
                dk_ref[pl.ds(off, tk), sl] += jnp.dot(dst, qs[h], preferred_element_type=F32)
                acc_ref[h] += lax.dot_general(k_ref[pl.ds(off, tk), sl], dst, TN_DIMS, preferred_element_type=F32)
            return carry

        lax.fori_loop(0, nk, step, 0)
        dq_ref[:, :LANES] = acc_ref[0].T
        dq_ref[:, LANES:] = acc_ref[1].T
        if n:
            finish()

    return pl.pallas_call(
        kern, grid=(HEADS // 2, nq),
        in_specs=[pl.BlockSpec((tq, 2 * LANES), lambda p, i: (i, p)), pl.BlockSpec((tq, LANES), lambda p, i: (i, p)),
                  pl.BlockSpec((1, 2, tq), lambda p, i: (p, 0, i)), pl.BlockSpec((1, 2, tq), lambda p, i: (p, 0, i)),
                  pl.BlockSpec((s, 2 * LANES), lambda p, i: (0, p)), pl.BlockSpec((s, LANES), lambda p, i: (0, p))]
        + [ANY] * n,
        out_specs=[pl.BlockSpec((tq, 2 * LANES), lambda p, i: (i, p)), pl.BlockSpec((s, 2 * LANES), lambda p, i: (0, p)),
                   pl.BlockSpec((s, LANES), lambda p, i: (0, p))] + [ANY] * n,
        out_shape=[jax.ShapeDtypeStruct((s, HEADS * LANES), F32), jax.ShapeDtypeStruct((s, HEADS * LANES), F32),
                   jax.ShapeDtypeStruct((s, MLA_W), F32)] + _to_owner_shapes(contributions),
        scratch_shapes=[pltpu.VMEM((2, LANES, tq), F32)] + (_to_owner_sems(n) if n else []),
        name="attn_bwd", compiler_params=_cparams(2))(q, do, lse, delta, k, v, *contributions)


def _mix_out(hf, hb, yg, o, x, g_lru, g_mla, w_out, tm):
    def body(i, n, hf_ref, hb_ref, yg_ref, o_ref, x_ref, gl, gm, w_ref, x1_ref, mix_ref):
        lo = (hf_ref[...] + hb_ref[...]) * _gelu(yg_ref[...])
        ov = o_ref[...]
        mix_ref[:, :LRU_W] = (lo * _rstd(lo) * gl[...]).astype(BF16)
        mix_ref[:, LRU_W:] = (ov * _rstd(ov) * gm[...]).astype(BF16)
        x1_ref[...] = x_ref[...] + jnp.dot(mix_ref[...], w_ref[...], preferred_element_type=F32)

    return _rowwise(body, "mix_out", x.shape[0], tm, rows=[hf, hb, yg, o, x], fulls=[g_lru, g_mla, w_out],
                    outs=[(D_MODEL, F32), (2 * LRU_W, BF16)])


def _mem_kv(mem, g_mem, w_kv, g_k):
    m = mem.shape[0]

    def body(i, n, mem_ref, g_ref, w_ref, gk_ref, km_ref, vm_ref):
        mv = mem_ref[...]
        kv = _bdot(mv * _rstd(mv) * g_ref[...], w_ref[...])
        vm_ref[...] = kv[:, MEM_W:].astype(BF16)
        for h in range(MEM_HEADS):
            sl = slice(h * MEM_HD, (h + 1) * MEM_HD)
            kh = kv[:, sl]
            km_ref[:, sl] = (kh * _rstd(kh) * gk_ref[...]).astype(BF16)

    return _rowwise(body, "mem_kv", m, m, rows=[mem], fulls=[g_mem, w_kv, g_k], outs=[(MEM_W, BF16), (MEM_W, BF16)])


def _mem_attn_core(x1v, g_ref, wq_ref, gq_ref, km_ref, vm_ref):
    scale = MEM_HD ** -0.5
    hm = (x1v * _rstd(x1v) * g_ref[...]).astype(BF16)
    qr = jnp.dot(hm, wq_ref[...], preferred_element_type=F32)
    heads = []
    for h in range(MEM_HEADS):
        sl = slice(h * MEM_HD, (h + 1) * MEM_HD)
        qh = qr[:, sl]
        rs = _rstd(qh)
        qn = (qh * rs * gq_ref[...]).astype(BF16)
        sc = lax.dot_general(qn, km_ref[:, sl], (((1,), (1,)), ((), ())), preferred_element_type=F32) * scale
        e = jnp.exp(sc - jnp.max(sc, axis=-1, keepdims=True))
        p = e / jnp.sum(e, axis=-1, keepdims=True)
        oh = jnp.dot(p.astype(BF16), vm_ref[:, sl], preferred_element_type=F32)
        heads.append((qh, rs, qn, p, oh))
    return hm, heads


def _mem_attn(x1, g, w_q, g_q, km, vm, w_o, tm):
    cs = D_MODEL // N_CHIPS

    def body(i, n, x1_ref, g_ref, wq_ref, gq_ref, km_ref, vm_ref, wo_ref, x2_ref, ob_ref):
        x1v = x1_ref[...]
        _, heads = _mem_attn_core(x1v, g_ref, wq_ref, gq_ref, km_ref, vm_ref)
        for h in range(MEM_HEADS):
            ob_ref[:, h * MEM_HD:(h + 1) * MEM_HD] = heads[h][4].astype(BF16)
        for k in range(N_CHIPS):
            sl = slice(k * cs, (k + 1) * cs)
            x2_ref[:, sl] = x1v[:, sl] + jnp.dot(ob_ref[...], wo_ref[k], preferred_element_type=F32)

    return _rowwise(body, "mem_attn", x1.shape[0], tm, rows=[x1], fulls=[g, w_q, g_q, km, vm, w_o],
                    outs=[(D_MODEL, F32), (MEM_W, BF16)])


def _ffn_up(x2, g, w_up, tm):
    cs = 2 * D_FF // N_CHIPS

    def body(i, n, x_ref, g_ref, w_ref, gu_ref, hb_ref):
        xv = x_ref[...]
        hb_ref[...] = (xv * _rstd(xv) * g_ref[...]).astype(BF16)
        for k in range(N_CHIPS):
            gu_ref[:, k * cs:(k + 1) * cs] = jnp.dot(hb_ref[...], w_ref[k], preferred_element_type=F32)

    return _rowwise(body, "ffn_up", x2.shape[0], tm, rows=[x2], fulls=[g, w_up], outs=[(2 * D_FF, F32), (D_MODEL, BF16)])


def _ffn_conv(gu, prev, nxt, cw_ref, i, n):
    prev = jnp.where(i == 0, 0.0, prev)
    nxt = jnp.where(i == n - 1, 0.0, nxt)
    return (cw_ref[3:4, :] + cw_ref[0:1, :] * _shift_down(gu, 1, prev) + cw_ref[1:2, :] * gu
            + cw_ref[2:3, :] * _shift_up(gu, 1, nxt))


def _ffn_down_loss(gu_pre, x2, target, cw, w_down, tm):
    def body(i, n, gu_ref, x_ref, t_ref, pv_ref, nx_ref, cw_ref, w_ref, dy_ref, dyb_ref, act_ref, dgu_ref, loss_ref):
        _zero_first(i, loss_ref)
        gu = _ffn_conv(gu_ref[...], pv_ref[...], nx_ref[...], cw_ref, i, n)
        g, u = gu[:, :D_FF], gu[:, D_FF:]
        sg = _sigmoid(g)
        a = g * sg
        act_ref[...] = (a * u).astype(BF16)
        y = x_ref[...] + jnp.dot(act_ref[...], w_ref[...], preferred_element_type=F32)
        e = y - t_ref[...]
        loss_ref[...] += jnp.sum(e * e)
        dy = e * (1.0 / D_MODEL)
        dy_ref[...] = dy
        dyb_ref[...] = dy.astype(BF16)
        d_act = lax.dot_general(dyb_ref[...], w_ref[...], NT_DIMS, preferred_element_type=F32)
        dgu_ref[:, :D_FF] = ((d_act * u) * (sg + a - a * sg)).astype(BF16)
        dgu_ref[:, D_FF:] = (d_act * a).astype(BF16)

    return _rowwise(body, "ffn_down_loss", x2.shape[0], tm, rows=[gu_pre, x2, target], halos=[gu_pre], fulls=[cw, w_down],
                    outs=[(D_MODEL, F32), (D_MODEL, BF16), (D_FF, BF16), (2 * D_FF, BF16)], accs=[((SUBLANES, LANES), F32)])


def _ffn_bwd_conv(dgu, gu_pre, cw, tm):
    def body(i, n, d_ref, g_ref, dp_ref, dn_ref, cw_ref, dpre_ref, gc_ref):
        _zero_first(i, gc_ref)
        d = d_ref[...].astype(F32)
        g = g_ref[...]
        d_next = _shift_up(d, 1, jnp.where(i == n - 1, 0.0, dn_ref[...].astype(F32)[:SUBLANES]))
        d_prev = _shift_down(d, 1, jnp.where(i == 0, 0.0, dp_ref[...].astype(F32)[SUBLANES:]))
        dpre_ref[...] = (cw_ref[0:1, :] * d_next + cw_ref[1:2, :] * d + cw_ref[2:3, :] * d_prev).astype(BF16)
        _acc_row(gc_ref, 0, d_next * g)
        _acc_row(gc_ref, 1, d * g)
        _acc_row(gc_ref, 2, d_prev * g)
        _acc_row(gc_ref, 3, d)

    return _rowwise(body, "ffn_bwd_conv", dgu.shape[0], tm, rows=[dgu, gu_pre], halos=[dgu], fulls=[cw],
                    outs=[(2 * D_FF, BF16)], accs=[((SUBLANES, 2 * D_FF), F32)])


def _ffn_bwd_in(dpre, x2, dy, g, w_up, tm):
    cs = 2 * D_FF // N_CHIPS

    def body(i, n, dp_ref, x_ref, dy_ref, g_ref, w_ref, dx_ref, dxb_ref, gg_ref):
        _zero_first(i, gg_ref)
        d_h = jnp.zeros(x_ref.shape, F32)
        for k in range(N_CHIPS):
            d_h = d_h + lax.dot_general(dp_ref[:, k * cs:(k + 1) * cs], w_ref[k], (((1,), (1,)), ((), ())),
                                        preferred_element_type=F32)
        xv = x_ref[...]
        dx, dg = _norm_bwd(xv, _rstd(xv), g_ref[...], d_h)
        _acc_row(gg_ref, 0, dg)
        dx = dx + dy_ref[...]
        dx_ref[...] = dx
        dxb_ref[...] = dx.astype(BF16)

    return _rowwise(body, "ffn_bwd_in", x2.shape[0], tm, rows=[dpre, x2, dy], fulls=[g, w_up],
                    outs=[(D_MODEL, F32), (D_MODEL, BF16)], accs=[((SUBLANES, D_MODEL), F32)])


def _mem_attn_bwd(x1, dx2, dx2b, g, w_q, g_q, km, vm, w_o, tm):
    scale = MEM_HD ** -0.5
    m = km.shape[0]

    def body(i, n, x1_ref, dx2_ref, dx2b_ref, g_ref, wq_ref, gq_ref, km_ref, vm_ref, wo_ref,
             dx1_ref, dx1b_ref, hm_ref, dqr_ref, dkm_ref, dvm_ref, gg_ref, ggq_ref):
        _zero_first(i, dkm_ref, dvm_ref, gg_ref, ggq_ref)
        x1v = x1_ref[...]
        hm, heads = _mem_attn_core(x1v, g_ref, wq_ref, gq_ref, km_ref, vm_ref)
        hm_ref[...] = hm
        cs = D_MODEL // N_CHIPS
        d_o = jnp.zeros((x1v.shape[0], MEM_W), F32)
        for k in range(N_CHIPS):
            d_o = d_o + lax.dot_general(dx2b_ref[:, k * cs:(k + 1) * cs], wo_ref[k], (((1,), (1,)), ((), ())),
                                        preferred_element_type=F32)
        for h in range(MEM_HEADS):
            sl = slice(h * MEM_HD, (h + 1) * MEM_HD)
            qh, rs, qn, p, _ = heads[h]
            d_oh = d_o[:, sl].astype(BF16)
            dp = lax.dot_general(d_oh, vm_ref[:, sl], (((1,), (1,)), ((), ())), preferred_element_type=F32)
            ds = (p * (dp - jnp.sum(dp * p, axis=-1, keepdims=True)) * scale).astype(BF16)
            dqn = jnp.dot(ds, km_ref[:, sl], preferred_element_type=F32)
            dkm_ref[:, sl] += lax.dot_general(ds, qn, (((0,), (0,)), ((), ())), preferred_element_type=F32)
            dvm_ref[:, sl] += lax.dot_general(p.astype(BF16), d_oh, (((0,), (0,)), ((), ())), preferred_element_type=F32)
            dqh, dgq = _norm_bwd(qh, rs, gq_ref[...], dqn)
            _acc_row(ggq_ref, 0, dgq)
            dqr_ref[:, sl] = dqh.astype(BF16)
        d_hm = lax.dot_general(dqr_ref[...], wq_ref[...], (((1,), (1,)), ((), ())), preferred_element_type=F32)
        dx, dg = _norm_bwd(x1v, _rstd(x1v), g_ref[...], d_hm)
        _acc_row(gg_ref, 0, dg)
        dx = dx + dx2_ref[...]
        dx1_ref[...] = dx
        dx1b_ref[...] = dx.astype(BF16)

    return _rowwise(body, "mem_attn_bwd", x1.shape[0], tm, rows=[x1, dx2, dx2b], fulls=[g, w_q, g_q, km, vm, w_o],
                    outs=[(D_MODEL, F32), (D_MODEL, BF16), (D_MODEL, BF16), (MEM_W, BF16)],
                    accs=[((m, MEM_W), F32), ((m, MEM_W), F32), ((SUBLANES, D_MODEL), F32), ((SUBLANES, MEM_HD), F32)])


def _mem_kv_bwd(mem, g_mem, w_kv, g_k, dkm, dvm):
    m = mem.shape[0]

    def body(i, n, mem_ref, dkm_ref, dvm_ref, g_ref, w_ref, gk_ref, gw_ref, gg_ref, ggk_ref, dkv_ref):
        gg_ref[...] = jnp.zeros_like(gg_ref)
        ggk_ref[...] = jnp.zeros_like(ggk_ref)
        mv = mem_ref[...]
        rs_m = _rstd(mv)
        mem_n = (mv * rs_m * g_ref[...]).astype(BF16)
        kv = jnp.dot(mem_n, w_ref[...], preferred_element_type=F32)
        for h in range(MEM_HEADS):
            sl = slice(h * MEM_HD, (h + 1) * MEM_HD)
            kh = kv[:, sl]
            dkh, dgk = _norm_bwd(kh, _rstd(kh), gk_ref[...], dkm_ref[:, sl])
            _acc_row(ggk_ref, 0, dgk)
            dkv_ref[:, sl] = dkh.astype(BF16)
        dkv_ref[:, MEM_W:] = dvm_ref[...].astype(BF16)
        gw_ref[...] = lax.dot_general(mem_n, dkv_ref[...], (((0,), (0,)), ((), ())), preferred_element_type=F32)
        d_mn = lax.dot_general(dkv_ref[...], w_ref[...], (((1,), (1,)), ((), ())), preferred_element_type=F32)
        _acc_row(gg_ref, 0, d_mn * (mv * rs_m))

    return _rowwise(body, "mem_kv_bwd", m, m, rows=[mem, dkm, dvm], fulls=[g_mem, w_kv, g_k],
                    accs=[((D_MODEL, 2 * MEM_W), F32), ((SUBLANES, D_MODEL), F32), ((SUBLANES, MEM_HD), F32),
                          ((m, 2 * MEM_W), BF16)])


def _mix_out_bwd(dx1b, hf, hb, yg, o, g_lru, g_mla, w_out, tm):
    def body(i, n, dx_ref, hf_ref, hb_ref, yg_ref, o_ref, gl, gm, w_ref, dh_ref, dyg_ref, dob_ref, dl_ref, ggl_ref, ggm_ref):
        _zero_first(i, ggl_ref, ggm_ref)
        dmix = lax.dot_general(dx_ref[...], w_ref[...], (((1,), (1,)), ((), ())), preferred_element_type=F32)
        hs = hf_ref[...] + hb_ref[...]
        ygv = yg_ref[...]
        ge = _gelu(ygv)
        lo = hs * ge
        d_lo, dgl = _norm_bwd(lo, _rstd(lo), gl[...], dmix[:, :LRU_W])
        _acc_row(ggl_ref, 0, dgl)
        dh_ref[...] = d_lo * ge
        dyg_ref[...] = (d_lo * hs * _gelu_grad(ygv)).astype(BF16)
        ov = o_ref[...]
        d_o, dgm = _norm_bwd(ov, _rstd(ov), gm[...], dmix[:, LRU_W:])
        _acc_row(ggm_ref, 0, dgm)
        dob_ref[...] = d_o.astype(BF16)
        prod = d_o * ov
        lane_w = lax.broadcasted_iota(jnp.int32, prod.shape, 1)
        lane = lax.broadcasted_iota(jnp.int32, (prod.shape[0], LANES), 1)
        dl = jnp.zeros((prod.shape[0], LANES), F32)
        for h in range(HEADS):
            in_head = (lane_w >= h * V_DIM) & (lane_w < (h + 1) * V_DIM)
            dl = dl + jnp.where(lane == h, jnp.sum(jnp.where(in_head, prod, 0.0), axis=-1, keepdims=True), 0.0)
        dl_ref[...] = dl

    return _rowwise(body, "mix_out_bwd", dx1b.shape[0], tm, rows=[dx1b, hf, hb, yg, o], fulls=[g_lru, g_mla, w_out],
                    outs=[(LRU_W, F32), (LRU_W, BF16), (MLA_W, BF16), (LANES, F32)],
                    accs=[((SUBLANES, LRU_W), F32), ((SUBLANES, MLA_W), F32)])


def _mla_qkv_bwd(cq, ckv, krp, cos_t, sin_t, dq, dk, dv, g_qa, g_kva, g_qn, g_kn, w_uq_p, w_uk_p, w_uv, tm):
    scale = QK_HEAD ** -0.5

    def body(i, n, cq_ref, ckv_ref, kr_ref, c_ref, s_ref, dq_ref, dk_ref, dv_ref, gqa, gkva, gqn, gkn, wq, wk, wv,
             dcq_ref, dckv_ref, dkr_ref, cqb_ref, dqr_ref, ckvb_ref, dkn_ref, dvb_ref, ggqa, ggkva, ggqn, ggkn):
        _zero_first(i, ggqa, ggkva, ggqn, ggkn)
        cosv, sinv = c_ref[...], s_ref[...]
        cqv = cq_ref[...]
        rs_q = _rstd(cqv)
        cqb_ref[...] = (cqv * rs_q * gqa[...]).astype(BF16)
        qr = jnp.dot(cqb_ref[...], wq[...], preferred_element_type=F32)
        ckvv = ckv_ref[...]
        rs_kv = _rstd(ckvv)
        ckvb_ref[...] = (ckvv * rs_kv * gkva[...]).astype(BF16)
        kn = jnp.dot(ckvb_ref[...], wk[...], preferred_element_type=F32)
        kr = kr_ref[...]
        dkr = jnp.zeros_like(kr)
        for h in range(HEADS):
            sl = slice(h * LANES, (h + 1) * LANES)
            qh = qr[:, sl]
            d_qn = _rope_t(dq_ref[:, sl] * scale, cosv, sinv)
            dqh, dgq = _norm_bwd(qh, _rstd(qh, QK_HEAD), gqn[...], d_qn, QK_HEAD)
            _acc_row(ggqn, 0, dgq)
            dqr_ref[:, sl] = dqh.astype(BF16)
            kh = kn[:, sl] + kr
            d_kn = _rope_t(dk_ref[:, sl] * (1.0 / LOG2E), cosv, sinv)
            dkh, dgk = _norm_bwd(kh, _rstd(kh, QK_HEAD), gkn[...], d_kn, QK_HEAD)
            _acc_row(ggkn, 0, dgk)
            dkn_ref[:, sl] = dkh.astype(BF16)
            dkr = dkr + dkh
        dkr_ref[...] = dkr.astype(BF16)
        dvb_ref[...] = dv_ref[...].astype(BF16)
        d_cq = lax.dot_general(dqr_ref[...], wq[...], (((1,), (1,)), ((), ())), preferred_element_type=F32)
        dcq, dg = _norm_bwd(cqv, rs_q, gqa[...], d_cq)
        _acc_row(ggqa, 0, dg)
        dcq_ref[...] = dcq.astype(BF16)
        d_ckv = (lax.dot_general(dkn_ref[...], wk[...], (((1,), (1,)), ((), ())), preferred_element_type=F32)
                 + lax.dot_general(dvb_ref[...], wv[...], (((1,), (1,)), ((), ())), preferred_element_type=F32))
        dckv, dg = _norm_bwd(ckvv, rs_kv, gkva[...], d_ckv)
        _acc_row(ggkva, 0, dg)
        dckv_ref[...] = dckv.astype(BF16)

    return _rowwise(body, "mla_qkv_bwd", cq.shape[0], tm, rows=[cq, ckv, krp, cos_t, sin_t, dq, dk, dv],
                    fulls=[g_qa, g_kva, g_qn, g_kn, w_uq_p, w_uk_p, w_uv],
                    outs=[(Q_LORA, BF16), (KV_LORA, BF16), (LANES, BF16), (Q_LORA, BF16), (HEADS * LANES, BF16),
                          (KV_LORA, BF16), (HEADS * LANES, BF16), (MLA_W, BF16)],
                    accs=[((SUBLANES, Q_LORA), F32), ((SUBLANES, KV_LORA), F32), ((SUBLANES, LANES), F32),
                          ((SUBLANES, LANES), F32)])


def _in_proj_bwd(x, dx1, dxr_f, dxr_b, dyg, dcq, dckv, dkrp, g, w_in_p, tm):
    def body(i, n, x_ref, dx1_ref, df_ref, db_ref, dyg_ref, dcq_ref, dckv_ref, dkr_ref, g_ref, w_ref, gx_ref, dp_ref, gg_ref):
        _zero_first(i, gg_ref)
        dp_ref[:, :LRU_W] = (df_ref[...].astype(F32) + db_ref[...].astype(F32)).astype(BF16)
        dp_ref[:, LRU_W:2 * LRU_W] = dyg_ref[...].astype(BF16)
        dp_ref[:, 2 * LRU_W:2 * LRU_W + Q_LORA] = dcq_ref[...].astype(BF16)
        dp_ref[:, 2 * LRU_W + Q_LORA:OFF_KR] = dckv_ref[...].astype(BF16)
        dp_ref[:, OFF_KR:] = dkr_ref[...].astype(BF16)
        d_h = lax.dot_general(dp_ref[...], w_ref[...], (((1,), (1,)), ((), ())), preferred_element_type=F32)
        xv = x_ref[...]
        dx, dg = _norm_bwd(xv, _rstd(xv), g_ref[...], d_h)
        _acc_row(gg_ref, 0, dg)
        gx_ref[...] = dx + dx1_ref[...]

    return _rowwise(body, "in_proj_bwd", x.shape[0], tm, rows=[x, dx1, dxr_f, dxr_b, dyg, dcq, dckv, dkrp],
                    fulls=[g, w_in_p], outs=[(D_MODEL, F32), (IN_PAD, BF16)], accs=[((SUBLANES, D_MODEL), F32)])


ANY = pl.BlockSpec(memory_space=pl.ANY)


def _chip_peers(x, y):
    return ((1 - x, y), (x, 1 - y), (1 - x, 1 - y))


def _exchange_call(kern, name, ins, out_shapes, n_sems, aliases=None):
    return pl.pallas_call(
        kern, in_specs=[ANY] * len(ins), out_specs=[ANY] * len(out_shapes), out_shape=out_shapes,
        scratch_shapes=[pltpu.SemaphoreType.DMA((n,)) for n in n_sems], input_output_aliases=aliases or {},
        name=name)(*ins)


def _start_then_wait(copies):
    for cp in copies:
        cp.start()
    for cp in copies:
        cp.wait()


N_DEV = 8
RELATIONS = tuple((dx, dy, dc) for dx in (0, 1) for dy in (0, 1) for dc in (0, 1))[1:]


def _flip(v, d):
    return 1 - v if d else v


def _gather_copies(ins, outs, ssem, rsem, lsem):
    x, y, c = lax.axis_index("x"), lax.axis_index("y"), lax.axis_index("c")
    me = 2 * x + y
    cps = []
    for i, (a, o) in enumerate(zip(ins, outs)):
        cps.append(pltpu.make_async_copy(a, o.at[me], lsem.at[i]))
        for j, (px, py) in enumerate(_chip_peers(x, y)):
            cps.append(pltpu.make_async_remote_copy(a, o.at[me], ssem.at[3 * i + j], rsem.at[3 * i + j],
                                                    device_id=(px, py, c), device_id_type=MESH))
    return cps


def _gather_shapes(arrs):
    return [jax.ShapeDtypeStruct((N_CHIPS,) + a.shape, a.dtype) for a in arrs]


def _gather_sems(n):
    return [pltpu.SemaphoreType.DMA((3 * n,)), pltpu.SemaphoreType.DMA((3 * n,)), pltpu.SemaphoreType.DMA((n,))]


def _gather_chips(arrs):
    n = len(arrs)

    def kern(*refs):
        _start_then_wait(_gather_copies(refs[:n], refs[n:2 * n], *refs[2 * n:]))

    return _exchange_call(kern, "gather_weights", arrs, _gather_shapes(arrs), (3 * n, 3 * n, n))


def _to_owner_copies(ins, outs, ssem, rsem, lsem):
    x, y, c = lax.axis_index("x"), lax.axis_index("y"), lax.axis_index("c")
    me = 4 * x + 2 * y + c
    cps = []
    for i, (a, o) in enumerate(zip(ins, outs)):
        cps.append(pltpu.make_async_copy(a.at[2 * x + y, c], o.at[me], lsem.at[i]))
        for r, (dx, dy, dc) in enumerate(RELATIONS):
            tx, ty, tc = _flip(x, dx), _flip(y, dy), _flip(c, dc)
            cps.append(pltpu.make_async_remote_copy(a.at[2 * tx + ty, tc], o.at[me], ssem.at[7 * i + r], rsem.at[7 * i + r],
                                                    device_id=(tx, ty, tc), device_id_type=MESH))
    return cps


def _to_owner_shapes(arrs):
    return [jax.ShapeDtypeStruct((N_DEV,) + a.shape[2:], a.dtype) for a in arrs]


def _to_owner_sems(n):
    return [pltpu.SemaphoreType.DMA((7 * n,)), pltpu.SemaphoreType.DMA((7 * n,)), pltpu.SemaphoreType.DMA((n,))]


def _to_owner(arrs, name):
    n = len(arrs)

    def kern(*refs):
        _start_then_wait(_to_owner_copies(refs[:n], refs[n:2 * n], *refs[2 * n:]))

    return _exchange_call(kern, name, arrs, _to_owner_shapes(arrs), (7 * n, 7 * n, n))


def _join_halves(arrs):
    n = len(arrs)

    def kern(*refs):
        outs, (ssem, rsem) = refs[n:2 * n], refs[2 * n:]
        x, y, c = lax.axis_index("x"), lax.axis_index("y"), lax.axis_index("c")
        _start_then_wait([
            pltpu.make_async_remote_copy(outs[i].at[c], outs[i].at[c], ssem.at[i], rsem.at[i],
                                         device_id=(x, y, 1 - c), device_id_type=MESH) for i in range(n)])

    outs = [jax.ShapeDtypeStruct(a.shape, a.dtype) for a in arrs]
    return _exchange_call(kern, "grad_join_halves", arrs, outs, (n, n), aliases={i: i for i in range(n)})


def _row_block(rows, row_bytes, limit=1 << 20):
    best = None
    for d in range(16, rows + 1, 16):
        if rows % d == 0 and d * row_bytes <= limit:
            best = d
    return best if best is not None else rows


def _sum_devices(b, c, name):
    _, h, cols = b.shape
    hb = _row_block(h, cols * 4)

    def kern(c_ref, b_ref, o_ref):
        acc = b_ref[0].astype(F32)
        for j in range(1, N_DEV):
            acc = acc + b_ref[j].astype(F32)
        o_ref[...] = acc

    return pl.pallas_call(
        kern,
        grid_spec=pltpu.PrefetchScalarGridSpec(
            num_scalar_prefetch=1, grid=(h // hb,),
            in_specs=[pl.BlockSpec((N_DEV, hb, cols), lambda i, c_ref: (0, i, 0))],
            out_specs=pl.BlockSpec((None, hb, cols), lambda i, c_ref: (c_ref[0], i, 0))),
        out_shape=jax.ShapeDtypeStruct((2, h, cols), F32), name=name, compiler_params=_cparams(1))(c, b)


def _adamw(w, g, m, v, name):
    rows, cols = w.shape
    rb = _row_block(rows, cols * 4)
    c1 = 1.0 - ADAM_B1 ** ADAM_STEP
    c2 = 1.0 - ADAM_B2 ** ADAM_STEP

    def kern(w_ref, g_ref, m_ref, v_ref, d_ref, mo_ref, vo_ref):
        gv = g_ref[...]
        mn = ADAM_B1 * m_ref[...] + (1.0 - ADAM_B1) * gv
        vn = ADAM_B2 * v_ref[...] + (1.0 - ADAM_B2) * (gv * gv)
        mo_ref[...] = mn
        vo_ref[...] = vn
        d_ref[...] = (-ADAM_LR) * ((mn / c1) / (jnp.sqrt(vn / c2) + ADAM_EPS) + ADAM_WD * w_ref[...])

    spec = pl.BlockSpec((rb, cols), lambda i: (i, 0))
    return pl.pallas_call(
        kern, grid=(rows // rb,), in_specs=[spec] * 4, out_specs=[spec] * 3,
        out_shape=[jax.ShapeDtypeStruct(w.shape, F32)] * 3, name=name, compiler_params=_cparams(1))(w, g, m, v)


def _pad_rows(flat, rows):
    return jnp.pad(flat, (0, rows * LANES - flat.shape[0])).reshape(rows, LANES)


def _round_up(n, m):
    return (n + m - 1) // m * m


def _shard_shape(shape, axis):
    return tuple(s // N_CHIPS if a == axis else s for a, s in enumerate(shape))


def _to_shards(full, axis):
    shape = full.shape
    t = full.reshape(shape[:axis] + (N_CHIPS, shape[axis] // N_CHIPS) + shape[axis + 1:])
    return jnp.moveaxis(t, axis, 0).reshape(N_CHIPS, -1)


def _from_shards(sh, shape, axis):
    t = sh.reshape((N_CHIPS,) + _shard_shape(shape, axis))
    t = jnp.moveaxis(t, 0, axis)
    return t.reshape(shape)


BIG = tuple((name, shape, axis) for name, shape, axis, big in SHARDED if big)
EARLY_WEIGHTS = ("w_in", "w_uq", "w_ukv")
SMALL_SHARDED = tuple((name, shape, axis) for name, shape, axis, big in SHARDED if not big)


def _pack_small_weights(p):
    flat = jnp.concatenate([p[name].reshape(-1) for name, _, _ in SMALL_SHARDED])
    return _pad_rows(flat, _round_up(-(-flat.shape[0] // LANES), SUBLANES))


def _unpack_small_weights(gathered):
    flat = gathered.reshape(N_CHIPS, -1)
    out, off = {}, 0
    for name, shape, axis in SMALL_SHARDED:
        n = _numel(shape) // N_CHIPS
        out[name] = _from_shards(flat[:, off:off + n], shape, axis)
        off += n
    return out


def _pack_small_local(p, prefix=""):
    parts = [p[prefix + name].reshape(-1) for name, _, _ in SMALL_SHARDED]
    parts += [p[prefix + name].reshape(-1) for name, _ in REPLICATED]
    return jnp.concatenate(parts)


def _pack_small_grads(g, loss_part):
    parts = [_to_shards(g[name], axis) for name, _, axis in SMALL_SHARDED]
    rep = jnp.concatenate([g[name].reshape(-1) for name, _ in REPLICATED] + [loss_part.reshape(1)])
    parts.append(jnp.broadcast_to(rep[None], (N_CHIPS, rep.shape[0])))
    return jnp.concatenate(parts, axis=1)


def _unpack_small_local(flat):
    out, off = {}, 0
    for name, shape, axis in SMALL_SHARDED:
        n = _numel(shape) // N_CHIPS
        out[name] = flat[off:off + n].reshape((1,) + _shard_shape(shape, axis))
        off += n
    for name, shape in REPLICATED:
        n = _numel(shape)
        out[name] = flat[off:off + n].reshape((1,) + shape)
        off += n
    return out


def _grad_shards(g, shape, axis):
    if axis == 0:
        return g.reshape((N_CHIPS,) + _shard_shape(shape, axis))
    return jnp.transpose(g.reshape(shape[0], N_CHIPS, shape[1] // N_CHIPS), (1, 0, 2))


def _cols_from_shards(w4):
    return jnp.transpose(w4, (1, 0, 2)).reshape(w4.shape[1], -1)


def _block_diag(w):
    eye = jnp.eye(LRU_BLOCKS, dtype=w.dtype)
    return jnp.einsum("ncd,nm->ncmd", w, eye).reshape(LRU_W, LRU_W)


def _block_diag_t(g):
    g4 = g.reshape(LRU_BLOCKS, 64, LRU_BLOCKS, 64)
    eye = jnp.eye(LRU_BLOCKS, dtype=g.dtype)[:, None, :, None]
    return jnp.sum(g4 * eye, axis=2)


def _pad8(a):
    return jnp.pad(a, ((0, SUBLANES - a.shape[0]), (0, 0)))


def kernel(x, mem, positions, attn_norm, w_in, lru_conv_w, lru_conv_b, lru_w_a, lru_b_a, lru_w_i, lru_b_i, lru_lambda, q_a_norm, w_uq, kv_a_norm, w_ukv, mla_q_norm, mla_k_norm, lru_out_norm, mla_out_norm, w_out, mem_attn_norm, mem_norm, w_mem_q, w_mem_kv, mem_q_norm, mem_k_norm, w_mem_o, ffn_norm, w_up, ffn_conv_w, ffn_conv_b, w_down, loss_target, m_attn_norm, m_w_in, m_lru_conv_w, m_lru_conv_b, m_lru_w_a, m_lru_b_a, m_lru_w_i, m_lru_b_i, m_lru_lambda, m_q_a_norm, m_w_uq, m_kv_a_norm, m_w_ukv, m_mla_q_norm, m_mla_k_norm, m_lru_out_norm, m_mla_out_norm, m_w_out, m_mem_attn_norm, m_mem_norm, m_w_mem_q, m_w_mem_kv, m_mem_q_norm, m_mem_k_norm, m_w_mem_o, m_ffn_norm, m_w_up, m_ffn_conv_w, m_ffn_conv_b, m_w_down, v_attn_norm, v_w_in, v_lru_conv_w, v_lru_conv_b, v_lru_w_a, v_lru_b_a, v_lru_w_i, v_lru_b_i, v_lru_lambda, v_q_a_norm, v_w_uq, v_kv_a_norm, v_w_ukv, v_mla_q_norm, v_mla_k_norm, v_lru_out_norm, v_mla_out_norm, v_w_out, v_mem_attn_norm, v_mem_norm, v_w_mem_q, v_w_mem_kv, v_mem_q_norm, v_mem_k_norm, v_w_mem_o, v_ffn_norm, v_w_up, v_ffn_conv_w, v_ffn_conv_b, v_w_down):
    given = dict(locals())
    local = {name: given[name][0] for name in WEIGHT_ORDER}
    s = x.shape[1]
    x2d, mem2d, tgt = x[0], mem[0], loss_target[0]
    tm = min(512, s)
    tm_wide = min(1024, s)
    tm_ffn = min(256, s)
    t_scan = min(1024, s)
    tq_f, tq_b, tk = min(4096, s), min(2048, s), min(512, s)

    early = [b for b in BIG if b[0] in EARLY_WEIGHTS]
    late = [b for b in BIG if b[0] not in EARLY_WEIGHTS]
    got = _gather_chips([local[name].astype(BF16) for name, _, _ in early] + [_pack_small_weights(local)])
    full = _unpack_small_weights(got[-1])

    def take_gathered(entries, arrays):
        for (name, shape, axis), w4 in zip(entries, arrays):
            if axis == 0:
                full[name] = w4.reshape(shape)
            elif name in ("w_up", "w_mem_o"):
                full[name] = w4
            else:
                full[name] = _cols_from_shards(w4)

    take_gathered(early, got)
    row = lambda a: a.reshape(1, -1)
    b16 = lambda a: a.astype(BF16)
    zeros = lambda r, c: jnp.zeros((r, c), BF16)
    w_in_f = full["w_in"]
    w_in_p = jnp.concatenate([w_in_f[:, :OFF_KR], _head_tile(zeros(D_MODEL, QK_NOPE), w_in_f[:, OFF_KR:])], axis=1)
    uq = full["w_uq"].reshape(Q_LORA, HEADS, QK_HEAD)
    w_uq_p = _head_tile(uq[:, :, :QK_NOPE], uq[:, :, QK_NOPE:]).reshape(Q_LORA, -1)
    ukv = full["w_ukv"].reshape(KV_LORA, HEADS, QK_NOPE + V_DIM)
    w_uk_p = _head_tile(ukv[:, :, :QK_NOPE], None).reshape(KV_LORA, -1)
    w_uv = ukv[:, :, QK_NOPE:].reshape(KV_LORA, MLA_W)
    wa = [b16(_block_diag(local["lru_w_a"][d])) for d in range(2)]
    wi = [b16(_block_diag(local["lru_w_i"][d])) for d in range(2)]
    cw = [_pad8(full["lru_conv_w"][d]) for d in range(2)]
    pv = [_pad8(jnp.stack([full["lru_conv_b"][d], full["lru_b_a"][d], full["lru_b_i"][d], full["lru_lambda"][d]]))
          for d in range(2)]
    ffn_cw = _pad8(jnp.concatenate([full["ffn_conv_w"], row(local["ffn_conv_b"])], axis=0))
    g_attn, g_qa, g_kva = row(local["attn_norm"]), row(local["q_a_norm"]), row(local["kv_a_norm"])
    g_qn = _head_tile(row(local["mla_q_norm"])[:, :QK_NOPE], row(local["mla_q_norm"])[:, QK_NOPE:])
    g_kn = _head_tile(row(local["mla_k_norm"])[:, :QK_NOPE], row(local["mla_k_norm"])[:, QK_NOPE:])
    g_lru, g_mla = row(local["lru_out_norm"]), row(local["mla_out_norm"])
    g_memattn, g_mem = row(local["mem_attn_norm"]), row(local["mem_norm"])
    g_mq, g_mk, g_ffn = row(local["mem_q_norm"]), row(local["mem_k_norm"]), row(local["ffn_norm"])

    inv = ROPE_THETA ** (-jnp.arange(0, QK_ROPE, 2, dtype=F32) / QK_ROPE)
    no_nope = jnp.zeros((1, QK_NOPE), F32)
    inv_tile = _head_tile(no_nope, jnp.concatenate([inv, inv])[None])
    sign_tile = _head_tile(no_nope, jnp.concatenate([-jnp.ones_like(inv), jnp.ones_like(inv)])[None])
    ang = positions[0].astype(F32)[:, None] * inv_tile
    cos_t, sin_t = jnp.cos(ang), jnp.sin(ang) * sign_tile

    xr, yg, cq, ckv, krp, hb_in = _in_proj(x2d, g_attn, w_in_p, tm_wide)
    h_f, *saved_f = _lru_scan_fwd(xr, cw[0], pv[0], wa[0], wi[0], False, t_scan)
    h_b, *saved_b = _lru_scan_fwd(xr, cw[1], pv[1], wa[1], wi[1], True, t_scan)
    q, k, v = _mla_qkv(cq, ckv, krp, cos_t, sin_t, g_qa, g_kva, g_qn, g_kn, w_uq_p, w_uk_p, w_uv, tm_wide)
    o, lse, *got = _attn_fwd(q, k, v, tq_f, tk, shards=[local[name].astype(BF16) for name, _, _ in late])
    take_gathered(late, got)
    x1, mixed = _mix_out(h_f, h_b, yg, o, x2d, g_lru, g_mla, full["w_out"], tm_wide)
    km, vm = _mem_kv(mem2d, g_mem, full["w_mem_kv"], g_mk)
    x2, o_mem = _mem_attn(x1, g_memattn, full["w_mem_q"], g_mq, km, vm, full["w_mem_o"], tm_wide)
    gu_pre, hb_ffn = _ffn_up(x2, g_ffn, full["w_up"], tm)
    dy, dyb, act, dgu, loss_acc = _ffn_down_loss(gu_pre, x2, tgt, ffn_cw, full["w_down"], tm_ffn)

    grads = {}
    grads["w_down"] = _matmul_tn(act, dyb, "grad_w_down", out_dtype=BF16)
    dpre, g_conv = _ffn_bwd_conv(dgu, gu_pre, ffn_cw, tm_ffn)
    grads["ffn_conv_w"], grads["ffn_conv_b"] = g_conv[:3], g_conv[3]
    grads["w_up"] = _matmul_tn(hb_ffn, dpre, "grad_w_up", col_shards=True, out_dtype=BF16)
    dx2, dx2b, gg = _ffn_bwd_in(dpre, x2, dy, g_ffn, full["w_up"], tm)
    grads["ffn_norm"] = gg[0]
    grads["w_mem_o"] = _matmul_tn(o_mem, dx2b, "grad_w_mem_o", out_dtype=BF16)
    dx1, dx1b, hm, dqr_mem, dkm, dvm, gg, ggq = _mem_attn_bwd(x1, dx2, dx2b, g_memattn, full["w_mem_q"], g_mq, km, vm,
                                                                 full["w_mem_o"], tm)
    grads["mem_attn_norm"], grads["mem_q_norm"] = gg[0], ggq[0]
    grads["w_mem_q"] = _matmul_tn(hm, dqr_mem, "grad_w_mem_q", out_dtype=BF16)
    g_mem_kv, gg, ggk, _ = _mem_kv_bwd(mem2d, g_mem, full["w_mem_kv"], g_mk, dkm, dvm)
    grads["w_mem_kv"] = g_mem_kv.astype(BF16)
    grads["mem_norm"], grads["mem_k_norm"] = gg[0], ggk[0]
    grads["w_out"] = _matmul_tn(mixed, dx1b, "grad_w_out", out_dtype=BF16)
    dh, dyg, dob, dl128, ggl, ggm = _mix_out_bwd(dx1b, h_f, h_b, yg, o, g_lru, g_mla, full["w_out"], tm)
    grads["lru_out_norm"], grads["mla_out_norm"] = ggl[0], ggm[0]
    delta_t = jnp.transpose(dl128[:, :HEADS]).reshape(HEADS // 2, 2, s)
    def halves(name, shape, axis):
        g4 = grads[name] if grads[name].ndim == 3 else _grad_shards(grads[name], shape, axis)
        return g4.reshape(N_CHIPS, 2, g4.shape[1] // 2, g4.shape[2])

    dq, dk, dv, *arrived_late = _attn_bwd(q, k, v, dob, lse, delta_t, tq_b, tk,
                                          contributions=[halves(*e) for e in late])
    (dcq, dckv, dkrp, cqb, dqr, ckvb, dkn, dvb, ggqa, ggkva, ggqn, ggkn) = _mla_qkv_bwd(
        cq, ckv, krp, cos_t, sin_t, dq, dk, dv, g_qa, g_kva, g_qn, g_kn, w_uq_p, w_uk_p, w_uv, tm_wide)
    grads["q_a_norm"], grads["kv_a_norm"] = ggqa[0], ggkva[0]
    grads["mla_q_norm"] = jnp.concatenate(_from_head_tile(ggqn[0]))
    grads["mla_k_norm"] = jnp.concatenate(_from_head_tile(ggkn[0]))
    g_uq_p = _matmul_tn(cqb, dqr, "grad_w_uq")
    grads["w_uq"] = jnp.concatenate(_from_head_tile(g_uq_p.reshape(Q_LORA, HEADS, LANES)), axis=-1).reshape(Q_LORA, -1)
    g_uk_p = _from_head_tile(_matmul_tn(ckvb, dkn, "grad_w_uk").reshape(KV_LORA, HEADS, LANES))[0]
    g_uv = _matmul_tn(ckvb, dvb, "grad_w_uv").reshape(KV_LORA, HEADS, V_DIM)
    grads["w_ukv"] = jnp.concatenate([g_uk_p, g_uv], axis=2).reshape(KV_LORA, -1)
    dxr, gwa, gwi, gvec = [], [], [], []
    for d, (hd, saved) in enumerate(((h_f, saved_f), (h_b, saved_b))):
        r = _lru_scan_bwd(xr, saved, hd, dh, cw[d], pv[d], wa[d], wi[d], d == 1, t_scan)
        dxr.append(r[0])
        gwa.append(_block_diag_t(r[1]))
        gwi.append(_block_diag_t(r[2]))
        gvec.append(r[3])
    grads["lru_w_a"], grads["lru_w_i"] = jnp.stack(gwa), jnp.stack(gwi)
    grads["lru_conv_w"] = jnp.stack([gv[:CONV_W] for gv in gvec])
    for r_i, name in ((4, "lru_conv_b"), (5, "lru_b_a"), (6, "lru_b_i"), (7, "lru_lambda")):
        grads[name] = jnp.stack([gv[r_i] for gv in gvec])
    grad_x, dproj, gg = _in_proj_bwd(x2d, dx1, dxr[0], dxr[1], dyg, dcq, dckv, dkrp, g_attn, w_in_p, tm)
    grads["attn_norm"] = gg[0]
    g_in_p = _matmul_tn(hb_in, dproj, "grad_w_in")
    grads["w_in"] = jnp.concatenate([g_in_p[:, :OFF_KR], _from_head_tile(g_in_p[:, OFF_KR:])[1]], axis=1)

    small = _pack_small_grads(grads, loss_acc[0, 0] * (0.5 / D_MODEL))
    length = small.shape[1]
    hrows = _round_up(-(-length // (2 * LANES)), 16)
    small = jnp.pad(small, ((0, 0), (0, 2 * hrows * LANES - length))).reshape(N_CHIPS, 2, hrows, LANES)
    for name, _, _ in early:
        grads[name] = grads[name].astype(BF16)
    arrived_early = _to_owner([halves(*e) for e in early] + [small], "grad_to_owner")
    names = [name for name, _, _ in late + early] + ["small"]
    c_idx = lax.axis_index("c").astype(jnp.int32).reshape(1)
    reduced = _join_halves([_sum_devices(b, c_idx, "grad_sum_" + n)
                            for n, b in zip(names, list(arrived_late) + list(arrived_early))])

    outs = [{}, {}, {}, {}]
    for (name, shape, axis), r in zip(late + early, reduced):
        g2 = r.reshape(_shard_shape(shape, axis))
        res = _adamw(local[name], g2, given["m_" + name][0], given["v_" + name][0], "adamw_" + name)
        for o_, a in zip(outs, (g2, *res)):
            o_[name] = a[None]
    pack = lambda prefix: _pad_rows(_pack_small_local({n: given[prefix + n] for n in WEIGHT_ORDER}), 2 * hrows)
    g_small = reduced[-1].reshape(2 * hrows, LANES)
    res = _adamw(pack(""), g_small, pack("m_"), pack("v_"), "adamw_small")
    for o_, a in zip(outs, (g_small, *res)):
        o_.update(_unpack_small_local(a.reshape(-1)))
    loss = g_small.reshape(-1)[length - 1]
    return (loss, grad_x[None], *[o_[n] for o_ in outs for n in WEIGHT_ORDER])
```

```python
import jax
import jax.numpy as jnp
from jax import lax
from jax.experimental import pallas as pl
from jax.experimental.pallas import tpu as pltpu

F32, BF16 = jnp.float32, jnp.bfloat16
MESH = pl.DeviceIdType.MESH

D_MODEL = 1024
EPS = 1e-6
LRU_W = 512
LRU_BLOCKS = 8
LRU_C = 8.0
CONV_W = 4
HEADS = 8
QK_NOPE, QK_ROPE, QK_HEAD, V_DIM = 64, 32, 96, 64
Q_LORA, KV_LORA = 256, 128
MLA_W = HEADS * V_DIM
ROPE_THETA = 10000.0
IN_COLS = 2 * LRU_W + Q_LORA + KV_LORA + QK_ROPE
OFF_KR = IN_COLS - QK_ROPE
IN_PAD = 1536
MEM_HEADS, MEM_HD = 4, 128
MEM_W = MEM_HEADS * MEM_HD
D_FF = 2816
N_CHIPS = 4
ADAM_LR, ADAM_B1, ADAM_B2, ADAM_EPS, ADAM_WD, ADAM_STEP = 0.001, 0.9, 0.999, 1e-08, 0.01, 10

LANES = 128
SUBLANES = 8
V7X_VMEM_BYTES = 64 * 1024 * 1024
VMEM_LIMIT = V7X_VMEM_BYTES * 7 // 8

SHARDED = (
    ("w_in", (D_MODEL, IN_COLS), 1, True),
    ("lru_conv_w", (2, CONV_W, LRU_W), 2, False),
    ("lru_conv_b", (2, LRU_W), 1, False),
    ("lru_b_a", (2, LRU_W), 1, False),
    ("lru_b_i", (2, LRU_W), 1, False),
    ("lru_lambda", (2, LRU_W), 1, False),
    ("w_uq", (Q_LORA, HEADS * QK_HEAD), 1, True),
    ("w_ukv", (KV_LORA, HEADS * (QK_NOPE + V_DIM)), 1, True),
    ("w_out", (2 * LRU_W, D_MODEL), 0, True),
    ("w_mem_q", (D_MODEL, MEM_W), 0, True),
    ("w_mem_kv", (D_MODEL, 2 * MEM_W), 0, True),
    ("w_mem_o", (MEM_W, D_MODEL), 1, True),
    ("w_up", (D_MODEL, 2 * D_FF), 1, True),
    ("ffn_conv_w", (3, 2 * D_FF), 1, False),
    ("w_down", (D_FF, D_MODEL), 0, True),
)
REPLICATED = (
    ("attn_norm", (D_MODEL,)), ("lru_w_a", (2, LRU_BLOCKS, 64, 64)), ("lru_w_i", (2, LRU_BLOCKS, 64, 64)),
    ("q_a_norm", (Q_LORA,)), ("kv_a_norm", (KV_LORA,)), ("mla_q_norm", (QK_HEAD,)), ("mla_k_norm", (QK_HEAD,)),
    ("lru_out_norm", (LRU_W,)), ("mla_out_norm", (MLA_W,)), ("mem_attn_norm", (D_MODEL,)), ("mem_norm", (D_MODEL,)),
    ("mem_q_norm", (MEM_HD,)), ("mem_k_norm", (MEM_HD,)), ("ffn_norm", (D_MODEL,)), ("ffn_conv_b", (2 * D_FF,)),
)
WEIGHT_ORDER = ('attn_norm', 'w_in', 'lru_conv_w', 'lru_conv_b', 'lru_w_a', 'lru_b_a', 'lru_w_i', 'lru_b_i', 'lru_lambda',
                'q_a_norm', 'w_uq', 'kv_a_norm', 'w_ukv', 'mla_q_norm', 'mla_k_norm', 'lru_out_norm', 'mla_out_norm', 'w_out',
                'mem_attn_norm', 'mem_norm', 'w_mem_q', 'w_mem_kv', 'mem_q_norm', 'mem_k_norm', 'w_mem_o', 'ffn_norm', 'w_up',
                'ffn_conv_w', 'ffn_conv_b', 'w_down')


def _numel(shape):
    n = 1
    for s in shape:
        n *= s
    return n


def _cparams(n_axes):
    return pltpu.CompilerParams(dimension_semantics=("arbitrary",) * n_axes, vmem_limit_bytes=VMEM_LIMIT)


def _bdot(a, b):
    return jnp.dot(a.astype(BF16), b.astype(BF16), preferred_element_type=F32)


def _bdot_nt(a, b):
    return lax.dot_general(a.astype(BF16), b.astype(BF16), (((1,), (1,)), ((), ())), preferred_element_type=F32)


def _bdot_tn(a, b):
    return lax.dot_general(a.astype(BF16), b.astype(BF16), (((0,), (0,)), ((), ())), preferred_element_type=F32)


def _rstd(x, n=None):
    n = x.shape[-1] if n is None else n
    return lax.rsqrt(jnp.sum(x * x, axis=-1, keepdims=True) * (1.0 / n) + EPS)


def _norm_bwd(x, rs, g, dy, n=None):
    n = x.shape[-1] if n is None else n
    xhat = x * rs
    dxh = dy * g
    dx = rs * (dxh - xhat * (jnp.sum(dxh * xhat, axis=-1, keepdims=True) * (1.0 / n)))
    return dx, dy * xhat


def _acc_row(ref, r, val):
    ref[r:r + 1, :] += jnp.sum(val, axis=0, keepdims=True)


def _zero_first(i, *refs):
    @pl.when(i == 0)
    def _():
        for r in refs:
            r[...] = jnp.zeros_like(r)


def _shift_down(x, j, halo):
    if j == 0:
        return x
    xs = pltpu.roll(x, j, 0)
    hs = pltpu.roll(halo, j, 0)
    row = lax.broadcasted_iota(jnp.int32, hs.shape, 0)
    top = jnp.where(row < j, hs, xs[:SUBLANES])
    return jnp.concatenate([top, xs[SUBLANES:]], axis=0)


def _shift_up(x, j, halo):
    if j == 0:
        return x
    t = x.shape[0]
    xs = pltpu.roll(x, t - j, 0)
    hs = pltpu.roll(halo, SUBLANES - j, 0)
    row = lax.broadcasted_iota(jnp.int32, hs.shape, 0)
    bot = jnp.where(row >= SUBLANES - j, hs, xs[t - SUBLANES:])
    return jnp.concatenate([xs[:t - SUBLANES], bot], axis=0)


def _shift(x, j, halo, down):
    return _shift_down(x, j, halo) if down else _shift_up(x, j, halo)


def _scan(a, b, h_in, down):
    t, c = a.shape
    g = t // SUBLANES
    a3, b3 = a.reshape(g, SUBLANES, c), b.reshape(g, SUBLANES, c)
    sub = lax.broadcasted_iota(jnp.int32, a3.shape, 1)
    d = 1
    while d < SUBLANES:
        keep = (sub >= d) if down else (sub < SUBLANES - d)
        shift = d if down else SUBLANES - d
        a_s = jnp.where(keep, pltpu.roll(a3, shift, 1), 1.0)
        b_s = jnp.where(keep, pltpu.roll(b3, shift, 1), 0.0)
        b3 = a3 * b_s + b3
        a3 = a3 * a_s
        d *= 2
    hs = [None] * g
    carry = h_in
    for i in (range(g) if down else range(g - 1, -1, -1)):
        hs[i] = a3[i] * carry + b3[i]
        carry = hs[i][SUBLANES - 1:, :] if down else hs[i][:1, :]
    return jnp.concatenate(hs, axis=0)


def _sigmoid(x):
    return 0.5 * jnp.tanh(0.5 * x) + 0.5


LOG2E = 1.4426950408889634
GELU_K = 0.7978845608028654
GELU_C = 0.044715


def _gelu(x):
    return 0.5 * x * (1.0 + jnp.tanh(GELU_K * (x + GELU_C * x * x * x)))


def _gelu_grad(x):
    t = jnp.tanh(GELU_K * (x + GELU_C * x * x * x))
    return 0.5 * (1.0 + t) + 0.5 * x * (1.0 - t * t) * GELU_K * (1.0 + 3.0 * GELU_C * x * x)


ROPE_HALF = QK_ROPE // 2
ROPE_LANE = 32


def _head_tile(nope, rope):
    z = lambda n: jnp.zeros(nope.shape[:-1] + (n,), nope.dtype)
    r1, r2 = (z(ROPE_HALF), z(ROPE_HALF)) if rope is None else (rope[..., :ROPE_HALF], rope[..., ROPE_HALF:])
    return jnp.concatenate([nope[..., :ROPE_LANE], r1, nope[..., ROPE_LANE:], z(ROPE_HALF), r2, z(ROPE_HALF)], axis=-1)


def _from_head_tile(t):
    a, b = ROPE_LANE + ROPE_HALF, ROPE_LANE + LANES // 2
    return (jnp.concatenate([t[..., :ROPE_LANE], t[..., a:a + QK_NOPE - ROPE_LANE]], axis=-1),
            jnp.concatenate([t[..., ROPE_LANE:a], t[..., b:b + ROPE_HALF]], axis=-1))


def _rope_partner(x):
    lane = lax.broadcasted_iota(jnp.int32, x.shape, 1) & (LANES // 2 - 1)
    return jnp.where((lane >= ROPE_LANE) & (lane < ROPE_LANE + ROPE_HALF), pltpu.roll(x, LANES // 2, 1), 0.0)


def _rope(x, cos_t, sin_t):
    return x * cos_t + _rope_partner(x) * sin_t


def _rope_t(dy, cos_t, sin_t):
    return dy * cos_t + _rope_partner(dy * sin_t)


def _rowwise(body, name, s, tm, rows=(), halos=(), fulls=(), outs=(), accs=()):
    n = s // tm
    in_specs, args = [], []
    for a in rows:
        in_specs.append(pl.BlockSpec((tm, a.shape[1]), lambda i: (i, 0)))
        args.append(a)
    for a in halos:
        hr = 2 * SUBLANES if a.dtype == BF16 else SUBLANES
        in_specs.append(pl.BlockSpec((hr, a.shape[1]), lambda i, hr=hr: (jnp.maximum(i * (tm // hr) - 1, 0), 0)))
        in_specs.append(pl.BlockSpec((hr, a.shape[1]), lambda i, hr=hr: (jnp.minimum((i + 1) * (tm // hr), s // hr - 1), 0)))
        args += [a, a]
    for a in fulls:
        in_specs.append(pl.BlockSpec(a.shape, lambda i, nd=a.ndim: (0,) * nd))
        args.append(a)
    out_shape, out_specs = [], []
    for c, dt in outs:
        out_shape.append(jax.ShapeDtypeStruct((s, c), dt))
        out_specs.append(pl.BlockSpec((tm, c), lambda i: (i, 0)))
    for shp, dt in accs:
        out_shape.append(jax.ShapeDtypeStruct(shp, dt))
        out_specs.append(pl.BlockSpec(shp, lambda i, nd=len(shp): (0,) * nd))

    def kern(*refs):
        body(pl.program_id(0), n, *refs)

    return pl.pallas_call(kern, grid=(n,), in_specs=in_specs, out_specs=out_specs, out_shape=out_shape, name=name,
                          compiler_params=_cparams(1))(*args)


def _matmul_tn(a, b, name, col_shards=False, out_dtype=F32):
    t, m = a.shape
    n = b.shape[1]
    bm = m
    for cand in range(LANES, m + 1, LANES):
        if m % cand == 0 and cand * (n // N_CHIPS if col_shards else min(n, 2048)) * 4 <= 6 * 1024 * 1024:
            bm = cand
    bn = n // N_CHIPS if col_shards else (n if n <= 2048 else 1408)
    bt = min(t, 2048)
    nt = t // bt

    def kern(a_ref, b_ref, o_ref, acc_ref):
        k = pl.program_id(2)

        @pl.when(k == 0)
        def _():
            acc_ref[...] = jnp.zeros_like(acc_ref)
        acc_ref[...] += _bdot_tn(a_ref[...], b_ref[...])

        @pl.when(k == nt - 1)
        def _():
            o_ref[...] = acc_ref[...].astype(out_dtype)

    if col_shards:
        out_spec = pl.BlockSpec((None, bm, bn), lambda i, j, k: (j, i, 0))
        out_shape = jax.ShapeDtypeStruct((N_CHIPS, m, bn), out_dtype)
    else:
        out_spec = pl.BlockSpec((bm, bn), lambda i, j, k: (i, j))
        out_shape = jax.ShapeDtypeStruct((m, n), out_dtype)
    return pl.pallas_call(
        kern, grid=(m // bm, n // bn, nt),
        in_specs=[pl.BlockSpec((bt, bm), lambda i, j, k: (k, i)), pl.BlockSpec((bt, bn), lambda i, j, k: (k, j))],
        out_specs=out_spec, out_shape=out_shape, scratch_shapes=[pltpu.VMEM((bm, bn), F32)], name=name,
        compiler_params=_cparams(3))(a, b)


def _in_proj(x, g, w_in_p, tm):
    def body(i, n, x_ref, g_ref, w_ref, xr, yg, cq, ckv, krp, hb):
        xv = x_ref[...]
        h = (xv * _rstd(xv) * g_ref[...]).astype(BF16)
        hb[...] = h
        p = jnp.dot(h, w_ref[...], preferred_element_type=F32)
        xr[...] = p[:, :LRU_W]
        yg[...] = p[:, LRU_W:2 * LRU_W]
        cq[...] = p[:, 2 * LRU_W:2 * LRU_W + Q_LORA]
        ckv[...] = p[:, 2 * LRU_W + Q_LORA:OFF_KR]
        krp[...] = p[:, OFF_KR:IN_PAD]

    return _rowwise(body, "in_proj", x.shape[0], tm, rows=[x], fulls=[g, w_in_p],
                    outs=[(LRU_W, F32), (LRU_W, F32), (Q_LORA, F32), (KV_LORA, F32), (LANES, F32), (D_MODEL, BF16)])


def _softplus_neg(lam):
    e = jnp.exp(-jnp.abs(lam))
    return jnp.maximum(-lam, 0.0) + jnp.where(e < 1e-2, e * (1.0 - e * (0.5 - e * (1.0 / 3.0))), jnp.log(1.0 + e))


def _lru_gates(x, halo, cw_ref, pv_ref, wa_ref, wi_ref, rev):
    down = not rev
    xc = pv_ref[0:1, :] + jnp.zeros_like(x)
    for j in range(CONV_W):
        k = j if rev else CONV_W - 1 - j
        xc = xc + cw_ref[k:k + 1, :] * _shift(x, j, halo, down)
    r = _sigmoid(_bdot(xc, wa_ref[...]) + pv_ref[1:2, :])
    ig = _sigmoid(_bdot(xc, wi_ref[...]) + pv_ref[2:3, :])
    lam = pv_ref[3:4, :]
    sp = _softplus_neg(lam)
    log_a = (-LRU_C) * r * sp
    a = jnp.exp(log_a)
    z = 2.0 * log_a
    series = -(z * (1.0 + z * (0.5 + z * (1.0 / 6.0 + z * (1.0 / 24.0)))))
    om = jnp.where(z > -0.02, series, 1.0 - a * a)
    mult = jnp.sqrt(om)
    return xc, r, ig, sp, a, mult


def _lru_scan_fwd(xr, cw, pv, wa, wi, rev, t):
    s = xr.shape[0]
    n = s // t
    hb = t // SUBLANES
    last8 = s // SUBLANES - 1
    down = not rev

    def kern(x_ref, halo_ref, cw_ref, pv_ref, wa_ref, wi_ref, h_ref, xc_ref, r_ref, ig_ref, a_ref, mult_ref, carry_ref):
        i = pl.program_id(0)
        _zero_first(i, carry_ref)
        halo = jnp.where(i == 0, 0.0, halo_ref[...])
        xc, r, ig, sp, a, mult = _lru_gates(x_ref[...], halo, cw_ref, pv_ref, wa_ref, wi_ref, rev)
        xc_ref[...], r_ref[...], ig_ref[...], a_ref[...], mult_ref[...] = xc, r, ig, a, mult
        h_ref[...] = _scan(a, mult * ig * xc, carry_ref[...], down)
        carry_ref[...] = h_ref[pl.ds(t - 1 if down else 0, 1), :]

    if rev:
        blk = lambda i: (n - 1 - i, 0)
        hal = lambda i: (jnp.minimum((n - i) * hb, last8), 0)
    else:
        blk = lambda i: (i, 0)
        hal = lambda i: (jnp.maximum(i * hb - 1, 0), 0)
    full = lambda a: pl.BlockSpec(a.shape, lambda i: (0, 0))
    return pl.pallas_call(
        kern, grid=(n,),
        in_specs=[pl.BlockSpec((t, LRU_W), blk), pl.BlockSpec((SUBLANES, LRU_W), hal), full(cw), full(pv), full(wa), full(wi)],
        out_specs=[pl.BlockSpec((t, LRU_W), blk)] * 6, out_shape=[jax.ShapeDtypeStruct((s, LRU_W), F32)] * 6,
        scratch_shapes=[pltpu.VMEM((1, LRU_W), F32)], name="lru_scan_rev" if rev else "lru_scan_fwd",
        compiler_params=_cparams(1))(xr, xr, cw, pv, wa, wi)


def _lru_scan_bwd(xr, saved, h, dh, cw, pv, wa, wi, rev, t):
    s = xr.shape[0]
    n = s // t
    hb = t // SUBLANES
    last8 = s // SUBLANES - 1
    down = not rev

    def kern(x_ref, xc_ref, r_ref, ig_ref, a_ref, mult_ref, h_ref, hh_ref, dh_ref, cw_ref, pv_ref, wa_ref, wi_ref,
             dx_ref, gwa_ref, gwi_ref, gv_ref, p_ref, dxc_halo_ref, tmp_ref):
        i = pl.program_id(0)
        _zero_first(i, gwa_ref, gwi_ref, gv_ref, p_ref, dxc_halo_ref)
        at_start = i == n - 1
        x = x_ref[...]
        hhalo = jnp.where(at_start, 0.0, hh_ref[...])
        xc, r, ig, a, mult = xc_ref[...], r_ref[...], ig_ref[...], a_ref[...], mult_ref[...]
        lam = pv_ref[3:4, :]
        sp = _softplus_neg(lam)
        h_prev = _shift(h_ref[...], 1, hhalo, down)
        row = lax.broadcasted_iota(jnp.int32, x.shape, 0)
        edge = t - 1 if down else 0
        dh_mod = dh_ref[...] + jnp.where(row == edge, p_ref[...], 0.0)
        a_next = _shift(a, 1, jnp.zeros((SUBLANES, LRU_W), F32), not down)
        g = _scan(a_next, dh_mod, jnp.zeros((1, LRU_W), F32), not down)
        tmp_ref[...] = a * g
        p_ref[...] = tmp_ref[pl.ds(0 if down else t - 1, 1), :]
        da = g * h_prev
        d_ig = g * mult * xc
        d_xc = g * mult * ig
        d_om = g * ig * xc * (0.5 / jnp.maximum(mult, 1e-30))
        d_log_a = da * a - 2.0 * d_om * a * a
        d_r = d_log_a * ((-LRU_C) * sp)
        d_sp = jnp.sum(d_log_a * ((-LRU_C) * r), axis=0, keepdims=True)
        gv_ref[7:8, :] += d_sp * (-_sigmoid(-lam))
        d_ga = d_r * r * (1.0 - r)
        d_gi = d_ig * ig * (1.0 - ig)
        _acc_row(gv_ref, 5, d_ga)
        _acc_row(gv_ref, 6, d_gi)
        d_xc = d_xc + _bdot_nt(d_ga, wa_ref[...]) + _bdot_nt(d_gi, wi_ref[...])
        gwa_ref[...] += _bdot_tn(xc, d_ga)
        gwi_ref[...] += _bdot_tn(xc, d_gi)
        _acc_row(gv_ref, 4, d_xc)
        dx = jnp.zeros_like(x)
        dxc_halo = dxc_halo_ref[...]
        for j in range(CONV_W):
            k = j if rev else CONV_W - 1 - j
            d_shift = _shift(d_xc, j, dxc_halo, not down)
            _acc_row(gv_ref, k, d_shift * x)
            dx = dx + cw_ref[k:k + 1, :] * d_shift
        dx_ref[...] = dx.astype(BF16)
        dxc_halo_ref[...] = d_xc[:SUBLANES] if down else d_xc[t - SUBLANES:]

    if rev:
        blk = lambda i: (i, 0)
        hal = lambda i: (jnp.minimum((i + 1) * hb, last8), 0)
    else:
        blk = lambda i: (n - 1 - i, 0)
        hal = lambda i: (jnp.maximum((n - 1 - i) * hb - 1, 0), 0)
    full = lambda a: pl.BlockSpec(a.shape, lambda i: (0, 0))
    bs = pl.BlockSpec((t, LRU_W), blk)
    hs = pl.BlockSpec((SUBLANES, LRU_W), hal)
    return pl.pallas_call(
        kern, grid=(n,),
        in_specs=[bs] * 7 + [hs, bs, full(cw), full(pv), full(wa), full(wi)],
        out_specs=[bs, pl.BlockSpec((LRU_W, LRU_W), lambda i: (0, 0)), pl.BlockSpec((LRU_W, LRU_W), lambda i: (0, 0)),
                   pl.BlockSpec((SUBLANES, LRU_W), lambda i: (0, 0))],
        out_shape=[jax.ShapeDtypeStruct((s, LRU_W), BF16), jax.ShapeDtypeStruct((LRU_W, LRU_W), F32),
                   jax.ShapeDtypeStruct((LRU_W, LRU_W), F32), jax.ShapeDtypeStruct((SUBLANES, LRU_W), F32)],
        scratch_shapes=[pltpu.VMEM((1, LRU_W), F32), pltpu.VMEM((SUBLANES, LRU_W), F32), pltpu.VMEM((t, LRU_W), F32)],
        name="lru_bwd_rev" if rev else "lru_bwd_fwd", compiler_params=_cparams(1))(xr, *saved, h, h, dh, cw, pv, wa, wi)


def _mla_qkv(cq, ckv, krp, cos_t, sin_t, g_qa, g_kva, g_qn, g_kn, w_uq_p, w_uk_p, w_uv, tm):
    scale = QK_HEAD ** -0.5 * LOG2E

    def body(i, n, cq_ref, ckv_ref, kr_ref, c_ref, s_ref, gqa, gkva, gqn, gkn, wq, wk, wv, q_out, k_out, v_out):
        cosv, sinv = c_ref[...], s_ref[...]
        cqv = cq_ref[...]
        qr = _bdot(cqv * _rstd(cqv) * gqa[...], wq[...])
        ckvv = ckv_ref[...]
        c_kv = (ckvv * _rstd(ckvv) * gkva[...]).astype(BF16)
        kn = jnp.dot(c_kv, wk[...], preferred_element_type=F32)
        v_out[...] = jnp.dot(c_kv, wv[...], preferred_element_type=F32).astype(BF16)
        kr = kr_ref[...]
        kr_swapped = _rope_partner(kr * gkn[...]) * sinv
        for h in range(HEADS):
            sl = slice(h * LANES, (h + 1) * LANES)
            qh = qr[:, sl]
            qh = _rope(qh * _rstd(qh, QK_HEAD) * gqn[...], cosv, sinv) * scale
            q_out[:, sl] = qh.astype(BF16)
            kh = kn[:, sl] + kr
            rs = _rstd(kh, QK_HEAD)
            k_out[:, sl] = (kh * rs * gkn[...] * cosv + kr_swapped * rs).astype(BF16)

    return _rowwise(body, "mla_qkv", cq.shape[0], tm, rows=[cq, ckv, krp, cos_t, sin_t],
                    fulls=[g_qa, g_kva, g_qn, g_kn, w_uq_p, w_uk_p, w_uv],
                    outs=[(HEADS * LANES, BF16), (HEADS * LANES, BF16), (MLA_W, BF16)])


NT_DIMS = (((1,), (1,)), ((), ()))
TN_DIMS = (((0,), (0,)), ((), ()))


def _riding_exchange(copies_fn, first, last):
    @pl.when(first)
    def _():
        for cp in copies_fn():
            cp.start()

    def finish():
        @pl.when(last)
        def _():
            for cp in copies_fn():
                cp.wait()
    return finish


def _attn_fwd(q, k, v, tq, tk, shards=()):
    s = q.shape[0]
    nq, nk = s // tq, s // tk
    n = len(shards)

    def kern(*refs):
        q_ref, k_ref, v_ref = refs[:3]
        o_ref, lse_ref = refs[3 + n:5 + n]
        acc_ref = refs[5 + 2 * n]
        p_id, i_id = pl.program_id(0), pl.program_id(1)
        finish = _riding_exchange(lambda: _gather_copies(refs[3:3 + n], refs[5 + n:5 + 2 * n], *refs[6 + 2 * n:]),
                                  (p_id == 0) & (i_id == 0), (p_id == HEADS // 2 - 1) & (i_id == nq - 1)) if n else None
        qs = (q_ref[:, :LANES], q_ref[:, LANES:])
        acc_ref[...] = jnp.zeros_like(acc_ref)

        def step(j, carry):
            off = pl.multiple_of(j * tk, tk)
            vc = v_ref[pl.ds(off, tk), :]
            out = []
            for h in range(2):
                m, l = carry[2 * h:2 * h + 2]
                st = lax.dot_general(k_ref[pl.ds(off, tk), h * LANES:(h + 1) * LANES], qs[h], NT_DIMS,
                                     preferred_element_type=F32)
                mn = jnp.maximum(m, jnp.max(st, axis=0, keepdims=True))
                al = jnp.exp2(m - mn)
                pt = jnp.exp2(st - mn)
                l = al * l + jnp.sum(pt, axis=0, keepdims=True)
                acc_ref[h] = al * acc_ref[h] + lax.dot_general(vc, pt.astype(BF16), TN_DIMS, preferred_element_type=F32)
                out += [mn, l]
            return tuple(out)

        init = (jnp.full((1, tq), -1e30, F32), jnp.zeros((1, tq), F32)) * 2
        m0, l0, m1, l1 = lax.fori_loop(0, nk, step, init)
        row = lax.broadcasted_iota(jnp.int32, (LANES, tq), 0)
        o_ref[...] = jnp.where(row < V_DIM, acc_ref[0] / l0, acc_ref[1] / l1).T
        lse_ref[0, 0:1, :] = m0 + jnp.log2(l0)
        lse_ref[0, 1:2, :] = m1 + jnp.log2(l1)
        if n:
            finish()

    return pl.pallas_call(
        kern, grid=(HEADS // 2, nq),
        in_specs=[pl.BlockSpec((tq, 2 * LANES), lambda p, i: (i, p)), pl.BlockSpec((s, 2 * LANES), lambda p, i: (0, p)),
                  pl.BlockSpec((s, LANES), lambda p, i: (0, p))] + [ANY] * n,
        out_specs=[pl.BlockSpec((tq, LANES), lambda p, i: (i, p)), pl.BlockSpec((1, 2, tq), lambda p, i: (p, 0, i))]
        + [ANY] * n,
        out_shape=[jax.ShapeDtypeStruct((s, MLA_W), F32), jax.ShapeDtypeStruct((HEADS // 2, 2, s), F32)]
        + _gather_shapes(shards),
        scratch_shapes=[pltpu.VMEM((2, LANES, tq), F32)] + (_gather_sems(n) if n else []),
        name="attn_fwd", compiler_params=_cparams(2))(q, k, v, *shards)


def _attn_bwd(q, k, v, do, lse, delta, tq, tk, contributions=()):
    s = q.shape[0]
    nq, nk = s // tq, s // tk
    n = len(contributions)

    def kern(*refs):
        q_ref, do_ref, lse_ref, dl_ref, k_ref, v_ref = refs[:6]
        dq_ref, dk_ref, dv_ref = refs[6 + n:9 + n]
        acc_ref = refs[9 + 2 * n]
        p_id, i_id = pl.program_id(0), pl.program_id(1)
        finish = _riding_exchange(lambda: _to_owner_copies(refs[6:6 + n], refs[9 + n:9 + 2 * n], *refs[10 + 2 * n:]),
                                  (p_id == 0) & (i_id == 0), (p_id == HEADS // 2 - 1) & (i_id == nq - 1)) if n else None
        _zero_first(pl.program_id(1), dk_ref, dv_ref)
        acc_ref[...] = jnp.zeros_like(acc_ref)
        qs = (q_ref[:, :LANES], q_ref[:, LANES:])
        doc = do_ref[...]
        lane_q = lax.broadcasted_iota(jnp.int32, (tq, LANES), 1)
        zq = jnp.zeros_like(doc)
        dos = (jnp.where(lane_q < V_DIM, doc, zq), jnp.where(lane_q >= V_DIM, doc, zq))
        lses = (lse_ref[0, 0:1, :], lse_ref[0, 1:2, :])
        dls = (dl_ref[0, 0:1, :], dl_ref[0, 1:2, :])

        def step(j, carry):
            off = pl.multiple_of(j * tk, tk)
            vp = v_ref[pl.ds(off, tk), :]
            lane_k = lax.broadcasted_iota(jnp.int32, (tk, LANES), 1)
            zero = jnp.zeros_like(vp)
            vs = (jnp.where(lane_k < V_DIM, vp, zero), jnp.where(lane_k >= V_DIM, vp, zero))
            for h in range(2):
                sl = slice(h * LANES, (h + 1) * LANES)
                st = lax.dot_general(k_ref[pl.ds(off, tk), sl], qs[h], NT_DIMS, preferred_element_type=F32)
                pt = jnp.exp2(st - lses[h])
                dpt = lax.dot_general(vs[h], doc, NT_DIMS, preferred_element_type=F32)
                dst = (pt * (dpt - dls[h])).astype(BF16)
                dv_ref[pl.ds(off, tk), :] += jnp.dot(pt.astype(BF16), dos[h], preferred_element_type=F32)
                dk_ref[pl.ds(off, tk), sl] += jnp.dot(dst, qs[h], preferred_element_type=F32)
                acc_ref[h] += lax.dot_general(k_ref[pl.ds(off, tk), sl], dst, TN_DIMS, preferred_element_type=F32)
            return carry

        lax.fori_loop(0, nk, step, 0)
        dq_ref[:, :LANES] = acc_ref[0].T
        dq_ref[:, LANES:] = acc_ref[1].T
        if n:
            finish()

    return pl.pallas_call(
        kern, grid=(HEADS // 2, nq),
        in_specs=[pl.BlockSpec((tq, 2 * LANES), lambda p, i: (i, p)), pl.BlockSpec((tq, LANES), lambda p, i: (i, p)),
                  pl.BlockSpec((1, 2, tq), lambda p, i: (p, 0, i)), pl.BlockSpec((1, 2, tq), lambda p, i: (p, 0, i)),
                  pl.BlockSpec((s, 2 * LANES), lambda p, i: (0, p)), pl.BlockSpec((s, LANES), lambda p, i: (0, p))]
        + [ANY] * n,
        out_specs=[pl.BlockSpec((tq, 2 * LANES), lambda p, i: (i, p)), pl.BlockSpec((s, 2 * LANES), lambda p, i: (0, p)),
                   pl.BlockSpec((s, LANES), lambda p, i: (0, p))] + [ANY] * n,
        out_shape=[jax.ShapeDtypeStruct((s, HEADS * LANES), F32), jax.ShapeDtypeStruct((s, HEADS * LANES), F32),
                   jax.ShapeDtypeStruct((s, MLA_W), F32)] + _to_owner_shapes(contributions),
        scratch_shapes=[pltpu.VMEM((2, LANES, tq), F32)] + (_to_owner_sems(n) if n else []),
        name="attn_bwd", compiler_params=_cparams(2))(q, do, lse, delta, k, v, *contributions)


def _mix_out(hf, hb, yg, o, x, g_lru, g_mla, w_out, tm):
    def body(i, n, hf_ref, hb_ref, yg_ref, o_ref, x_ref, gl, gm, w_ref, x1_ref, mix_ref):
        lo = (hf_ref[...] + hb_ref[...]) * _gelu(yg_ref[...])
        ov = o_ref[...]
        mix_ref[:, :LRU_W] = (lo * _rstd(lo) * gl[...]).astype(BF16)
        mix_ref[:, LRU_W:] = (ov * _rstd(ov) * gm[...]).astype(BF16)
        x1_ref[...] = x_ref[...] + jnp.dot(mix_ref[...], w_ref[...], preferred_element_type=F32)

    return _rowwise(body, "mix_out", x.shape[0], tm, rows=[hf, hb, yg, o, x], fulls=[g_lru, g_mla, w_out],
                    outs=[(D_MODEL, F32), (2 * LRU_W, BF16)])


def _mem_kv(mem, g_mem, w_kv, g_k):
    m = mem.shape[0]

    def body(i, n, mem_ref, g_ref, w_ref, gk_ref, km_ref, vm_ref):
        mv = mem_ref[...]
        kv = _bdot(mv * _rstd(mv) * g_ref[...], w_ref[...])
        vm_ref[...] = kv[:, MEM_W:].astype(BF16)
        for h in range(MEM_HEADS):
            sl = slice(h * MEM_HD, (h + 1) * MEM_HD)
            kh = kv[:, sl]
            km_ref[:, sl] = (kh * _rstd(kh) * gk_ref[...]).astype(BF16)

    return _rowwise(body, "mem_kv", m, m, rows=[mem], fulls=[g_mem, w_kv, g_k], outs=[(MEM_W, BF16), (MEM_W, BF16)])


def _mem_attn_core(x1v, g_ref, wq_ref, gq_ref, km_ref, vm_ref):
    scale = MEM_HD ** -0.5
    hm = (x1v * _rstd(x1v) * g_ref[...]).astype(BF16)
    qr = jnp.dot(hm, wq_ref[...], preferred_element_type=F32)
    heads = []
    for h in range(MEM_HEADS):
        sl = slice(h * MEM_HD, (h + 1) * MEM_HD)
        qh = qr[:, sl]
        rs = _rstd(qh)
        qn = (qh * rs * gq_ref[...]).astype(BF16)
        sc = lax.dot_general(qn, km_ref[:, sl], (((1,), (1,)), ((), ())), preferred_element_type=F32) * scale
        e = jnp.exp(sc - jnp.max(sc, axis=-1, keepdims=True))
        p = e / jnp.sum(e, axis=-1, keepdims=True)
        oh = jnp.dot(p.astype(BF16), vm_ref[:, sl], preferred_element_type=F32)
        heads.append((qh, rs, qn, p, oh))
    return hm, heads


def _mem_attn(x1, g, w_q, g_q, km, vm, w_o, tm):
    cs = D_MODEL // N_CHIPS

    def body(i, n, x1_ref, g_ref, wq_ref, gq_ref, km_ref, vm_ref, wo_ref, x2_ref, ob_ref):
        x1v = x1_ref[...]
        _, heads = _mem_attn_core(x1v, g_ref, wq_ref, gq_ref, km_ref, vm_ref)
        for h in range(MEM_HEADS):
            ob_ref[:, h * MEM_HD:(h + 1) * MEM_HD] = heads[h][4].astype(BF16)
        for k in range(N_CHIPS):
            sl = slice(k * cs, (k + 1) * cs)
            x2_ref[:, sl] = x1v[:, sl] + jnp.dot(ob_ref[...], wo_ref[k], preferred_element_type=F32)

    return _rowwise(body, "mem_attn", x1.shape[0], tm, rows=[x1], fulls=[g, w_q, g_q, km, vm, w_o],
                    outs=[(D_MODEL, F32), (MEM_W, BF16)])


def _ffn_up(x2, g, w_up, tm):
    cs = 2 * D_FF // N_CHIPS

    def body(i, n, x_ref, g_ref, w_ref, gu_ref, hb_ref):
        xv = x_ref[...]
        hb_ref[...] = (xv * _rstd(xv) * g_ref[...]).astype(BF16)
        for k in range(N_CHIPS):
            gu_ref[:, k * cs:(k + 1) * cs] = jnp.dot(hb_ref[...], w_ref[k], preferred_element_type=F32)

    return _rowwise(body, "ffn_up", x2.shape[0], tm, rows=[x2], fulls=[g, w_up], outs=[(2 * D_FF, F32), (D_MODEL, BF16)])


def _ffn_conv(gu, prev, nxt, cw_ref, i, n):
    prev = jnp.where(i == 0, 0.0, prev)
    nxt = jnp.where(i == n - 1, 0.0, nxt)
    return (cw_ref[3:4, :] + cw_ref[0:1, :] * _shift_down(gu, 1, prev) + cw_ref[1:2, :] * gu
            + cw_ref[2:3, :] * _shift_up(gu, 1, nxt))


def _ffn_down_loss(gu_pre, x2, target, cw, w_down, tm):
    def body(i, n, gu_ref, x_ref, t_ref, pv_ref, nx_ref, cw_ref, w_ref, dy_ref, dyb_ref, act_ref, dgu_ref, loss_ref):
        _zero_first(i, loss_ref)
        gu = _ffn_conv(gu_ref[...], pv_ref[...], nx_ref[...], cw_ref, i, n)
        g, u = gu[:, :D_FF], gu[:, D_FF:]
        sg = _sigmoid(g)
        a = g * sg
        act_ref[...] = (a * u).astype(BF16)
        y = x_ref[...] + jnp.dot(act_ref[...], w_ref[...], preferred_element_type=F32)
        e = y - t_ref[...]
        loss_ref[...] += jnp.sum(e * e)
        dy = e * (1.0 / D_MODEL)
        dy_ref[...] = dy
        dyb_ref[...] = dy.astype(BF16)
        d_act = lax.dot_general(dyb_ref[...], w_ref[...], NT_DIMS, preferred_element_type=F32)
        dgu_ref[:, :D_FF] = ((d_act * u) * (sg + a - a * sg)).astype(BF16)
        dgu_ref[:, D_FF:] = (d_act * a).astype(BF16)

    return _rowwise(body, "ffn_down_loss", x2.shape[0], tm, rows=[gu_pre, x2, target], halos=[gu_pre], fulls=[cw, w_down],
                    outs=[(D_MODEL, F32), (D_MODEL, BF16), (D_FF, BF16), (2 * D_FF, BF16)], accs=[((SUBLANES, LANES), F32)])


def _ffn_bwd_conv(dgu, gu_pre, cw, tm):
    def body(i, n, d_ref, g_ref, dp_ref, dn_ref, cw_ref, dpre_ref, gc_ref):
        _zero_first(i, gc_ref)
        d = d_ref[...].astype(F32)
        g = g_ref[...]
        d_next = _shift_up(d, 1, jnp.where(i == n - 1, 0.0, dn_ref[...].astype(F32)[:SUBLANES]))
        d_prev = _shift_down(d, 1, jnp.where(i == 0, 0.0, dp_ref[...].astype(F32)[SUBLANES:]))
        dpre_ref[...] = (cw_ref[0:1, :] * d_next + cw_ref[1:2, :] * d + cw_ref[2:3, :] * d_prev).astype(BF16)
        _acc_row(gc_ref, 0, d_next * g)
        _acc_row(gc_ref, 1, d * g)
        _acc_row(gc_ref, 2, d_prev * g)
        _acc_row(gc_ref, 3, d)

    return _rowwise(body, "ffn_bwd_conv", dgu.shape[0], tm, rows=[dgu, gu_pre], halos=[dgu], fulls=[cw],
                    outs=[(2 * D_FF, BF16)], accs=[((SUBLANES, 2 * D_FF), F32)])


def _ffn_bwd_in(dpre, x2, dy, g, w_up, tm):
    cs = 2 * D_FF // N_CHIPS

    def body(i, n, dp_ref, x_ref, dy_ref, g_ref, w_ref, dx_ref, dxb_ref, gg_ref):
        _zero_first(i, gg_ref)
        d_h = jnp.zeros(x_ref.shape, F32)
        for k in range(N_CHIPS):
            d_h = d_h + lax.dot_general(dp_ref[:, k * cs:(k + 1) * cs], w_ref[k], (((1,), (1,)), ((), ())),
                                        preferred_element_type=F32)
        xv = x_ref[...]
        dx, dg = _norm_bwd(xv, _rstd(xv), g_ref[...], d_h)
        _acc_row(gg_ref, 0, dg)
        dx = dx + dy_ref[...]
        dx_ref[...] = dx
        dxb_ref[...] = dx.astype(BF16)

    return _rowwise(body, "ffn_bwd_in", x2.shape[0], tm, rows=[dpre, x2, dy], fulls=[g, w_up],
                    outs=[(D_MODEL, F32), (D_MODEL, BF16)], accs=[((SUBLANES, D_MODEL), F32)])


def _mem_attn_bwd(x1, dx2, dx2b, g, w_q, g_q, km, vm, w_o, tm):
    scale = MEM_HD ** -0.5
    m = km.shape[0]

    def body(i, n, x1_ref, dx2_ref, dx2b_ref, g_ref, wq_ref, gq_ref, km_ref, vm_ref, wo_ref,
             dx1_ref, dx1b_ref, hm_ref, dqr_ref, dkm_ref, dvm_ref, gg_ref, ggq_ref):
        _zero_first(i, dkm_ref, dvm_ref, gg_ref, ggq_ref)
        x1v = x1_ref[...]
        hm, heads = _mem_attn_core(x1v, g_ref, wq_ref, gq_ref, km_ref, vm_ref)
        hm_ref[...] = hm
        cs = D_MODEL // N_CHIPS
        d_o = jnp.zeros((x1v.shape[0], MEM_W), F32)
        for k in range(N_CHIPS):
            d_o = d_o + lax.dot_general(dx2b_ref[:, k * cs:(k + 1) * cs], wo_ref[k], (((1,), (1,)), ((), ())),
                                        preferred_element_type=F32)
        for h in range(MEM_HEADS):
            sl = slice(h * MEM_HD, (h + 1) * MEM_HD)
            qh, rs, qn, p, _ = heads[h]
            d_oh = d_o[:, sl].astype(BF16)
            dp = lax.dot_general(d_oh, vm_ref[:, sl], (((1,), (1,)), ((), ())), preferred_element_type=F32)
            ds = (p * (dp - jnp.sum(dp * p, axis=-1, keepdims=True)) * scale).astype(BF16)
            dqn = jnp.dot(ds, km_ref[:, sl], preferred_element_type=F32)
            dkm_ref[:, sl] += lax.dot_general(ds, qn, (((0,), (0,)), ((), ())), preferred_element_type=F32)
            dvm_ref[:, sl] += lax.dot_general(p.astype(BF16), d_oh, (((0,), (0,)), ((), ())), preferred_element_type=F32)
            dqh, dgq = _norm_bwd(qh, rs, gq_ref[...], dqn)
            _acc_row(ggq_ref, 0, dgq)
            dqr_ref[:, sl] = dqh.astype(BF16)
        d_hm = lax.dot_general(dqr_ref[...], wq_ref[...], (((1,), (1,)), ((), ())), preferred_element_type=F32)
        dx, dg = _norm_bwd(x1v, _rstd(x1v), g_ref[...], d_hm)
        _acc_row(gg_ref, 0, dg)
        dx = dx + dx2_ref[...]
        dx1_ref[...] = dx
        dx1b_ref[...] = dx.astype(BF16)

    return _rowwise(body, "mem_attn_bwd", x1.shape[0], tm, rows=[x1, dx2, dx2b], fulls=[g, w_q, g_q, km, vm, w_o],
                    outs=[(D_MODEL, F32), (D_MODEL, BF16), (D_MODEL, BF16), (MEM_W, BF16)],
                    accs=[((m, MEM_W), F32), ((m, MEM_W), F32), ((SUBLANES, D_MODEL), F32), ((SUBLANES, MEM_HD), F32)])


def _mem_kv_bwd(mem, g_mem, w_kv, g_k, dkm, dvm):
    m = mem.shape[0]

    def body(i, n, mem_ref, dkm_ref, dvm_ref, g_ref, w_ref, gk_ref, gw_ref, gg_ref, ggk_ref, dkv_ref):
        gg_ref[...] = jnp.zeros_like(gg_ref)
        ggk_ref[...] = jnp.zeros_like(ggk_ref)
        mv = mem_ref[...]
        rs_m = _rstd(mv)
        mem_n = (mv * rs_m * g_ref[...]).astype(BF16)
        kv = jnp.dot(mem_n, w_ref[...], preferred_element_type=F32)
        for h in range(MEM_HEADS):
            sl = slice(h * MEM_HD, (h + 1) * MEM_HD)
            kh = kv[:, sl]
            dkh, dgk = _norm_bwd(kh, _rstd(kh), gk_ref[...], dkm_ref[:, sl])
            _acc_row(ggk_ref, 0, dgk)
            dkv_ref[:, sl] = dkh.astype(BF16)
        dkv_ref[:, MEM_W:] = dvm_ref[...].astype(BF16)
        gw_ref[...] = lax.dot_general(mem_n, dkv_ref[...], (((0,), (0,)), ((), ())), preferred_element_type=F32)
        d_mn = lax.dot_general(dkv_ref[...], w_ref[...], (((1,), (1,)), ((), ())), preferred_element_type=F32)
        _acc_row(gg_ref, 0, d_mn * (mv * rs_m))

    return _rowwise(body, "mem_kv_bwd", m, m, rows=[mem, dkm, dvm], fulls=[g_mem, w_kv, g_k],
                    accs=[((D_MODEL, 2 * MEM_W), F32), ((SUBLANES, D_MODEL), F32), ((SUBLANES, MEM_HD), F32),
                          ((m, 2 * MEM_W), BF16)])


def _mix_out_bwd(dx1b, hf, hb, yg, o, g_lru, g_mla, w_out, tm):
    def body(i, n, dx_ref, hf_ref, hb_ref, yg_ref, o_ref, gl, gm, w_ref, dh_ref, dyg_ref, dob_ref, dl_ref, ggl_ref, ggm_ref):
        _zero_first(i, ggl_ref, ggm_ref)
        dmix = lax.dot_general(dx_ref[...], w_ref[...], (((1,), (1,)), ((), ())), preferred_element_type=F32)
        hs = hf_ref[...] + hb_ref[...]
        ygv = yg_ref[...]
        ge = _gelu(ygv)
        lo = hs * ge
        d_lo, dgl = _norm_bwd(lo, _rstd(lo), gl[...], dmix[:, :LRU_W])
        _acc_row(ggl_ref, 0, dgl)
        dh_ref[...] = d_lo * ge
        dyg_ref[...] = (d_lo * hs * _gelu_grad(ygv)).astype(BF16)
        ov = o_ref[...]
        d_o, dgm = _norm_bwd(ov, _rstd(ov), gm[...], dmix[:, LRU_W:])
        _acc_row(ggm_ref, 0, dgm)
        dob_ref[...] = d_o.astype(BF16)
        prod = d_o * ov
        lane_w = lax.broadcasted_iota(jnp.int32, prod.shape, 1)
        lane = lax.broadcasted_iota(jnp.int32, (prod.shape[0], LANES), 1)
        dl = jnp.zeros((prod.shape[0], LANES), F32)
        for h in range(HEADS):
            in_head = (lane_w >= h * V_DIM) & (lane_w < (h + 1) * V_DIM)
            dl = dl + jnp.where(lane == h, jnp.sum(jnp.where(in_head, prod, 0.0), axis=-1, keepdims=True), 0.0)
        dl_ref[...] = dl

    return _rowwise(body, "mix_out_bwd", dx1b.shape[0], tm, rows=[dx1b, hf, hb, yg, o], fulls=[g_lru, g_mla, w_out],
                    outs=[(LRU_W, F32), (LRU_W, BF16), (MLA_W, BF16), (LANES, F32)],
                    accs=[((SUBLANES, LRU_W), F32), ((SUBLANES, MLA_W), F32)])


def _mla_qkv_bwd(cq, ckv, krp, cos_t, sin_t, dq, dk, dv, g_qa, g_kva, g_qn, g_kn, w_uq_p, w_uk_p, w_uv, tm):
    scale = QK_HEAD ** -0.5

    def body(i, n, cq_ref, ckv_ref, kr_ref, c_ref, s_ref, dq_ref, dk_ref, dv_ref, gqa, gkva, gqn, gkn, wq, wk, wv,
             dcq_ref, dckv_ref, dkr_ref, cqb_ref, dqr_ref, ckvb_ref, dkn_ref, dvb_ref, ggqa, ggkva, ggqn, ggkn):
        _zero_first(i, ggqa, ggkva, ggqn, ggkn)
        cosv, sinv = c_ref[...], s_ref[...]
        cqv = cq_ref[...]
        rs_q = _rstd(cqv)
        cqb_ref[...] = (cqv * rs_q * gqa[...]).astype(BF16)
        qr = jnp.dot(cqb_ref[...], wq[...], preferred_element_type=F32)
        ckvv = ckv_ref[...]
        rs_kv = _rstd(ckvv)
        ckvb_ref[...] = (ckvv * rs_kv * gkva[...]).astype(BF16)
        kn = jnp.dot(ckvb_ref[...], wk[...], preferred_element_type=F32)
        kr = kr_ref[...]
        dkr = jnp.zeros_like(kr)
        for h in range(HEADS):
            sl = slice(h * LANES, (h + 1) * LANES)
            qh = qr[:, sl]
            d_qn = _rope_t(dq_ref[:, sl] * scale, cosv, sinv)
            dqh, dgq = _norm_bwd(qh, _rstd(qh, QK_HEAD), gqn[...], d_qn, QK_HEAD)
            _acc_row(ggqn, 0, dgq)
            dqr_ref[:, sl] = dqh.astype(BF16)
            kh = kn[:, sl] + kr
            d_kn = _rope_t(dk_ref[:, sl] * (1.0 / LOG2E), cosv, sinv)
            dkh, dgk = _norm_bwd(kh, _rstd(kh, QK_HEAD), gkn[...], d_kn, QK_HEAD)
            _acc_row(ggkn, 0, dgk)
            dkn_ref[:, sl] = dkh.astype(BF16)
            dkr = dkr + dkh
        dkr_ref[...] = dkr.astype(BF16)
        dvb_ref[...] = dv_ref[...].astype(BF16)
        d_cq = lax.dot_general(dqr_ref[...], wq[...], (((1,), (1,)), ((), ())), preferred_element_type=F32)
        dcq, dg = _norm_bwd(cqv, rs_q, gqa[...], d_cq)
        _acc_row(ggqa, 0, dg)
        dcq_ref[...] = dcq.astype(BF16)
        d_ckv = (lax.dot_general(dkn_ref[...], wk[...], (((1,), (1,)), ((), ())), preferred_element_type=F32)
                 + lax.dot_general(dvb_ref[...], wv[...], (((1,), (1,)), ((), ())), preferred_element_type=F32))
        dckv, dg = _norm_bwd(ckvv, rs_kv, gkva[...], d_ckv)
        _acc_row(ggkva, 0, dg)
        dckv_ref[...] = dckv.astype(BF16)

    return _rowwise(body, "mla_qkv_bwd", cq.shape[0], tm, rows=[cq, ckv, krp, cos_t, sin_t, dq, dk, dv],
                    fulls=[g_qa, g_kva, g_qn, g_kn, w_uq_p, w_uk_p, w_uv],
                    outs=[(Q_LORA, BF16), (KV_LORA, BF16), (LANES, BF16), (Q_LORA, BF16), (HEADS * LANES, BF16),
                          (KV_LORA, BF16), (HEADS * LANES, BF16), (MLA_W, BF16)],
                    accs=[((SUBLANES, Q_LORA), F32), ((SUBLANES, KV_LORA), F32), ((SUBLANES, LANES), F32),
                          ((SUBLANES, LANES), F32)])


def _in_proj_bwd(x, dx1, dxr_f, dxr_b, dyg, dcq, dckv, dkrp, g, w_in_p, tm):
    def body(i, n, x_ref, dx1_ref, df_ref, db_ref, dyg_ref, dcq_ref, dckv_ref, dkr_ref, g_ref, w_ref, gx_ref, dp_ref, gg_ref):
        _zero_first(i, gg_ref)
        dp_ref[:, :LRU_W] = (df_ref[...].astype(F32) + db_ref[...].astype(F32)).astype(BF16)
        dp_ref[:, LRU_W:2 * LRU_W] = dyg_ref[...].astype(BF16)
        dp_ref[:, 2 * LRU_W:2 * LRU_W + Q_LORA] = dcq_ref[...].astype(BF16)
        dp_ref[:, 2 * LRU_W + Q_LORA:OFF_KR] = dckv_ref[...].astype(BF16)
        dp_ref[:, OFF_KR:] = dkr_ref[...].astype(BF16)
        d_h = lax.dot_general(dp_ref[...], w_ref[...], (((1,), (1,)), ((), ())), preferred_element_type=F32)
        xv = x_ref[...]
        dx, dg = _norm_bwd(xv, _rstd(xv), g_ref[...], d_h)
        _acc_row(gg_ref, 0, dg)
        gx_ref[...] = dx + dx1_ref[...]

    return _rowwise(body, "in_proj_bwd", x.shape[0], tm, rows=[x, dx1, dxr_f, dxr_b, dyg, dcq, dckv, dkrp],
                    fulls=[g, w_in_p], outs=[(D_MODEL, F32), (IN_PAD, BF16)], accs=[((SUBLANES, D_MODEL), F32)])


ANY = pl.BlockSpec(memory_space=pl.ANY)


def _chip_peers(x, y):
    return ((1 - x, y), (x, 1 - y), (1 - x, 1 - y))


def _exchange_call(kern, name, ins, out_shapes, n_sems, aliases=None):
    return pl.pallas_call(
        kern, in_specs=[ANY] * len(ins), out_specs=[ANY] * len(out_shapes), out_shape=out_shapes,
        scratch_shapes=[pltpu.SemaphoreType.DMA((n,)) for n in n_sems], input_output_aliases=aliases or {},
        name=name)(*ins)


def _start_then_wait(copies):
    for cp in copies:
        cp.start()
    for cp in copies:
        cp.wait()


N_DEV = 8
RELATIONS = tuple((dx, dy, dc) for dx in (0, 1) for dy in (0, 1) for dc in (0, 1))[1:]


def _flip(v, d):
    return 1 - v if d else v


def _gather_copies(ins, outs, ssem, rsem, lsem):
    x, y, c = lax.axis_index("x"), lax.axis_index("y"), lax.axis_index("c")
    me = 2 * x + y
    cps = []
    for i, (a, o) in enumerate(zip(ins, outs)):
        cps.append(pltpu.make_async_copy(a, o.at[me], lsem.at[i]))
        for j, (px, py) in enumerate(_chip_peers(x, y)):
            cps.append(pltpu.make_async_remote_copy(a, o.at[me], ssem.at[3 * i + j], rsem.at[3 * i + j],
                                                    device_id=(px, py, c), device_id_type=MESH))
    return cps


def _gather_shapes(arrs):
    return [jax.ShapeDtypeStruct((N_CHIPS,) + a.shape, a.dtype) for a in arrs]


def _gather_sems(n):
    return [pltpu.SemaphoreType.DMA((3 * n,)), pltpu.SemaphoreType.DMA((3 * n,)), pltpu.SemaphoreType.DMA((n,))]


def _gather_chips(arrs):
    n = len(arrs)

    def kern(*refs):
        _start_then_wait(_gather_copies(refs[:n], refs[n:2 * n], *refs[2 * n:]))

    return _exchange_call(kern, "gather_weights", arrs, _gather_shapes(arrs), (3 * n, 3 * n, n))


def _to_owner_copies(ins, outs, ssem, rsem, lsem):
    x, y, c = lax.axis_index("x"), lax.axis_index("y"), lax.axis_index("c")
    me = 4 * x + 2 * y + c
    cps = []
    for i, (a, o) in enumerate(zip(ins, outs)):
        cps.append(pltpu.make_async_copy(a.at[2 * x + y, c], o.at[me], lsem.at[i]))
        for r, (dx, dy, dc) in enumerate(RELATIONS):
            tx, ty, tc = _flip(x, dx), _flip(y, dy), _flip(c, dc)
            cps.append(pltpu.make_async_remote_copy(a.at[2 * tx + ty, tc], o.at[me], ssem.at[7 * i + r], rsem.at[7 * i + r],
                                                    device_id=(tx, ty, tc), device_id_type=MESH))
    return cps


def _to_owner_shapes(arrs):
    return [jax.ShapeDtypeStruct((N_DEV,) + a.shape[2:], a.dtype) for a in arrs]


def _to_owner_sems(n):
    return [pltpu.SemaphoreType.DMA((7 * n,)), pltpu.SemaphoreType.DMA((7 * n,)), pltpu.SemaphoreType.DMA((n,))]


def _to_owner(arrs, name):
    n = len(arrs)

    def kern(*refs):
        _start_then_wait(_to_owner_copies(refs[:n], refs[n:2 * n], *refs[2 * n:]))

    return _exchange_call(kern, name, arrs, _to_owner_shapes(arrs), (7 * n, 7 * n, n))


def _join_halves(arrs):
    n = len(arrs)

    def kern(*refs):
        outs, (ssem, rsem) = refs[n:2 * n], refs[2 * n:]
        x, y, c = lax.axis_index("x"), lax.axis_index("y"), lax.axis_index("c")
        _start_then_wait([
            pltpu.make_async_remote_copy(outs[i].at[c], outs[i].at[c], ssem.at[i], rsem.at[i],
                                         device_id=(x, y, 1 - c), device_id_type=MESH) for i in range(n)])

    outs = [jax.ShapeDtypeStruct(a.shape, a.dtype) for a in arrs]
    return _exchange_call(kern, "grad_join_halves", arrs, outs, (n, n), aliases={i: i for i in range(n)})


def _row_block(rows, row_bytes, limit=1 << 20):
    best = None
    for d in range(16, rows + 1, 16):
        if rows % d == 0 and d * row_bytes <= limit:
            best = d
    return best if best is not None else rows


def _sum_devices(b, c, name):
    _, h, cols = b.shape
    hb = _row_block(h, cols * 4)

    def kern(c_ref, b_ref, o_ref):
        acc = b_ref[0].astype(F32)
        for j in range(1, N_DEV):
            acc = acc + b_ref[j].astype(F32)
        o_ref[...] = acc

    return pl.pallas_call(
        kern,
        grid_spec=pltpu.PrefetchScalarGridSpec(
            num_scalar_prefetch=1, grid=(h // hb,),
            in_specs=[pl.BlockSpec((N_DEV, hb, cols), lambda i, c_ref: (0, i, 0))],
            out_specs=pl.BlockSpec((None, hb, cols), lambda i, c_ref: (c_ref[0], i, 0))),
        out_shape=jax.ShapeDtypeStruct((2, h, cols), F32), name=name, compiler_params=_cparams(1))(c, b)


def _adamw(w, g, m, v, name):
    rows, cols = w.shape
    rb = _row_block(rows, cols * 4)
    c1 = 1.0 - ADAM_B1 ** ADAM_STEP
    c2 = 1.0 - ADAM_B2 ** ADAM_STEP

    def kern(w_ref, g_ref, m_ref, v_ref, d_ref, mo_ref, vo_ref):
        gv = g_ref[...]
        mn = ADAM_B1 * m_ref[...] + (1.0 - ADAM_B1) * gv
        vn = ADAM_B2 * v_ref[...] + (1.0 - ADAM_B2) * (gv * gv)
        mo_ref[...] = mn
        vo_ref[...] = vn
        d_ref[...] = (-ADAM_LR) * ((mn / c1) / (jnp.sqrt(vn / c2) + ADAM_EPS) + ADAM_WD * w_ref[...])

    spec = pl.BlockSpec((rb, cols), lambda i: (i, 0))
    return pl.pallas_call(
        kern, grid=(rows // rb,), in_specs=[spec] * 4, out_specs=[spec] * 3,
        out_shape=[jax.ShapeDtypeStruct(w.shape, F32)] * 3, name=name, compiler_params=_cparams(1))(w, g, m, v)


def _pad_rows(flat, rows):
    return jnp.pad(flat, (0, rows * LANES - flat.shape[0])).reshape(rows, LANES)


def _round_up(n, m):
    return (n + m - 1) // m * m


def _shard_shape(shape, axis):
    return tuple(s // N_CHIPS if a == axis else s for a, s in enumerate(shape))


def _to_shards(full, axis):
    shape = full.shape
    t = full.reshape(shape[:axis] + (N_CHIPS, shape[axis] // N_CHIPS) + shape[axis + 1:])
    return jnp.moveaxis(t, axis, 0).reshape(N_CHIPS, -1)


def _from_shards(sh, shape, axis):
    t = sh.reshape((N_CHIPS,) + _shard_shape(shape, axis))
    t = jnp.moveaxis(t, 0, axis)
    return t.reshape(shape)


BIG = tuple((name, shape, axis) for name, shape, axis, big in SHARDED if big)
EARLY_WEIGHTS = ("w_in", "w_uq", "w_ukv")
SMALL_SHARDED = tuple((name, shape, axis) for name, shape, axis, big in SHARDED if not big)


def _pack_small_weights(p):
    flat = jnp.concatenate([p[name].reshape(-1) for name, _, _ in SMALL_SHARDED])
    return _pad_rows(flat, _round_up(-(-flat.shape[0] // LANES), SUBLANES))


def _unpack_small_weights(gathered):
    flat = gathered.reshape(N_CHIPS, -1)
    out, off = {}, 0
    for name, shape, axis in SMALL_SHARDED:
        n = _numel(shape) // N_CHIPS
        out[name] = _from_shards(flat[:, off:off + n], shape, axis)
        off += n
    return out


def _pack_small_local(p, prefix=""):
    parts = [p[prefix + name].reshape(-1) for name, _, _ in SMALL_SHARDED]
    parts += [p[prefix + name].reshape(-1) for name, _ in REPLICATED]
    return jnp.concatenate(parts)


def _pack_small_grads(g, loss_part):
    parts = [_to_shards(g[name], axis) for name, _, axis in SMALL_SHARDED]
    rep = jnp.concatenate([g[name].reshape(-1) for name, _ in REPLICATED] + [loss_part.reshape(1)])
    parts.append(jnp.broadcast_to(rep[None], (N_CHIPS, rep.shape[0])))
    return jnp.concatenate(parts, axis=1)


def _unpack_small_local(flat):
    out, off = {}, 0
    for name, shape, axis in SMALL_SHARDED:
        n = _numel(shape) // N_CHIPS
        out[name] = flat[off:off + n].reshape((1,) + _shard_shape(shape, axis))
        off += n
    for name, shape in REPLICATED:
        n = _numel(shape)
        out[name] = flat[off:off + n].reshape((1,) + shape)
        off += n
    return out


def _grad_shards(g, shape, axis):
    if axis == 0:
        return g.reshape((N_CHIPS,) + _shard_shape(shape, axis))
    return jnp.transpose(g.reshape(shape[0], N_CHIPS, shape[1] // N_CHIPS), (1, 0, 2))


def _cols_from_shards(w4):
    return jnp.transpose(w4, (1, 0, 2)).reshape(w4.shape[1], -1)


def _block_diag(w):
    eye = jnp.eye(LRU_BLOCKS, dtype=w.dtype)
    return jnp.einsum("ncd,nm->ncmd", w, eye).reshape(LRU_W, LRU_W)


def _block_diag_t(g):
    g4 = g.reshape(LRU_BLOCKS, 64, LRU_BLOCKS, 64)
    eye = jnp.eye(LRU_BLOCKS, dtype=g.dtype)[:, None, :, None]
    return jnp.sum(g4 * eye, axis=2)


def _pad8(a):
    return jnp.pad(a, ((0, SUBLANES - a.shape[0]), (0, 0)))


def kernel(x, mem, positions, attn_norm, w_in, lru_conv_w, lru_conv_b, lru_w_a, lru_b_a, lru_w_i, lru_b_i, lru_lambda, q_a_norm, w_uq, kv_a_norm, w_ukv, mla_q_norm, mla_k_norm, lru_out_norm, mla_out_norm, w_out, mem_attn_norm, mem_norm, w_mem_q, w_mem_kv, mem_q_norm, mem_k_norm, w_mem_o, ffn_norm, w_up, ffn_conv_w, ffn_conv_b, w_down, loss_target, m_attn_norm, m_w_in, m_lru_conv_w, m_lru_conv_b, m_lru_w_a, m_lru_b_a, m_lru_w_i, m_lru_b_i, m_lru_lambda, m_q_a_norm, m_w_uq, m_kv_a_norm, m_w_ukv, m_mla_q_norm, m_mla_k_norm, m_lru_out_norm, m_mla_out_norm, m_w_out, m_mem_attn_norm, m_mem_norm, m_w_mem_q, m_w_mem_kv, m_mem_q_norm, m_mem_k_norm, m_w_mem_o, m_ffn_norm, m_w_up, m_ffn_conv_w, m_ffn_conv_b, m_w_down, v_attn_norm, v_w_in, v_lru_conv_w, v_lru_conv_b, v_lru_w_a, v_lru_b_a, v_lru_w_i, v_lru_b_i, v_lru_lambda, v_q_a_norm, v_w_uq, v_kv_a_norm, v_w_ukv, v_mla_q_norm, v_mla_k_norm, v_lru_out_norm, v_mla_out_norm, v_w_out, v_mem_attn_norm, v_mem_norm, v_w_mem_q, v_w_mem_kv, v_mem_q_norm, v_mem_k_norm, v_w_mem_o, v_ffn_norm, v_w_up, v_ffn_conv_w, v_ffn_conv_b, v_w_down):
    given = dict(locals())
    local = {name: given[name][0] for name in WEIGHT_ORDER}
    s = x.shape[1]
    x2d, mem2d, tgt = x[0], mem[0], loss_target[0]
    tm = min(512, s)
    tm_wide = min(1024, s)
    tm_ffn = min(256, s)
    t_scan = min(1024, s)
    tq_f, tq_b, tk = min(4096, s), min(2048, s), min(512, s)

    early = [b for b in BIG if b[0] in EARLY_WEIGHTS]
    late = [b for b in BIG if b[0] not in EARLY_WEIGHTS]
    got = _gather_chips([local[name].astype(BF16) for name, _, _ in early] + [_pack_small_weights(local)])
    full = _unpack_small_weights(got[-1])

    def take_gathered(entries, arrays):
        for (name, shape, axis), w4 in zip(entries, arrays):
            if axis == 0:
                full[name] = w4.reshape(shape)
            elif name in ("w_up", "w_mem_o"):
                full[name] = w4
            else:
                full[name] = _cols_from_shards(w4)

    take_gathered(early, got)
    row = lambda a: a.reshape(1, -1)
    b16 = lambda a: a.astype(BF16)
    zeros = lambda r, c: jnp.zeros((r, c), BF16)
    w_in_f = full["w_in"]
    w_in_p = jnp.concatenate([w_in_f[:, :OFF_KR], _head_tile(zeros(D_MODEL, QK_NOPE), w_in_f[:, OFF_KR:])], axis=1)
    uq = full["w_uq"].reshape(Q_LORA, HEADS, QK_HEAD)
    w_uq_p = _head_tile(uq[:, :, :QK_NOPE], uq[:, :, QK_NOPE:]).reshape(Q_LORA, -1)
    ukv = full["w_ukv"].reshape(KV_LORA, HEADS, QK_NOPE + V_DIM)
    w_uk_p = _head_tile(ukv[:, :, :QK_NOPE], None).reshape(KV_LORA, -1)
    w_uv = ukv[:, :, QK_NOPE:].reshape(KV_LORA, MLA_W)
    wa = [b16(_block_diag(local["lru_w_a"][d])) for d in range(2)]
    wi = [b16(_block_diag(local["lru_w_i"][d])) for d in range(2)]
    cw = [_pad8(full["lru_conv_w"][d]) for d in range(2)]
    pv = [_pad8(jnp.stack([full["lru_conv_b"][d], full["lru_b_a"][d], full["lru_b_i"][d], full["lru_lambda"][d]]))
          for d in range(2)]
    ffn_cw = _pad8(jnp.concatenate([full["ffn_conv_w"], row(local["ffn_conv_b"])], axis=0))
    g_attn, g_qa, g_kva = row(local["attn_norm"]), row(local["q_a_norm"]), row(local["kv_a_norm"])
    g_qn = _head_tile(row(local["mla_q_norm"])[:, :QK_NOPE], row(local["mla_q_norm"])[:, QK_NOPE:])
    g_kn = _head_tile(row(local["mla_k_norm"])[:, :QK_NOPE], row(local["mla_k_norm"])[:, QK_NOPE:])
    g_lru, g_mla = row(local["lru_out_norm"]), row(local["mla_out_norm"])
    g_memattn, g_mem = row(local["mem_attn_norm"]), row(local["mem_norm"])
    g_mq, g_mk, g_ffn = row(local["mem_q_norm"]), row(local["mem_k_norm"]), row(local["ffn_norm"])

    inv = ROPE_THETA ** (-jnp.arange(0, QK_ROPE, 2, dtype=F32) / QK_ROPE)
    no_nope = jnp.zeros((1, QK_NOPE), F32)
    inv_tile = _head_tile(no_nope, jnp.concatenate([inv, inv])[None])
    sign_tile = _head_tile(no_nope, jnp.concatenate([-jnp.ones_like(inv), jnp.ones_like(inv)])[None])
    ang = positions[0].astype(F32)[:, None] * inv_tile
    cos_t, sin_t = jnp.cos(ang), jnp.sin(ang) * sign_tile

    xr, yg, cq, ckv, krp, hb_in = _in_proj(x2d, g_attn, w_in_p, tm_wide)
    h_f, *saved_f = _lru_scan_fwd(xr, cw[0], pv[0], wa[0], wi[0], False, t_scan)
    h_b, *saved_b = _lru_scan_fwd(xr, cw[1], pv[1], wa[1], wi[1], True, t_scan)
    q, k, v = _mla_qkv(cq, ckv, krp, cos_t, sin_t, g_qa, g_kva, g_qn, g_kn, w_uq_p, w_uk_p, w_uv, tm_wide)
    o, lse, *got = _attn_fwd(q, k, v, tq_f, tk, shards=[local[name].astype(BF16) for name, _, _ in late])
    take_gathered(late, got)
    x1, mixed = _mix_out(h_f, h_b, yg, o, x2d, g_lru, g_mla, full["w_out"], tm_wide)
    km, vm = _mem_kv(mem2d, g_mem, full["w_mem_kv"], g_mk)
    x2, o_mem = _mem_attn(x1, g_memattn, full["w_mem_q"], g_mq, km, vm, full["w_mem_o"], tm_wide)
    gu_pre, hb_ffn = _ffn_up(x2, g_ffn, full["w_up"], tm)
    dy, dyb, act, dgu, loss_acc = _ffn_down_loss(gu_pre, x2, tgt, ffn_cw, full["w_down"], tm_ffn)

    grads = {}
    grads["w_down"] = _matmul_tn(act, dyb, "grad_w_down", out_dtype=BF16)
    dpre, g_conv = _ffn_bwd_conv(dgu, gu_pre, ffn_cw, tm_ffn)
    grads["ffn_conv_w"], grads["ffn_conv_b"] = g_conv[:3], g_conv[3]
    grads["w_up"] = _matmul_tn(hb_ffn, dpre, "grad_w_up", col_shards=True, out_dtype=BF16)
    dx2, dx2b, gg = _ffn_bwd_in(dpre, x2, dy, g_ffn, full["w_up"], tm)
    grads["ffn_norm"] = gg[0]
    grads["w_mem_o"] = _matmul_tn(o_mem, dx2b, "grad_w_mem_o", out_dtype=BF16)
    dx1, dx1b, hm, dqr_mem, dkm, dvm, gg, ggq = _mem_attn_bwd(x1, dx2, dx2b, g_memattn, full["w_mem_q"], g_mq, km, vm,
                                                                 full["w_mem_o"], tm)
    grads["mem_attn_norm"], grads["mem_q_norm"] = gg[0], ggq[0]
    grads["w_mem_q"] = _matmul_tn(hm, dqr_mem, "grad_w_mem_q", out_dtype=BF16)
    g_mem_kv, gg, ggk, _ = _mem_kv_bwd(mem2d, g_mem, full["w_mem_kv"], g_mk, dkm, dvm)
    grads["w_mem_kv"] = g_mem_kv.astype(BF16)
    grads["mem_norm"], grads["mem_k_norm"] = gg[0], ggk[0]
    grads["w_out"] = _matmul_tn(mixed, dx1b, "grad_w_out", out_dtype=BF16)
    dh, dyg, dob, dl128, ggl, ggm = _mix_out_bwd(dx1b, h_f, h_b, yg, o, g_lru, g_mla, full["w_out"], tm_wide)
    grads["lru_out_norm"], grads["mla_out_norm"] = ggl[0], ggm[0]
    delta_t = jnp.transpose(dl128[:, :HEADS]).reshape(HEADS // 2, 2, s)
    def halves(name, shape, axis):
        g4 = grads[name] if grads[name].ndim == 3 else _grad_shards(grads[name], shape, axis)
        return g4.reshape(N_CHIPS, 2, g4.shape[1] // 2, g4.shape[2])

    dq, dk, dv, *arrived_late = _attn_bwd(q, k, v, dob, lse, delta_t, tq_b, tk,
                                          contributions=[halves(*e) for e in late])
    (dcq, dckv, dkrp, cqb, dqr, ckvb, dkn, dvb, ggqa, ggkva, ggqn, ggkn) = _mla_qkv_bwd(
        cq, ckv, krp, cos_t, sin_t, dq, dk, dv, g_qa, g_kva, g_qn, g_kn, w_uq_p, w_uk_p, w_uv, tm_wide)
    grads["q_a_norm"], grads["kv_a_norm"] = ggqa[0], ggkva[0]
    grads["mla_q_norm"] = jnp.concatenate(_from_head_tile(ggqn[0]))
    grads["mla_k_norm"] = jnp.concatenate(_from_head_tile(ggkn[0]))
    g_uq_p = _matmul_tn(cqb, dqr, "grad_w_uq")
    grads["w_uq"] = jnp.concatenate(_from_head_tile(g_uq_p.reshape(Q_LORA, HEADS, LANES)), axis=-1).reshape(Q_LORA, -1)
    g_uk_p = _from_head_tile(_matmul_tn(ckvb, dkn, "grad_w_uk").reshape(KV_LORA, HEADS, LANES))[0]
    g_uv = _matmul_tn(ckvb, dvb, "grad_w_uv").reshape(KV_LORA, HEADS, V_DIM)
    grads["w_ukv"] = jnp.concatenate([g_uk_p, g_uv], axis=2).reshape(KV_LORA, -1)
    dxr, gwa, gwi, gvec = [], [], [], []
    for d, (hd, saved) in enumerate(((h_f, saved_f), (h_b, saved_b))):
        r = _lru_scan_bwd(xr, saved, hd, dh, cw[d], pv[d], wa[d], wi[d], d == 1, t_scan)
        dxr.append(r[0])
        gwa.append(_block_diag_t(r[1]))
        gwi.append(_block_diag_t(r[2]))
        gvec.append(r[3])
    grads["lru_w_a"], grads["lru_w_i"] = jnp.stack(gwa), jnp.stack(gwi)
    grads["lru_conv_w"] = jnp.stack([gv[:CONV_W] for gv in gvec])
    for r_i, name in ((4, "lru_conv_b"), (5, "lru_b_a"), (6, "lru_b_i"), (7, "lru_lambda")):
        grads[name] = jnp.stack([gv[r_i] for gv in gvec])
    grad_x, dproj, gg = _in_proj_bwd(x2d, dx1, dxr[0], dxr[1], dyg, dcq, dckv, dkrp, g_attn, w_in_p, tm_wide)
    grads["attn_norm"] = gg[0]
    g_in_p = _matmul_tn(hb_in, dproj, "grad_w_in")
    grads["w_in"] = jnp.concatenate([g_in_p[:, :OFF_KR], _from_head_tile(g_in_p[:, OFF_KR:])[1]], axis=1)

    small = _pack_small_grads(grads, loss_acc[0, 0] * (0.5 / D_MODEL))
    length = small.shape[1]
    hrows = _round_up(-(-length // (2 * LANES)), 16)
    small = jnp.pad(small, ((0, 0), (0, 2 * hrows * LANES - length))).reshape(N_CHIPS, 2, hrows, LANES)
    for name, _, _ in early:
        grads[name] = grads[name].astype(BF16)
    arrived_early = _to_owner([halves(*e) for e in early] + [small], "grad_to_owner")
    names = [name for name, _, _ in late + early] + ["small"]
    c_idx = lax.axis_index("c").astype(jnp.int32).reshape(1)
    reduced = _join_halves([_sum_devices(b, c_idx, "grad_sum_" + n)
                            for n, b in zip(names, list(arrived_late) + list(arrived_early))])

    outs = [{}, {}, {}, {}]
    for (name, shape, axis), r in zip(late + early, reduced):
        g2 = r.reshape(_shard_shape(shape, axis))
        res = _adamw(local[name], g2, given["m_" + name][0], given["v_" + name][0], "adamw_" + name)
        for o_, a in zip(outs, (g2, *res)):
            o_[name] = a[None]
    pack = lambda prefix: _pad_rows(_pack_small_local({n: given[prefix + n] for n in WEIGHT_ORDER}), 2 * hrows)
    g_small = reduced[-1].reshape(2 * hrows, LANES)
    res = _adamw(pack(""), g_small, pack("m_"), pack("v_"), "adamw_small")
    for o_, a in zip(outs, (g_small, *res)):
        o_.update(_unpack_small_local(a.reshape(-1)))
    loss = g_small.reshape(-1)[length - 1]
    return (loss, grad_x[None], *[o_[n] for o_ in outs for n in WEIGHT_ORDER])
```

```python
import jax
import jax.numpy as jnp
from jax import lax
from jax.experimental import pallas as pl
from jax.experimental.pallas import tpu as pltpu

F32, BF16 = jnp.float32, jnp.bfloat16
MESH = pl.DeviceIdType.MESH

D_MODEL = 1024
EPS = 1e-6
LRU_W = 512
LRU_BLOCKS = 8
LRU_C = 8.0
CONV_W = 4
HEADS = 8
QK_NOPE, QK_ROPE, QK_HEAD, V_DIM = 64, 32, 96, 64
Q_LORA, KV_LORA = 256, 128
MLA_W = HEADS * V_DIM
ROPE_THETA = 10000.0
IN_COLS = 2 * LRU_W + Q_LORA + KV_LORA + QK_ROPE
OFF_KR = IN_COLS - QK_ROPE
IN_PAD = 1536
MEM_HEADS, MEM_HD = 4, 128
MEM_W = MEM_HEADS * MEM_HD
D_FF = 2816
N_CHIPS = 4
ADAM_LR, ADAM_B1, ADAM_B2, ADAM_EPS, ADAM_WD, ADAM_STEP = 0.001, 0.9, 0.999, 1e-08, 0.01, 10

LANES = 128
SUBLANES = 8
V7X_VMEM_BYTES = 64 * 1024 * 1024
VMEM_LIMIT = V7X_VMEM_BYTES * 7 // 8

SHARDED = (
    ("w_in", (D_MODEL, IN_COLS), 1, True),
    ("lru_conv_w", (2, CONV_W, LRU_W), 2, False),
    ("lru_conv_b", (2, LRU_W), 1, False),
    ("lru_b_a", (2, LRU_W), 1, False),
    ("lru_b_i", (2, LRU_W), 1, False),
    ("lru_lambda", (2, LRU_W), 1, False),
    ("w_uq", (Q_LORA, HEADS * QK_HEAD), 1, True),
    ("w_ukv", (KV_LORA, HEADS * (QK_NOPE + V_DIM)), 1, True),
    ("w_out", (2 * LRU_W, D_MODEL), 0, True),
    ("w_mem_q", (D_MODEL, MEM_W), 0, True),
    ("w_mem_kv", (D_MODEL, 2 * MEM_W), 0, True),
    ("w_mem_o", (MEM_W, D_MODEL), 1, True),
    ("w_up", (D_MODEL, 2 * D_FF), 1, True),
    ("ffn_conv_w", (3, 2 * D_FF), 1, False),
    ("w_down", (D_FF, D_MODEL), 0, True),
)
REPLICATED = (
    ("attn_norm", (D_MODEL,)), ("lru_w_a", (2, LRU_BLOCKS, 64, 64)), ("lru_w_i", (2, LRU_BLOCKS, 64, 64)),
    ("q_a_norm", (Q_LORA,)), ("kv_a_norm", (KV_LORA,)), ("mla_q_norm", (QK_HEAD,)), ("mla_k_norm", (QK_HEAD,)),
    ("lru_out_norm", (LRU_W,)), ("mla_out_norm", (MLA_W,)), ("mem_attn_norm", (D_MODEL,)), ("mem_norm", (D_MODEL,)),
    ("mem_q_norm", (MEM_HD,)), ("mem_k_norm", (MEM_HD,)), ("ffn_norm", (D_MODEL,)), ("ffn_conv_b", (2 * D_FF,)),
)
WEIGHT_ORDER = ('attn_norm', 'w_in', 'lru_conv_w', 'lru_conv_b', 'lru_w_a', 'lru_b_a', 'lru_w_i', 'lru_b_i', 'lru_lambda',
                'q_a_norm', 'w_uq', 'kv_a_norm', 'w_ukv', 'mla_q_norm', 'mla_k_norm', 'lru_out_norm', 'mla_out_norm', 'w_out',
                'mem_attn_norm', 'mem_norm', 'w_mem_q', 'w_mem_kv', 'mem_q_norm', 'mem_k_norm', 'w_mem_o', 'ffn_norm', 'w_up',
                'ffn_conv_w', 'ffn_conv_b', 'w_down')


def _numel(shape):
    n = 1
    for s in shape:
        n *= s
    return n


def _cparams(n_axes):
    return pltpu.CompilerParams(dimension_semantics=("arbitrary",) * n_axes, vmem_limit_bytes=VMEM_LIMIT)


def _bdot(a, b):
    return jnp.dot(a.astype(BF16), b.astype(BF16), preferred_element_type=F32)


def _bdot_nt(a, b):
    return lax.dot_general(a.astype(BF16), b.astype(BF16), (((1,), (1,)), ((), ())), preferred_element_type=F32)


def _bdot_tn(a, b):
    return lax.dot_general(a.astype(BF16), b.astype(BF16), (((0,), (0,)), ((), ())), preferred_element_type=F32)


def _rstd(x, n=None):
    n = x.shape[-1] if n is None else n
    return lax.rsqrt(jnp.sum(x * x, axis=-1, keepdims=True) * (1.0 / n) + EPS)


def _norm_bwd(x, rs, g, dy, n=None):
    n = x.shape[-1] if n is None else n
    xhat = x * rs
    dxh = dy * g
    dx = rs * (dxh - xhat * (jnp.sum(dxh * xhat, axis=-1, keepdims=True) * (1.0 / n)))
    return dx, dy * xhat


def _acc_row(ref, r, val):
    ref[r:r + 1, :] += jnp.sum(val, axis=0, keepdims=True)


def _zero_first(i, *refs):
    @pl.when(i == 0)
    def _():
        for r in refs:
            r[...] = jnp.zeros_like(r)


def _shift_down(x, j, halo):
    if j == 0:
        return x
    xs = pltpu.roll(x, j, 0)
    hs = pltpu.roll(halo, j, 0)
    row = lax.broadcasted_iota(jnp.int32, hs.shape, 0)
    top = jnp.where(row < j, hs, xs[:SUBLANES])
    return jnp.concatenate([top, xs[SUBLANES:]], axis=0)


def _shift_up(x, j, halo):
    if j == 0:
        return x
    t = x.shape[0]
    xs = pltpu.roll(x, t - j, 0)
    hs = pltpu.roll(halo, SUBLANES - j, 0)
    row = lax.broadcasted_iota(jnp.int32, hs.shape, 0)
    bot = jnp.where(row >= SUBLANES - j, hs, xs[t - SUBLANES:])
    return jnp.concatenate([xs[:t - SUBLANES], bot], axis=0)


def _shift(x, j, halo, down):
    return _shift_down(x, j, halo) if down else _shift_up(x, j, halo)


def _scan(a, b, h_in, down):
    t, c = a.shape
    g = t // SUBLANES
    a3, b3 = a.reshape(g, SUBLANES, c), b.reshape(g, SUBLANES, c)
    sub = lax.broadcasted_iota(jnp.int32, a3.shape, 1)
    d = 1
    while d < SUBLANES:
        keep = (sub >= d) if down else (sub < SUBLANES - d)
        shift = d if down else SUBLANES - d
        a_s = jnp.where(keep, pltpu.roll(a3, shift, 1), 1.0)
        b_s = jnp.where(keep, pltpu.roll(b3, shift, 1), 0.0)
        b3 = a3 * b_s + b3
        a3 = a3 * a_s
        d *= 2
    hs = [None] * g
    carry = h_in
    for i in (range(g) if down else range(g - 1, -1, -1)):
        hs[i] = a3[i] * carry + b3[i]
        carry = hs[i][SUBLANES - 1:, :] if down else hs[i][:1, :]
    return jnp.concatenate(hs, axis=0)


def _sigmoid(x):
    return 0.5 * jnp.tanh(0.5 * x) + 0.5


LOG2E = 1.4426950408889634
GELU_K = 0.7978845608028654
GELU_C = 0.044715


def _gelu(x):
    return 0.5 * x * (1.0 + jnp.tanh(GELU_K * (x + GELU_C * x * x * x)))


def _gelu_grad(x):
    t = jnp.tanh(GELU_K * (x + GELU_C * x * x * x))
    return 0.5 * (1.0 + t) + 0.5 * x * (1.0 - t * t) * GELU_K * (1.0 + 3.0 * GELU_C * x * x)


ROPE_HALF = QK_ROPE // 2
ROPE_LANE = 32


def _head_tile(nope, rope):
    z = lambda n: jnp.zeros(nope.shape[:-1] + (n,), nope.dtype)
    r1, r2 = (z(ROPE_HALF), z(ROPE_HALF)) if rope is None else (rope[..., :ROPE_HALF], rope[..., ROPE_HALF:])
    return jnp.concatenate([nope[..., :ROPE_LANE], r1, nope[..., ROPE_LANE:], z(ROPE_HALF), r2, z(ROPE_HALF)], axis=-1)


def _from_head_tile(t):
    a, b = ROPE_LANE + ROPE_HALF, ROPE_LANE + LANES // 2
    return (jnp.concatenate([t[..., :ROPE_LANE], t[..., a:a + QK_NOPE - ROPE_LANE]], axis=-1),
            jnp.concatenate([t[..., ROPE_LANE:a], t[..., b:b + ROPE_HALF]], axis=-1))


def _rope_partner(x):
    lane = lax.broadcasted_iota(jnp.int32, x.shape, 1) & (LANES // 2 - 1)
    return jnp.where((lane >= ROPE_LANE) & (lane < ROPE_LANE + ROPE_HALF), pltpu.roll(x, LANES // 2, 1), 0.0)


def _rope(x, cos_t, sin_t):
    return x * cos_t + _rope_partner(x) * sin_t


def _rope_t(dy, cos_t, sin_t):
    return dy * cos_t + _rope_partner(dy * sin_t)


def _rowwise(body, name, s, tm, rows=(), halos=(), fulls=(), outs=(), accs=()):
    n = s // tm
    in_specs, args = [], []
    for a in rows:
        in_specs.append(pl.BlockSpec((tm, a.shape[1]), lambda i: (i, 0)))
        args.append(a)
    for a in halos:
        hr = 2 * SUBLANES if a.dtype == BF16 else SUBLANES
        in_specs.append(pl.BlockSpec((hr, a.shape[1]), lambda i, hr=hr: (jnp.maximum(i * (tm // hr) - 1, 0), 0)))
        in_specs.append(pl.BlockSpec((hr, a.shape[1]), lambda i, hr=hr: (jnp.minimum((i + 1) * (tm // hr), s // hr - 1), 0)))
        args += [a, a]
    for a in fulls:
        in_specs.append(pl.BlockSpec(a.shape, lambda i, nd=a.ndim: (0,) * nd))
        args.append(a)
    out_shape, out_specs = [], []
    for c, dt in outs:
        out_shape.append(jax.ShapeDtypeStruct((s, c), dt))
        out_specs.append(pl.BlockSpec((tm, c), lambda i: (i, 0)))
    for shp, dt in accs:
        out_shape.append(jax.ShapeDtypeStruct(shp, dt))
        out_specs.append(pl.BlockSpec(shp, lambda i, nd=len(shp): (0,) * nd))

    def kern(*refs):
        body(pl.program_id(0), n, *refs)

    return pl.pallas_call(kern, grid=(n,), in_specs=in_specs, out_specs=out_specs, out_shape=out_shape, name=name,
                          compiler_params=_cparams(1))(*args)


def _matmul_tn(a, b, name, col_shards=False, out_dtype=F32):
    t, m = a.shape
    n = b.shape[1]
    bm = m
    for cand in range(LANES, m + 1, LANES):
        if m % cand == 0 and cand * (n // N_CHIPS if col_shards else min(n, 2048)) * 4 <= 6 * 1024 * 1024:
            bm = cand
    bn = n // N_CHIPS if col_shards else (n if n <= 2048 else 1408)
    bt = min(t, 2048)
    nt = t // bt

    def kern(a_ref, b_ref, o_ref, acc_ref):
        k = pl.program_id(2)

        @pl.when(k == 0)
        def _():
            acc_ref[...] = jnp.zeros_like(acc_ref)
        acc_ref[...] += _bdot_tn(a_ref[...], b_ref[...])

        @pl.when(k == nt - 1)
        def _():
            o_ref[...] = acc_ref[...].astype(out_dtype)

    if col_shards:
        out_spec = pl.BlockSpec((None, bm, bn), lambda i, j, k: (j, i, 0))
        out_shape = jax.ShapeDtypeStruct((N_CHIPS, m, bn), out_dtype)
    else:
        out_spec = pl.BlockSpec((bm, bn), lambda i, j, k: (i, j))
        out_shape = jax.ShapeDtypeStruct((m, n), out_dtype)
    return pl.pallas_call(
        kern, grid=(m // bm, n // bn, nt),
        in_specs=[pl.BlockSpec((bt, bm), lambda i, j, k: (k, i)), pl.BlockSpec((bt, bn), lambda i, j, k: (k, j))],
        out_specs=out_spec, out_shape=out_shape, scratch_shapes=[pltpu.VMEM((bm, bn), F32)], name=name,
        compiler_params=_cparams(3))(a, b)


def _in_proj(x, g, w_in_p, tm):
    def body(i, n, x_ref, g_ref, w_ref, xr, yg, cq, ckv, krp, hb):
        xv = x_ref[...]
        h = (xv * _rstd(xv) * g_ref[...]).astype(BF16)
        hb[...] = h
        p = jnp.dot(h, w_ref[...], preferred_element_type=F32)
        xr[...] = p[:, :LRU_W]
        yg[...] = p[:, LRU_W:2 * LRU_W]
        cq[...] = p[:, 2 * LRU_W:2 * LRU_W + Q_LORA]
        ckv[...] = p[:, 2 * LRU_W + Q_LORA:OFF_KR]
        krp[...] = p[:, OFF_KR:IN_PAD]

    return _rowwise(body, "in_proj", x.shape[0], tm, rows=[x], fulls=[g, w_in_p],
                    outs=[(LRU_W, F32), (LRU_W, F32), (Q_LORA, F32), (KV_LORA, F32), (LANES, F32), (D_MODEL, BF16)])


def _softplus_neg(lam):
    e = jnp.exp(-jnp.abs(lam))
    return jnp.maximum(-lam, 0.0) + jnp.where(e < 1e-2, e * (1.0 - e * (0.5 - e * (1.0 / 3.0))), jnp.log(1.0 + e))


def _lru_gates(x, halo, cw_ref, pv_ref, wa_ref, wi_ref, rev):
    down = not rev
    xc = pv_ref[0:1, :] + jnp.zeros_like(x)
    for j in range(CONV_W):
        k = j if rev else CONV_W - 1 - j
        xc = xc + cw_ref[k:k + 1, :] * _shift(x, j, halo, down)
    r = _sigmoid(_bdot(xc, wa_ref[...]) + pv_ref[1:2, :])
    ig = _sigmoid(_bdot(xc, wi_ref[...]) + pv_ref[2:3, :])
    lam = pv_ref[3:4, :]
    sp = _softplus_neg(lam)
    log_a = (-LRU_C) * r * sp
    a = jnp.exp(log_a)
    z = 2.0 * log_a
    series = -(z * (1.0 + z * (0.5 + z * (1.0 / 6.0 + z * (1.0 / 24.0)))))
    om = jnp.where(z > -0.02, series, 1.0 - a * a)
    mult = jnp.sqrt(om)
    return xc, r, ig, sp, a, mult


def _lru_scan_fwd(xr, cw, pv, wa, wi, rev, t):
    s = xr.shape[0]
    n = s // t
    hb = t // SUBLANES
    last8 = s // SUBLANES - 1
    down = not rev

    def kern(x_ref, halo_ref, cw_ref, pv_ref, wa_ref, wi_ref, h_ref, xc_ref, r_ref, ig_ref, a_ref, mult_ref, carry_ref):
        i = pl.program_id(0)
        _zero_first(i, carry_ref)
        halo = jnp.where(i == 0, 0.0, halo_ref[...])
        xc, r, ig, sp, a, mult = _lru_gates(x_ref[...], halo, cw_ref, pv_ref, wa_ref, wi_ref, rev)
        xc_ref[...], r_ref[...], ig_ref[...], a_ref[...], mult_ref[...] = xc, r, ig, a, mult
        h_ref[...] = _scan(a, mult * ig * xc, carry_ref[...], down)
        carry_ref[...] = h_ref[pl.ds(t - 1 if down else 0, 1), :]

    if rev:
        blk = lambda i: (n - 1 - i, 0)
        hal = lambda i: (jnp.minimum((n - i) * hb, last8), 0)
    else:
        blk = lambda i: (i, 0)
        hal = lambda i: (jnp.maximum(i * hb - 1, 0), 0)
    full = lambda a: pl.BlockSpec(a.shape, lambda i: (0, 0))
    return pl.pallas_call(
        kern, grid=(n,),
        in_specs=[pl.BlockSpec((t, LRU_W), blk), pl.BlockSpec((SUBLANES, LRU_W), hal), full(cw), full(pv), full(wa), full(wi)],
        out_specs=[pl.BlockSpec((t, LRU_W), blk)] * 6, out_shape=[jax.ShapeDtypeStruct((s, LRU_W), F32)] * 6,
        scratch_shapes=[pltpu.VMEM((1, LRU_W), F32)], name="lru_scan_rev" if rev else "lru_scan_fwd",
        compiler_params=_cparams(1))(xr, xr, cw, pv, wa, wi)


def _lru_scan_bwd(xr, saved, h, dh, cw, pv, wa, wi, rev, t):
    s = xr.shape[0]
    n = s // t
    hb = t // SUBLANES
    last8 = s // SUBLANES - 1
    down = not rev

    def kern(x_ref, xc_ref, r_ref, ig_ref, a_ref, mult_ref, h_ref, hh_ref, dh_ref, cw_ref, pv_ref, wa_ref, wi_ref,
             dx_ref, gwa_ref, gwi_ref, gv_ref, p_ref, dxc_halo_ref, tmp_ref):
        i = pl.program_id(0)
        _zero_first(i, gwa_ref, gwi_ref, gv_ref, p_ref, dxc_halo_ref)
        at_start = i == n - 1
        x = x_ref[...]
        hhalo = jnp.where(at_start, 0.0, hh_ref[...])
        xc, r, ig, a, mult = xc_ref[...], r_ref[...], ig_ref[...], a_ref[...], mult_ref[...]
        lam = pv_ref[3:4, :]
        sp = _softplus_neg(lam)
        h_prev = _shift(h_ref[...], 1, hhalo, down)
        row = lax.broadcasted_iota(jnp.int32, x.shape, 0)
        edge = t - 1 if down else 0
        dh_mod = dh_ref[...] + jnp.where(row == edge, p_ref[...], 0.0)
        a_next = _shift(a, 1, jnp.zeros((SUBLANES, LRU_W), F32), not down)
        g = _scan(a_next, dh_mod, jnp.zeros((1, LRU_W), F32), not down)
        tmp_ref[...] = a * g
        p_ref[...] = tmp_ref[pl.ds(0 if down else t - 1, 1), :]
        da = g * h_prev
        d_ig = g * mult * xc
        d_xc = g * mult * ig
        d_om = g * ig * xc * (0.5 / jnp.maximum(mult, 1e-30))
        d_log_a = da * a - 2.0 * d_om * a * a
        d_r = d_log_a * ((-LRU_C) * sp)
        d_sp = jnp.sum(d_log_a * ((-LRU_C) * r), axis=0, keepdims=True)
        gv_ref[7:8, :] += d_sp * (-_sigmoid(-lam))
        d_ga = d_r * r * (1.0 - r)
        d_gi = d_ig * ig * (1.0 - ig)
        _acc_row(gv_ref, 5, d_ga)
        _acc_row(gv_ref, 6, d_gi)
        d_xc = d_xc + _bdot_nt(d_ga, wa_ref[...]) + _bdot_nt(d_gi, wi_ref[...])
        gwa_ref[...] += _bdot_tn(xc, d_ga)
        gwi_ref[...] += _bdot_tn(xc, d_gi)
        _acc_row(gv_ref, 4, d_xc)
        dx = jnp.zeros_like(x)
        dxc_halo = dxc_halo_ref[...]
        for j in range(CONV_W):
            k = j if rev else CONV_W - 1 - j
            d_shift = _shift(d_xc, j, dxc_halo, not down)
            _acc_row(gv_ref, k, d_shift * x)
            dx = dx + cw_ref[k:k + 1, :] * d_shift
        dx_ref[...] = dx.astype(BF16)
        dxc_halo_ref[...] = d_xc[:SUBLANES] if down else d_xc[t - SUBLANES:]

    if rev:
        blk = lambda i: (i, 0)
        hal = lambda i: (jnp.minimum((i + 1) * hb, last8), 0)
    else:
        blk = lambda i: (n - 1 - i, 0)
        hal = lambda i: (jnp.maximum((n - 1 - i) * hb - 1, 0), 0)
    full = lambda a: pl.BlockSpec(a.shape, lambda i: (0, 0))
    bs = pl.BlockSpec((t, LRU_W), blk)
    hs = pl.BlockSpec((SUBLANES, LRU_W), hal)
    return pl.pallas_call(
        kern, grid=(n,),
        in_specs=[bs] * 7 + [hs, bs, full(cw), full(pv), full(wa), full(wi)],
        out_specs=[bs, pl.BlockSpec((LRU_W, LRU_W), lambda i: (0, 0)), pl.BlockSpec((LRU_W, LRU_W), lambda i: (0, 0)),
                   pl.BlockSpec((SUBLANES, LRU_W), lambda i: (0, 0))],
        out_shape=[jax.ShapeDtypeStruct((s, LRU_W), BF16), jax.ShapeDtypeStruct((LRU_W, LRU_W), F32),
                   jax.ShapeDtypeStruct((LRU_W, LRU_W), F32), jax.ShapeDtypeStruct((SUBLANES, LRU_W), F32)],
        scratch_shapes=[pltpu.VMEM((1, LRU_W), F32), pltpu.VMEM((SUBLANES, LRU_W), F32), pltpu.VMEM((t, LRU_W), F32)],
        name="lru_bwd_rev" if rev else "lru_bwd_fwd", compiler_params=_cparams(1))(xr, *saved, h, h, dh, cw, pv, wa, wi)


def _mla_qkv(cq, ckv, krp, cos_t, sin_t, g_qa, g_kva, g_qn, g_kn, w_uq_p, w_uk_p, w_uv, tm):
    scale = QK_HEAD ** -0.5 * LOG2E

    def body(i, n, cq_ref, ckv_ref, kr_ref, c_ref, s_ref, gqa, gkva, gqn, gkn, wq, wk, wv, q_out, k_out, v_out):
        cosv, sinv = c_ref[...], s_ref[...]
        cqv = cq_ref[...]
        qr = _bdot(cqv * _rstd(cqv) * gqa[...], wq[...])
        ckvv = ckv_ref[...]
        c_kv = (ckvv * _rstd(ckvv) * gkva[...]).astype(BF16)
        kn = jnp.dot(c_kv, wk[...], preferred_element_type=F32)
        v_out[...] = jnp.dot(c_kv, wv[...], preferred_element_type=F32).astype(BF16)
        kr = kr_ref[...]
        kr_swapped = _rope_partner(kr * gkn[...]) * sinv
        for h in range(HEADS):
            sl = slice(h * LANES, (h + 1) * LANES)
            qh = qr[:, sl]
            qh = _rope(qh * _rstd(qh, QK_HEAD) * gqn[...], cosv, sinv) * scale
            q_out[:, sl] = qh.astype(BF16)
            kh = kn[:, sl] + kr
            rs = _rstd(kh, QK_HEAD)
            k_out[:, sl] = (kh * rs * gkn[...] * cosv + kr_swapped * rs).astype(BF16)

    return _rowwise(body, "mla_qkv", cq.shape[0], tm, rows=[cq, ckv, krp, cos_t, sin_t],
                    fulls=[g_qa, g_kva, g_qn, g_kn, w_uq_p, w_uk_p, w_uv],
                    outs=[(HEADS * LANES, BF16), (HEADS * LANES, BF16), (MLA_W, BF16)])


NT_DIMS = (((1,), (1,)), ((), ()))
TN_DIMS = (((0,), (0,)), ((), ()))


def _riding_exchange(copies_fn, first, last):
    @pl.when(first)
    def _():
        for cp in copies_fn():
            cp.start()

    def finish():
        @pl.when(last)
        def _():
            for cp in copies_fn():
                cp.wait()
    return finish


def _attn_fwd(q, k, v, tq, tk, shards=()):
    s = q.shape[0]
    nq, nk = s // tq, s // tk
    n = len(shards)

    def kern(*refs):
        q_ref, k_ref, v_ref = refs[:3]
        o_ref, lse_ref = refs[3 + n:5 + n]
        acc_ref = refs[5 + 2 * n]
        p_id, i_id = pl.program_id(0), pl.program_id(1)
        finish = _riding_exchange(lambda: _gather_copies(refs[3:3 + n], refs[5 + n:5 + 2 * n], *refs[6 + 2 * n:]),
                                  (p_id == 0) & (i_id == 0), (p_id == HEADS // 2 - 1) & (i_id == nq - 1)) if n else None
        qs = (q_ref[:, :LANES], q_ref[:, LANES:])
        acc_ref[...] = jnp.zeros_like(acc_ref)

        def step(j, carry):
            off = pl.multiple_of(j * tk, tk)
            vc = v_ref[pl.ds(off, tk), :]
            out = []
            for h in range(2):
                m, l = carry[2 * h:2 * h + 2]
                st = lax.dot_general(k_ref[pl.ds(off, tk), h * LANES:(h + 1) * LANES], qs[h], NT_DIMS,
                                     preferred_element_type=F32)
                mn = jnp.maximum(m, jnp.max(st, axis=0, keepdims=True))
                al = jnp.exp2(m - mn)
                pt = jnp.exp2(st - mn)
                l = al * l + jnp.sum(pt, axis=0, keepdims=True)
                acc_ref[h] = al * acc_ref[h] + lax.dot_general(vc, pt.astype(BF16), TN_DIMS, preferred_element_type=F32)
                out += [mn, l]
            return tuple(out)

        init = (jnp.full((1, tq), -1e30, F32), jnp.zeros((1, tq), F32)) * 2
        m0, l0, m1, l1 = lax.fori_loop(0, nk, step, init)
        row = lax.broadcasted_iota(jnp.int32, (LANES, tq), 0)
        o_ref[...] = jnp.where(row < V_DIM, acc_ref[0] / l0, acc_ref[1] / l1).T
        lse_ref[0, 0:1, :] = m0 + jnp.log2(l0)
        lse_ref[0, 1:2, :] = m1 + jnp.log2(l1)
        if n:
            finish()

    return pl.pallas_call(
        kern, grid=(HEADS // 2, nq),
        in_specs=[pl.BlockSpec((tq, 2 * LANES), lambda p, i: (i, p)), pl.BlockSpec((s, 2 * LANES), lambda p, i: (0, p)),
                  pl.BlockSpec((s, LANES), lambda p, i: (0, p))] + [ANY] * n,
        out_specs=[pl.BlockSpec((tq, LANES), lambda p, i: (i, p)), pl.BlockSpec((1, 2, tq), lambda p, i: (p, 0, i))]
        + [ANY] * n,
        out_shape=[jax.ShapeDtypeStruct((s, MLA_W), F32), jax.ShapeDtypeStruct((HEADS // 2, 2, s), F32)]
        + _gather_shapes(shards),
        scratch_shapes=[pltpu.VMEM((2, LANES, tq), F32)] + (_gather_sems(n) if n else []),
        name="attn_fwd", compiler_params=_cparams(2))(q, k, v, *shards)


def _attn_bwd(q, k, v, do, lse, delta, tq, tk, contributions=()):
    s = q.shape[0]
    nq, nk = s // tq, s // tk
    n = len(contributions)

    def kern(*refs):
        q_ref, do_ref, lse_ref, dl_ref, k_ref, v_ref = refs[:6]
        dq_ref, dk_ref, dv_ref = refs[6 + n:9 + n]
        acc_ref = refs[9 + 2 * n]
        p_id, i_id = pl.program_id(0), pl.program_id(1)
        finish = _riding_exchange(lambda: _to_owner_copies(refs[6:6 + n], refs[9 + n:9 + 2 * n], *refs[10 + 2 * n:]),
                                  (p_id == 0) & (i_id == 0), (p_id == HEADS // 2 - 1) & (i_id == nq - 1)) if n else None
        _zero_first(pl.program_id(1), dk_ref, dv_ref)
        acc_ref[...] = jnp.zeros_like(acc_ref)
        qs = (q_ref[:, :LANES], q_ref[:, LANES:])
        doc = do_ref[...]
        lane_q = lax.broadcasted_iota(jnp.int32, (tq, LANES), 1)
        zq = jnp.zeros_like(doc)
        dos = (jnp.where(lane_q < V_DIM, doc, zq), jnp.where(lane_q >= V_DIM, doc, zq))
        lses = (lse_ref[0, 0:1, :], lse_ref[0, 1:2, :])
        dls = (dl_ref[0, 0:1, :], dl_ref[0, 1:2, :])

        def step(j, carry):
            off = pl.multiple_of(j * tk, tk)
            vp = v_ref[pl.ds(off, tk), :]
            lane_k = lax.broadcasted_iota(jnp.int32, (tk, LANES), 1)
            zero = jnp.zeros_like(vp)
            vs = (jnp.where(lane_k < V_DIM, vp, zero), jnp.where(lane_k >= V_DIM, vp, zero))
            for h in range(2):
                sl = slice(h * LANES, (h + 1) * LANES)
                st = lax.dot_general(k_ref[pl.ds(off, tk), sl], qs[h], NT_DIMS, preferred_element_type=F32)
                pt = jnp.exp2(st - lses[h])
                dpt = lax.dot_general(vs[h], doc, NT_DIMS, preferred_element_type=F32)
                dst = (pt * (dpt - dls[h])).astype(BF16)
                dv_ref[pl.ds(off, tk), :] += jnp.dot(pt.astype(BF16), dos[h], preferred_element_type=F32)
                dk_ref[pl.ds(off, tk), sl] += jnp.dot(dst, qs[h], preferred_element_type=F32)
                acc_ref[h] += lax.dot_general(k_ref[pl.ds(off, tk), sl], dst, TN_DIMS, preferred_element_type=F32)
            return carry

        lax.fori_loop(0, nk, step, 0)
        dq_ref[:, :LANES] = acc_ref[0].T
        dq_ref[:, LANES:] = acc_ref[1].T
        if n:
            finish()

    return pl.pallas_call(
        kern, grid=(HEADS // 2, nq),
        in_specs=[pl.BlockSpec((tq, 2 * LANES), lambda p, i: (i, p)), pl.BlockSpec((tq, LANES), lambda p, i: (i, p)),
                  pl.BlockSpec((1, 2, tq), lambda p, i: (p, 0, i)), pl.BlockSpec((1, 2, tq), lambda p, i: (p, 0, i)),
                  pl.BlockSpec((s, 2 * LANES), lambda p, i: (0, p)), pl.BlockSpec((s, LANES), lambda p, i: (0, p))]
        + [ANY] * n,
        out_specs=[pl.BlockSpec((tq, 2 * LANES), lambda p, i: (i, p)), pl.BlockSpec((s, 2 * LANES), lambda p, i: (0, p)),
                   pl.BlockSpec((s, LANES), lambda p, i: (0, p))] + [ANY] * n,
        out_shape=[jax.ShapeDtypeStruct((s, HEADS * LANES), F32), jax.ShapeDtypeStruct((s, HEADS * LANES), F32),
                   jax.ShapeDtypeStruct((s, MLA_W), F32)] + _to_owner_shapes(contributions),
        scratch_shapes=[pltpu.VMEM((2, LANES, tq), F32)] + (_to_owner_sems(n) if n else []),
        name="attn_bwd", compiler_params=_cparams(2))(q, do, lse, delta, k, v, *contributions)


def _mix_out(hf, hb, yg, o, x, g_lru, g_mla, w_out, tm):
    def body(i, n, hf_ref, hb_ref, yg_ref, o_ref, x_ref, gl, gm, w_ref, x1_ref, mix_ref):
        lo = (hf_ref[...] + hb_ref[...]) * _gelu(yg_ref[...])
        ov = o_ref[...]
        mix_ref[:, :LRU_W] = (lo * _rstd(lo) * gl[...]).astype(BF16)
        mix_ref[:, LRU_W:] = (ov * _rstd(ov) * gm[...]).astype(BF16)
        x1_ref[...] = x_ref[...] + jnp.dot(mix_ref[...], w_ref[...], preferred_element_type=F32)

    return _rowwise(body, "mix_out", x.shape[0], tm, rows=[hf, hb, yg, o, x], fulls=[g_lru, g_mla, w_out],
                    outs=[(D_MODEL, F32), (2 * LRU_W, BF16)])


def _mem_kv(mem, g_mem, w_kv, g_k):
    m = mem.shape[0]

    def body(i, n, mem_ref, g_ref, w_ref, gk_ref, km_ref, vm_ref):
        mv = mem_ref[...]
        kv = _bdot(mv * _rstd(mv) * g_ref[...], w_ref[...])
        vm_ref[...] = kv[:, MEM_W:].astype(BF16)
        for h in range(MEM_HEADS):
            sl = slice(h * MEM_HD, (h + 1) * MEM_HD)
            kh = kv[:, sl]
            km_ref[:, sl] = (kh * _rstd(kh) * gk_ref[...]).astype(BF16)

    return _rowwise(body, "mem_kv", m, m, rows=[mem], fulls=[g_mem, w_kv, g_k], outs=[(MEM_W, BF16), (MEM_W, BF16)])


def _mem_attn_core(x1v, g_ref, wq_ref, gq_ref, km_ref, vm_ref):
    scale = MEM_HD ** -0.5
    hm = (x1v * _rstd(x1v) * g_ref[...]).astype(BF16)
    qr = jnp.dot(hm, wq_ref[...], preferred_element_type=F32)
    heads = []
    for h in range(MEM_HEADS):
        sl = slice(h * MEM_HD, (h + 1) * MEM_HD)
        qh = qr[:, sl]
        rs = _rstd(qh)
        qn = (qh * rs * gq_ref[...]).astype(BF16)
        sc = lax.dot_general(qn, km_ref[:, sl], (((1,), (1,)), ((), ())), preferred_element_type=F32) * scale
        e = jnp.exp(sc - jnp.max(sc, axis=-1, keepdims=True))
        p = e / jnp.sum(e, axis=-1, keepdims=True)
        oh = jnp.dot(p.astype(BF16), vm_ref[:, sl], preferred_element_type=F32)
        heads.append((qh, rs, qn, p, oh))
    return hm, heads


def _mem_attn(x1, g, w_q, g_q, km, vm, w_o, tm):
    cs = D_MODEL // N_CHIPS

    def body(i, n, x1_ref, g_ref, wq_ref, gq_ref, km_ref, vm_ref, wo_ref, x2_ref, ob_ref):
        x1v = x1_ref[...]
        _, heads = _mem_attn_core(x1v, g_ref, wq_ref, gq_ref, km_ref, vm_ref)
        for h in range(MEM_HEADS):
            ob_ref[:, h * MEM_HD:(h + 1) * MEM_HD] = heads[h][4].astype(BF16)
        for k in range(N_CHIPS):
            sl = slice(k * cs, (k + 1) * cs)
            x2_ref[:, sl] = x1v[:, sl] + jnp.dot(ob_ref[...], wo_ref[k], preferred_element_type=F32)

    return _rowwise(body, "mem_attn", x1.shape[0], tm, rows=[x1], fulls=[g, w_q, g_q, km, vm, w_o],
                    outs=[(D_MODEL, F32), (MEM_W, BF16)])


def _ffn_up(x2, g, w_up, tm):
    cs = 2 * D_FF // N_CHIPS

    def body(i, n, x_ref, g_ref, w_ref, gu_ref, hb_ref):
        xv = x_ref[...]
        hb_ref[...] = (xv * _rstd(xv) * g_ref[...]).astype(BF16)
        for k in range(N_CHIPS):
            gu_ref[:, k * cs:(k + 1) * cs] = jnp.dot(hb_ref[...], w_ref[k], preferred_element_type=F32)

    return _rowwise(body, "ffn_up", x2.shape[0], tm, rows=[x2], fulls=[g, w_up], outs=[(2 * D_FF, F32), (D_MODEL, BF16)])


def _ffn_conv(gu, prev, nxt, cw_ref, i, n):
    prev = jnp.where(i == 0, 0.0, prev)
    nxt = jnp.where(i == n - 1, 0.0, nxt)
    return (cw_ref[3:4, :] + cw_ref[0:1, :] * _shift_down(gu, 1, prev) + cw_ref[1:2, :] * gu
            + cw_ref[2:3, :] * _shift_up(gu, 1, nxt))


def _ffn_down_loss(gu_pre, x2, target, cw, w_down, tm):
    def body(i, n, gu_ref, x_ref, t_ref, pv_ref, nx_ref, cw_ref, w_ref, dy_ref, dyb_ref, act_ref, dgu_ref, loss_ref):
        _zero_first(i, loss_ref)
        gu = _ffn_conv(gu_ref[...], pv_ref[...], nx_ref[...], cw_ref, i, n)
        g, u = gu[:, :D_FF], gu[:, D_FF:]
        sg = _sigmoid(g)
        a = g * sg
        act_ref[...] = (a * u).astype(BF16)
        y = x_ref[...] + jnp.dot(act_ref[...], w_ref[...], preferred_element_type=F32)
        e = y - t_ref[...]
        loss_ref[...] += jnp.sum(e * e)
        dy = e * (1.0 / D_MODEL)
        dy_ref[...] = dy
        dyb_ref[...] = dy.astype(BF16)
        d_act = lax.dot_general(dyb_ref[...], w_ref[...], NT_DIMS, preferred_element_type=F32)
        dgu_ref[:, :D_FF] = ((d_act * u) * (sg + a - a * sg)).astype(BF16)
        dgu_ref[:, D_FF:] = (d_act * a).astype(BF16)

    return _rowwise(body, "ffn_down_loss", x2.shape[0], tm, rows=[gu_pre, x2, target], halos=[gu_pre], fulls=[cw, w_down],
                    outs=[(D_MODEL, F32), (D_MODEL, BF16), (D_FF, BF16), (2 * D_FF, BF16)], accs=[((SUBLANES, LANES), F32)])


def _ffn_bwd_conv(dgu, gu_pre, cw, tm):
    def body(i, n, d_ref, g_ref, dp_ref, dn_ref, cw_ref, dpre_ref, gc_ref):
        _zero_first(i, gc_ref)
        d = d_ref[...].astype(F32)
        g = g_ref[...]
        d_next = _shift_up(d, 1, jnp.where(i == n - 1, 0.0, dn_ref[...].astype(F32)[:SUBLANES]))
        d_prev = _shift_down(d, 1, jnp.where(i == 0, 0.0, dp_ref[...].astype(F32)[SUBLANES:]))
        dpre_ref[...] = (cw_ref[0:1, :] * d_next + cw_ref[1:2, :] * d + cw_ref[2:3, :] * d_prev).astype(BF16)
        _acc_row(gc_ref, 0, d_next * g)
        _acc_row(gc_ref, 1, d * g)
        _acc_row(gc_ref, 2, d_prev * g)
        _acc_row(gc_ref, 3, d)

    return _rowwise(body, "ffn_bwd_conv", dgu.shape[0], tm, rows=[dgu, gu_pre], halos=[dgu], fulls=[cw],
                    outs=[(2 * D_FF, BF16)], accs=[((SUBLANES, 2 * D_FF), F32)])


def _ffn_bwd_in(dpre, x2, dy, g, w_up, tm):
    cs = 2 * D_FF // N_CHIPS

    def body(i, n, dp_ref, x_ref, dy_ref, g_ref, w_ref, dx_ref, dxb_ref, gg_ref):
        _zero_first(i, gg_ref)
        d_h = jnp.zeros(x_ref.shape, F32)
        for k in range(N_CHIPS):
            d_h = d_h + lax.dot_general(dp_ref[:, k * cs:(k + 1) * cs], w_ref[k], (((1,), (1,)), ((), ())),
                                        preferred_element_type=F32)
        xv = x_ref[...]
        dx, dg = _norm_bwd(xv, _rstd(xv), g_ref[...], d_h)
        _acc_row(gg_ref, 0, dg)
        dx = dx + dy_ref[...]
        dx_ref[...] = dx
        dxb_ref[...] = dx.astype(BF16)

    return _rowwise(body, "ffn_bwd_in", x2.shape[0], tm, rows=[dpre, x2, dy], fulls=[g, w_up],
                    outs=[(D_MODEL, F32), (D_MODEL, BF16)], accs=[((SUBLANES, D_MODEL), F32)])


def _mem_attn_bwd(x1, dx2, dx2b, g, w_q, g_q, km, vm, w_o, tm):
    scale = MEM_HD ** -0.5
    m = km.shape[0]

    def body(i, n, x1_ref, dx2_ref, dx2b_ref, g_ref, wq_ref, gq_ref, km_ref, vm_ref, wo_ref,
             dx1_ref, dx1b_ref, hm_ref, dqr_ref, dkm_ref, dvm_ref, gg_ref, ggq_ref):
        _zero_first(i, dkm_ref, dvm_ref, gg_ref, ggq_ref)
        x1v = x1_ref[...]
        hm, heads = _mem_attn_core(x1v, g_ref, wq_ref, gq_ref, km_ref, vm_ref)
        hm_ref[...] = hm
        cs = D_MODEL // N_CHIPS
        d_o = jnp.zeros((x1v.shape[0], MEM_W), F32)
        for k in range(N_CHIPS):
            d_o = d_o + lax.dot_general(dx2b_ref[:, k * cs:(k + 1) * cs], wo_ref[k], (((1,), (1,)), ((), ())),
                                        preferred_element_type=F32)
        for h in range(MEM_HEADS):
            sl = slice(h * MEM_HD, (h + 1) * MEM_HD)
            qh, rs, qn, p, _ = heads[h]
            d_oh = d_o[:, sl].astype(BF16)
            dp = lax.dot_general(d_oh, vm_ref[:, sl], (((1,), (1,)), ((), ())), preferred_element_type=F32)
            ds = (p * (dp - jnp.sum(dp * p, axis=-1, keepdims=True)) * scale).astype(BF16)
            dqn = jnp.dot(ds, km_ref[:, sl], preferred_element_type=F32)
            dkm_ref[:, sl] += lax.dot_general(ds, qn, (((0,), (0,)), ((), ())), preferred_element_type=F32)
            dvm_ref[:, sl] += lax.dot_general(p.astype(BF16), d_oh, (((0,), (0,)), ((), ())), preferred_element_type=F32)
            dqh, dgq = _norm_bwd(qh, rs, gq_ref[...], dqn)
            _acc_row(ggq_ref, 0, dgq)
            dqr_ref[:, sl] = dqh.astype(BF16)
        d_hm = lax.dot_general(dqr_ref[...], wq_ref[...], (((1,), (1,)), ((), ())), preferred_element_type=F32)
        dx, dg = _norm_bwd(x1v, _rstd(x1v), g_ref[...], d_hm)
        _acc_row(gg_ref, 0, dg)
        dx = dx + dx2_ref[...]
        dx1_ref[...] = dx
        dx1b_ref[...] = dx.astype(BF16)

    return _rowwise(body, "mem_attn_bwd", x1.shape[0], tm, rows=[x1, dx2, dx2b], fulls=[g, w_q, g_q, km, vm, w_o],
                    outs=[(D_MODEL, F32), (D_MODEL, BF16), (D_MODEL, BF16), (MEM_W, BF16)],
                    accs=[((m, MEM_W), F32), ((m, MEM_W), F32), ((SUBLANES, D_MODEL), F32), ((SUBLANES, MEM_HD), F32)])


def _mem_kv_bwd(mem, g_mem, w_kv, g_k, dkm, dvm):
    m = mem.shape[0]

    def body(i, n, mem_ref, dkm_ref, dvm_ref, g_ref, w_ref, gk_ref, gw_ref, gg_ref, ggk_ref, dkv_ref):
        gg_ref[...] = jnp.zeros_like(gg_ref)
        ggk_ref[...] = jnp.zeros_like(ggk_ref)
        mv = mem_ref[...]
        rs_m = _rstd(mv)
        mem_n = (mv * rs_m * g_ref[...]).astype(BF16)
        kv = jnp.dot(mem_n, w_ref[...], preferred_element_type=F32)
        for h in range(MEM_HEADS):
            sl = slice(h * MEM_HD, (h + 1) * MEM_HD)
            kh = kv[:, sl]
            dkh, dgk = _norm_bwd(kh, _rstd(kh), gk_ref[...], dkm_ref[:, sl])
            _acc_row(ggk_ref, 0, dgk)
            dkv_ref[:, sl] = dkh.astype(BF16)
        dkv_ref[:, MEM_W:] = dvm_ref[...].astype(BF16)
        gw_ref[...] = lax.dot_general(mem_n, dkv_ref[...], (((0,), (0,)), ((), ())), preferred_element_type=F32)
        d_mn = lax.dot_general(dkv_ref[...], w_ref[...], (((1,), (1,)), ((), ())), preferred_element_type=F32)
        _acc_row(gg_ref, 0, d_mn * (mv * rs_m))

    return _rowwise(body, "mem_kv_bwd", m, m, rows=[mem, dkm, dvm], fulls=[g_mem, w_kv, g_k],
                    accs=[((D_MODEL, 2 * MEM_W), F32), ((SUBLANES, D_MODEL), F32), ((SUBLANES, MEM_HD), F32),
                          ((m, 2 * MEM_W), BF16)])


def _mix_out_bwd(dx1b, hf, hb, yg, o, g_lru, g_mla, w_out, tm):
    def body(i, n, dx_ref, hf_ref, hb_ref, yg_ref, o_ref, gl, gm, w_ref, dh_ref, dyg_ref, dob_ref, dl_ref, ggl_ref, ggm_ref):
        _zero_first(i, ggl_ref, ggm_ref)
        dmix = lax.dot_general(dx_ref[...], w_ref[...], (((1,), (1,)), ((), ())), preferred_element_type=F32)
        hs = hf_ref[...] + hb_ref[...]
        ygv = yg_ref[...]
        ge = _gelu(ygv)
        lo = hs * ge
        d_lo, dgl = _norm_bwd(lo, _rstd(lo), gl[...], dmix[:, :LRU_W])
        _acc_row(ggl_ref, 0, dgl)
        dh_ref[...] = d_lo * ge
        dyg_ref[...] = (d_lo * hs * _gelu_grad(ygv)).astype(BF16)
        ov = o_ref[...]
        d_o, dgm = _norm_bwd(ov, _rstd(ov), gm[...], dmix[:, LRU_W:])
        _acc_row(ggm_ref, 0, dgm)
        dob_ref[...] = d_o.astype(BF16)
        prod = d_o * ov
        lane_w = lax.broadcasted_iota(jnp.int32, prod.shape, 1)
        lane = lax.broadcasted_iota(jnp.int32, (prod.shape[0], LANES), 1)
        dl = jnp.zeros((prod.shape[0], LANES), F32)
        for h in range(HEADS):
            in_head = (lane_w >= h * V_DIM) & (lane_w < (h + 1) * V_DIM)
            dl = dl + jnp.where(lane == h, jnp.sum(jnp.where(in_head, prod, 0.0), axis=-1, keepdims=True), 0.0)
        dl_ref[...] = dl

    return _rowwise(body, "mix_out_bwd", dx1b.shape[0], tm, rows=[dx1b, hf, hb, yg, o], fulls=[g_lru, g_mla, w_out],
                    outs=[(LRU_W, F32), (LRU_W, BF16), (MLA_W, BF16), (LANES, F32)],
                    accs=[((SUBLANES, LRU_W), F32), ((SUBLANES, MLA_W), F32)])


def _mla_qkv_bwd(cq, ckv, krp, cos_t, sin_t, dq, dk, dv, g_qa, g_kva, g_qn, g_kn, w_uq_p, w_uk_p, w_uv, tm):
    scale = QK_HEAD ** -0.5

    def body(i, n, cq_ref, ckv_ref, kr_ref, c_ref, s_ref, dq_ref, dk_ref, dv_ref, gqa, gkva, gqn, gkn, wq, wk, wv,
             dcq_ref, dckv_ref, dkr_ref, cqb_ref, dqr_ref, ckvb_ref, dkn_ref, dvb_ref, ggqa, ggkva, ggqn, ggkn):
        _zero_first(i, ggqa, ggkva, ggqn, ggkn)
        cosv, sinv = c_ref[...], s_ref[...]
        cqv = cq_ref[...]
        rs_q = _rstd(cqv)
        cqb_ref[...] = (cqv * rs_q * gqa[...]).astype(BF16)
        qr = jnp.dot(cqb_ref[...], wq[...], preferred_element_type=F32)
        ckvv = ckv_ref[...]
        rs_kv = _rstd(ckvv)
        ckvb_ref[...] = (ckvv * rs_kv * gkva[...]).astype(BF16)
        kn = jnp.dot(ckvb_ref[...], wk[...], preferred_element_type=F32)
        kr = kr_ref[...]
        dkr = jnp.zeros_like(kr)
        for h in range(HEADS):
            sl = slice(h * LANES, (h + 1) * LANES)
            qh = qr[:, sl]
            d_qn = _rope_t(dq_ref[:, sl] * scale, cosv, sinv)
            dqh, dgq = _norm_bwd(qh, _rstd(qh, QK_HEAD), gqn[...], d_qn, QK_HEAD)
            _acc_row(ggqn, 0, dgq)
            dqr_ref[:, sl] = dqh.astype(BF16)
            kh = kn[:, sl] + kr
            d_kn = _rope_t(dk_ref[:, sl] * (1.0 / LOG2E), cosv, sinv)
            dkh, dgk = _norm_bwd(kh, _rstd(kh, QK_HEAD), gkn[...], d_kn, QK_HEAD)
            _acc_row(ggkn, 0, dgk)
            dkn_ref[:, sl] = dkh.astype(BF16)
            dkr = dkr + dkh
        dkr_ref[...] = dkr.astype(BF16)
        dvb_ref[...] = dv_ref[...].astype(BF16)
        d_cq = lax.dot_general(dqr_ref[...], wq[...], (((1,), (1,)), ((), ())), preferred_element_type=F32)
        dcq, dg = _norm_bwd(cqv, rs_q, gqa[...], d_cq)
        _acc_row(ggqa, 0, dg)
        dcq_ref[...] = dcq.astype(BF16)
        d_ckv = (lax.dot_general(dkn_ref[...], wk[...], (((1,), (1,)), ((), ())), preferred_element_type=F32)
                 + lax.dot_general(dvb_ref[...], wv[...], (((1,), (1,)), ((), ())), preferred_element_type=F32))
        dckv, dg = _norm_bwd(ckvv, rs_kv, gkva[...], d_ckv)
        _acc_row(ggkva, 0, dg)
        dckv_ref[...] = dckv.astype(BF16)

    return _rowwise(body, "mla_qkv_bwd", cq.shape[0], tm, rows=[cq, ckv, krp, cos_t, sin_t, dq, dk, dv],
                    fulls=[g_qa, g_kva, g_qn, g_kn, w_uq_p, w_uk_p, w_uv],
                    outs=[(Q_LORA, BF16), (KV_LORA, BF16), (LANES, BF16), (Q_LORA, BF16), (HEADS * LANES, BF16),
                          (KV_LORA, BF16), (HEADS * LANES, BF16), (MLA_W, BF16)],
                    accs=[((SUBLANES, Q_LORA), F32), ((SUBLANES, KV_LORA), F32), ((SUBLANES, LANES), F32),
                          ((SUBLANES, LANES), F32)])


def _in_proj_bwd(x, dx1, dxr_f, dxr_b, dyg, dcq, dckv, dkrp, g, w_in_p, tm):
    def body(i, n, x_ref, dx1_ref, df_ref, db_ref, dyg_ref, dcq_ref, dckv_ref, dkr_ref, g_ref, w_ref, gx_ref, dp_ref, gg_ref):
        _zero_first(i, gg_ref)
        dp_ref[:, :LRU_W] = (df_ref[...].astype(F32) + db_ref[...].astype(F32)).astype(BF16)
        dp_ref[:, LRU_W:2 * LRU_W] = dyg_ref[...].astype(BF16)
        dp_ref[:, 2 * LRU_W:2 * LRU_W + Q_LORA] = dcq_ref[...].astype(BF16)
        dp_ref[:, 2 * LRU_W + Q_LORA:OFF_KR] = dckv_ref[...].astype(BF16)
        dp_ref[:, OFF_KR:] = dkr_ref[...].astype(BF16)
        d_h = lax.dot_general(dp_ref[...], w_ref[...], (((1,), (1,)), ((), ())), preferred_element_type=F32)
        xv = x_ref[...]
        dx, dg = _norm_bwd(xv, _rstd(xv), g_ref[...], d_h)
        _acc_row(gg_ref, 0, dg)
        gx_ref[...] = dx + dx1_ref[...]

    return _rowwise(body, "in_proj_bwd", x.shape[0], tm, rows=[x, dx1, dxr_f, dxr_b, dyg, dcq, dckv, dkrp],
                    fulls=[g, w_in_p], outs=[(D_MODEL, F32), (IN_PAD, BF16)], accs=[((SUBLANES, D_MODEL), F32)])


ANY = pl.BlockSpec(memory_space=pl.ANY)


def _chip_peers(x, y):
    return ((1 - x, y), (x, 1 - y), (1 - x, 1 - y))


def _exchange_call(kern, name, ins, out_shapes, n_sems, aliases=None):
    return pl.pallas_call(
        kern, in_specs=[ANY] * len(ins), out_specs=[ANY] * len(out_shapes), out_shape=out_shapes,
        scratch_shapes=[pltpu.SemaphoreType.DMA((n,)) for n in n_sems], input_output_aliases=aliases or {},
        name=name)(*ins)


def _start_then_wait(copies):
    for cp in copies:
        cp.start()
    for cp in copies:
        cp.wait()


N_DEV = 8
RELATIONS = tuple((dx, dy, dc) for dx in (0, 1) for dy in (0, 1) for dc in (0, 1))[1:]


def _flip(v, d):
    return 1 - v if d else v


def _gather_copies(ins, outs, ssem, rsem, lsem):
    x, y, c = lax.axis_index("x"), lax.axis_index("y"), lax.axis_index("c")
    me = 2 * x + y
    cps = []
    for i, (a, o) in enumerate(zip(ins, outs)):
        cps.append(pltpu.make_async_copy(a, o.at[me], lsem.at[i]))
        for j, (px, py) in enumerate(_chip_peers(x, y)):
            cps.append(pltpu.make_async_remote_copy(a, o.at[me], ssem.at[3 * i + j], rsem.at[3 * i + j],
                                                    device_id=(px, py, c), device_id_type=MESH))
    return cps


def _gather_shapes(arrs):
    return [jax.ShapeDtypeStruct((N_CHIPS,) + a.shape, a.dtype) for a in arrs]


def _gather_sems(n):
    return [pltpu.SemaphoreType.DMA((3 * n,)), pltpu.SemaphoreType.DMA((3 * n,)), pltpu.SemaphoreType.DMA((n,))]


def _gather_chips(arrs):
    n = len(arrs)

    def kern(*refs):
        _start_then_wait(_gather_copies(refs[:n], refs[n:2 * n], *refs[2 * n:]))

    return _exchange_call(kern, "gather_weights", arrs, _gather_shapes(arrs), (3 * n, 3 * n, n))


def _to_owner_copies(ins, outs, ssem, rsem, lsem):
    x, y, c = lax.axis_index("x"), lax.axis_index("y"), lax.axis_index("c")
    me = 4 * x + 2 * y + c
    cps = []
    for i, (a, o) in enumerate(zip(ins, outs)):
        cps.append(pltpu.make_async_copy(a.at[2 * x + y, c], o.at[me], lsem.at[i]))
        for r, (dx, dy, dc) in enumerate(RELATIONS):
            tx, ty, tc = _flip(x, dx), _flip(y, dy), _flip(c, dc)
            cps.append(pltpu.make_async_remote_copy(a.at[2 * tx + ty, tc], o.at[me], ssem.at[7 * i + r], rsem.at[7 * i + r],
                                                    device_id=(tx, ty, tc), device_id_type=MESH))
    return cps


def _to_owner_shapes(arrs):
    return [jax.ShapeDtypeStruct((N_DEV,) + a.shape[2:], a.dtype) for a in arrs]


def _to_owner_sems(n):
    return [pltpu.SemaphoreType.DMA((7 * n,)), pltpu.SemaphoreType.DMA((7 * n,)), pltpu.SemaphoreType.DMA((n,))]


def _to_owner(arrs, name):
    n = len(arrs)

    def kern(*refs):
        _start_then_wait(_to_owner_copies(refs[:n], refs[n:2 * n], *refs[2 * n:]))

    return _exchange_call(kern, name, arrs, _to_owner_shapes(arrs), (7 * n, 7 * n, n))


def _join_halves(arrs):
    n = len(arrs)

    def kern(*refs):
        outs, (ssem, rsem) = refs[n:2 * n], refs[2 * n:]
        x, y, c = lax.axis_index("x"), lax.axis_index("y"), lax.axis_index("c")
        _start_then_wait([
            pltpu.make_async_remote_copy(outs[i].at[c], outs[i].at[c], ssem.at[i], rsem.at[i],
                                         device_id=(x, y, 1 - c), device_id_type=MESH) for i in range(n)])

    outs = [jax.ShapeDtypeStruct(a.shape, a.dtype) for a in arrs]
    return _exchange_call(kern, "grad_join_halves", arrs, outs, (n, n), aliases={i: i for i in range(n)})


def _row_block(rows, row_bytes, limit=2 << 20):
    best = None
    for d in range(16, rows + 1, 16):
        if rows % d == 0 and d * row_bytes <= limit:
            best = d
    return best if best is not None else rows


def _sum_devices(b, c, name):
    _, h, cols = b.shape
    hb = _row_block(h, cols * 4)

    def kern(c_ref, b_ref, o_ref):
        acc = b_ref[0].astype(F32)
        for j in range(1, N_DEV):
            acc = acc + b_ref[j].astype(F32)
        o_ref[...] = acc

    return pl.pallas_call(
        kern,
        grid_spec=pltpu.PrefetchScalarGridSpec(
            num_scalar_prefetch=1, grid=(h // hb,),
            in_specs=[pl.BlockSpec((N_DEV, hb, cols), lambda i, c_ref: (0, i, 0))],
            out_specs=pl.BlockSpec((None, hb, cols), lambda i, c_ref: (c_ref[0], i, 0))),
        out_shape=jax.ShapeDtypeStruct((2, h, cols), F32), name=name, compiler_params=_cparams(1))(c, b)


def _adamw(w, g, m, v, name):
    rows, cols = w.shape
    rb = _row_block(rows, cols * 4)
    c1 = 1.0 - ADAM_B1 ** ADAM_STEP
    c2 = 1.0 - ADAM_B2 ** ADAM_STEP

    def kern(w_ref, g_ref, m_ref, v_ref, d_ref, mo_ref, vo_ref):
        gv = g_ref[...]
        mn = ADAM_B1 * m_ref[...] + (1.0 - ADAM_B1) * gv
        vn = ADAM_B2 * v_ref[...] + (1.0 - ADAM_B2) * (gv * gv)
        mo_ref[...] = mn
        vo_ref[...] = vn
        d_ref[...] = (-ADAM_LR) * ((mn / c1) / (jnp.sqrt(vn / c2) + ADAM_EPS) + ADAM_WD * w_ref[...])

    spec = pl.BlockSpec((rb, cols), lambda i: (i, 0))
    return pl.pallas_call(
        kern, grid=(rows // rb,), in_specs=[spec] * 4, out_specs=[spec] * 3,
        out_shape=[jax.ShapeDtypeStruct(w.shape, F32)] * 3, name=name, compiler_params=_cparams(1))(w, g, m, v)


def _pad_rows(flat, rows):
    return jnp.pad(flat, (0, rows * LANES - flat.shape[0])).reshape(rows, LANES)


def _round_up(n, m):
    return (n + m - 1) // m * m


def _shard_shape(shape, axis):
    return tuple(s // N_CHIPS if a == axis else s for a, s in enumerate(shape))


def _to_shards(full, axis):
    shape = full.shape
    t = full.reshape(shape[:axis] + (N_CHIPS, shape[axis] // N_CHIPS) + shape[axis + 1:])
    return jnp.moveaxis(t, axis, 0).reshape(N_CHIPS, -1)


def _from_shards(sh, shape, axis):
    t = sh.reshape((N_CHIPS,) + _shard_shape(shape, axis))
    t = jnp.moveaxis(t, 0, axis)
    return t.reshape(shape)


BIG = tuple((name, shape, axis) for name, shape, axis, big in SHARDED if big)
EARLY_WEIGHTS = ("w_in", "w_uq", "w_ukv")
SMALL_SHARDED = tuple((name, shape, axis) for name, shape, axis, big in SHARDED if not big)


def _pack_small_weights(p):
    flat = jnp.concatenate([p[name].reshape(-1) for name, _, _ in SMALL_SHARDED])
    return _pad_rows(flat, _round_up(-(-flat.shape[0] // LANES), SUBLANES))


def _unpack_small_weights(gathered):
    flat = gathered.reshape(N_CHIPS, -1)
    out, off = {}, 0
    for name, shape, axis in SMALL_SHARDED:
        n = _numel(shape) // N_CHIPS
        out[name] = _from_shards(flat[:, off:off + n], shape, axis)
        off += n
    return out


def _pack_small_local(p, prefix=""):
    parts = [p[prefix + name].reshape(-1) for name, _, _ in SMALL_SHARDED]
    parts += [p[prefix + name].reshape(-1) for name, _ in REPLICATED]
    return jnp.concatenate(parts)


def _pack_small_grads(g, loss_part):
    parts = [_to_shards(g[name], axis) for name, _, axis in SMALL_SHARDED]
    rep = jnp.concatenate([g[name].reshape(-1) for name, _ in REPLICATED] + [loss_part.reshape(1)])
    parts.append(jnp.broadcast_to(rep[None], (N_CHIPS, rep.shape[0])))
    return jnp.concatenate(parts, axis=1)


def _unpack_small_local(flat):
    out, off = {}, 0
    for name, shape, axis in SMALL_SHARDED:
        n = _numel(shape) // N_CHIPS
        out[name] = flat[off:off + n].reshape((1,) + _shard_shape(shape, axis))
        off += n
    for name, shape in REPLICATED:
        n = _numel(shape)
        out[name] = flat[off:off + n].reshape((1,) + shape)
        off += n
    return out


def _grad_shards(g, shape, axis):
    if axis == 0:
        return g.reshape((N_CHIPS,) + _shard_shape(shape, axis))
    return jnp.transpose(g.reshape(shape[0], N_CHIPS, shape[1] // N_CHIPS), (1, 0, 2))


def _cols_from_shards(w4):
    return jnp.transpose(w4, (1, 0, 2)).reshape(w4.shape[1], -1)


def _block_diag(w):
    eye = jnp.eye(LRU_BLOCKS, dtype=w.dtype)
    return jnp.einsum("ncd,nm->ncmd", w, eye).reshape(LRU_W, LRU_W)


def _block_diag_t(g):
    g4 = g.reshape(LRU_BLOCKS, 64, LRU_BLOCKS, 64)
    eye = jnp.eye(LRU_BLOCKS, dtype=g.dtype)[:, None, :, None]
    return jnp.sum(g4 * eye, axis=2)


def _pad8(a):
    return jnp.pad(a, ((0, SUBLANES - a.shape[0]), (0, 0)))


def kernel(x, mem, positions, attn_norm, w_in, lru_conv_w, lru_conv_b, lru_w_a, lru_b_a, lru_w_i, lru_b_i, lru_lambda, q_a_norm, w_uq, kv_a_norm, w_ukv, mla_q_norm, mla_k_norm, lru_out_norm, mla_out_norm, w_out, mem_attn_norm, mem_norm, w_mem_q, w_mem_kv, mem_q_norm, mem_k_norm, w_mem_o, ffn_norm, w_up, ffn_conv_w, ffn_conv_b, w_down, loss_target, m_attn_norm, m_w_in, m_lru_conv_w, m_lru_conv_b, m_lru_w_a, m_lru_b_a, m_lru_w_i, m_lru_b_i, m_lru_lambda, m_q_a_norm, m_w_uq, m_kv_a_norm, m_w_ukv, m_mla_q_norm, m_mla_k_norm, m_lru_out_norm, m_mla_out_norm, m_w_out, m_mem_attn_norm, m_mem_norm, m_w_mem_q, m_w_mem_kv, m_mem_q_norm, m_mem_k_norm, m_w_mem_o, m_ffn_norm, m_w_up, m_ffn_conv_w, m_ffn_conv_b, m_w_down, v_attn_norm, v_w_in, v_lru_conv_w, v_lru_conv_b, v_lru_w_a, v_lru_b_a, v_lru_w_i, v_lru_b_i, v_lru_lambda, v_q_a_norm, v_w_uq, v_kv_a_norm, v_w_ukv, v_mla_q_norm, v_mla_k_norm, v_lru_out_norm, v_mla_out_norm, v_w_out, v_mem_attn_norm, v_mem_norm, v_w_mem_q, v_w_mem_kv, v_mem_q_norm, v_mem_k_norm, v_w_mem_o, v_ffn_norm, v_w_up, v_ffn_conv_w, v_ffn_conv_b, v_w_down):
    given = dict(locals())
    local = {name: given[name][0] for name in WEIGHT_ORDER}
    s = x.shape[1]
    x2d, mem2d, tgt = x[0], mem[0], loss_target[0]
    tm = min(512, s)
    tm_wide = min(1024, s)
    tm_ffn = min(256, s)
    t_scan = min(1024, s)
    tq_f, tq_b, tk = min(4096, s), min(2048, s), min(512, s)

    early = [b for b in BIG if b[0] in EARLY_WEIGHTS]
    late = [b for b in BIG if b[0] not in EARLY_WEIGHTS]
    got = _gather_chips([local[name].astype(BF16) for name, _, _ in early] + [_pack_small_weights(local)])
    full = _unpack_small_weights(got[-1])

    def take_gathered(entries, arrays):
        for (name, shape, axis), w4 in zip(entries, arrays):
            if axis == 0:
                full[name] = w4.reshape(shape)
            elif name in ("w_up", "w_mem_o"):
                full[name] = w4
            else:
                full[name] = _cols_from_shards(w4)

    take_gathered(early, got)
    row = lambda a: a.reshape(1, -1)
    b16 = lambda a: a.astype(BF16)
    zeros = lambda r, c: jnp.zeros((r, c), BF16)
    w_in_f = full["w_in"]
    w_in_p = jnp.concatenate([w_in_f[:, :OFF_KR], _head_tile(zeros(D_MODEL, QK_NOPE), w_in_f[:, OFF_KR:])], axis=1)
    uq = full["w_uq"].reshape(Q_LORA, HEADS, QK_HEAD)
    w_uq_p = _head_tile(uq[:, :, :QK_NOPE], uq[:, :, QK_NOPE:]).reshape(Q_LORA, -1)
    ukv = full["w_ukv"].reshape(KV_LORA, HEADS, QK_NOPE + V_DIM)
    w_uk_p = _head_tile(ukv[:, :, :QK_NOPE], None).reshape(KV_LORA, -1)
    w_uv = ukv[:, :, QK_NOPE:].reshape(KV_LORA, MLA_W)
    wa = [b16(_block_diag(local["lru_w_a"][d])) for d in range(2)]
    wi = [b16(_block_diag(local["lru_w_i"][d])) for d in range(2)]
    cw = [_pad8(full["lru_conv_w"][d]) for d in range(2)]
    pv = [_pad8(jnp.stack([full["lru_conv_b"][d], full["lru_b_a"][d], full["lru_b_i"][d], full["lru_lambda"][d]]))
          for d in range(2)]
    ffn_cw = _pad8(jnp.concatenate([full["ffn_conv_w"], row(local["ffn_conv_b"])], axis=0))
    g_attn, g_qa, g_kva = row(local["attn_norm"]), row(local["q_a_norm"]), row(local["kv_a_norm"])
    g_qn = _head_tile(row(local["mla_q_norm"])[:, :QK_NOPE], row(local["mla_q_norm"])[:, QK_NOPE:])
    g_kn = _head_tile(row(local["mla_k_norm"])[:, :QK_NOPE], row(local["mla_k_norm"])[:, QK_NOPE:])
    g_lru, g_mla = row(local["lru_out_norm"]), row(local["mla_out_norm"])
    g_memattn, g_mem = row(local["mem_attn_norm"]), row(local["mem_norm"])
    g_mq, g_mk, g_ffn = row(local["mem_q_norm"]), row(local["mem_k_norm"]), row(local["ffn_norm"])

    inv = ROPE_THETA ** (-jnp.arange(0, QK_ROPE, 2, dtype=F32) / QK_ROPE)
    no_nope = jnp.zeros((1, QK_NOPE), F32)
    inv_tile = _head_tile(no_nope, jnp.concatenate([inv, inv])[None])
    sign_tile = _head_tile(no_nope, jnp.concatenate([-jnp.ones_like(inv), jnp.ones_like(inv)])[None])
    ang = positions[0].astype(F32)[:, None] * inv_tile
    cos_t, sin_t = jnp.cos(ang), jnp.sin(ang) * sign_tile

    xr, yg, cq, ckv, krp, hb_in = _in_proj(x2d, g_attn, w_in_p, tm_wide)
    h_f, *saved_f = _lru_scan_fwd(xr, cw[0], pv[0], wa[0], wi[0], False, t_scan)
    h_b, *saved_b = _lru_scan_fwd(xr, cw[1], pv[1], wa[1], wi[1], True, t_scan)
    q, k, v = _mla_qkv(cq, ckv, krp, cos_t, sin_t, g_qa, g_kva, g_qn, g_kn, w_uq_p, w_uk_p, w_uv, tm_wide)
    o, lse, *got = _attn_fwd(q, k, v, tq_f, tk, shards=[local[name].astype(BF16) for name, _, _ in late])
    take_gathered(late, got)
    x1, mixed = _mix_out(h_f, h_b, yg, o, x2d, g_lru, g_mla, full["w_out"], tm_wide)
    km, vm = _mem_kv(mem2d, g_mem, full["w_mem_kv"], g_mk)
    x2, o_mem = _mem_attn(x1, g_memattn, full["w_mem_q"], g_mq, km, vm, full["w_mem_o"], tm_wide)
    gu_pre, hb_ffn = _ffn_up(x2, g_ffn, full["w_up"], tm)
    dy, dyb, act, dgu, loss_acc = _ffn_down_loss(gu_pre, x2, tgt, ffn_cw, full["w_down"], tm_ffn)

    grads = {}
    grads["w_down"] = _matmul_tn(act, dyb, "grad_w_down", out_dtype=BF16)
    dpre, g_conv = _ffn_bwd_conv(dgu, gu_pre, ffn_cw, tm_ffn)
    grads["ffn_conv_w"], grads["ffn_conv_b"] = g_conv[:3], g_conv[3]
    grads["w_up"] = _matmul_tn(hb_ffn, dpre, "grad_w_up", col_shards=True, out_dtype=BF16)
    dx2, dx2b, gg = _ffn_bwd_in(dpre, x2, dy, g_ffn, full["w_up"], tm)
    grads["ffn_norm"] = gg[0]
    grads["w_mem_o"] = _matmul_tn(o_mem, dx2b, "grad_w_mem_o", out_dtype=BF16)
    dx1, dx1b, hm, dqr_mem, dkm, dvm, gg, ggq = _mem_attn_bwd(x1, dx2, dx2b, g_memattn, full["w_mem_q"], g_mq, km, vm,
                                                                 full["w_mem_o"], tm)
    grads["mem_attn_norm"], grads["mem_q_norm"] = gg[0], ggq[0]
    grads["w_mem_q"] = _matmul_tn(hm, dqr_mem, "grad_w_mem_q", out_dtype=BF16)
    g_mem_kv, gg, ggk, _ = _mem_kv_bwd(mem2d, g_mem, full["w_mem_kv"], g_mk, dkm, dvm)
    grads["w_mem_kv"] = g_mem_kv.astype(BF16)
    grads["mem_norm"], grads["mem_k_norm"] = gg[0], ggk[0]
    grads["w_out"] = _matmul_tn(mixed, dx1b, "grad_w_out", out_dtype=BF16)
    dh, dyg, dob, dl128, ggl, ggm = _mix_out_bwd(dx1b, h_f, h_b, yg, o, g_lru, g_mla, full["w_out"], tm_wide)
    grads["lru_out_norm"], grads["mla_out_norm"] = ggl[0], ggm[0]
    delta_t = jnp.transpose(dl128[:, :HEADS]).reshape(HEADS // 2, 2, s)
    def halves(name, shape, axis):
        g4 = grads[name] if grads[name].ndim == 3 else _grad_shards(grads[name], shape, axis)
        return g4.reshape(N_CHIPS, 2, g4.shape[1] // 2, g4.shape[2])

    dq, dk, dv, *arrived_late = _attn_bwd(q, k, v, dob, lse, delta_t, tq_b, tk,
                                          contributions=[halves(*e) for e in late])
    (dcq, dckv, dkrp, cqb, dqr, ckvb, dkn, dvb, ggqa, ggkva, ggqn, ggkn) = _mla_qkv_bwd(
        cq, ckv, krp, cos_t, sin_t, dq, dk, dv, g_qa, g_kva, g_qn, g_kn, w_uq_p, w_uk_p, w_uv, tm_wide)
    grads["q_a_norm"], grads["kv_a_norm"] = ggqa[0], ggkva[0]
    grads["mla_q_norm"] = jnp.concatenate(_from_head_tile(ggqn[0]))
    grads["mla_k_norm"] = jnp.concatenate(_from_head_tile(ggkn[0]))
    g_uq_p = _matmul_tn(cqb, dqr, "grad_w_uq")
    grads["w_uq"] = jnp.concatenate(_from_head_tile(g_uq_p.reshape(Q_LORA, HEADS, LANES)), axis=-1).reshape(Q_LORA, -1)
    g_uk_p = _from_head_tile(_matmul_tn(ckvb, dkn, "grad_w_uk").reshape(KV_LORA, HEADS, LANES))[0]
    g_uv = _matmul_tn(ckvb, dvb, "grad_w_uv").reshape(KV_LORA, HEADS, V_DIM)
    grads["w_ukv"] = jnp.concatenate([g_uk_p, g_uv], axis=2).reshape(KV_LORA, -1)
    dxr, gwa, gwi, gvec = [], [], [], []
    for d, (hd, saved) in enumerate(((h_f, saved_f), (h_b, saved_b))):
        r = _lru_scan_bwd(xr, saved, hd, dh, cw[d], pv[d], wa[d], wi[d], d == 1, t_scan)
        dxr.append(r[0])
        gwa.append(_block_diag_t(r[1]))
        gwi.append(_block_diag_t(r[2]))
        gvec.append(r[3])
    grads["lru_w_a"], grads["lru_w_i"] = jnp.stack(gwa), jnp.stack(gwi)
    grads["lru_conv_w"] = jnp.stack([gv[:CONV_W] for gv in gvec])
    for r_i, name in ((4, "lru_conv_b"), (5, "lru_b_a"), (6, "lru_b_i"), (7, "lru_lambda")):
        grads[name] = jnp.stack([gv[r_i] for gv in gvec])
    grad_x, dproj, gg = _in_proj_bwd(x2d, dx1, dxr[0], dxr[1], dyg, dcq, dckv, dkrp, g_attn, w_in_p, tm_wide)
    grads["attn_norm"] = gg[0]
    g_in_p = _matmul_tn(hb_in, dproj, "grad_w_in")
    grads["w_in"] = jnp.concatenate([g_in_p[:, :OFF_KR], _from_head_tile(g_in_p[:, OFF_KR:])[1]], axis=1)

    small = _pack_small_grads(grads, loss_acc[0, 0] * (0.5 / D_MODEL))
    length = small.shape[1]
    hrows = _round_up(-(-length // (2 * LANES)), 16)
    small = jnp.pad(small, ((0, 0), (0, 2 * hrows * LANES - length))).reshape(N_CHIPS, 2, hrows, LANES)
    for name, _, _ in early:
        grads[name] = grads[name].astype(BF16)
    arrived_early = _to_owner([halves(*e) for e in early] + [small], "grad_to_owner")
    names = [name for name, _, _ in late + early] + ["small"]
    c_idx = lax.axis_index("c").astype(jnp.int32).reshape(1)
    reduced = _join_halves([_sum_devices(b, c_idx, "grad_sum_" + n)
                            for n, b in zip(names, list(arrived_late) + list(arrived_early))])

    outs = [{}, {}, {}, {}]
    for (name, shape, axis), r in zip(late + early, reduced):
        g2 = r.reshape(_shard_shape(shape, axis))
        res = _adamw(local[name], g2, given["m_" + name][0], given["v_" + name][0], "adamw_" + name)
        for o_, a in zip(outs, (g2, *res)):
            o_[name] = a[None]
    pack = lambda prefix: _pad_rows(_pack_small_local({n: given[prefix + n] for n in WEIGHT_ORDER}), 2 * hrows)
    g_small = reduced[-1].reshape(2 * hrows, LANES)
    res = _adamw(pack(""), g_small, pack("m_"), pack("v_"), "adamw_small")
    for o_, a in zip(outs, (g_small, *res)):
        o_.update(_unpack_small_local(a.reshape(-1)))
    loss = g_small.reshape(-1)[length - 1]
    return (loss, grad_x[None], *[o_[n] for o_ in outs for n in WEIGHT_ORDER])
```
